```python
import math
import jax
import jax.numpy as jnp
from jax import lax
import numpy as np

D_MODEL = 1024
BATCH = 8
SEQ = 8192
DEPTH = 1

HEAD_DIM = 64
ATTN_SCALE = HEAD_DIM ** -0.5
Q_BLOCK = 128
NSA_HEADS = 8
NSA_KV_HEADS = 2
NSA_GROUP = NSA_HEADS // NSA_KV_HEADS
CMP_BLOCK = 32
CMP_STRIDE = 16
CMP_HIDDEN = 128
SEL_BLOCK = 64
SEL_TOP_N = 8
SEL_INIT_BLOCKS = 1
SEL_LOCAL_BLOCKS = 2
SEL_FORCE = 1e6
NSA_WINDOW = 512
SWA_HEADS = 8
SWA_KV_HEADS = 2
SWA_GROUP = SWA_HEADS // SWA_KV_HEADS
SWA_WINDOW = 128
REL_BUCKETS = 32
REL_MAX_DIST = 128
N_BIAS_HEADS = NSA_HEADS + SWA_HEADS
N_EXPERTS = 256
TOP_K = 8
EXPERT_HIDDEN = 256
SHARED_HIDDEN = 256
N_EXPERT_GROUPS = 8
TOPK_EXPERT_GROUPS = 4
ROUTED_SCALE = 2.5
MOE_BLOCK = 128
LN_EPS = 1e-5
DN_ALPHA = (2 * DEPTH) ** 0.25
DN_BETA = (8 * DEPTH) ** -0.25

NSA_Q_W = NSA_HEADS * HEAD_DIM
NSA_KV_W = NSA_KV_HEADS * HEAD_DIM
SWA_Q_W = SWA_HEADS * HEAD_DIM
SWA_KV_W = SWA_KV_HEADS * HEAD_DIM
IN_SIZES = (NSA_Q_W, NSA_KV_W, NSA_KV_W, NSA_KV_W, NSA_KV_W, NSA_KV_W, NSA_KV_W, 3 * NSA_HEADS,
            SWA_Q_W, SWA_KV_W, SWA_KV_W, D_MODEL, D_MODEL)
IN_WIDTH = sum(IN_SIZES)

kernel_name = 'nsa_swa_sink_gated_hybrid_moe'


def _layer_norm(x, g, b):
    xf = x.astype(jnp.float32)
    mu = jnp.mean(xf, -1, keepdims=True)
    var = jnp.mean(jnp.square(xf - mu), -1, keepdims=True)
    y = (xf - mu) * lax.rsqrt(var + LN_EPS) * g.astype(jnp.float32) + b.astype(jnp.float32)
    return y.astype(x.dtype)


def _rel_bucket(dist):
    n = jnp.maximum(dist, 0)
    max_exact = REL_BUCKETS // 2
    nf = jnp.maximum(n, 1).astype(jnp.float32)
    log_b = max_exact + (jnp.log(nf / max_exact) / math.log(REL_MAX_DIST / max_exact)
                         * (REL_BUCKETS - max_exact)).astype(jnp.int32)
    log_b = jnp.minimum(log_b, REL_BUCKETS - 1)
    return jnp.where(n < max_exact, n, log_b)


def _masked_softmax(s, mask):
    s = jnp.where(mask, s, -jnp.inf)
    m = jnp.max(s, -1, keepdims=True)
    m = jnp.where(jnp.isfinite(m), m, 0.0)
    e = jnp.exp(s - m)
    return e / jnp.maximum(jnp.sum(e, -1, keepdims=True), 1e-30)


def _sink_softmax(s, mask, sink):
    s = jnp.where(mask, s, -jnp.inf)
    sk = sink.astype(jnp.float32)[None, :, :, None, None]
    m = jnp.maximum(jnp.max(s, -1, keepdims=True), sk)
    e = jnp.exp(s - m)
    return e / (jnp.sum(e, -1, keepdims=True) + jnp.exp(sk - m))


def _window_pad(window):
    return Q_BLOCK * (-(-(window - 1) // Q_BLOCK))


def _window_bias(tab, pad):
    kw_len = pad + Q_BLOCK
    dist = jnp.arange(Q_BLOCK)[:, None] + pad - jnp.arange(kw_len)[None, :]
    bias = jnp.transpose(tab[_rel_bucket(dist)], (2, 3, 0, 1)).astype(jnp.float32)
    return bias, dist


def _window_attend(q, kpad, vpad, t0, pad, window, bias, dist, sink):
    kw_len = pad + Q_BLOCK
    kw = lax.dynamic_slice_in_dim(kpad, t0, kw_len, axis=1)
    vw = lax.dynamic_slice_in_dim(vpad, t0, kw_len, axis=1)
    s = jnp.einsum('bqgrd,bkgd->bgrqk', q, kw).astype(jnp.float32) * ATTN_SCALE + bias
    kpos = t0 - pad + jnp.arange(kw_len)
    mask = (dist >= 0) & (dist < window) & (kpos >= 0)[None, :]
    p = _masked_softmax(s, mask) if sink is None else _sink_softmax(s, mask, sink)
    return jnp.einsum('bgrqk,bkgd->bqgrd', p, vw.astype(jnp.float32))


def _compress(t, pe, w1, b1, w2):
    bsz, seq, g, dh = t.shape
    n_chunks = seq // CMP_STRIDE
    r = CMP_BLOCK // CMP_STRIDE
    n_cmp = n_chunks - r + 1
    ch = t.reshape(bsz, n_chunks, CMP_STRIDE, g, dh)
    blk = jnp.concatenate([ch[:, j:j + n_cmp] for j in range(r)], axis=2) + pe[None, None, :, None, :]
    flat = jnp.moveaxis(blk, 3, 2).reshape(bsz, n_cmp, g, CMP_BLOCK * dh)
    return jax.nn.gelu(flat @ w1 + b1) @ w2


def _unblock(o):
    o = jnp.moveaxis(o, 0, 1)
    return o.reshape(o.shape[0], o.shape[1] * o.shape[2], o.shape[3], o.shape[4], o.shape[5])


def _token_mixer(x, w_in, cmp_pe, cmp_w1, cmp_b1, cmp_w2, sinks, bias_table, proj_a, proj_b, w_out):
    f32 = jnp.float32
    bsz, seq, _ = x.shape
    G, R = NSA_KV_HEADS, NSA_GROUP
    Gb, Rb = SWA_KV_HEADS, SWA_GROUP
    h = x @ w_in
    offs = [int(o) for o in np.cumsum(IN_SIZES)[:-1]]
    (q_a, k_c, v_c, k_s, v_s, k_w, v_w, g_a, q_b, k_b, v_b, gate_a, gate_b) = jnp.split(h, offs, axis=-1)
    q_a = q_a.reshape(bsz, seq, G, R, HEAD_DIM)
    q_b = q_b.reshape(bsz, seq, Gb, Rb, HEAD_DIM)
    kv_a = lambda t: t.reshape(bsz, seq, G, HEAD_DIM)
    kv_b = lambda t: t.reshape(bsz, seq, Gb, HEAD_DIM)

    k_cmp = _compress(kv_a(k_c), cmp_pe[0], cmp_w1[0], cmp_b1[0], cmp_w2[0])
    v_cmp = _compress(kv_a(v_c), cmp_pe[1], cmp_w1[1], cmp_b1[1], cmp_w2[1])
    n_cmp = k_cmp.shape[1]
    cmp_start = jnp.arange(n_cmp) * CMP_STRIDE
    cmp_end = cmp_start + CMP_BLOCK - 1
    n_sel = seq // SEL_BLOCK
    n_top = min(SEL_TOP_N, n_sel)
    n_tok = n_top * SEL_BLOCK
    sel_ids = jnp.arange(n_sel)
    sel_start = sel_ids * SEL_BLOCK
    cmp_to_sel = ((cmp_start[:, None] < sel_start[None, :] + SEL_BLOCK)
                  & (cmp_start[:, None] + CMP_BLOCK > sel_start[None, :])).astype(f32)
    k_sel_blk = jnp.transpose(kv_a(k_s).reshape(bsz, n_sel, SEL_BLOCK, G, HEAD_DIM), (0, 3, 1, 2, 4))
    v_sel_blk = jnp.transpose(kv_a(v_s).reshape(bsz, n_sel, SEL_BLOCK, G, HEAD_DIM), (0, 3, 1, 2, 4))
    bi = jnp.arange(bsz)[:, None, None, None]
    gi = jnp.arange(G)[None, :, None, None]

    tab_a = bias_table[:, :NSA_HEADS].reshape(REL_BUCKETS, G, R)
    tab_b = bias_table[:, NSA_HEADS:].reshape(REL_BUCKETS, Gb, Rb)

    pad_a = _window_pad(NSA_WINDOW)
    pad_b = _window_pad(SWA_WINDOW)
    padf = lambda t, p: jnp.pad(t, ((0, 0), (p, 0), (0, 0), (0, 0)))
    kw_pad, vw_pad = padf(kv_a(k_w), pad_a), padf(kv_a(v_w), pad_a)
    kb_pad, vb_pad = padf(kv_b(k_b), pad_b), padf(kv_b(v_b), pad_b)
    bias_wa, dist_wa = _window_bias(tab_a, pad_a)
    bias_wb, dist_wb = _window_bias(tab_b, pad_b)
    sink_b = sinks.reshape(Gb, Rb)

    def block(bidx):
        t0 = bidx * Q_BLOCK
        t_pos = t0 + jnp.arange(Q_BLOCK)
        qa = lax.dynamic_slice_in_dim(q_a, t0, Q_BLOCK, axis=1)
        qb = lax.dynamic_slice_in_dim(q_b, t0, Q_BLOCK, axis=1)
        s_c = jnp.einsum('bqgrd,bngd->bgrqn', qa, k_cmp).astype(f32) * ATTN_SCALE
        dist_c = t_pos[:, None] - cmp_end[None, :]
        bias_c = jnp.transpose(tab_a[_rel_bucket(dist_c)], (2, 3, 0, 1)).astype(f32)
        p_c = _masked_softmax(s_c + bias_c, dist_c >= 0)
        o_c = jnp.einsum('bgrqn,bngd->bqgrd', p_c, v_cmp.astype(f32))
        imp = jnp.einsum('bgrqn,nj->bgqj', p_c, cmp_to_sel)
        back = (t_pos // SEL_BLOCK)[:, None] - sel_ids[None, :]
        forced = (sel_ids[None, :] < SEL_INIT_BLOCKS) | ((back >= 0) & (back < SEL_LOCAL_BLOCKS))
        score = jnp.where(back >= 0, jnp.where(forced, SEL_FORCE, imp), -1.0)
        top_val, top_idx = lax.top_k(score, n_top)
        k_g = k_sel_blk[bi, gi, top_idx].reshape(bsz, G, Q_BLOCK, n_tok, HEAD_DIM)
        v_g = v_sel_blk[bi, gi, top_idx].reshape(bsz, G, Q_BLOCK, n_tok, HEAD_DIM)
        pos = (top_idx[..., None] * SEL_BLOCK + jnp.arange(SEL_BLOCK)).reshape(bsz, G, Q_BLOCK, n_tok)
        valid = jnp.repeat(top_val >= 0.0, SEL_BLOCK, axis=-1)
        mask_s = (pos <= t_pos[None, None, :, None]) & valid
        bias_s = jnp.transpose(tab_a[_rel_bucket(t_pos[None, None, :, None] - pos), gi], (0, 1, 4, 2, 3)).astype(f32)
        s_s = jnp.einsum('bqgrd,bgqkd->bgrqk', qa, k_g).astype(f32) * ATTN_SCALE + bias_s
        p_s = _masked_softmax(s_s, mask_s[:, :, None])
        o_s = jnp.einsum('bgrqk,bgqkd->bqgrd', p_s, v_g.astype(f32))
        o_w = _window_attend(qa, kw_pad, vw_pad, t0, pad_a, NSA_WINDOW, bias_wa, dist_wa, None)
        o_b = _window_attend(qb, kb_pad, vb_pad, t0, pad_b, SWA_WINDOW, bias_wb, dist_wb, sink_b)
        return o_c, o_s, o_w, o_b

    o_c, o_s, o_w, o_b = lax.map(block, jnp.arange(seq // Q_BLOCK, dtype=jnp.int32))
    o_c, o_s, o_w, o_b = _unblock(o_c), _unblock(o_s), _unblock(o_w), _unblock(o_b)

    ga = jax.nn.sigmoid(g_a.astype(f32)).reshape(bsz, seq, G, R, 3)
    o_a = ga[..., 0:1] * o_c + ga[..., 1:2] * o_s + ga[..., 2:3] * o_w
    o_a = o_a.reshape(bsz, seq, NSA_Q_W).astype(x.dtype)
    o_b = o_b.reshape(bsz, seq, SWA_Q_W).astype(x.dtype)
    merged = jax.nn.sigmoid(gate_a) * (o_a @ proj_a) + jax.nn.sigmoid(gate_b) * (o_b @ proj_b)
    return merged @ w_out


def _moe_ffn(x, w_router, router_bias, e_gate, e_up, e_down, s_gate, s_up, s_down):
    f32 = jnp.float32
    bsz, seq, d = x.shape
    n_tokens = bsz * seq
    xt = x.reshape(n_tokens, d)
    scores = jax.nn.sigmoid((xt @ w_router).astype(f32))
    choice = scores + router_bias.astype(f32)
    per_group = N_EXPERTS // N_EXPERT_GROUPS
    grp_score = jnp.sum(lax.top_k(choice.reshape(n_tokens, N_EXPERT_GROUPS, per_group), 2)[0], -1)
    _, grp_idx = lax.top_k(grp_score, TOPK_EXPERT_GROUPS)
    grp_keep = jnp.any(grp_idx[..., None] == jnp.arange(N_EXPERT_GROUPS), axis=1)
    keep = jnp.repeat(grp_keep, per_group, axis=1)
    _, eidx = lax.top_k(jnp.where(keep, choice, -jnp.inf), TOP_K)
    gate = jnp.take_along_axis(scores, eidx, axis=1)
    gate = gate / jnp.sum(gate, -1, keepdims=True) * ROUTED_SCALE

    n_assign = n_tokens * TOP_K
    n_blocks = (n_assign + N_EXPERTS * (MOE_BLOCK - 1) + MOE_BLOCK - 1) // MOE_BLOCK
    n_rows = n_blocks * MOE_BLOCK
    flat_e = eidx.reshape(n_assign).astype(jnp.int32)
    flat_t = jnp.repeat(jnp.arange(n_tokens, dtype=jnp.int32), TOP_K)
    flat_w = gate.reshape(n_assign)
    order = jnp.argsort(flat_e)
    e_s, t_s, w_s = flat_e[order], flat_t[order], flat_w[order]
    counts = jnp.bincount(flat_e, length=N_EXPERTS).astype(jnp.int32)
    starts = jnp.cumsum(counts) - counts
    padded = (counts + MOE_BLOCK - 1) // MOE_BLOCK * MOE_BLOCK
    pends = jnp.cumsum(padded)
    pstarts = pends - padded
    dest = pstarts[e_s] + jnp.arange(n_assign, dtype=jnp.int32) - starts[e_s]
    row_tok = jnp.full((n_rows,), n_tokens, jnp.int32).at[dest].set(t_s)
    row_w = jnp.zeros((n_rows,), f32).at[dest].set(w_s)
    blk_e = jnp.minimum(jnp.searchsorted(pends, jnp.arange(n_blocks) * MOE_BLOCK, side='right'),
                        N_EXPERTS - 1).astype(jnp.int32)
    xpad = jnp.concatenate([xt, jnp.zeros((1, d), xt.dtype)], axis=0)

    def step(acc, inp):
        tok, wr, e = inp
        xb = xpad[tok]
        hb = jax.nn.silu(xb @ e_gate[e]) * (xb @ e_up[e])
        yb = (hb @ e_down[e]).astype(f32) * wr[:, None]
        return acc.at[tok].add(yb), None

    acc, _ = lax.scan(step, jnp.zeros((n_tokens + 1, d), f32),
                      (row_tok.reshape(n_blocks, MOE_BLOCK), row_w.reshape(n_blocks, MOE_BLOCK), blk_e))
    routed = acc[:n_tokens].astype(x.dtype)
    shared = (jax.nn.silu(xt @ s_gate) * (xt @ s_up)) @ s_down
    return (routed + shared).reshape(bsz, seq, d)


def setup_inputs(seed: int = 0) -> dict:
    key = jax.random.key(seed)
    ks = jax.random.split(key, 24)
    f32 = jnp.float32
    L = DEPTH

    def nrm(k, shape, scale):
        return jax.random.normal(k, shape, f32) * scale

    return {
        'x': nrm(ks[0], (BATCH, SEQ, D_MODEL), 1.0),
        'w_in': nrm(ks[1], (L, D_MODEL, IN_WIDTH), D_MODEL ** -0.5),
        'cmp_pe': nrm(ks[2], (L, 2, CMP_BLOCK, HEAD_DIM), 0.1),
        'cmp_w1': nrm(ks[3], (L, 2, CMP_BLOCK * HEAD_DIM, CMP_HIDDEN), (CMP_BLOCK * HEAD_DIM) ** -0.5),
        'cmp_b1': nrm(ks[4], (L, 2, CMP_HIDDEN), 0.01),
        'cmp_w2': nrm(ks[5], (L, 2, CMP_HIDDEN, HEAD_DIM), CMP_HIDDEN ** -0.5),
        'attn_sinks': nrm(ks[6], (L, SWA_HEADS), 0.5),
        'rel_bias_table': nrm(ks[7], (REL_BUCKETS, N_BIAS_HEADS), 0.5),
        'proj_a': nrm(ks[8], (L, NSA_Q_W, D_MODEL), NSA_Q_W ** -0.5),
        'proj_b': nrm(ks[9], (L, SWA_Q_W, D_MODEL), SWA_Q_W ** -0.5),
        'w_out': nrm(ks[10], (L, D_MODEL, D_MODEL), D_MODEL ** -0.5 * DN_BETA),
        'ln1_g': 1.0 + nrm(ks[11], (L, D_MODEL), 0.02),
        'ln1_b': nrm(ks[12], (L, D_MODEL), 0.02),
        'w_router': nrm(ks[13], (L, D_MODEL, N_EXPERTS), D_MODEL ** -0.5),
        'router_bias': nrm(ks[14], (L, N_EXPERTS), 0.01),
        'expert_w_gate': nrm(ks[15], (L, N_EXPERTS, D_MODEL, EXPERT_HIDDEN), D_MODEL ** -0.5),
        'expert_w_up': nrm(ks[16], (L, N_EXPERTS, D_MODEL, EXPERT_HIDDEN), D_MODEL ** -0.5),
        'expert_w_down': nrm(ks[17], (L, N_EXPERTS, EXPERT_HIDDEN, D_MODEL), EXPERT_HIDDEN ** -0.5 * DN_BETA),
        'shared_w_gate': nrm(ks[18], (L, D_MODEL, SHARED_HIDDEN), D_MODEL ** -0.5),
        'shared_w_up': nrm(ks[19], (L, D_MODEL, SHARED_HIDDEN), D_MODEL ** -0.5),
        'shared_w_down': nrm(ks[20], (L, SHARED_HIDDEN, D_MODEL), SHARED_HIDDEN ** -0.5 * DN_BETA),
        'ln2_g': 1.0 + nrm(ks[21], (L, D_MODEL), 0.02),
        'ln2_b': nrm(ks[22], (L, D_MODEL), 0.02),
    }


def reference(x, w_in, cmp_pe, cmp_w1, cmp_b1, cmp_w2, attn_sinks, rel_bias_table, proj_a, proj_b, w_out,
              ln1_g, ln1_b, w_router, router_bias, expert_w_gate, expert_w_up, expert_w_down,
              shared_w_gate, shared_w_up, shared_w_down, ln2_g, ln2_b):
    h = x
    for l in range(DEPTH):
        mix = _token_mixer(h, w_in[l], cmp_pe[l], cmp_w1[l], cmp_b1[l], cmp_w2[l], attn_sinks[l],
                           rel_bias_table, proj_a[l], proj_b[l], w_out[l])
        h = _layer_norm(DN_ALPHA * h + mix, ln1_g[l], ln1_b[l])
        ffn = _moe_ffn(h, w_router[l], router_bias[l], expert_w_gate[l], expert_w_up[l], expert_w_down[l],
                       shared_w_gate[l], shared_w_up[l], shared_w_down[l])
        h = _layer_norm(DN_ALPHA * h + ffn, ln2_g[l], ln2_b[l])
    return h
```

```python
import functools
import math

import numpy as np
import jax
import jax.numpy as jnp
from jax import lax
from jax.experimental import pallas as pl
from jax.experimental.pallas import tpu as pltpu

F32 = jnp.float32
BF16 = jnp.bfloat16
MXU_DTYPE = jnp.bfloat16

D_MODEL = 1024
HEAD_DIM = 64
ATTN_SCALE = HEAD_DIM ** -0.5
Q_BLOCK = 128
N_HEADS = 8
N_GROUPS = 2
GROUP = 4
CMP_BLOCK = 32
CMP_STRIDE = 16
CMP_HIDDEN = 128
SEL_BLOCK = 64
SEL_TOP_N = 8
SEL_INIT_BLOCKS = 1
SEL_LOCAL_BLOCKS = 2
SEL_FORCE = 1e6
NSA_WINDOW = 512
SWA_WINDOW = 128
REL_BUCKETS = 32
REL_MAX_DIST = 128
N_EXPERTS = 256
TOP_K = 8
EXPERT_HIDDEN = 256
SHARED_HIDDEN = 256
N_EXPERT_GROUPS = 8
TOPK_EXPERT_GROUPS = 4
ROUTED_SCALE = 2.5
LN_EPS = 1e-5
DEPTH = 1
DN_ALPHA = (2 * DEPTH) ** 0.25

NEG = -1e30
LANES = 128
ROW_TILE = (8, LANES)
CMP_FRONT = 16
CMP_NEAR = 32
VMEM_LIMIT = 56 * 1024 * 1024

IN_TM = 512
OUT_TM = 256
MOE_BM = 256
DISP_TM = 256
COMB_TM = 128


def _dot(a, b):
    return jnp.dot(a, b, preferred_element_type=F32)


def _dot_nt(a, b):
    return lax.dot_general(a, b, (((1,), (1,)), ((), ())), preferred_element_type=F32)


def _mx(a):
    return a.astype(MXU_DTYPE)


_IN_COLS = (('qa', 512), ('qb', 512), ('kc', 128), ('vc', 128), ('ks', 128), ('vs', 128), ('kw', 128),
            ('vw', 128), ('kb', 128), ('vb', 128), ('ga', 128), ('sg', 2048))


def _inproj_kernel(x_ref, w_ref, qa_ref, qb_ref, kc_ref, vc_ref, ks_ref, vs_ref, kw_ref, vw_ref, kb_ref, vb_ref,
                   ga_ref, sg_ref):
    xb = _mx(x_ref[...])
    outs = dict(qa=qa_ref, qb=qb_ref, kc=kc_ref, vc=vc_ref, ks=ks_ref, vs=vs_ref, kw=kw_ref, vw=vw_ref,
                kb=kb_ref, vb=vb_ref, ga=ga_ref, sg=sg_ref)
    off = 0
    for name, width in _IN_COLS:
        for c0 in range(0, width, 512):
            cw = min(512, width - c0)
            y = _dot(xb, w_ref[:, off + c0:off + c0 + cw])
            if name in ('ga', 'sg'):
                y = jax.nn.sigmoid(y)
            outs[name][:, c0:c0 + cw] = y.astype(outs[name].dtype)
        off += width


def _head_pair_perm():
    idx = np.zeros(N_HEADS * HEAD_DIM, np.int32)
    for r in range(GROUP):
        for g in range(N_GROUPS):
            for d in range(HEAD_DIM):
                idx[r * 128 + g * 64 + d] = (g * GROUP + r) * HEAD_DIM + d
    return idx


def _in_projection(x2, w_in):
    t = x2.shape[0]
    sizes = (512, 128, 128, 128, 128, 128, 128, 24, 512, 128, 128, 1024, 1024)
    offs = np.cumsum((0,) + sizes)
    part = [w_in[:, offs[k]:offs[k + 1]] for k in range(len(sizes))]
    w_qa, w_kc, w_vc, w_ks, w_vs, w_kw, w_vw, w_g, w_qb, w_kb, w_vb, w_gate_a, w_gate_b = part
    perm = _head_pair_perm()
    w_qa = w_qa[:, perm] * ATTN_SCALE
    w_qb = w_qb[:, perm] * ATTN_SCALE
    gperm = np.zeros(24, np.int32)
    for c in range(3):
        for r in range(GROUP):
            for g in range(N_GROUPS):
                gperm[c * 8 + r * 2 + g] = (g * GROUP + r) * 3 + c
    w_ga = jnp.pad(w_g[:, gperm], ((0, 0), (0, LANES - 24)))
    w_all = jnp.concatenate([w_qa, w_qb, w_kc, w_vc, w_ks, w_vs, w_kw, w_vw, w_kb, w_vb, w_ga, w_gate_a, w_gate_b],
                            axis=1).astype(MXU_DTYPE)
    n_all = w_all.shape[1]
    out_shape = []
    out_specs = []
    for name, width in _IN_COLS:
        dt = F32 if name == 'ga' else BF16
        out_shape.append(jax.ShapeDtypeStruct((t, width), dt))
        out_specs.append(pl.BlockSpec((IN_TM, width), lambda i: (i, 0)))
    outs = pl.pallas_call(
        _inproj_kernel,
        grid=(t // IN_TM,),
        in_specs=[pl.BlockSpec((IN_TM, D_MODEL), lambda i: (i, 0)),
                  pl.BlockSpec((D_MODEL, n_all), lambda i: (0, 0))],
        out_specs=out_specs,
        out_shape=out_shape,
        compiler_params=pltpu.CompilerParams(dimension_semantics=("arbitrary",), vmem_limit_bytes=VMEM_LIMIT),
        name="in_projection",
    )(x2, w_all)
    return dict(zip([n for n, _ in _IN_COLS], outs))


def _compress_kernel(tok_ref, w1_ref, pe_ref, w1o_ref, b1_ref, w2_ref, out_ref):
    n_chunks = tok_ref.shape[2]
    ab = _dot(tok_ref[0, 0], w1_ref[0])
    a = ab[:, :2 * CMP_HIDDEN]
    b_next = pltpu.roll(ab[:, 2 * CMP_HIDDEN:], n_chunks - 1, 0)
    cb = _dot(_mx(pe_ref[0]), _mx(w1o_ref[0]))[0:1, :] + b1_ref[0]
    cb2 = jnp.concatenate([cb, cb], axis=1)
    hid = jax.nn.gelu(a + b_next + cb2)
    out = _dot(_mx(hid), w2_ref[0])
    row = lax.broadcasted_iota(jnp.int32, out.shape, 0)
    out = jnp.where(row < n_chunks - 1, out, 0.0)
    out_ref[0, 0, 0:CMP_FRONT, :] = jnp.zeros((CMP_FRONT, LANES), F32)
    out_ref[0, 0, CMP_FRONT:CMP_FRONT + n_chunks, :] = out
    out_ref[0, 0, CMP_FRONT + n_chunks:, :] = jnp.zeros((CMP_NEAR - CMP_FRONT, LANES), F32)


def _compress(kc, vc, bsz, seq, cmp_pe, cmp_w1, cmp_b1, cmp_w2):
    n_chunks = seq // CMP_STRIDE
    tok = jnp.stack([kc, vc]).reshape(2, bsz, n_chunks, CMP_STRIDE * LANES)
    eye = jnp.eye(N_GROUPS, dtype=F32)
    w1r = cmp_w1.reshape(2, 2, CMP_STRIDE, HEAD_DIM, CMP_HIDDEN)
    w1 = jnp.einsum('khjdn,gG->kjgdhGn', w1r, eye).reshape(2, CMP_STRIDE * LANES, 4 * CMP_HIDDEN).astype(MXU_DTYPE)
    w2 = jnp.einsum('knd,gG->kgnGd', cmp_w2, eye).reshape(2, 2 * CMP_HIDDEN, LANES).astype(MXU_DTYPE)
    pe = jnp.pad(cmp_pe.reshape(2, 1, CMP_BLOCK * HEAD_DIM), ((0, 0), (0, 7), (0, 0)))
    b1 = cmp_b1.reshape(2, 1, CMP_HIDDEN)
    rows = CMP_FRONT + n_chunks + CMP_NEAR - CMP_FRONT
    return pl.pallas_call(
        _compress_kernel,
        grid=(2, bsz),
        in_specs=[pl.BlockSpec((1, 1, n_chunks, CMP_STRIDE * LANES), lambda k, b: (k, b, 0, 0)),
                  pl.BlockSpec((1, CMP_STRIDE * LANES, 4 * CMP_HIDDEN), lambda k, b: (k, 0, 0)),
                  pl.BlockSpec((1, 8, CMP_BLOCK * HEAD_DIM), lambda k, b: (k, 0, 0)),
                  pl.BlockSpec((1, CMP_BLOCK * HEAD_DIM, CMP_HIDDEN), lambda k, b: (k, 0, 0)),
                  pl.BlockSpec((1, 1, CMP_HIDDEN), lambda k, b: (k, 0, 0)),
                  pl.BlockSpec((1, 2 * CMP_HIDDEN, LANES), lambda k, b: (k, 0, 0))],
        out_specs=pl.BlockSpec((1, 1, rows, LANES), lambda k, b: (k, b, 0, 0)),
        out_shape=jax.ShapeDtypeStruct((2, bsz, rows, LANES), F32),
        compiler_params=pltpu.CompilerParams(dimension_semantics=("arbitrary", "arbitrary"),
                                             vmem_limit_bytes=VMEM_LIMIT),
        name="nsa_compress",
    )(tok, w1, pe, cmp_w1, b1, w2)


def _stack_heads(q_ref, dst):
    lo = lax.broadcasted_iota(jnp.int32, (Q_BLOCK, LANES), 1) < HEAD_DIM
    for r in range(GROUP):
        qr = q_ref[:, r * LANES:(r + 1) * LANES].astype(dst.dtype)
        z = jnp.zeros_like(qr)
        dst[(2 * r) * Q_BLOCK:(2 * r + 1) * Q_BLOCK, :] = jnp.where(lo, qr, z)
        dst[(2 * r + 1) * Q_BLOCK:(2 * r + 2) * Q_BLOCK, :] = jnp.where(lo, z, qr)


def _pair_heads(o, r):
    lo = lax.broadcasted_iota(jnp.int32, (Q_BLOCK, LANES), 1) < HEAD_DIM
    return jnp.where(lo, o[(2 * r) * Q_BLOCK:(2 * r + 1) * Q_BLOCK], o[(2 * r + 1) * Q_BLOCK:(2 * r + 2) * Q_BLOCK])


def _top_blocks(score, n_top):
    lane = lax.broadcasted_iota(jnp.int32, score.shape, 1)
    sel = jnp.zeros(score.shape, jnp.bool_)
    for _ in range(n_top):
        m = jnp.max(score, axis=1, keepdims=True)
        idx = jnp.min(jnp.where(score == m, lane, LANES), axis=1, keepdims=True)
        hit = lane == idx
        sel = sel | (hit & (m >= 0.0))
        score = jnp.where(hit, -2.0, score)
    return sel


def _attn_kernel(sink_ref, qa_ref, qb_ref, ga_ref, kcmp_ref, vcmp_ref, ks_ref, vs_ref, kw_ref, vw_ref, kb_ref,
                 vb_ref, cmat_ref, e3_ref, tnear_ref, tsel_ref, twin_ref, tswa_ref, oa_ref, ob_ref,
                 qall, mneg, m_s, l_s, acc_s, *, n_far, n_top):
    i = pl.program_id(1)
    rows = N_HEADS * Q_BLOCK
    _stack_heads(qa_ref, qall)
    q = qall[...]

    off = pl.multiple_of(i * (Q_BLOCK // CMP_STRIDE), 8)
    k_far = _mx(kcmp_ref[0, 0, 0:n_far, :])
    v_far = _mx(vcmp_ref[0, 0, 0:n_far, :])
    k_near = _mx(kcmp_ref[0, 0, pl.ds(off, CMP_NEAR), :])
    v_near = _mx(vcmp_ref[0, 0, pl.ds(off, CMP_NEAR), :])
    col = lax.broadcasted_iota(jnp.int32, (rows, n_far), 1)
    s_far = jnp.where((col >= CMP_FRONT) & (col < off), _dot_nt(q, k_far), NEG)
    coln = lax.broadcasted_iota(jnp.int32, (rows, CMP_NEAR), 1)
    s_near = jnp.where(coln + off >= CMP_FRONT, _dot_nt(q, k_near) + tnear_ref[...], NEG)
    m = jnp.maximum(jnp.max(s_far, axis=1, keepdims=True), jnp.max(s_near, axis=1, keepdims=True))
    m = jnp.where(m > 0.5 * NEG, m, 0.0)
    e_far = jnp.exp(s_far - m)
    e_near = jnp.exp(s_near - m)
    denom = jnp.sum(e_far, axis=1, keepdims=True) + jnp.sum(e_near, axis=1, keepdims=True)
    inv = 1.0 / jnp.maximum(denom, 1e-30)
    p_far = e_far * inv
    p_near = e_near * inv
    o_c = _dot(_mx(p_far), v_far) + _dot(_mx(p_near), v_near)

    qrow = lax.broadcasted_iota(jnp.int32, (Q_BLOCK, LANES), 0)
    blk = lax.broadcasted_iota(jnp.int32, (Q_BLOCK, LANES), 1)
    back = (2 * i + (qrow >= SEL_BLOCK).astype(jnp.int32)) - blk
    forced = (blk < SEL_INIT_BLOCKS) | ((back >= 0) & (back < SEL_LOCAL_BLOCKS))
    for g in range(N_GROUPS):
        pg_far = sum(p_far[(2 * r + g) * Q_BLOCK:(2 * r + g + 1) * Q_BLOCK] for r in range(GROUP))
        pg_near = sum(p_near[(2 * r + g) * Q_BLOCK:(2 * r + g + 1) * Q_BLOCK] for r in range(GROUP))
        c_far = _mx(cmat_ref[0:n_far, :])
        c_near = _mx(cmat_ref[pl.ds(off, CMP_NEAR), :])
        imp = jnp.zeros((Q_BLOCK, LANES), F32)
        for part, cm in ((pg_far, c_far), (pg_near, c_near)):
            hi = _mx(part)
            lo = _mx(part - hi.astype(F32))
            imp = imp + _dot(hi, cm) + _dot(lo, cm)
        score = jnp.where(back >= 0, jnp.where(forced, SEL_FORCE, imp), -1.0)
        sel = _top_blocks(score, n_top)
        mneg[g * Q_BLOCK:(g + 1) * Q_BLOCK, :] = jnp.where(sel, 0.0, NEG).astype(mneg.dtype)

    m_s[...] = jnp.full(m_s.shape, NEG, F32)
    l_s[...] = jnp.zeros(l_s.shape, F32)
    acc_s[...] = jnp.zeros(acc_s.shape, F32)

    def sel_step(j, extra):
        start = pl.multiple_of(j * Q_BLOCK, Q_BLOCK)
        kc = _mx(ks_ref[pl.ds(start, Q_BLOCK), :])
        vc = _mx(vs_ref[pl.ds(start, Q_BLOCK), :])
        madd = _dot(mneg[...], e3_ref[j])
        s = _dot_nt(q, kc) + jnp.concatenate([madd] * GROUP, axis=0)
        if extra is not None:
            s = s + extra
        m_old = m_s[...]
        m_new = jnp.maximum(m_old, jnp.max(s, axis=1, keepdims=True))
        alpha = jnp.exp(m_old - m_new)
        p = jnp.exp(s - m_new)
        l_s[...] = alpha * l_s[...] + jnp.sum(p, axis=1, keepdims=True)
        acc_s[...] = alpha * acc_s[...] + _dot(_mx(p), vc)
        m_s[...] = m_new

    def far_body(j, carry):
        sel_step(j, None)
        return carry

    lax.fori_loop(0, jnp.maximum(i - 1, 0), far_body, 0)
    prev_ok = jnp.where(i > 0, 0.0, NEG)
    sel_step(jnp.maximum(i - 1, 0), tsel_ref[:, 0:Q_BLOCK] + prev_ok)
    sel_step(i, tsel_ref[:, Q_BLOCK:2 * Q_BLOCK])
    o_s = acc_s[...] / l_s[...]

    wpad = kw_ref.shape[1] - ks_ref.shape[0]
    wstart = pl.multiple_of(i * Q_BLOCK, Q_BLOCK)
    kwin = _mx(kw_ref[0, pl.ds(wstart, wpad + Q_BLOCK), :])
    vwin = _mx(vw_ref[0, pl.ds(wstart, wpad + Q_BLOCK), :])
    colw = lax.broadcasted_iota(jnp.int32, (rows, wpad + Q_BLOCK), 1)
    s = jnp.where(colw + wstart >= wpad, _dot_nt(q, kwin) + twin_ref[...], NEG)
    m = jnp.max(s, axis=1, keepdims=True)
    p = jnp.exp(s - m)
    o_w = _dot(_mx(p), vwin) / jnp.sum(p, axis=1, keepdims=True)

    lo = lax.broadcasted_iota(jnp.int32, (Q_BLOCK, LANES), 1) < HEAD_DIM
    gates = ga_ref[...]
    for r in range(GROUP):
        tile = jnp.zeros((Q_BLOCK, LANES), F32)
        for c, o in enumerate((o_c, o_s, o_w)):
            g0 = gates[:, c * 8 + 2 * r:c * 8 + 2 * r + 1]
            g1 = gates[:, c * 8 + 2 * r + 1:c * 8 + 2 * r + 2]
            tile = tile + jnp.where(lo, g0, g1) * _pair_heads(o, r)
        oa_ref[:, r * LANES:(r + 1) * LANES] = tile.astype(oa_ref.dtype)

    _stack_heads(qb_ref, qall)
    q = qall[...]
    bpad = kb_ref.shape[1] - ks_ref.shape[0]
    kwin = _mx(kb_ref[0, pl.ds(wstart, bpad + Q_BLOCK), :])
    vwin = _mx(vb_ref[0, pl.ds(wstart, bpad + Q_BLOCK), :])
    colb = lax.broadcasted_iota(jnp.int32, (rows, bpad + Q_BLOCK), 1)
    s = jnp.where(colb + wstart >= bpad, _dot_nt(q, kwin) + tswa_ref[...], NEG)
    sink = jnp.concatenate([jnp.full((Q_BLOCK, 1), sink_ref[(h % 2) * GROUP + h // 2], F32) for h in range(N_HEADS)],
                           axis=0)
    m = jnp.maximum(jnp.max(s, axis=1, keepdims=True), sink)
    p = jnp.exp(s - m)
    o_b = _dot(_mx(p), vwin) / (jnp.sum(p, axis=1, keepdims=True) + jnp.exp(sink - m))
    for r in range(GROUP):
        ob_ref[:, r * LANES:(r + 1) * LANES] = _pair_heads(o_b, r).astype(ob_ref.dtype)


def _rel_bucket(dist):
    n = jnp.maximum(dist, 0)
    max_exact = REL_BUCKETS // 2
    nf = jnp.maximum(n, 1).astype(F32)
    log_b = max_exact + (jnp.log(nf / max_exact) / math.log(REL_MAX_DIST / max_exact)
                         * (REL_BUCKETS - max_exact)).astype(jnp.int32)
    log_b = jnp.minimum(log_b, REL_BUCKETS - 1)
    return jnp.where(n < max_exact, n, log_b)


def _bias_tile(tab, dist, valid, shift_far):
    b = tab[_rel_bucket(jnp.asarray(dist))]
    if shift_far:
        b = b - tab[REL_BUCKETS - 1][None, None, :]
    b = jnp.where(jnp.asarray(valid)[:, :, None], b, NEG)
    order = np.array([(h % 2) * GROUP + h // 2 for h in range(N_HEADS)])
    b = jnp.transpose(b[:, :, order], (2, 0, 1))
    return b.reshape(N_HEADS * Q_BLOCK, -1).astype(F32)


def _attention(proj, kvcmp, sinks, bias_table, bsz, seq):
    nq = seq // Q_BLOCK
    n_far = seq // CMP_STRIDE
    n_sel = seq // SEL_BLOCK
    n_top = min(SEL_TOP_N, n_sel)
    wpad = Q_BLOCK * (-(-(NSA_WINDOW - 1) // Q_BLOCK))
    bpad = Q_BLOCK * (-(-(SWA_WINDOW - 1) // Q_BLOCK))
    tab_a = bias_table[:, :N_HEADS].astype(F32)
    tab_b = bias_table[:, N_HEADS:].astype(F32)
    qi = np.arange(Q_BLOCK)[:, None]
    d = qi - CMP_STRIDE * np.arange(CMP_NEAR)[None, :] + (CMP_STRIDE * CMP_FRONT - (CMP_BLOCK - 1))
    t_near = _bias_tile(tab_a, d, d >= 0, True)
    d = qi + Q_BLOCK - np.arange(2 * Q_BLOCK)[None, :]
    t_sel = _bias_tile(tab_a, d, d >= 0, True)
    d = qi + wpad - np.arange(wpad + Q_BLOCK)[None, :]
    t_win = _bias_tile(tab_a, d, (d >= 0) & (d < NSA_WINDOW), False)
    d = qi + bpad - np.arange(bpad + Q_BLOCK)[None, :]
    t_swa = _bias_tile(tab_b, d, (d >= 0) & (d < SWA_WINDOW), False)
    n_rows = kvcmp.shape[2]
    cn = (np.arange(n_rows) - CMP_FRONT)[:, None] * CMP_STRIDE
    sj = np.arange(LANES)[None, :] * SEL_BLOCK
    cmat = ((cn < sj + SEL_BLOCK) & (cn + CMP_BLOCK > sj) & (cn >= 0) & (cn + CMP_BLOCK <= seq)
            & (sj < seq)).astype(np.float32)
    cmat = jnp.asarray(cmat, F32)
    e3 = (np.arange(LANES)[None, :, None]
          == (2 * np.arange(nq)[:, None, None] + np.arange(Q_BLOCK)[None, None, :] // SEL_BLOCK))
    e3 = jnp.asarray(e3.astype(np.float32), MXU_DTYPE)
    kw = jnp.pad(proj['kw'].reshape(bsz, seq, LANES), ((0, 0), (wpad, 0), (0, 0)))
    vw = jnp.pad(proj['vw'].reshape(bsz, seq, LANES), ((0, 0), (wpad, 0), (0, 0)))
    kb = jnp.pad(proj['kb'].reshape(bsz, seq, LANES), ((0, 0), (bpad, 0), (0, 0)))
    vb = jnp.pad(proj['vb'].reshape(bsz, seq, LANES), ((0, 0), (bpad, 0), (0, 0)))
    rows = N_HEADS * Q_BLOCK
    qspec = pl.BlockSpec((Q_BLOCK, 4 * LANES), lambda b, i: (b * nq + i, 0))
    const2 = lambda shape: pl.BlockSpec(shape, lambda b, i: (0, 0))
    kernel = functools.partial(_attn_kernel, n_far=n_far, n_top=n_top)
    return pl.pallas_call(
        kernel,
        grid=(bsz, nq),
        in_specs=[pl.BlockSpec(memory_space=pltpu.SMEM),
                  qspec, qspec,
                  pl.BlockSpec((Q_BLOCK, LANES), lambda b, i: (b * nq + i, 0)),
                  pl.BlockSpec((1, 1, n_rows, LANES), lambda b, i: (0, b, 0, 0)),
                  pl.BlockSpec((1, 1, n_rows, LANES), lambda b, i: (1, b, 0, 0)),
                  pl.BlockSpec((seq, LANES), lambda b, i: (b, 0)),
                  pl.BlockSpec((seq, LANES), lambda b, i: (b, 0)),
                  pl.BlockSpec((1, seq + wpad, LANES), lambda b, i: (b, 0, 0)),
                  pl.BlockSpec((1, seq + wpad, LANES), lambda b, i: (b, 0, 0)),
                  pl.BlockSpec((1, seq + bpad, LANES), lambda b, i: (b, 0, 0)),
                  pl.BlockSpec((1, seq + bpad, LANES), lambda b, i: (b, 0, 0)),
                  const2((n_rows, LANES)),
                  pl.BlockSpec((nq, LANES, Q_BLOCK), lambda b, i: (0, 0, 0)),
                  const2((rows, CMP_NEAR)),
                  const2((rows, 2 * Q_BLOCK)),
                  const2((rows, wpad + Q_BLOCK)),
                  const2((rows, bpad + Q_BLOCK))],
        out_specs=[qspec, qspec],
        out_shape=[jax.ShapeDtypeStruct((bsz * seq, 4 * LANES), BF16)] * 2,
        scratch_shapes=[pltpu.VMEM((rows, LANES), MXU_DTYPE),
                        pltpu.VMEM((N_GROUPS * Q_BLOCK, LANES), MXU_DTYPE),
                        pltpu.VMEM((rows, 1), F32),
                        pltpu.VMEM((rows, 1), F32),
                        pltpu.VMEM((rows, LANES), F32)],
        compiler_params=pltpu.CompilerParams(dimension_semantics=("arbitrary", "arbitrary"),
                                             vmem_limit_bytes=VMEM_LIMIT),
        name="attention",
    )(sinks.astype(F32), proj['qa'], proj['qb'], proj['ga'], kvcmp, kvcmp, proj['ks'], proj['vs'], kw, vw, kb, vb,
      cmat, e3, t_near, t_sel, t_win, t_swa)


def _layer_norm(y, g, b):
    mu = jnp.mean(y, axis=-1, keepdims=True)
    yc = y - mu
    var = jnp.mean(yc * yc, axis=-1, keepdims=True)
    return yc * lax.rsqrt(var + LN_EPS) * g + b


def _outproj_kernel(oa_ref, ob_ref, sg_ref, x_ref, pa_ref, pb_ref, wo_ref, g1_ref, b1_ref, wr_ref, rb_ref, sgu_ref,
                    sd_ref, tri_ref, h_ref, base_ref, eidx_ref, gate_ref, rank_ref, cnt_ref, carry):
    step = pl.program_id(0)
    tm = oa_ref.shape[0]

    @pl.when(step == 0)
    def _():
        carry[...] = jnp.zeros(carry.shape, F32)

    sg = sg_ref[...].astype(F32)
    merged = (sg[:, :D_MODEL] * _dot(_mx(oa_ref[...]), pa_ref[...])
              + sg[:, D_MODEL:] * _dot(_mx(ob_ref[...]), pb_ref[...]))
    mix = _dot(_mx(merged), wo_ref[...])
    h = _layer_norm(DN_ALPHA * x_ref[...] + mix, g1_ref[...], b1_ref[...])
    h_ref[...] = h
    hb = _mx(h)

    gu = _dot(hb, sgu_ref[...])
    shared = _dot(_mx(jax.nn.silu(gu[:, :SHARED_HIDDEN]) * gu[:, SHARED_HIDDEN:]), sd_ref[...])
    base_ref[...] = DN_ALPHA * h + shared

    scores = jax.nn.sigmoid(_dot_nt(wr_ref[...], hb))
    choice = scores + rb_ref[:, 0:1]
    per_group = N_EXPERTS // N_EXPERT_GROUPS
    gs = []
    for g in range(N_EXPERT_GROUPS):
        cg = choice[g * per_group:(g + 1) * per_group]
        m1 = jnp.max(cg, axis=0, keepdims=True)
        is_m = cg == m1
        n_m = jnp.sum(is_m.astype(F32), axis=0, keepdims=True)
        m2 = jnp.max(jnp.where(is_m, -jnp.inf, cg), axis=0, keepdims=True)
        gs.append(m1 + jnp.where(n_m > 1.5, m1, m2))
    gs = jnp.concatenate(gs, axis=0)
    gid = lax.broadcasted_iota(jnp.int32, gs.shape, 0)
    beaten = jnp.zeros(gs.shape, jnp.int32)
    for g in range(N_EXPERT_GROUPS):
        other = gs[g:g + 1]
        beaten = beaten + ((other > gs) | ((other == gs) & (g < gid))).astype(jnp.int32)
    keep_g = beaten < TOPK_EXPERT_GROUPS
    keep = jnp.concatenate([jnp.broadcast_to(keep_g[g:g + 1], (per_group, tm)) for g in range(N_EXPERT_GROUPS)],
                           axis=0)
    cand = jnp.where(keep, choice, -jnp.inf)
    eid = lax.broadcasted_iota(jnp.int32, cand.shape, 0)
    hits = []
    e_rows = []
    w_rows = []
    for _ in range(TOP_K):
        m = jnp.max(cand, axis=0, keepdims=True)
        idx = jnp.min(jnp.where(cand == m, eid, N_EXPERTS), axis=0, keepdims=True)
        hit = eid == idx
        hits.append(hit)
        e_rows.append(idx)
        w_rows.append(jnp.sum(jnp.where(hit, scores, 0.0), axis=0, keepdims=True))
        cand = jnp.where(hit, -jnp.inf, cand)
    w = jnp.concatenate(w_rows, axis=0)
    gate_ref[...] = w / jnp.sum(w, axis=0, keepdims=True) * ROUTED_SCALE
    eidx_ref[...] = jnp.concatenate(e_rows, axis=0)

    onehot = jnp.zeros(cand.shape, F32)
    for hit in hits:
        onehot = onehot + hit.astype(F32)
    before = _dot(onehot.astype(BF16), tri_ref[...]) + carry[:, 0:1]
    rank_ref[...] = jnp.concatenate(
        [jnp.sum(jnp.where(hit, before, 0.0), axis=0, keepdims=True) for hit in hits], axis=0).astype(jnp.int32)
    carry[...] = carry[...] + jnp.sum(onehot, axis=1, keepdims=True)
    cnt_ref[...] = carry[...]


def _out_projection(oa, ob, sg, x2, proj_a, proj_b, w_out, ln_g, ln_b, w_router, router_bias, s_gate, s_up, s_down):
    t = x2.shape[0]
    tm = OUT_TM
    perm = _head_pair_perm()
    pa = proj_a[perm].astype(MXU_DTYPE)
    pb = proj_b[perm].astype(MXU_DTYPE)
    tri = jnp.asarray(np.triu(np.ones((tm, tm), np.float32), 1), BF16)
    row = lambda i: (i, 0)
    fixed = lambda i: (0, 0)
    col = lambda i: (0, i)
    outs = pl.pallas_call(
        _outproj_kernel,
        grid=(t // tm,),
        in_specs=[pl.BlockSpec((tm, 4 * LANES), row), pl.BlockSpec((tm, 4 * LANES), row),
                  pl.BlockSpec((tm, 2 * D_MODEL), row), pl.BlockSpec((tm, D_MODEL), row),
                  pl.BlockSpec((4 * LANES, D_MODEL), fixed), pl.BlockSpec((4 * LANES, D_MODEL), fixed),
                  pl.BlockSpec((D_MODEL, D_MODEL), fixed),
                  pl.BlockSpec((1, D_MODEL), fixed), pl.BlockSpec((1, D_MODEL), fixed),
                  pl.BlockSpec((N_EXPERTS, D_MODEL), fixed), pl.BlockSpec((N_EXPERTS, LANES), fixed),
                  pl.BlockSpec((D_MODEL, 2 * SHARED_HIDDEN), fixed), pl.BlockSpec((SHARED_HIDDEN, D_MODEL), fixed),
                  pl.BlockSpec((tm, tm), fixed)],
        out_specs=[pl.BlockSpec((tm, D_MODEL), row), pl.BlockSpec((tm, D_MODEL), row),
                   pl.BlockSpec((TOP_K, tm), col), pl.BlockSpec((TOP_K, tm), col), pl.BlockSpec((TOP_K, tm), col),
                   pl.BlockSpec((N_EXPERTS, LANES), fixed)],
        out_shape=[jax.ShapeDtypeStruct((t, D_MODEL), F32), jax.ShapeDtypeStruct((t, D_MODEL), F32),
                   jax.ShapeDtypeStruct((TOP_K, t), jnp.int32), jax.ShapeDtypeStruct((TOP_K, t), F32),
                   jax.ShapeDtypeStruct((TOP_K, t), jnp.int32), jax.ShapeDtypeStruct((N_EXPERTS, LANES), F32)],
        scratch_shapes=[pltpu.VMEM((N_EXPERTS, LANES), F32)],
        compiler_params=pltpu.CompilerParams(dimension_semantics=("arbitrary",), vmem_limit_bytes=VMEM_LIMIT),
        name="out_projection_router",
    )(oa, ob, sg, x2, pa, pb, w_out.astype(MXU_DTYPE), ln_g.reshape(1, -1), ln_b.reshape(1, -1),
      w_router.T.astype(MXU_DTYPE), jnp.broadcast_to(router_bias.astype(F32)[:, None], (N_EXPERTS, LANES)),
      jnp.concatenate([s_gate, s_up], axis=1).astype(MXU_DTYPE), s_down.astype(MXU_DTYPE), tri)
    return outs


def _row_copy_wait(src, dst, sem, n):
    def body(_, c):
        pltpu.make_async_copy(src, dst, sem).wait()
        return c
    lax.fori_loop(0, n, body, 0)


def _dispatch_kernel(zstart_ref, cnt_ref, dest_ref, h_ref, xs_ref, zeros, sem):
    step = pl.program_id(0)
    tm = h_ref.shape[0]

    @pl.when(step == 0)
    def _():
        zeros[...] = jnp.zeros(zeros.shape, F32)

        def fill(e, c):
            @pl.when(cnt_ref[e] > 0)
            def _():
                cp = pltpu.make_async_copy(zeros, xs_ref.at[pl.ds(zstart_ref[e], MOE_BM)], sem)
                cp.start()
                cp.wait()
            return c
        lax.fori_loop(0, N_EXPERTS, fill, 0)

    def issue(t, c):
        for k in range(TOP_K):
            pltpu.make_async_copy(h_ref.at[t], xs_ref.at[dest_ref[k, t]], sem).start()
        return c
    lax.fori_loop(0, tm, issue, 0)
    _row_copy_wait(h_ref.at[0], xs_ref.at[0], sem, tm * TOP_K)


def _dispatch(h3, dest, zstart, counts, n_rows):
    t = h3.shape[0]
    tm = DISP_TM
    return pl.pallas_call(
        _dispatch_kernel,
        grid_spec=pltpu.PrefetchScalarGridSpec(
            num_scalar_prefetch=2,
            grid=(t // tm,),
            in_specs=[pl.BlockSpec((TOP_K, tm), lambda i, *_: (0, i), memory_space=pltpu.SMEM),
                      pl.BlockSpec((tm,) + ROW_TILE, lambda i, *_: (i, 0, 0))],
            out_specs=pl.BlockSpec(memory_space=pl.ANY),
            scratch_shapes=[pltpu.VMEM((MOE_BM,) + ROW_TILE, F32), pltpu.SemaphoreType.DMA(())]),
        out_shape=jax.ShapeDtypeStruct((n_rows,) + ROW_TILE, F32),
        compiler_params=pltpu.CompilerParams(dimension_semantics=("arbitrary",), vmem_limit_bytes=VMEM_LIMIT),
        name="moe_dispatch",
    )(zstart, counts, dest, h3)


def _experts_kernel(blk_e_ref, nused_ref, xs_ref, wg_ref, wu_ref, wd_ref, ys_ref, wg_s, wu_s, wd_s):
    b = pl.program_id(0)
    prev = blk_e_ref[jnp.maximum(b - 1, 0)]

    @pl.when((b == 0) | (blk_e_ref[b] != prev))
    def _():
        wg_s[...] = _mx(wg_ref[0])
        wu_s[...] = _mx(wu_ref[0])
        wd_s[...] = _mx(wd_ref[0])

    @pl.when(b < nused_ref[0])
    def _():
        xb = _mx(xs_ref[...])
        hid = jax.nn.silu(_dot(xb, wg_s[...])) * _dot(xb, wu_s[...])
        ys_ref[...] = _dot(_mx(hid), wd_s[...])

    @pl.when(b >= nused_ref[0])
    def _():
        ys_ref[...] = jnp.zeros(ys_ref.shape, F32)


def _experts(xs, blk_e, nused, e_gate, e_up, e_down):
    n_rows = xs.shape[0]
    n_blocks = n_rows // MOE_BM
    xmap = lambda b, be, nu: (jnp.minimum(b, nu[0] - 1), 0)
    wmap = lambda b, be, nu: (be[b], 0, 0)
    return pl.pallas_call(
        _experts_kernel,
        grid_spec=pltpu.PrefetchScalarGridSpec(
            num_scalar_prefetch=2,
            grid=(n_blocks,),
            in_specs=[pl.BlockSpec((MOE_BM, D_MODEL), xmap),
                      pl.BlockSpec((1, D_MODEL, EXPERT_HIDDEN), wmap),
                      pl.BlockSpec((1, D_MODEL, EXPERT_HIDDEN), wmap),
                      pl.BlockSpec((1, EXPERT_HIDDEN, D_MODEL), wmap)],
            out_specs=pl.BlockSpec((MOE_BM, D_MODEL), lambda b, be, nu: (b, 0)),
            scratch_shapes=[pltpu.VMEM((D_MODEL, EXPERT_HIDDEN), MXU_DTYPE),
                            pltpu.VMEM((D_MODEL, EXPERT_HIDDEN), MXU_DTYPE),
                            pltpu.VMEM((EXPERT_HIDDEN, D_MODEL), MXU_DTYPE)]),
        out_shape=jax.ShapeDtypeStruct((n_rows, D_MODEL), F32),
        compiler_params=pltpu.CompilerParams(dimension_semantics=("arbitrary",), vmem_limit_bytes=VMEM_LIMIT),
        name="moe_experts",
    )(blk_e, nused, xs, e_gate, e_up, e_down)


def _combine_kernel(dest_ref, gate_ref, base_ref, g2_ref, b2_ref, ys_ref, out_ref, buf, sem):
    tm = base_ref.shape[0]

    def issue(t, c):
        for k in range(TOP_K):
            pltpu.make_async_copy(ys_ref.at[dest_ref[k, t]], buf.at[k, t], sem).start()
        return c
    lax.fori_loop(0, tm, issue, 0)
    _row_copy_wait(ys_ref.at[0], buf.at[0, 0], sem, tm * TOP_K)

    def token(t, c):
        y = base_ref[t]
        for k in range(TOP_K):
            y = y + gate_ref[k, t] * buf[k, t]
        out_ref[t] = y
        return c
    lax.fori_loop(0, tm, token, 0)
    y = out_ref[...]
    n = float(D_MODEL)
    mu = jnp.sum(jnp.sum(y, axis=2, keepdims=True), axis=1, keepdims=True) / n
    yc = y - mu
    var = jnp.sum(jnp.sum(yc * yc, axis=2, keepdims=True), axis=1, keepdims=True) / n
    out_ref[...] = yc * lax.rsqrt(var + LN_EPS) * g2_ref[...] + b2_ref[...]


def _combine(ys3, dest, gate, base3, ln_g, ln_b):
    t = base3.shape[0]
    tm = COMB_TM
    return pl.pallas_call(
        _combine_kernel,
        grid=(t // tm,),
        in_specs=[pl.BlockSpec((TOP_K, tm), lambda i: (0, i), memory_space=pltpu.SMEM),
                  pl.BlockSpec((TOP_K, tm), lambda i: (0, i), memory_space=pltpu.SMEM),
                  pl.BlockSpec((tm,) + ROW_TILE, lambda i: (i, 0, 0)),
                  pl.BlockSpec(ROW_TILE, lambda i: (0, 0)),
                  pl.BlockSpec(ROW_TILE, lambda i: (0, 0)),
                  pl.BlockSpec(memory_space=pl.ANY)],
        out_specs=pl.BlockSpec((tm,) + ROW_TILE, lambda i: (i, 0, 0)),
        out_shape=jax.ShapeDtypeStruct((t,) + ROW_TILE, F32),
        scratch_shapes=[pltpu.VMEM((TOP_K, tm) + ROW_TILE, F32), pltpu.SemaphoreType.DMA(())],
        compiler_params=pltpu.CompilerParams(dimension_semantics=("arbitrary",), vmem_limit_bytes=VMEM_LIMIT),
        name="moe_combine",
    )(dest, gate, base3, ln_g.reshape(ROW_TILE), ln_b.reshape(ROW_TILE), ys3)


def _moe_layout(eidx, rank, counts):
    n_assign = eidx.size
    n_blocks = (n_assign + N_EXPERTS * (MOE_BM - 1)) // MOE_BM
    padded = (counts + MOE_BM - 1) // MOE_BM * MOE_BM
    pends = jnp.cumsum(padded)
    pstarts = pends - padded
    dest = pstarts[eidx] + rank
    blk_e = jnp.minimum(jnp.searchsorted(pends, jnp.arange(n_blocks, dtype=jnp.int32) * MOE_BM, side='right'),
                        N_EXPERTS - 1).astype(jnp.int32)
    nused = (pends[-1:] // MOE_BM).astype(jnp.int32)
    zstart = jnp.maximum(pends - MOE_BM, 0).astype(jnp.int32)
    return dest.astype(jnp.int32), blk_e, nused, zstart, n_blocks * MOE_BM


def _layer(x, w_in, cmp_pe, cmp_w1, cmp_b1, cmp_w2, sinks, bias_table, proj_a, proj_b, w_out, ln1_g, ln1_b,
           w_router, router_bias, e_gate, e_up, e_down, s_gate, s_up, s_down, ln2_g, ln2_b):
    bsz, seq, d = x.shape
    x2 = x.reshape(bsz * seq, d)
    proj = _in_projection(x2, w_in)
    kvcmp = _compress(proj['kc'], proj['vc'], bsz, seq, cmp_pe, cmp_w1, cmp_b1, cmp_w2)
    oa, ob = _attention(proj, kvcmp, sinks, bias_table, bsz, seq)
    h, base, eidx, gate, rank, cnt = _out_projection(oa, ob, proj['sg'], x2, proj_a, proj_b, w_out, ln1_g, ln1_b,
                                                     w_router, router_bias, s_gate, s_up, s_down)
    counts = cnt[:, 0].astype(jnp.int32)
    dest, blk_e, nused, zstart, n_rows = _moe_layout(eidx, rank, counts)
    xs = _dispatch(h.reshape((-1,) + ROW_TILE), dest, zstart, counts, n_rows)
    ys = _experts(xs.reshape(n_rows, d), blk_e, nused, e_gate, e_up, e_down)
    out = _combine(ys.reshape((n_rows,) + ROW_TILE), dest, gate, base.reshape((-1,) + ROW_TILE), ln2_g, ln2_b)
    return out.reshape(bsz, seq, d)


def kernel(x, w_in, cmp_pe, cmp_w1, cmp_b1, cmp_w2, attn_sinks, rel_bias_table, proj_a, proj_b, w_out, ln1_g, ln1_b,
           w_router, router_bias, expert_w_gate, expert_w_up, expert_w_down, shared_w_gate, shared_w_up,
           shared_w_down, ln2_g, ln2_b):
    h = x
    for l in range(DEPTH):
        h = _layer(h, w_in[l], cmp_pe[l], cmp_w1[l], cmp_b1[l], cmp_w2[l], attn_sinks[l], rel_bias_table, proj_a[l],
                   proj_b[l], w_out[l], ln1_g[l], ln1_b[l], w_router[l], router_bias[l], expert_w_gate[l],
                   expert_w_up[l], expert_w_down[l], shared_w_gate[l], shared_w_up[l], shared_w_down[l], ln2_g[l],
                   ln2_b[l])
    return h
```

```python
import functools
import math

import numpy as np
import jax
import jax.numpy as jnp
from jax import lax
from jax.experimental import pallas as pl
from jax.experimental.pallas import tpu as pltpu

F32 = jnp.float32
BF16 = jnp.bfloat16
MXU_DTYPE = jnp.bfloat16

D_MODEL = 1024
HEAD_DIM = 64
ATTN_SCALE = HEAD_DIM ** -0.5
Q_BLOCK = 128
N_HEADS = 8
N_GROUPS = 2
GROUP = 4
CMP_BLOCK = 32
CMP_STRIDE = 16
CMP_HIDDEN = 128
SEL_BLOCK = 64
SEL_TOP_N = 8
SEL_INIT_BLOCKS = 1
SEL_LOCAL_BLOCKS = 2
NSA_WINDOW = 512
SWA_WINDOW = 128
REL_BUCKETS = 32
REL_MAX_DIST = 128
N_EXPERTS = 256
TOP_K = 8
EXPERT_HIDDEN = 256
SHARED_HIDDEN = 256
N_EXPERT_GROUPS = 8
TOPK_EXPERT_GROUPS = 4
ROUTED_SCALE = 2.5
LN_EPS = 1e-5
DEPTH = 1
DN_ALPHA = (2 * DEPTH) ** 0.25

NEG = -1e30
LANES = 128
ROW_TILE = (8, LANES)
CMP_FRONT = 16
CMP_NEAR = LANES
SEL_CHUNK = 512
VMEM_LIMIT = 56 * 1024 * 1024

IN_TM = 512
OUT_TM = 256
MOE_BM = 256
DISP_TM = 256
COMB_TM = 128


def _dot(a, b):
    return jnp.dot(a, b, preferred_element_type=F32)


def _dot_nt(a, b):
    return lax.dot_general(a, b, (((1,), (1,)), ((), ())), preferred_element_type=F32)


def _mx(a):
    return a.astype(MXU_DTYPE)


_IN_COLS = (('qa', 512), ('qb', 512), ('kc', 128), ('vc', 128), ('ks', 128), ('vs', 128), ('kw', 128),
            ('vw', 128), ('kb', 128), ('vb', 128), ('ga', 128), ('sg', 2048))


def _inproj_kernel(x_ref, w_ref, qa_ref, qb_ref, kc_ref, vc_ref, ks_ref, vs_ref, kw_ref, vw_ref, kb_ref, vb_ref,
                   ga_ref, sg_ref):
    xb = _mx(x_ref[...])
    outs = dict(qa=qa_ref, qb=qb_ref, kc=kc_ref, vc=vc_ref, ks=ks_ref, vs=vs_ref, kw=kw_ref, vw=vw_ref,
                kb=kb_ref, vb=vb_ref, ga=ga_ref, sg=sg_ref)
    off = 0
    for name, width in _IN_COLS:
        for c0 in range(0, width, 512):
            cw = min(512, width - c0)
            y = _dot(xb, w_ref[:, off + c0:off + c0 + cw])
            if name in ('ga', 'sg'):
                y = jax.nn.sigmoid(y)
            outs[name][:, c0:c0 + cw] = y.astype(outs[name].dtype)
        off += width


def _pair_head_columns(w):
    return w.reshape(w.shape[0], N_GROUPS, GROUP, HEAD_DIM).transpose(0, 2, 1, 3).reshape(w.shape[0], -1)


def _in_projection(x2, w_in):
    t = x2.shape[0]
    sizes = (512, 128, 128, 128, 128, 128, 128, 24, 512, 128, 128, 1024, 1024)
    offs = np.cumsum((0,) + sizes)
    part = [w_in[:, offs[k]:offs[k + 1]] for k in range(len(sizes))]
    w_qa, w_kc, w_vc, w_ks, w_vs, w_kw, w_vw, w_g, w_qb, w_kb, w_vb, w_gate_a, w_gate_b = part
    w_qa = _pair_head_columns(w_qa) * ATTN_SCALE
    w_qb = _pair_head_columns(w_qb) * ATTN_SCALE
    w_ga = w_g.reshape(-1, N_GROUPS, GROUP, 3).transpose(0, 3, 2, 1).reshape(-1, 24)
    w_ga = jnp.pad(w_ga, ((0, 0), (0, LANES - 24)))
    w_all = jnp.concatenate([w_qa, w_qb, w_kc, w_vc, w_ks, w_vs, w_kw, w_vw, w_kb, w_vb, w_ga, w_gate_a, w_gate_b],
                            axis=1).astype(MXU_DTYPE)
    n_all = w_all.shape[1]
    out_shape = []
    out_specs = []
    for name, width in _IN_COLS:
        dt = F32 if name == 'ga' else BF16
        out_shape.append(jax.ShapeDtypeStruct((t, width), dt))
        out_specs.append(pl.BlockSpec((IN_TM, width), lambda i: (i, 0)))
    outs = pl.pallas_call(
        _inproj_kernel,
        grid=(t // IN_TM,),
        in_specs=[pl.BlockSpec((IN_TM, D_MODEL), lambda i: (i, 0)),
                  pl.BlockSpec((D_MODEL, n_all), lambda i: (0, 0))],
        out_specs=out_specs,
        out_shape=out_shape,
        compiler_params=pltpu.CompilerParams(dimension_semantics=("arbitrary",), vmem_limit_bytes=VMEM_LIMIT),
        name="in_projection",
    )(x2, w_all)
    return dict(zip([n for n, _ in _IN_COLS], outs))


def _compress_kernel(tok_ref, w1_ref, pe_ref, w1o_ref, b1_ref, w2_ref, out_ref):
    n_chunks = tok_ref.shape[2]
    ab = _dot(tok_ref[0, 0], w1_ref[0])
    a = ab[:, :2 * CMP_HIDDEN]
    b_next = pltpu.roll(ab[:, 2 * CMP_HIDDEN:], n_chunks - 1, 0)
    cb = _dot(_mx(pe_ref[0]), _mx(w1o_ref[0]))[0:1, :] + b1_ref[0]
    cb2 = jnp.concatenate([cb, cb], axis=1)
    hid = jax.nn.gelu(a + b_next + cb2)
    out = _dot(_mx(hid), w2_ref[0])
    row = lax.broadcasted_iota(jnp.int32, out.shape, 0)
    out = jnp.where(row < n_chunks - 1, out, 0.0)
    out_ref[0, 0, 0:CMP_FRONT, :] = jnp.zeros((CMP_FRONT, LANES), F32)
    out_ref[0, 0, CMP_FRONT:CMP_FRONT + n_chunks, :] = out
    out_ref[0, 0, CMP_FRONT + n_chunks:, :] = jnp.zeros((CMP_NEAR - CMP_FRONT, LANES), F32)


def _compress(kc, vc, bsz, seq, cmp_pe, cmp_w1, cmp_b1, cmp_w2):
    n_chunks = seq // CMP_STRIDE
    tok = jnp.stack([kc, vc]).reshape(2, bsz, n_chunks, CMP_STRIDE * LANES)
    eye = jnp.eye(N_GROUPS, dtype=F32)
    w1r = cmp_w1.reshape(2, 2, CMP_STRIDE, HEAD_DIM, CMP_HIDDEN)
    w1 = jnp.einsum('khjdn,gG->kjgdhGn', w1r, eye).reshape(2, CMP_STRIDE * LANES, 4 * CMP_HIDDEN).astype(MXU_DTYPE)
    w2 = jnp.einsum('knd,gG->kgnGd', cmp_w2, eye).reshape(2, 2 * CMP_HIDDEN, LANES).astype(MXU_DTYPE)
    pe = jnp.pad(cmp_pe.reshape(2, 1, CMP_BLOCK * HEAD_DIM), ((0, 0), (0, 7), (0, 0)))
    b1 = cmp_b1.reshape(2, 1, CMP_HIDDEN)
    rows = CMP_FRONT + n_chunks + CMP_NEAR - CMP_FRONT
    return pl.pallas_call(
        _compress_kernel,
        grid=(2, bsz),
        in_specs=[pl.BlockSpec((1, 1, n_chunks, CMP_STRIDE * LANES), lambda k, b: (k, b, 0, 0)),
                  pl.BlockSpec((1, CMP_STRIDE * LANES, 4 * CMP_HIDDEN), lambda k, b: (k, 0, 0)),
                  pl.BlockSpec((1, 8, CMP_BLOCK * HEAD_DIM), lambda k, b: (k, 0, 0)),
                  pl.BlockSpec((1, CMP_BLOCK * HEAD_DIM, CMP_HIDDEN), lambda k, b: (k, 0, 0)),
                  pl.BlockSpec((1, 1, CMP_HIDDEN), lambda k, b: (k, 0, 0)),
                  pl.BlockSpec((1, 2 * CMP_HIDDEN, LANES), lambda k, b: (k, 0, 0))],
        out_specs=pl.BlockSpec((1, 1, rows, LANES), lambda k, b: (k, b, 0, 0)),
        out_shape=jax.ShapeDtypeStruct((2, bsz, rows, LANES), F32),
        compiler_params=pltpu.CompilerParams(dimension_semantics=("arbitrary", "arbitrary"),
                                             vmem_limit_bytes=VMEM_LIMIT),
        name="nsa_compress",
    )(tok, w1, pe, cmp_w1, b1, w2)


def _stack_heads(q_ref, dst):
    lo = lax.broadcasted_iota(jnp.int32, (Q_BLOCK, LANES), 1) < HEAD_DIM
    for r in range(GROUP):
        qr = q_ref[:, r * LANES:(r + 1) * LANES].astype(dst.dtype)
        z = jnp.zeros_like(qr)
        dst[(2 * r) * Q_BLOCK:(2 * r + 1) * Q_BLOCK, :] = jnp.where(lo, qr, z)
        dst[(2 * r + 1) * Q_BLOCK:(2 * r + 2) * Q_BLOCK, :] = jnp.where(lo, z, qr)


def _pair_heads(o, r):
    lo = lax.broadcasted_iota(jnp.int32, (Q_BLOCK, LANES), 1) < HEAD_DIM
    return jnp.where(lo, o[(2 * r) * Q_BLOCK:(2 * r + 1) * Q_BLOCK], o[(2 * r + 1) * Q_BLOCK:(2 * r + 2) * Q_BLOCK])


def _lane_tiles(x):
    return [x[:, t * LANES:(t + 1) * LANES] for t in range(x.shape[1] // LANES)]


def _row_max(tiles):
    mx = tiles[0]
    for t in tiles[1:]:
        mx = jnp.maximum(mx, t)
    return jnp.broadcast_to(jnp.max(mx, axis=1, keepdims=True), mx.shape)


def _with_ones(v):
    return jnp.concatenate([v, jnp.ones(v.shape, v.dtype)], axis=1)


def _block_of_key(n_keys, first_block):
    b = lax.broadcasted_iota(jnp.int32, (LANES, n_keys), 0)
    k = lax.broadcasted_iota(jnp.int32, (LANES, n_keys), 1)
    return (b == (k // SEL_BLOCK) + first_block).astype(MXU_DTYPE)


def _select_blocks_t(imp_t, i, n_top):
    blk = lax.broadcasted_iota(jnp.int32, imp_t.shape, 0)
    qcol = lax.broadcasted_iota(jnp.int32, imp_t.shape, 1)
    back = (2 * i + (qcol >= SEL_BLOCK).astype(jnp.int32)) - blk
    sel = (back >= 0) & ((blk < SEL_INIT_BLOCKS) | (back < SEL_LOCAL_BLOCKS))
    cand = jnp.where((back >= SEL_LOCAL_BLOCKS) & (blk >= SEL_INIT_BLOCKS), imp_t, -1.0)
    blk_f = blk.astype(F32)
    for _ in range(n_top - SEL_INIT_BLOCKS - SEL_LOCAL_BLOCKS):
        m = jnp.max(cand, axis=0, keepdims=True)
        idx = jnp.min(jnp.where(cand == m, blk_f, float(LANES)), axis=0, keepdims=True)
        hit = blk_f == idx
        sel = sel | (hit & (m >= 0.0))
        cand = jnp.where(hit, -2.0, cand)
    return sel


def _attn_kernel(sink_ref, qa_ref, qb_ref, ga_ref, kcmp_ref, vcmp_ref, ks_ref, vs_ref, kw_ref, vw_ref, kb_ref,
                 vb_ref, cmat_ref, tnear_ref, tsel_ref, twin_ref, tswa_ref, oa_ref, ob_ref,
                 qall, mneg, mneg_far, m_s, acc_s, *, n_far, n_top):
    i = pl.program_id(1)
    rows = N_HEADS * Q_BLOCK
    _stack_heads(qa_ref, qall)
    q = qall[...]

    off = pl.multiple_of(i * (Q_BLOCK // CMP_STRIDE), 8)
    k_cmp = _mx(jnp.concatenate([kcmp_ref[0, 0, 0:n_far, :], kcmp_ref[0, 0, pl.ds(off, CMP_NEAR), :]], axis=0))
    v_cmp = _mx(jnp.concatenate([vcmp_ref[0, 0, 0:n_far, :], vcmp_ref[0, 0, pl.ds(off, CMP_NEAR), :]], axis=0))
    colf = lax.broadcasted_iota(jnp.int32, (1, n_far), 1)
    coln = lax.broadcasted_iota(jnp.int32, (1, CMP_NEAR), 1)
    col_ok = jnp.concatenate([(colf >= CMP_FRONT) & (colf < off), coln + off >= CMP_FRONT], axis=1)
    tiles = _lane_tiles(_dot_nt(q, k_cmp) + jnp.where(col_ok, 0.0, NEG))
    tiles[-1] = tiles[-1] + tnear_ref[...]
    m = _row_max(tiles)
    m = jnp.where(m > 0.5 * NEG, m, 0.0)
    e = [jnp.exp(t - m) for t in tiles]
    ov = _dot(_mx(jnp.concatenate(e, axis=1)), _with_ones(v_cmp))
    inv = 1.0 / jnp.maximum(ov[:, LANES:], 1e-30)
    o_c = ov[:, :LANES] * inv

    blkcol = lax.broadcasted_iota(jnp.int32, (Q_BLOCK, LANES), 1)
    for g in range(N_GROUPS):
        imp = jnp.zeros((Q_BLOCK, LANES), F32)
        for t, et in enumerate(e):
            pt = et * inv
            pg = sum(pt[(2 * r + g) * Q_BLOCK:(2 * r + g + 1) * Q_BLOCK] for r in range(GROUP))
            if t < len(e) - 1:
                cm = _mx(cmat_ref[t * LANES:(t + 1) * LANES, :])
            else:
                cm = _mx(cmat_ref[pl.ds(off, CMP_NEAR), :])
            hi = _mx(pg)
            lo = _mx(pg - hi.astype(F32))
            imp = imp + _dot(hi, cm) + _dot(lo, cm)
        sel = _select_blocks_t(imp.T, i, n_top)
        neg = jnp.where(sel, 0.0, NEG).T
        mneg[g * Q_BLOCK:(g + 1) * Q_BLOCK, :] = neg.astype(mneg.dtype)
        mneg_far[g * Q_BLOCK:(g + 1) * Q_BLOCK, :] = jnp.where(blkcol < 2 * (i - 1), neg, NEG).astype(mneg.dtype)

    m_s[...] = jnp.full(m_s.shape, NEG, F32)
    acc_s[...] = jnp.zeros(acc_s.shape, F32)

    half = rows // 2

    def flash_update(rs, s, v1):
        s_tiles = _lane_tiles(s)
        m_old = m_s[rs, :]
        m_new = jnp.maximum(m_old, _row_max(s_tiles))
        alpha = jnp.exp(m_old - m_new)
        p = jnp.concatenate([jnp.exp(t - m_new) for t in s_tiles], axis=1)
        acc_s[rs, :] = jnp.concatenate([alpha, alpha], axis=1) * acc_s[rs, :] + _dot(_mx(p), v1)
        m_s[rs, :] = m_new

    def far_start(j):
        return pl.multiple_of(Q_BLOCK + j * SEL_CHUNK, Q_BLOCK)

    def far_body(j, carry):
        kc = _mx(ks_ref[0, pl.ds(far_start(j), SEL_CHUNK), :])
        v1 = _with_ones(_mx(vs_ref[0, pl.ds(far_start(j), SEL_CHUNK), :]))
        madd = _dot(mneg_far[...], _block_of_key(SEL_CHUNK, j * (SEL_CHUNK // SEL_BLOCK)))
        for h in range(2):
            rs = slice(h * half, (h + 1) * half)
            s = _dot_nt(qall[rs, :], kc) + jnp.concatenate([madd] * (GROUP // 2), axis=0)
            flash_update(rs, s, v1)
        return carry

    n_far_keys = jnp.maximum(i - 1, 0) * Q_BLOCK
    lax.fori_loop(0, (n_far_keys + SEL_CHUNK - 1) // SEL_CHUNK, far_body, 0)
    nstart = pl.multiple_of(i * Q_BLOCK, Q_BLOCK)
    kc = _mx(ks_ref[0, pl.ds(nstart, 2 * Q_BLOCK), :])
    v1 = _with_ones(_mx(vs_ref[0, pl.ds(nstart, 2 * Q_BLOCK), :]))
    madd = _dot(mneg[...], _block_of_key(2 * Q_BLOCK, 2 * (i - 1)))
    col2 = lax.broadcasted_iota(jnp.int32, (1, 2 * Q_BLOCK), 1)
    extra = tsel_ref[...] + jnp.where((col2 < Q_BLOCK) & (i == 0), NEG, 0.0)
    for h in range(2):
        rs = slice(h * half, (h + 1) * half)
        s = _dot_nt(qall[rs, :], kc) + jnp.concatenate([madd] * (GROUP // 2), axis=0) + extra[rs, :]
        flash_update(rs, s, v1)
    acc = acc_s[...]
    o_s = acc[:, :LANES] / acc[:, LANES:]

    wpad = kw_ref.shape[1] - ks_ref.shape[1] + Q_BLOCK
    kwin = _mx(kw_ref[0, pl.ds(nstart, wpad + Q_BLOCK), :])
    vwin = _with_ones(_mx(vw_ref[0, pl.ds(nstart, wpad + Q_BLOCK), :]))
    colw = lax.broadcasted_iota(jnp.int32, (1, wpad + Q_BLOCK), 1)
    s = _dot_nt(q, kwin) + twin_ref[...] + jnp.where(colw + nstart >= wpad, 0.0, NEG)
    tiles = _lane_tiles(s)
    m = _row_max(tiles)
    ov = _dot(_mx(jnp.concatenate([jnp.exp(t - m) for t in tiles], axis=1)), vwin)
    o_w = ov[:, :LANES] / ov[:, LANES:]

    lo = lax.broadcasted_iota(jnp.int32, (Q_BLOCK, LANES), 1) < HEAD_DIM
    gates = ga_ref[...]
    for r in range(GROUP):
        tile = jnp.zeros((Q_BLOCK, LANES), F32)
        for c, o in enumerate((o_c, o_s, o_w)):
            g0 = gates[:, c * 8 + 2 * r:c * 8 + 2 * r + 1]
            g1 = gates[:, c * 8 + 2 * r + 1:c * 8 + 2 * r + 2]
            tile = tile + jnp.where(lo, g0, g1) * _pair_heads(o, r)
        oa_ref[:, r * LANES:(r + 1) * LANES] = tile.astype(oa_ref.dtype)

    _stack_heads(qb_ref, qall)
    q = qall[...]
    bpad = kb_ref.shape[1] - ks_ref.shape[1] + Q_BLOCK
    kwin = _mx(kb_ref[0, pl.ds(nstart, bpad + Q_BLOCK), :])
    vwin = _with_ones(_mx(vb_ref[0, pl.ds(nstart, bpad + Q_BLOCK), :]))
    colb = lax.broadcasted_iota(jnp.int32, (1, bpad + Q_BLOCK), 1)
    s = _dot_nt(q, kwin) + tswa_ref[...] + jnp.where(colb + nstart >= bpad, 0.0, NEG)
    sink = jnp.concatenate([jnp.full((Q_BLOCK, LANES), sink_ref[(h % 2) * GROUP + h // 2], F32)
                            for h in range(N_HEADS)], axis=0)
    tiles = _lane_tiles(s)
    m = jnp.maximum(_row_max(tiles), sink)
    ov = _dot(_mx(jnp.concatenate([jnp.exp(t - m) for t in tiles], axis=1)), vwin)
    o_b = ov[:, :LANES] / (ov[:, LANES:] + jnp.exp(sink - m))
    for r in range(GROUP):
        ob_ref[:, r * LANES:(r + 1) * LANES] = _pair_heads(o_b, r).astype(ob_ref.dtype)


def _rel_bucket_np(dist):
    n = np.maximum(dist, 0)
    max_exact = REL_BUCKETS // 2
    nf = np.maximum(n, 1).astype(np.float32)
    log_b = max_exact + (np.log(nf / max_exact) / math.log(REL_MAX_DIST / max_exact)
                         * (REL_BUCKETS - max_exact)).astype(np.int32)
    log_b = np.minimum(log_b, REL_BUCKETS - 1)
    return np.where(n < max_exact, n, log_b)


def _toeplitz_bias(tab, pad, width, window, shift_far):
    length = width + Q_BLOCK
    dist = pad + Q_BLOCK - 1 - np.arange(length)
    onehot = np.zeros((length, REL_BUCKETS), np.float32)
    onehot[np.arange(length), _rel_bucket_np(dist)] = 1.0
    vals = jnp.dot(jnp.asarray(onehot), tab, precision=lax.Precision.HIGHEST)
    if shift_far:
        vals = vals - tab[REL_BUCKETS - 1][None, :]
    valid = (dist >= 0) & (dist < window)
    vals = jnp.where(jnp.asarray(valid)[:, None], vals, NEG).T
    skew = jnp.tile(vals, (1, Q_BLOCK))[:, :Q_BLOCK * (length - 1)].reshape(N_HEADS, Q_BLOCK, length - 1)
    return skew[:, :, Q_BLOCK - 1:Q_BLOCK - 1 + width].reshape(N_HEADS * Q_BLOCK, width).astype(F32)


def _attention(proj, kvcmp, sinks, bias_table, bsz, seq):
    assert seq % SEL_CHUNK == 0
    nq = seq // Q_BLOCK
    n_far = seq // CMP_STRIDE
    n_sel = seq // SEL_BLOCK
    n_top = min(SEL_TOP_N, n_sel)
    assert n_top >= SEL_INIT_BLOCKS + SEL_LOCAL_BLOCKS and n_sel <= LANES
    wpad = Q_BLOCK * (-(-(NSA_WINDOW - 1) // Q_BLOCK))
    bpad = Q_BLOCK * (-(-(SWA_WINDOW - 1) // Q_BLOCK))
    pair = lambda tab: tab.astype(F32).reshape(REL_BUCKETS, N_GROUPS, GROUP).transpose(0, 2, 1).reshape(REL_BUCKETS, -1)
    tab_a = pair(bias_table[:, :N_HEADS])
    tab_b = pair(bias_table[:, N_HEADS:])
    near_pad = CMP_STRIDE * CMP_FRONT - (CMP_BLOCK - 1)
    t_near = _toeplitz_bias(tab_a, near_pad, CMP_STRIDE * CMP_NEAR, 1 << 30, True)[:, ::CMP_STRIDE]
    t_sel = _toeplitz_bias(tab_a, Q_BLOCK, 2 * Q_BLOCK, 1 << 30, True)
    t_win = _toeplitz_bias(tab_a, wpad, wpad + Q_BLOCK, NSA_WINDOW, False)
    t_swa = _toeplitz_bias(tab_b, bpad, bpad + Q_BLOCK, SWA_WINDOW, False)
    n_rows = kvcmp.shape[2]
    cn = (np.arange(n_rows) - CMP_FRONT)[:, None] * CMP_STRIDE
    sj = np.arange(LANES)[None, :] * SEL_BLOCK
    cmat = ((cn < sj + SEL_BLOCK) & (cn + CMP_BLOCK > sj) & (cn >= 0) & (cn + CMP_BLOCK <= seq)
            & (sj < seq)).astype(np.float32)
    cmat = jnp.asarray(cmat, F32)
    padded = lambda name, p: jnp.pad(proj[name].reshape(bsz, seq, LANES), ((0, 0), (p, 0), (0, 0)))
    ks, vs = padded('ks', Q_BLOCK), padded('vs', Q_BLOCK)
    kw, vw = padded('kw', wpad), padded('vw', wpad)
    kb, vb = padded('kb', bpad), padded('vb', bpad)
    rows = N_HEADS * Q_BLOCK
    qspec = pl.BlockSpec((Q_BLOCK, 4 * LANES), lambda b, i: (b * nq + i, 0))
    const2 = lambda shape: pl.BlockSpec(shape, lambda b, i: (0, 0))
    batch3 = lambda n: pl.BlockSpec((1, n, LANES), lambda b, i: (b, 0, 0))
    kernel = functools.partial(_attn_kernel, n_far=n_far, n_top=n_top)
    return pl.pallas_call(
        kernel,
        grid=(bsz, nq),
        in_specs=[pl.BlockSpec(memory_space=pltpu.SMEM),
                  qspec, qspec,
                  pl.BlockSpec((Q_BLOCK, LANES), lambda b, i: (b * nq + i, 0)),
                  pl.BlockSpec((1, 1, n_rows, LANES), lambda b, i: (0, b, 0, 0)),
                  pl.BlockSpec((1, 1, n_rows, LANES), lambda b, i: (1, b, 0, 0)),
                  batch3(seq + Q_BLOCK), batch3(seq + Q_BLOCK),
                  batch3(seq + wpad), batch3(seq + wpad),
                  batch3(seq + bpad), batch3(seq + bpad),
                  const2((n_rows, LANES)),
                  const2((rows, CMP_NEAR)),
                  const2((rows, 2 * Q_BLOCK)),
                  const2((rows, wpad + Q_BLOCK)),
                  const2((rows, bpad + Q_BLOCK))],
        out_specs=[qspec, qspec],
        out_shape=[jax.ShapeDtypeStruct((bsz * seq, 4 * LANES), BF16)] * 2,
        scratch_shapes=[pltpu.VMEM((rows, LANES), MXU_DTYPE),
                        pltpu.VMEM((N_GROUPS * Q_BLOCK, LANES), MXU_DTYPE),
                        pltpu.VMEM((N_GROUPS * Q_BLOCK, LANES), MXU_DTYPE),
                        pltpu.VMEM((rows, LANES), F32),
                        pltpu.VMEM((rows, 2 * LANES), F32)],
        compiler_params=pltpu.CompilerParams(dimension_semantics=("arbitrary", "arbitrary"),
                                             vmem_limit_bytes=VMEM_LIMIT),
        name="attention",
    )(sinks.astype(F32), proj['qa'], proj['qb'], proj['ga'], kvcmp, kvcmp, ks, vs, kw, vw, kb, vb,
      cmat, t_near, t_sel, t_win, t_swa)


def _layer_norm(y, g, b):
    mu = jnp.mean(y, axis=-1, keepdims=True)
    yc = y - mu
    var = jnp.mean(yc * yc, axis=-1, keepdims=True)
    return yc * lax.rsqrt(var + LN_EPS) * g + b


def _outproj_kernel(oa_ref, ob_ref, sg_ref, x_ref, pa_ref, pb_ref, wo_ref, g1_ref, b1_ref, wr_ref, rb_ref, sgu_ref,
                    sd_ref, tri_ref, h_ref, base_ref, eidx_ref, gate_ref, rank_ref, cnt_ref, carry):
    step = pl.program_id(0)
    tm = oa_ref.shape[0]

    @pl.when(step == 0)
    def _():
        carry[...] = jnp.zeros(carry.shape, F32)

    sg = sg_ref[...].astype(F32)
    merged = (sg[:, :D_MODEL] * _dot(_mx(oa_ref[...]), pa_ref[...])
              + sg[:, D_MODEL:] * _dot(_mx(ob_ref[...]), pb_ref[...]))
    mix = _dot(_mx(merged), wo_ref[...])
    h = _layer_norm(DN_ALPHA * x_ref[...] + mix, g1_ref[...], b1_ref[...])
    h_ref[...] = h
    hb = _mx(h)

    gu = _dot(hb, sgu_ref[...])
    shared = _dot(_mx(jax.nn.silu(gu[:, :SHARED_HIDDEN]) * gu[:, SHARED_HIDDEN:]), sd_ref[...])
    base_ref[...] = DN_ALPHA * h + shared

    scores = jax.nn.sigmoid(_dot_nt(wr_ref[...], hb))
    choice = scores + rb_ref[:, 0:1]
    per_group = N_EXPERTS // N_EXPERT_GROUPS
    gs = []
    for g in range(N_EXPERT_GROUPS):
        cg = choice[g * per_group:(g + 1) * per_group]
        m1 = jnp.max(cg, axis=0, keepdims=True)
        is_m = cg == m1
        n_m = jnp.sum(is_m.astype(F32), axis=0, keepdims=True)
        m2 = jnp.max(jnp.where(is_m, -jnp.inf, cg), axis=0, keepdims=True)
        gs.append(m1 + jnp.where(n_m > 1.5, m1, m2))
    gs = jnp.concatenate(gs, axis=0)
    gid = lax.broadcasted_iota(jnp.int32, gs.shape, 0)
    beaten = jnp.zeros(gs.shape, jnp.int32)
    for g in range(N_EXPERT_GROUPS):
        other = gs[g:g + 1]
        beaten = beaten + ((other > gs) | ((other == gs) & (g < gid))).astype(jnp.int32)
    keep_g = beaten < TOPK_EXPERT_GROUPS
    keep = jnp.concatenate([jnp.broadcast_to(keep_g[g:g + 1], (per_group, tm)) for g in range(N_EXPERT_GROUPS)],
                           axis=0)
    cand = jnp.where(keep, choice, -jnp.inf)
    eid = lax.broadcasted_iota(jnp.int32, cand.shape, 0)
    hits = []
    e_rows = []
    w_rows = []
    for _ in range(TOP_K):
        m = jnp.max(cand, axis=0, keepdims=True)
        idx = jnp.min(jnp.where(cand == m, eid, N_EXPERTS), axis=0, keepdims=True)
        hit = eid == idx
        hits.append(hit)
        e_rows.append(idx)
        w_rows.append(jnp.sum(jnp.where(hit, scores, 0.0), axis=0, keepdims=True))
        cand = jnp.where(hit, -jnp.inf, cand)
    w = jnp.concatenate(w_rows, axis=0)
    gate_ref[...] = w / jnp.sum(w, axis=0, keepdims=True) * ROUTED_SCALE
    eidx_ref[...] = jnp.concatenate(e_rows, axis=0)

    onehot = jnp.zeros(cand.shape, F32)
    for hit in hits:
        onehot = onehot + hit.astype(F32)
    before = _dot(onehot.astype(BF16), tri_ref[...]) + carry[:, 0:1]
    rank_ref[...] = jnp.concatenate(
        [jnp.sum(jnp.where(hit, before, 0.0), axis=0, keepdims=True) for hit in hits], axis=0).astype(jnp.int32)
    carry[...] = carry[...] + jnp.sum(onehot, axis=1, keepdims=True)
    cnt_ref[...] = carry[...]


def _out_projection(oa, ob, sg, x2, proj_a, proj_b, w_out, ln_g, ln_b, w_router, router_bias, s_gate, s_up, s_down):
    t = x2.shape[0]
    tm = OUT_TM
    pair_rows = lambda p: p.reshape(N_GROUPS, GROUP, HEAD_DIM, -1).transpose(1, 0, 2, 3).reshape(p.shape)
    pa = pair_rows(proj_a).astype(MXU_DTYPE)
    pb = pair_rows(proj_b).astype(MXU_DTYPE)
    tri = jnp.asarray(np.triu(np.ones((tm, tm), np.float32), 1), BF16)
    row = lambda i: (i, 0)
    fixed = lambda i: (0, 0)
    col = lambda i: (0, i)
    outs = pl.pallas_call(
        _outproj_kernel,
        grid=(t // tm,),
        in_specs=[pl.BlockSpec((tm, 4 * LANES), row), pl.BlockSpec((tm, 4 * LANES), row),
                  pl.BlockSpec((tm, 2 * D_MODEL), row), pl.BlockSpec((tm, D_MODEL), row),
                  pl.BlockSpec((4 * LANES, D_MODEL), fixed), pl.BlockSpec((4 * LANES, D_MODEL), fixed),
                  pl.BlockSpec((D_MODEL, D_MODEL), fixed),
                  pl.BlockSpec((1, D_MODEL), fixed), pl.BlockSpec((1, D_MODEL), fixed),
                  pl.BlockSpec((N_EXPERTS, D_MODEL), fixed), pl.BlockSpec((N_EXPERTS, LANES), fixed),
                  pl.BlockSpec((D_MODEL, 2 * SHARED_HIDDEN), fixed), pl.BlockSpec((SHARED_HIDDEN, D_MODEL), fixed),
                  pl.BlockSpec((tm, tm), fixed)],
        out_specs=[pl.BlockSpec((tm, D_MODEL), row), pl.BlockSpec((tm, D_MODEL), row),
                   pl.BlockSpec((TOP_K, tm), col), pl.BlockSpec((TOP_K, tm), col), pl.BlockSpec((TOP_K, tm), col),
                   pl.BlockSpec((N_EXPERTS, LANES), fixed)],
        out_shape=[jax.ShapeDtypeStruct((t, D_MODEL), F32), jax.ShapeDtypeStruct((t, D_MODEL), F32),
                   jax.ShapeDtypeStruct((TOP_K, t), jnp.int32), jax.ShapeDtypeStruct((TOP_K, t), F32),
                   jax.ShapeDtypeStruct((TOP_K, t), jnp.int32), jax.ShapeDtypeStruct((N_EXPERTS, LANES), F32)],
        scratch_shapes=[pltpu.VMEM((N_EXPERTS, LANES), F32)],
        compiler_params=pltpu.CompilerParams(dimension_semantics=("arbitrary",), vmem_limit_bytes=VMEM_LIMIT),
        name="out_projection_router",
    )(oa, ob, sg, x2, pa, pb, w_out.astype(MXU_DTYPE), ln_g.reshape(1, -1), ln_b.reshape(1, -1),
      w_router.T.astype(MXU_DTYPE), jnp.broadcast_to(router_bias.astype(F32)[:, None], (N_EXPERTS, LANES)),
      jnp.concatenate([s_gate, s_up], axis=1).astype(MXU_DTYPE), s_down.astype(MXU_DTYPE), tri)
    return outs


def _row_copy_wait(src, dst, sem, n):
    def body(_, c):
        pltpu.make_async_copy(src, dst, sem).wait()
        return c
    lax.fori_loop(0, n, body, 0)


def _dispatch_kernel(zstart_ref, cnt_ref, dest_ref, h_ref, xs_ref, zeros, sem):
    step = pl.program_id(0)
    tm = h_ref.shape[0]

    @pl.when(step == 0)
    def _():
        zeros[...] = jnp.zeros(zeros.shape, F32)

        def fill(e, c):
            @pl.when(cnt_ref[e] > 0)
            def _():
                cp = pltpu.make_async_copy(zeros, xs_ref.at[pl.ds(zstart_ref[e], MOE_BM)], sem)
                cp.start()
                cp.wait()
            return c
        lax.fori_loop(0, N_EXPERTS, fill, 0)

    def issue(t, c):
        for k in range(TOP_K):
            pltpu.make_async_copy(h_ref.at[t], xs_ref.at[dest_ref[k, t]], sem).start()
        return c
    lax.fori_loop(0, tm, issue, 0)
    _row_copy_wait(h_ref.at[0], xs_ref.at[0], sem, tm * TOP_K)


def _dispatch(h3, dest, zstart, counts, n_rows):
    t = h3.shape[0]
    tm = DISP_TM
    return pl.pallas_call(
        _dispatch_kernel,
        grid_spec=pltpu.PrefetchScalarGridSpec(
            num_scalar_prefetch=2,
            grid=(t // tm,),
            in_specs=[pl.BlockSpec((TOP_K, tm), lambda i, *_: (0, i), memory_space=pltpu.SMEM),
                      pl.BlockSpec((tm,) + ROW_TILE, lambda i, *_: (i, 0, 0))],
            out_specs=pl.BlockSpec(memory_space=pl.ANY),
            scratch_shapes=[pltpu.VMEM((MOE_BM,) + ROW_TILE, F32), pltpu.SemaphoreType.DMA(())]),
        out_shape=jax.ShapeDtypeStruct((n_rows,) + ROW_TILE, F32),
        compiler_params=pltpu.CompilerParams(dimension_semantics=("arbitrary",), vmem_limit_bytes=VMEM_LIMIT),
        name="moe_dispatch",
    )(zstart, counts, dest, h3)


def _experts_kernel(blk_e_ref, nused_ref, xs_ref, wg_ref, wu_ref, wd_ref, ys_ref, wg_s, wu_s, wd_s):
    b = pl.program_id(0)
    prev = blk_e_ref[jnp.maximum(b - 1, 0)]

    @pl.when((b == 0) | (blk_e_ref[b] != prev))
    def _():
        wg_s[...] = _mx(wg_ref[0])
        wu_s[...] = _mx(wu_ref[0])
        wd_s[...] = _mx(wd_ref[0])

    @pl.when(b < nused_ref[0])
    def _():
        xb = _mx(xs_ref[...])
        hid = jax.nn.silu(_dot(xb, wg_s[...])) * _dot(xb, wu_s[...])
        ys_ref[...] = _dot(_mx(hid), wd_s[...])

    @pl.when(b >= nused_ref[0])
    def _():
        ys_ref[...] = jnp.zeros(ys_ref.shape, F32)


def _experts(xs, blk_e, nused, e_gate, e_up, e_down):
    n_rows = xs.shape[0]
    n_blocks = n_rows // MOE_BM
    xmap = lambda b, be, nu: (jnp.minimum(b, nu[0] - 1), 0)
    wmap = lambda b, be, nu: (be[b], 0, 0)
    return pl.pallas_call(
        _experts_kernel,
        grid_spec=pltpu.PrefetchScalarGridSpec(
            num_scalar_prefetch=2,
            grid=(n_blocks,),
            in_specs=[pl.BlockSpec((MOE_BM, D_MODEL), xmap),
                      pl.BlockSpec((1, D_MODEL, EXPERT_HIDDEN), wmap),
                      pl.BlockSpec((1, D_MODEL, EXPERT_HIDDEN), wmap),
                      pl.BlockSpec((1, EXPERT_HIDDEN, D_MODEL), wmap)],
            out_specs=pl.BlockSpec((MOE_BM, D_MODEL), lambda b, be, nu: (b, 0)),
            scratch_shapes=[pltpu.VMEM((D_MODEL, EXPERT_HIDDEN), MXU_DTYPE),
                            pltpu.VMEM((D_MODEL, EXPERT_HIDDEN), MXU_DTYPE),
                            pltpu.VMEM((EXPERT_HIDDEN, D_MODEL), MXU_DTYPE)]),
        out_shape=jax.ShapeDtypeStruct((n_rows, D_MODEL), F32),
        compiler_params=pltpu.CompilerParams(dimension_semantics=("arbitrary",), vmem_limit_bytes=VMEM_LIMIT),
        name="moe_experts",
    )(blk_e, nused, xs, e_gate, e_up, e_down)


def _combine_kernel(dest_ref, gate_ref, base_ref, g2_ref, b2_ref, ys_ref, out_ref, buf, sem):
    tm = base_ref.shape[0]

    def issue(t, c):
        for k in range(TOP_K):
            pltpu.make_async_copy(ys_ref.at[dest_ref[k, t]], buf.at[k, t], sem).start()
        return c
    lax.fori_loop(0, tm, issue, 0)
    _row_copy_wait(ys_ref.at[0], buf.at[0, 0], sem, tm * TOP_K)

    def token(t, c):
        y = base_ref[t]
        for k in range(TOP_K):
            y = y + gate_ref[k, t] * buf[k, t]
        out_ref[t] = y
        return c
    lax.fori_loop(0, tm, token, 0)
    y = out_ref[...]
    n = float(D_MODEL)
    mu = jnp.sum(jnp.sum(y, axis=2, keepdims=True), axis=1, keepdims=True) / n
    yc = y - mu
    var = jnp.sum(jnp.sum(yc * yc, axis=2, keepdims=True), axis=1, keepdims=True) / n
    out_ref[...] = yc * lax.rsqrt(var + LN_EPS) * g2_ref[...] + b2_ref[...]


def _combine(ys3, dest, gate, base3, ln_g, ln_b):
    t = base3.shape[0]
    tm = COMB_TM
    return pl.pallas_call(
        _combine_kernel,
        grid=(t // tm,),
        in_specs=[pl.BlockSpec((TOP_K, tm), lambda i: (0, i), memory_space=pltpu.SMEM),
                  pl.BlockSpec((TOP_K, tm), lambda i: (0, i), memory_space=pltpu.SMEM),
                  pl.BlockSpec((tm,) + ROW_TILE, lambda i: (i, 0, 0)),
                  pl.BlockSpec(ROW_TILE, lambda i: (0, 0)),
                  pl.BlockSpec(ROW_TILE, lambda i: (0, 0)),
                  pl.BlockSpec(memory_space=pl.ANY)],
        out_specs=pl.BlockSpec((tm,) + ROW_TILE, lambda i: (i, 0, 0)),
        out_shape=jax.ShapeDtypeStruct((t,) + ROW_TILE, F32),
        scratch_shapes=[pltpu.VMEM((TOP_K, tm) + ROW_TILE, F32), pltpu.SemaphoreType.DMA(())],
        compiler_params=pltpu.CompilerParams(dimension_semantics=("arbitrary",), vmem_limit_bytes=VMEM_LIMIT),
        name="moe_combine",
    )(dest, gate, base3, ln_g.reshape(ROW_TILE), ln_b.reshape(ROW_TILE), ys3)


def _moe_layout(eidx, rank, counts):
    n_assign = eidx.size
    n_blocks = (n_assign + N_EXPERTS * (MOE_BM - 1)) // MOE_BM
    padded = (counts + MOE_BM - 1) // MOE_BM * MOE_BM
    pends = jnp.cumsum(padded)
    pstarts = pends - padded
    experts = jnp.arange(N_EXPERTS, dtype=jnp.int32)
    dest = jnp.sum(jnp.where(eidx[..., None] == experts, pstarts, 0), axis=-1) + rank
    block_row = jnp.arange(n_blocks, dtype=jnp.int32) * MOE_BM
    blk_e = jnp.minimum(jnp.sum(pends[None, :] <= block_row[:, None], axis=1), N_EXPERTS - 1).astype(jnp.int32)
    nused = (pends[-1:] // MOE_BM).astype(jnp.int32)
    zstart = jnp.maximum(pends - MOE_BM, 0).astype(jnp.int32)
    return dest.astype(jnp.int32), blk_e, nused, zstart, n_blocks * MOE_BM


def _layer(x, w_in, cmp_pe, cmp_w1, cmp_b1, cmp_w2, sinks, bias_table, proj_a, proj_b, w_out, ln1_g, ln1_b,
           w_router, router_bias, e_gate, e_up, e_down, s_gate, s_up, s_down, ln2_g, ln2_b):
    bsz, seq, d = x.shape
    x2 = x.reshape(bsz * seq, d)
    proj = _in_projection(x2, w_in)
    kvcmp = _compress(proj['kc'], proj['vc'], bsz, seq, cmp_pe, cmp_w1, cmp_b1, cmp_w2)
    oa, ob = _attention(proj, kvcmp, sinks, bias_table, bsz, seq)
    h, base, eidx, gate, rank, cnt = _out_projection(oa, ob, proj['sg'], x2, proj_a, proj_b, w_out, ln1_g, ln1_b,
                                                     w_router, router_bias, s_gate, s_up, s_down)
    counts = cnt[:, 0].astype(jnp.int32)
    dest, blk_e, nused, zstart, n_rows = _moe_layout(eidx, rank, counts)
    xs = _dispatch(h.reshape((-1,) + ROW_TILE), dest, zstart, counts, n_rows)
    ys = _experts(xs.reshape(n_rows, d), blk_e, nused, e_gate, e_up, e_down)
    out = _combine(ys.reshape((n_rows,) + ROW_TILE), dest, gate, base.reshape((-1,) + ROW_TILE), ln2_g, ln2_b)
    return out.reshape(bsz, seq, d)


def kernel(x, w_in, cmp_pe, cmp_w1, cmp_b1, cmp_w2, attn_sinks, rel_bias_table, proj_a, proj_b, w_out, ln1_g, ln1_b,
           w_router, router_bias, expert_w_gate, expert_w_up, expert_w_down, shared_w_gate, shared_w_up,
           shared_w_down, ln2_g, ln2_b):
    h = x
    for l in range(DEPTH):
        h = _layer(h, w_in[l], cmp_pe[l], cmp_w1[l], cmp_b1[l], cmp_w2[l], attn_sinks[l], rel_bias_table, proj_a[l],
                   proj_b[l], w_out[l], ln1_g[l], ln1_b[l], w_router[l], router_bias[l], expert_w_gate[l],
                   expert_w_up[l], expert_w_down[l], shared_w_gate[l], shared_w_up[l], shared_w_down[l], ln2_g[l],
                   ln2_b[l])
    return h
```

```python
import functools
import math

import numpy as np
import jax
import jax.numpy as jnp
from jax import lax
from jax.experimental import pallas as pl
from jax.experimental.pallas import tpu as pltpu

F32 = jnp.float32
BF16 = jnp.bfloat16
MXU_DTYPE = jnp.bfloat16

D_MODEL = 1024
HEAD_DIM = 64
ATTN_SCALE = HEAD_DIM ** -0.5
Q_BLOCK = 128
N_HEADS = 8
N_GROUPS = 2
GROUP = 4
CMP_BLOCK = 32
CMP_STRIDE = 16
CMP_HIDDEN = 128
SEL_BLOCK = 64
SEL_TOP_N = 8
SEL_INIT_BLOCKS = 1
SEL_LOCAL_BLOCKS = 2
NSA_WINDOW = 512
SWA_WINDOW = 128
REL_BUCKETS = 32
REL_MAX_DIST = 128
N_EXPERTS = 256
TOP_K = 8
EXPERT_HIDDEN = 256
SHARED_HIDDEN = 256
N_EXPERT_GROUPS = 8
TOPK_EXPERT_GROUPS = 4
ROUTED_SCALE = 2.5
LN_EPS = 1e-5
DEPTH = 1
DN_ALPHA = (2 * DEPTH) ** 0.25

NEG = -1e30
LANES = 128
ROW_TILE = (8, LANES)
CMP_FRONT = 16
CMP_NEAR = LANES
SEL_CHUNK = 512
VMEM_LIMIT = 56 * 1024 * 1024

IN_TM = 512
OUT_TM = 256
MOE_BM = 256
DISP_TM = 256
COMB_TM = 128


def _dot(a, b):
    return jnp.dot(a, b, preferred_element_type=F32)


def _dot_nt(a, b):
    return lax.dot_general(a, b, (((1,), (1,)), ((), ())), preferred_element_type=F32)


def _mx(a):
    return a.astype(MXU_DTYPE)


_IN_COLS = (('qa', 512), ('qb', 512), ('kc', 128), ('vc', 128), ('ks', 128), ('vs', 128), ('kw', 128),
            ('vw', 128), ('kb', 128), ('vb', 128), ('ga', 128), ('sg', 2048))


def _inproj_kernel(x_ref, w_ref, qa_ref, qb_ref, kc_ref, vc_ref, ks_ref, vs_ref, kw_ref, vw_ref, kb_ref, vb_ref,
                   ga_ref, sg_ref):
    xb = _mx(x_ref[...])
    outs = dict(qa=qa_ref, qb=qb_ref, kc=kc_ref, vc=vc_ref, ks=ks_ref, vs=vs_ref, kw=kw_ref, vw=vw_ref,
                kb=kb_ref, vb=vb_ref, ga=ga_ref, sg=sg_ref)
    off = 0
    for name, width in _IN_COLS:
        for c0 in range(0, width, 512):
            cw = min(512, width - c0)
            y = _dot(xb, w_ref[:, off + c0:off + c0 + cw])
            if name in ('ga', 'sg'):
                y = jax.nn.sigmoid(y)
            outs[name][:, c0:c0 + cw] = y.astype(outs[name].dtype)
        off += width


def _pair_head_columns(w):
    return w.reshape(w.shape[0], N_GROUPS, GROUP, HEAD_DIM).transpose(0, 2, 1, 3).reshape(w.shape[0], -1)


def _in_projection(x2, w_in):
    t = x2.shape[0]
    sizes = (512, 128, 128, 128, 128, 128, 128, 24, 512, 128, 128, 1024, 1024)
    offs = np.cumsum((0,) + sizes)
    part = [w_in[:, offs[k]:offs[k + 1]] for k in range(len(sizes))]
    w_qa, w_kc, w_vc, w_ks, w_vs, w_kw, w_vw, w_g, w_qb, w_kb, w_vb, w_gate_a, w_gate_b = part
    w_qa = _pair_head_columns(w_qa) * ATTN_SCALE
    w_qb = _pair_head_columns(w_qb) * ATTN_SCALE
    w_ga = w_g.reshape(-1, N_GROUPS, GROUP, 3).transpose(0, 3, 2, 1).reshape(-1, 24)
    w_ga = jnp.pad(w_ga, ((0, 0), (0, LANES - 24)))
    w_all = jnp.concatenate([w_qa, w_qb, w_kc, w_vc, w_ks, w_vs, w_kw, w_vw, w_kb, w_vb, w_ga, w_gate_a, w_gate_b],
                            axis=1).astype(MXU_DTYPE)
    n_all = w_all.shape[1]
    out_shape = []
    out_specs = []
    for name, width in _IN_COLS:
        dt = F32 if name == 'ga' else BF16
        out_shape.append(jax.ShapeDtypeStruct((t, width), dt))
        out_specs.append(pl.BlockSpec((IN_TM, width), lambda i: (i, 0)))
    outs = pl.pallas_call(
        _inproj_kernel,
        grid=(t // IN_TM,),
        in_specs=[pl.BlockSpec((IN_TM, D_MODEL), lambda i: (i, 0)),
                  pl.BlockSpec((D_MODEL, n_all), lambda i: (0, 0))],
        out_specs=out_specs,
        out_shape=out_shape,
        compiler_params=pltpu.CompilerParams(dimension_semantics=("arbitrary",), vmem_limit_bytes=VMEM_LIMIT),
        name="in_projection",
    )(x2, w_all)
    return dict(zip([n for n, _ in _IN_COLS], outs))


def _compress_kernel(tok_ref, w1_ref, pe_ref, w1o_ref, b1_ref, w2_ref, out_ref):
    n_chunks = tok_ref.shape[2]
    ab = _dot(tok_ref[0, 0], w1_ref[0])
    a = ab[:, :2 * CMP_HIDDEN]
    b_next = pltpu.roll(ab[:, 2 * CMP_HIDDEN:], n_chunks - 1, 0)
    cb = _dot(_mx(pe_ref[0]), _mx(w1o_ref[0]))[0:1, :] + b1_ref[0]
    cb2 = jnp.concatenate([cb, cb], axis=1)
    hid = jax.nn.gelu(a + b_next + cb2)
    out = _dot(_mx(hid), w2_ref[0])
    row = lax.broadcasted_iota(jnp.int32, out.shape, 0)
    out = jnp.where(row < n_chunks - 1, out, 0.0)
    out_ref[0, 0, 0:CMP_FRONT, :] = jnp.zeros((CMP_FRONT, LANES), F32)
    out_ref[0, 0, CMP_FRONT:CMP_FRONT + n_chunks, :] = out
    out_ref[0, 0, CMP_FRONT + n_chunks:, :] = jnp.zeros((CMP_NEAR - CMP_FRONT, LANES), F32)


def _compress(kc, vc, bsz, seq, cmp_pe, cmp_w1, cmp_b1, cmp_w2):
    n_chunks = seq // CMP_STRIDE
    tok = jnp.stack([kc, vc]).reshape(2, bsz, n_chunks, CMP_STRIDE * LANES)
    eye = jnp.eye(N_GROUPS, dtype=F32)
    w1r = cmp_w1.reshape(2, 2, CMP_STRIDE, HEAD_DIM, CMP_HIDDEN)
    w1 = jnp.einsum('khjdn,gG->kjgdhGn', w1r, eye).reshape(2, CMP_STRIDE * LANES, 4 * CMP_HIDDEN).astype(MXU_DTYPE)
    w2 = jnp.einsum('knd,gG->kgnGd', cmp_w2, eye).reshape(2, 2 * CMP_HIDDEN, LANES).astype(MXU_DTYPE)
    pe = jnp.pad(cmp_pe.reshape(2, 1, CMP_BLOCK * HEAD_DIM), ((0, 0), (0, 7), (0, 0)))
    b1 = cmp_b1.reshape(2, 1, CMP_HIDDEN)
    rows = CMP_FRONT + n_chunks + CMP_NEAR - CMP_FRONT
    return pl.pallas_call(
        _compress_kernel,
        grid=(2, bsz),
        in_specs=[pl.BlockSpec((1, 1, n_chunks, CMP_STRIDE * LANES), lambda k, b: (k, b, 0, 0)),
                  pl.BlockSpec((1, CMP_STRIDE * LANES, 4 * CMP_HIDDEN), lambda k, b: (k, 0, 0)),
                  pl.BlockSpec((1, 8, CMP_BLOCK * HEAD_DIM), lambda k, b: (k, 0, 0)),
                  pl.BlockSpec((1, CMP_BLOCK * HEAD_DIM, CMP_HIDDEN), lambda k, b: (k, 0, 0)),
                  pl.BlockSpec((1, 1, CMP_HIDDEN), lambda k, b: (k, 0, 0)),
                  pl.BlockSpec((1, 2 * CMP_HIDDEN, LANES), lambda k, b: (k, 0, 0))],
        out_specs=pl.BlockSpec((1, 1, rows, LANES), lambda k, b: (k, b, 0, 0)),
        out_shape=jax.ShapeDtypeStruct((2, bsz, rows, LANES), F32),
        compiler_params=pltpu.CompilerParams(dimension_semantics=("arbitrary", "arbitrary"),
                                             vmem_limit_bytes=VMEM_LIMIT),
        name="nsa_compress",
    )(tok, w1, pe, cmp_w1, b1, w2)


def _stack_heads(q_ref, dst):
    lo = lax.broadcasted_iota(jnp.int32, (Q_BLOCK, LANES), 1) < HEAD_DIM
    for r in range(GROUP):
        qr = q_ref[:, r * LANES:(r + 1) * LANES].astype(dst.dtype)
        z = jnp.zeros_like(qr)
        dst[(2 * r) * Q_BLOCK:(2 * r + 1) * Q_BLOCK, :] = jnp.where(lo, qr, z)
        dst[(2 * r + 1) * Q_BLOCK:(2 * r + 2) * Q_BLOCK, :] = jnp.where(lo, z, qr)


def _pair_heads(o, r):
    lo = lax.broadcasted_iota(jnp.int32, (Q_BLOCK, LANES), 1) < HEAD_DIM
    return jnp.where(lo, o[(2 * r) * Q_BLOCK:(2 * r + 1) * Q_BLOCK], o[(2 * r + 1) * Q_BLOCK:(2 * r + 2) * Q_BLOCK])


def _lane_tiles(x):
    return [x[:, t * LANES:(t + 1) * LANES] for t in range(x.shape[1] // LANES)]


def _row_max(tiles):
    mx = tiles[0]
    for t in tiles[1:]:
        mx = jnp.maximum(mx, t)
    return jnp.broadcast_to(jnp.max(mx, axis=1, keepdims=True), mx.shape)


def _with_ones(v):
    return jnp.concatenate([v, jnp.ones(v.shape, v.dtype)], axis=1)


def _block_of_key(n_keys, first_block):
    b = lax.broadcasted_iota(jnp.int32, (LANES, n_keys), 0)
    k = lax.broadcasted_iota(jnp.int32, (LANES, n_keys), 1)
    return (b == (k // SEL_BLOCK) + first_block).astype(MXU_DTYPE)


def _select_blocks_t(imp_t, i, n_top):
    blk = lax.broadcasted_iota(jnp.int32, imp_t.shape, 0)
    qcol = lax.broadcasted_iota(jnp.int32, imp_t.shape, 1)
    back = (2 * i + (qcol >= SEL_BLOCK).astype(jnp.int32)) - blk
    sel = (back >= 0) & ((blk < SEL_INIT_BLOCKS) | (back < SEL_LOCAL_BLOCKS))
    cand = jnp.where((back >= SEL_LOCAL_BLOCKS) & (blk >= SEL_INIT_BLOCKS), imp_t, -1.0)
    blk_f = blk.astype(F32)
    for _ in range(n_top - SEL_INIT_BLOCKS - SEL_LOCAL_BLOCKS):
        m = jnp.max(cand, axis=0, keepdims=True)
        idx = jnp.min(jnp.where(cand == m, blk_f, float(LANES)), axis=0, keepdims=True)
        hit = blk_f == idx
        sel = sel | (hit & (m >= 0.0))
        cand = jnp.where(hit, -2.0, cand)
    return sel


def _attn_kernel(sink_ref, qa_ref, qb_ref, ga_ref, kcmp_ref, vcmp_ref, ks_ref, vs_ref, kw_ref, vw_ref, kb_ref,
                 vb_ref, cmat_ref, tnear_ref, tsel_ref, twin_ref, tswa_ref, oa_ref, ob_ref,
                 qall, mneg, mneg_far, m_s, acc_s, *, n_far, n_top):
    i = pl.program_id(1)
    rows = N_HEADS * Q_BLOCK
    _stack_heads(qa_ref, qall)
    q = qall[...]

    off = pl.multiple_of(i * (Q_BLOCK // CMP_STRIDE), 8)
    k_cmp = _mx(jnp.concatenate([kcmp_ref[0, 0, 0:n_far, :], kcmp_ref[0, 0, pl.ds(off, CMP_NEAR), :]], axis=0))
    v_cmp = _mx(jnp.concatenate([vcmp_ref[0, 0, 0:n_far, :], vcmp_ref[0, 0, pl.ds(off, CMP_NEAR), :]], axis=0))
    colf = lax.broadcasted_iota(jnp.int32, (1, n_far), 1)
    coln = lax.broadcasted_iota(jnp.int32, (1, CMP_NEAR), 1)
    col_ok = jnp.concatenate([(colf >= CMP_FRONT) & (colf < off), coln + off >= CMP_FRONT], axis=1)
    tiles = _lane_tiles(_dot_nt(q, k_cmp) + jnp.where(col_ok, 0.0, NEG))
    tiles[-1] = tiles[-1] + tnear_ref[...]
    m = _row_max(tiles)
    m = jnp.where(m > 0.5 * NEG, m, 0.0)
    e = [jnp.exp(t - m) for t in tiles]
    ov = _dot(_mx(jnp.concatenate(e, axis=1)), _with_ones(v_cmp))
    inv = 1.0 / jnp.maximum(ov[:, LANES:], 1e-30)
    o_c = ov[:, :LANES] * inv

    blkcol = lax.broadcasted_iota(jnp.int32, (Q_BLOCK, LANES), 1)
    for g in range(N_GROUPS):
        imp = jnp.zeros((Q_BLOCK, LANES), F32)
        for t, et in enumerate(e):
            pt = et * inv
            pg = sum(pt[(2 * r + g) * Q_BLOCK:(2 * r + g + 1) * Q_BLOCK] for r in range(GROUP))
            if t < len(e) - 1:
                cm = _mx(cmat_ref[t * LANES:(t + 1) * LANES, :])
            else:
                cm = _mx(cmat_ref[pl.ds(off, CMP_NEAR), :])
            hi = _mx(pg)
            lo = _mx(pg - hi.astype(F32))
            imp = imp + _dot(hi, cm) + _dot(lo, cm)
        sel = _select_blocks_t(imp.T, i, n_top)
        neg = jnp.where(sel, 0.0, NEG).T
        mneg[g * Q_BLOCK:(g + 1) * Q_BLOCK, :] = neg.astype(mneg.dtype)
        mneg_far[g * Q_BLOCK:(g + 1) * Q_BLOCK, :] = jnp.where(blkcol < 2 * (i - 1), neg, NEG).astype(mneg.dtype)

    m_s[...] = jnp.full(m_s.shape, NEG, F32)
    acc_s[...] = jnp.zeros(acc_s.shape, F32)

    half = rows // 2

    def flash_update(rs, s, v1):
        s_tiles = _lane_tiles(s)
        m_old = m_s[rs, :]
        m_new = jnp.maximum(m_old, _row_max(s_tiles))
        alpha = jnp.exp(m_old - m_new)
        p = jnp.concatenate([jnp.exp(t - m_new) for t in s_tiles], axis=1)
        acc_s[rs, :] = jnp.concatenate([alpha, alpha], axis=1) * acc_s[rs, :] + _dot(_mx(p), v1)
        m_s[rs, :] = m_new

    def far_start(j):
        return pl.multiple_of(Q_BLOCK + j * SEL_CHUNK, Q_BLOCK)

    def far_body(j, carry):
        kc = _mx(ks_ref[0, pl.ds(far_start(j), SEL_CHUNK), :])
        v1 = _with_ones(_mx(vs_ref[0, pl.ds(far_start(j), SEL_CHUNK), :]))
        madd = _dot(mneg_far[...], _block_of_key(SEL_CHUNK, j * (SEL_CHUNK // SEL_BLOCK)))
        for h in range(2):
            rs = slice(h * half, (h + 1) * half)
            s = _dot_nt(qall[rs, :], kc) + jnp.concatenate([madd] * (GROUP // 2), axis=0)
            flash_update(rs, s, v1)
        return carry

    n_far_keys = jnp.maximum(i - 1, 0) * Q_BLOCK
    lax.fori_loop(0, (n_far_keys + SEL_CHUNK - 1) // SEL_CHUNK, far_body, 0)
    nstart = pl.multiple_of(i * Q_BLOCK, Q_BLOCK)
    kc = _mx(ks_ref[0, pl.ds(nstart, 2 * Q_BLOCK), :])
    v1 = _with_ones(_mx(vs_ref[0, pl.ds(nstart, 2 * Q_BLOCK), :]))
    madd = _dot(mneg[...], _block_of_key(2 * Q_BLOCK, 2 * (i - 1)))
    col2 = lax.broadcasted_iota(jnp.int32, (1, 2 * Q_BLOCK), 1)
    extra = tsel_ref[...] + jnp.where((col2 < Q_BLOCK) & (i == 0), NEG, 0.0)
    for h in range(2):
        rs = slice(h * half, (h + 1) * half)
        s = _dot_nt(qall[rs, :], kc) + jnp.concatenate([madd] * (GROUP // 2), axis=0) + extra[rs, :]
        flash_update(rs, s, v1)
    acc = acc_s[...]
    o_s = acc[:, :LANES] / acc[:, LANES:]

    wpad = kw_ref.shape[1] - ks_ref.shape[1] + Q_BLOCK
    kwin = _mx(kw_ref[0, pl.ds(nstart, wpad + Q_BLOCK), :])
    vwin = _with_ones(_mx(vw_ref[0, pl.ds(nstart, wpad + Q_BLOCK), :]))
    colw = lax.broadcasted_iota(jnp.int32, (1, wpad + Q_BLOCK), 1)
    s = _dot_nt(q, kwin) + twin_ref[...] + jnp.where(colw + nstart >= wpad, 0.0, NEG)
    tiles = _lane_tiles(s)
    m = _row_max(tiles)
    ov = _dot(_mx(jnp.concatenate([jnp.exp(t - m) for t in tiles], axis=1)), vwin)
    o_w = ov[:, :LANES] / ov[:, LANES:]

    lo = lax.broadcasted_iota(jnp.int32, (Q_BLOCK, LANES), 1) < HEAD_DIM
    gates = ga_ref[...]
    for r in range(GROUP):
        tile = jnp.zeros((Q_BLOCK, LANES), F32)
        for c, o in enumerate((o_c, o_s, o_w)):
            g0 = gates[:, c * 8 + 2 * r:c * 8 + 2 * r + 1]
            g1 = gates[:, c * 8 + 2 * r + 1:c * 8 + 2 * r + 2]
            tile = tile + jnp.where(lo, g0, g1) * _pair_heads(o, r)
        oa_ref[:, r * LANES:(r + 1) * LANES] = tile.astype(oa_ref.dtype)

    _stack_heads(qb_ref, qall)
    q = qall[...]
    bpad = kb_ref.shape[1] - ks_ref.shape[1] + Q_BLOCK
    kwin = _mx(kb_ref[0, pl.ds(nstart, bpad + Q_BLOCK), :])
    vwin = _with_ones(_mx(vb_ref[0, pl.ds(nstart, bpad + Q_BLOCK), :]))
    colb = lax.broadcasted_iota(jnp.int32, (1, bpad + Q_BLOCK), 1)
    s = _dot_nt(q, kwin) + tswa_ref[...] + jnp.where(colb + nstart >= bpad, 0.0, NEG)
    sink = jnp.concatenate([jnp.full((Q_BLOCK, LANES), sink_ref[(h % 2) * GROUP + h // 2], F32)
                            for h in range(N_HEADS)], axis=0)
    tiles = _lane_tiles(s)
    m = jnp.maximum(_row_max(tiles), sink)
    ov = _dot(_mx(jnp.concatenate([jnp.exp(t - m) for t in tiles], axis=1)), vwin)
    o_b = ov[:, :LANES] / (ov[:, LANES:] + jnp.exp(sink - m))
    for r in range(GROUP):
        ob_ref[:, r * LANES:(r + 1) * LANES] = _pair_heads(o_b, r).astype(ob_ref.dtype)


def _rel_bucket_np(dist):
    n = np.maximum(dist, 0)
    max_exact = REL_BUCKETS // 2
    nf = np.maximum(n, 1).astype(np.float32)
    log_b = max_exact + (np.log(nf / max_exact) / math.log(REL_MAX_DIST / max_exact)
                         * (REL_BUCKETS - max_exact)).astype(np.int32)
    log_b = np.minimum(log_b, REL_BUCKETS - 1)
    return np.where(n < max_exact, n, log_b)


def _toeplitz_bias(tab, pad, width, window, shift_far):
    length = width + Q_BLOCK
    dist = pad + Q_BLOCK - 1 - np.arange(length)
    onehot = np.zeros((length, REL_BUCKETS), np.float32)
    onehot[np.arange(length), _rel_bucket_np(dist)] = 1.0
    vals = jnp.dot(jnp.asarray(onehot), tab, precision=lax.Precision.HIGHEST)
    if shift_far:
        vals = vals - tab[REL_BUCKETS - 1][None, :]
    valid = (dist >= 0) & (dist < window)
    vals = jnp.where(jnp.asarray(valid)[:, None], vals, NEG).T
    skew = jnp.tile(vals, (1, Q_BLOCK))[:, :Q_BLOCK * (length - 1)].reshape(N_HEADS, Q_BLOCK, length - 1)
    return skew[:, :, Q_BLOCK - 1:Q_BLOCK - 1 + width].reshape(N_HEADS * Q_BLOCK, width).astype(F32)


def _attention(proj, kvcmp, sinks, bias_table, bsz, seq):
    assert seq % SEL_CHUNK == 0
    nq = seq // Q_BLOCK
    n_far = seq // CMP_STRIDE
    n_sel = seq // SEL_BLOCK
    n_top = min(SEL_TOP_N, n_sel)
    assert n_top >= SEL_INIT_BLOCKS + SEL_LOCAL_BLOCKS and n_sel <= LANES
    wpad = Q_BLOCK * (-(-(NSA_WINDOW - 1) // Q_BLOCK))
    bpad = Q_BLOCK * (-(-(SWA_WINDOW - 1) // Q_BLOCK))
    pair = lambda tab: tab.astype(F32).reshape(REL_BUCKETS, N_GROUPS, GROUP).transpose(0, 2, 1).reshape(REL_BUCKETS, -1)
    tab_a = pair(bias_table[:, :N_HEADS])
    tab_b = pair(bias_table[:, N_HEADS:])
    near_pad = CMP_STRIDE * CMP_FRONT - (CMP_BLOCK - 1)
    t_near = _toeplitz_bias(tab_a, near_pad, CMP_STRIDE * CMP_NEAR, 1 << 30, True)[:, ::CMP_STRIDE]
    t_sel = _toeplitz_bias(tab_a, Q_BLOCK, 2 * Q_BLOCK, 1 << 30, True)
    t_win = _toeplitz_bias(tab_a, wpad, wpad + Q_BLOCK, NSA_WINDOW, False)
    t_swa = _toeplitz_bias(tab_b, bpad, bpad + Q_BLOCK, SWA_WINDOW, False)
    n_rows = kvcmp.shape[2]
    cn = (np.arange(n_rows) - CMP_FRONT)[:, None] * CMP_STRIDE
    sj = np.arange(LANES)[None, :] * SEL_BLOCK
    cmat = ((cn < sj + SEL_BLOCK) & (cn + CMP_BLOCK > sj) & (cn >= 0) & (cn + CMP_BLOCK <= seq)
            & (sj < seq)).astype(np.float32)
    cmat = jnp.asarray(cmat, F32)
    padded = lambda name, p: jnp.pad(proj[name].reshape(bsz, seq, LANES), ((0, 0), (p, 0), (0, 0)))
    ks, vs = padded('ks', Q_BLOCK), padded('vs', Q_BLOCK)
    kw, vw = padded('kw', wpad), padded('vw', wpad)
    kb, vb = padded('kb', bpad), padded('vb', bpad)
    rows = N_HEADS * Q_BLOCK
    qspec = pl.BlockSpec((Q_BLOCK, 4 * LANES), lambda b, i: (b * nq + i, 0))
    const2 = lambda shape: pl.BlockSpec(shape, lambda b, i: (0, 0))
    batch3 = lambda n: pl.BlockSpec((1, n, LANES), lambda b, i: (b, 0, 0))
    kernel = functools.partial(_attn_kernel, n_far=n_far, n_top=n_top)
    return pl.pallas_call(
        kernel,
        grid=(bsz, nq),
        in_specs=[pl.BlockSpec(memory_space=pltpu.SMEM),
                  qspec, qspec,
                  pl.BlockSpec((Q_BLOCK, LANES), lambda b, i: (b * nq + i, 0)),
                  pl.BlockSpec((1, 1, n_rows, LANES), lambda b, i: (0, b, 0, 0)),
                  pl.BlockSpec((1, 1, n_rows, LANES), lambda b, i: (1, b, 0, 0)),
                  batch3(seq + Q_BLOCK), batch3(seq + Q_BLOCK),
                  batch3(seq + wpad), batch3(seq + wpad),
                  batch3(seq + bpad), batch3(seq + bpad),
                  const2((n_rows, LANES)),
                  const2((rows, CMP_NEAR)),
                  const2((rows, 2 * Q_BLOCK)),
                  const2((rows, wpad + Q_BLOCK)),
                  const2((rows, bpad + Q_BLOCK))],
        out_specs=[qspec, qspec],
        out_shape=[jax.ShapeDtypeStruct((bsz * seq, 4 * LANES), BF16)] * 2,
        scratch_shapes=[pltpu.VMEM((rows, LANES), MXU_DTYPE),
                        pltpu.VMEM((N_GROUPS * Q_BLOCK, LANES), MXU_DTYPE),
                        pltpu.VMEM((N_GROUPS * Q_BLOCK, LANES), MXU_DTYPE),
                        pltpu.VMEM((rows, LANES), F32),
                        pltpu.VMEM((rows, 2 * LANES), F32)],
        compiler_params=pltpu.CompilerParams(dimension_semantics=("arbitrary", "arbitrary"),
                                             vmem_limit_bytes=VMEM_LIMIT),
        name="attention",
    )(sinks.astype(F32), proj['qa'], proj['qb'], proj['ga'], kvcmp, kvcmp, ks, vs, kw, vw, kb, vb,
      cmat, t_near, t_sel, t_win, t_swa)


def _layer_norm(y, g, b):
    mu = jnp.mean(y, axis=-1, keepdims=True)
    yc = y - mu
    var = jnp.mean(yc * yc, axis=-1, keepdims=True)
    return yc * lax.rsqrt(var + LN_EPS) * g + b


def _outproj_kernel(oa_ref, ob_ref, sg_ref, x_ref, pa_ref, pb_ref, wo_ref, g1_ref, b1_ref, wr_ref, rb_ref, sgu_ref,
                    sd_ref, tri_ref, h_ref, base_ref, eidx_ref, gate_ref, rank_ref, cnt_ref, carry):
    step = pl.program_id(0)
    tm = oa_ref.shape[0]

    @pl.when(step == 0)
    def _():
        carry[...] = jnp.zeros(carry.shape, F32)

    sg = sg_ref[...].astype(F32)
    merged = (sg[:, :D_MODEL] * _dot(_mx(oa_ref[...]), pa_ref[...])
              + sg[:, D_MODEL:] * _dot(_mx(ob_ref[...]), pb_ref[...]))
    mix = _dot(_mx(merged), wo_ref[...])
    h = _layer_norm(DN_ALPHA * x_ref[...] + mix, g1_ref[...], b1_ref[...])
    h_ref[...] = h
    hb = _mx(h)

    gu = _dot(hb, sgu_ref[...])
    shared = _dot(_mx(jax.nn.silu(gu[:, :SHARED_HIDDEN]) * gu[:, SHARED_HIDDEN:]), sd_ref[...])
    base_ref[...] = DN_ALPHA * h + shared

    scores = jax.nn.sigmoid(_dot_nt(wr_ref[...], hb))
    choice = scores + rb_ref[:, 0:1]
    per_group = N_EXPERTS // N_EXPERT_GROUPS
    gs = []
    for g in range(N_EXPERT_GROUPS):
        cg = choice[g * per_group:(g + 1) * per_group]
        m1 = jnp.max(cg, axis=0, keepdims=True)
        is_m = cg == m1
        n_m = jnp.sum(is_m.astype(F32), axis=0, keepdims=True)
        m2 = jnp.max(jnp.where(is_m, -jnp.inf, cg), axis=0, keepdims=True)
        gs.append(m1 + jnp.where(n_m > 1.5, m1, m2))
    gs = jnp.concatenate(gs, axis=0)
    gid = lax.broadcasted_iota(jnp.int32, gs.shape, 0)
    beaten = jnp.zeros(gs.shape, jnp.int32)
    for g in range(N_EXPERT_GROUPS):
        other = gs[g:g + 1]
        beaten = beaten + ((other > gs) | ((other == gs) & (g < gid))).astype(jnp.int32)
    keep_g = beaten < TOPK_EXPERT_GROUPS
    keep = jnp.concatenate([jnp.broadcast_to(keep_g[g:g + 1], (per_group, tm)) for g in range(N_EXPERT_GROUPS)],
                           axis=0)
    cand = jnp.where(keep, choice, -jnp.inf)
    eid = lax.broadcasted_iota(jnp.int32, cand.shape, 0)
    hits = []
    e_rows = []
    w_rows = []
    for _ in range(TOP_K):
        m = jnp.max(cand, axis=0, keepdims=True)
        idx = jnp.min(jnp.where(cand == m, eid, N_EXPERTS), axis=0, keepdims=True)
        hit = eid == idx
        hits.append(hit)
        e_rows.append(idx)
        w_rows.append(jnp.sum(jnp.where(hit, scores, 0.0), axis=0, keepdims=True))
        cand = jnp.where(hit, -jnp.inf, cand)
    w = jnp.concatenate(w_rows, axis=0)
    gate_ref[...] = w / jnp.sum(w, axis=0, keepdims=True) * ROUTED_SCALE
    eidx_ref[...] = jnp.concatenate(e_rows, axis=0)

    onehot = jnp.zeros(cand.shape, F32)
    for hit in hits:
        onehot = onehot + hit.astype(F32)
    before = _dot(onehot.astype(BF16), tri_ref[...]) + carry[:, 0:1]
    rank_ref[...] = jnp.concatenate(
        [jnp.sum(jnp.where(hit, before, 0.0), axis=0, keepdims=True) for hit in hits], axis=0).astype(jnp.int32)
    carry[...] = carry[...] + jnp.sum(onehot, axis=1, keepdims=True)
    cnt_ref[...] = carry[...]


def _out_projection(oa, ob, sg, x2, proj_a, proj_b, w_out, ln_g, ln_b, w_router, router_bias, s_gate, s_up, s_down):
    t = x2.shape[0]
    tm = OUT_TM
    pair_rows = lambda p: p.reshape(N_GROUPS, GROUP, HEAD_DIM, -1).transpose(1, 0, 2, 3).reshape(p.shape)
    pa = pair_rows(proj_a).astype(MXU_DTYPE)
    pb = pair_rows(proj_b).astype(MXU_DTYPE)
    tri = jnp.asarray(np.triu(np.ones((tm, tm), np.float32), 1), BF16)
    row = lambda i: (i, 0)
    fixed = lambda i: (0, 0)
    col = lambda i: (0, i)
    outs = pl.pallas_call(
        _outproj_kernel,
        grid=(t // tm,),
        in_specs=[pl.BlockSpec((tm, 4 * LANES), row), pl.BlockSpec((tm, 4 * LANES), row),
                  pl.BlockSpec((tm, 2 * D_MODEL), row), pl.BlockSpec((tm, D_MODEL), row),
                  pl.BlockSpec((4 * LANES, D_MODEL), fixed), pl.BlockSpec((4 * LANES, D_MODEL), fixed),
                  pl.BlockSpec((D_MODEL, D_MODEL), fixed),
                  pl.BlockSpec((1, D_MODEL), fixed), pl.BlockSpec((1, D_MODEL), fixed),
                  pl.BlockSpec((N_EXPERTS, D_MODEL), fixed), pl.BlockSpec((N_EXPERTS, LANES), fixed),
                  pl.BlockSpec((D_MODEL, 2 * SHARED_HIDDEN), fixed), pl.BlockSpec((SHARED_HIDDEN, D_MODEL), fixed),
                  pl.BlockSpec((tm, tm), fixed)],
        out_specs=[pl.BlockSpec((tm, D_MODEL), row), pl.BlockSpec((tm, D_MODEL), row),
                   pl.BlockSpec((TOP_K, tm), col), pl.BlockSpec((TOP_K, tm), col), pl.BlockSpec((TOP_K, tm), col),
                   pl.BlockSpec((N_EXPERTS, LANES), fixed)],
        out_shape=[jax.ShapeDtypeStruct((t, D_MODEL), F32), jax.ShapeDtypeStruct((t, D_MODEL), F32),
                   jax.ShapeDtypeStruct((TOP_K, t), jnp.int32), jax.ShapeDtypeStruct((TOP_K, t), F32),
                   jax.ShapeDtypeStruct((TOP_K, t), jnp.int32), jax.ShapeDtypeStruct((N_EXPERTS, LANES), F32)],
        scratch_shapes=[pltpu.VMEM((N_EXPERTS, LANES), F32)],
        compiler_params=pltpu.CompilerParams(dimension_semantics=("arbitrary",), vmem_limit_bytes=VMEM_LIMIT),
        name="out_projection_router",
    )(oa, ob, sg, x2, pa, pb, w_out.astype(MXU_DTYPE), ln_g.reshape(1, -1), ln_b.reshape(1, -1),
      w_router.T.astype(MXU_DTYPE), jnp.broadcast_to(router_bias.astype(F32)[:, None], (N_EXPERTS, LANES)),
      jnp.concatenate([s_gate, s_up], axis=1).astype(MXU_DTYPE), s_down.astype(MXU_DTYPE), tri)
    return outs


def _dispatch_kernel(zstart_ref, cnt_ref, dest_ref, h_ref, xs_ref, zeros, sem):
    step = pl.program_id(0)
    tm = h_ref.shape[0]

    @pl.when(step == 0)
    def _():
        zeros[...] = jnp.zeros(zeros.shape, F32)

        def fill(e, c):
            @pl.when(cnt_ref[e] > 0)
            def _():
                cp = pltpu.make_async_copy(zeros, xs_ref.at[pl.ds(zstart_ref[e], MOE_BM)], sem)
                cp.start()
                cp.wait()
            return c
        lax.fori_loop(0, N_EXPERTS, fill, 0)

    def issue(t, c):
        for k in range(TOP_K):
            pltpu.make_async_copy(h_ref.at[t], xs_ref.at[dest_ref[k, t]], sem).start(priority=k % 2)
        return c
    lax.fori_loop(0, tm, issue, 0)
    for k in range(TOP_K):
        pltpu.make_async_copy(h_ref, xs_ref.at[pl.ds(0, tm)], sem).wait()


def _dispatch(h3, dest, zstart, counts, n_rows):
    t = h3.shape[0]
    tm = DISP_TM
    return pl.pallas_call(
        _dispatch_kernel,
        grid_spec=pltpu.PrefetchScalarGridSpec(
            num_scalar_prefetch=2,
            grid=(t // tm,),
            in_specs=[pl.BlockSpec((TOP_K, tm), lambda i, *_: (0, i), memory_space=pltpu.SMEM),
                      pl.BlockSpec((tm,) + ROW_TILE, lambda i, *_: (i, 0, 0))],
            out_specs=pl.BlockSpec(memory_space=pl.ANY),
            scratch_shapes=[pltpu.VMEM((MOE_BM,) + ROW_TILE, F32), pltpu.SemaphoreType.DMA(())]),
        out_shape=jax.ShapeDtypeStruct((n_rows,) + ROW_TILE, F32),
        compiler_params=pltpu.CompilerParams(dimension_semantics=("arbitrary",), vmem_limit_bytes=VMEM_LIMIT),
        name="moe_dispatch",
    )(zstart, counts, dest, h3)


def _experts_kernel(blk_e_ref, nused_ref, xs_ref, wg_ref, wu_ref, wd_ref, ys_ref, wg_s, wu_s, wd_s):
    b = pl.program_id(0)
    prev = blk_e_ref[jnp.maximum(b - 1, 0)]

    @pl.when((b == 0) | (blk_e_ref[b] != prev))
    def _():
        wg_s[...] = _mx(wg_ref[0])
        wu_s[...] = _mx(wu_ref[0])
        wd_s[...] = _mx(wd_ref[0])

    n_tiles = D_MODEL // LANES

    @pl.when(b < nused_ref[0])
    def _():
        xb = _mx(jnp.concatenate([xs_ref[:, c, :] for c in range(n_tiles)], axis=1))
        hid = jax.nn.silu(_dot(xb, wg_s[...])) * _dot(xb, wu_s[...])
        y = _dot(_mx(hid), wd_s[...])
        for c in range(n_tiles):
            ys_ref[:, c, :] = y[:, c * LANES:(c + 1) * LANES]

    @pl.when(b >= nused_ref[0])
    def _():
        ys_ref[...] = jnp.zeros(ys_ref.shape, F32)


def _experts(xs, blk_e, nused, e_gate, e_up, e_down):
    n_rows = xs.shape[0]
    n_blocks = n_rows // MOE_BM
    xmap = lambda b, be, nu: (jnp.minimum(b, nu[0] - 1), 0, 0)
    wmap = lambda b, be, nu: (be[b], 0, 0)
    return pl.pallas_call(
        _experts_kernel,
        grid_spec=pltpu.PrefetchScalarGridSpec(
            num_scalar_prefetch=2,
            grid=(n_blocks,),
            in_specs=[pl.BlockSpec((MOE_BM,) + ROW_TILE, xmap),
                      pl.BlockSpec((1, D_MODEL, EXPERT_HIDDEN), wmap),
                      pl.BlockSpec((1, D_MODEL, EXPERT_HIDDEN), wmap),
                      pl.BlockSpec((1, EXPERT_HIDDEN, D_MODEL), wmap)],
            out_specs=pl.BlockSpec((MOE_BM,) + ROW_TILE, lambda b, be, nu: (b, 0, 0)),
            scratch_shapes=[pltpu.VMEM((D_MODEL, EXPERT_HIDDEN), MXU_DTYPE),
                            pltpu.VMEM((D_MODEL, EXPERT_HIDDEN), MXU_DTYPE),
                            pltpu.VMEM((EXPERT_HIDDEN, D_MODEL), MXU_DTYPE)]),
        out_shape=jax.ShapeDtypeStruct((n_rows,) + ROW_TILE, F32),
        compiler_params=pltpu.CompilerParams(dimension_semantics=("arbitrary",), vmem_limit_bytes=VMEM_LIMIT),
        name="moe_experts",
    )(blk_e, nused, xs, e_gate, e_up, e_down)


def _combine_kernel(dest_ref, gate_ref, base_ref, g2_ref, b2_ref, ys_ref, out_ref, buf, sem):
    tm = base_ref.shape[0]

    def issue(t, c):
        for k in range(TOP_K):
            pltpu.make_async_copy(ys_ref.at[dest_ref[k, t]], buf.at[k, t], sem).start(priority=k % 2)
        return c
    lax.fori_loop(0, tm, issue, 0)
    for k in range(TOP_K):
        pltpu.make_async_copy(ys_ref.at[pl.ds(0, tm)], buf.at[k], sem).wait()

    def token(t, c):
        y = base_ref[t]
        for k in range(TOP_K):
            y = y + gate_ref[k, t] * buf[k, t]
        out_ref[t] = y
        return c
    lax.fori_loop(0, tm, token, 0)
    y = out_ref[...]
    n = float(D_MODEL)
    mu = jnp.sum(jnp.sum(y, axis=2, keepdims=True), axis=1, keepdims=True) / n
    yc = y - mu
    var = jnp.sum(jnp.sum(yc * yc, axis=2, keepdims=True), axis=1, keepdims=True) / n
    out_ref[...] = yc * lax.rsqrt(var + LN_EPS) * g2_ref[...] + b2_ref[...]


def _combine(ys3, dest, gate, base3, ln_g, ln_b):
    t = base3.shape[0]
    tm = COMB_TM
    return pl.pallas_call(
        _combine_kernel,
        grid=(t // tm,),
        in_specs=[pl.BlockSpec((TOP_K, tm), lambda i: (0, i), memory_space=pltpu.SMEM),
                  pl.BlockSpec((TOP_K, tm), lambda i: (0, i), memory_space=pltpu.SMEM),
                  pl.BlockSpec((tm,) + ROW_TILE, lambda i: (i, 0, 0)),
                  pl.BlockSpec(ROW_TILE, lambda i: (0, 0)),
                  pl.BlockSpec(ROW_TILE, lambda i: (0, 0)),
                  pl.BlockSpec(memory_space=pl.ANY)],
        out_specs=pl.BlockSpec((tm,) + ROW_TILE, lambda i: (i, 0, 0)),
        out_shape=jax.ShapeDtypeStruct((t,) + ROW_TILE, F32),
        scratch_shapes=[pltpu.VMEM((TOP_K, tm) + ROW_TILE, F32), pltpu.SemaphoreType.DMA(())],
        compiler_params=pltpu.CompilerParams(dimension_semantics=("arbitrary",), vmem_limit_bytes=VMEM_LIMIT),
        name="moe_combine",
    )(dest, gate, base3, ln_g.reshape(ROW_TILE), ln_b.reshape(ROW_TILE), ys3)


def _moe_layout(eidx, rank, counts):
    n_assign = eidx.size
    n_blocks = (n_assign + N_EXPERTS * (MOE_BM - 1)) // MOE_BM
    padded = (counts + MOE_BM - 1) // MOE_BM * MOE_BM
    pends = jnp.cumsum(padded)
    pstarts = pends - padded
    experts = jnp.arange(N_EXPERTS, dtype=jnp.int32)
    dest = jnp.sum(jnp.where(eidx[..., None] == experts, pstarts, 0), axis=-1) + rank
    block_row = jnp.arange(n_blocks, dtype=jnp.int32) * MOE_BM
    blk_e = jnp.minimum(jnp.sum(pends[None, :] <= block_row[:, None], axis=1), N_EXPERTS - 1).astype(jnp.int32)
    nused = (pends[-1:] // MOE_BM).astype(jnp.int32)
    zstart = jnp.maximum(pends - MOE_BM, 0).astype(jnp.int32)
    return dest.astype(jnp.int32), blk_e, nused, zstart, n_blocks * MOE_BM


def _layer(x, w_in, cmp_pe, cmp_w1, cmp_b1, cmp_w2, sinks, bias_table, proj_a, proj_b, w_out, ln1_g, ln1_b,
           w_router, router_bias, e_gate, e_up, e_down, s_gate, s_up, s_down, ln2_g, ln2_b):
    bsz, seq, d = x.shape
    x2 = x.reshape(bsz * seq, d)
    proj = _in_projection(x2, w_in)
    kvcmp = _compress(proj['kc'], proj['vc'], bsz, seq, cmp_pe, cmp_w1, cmp_b1, cmp_w2)
    oa, ob = _attention(proj, kvcmp, sinks, bias_table, bsz, seq)
    h, base, eidx, gate, rank, cnt = _out_projection(oa, ob, proj['sg'], x2, proj_a, proj_b, w_out, ln1_g, ln1_b,
                                                     w_router, router_bias, s_gate, s_up, s_down)
    counts = cnt[:, 0].astype(jnp.int32)
    dest, blk_e, nused, zstart, n_rows = _moe_layout(eidx, rank, counts)
    xs = _dispatch(h.reshape((-1,) + ROW_TILE), dest, zstart, counts, n_rows)
    ys = _experts(xs, blk_e, nused, e_gate, e_up, e_down)
    out = _combine(ys, dest, gate, base.reshape((-1,) + ROW_TILE), ln2_g, ln2_b)
    return out.reshape(bsz, seq, d)


def kernel(x, w_in, cmp_pe, cmp_w1, cmp_b1, cmp_w2, attn_sinks, rel_bias_table, proj_a, proj_b, w_out, ln1_g, ln1_b,
           w_router, router_bias, expert_w_gate, expert_w_up, expert_w_down, shared_w_gate, shared_w_up,
           shared_w_down, ln2_g, ln2_b):
    h = x
    for l in range(DEPTH):
        h = _layer(h, w_in[l], cmp_pe[l], cmp_w1[l], cmp_b1[l], cmp_w2[l], attn_sinks[l], rel_bias_table, proj_a[l],
                   proj_b[l], w_out[l], ln1_g[l], ln1_b[l], w_router[l], router_bias[l], expert_w_gate[l],
                   expert_w_up[l], expert_w_down[l], shared_w_gate[l], shared_w_up[l], shared_w_down[l], ln2_g[l],
                   ln2_b[l])
    return h
```

```python
import functools
import math

import numpy as np
import jax
import jax.numpy as jnp
from jax import lax
from jax.experimental import pallas as pl
from jax.experimental.pallas import tpu as pltpu

F32 = jnp.float32
BF16 = jnp.bfloat16
MXU_DTYPE = jnp.bfloat16

D_MODEL = 1024
HEAD_DIM = 64
ATTN_SCALE = HEAD_DIM ** -0.5
Q_BLOCK = 128
N_HEADS = 8
N_GROUPS = 2
GROUP = 4
CMP_BLOCK = 32
CMP_STRIDE = 16
CMP_HIDDEN = 128
SEL_BLOCK = 64
SEL_TOP_N = 8
SEL_INIT_BLOCKS = 1
SEL_LOCAL_BLOCKS = 2
NSA_WINDOW = 512
SWA_WINDOW = 128
REL_BUCKETS = 32
REL_MAX_DIST = 128
N_EXPERTS = 256
TOP_K = 8
EXPERT_HIDDEN = 256
SHARED_HIDDEN = 256
N_EXPERT_GROUPS = 8
TOPK_EXPERT_GROUPS = 4
ROUTED_SCALE = 2.5
LN_EPS = 1e-5
DEPTH = 1
DN_ALPHA = (2 * DEPTH) ** 0.25

NEG = -1e30
LANES = 128
ROW_TILE = (8, LANES)
CMP_FRONT = 16
CMP_NEAR = LANES
SEL_CHUNK = 512
VMEM_LIMIT = 56 * 1024 * 1024

IN_TM = 512
OUT_TM = 256
MOE_BM = 256
DISP_TM = 256
COMB_TM = 128


def _dot(a, b):
    return jnp.dot(a, b, preferred_element_type=F32)


def _dot_nt(a, b):
    return lax.dot_general(a, b, (((1,), (1,)), ((), ())), preferred_element_type=F32)


def _mx(a):
    return a.astype(MXU_DTYPE)


_IN_COLS = (('qa', 512), ('qb', 512), ('kc', 128), ('vc', 128), ('ks', 128), ('vs', 128), ('kw', 128),
            ('vw', 128), ('kb', 128), ('vb', 128), ('ga', 128), ('sg', 2048))


def _inproj_kernel(x_ref, w_ref, qa_ref, qb_ref, kc_ref, vc_ref, ks_ref, vs_ref, kw_ref, vw_ref, kb_ref, vb_ref,
                   ga_ref, sg_ref):
    xb = _mx(x_ref[...])
    outs = dict(qa=qa_ref, qb=qb_ref, kc=kc_ref, vc=vc_ref, ks=ks_ref, vs=vs_ref, kw=kw_ref, vw=vw_ref,
                kb=kb_ref, vb=vb_ref, ga=ga_ref, sg=sg_ref)
    off = 0
    for name, width in _IN_COLS:
        for c0 in range(0, width, 512):
            cw = min(512, width - c0)
            y = _dot(xb, w_ref[:, off + c0:off + c0 + cw])
            if name in ('ga', 'sg'):
                y = jax.nn.sigmoid(y)
            outs[name][:, c0:c0 + cw] = y.astype(outs[name].dtype)
        off += width


def _pair_head_columns(w):
    return w.reshape(w.shape[0], N_GROUPS, GROUP, HEAD_DIM).transpose(0, 2, 1, 3).reshape(w.shape[0], -1)


def _in_projection(x2, w_in):
    t = x2.shape[0]
    sizes = (512, 128, 128, 128, 128, 128, 128, 24, 512, 128, 128, 1024, 1024)
    offs = np.cumsum((0,) + sizes)
    part = [w_in[:, offs[k]:offs[k + 1]] for k in range(len(sizes))]
    w_qa, w_kc, w_vc, w_ks, w_vs, w_kw, w_vw, w_g, w_qb, w_kb, w_vb, w_gate_a, w_gate_b = part
    w_qa = _pair_head_columns(w_qa) * ATTN_SCALE
    w_qb = _pair_head_columns(w_qb) * ATTN_SCALE
    w_ga = w_g.reshape(-1, N_GROUPS, GROUP, 3).transpose(0, 3, 2, 1).reshape(-1, 24)
    w_ga = jnp.pad(w_ga, ((0, 0), (0, LANES - 24)))
    w_all = jnp.concatenate([w_qa, w_qb, w_kc, w_vc, w_ks, w_vs, w_kw, w_vw, w_kb, w_vb, w_ga, w_gate_a, w_gate_b],
                            axis=1).astype(MXU_DTYPE)
    n_all = w_all.shape[1]
    out_shape = []
    out_specs = []
    for name, width in _IN_COLS:
        dt = F32 if name == 'ga' else BF16
        out_shape.append(jax.ShapeDtypeStruct((t, width), dt))
        out_specs.append(pl.BlockSpec((IN_TM, width), lambda i: (i, 0)))
    outs = pl.pallas_call(
        _inproj_kernel,
        grid=(t // IN_TM,),
        in_specs=[pl.BlockSpec((IN_TM, D_MODEL), lambda i: (i, 0)),
                  pl.BlockSpec((D_MODEL, n_all), lambda i: (0, 0))],
        out_specs=out_specs,
        out_shape=out_shape,
        compiler_params=pltpu.CompilerParams(dimension_semantics=("arbitrary",), vmem_limit_bytes=VMEM_LIMIT),
        name="in_projection",
    )(x2, w_all)
    return dict(zip([n for n, _ in _IN_COLS], outs))


def _compress_kernel(tok_ref, w1_ref, pe_ref, w1o_ref, b1_ref, w2_ref, out_ref):
    n_chunks = tok_ref.shape[2]
    ab = _dot(tok_ref[0, 0], w1_ref[0])
    a = ab[:, :2 * CMP_HIDDEN]
    b_next = pltpu.roll(ab[:, 2 * CMP_HIDDEN:], n_chunks - 1, 0)
    cb = _dot(_mx(pe_ref[0]), _mx(w1o_ref[0]))[0:1, :] + b1_ref[0]
    cb2 = jnp.concatenate([cb, cb], axis=1)
    hid = jax.nn.gelu(a + b_next + cb2)
    out = _dot(_mx(hid), w2_ref[0])
    row = lax.broadcasted_iota(jnp.int32, out.shape, 0)
    out = jnp.where(row < n_chunks - 1, out, 0.0)
    out_ref[0, 0, 0:CMP_FRONT, :] = jnp.zeros((CMP_FRONT, LANES), F32)
    out_ref[0, 0, CMP_FRONT:CMP_FRONT + n_chunks, :] = out
    out_ref[0, 0, CMP_FRONT + n_chunks:, :] = jnp.zeros((CMP_NEAR - CMP_FRONT, LANES), F32)


def _compress(kc, vc, bsz, seq, cmp_pe, cmp_w1, cmp_b1, cmp_w2):
    n_chunks = seq // CMP_STRIDE
    tok = jnp.stack([kc, vc]).reshape(2, bsz, n_chunks, CMP_STRIDE * LANES)
    eye = jnp.eye(N_GROUPS, dtype=F32)
    w1r = cmp_w1.reshape(2, 2, CMP_STRIDE, HEAD_DIM, CMP_HIDDEN)
    w1 = jnp.einsum('khjdn,gG->kjgdhGn', w1r, eye).reshape(2, CMP_STRIDE * LANES, 4 * CMP_HIDDEN).astype(MXU_DTYPE)
    w2 = jnp.einsum('knd,gG->kgnGd', cmp_w2, eye).reshape(2, 2 * CMP_HIDDEN, LANES).astype(MXU_DTYPE)
    pe = jnp.pad(cmp_pe.reshape(2, 1, CMP_BLOCK * HEAD_DIM), ((0, 0), (0, 7), (0, 0)))
    b1 = cmp_b1.reshape(2, 1, CMP_HIDDEN)
    rows = CMP_FRONT + n_chunks + CMP_NEAR - CMP_FRONT
    return pl.pallas_call(
        _compress_kernel,
        grid=(2, bsz),
        in_specs=[pl.BlockSpec((1, 1, n_chunks, CMP_STRIDE * LANES), lambda k, b: (k, b, 0, 0)),
                  pl.BlockSpec((1, CMP_STRIDE * LANES, 4 * CMP_HIDDEN), lambda k, b: (k, 0, 0)),
                  pl.BlockSpec((1, 8, CMP_BLOCK * HEAD_DIM), lambda k, b: (k, 0, 0)),
                  pl.BlockSpec((1, CMP_BLOCK * HEAD_DIM, CMP_HIDDEN), lambda k, b: (k, 0, 0)),
                  pl.BlockSpec((1, 1, CMP_HIDDEN), lambda k, b: (k, 0, 0)),
                  pl.BlockSpec((1, 2 * CMP_HIDDEN, LANES), lambda k, b: (k, 0, 0))],
        out_specs=pl.BlockSpec((1, 1, rows, LANES), lambda k, b: (k, b, 0, 0)),
        out_shape=jax.ShapeDtypeStruct((2, bsz, rows, LANES), F32),
        compiler_params=pltpu.CompilerParams(dimension_semantics=("arbitrary", "arbitrary"),
                                             vmem_limit_bytes=VMEM_LIMIT),
        name="nsa_compress",
    )(tok, w1, pe, cmp_w1, b1, w2)


def _stack_heads(q_ref, dst):
    lo = lax.broadcasted_iota(jnp.int32, (Q_BLOCK, LANES), 1) < HEAD_DIM
    for r in range(GROUP):
        qr = q_ref[:, r * LANES:(r + 1) * LANES].astype(dst.dtype)
        z = jnp.zeros_like(qr)
        dst[(2 * r) * Q_BLOCK:(2 * r + 1) * Q_BLOCK, :] = jnp.where(lo, qr, z)
        dst[(2 * r + 1) * Q_BLOCK:(2 * r + 2) * Q_BLOCK, :] = jnp.where(lo, z, qr)


def _pair_heads(o, r):
    lo = lax.broadcasted_iota(jnp.int32, (Q_BLOCK, LANES), 1) < HEAD_DIM
    return jnp.where(lo, o[(2 * r) * Q_BLOCK:(2 * r + 1) * Q_BLOCK], o[(2 * r + 1) * Q_BLOCK:(2 * r + 2) * Q_BLOCK])


def _lane_tiles(x):
    return [x[:, t * LANES:(t + 1) * LANES] for t in range(x.shape[1] // LANES)]


def _row_max(tiles):
    mx = tiles[0]
    for t in tiles[1:]:
        mx = jnp.maximum(mx, t)
    return jnp.broadcast_to(jnp.max(mx, axis=1, keepdims=True), mx.shape)


def _with_ones(v):
    return jnp.concatenate([v, jnp.ones(v.shape, v.dtype)], axis=1)


def _block_of_key(n_keys, first_block):
    b = lax.broadcasted_iota(jnp.int32, (LANES, n_keys), 0)
    k = lax.broadcasted_iota(jnp.int32, (LANES, n_keys), 1)
    return (b == (k // SEL_BLOCK) + first_block).astype(MXU_DTYPE)


def _select_blocks_t(imp_t, i, n_top):
    blk = lax.broadcasted_iota(jnp.int32, imp_t.shape, 0)
    qcol = lax.broadcasted_iota(jnp.int32, imp_t.shape, 1)
    back = (2 * i + (qcol >= SEL_BLOCK).astype(jnp.int32)) - blk
    sel = (back >= 0) & ((blk < SEL_INIT_BLOCKS) | (back < SEL_LOCAL_BLOCKS))
    cand = jnp.where((back >= SEL_LOCAL_BLOCKS) & (blk >= SEL_INIT_BLOCKS), imp_t, -1.0)
    blk_f = blk.astype(F32)
    for _ in range(n_top - SEL_INIT_BLOCKS - SEL_LOCAL_BLOCKS):
        m = jnp.max(cand, axis=0, keepdims=True)
        idx = jnp.min(jnp.where(cand == m, blk_f, float(LANES)), axis=0, keepdims=True)
        hit = blk_f == idx
        sel = sel | (hit & (m >= 0.0))
        cand = jnp.where(hit, -2.0, cand)
    return sel


def _attn_kernel(sink_ref, qa_ref, qb_ref, ga_ref, kcmp_ref, vcmp_ref, ks_ref, vs_ref, kw_ref, vw_ref, kb_ref,
                 vb_ref, cmat_ref, tnear_ref, tsel_ref, twin_ref, tswa_ref, oa_ref, ob_ref,
                 qall, qball, mneg, mneg_far, m_s, acc_s, s_buf, oa_acc, *, n_far, n_top):
    i = pl.program_id(1)
    rows = N_HEADS * Q_BLOCK
    half = rows // 2
    halves = (slice(0, half), slice(half, rows))
    _stack_heads(qa_ref, qall)
    _stack_heads(qb_ref, qball)
    nstart = pl.multiple_of(i * Q_BLOCK, Q_BLOCK)
    lo = lax.broadcasted_iota(jnp.int32, (Q_BLOCK, LANES), 1) < HEAD_DIM
    gates = ga_ref[...]

    def gate_tile(c, r):
        return jnp.where(lo, gates[:, c * 8 + 2 * r:c * 8 + 2 * r + 1], gates[:, c * 8 + 2 * r + 1:c * 8 + 2 * r + 2])

    def softmax_pv(s_tiles, v1, fix_max=None):
        m = _row_max(s_tiles)
        if fix_max is not None:
            m = fix_max(m)
        e = [jnp.exp(t - m) for t in s_tiles]
        return e, m, _dot(_mx(jnp.concatenate(e, axis=1)), v1)

    off = pl.multiple_of(i * (Q_BLOCK // CMP_STRIDE), 8)
    k_cmp = _mx(jnp.concatenate([kcmp_ref[0, 0, 0:n_far, :], kcmp_ref[0, 0, pl.ds(off, CMP_NEAR), :]], axis=0))
    v_cmp = _with_ones(_mx(jnp.concatenate([vcmp_ref[0, 0, 0:n_far, :], vcmp_ref[0, 0, pl.ds(off, CMP_NEAR), :]],
                                           axis=0)))
    colf = lax.broadcasted_iota(jnp.int32, (1, n_far), 1)
    coln = lax.broadcasted_iota(jnp.int32, (1, CMP_NEAR), 1)
    col_ok = jnp.concatenate([(colf >= CMP_FRONT) & (colf < off), coln + off >= CMP_FRONT], axis=1)
    mask_c = jnp.where(col_ok, 0.0, NEG)
    no_key = lambda m: jnp.where(m > 0.5 * NEG, m, 0.0)
    p_cmp, o_c = [], []
    for rs in halves:
        tiles = _lane_tiles(_dot_nt(qall[rs, :], k_cmp) + mask_c)
        tiles[-1] = tiles[-1] + tnear_ref[rs, :]
        e, _, ov = softmax_pv(tiles, v_cmp, no_key)
        inv = 1.0 / jnp.maximum(ov[:, LANES:], 1e-30)
        o_c.append(ov[:, :LANES] * inv)
        p_cmp.append([t * inv for t in e])
    o_c = jnp.concatenate(o_c, axis=0)

    wpad = kw_ref.shape[1] - ks_ref.shape[1] + Q_BLOCK
    kwin = _mx(kw_ref[0, pl.ds(nstart, wpad + Q_BLOCK), :])
    vwin = _with_ones(_mx(vw_ref[0, pl.ds(nstart, wpad + Q_BLOCK), :]))
    colw = lax.broadcasted_iota(jnp.int32, (1, wpad + Q_BLOCK), 1)
    mask_w = jnp.where(colw + nstart >= wpad, 0.0, NEG)
    o_w = []
    for rs in halves:
        _, _, ov = softmax_pv(_lane_tiles(_dot_nt(qall[rs, :], kwin) + twin_ref[rs, :] + mask_w), vwin)
        o_w.append(ov[:, :LANES] / ov[:, LANES:])
    o_w = jnp.concatenate(o_w, axis=0)
    for r in range(GROUP):
        oa_acc[:, r * LANES:(r + 1) * LANES] = (gate_tile(0, r) * _pair_heads(o_c, r)
                                                + gate_tile(2, r) * _pair_heads(o_w, r))

    bpad = kb_ref.shape[1] - ks_ref.shape[1] + Q_BLOCK
    kwin = _mx(kb_ref[0, pl.ds(nstart, bpad + Q_BLOCK), :])
    vwin = _with_ones(_mx(vb_ref[0, pl.ds(nstart, bpad + Q_BLOCK), :]))
    colb = lax.broadcasted_iota(jnp.int32, (1, bpad + Q_BLOCK), 1)
    mask_b = jnp.where(colb + nstart >= bpad, 0.0, NEG)
    o_b = []
    for hh, rs in enumerate(halves):
        sink = jnp.concatenate([jnp.full((Q_BLOCK, LANES), sink_ref[(h % 2) * GROUP + h // 2], F32)
                                for h in range(hh * N_HEADS // 2, (hh + 1) * N_HEADS // 2)], axis=0)
        _, m, ov = softmax_pv(_lane_tiles(_dot_nt(qball[rs, :], kwin) + tswa_ref[rs, :] + mask_b), vwin,
                              lambda m: jnp.maximum(m, sink))
        o_b.append(ov[:, :LANES] / (ov[:, LANES:] + jnp.exp(sink - m)))
    o_b = jnp.concatenate(o_b, axis=0)
    for r in range(GROUP):
        ob_ref[:, r * LANES:(r + 1) * LANES] = _pair_heads(o_b, r).astype(ob_ref.dtype)

    blkcol = lax.broadcasted_iota(jnp.int32, (Q_BLOCK, LANES), 1)
    n_tiles = len(p_cmp[0])
    for g in range(N_GROUPS):
        imp = jnp.zeros((Q_BLOCK, LANES), F32)
        for t in range(n_tiles):
            pg = sum(p_cmp[r // 2][t][(2 * (r % 2) + g) * Q_BLOCK:(2 * (r % 2) + g + 1) * Q_BLOCK]
                     for r in range(GROUP))
            if t < n_tiles - 1:
                cm = _mx(cmat_ref[t * LANES:(t + 1) * LANES, :])
            else:
                cm = _mx(cmat_ref[pl.ds(off, CMP_NEAR), :])
            hi = _mx(pg)
            low = _mx(pg - hi.astype(F32))
            imp = imp + _dot(hi, cm) + _dot(low, cm)
        sel = _select_blocks_t(imp.T, i, n_top)
        neg = jnp.where(sel, 0.0, NEG).T
        mneg[g * Q_BLOCK:(g + 1) * Q_BLOCK, :] = neg.astype(mneg.dtype)
        mneg_far[g * Q_BLOCK:(g + 1) * Q_BLOCK, :] = jnp.where(blkcol < 2 * (i - 1), neg, NEG).astype(mneg.dtype)

    m_s[...] = jnp.full(m_s.shape, NEG, F32)
    acc_s[...] = jnp.zeros(acc_s.shape, F32)

    def flash_update(rs, s, v1):
        s_tiles = _lane_tiles(s)
        m_old = m_s[rs, :]
        m_new = jnp.maximum(m_old, _row_max(s_tiles))
        alpha = jnp.exp(m_old - m_new)
        p = jnp.concatenate([jnp.exp(t - m_new) for t in s_tiles], axis=1)
        acc_s[rs, :] = jnp.concatenate([alpha, alpha], axis=1) * acc_s[rs, :] + _dot(_mx(p), v1)
        m_s[rs, :] = m_new

    def far_start(j):
        return pl.multiple_of(Q_BLOCK + j * SEL_CHUNK, Q_BLOCK)

    def far_logits(j, slot):
        kc = _mx(ks_ref[0, pl.ds(far_start(j), SEL_CHUNK), :])
        madd = _dot(mneg_far[...], _block_of_key(SEL_CHUNK, j * (SEL_CHUNK // SEL_BLOCK)))
        for rs in halves:
            s_buf[slot, rs, :] = _dot_nt(qall[rs, :], kc) + jnp.concatenate([madd] * (GROUP // 2), axis=0)

    n_far_keys = jnp.maximum(i - 1, 0) * Q_BLOCK
    n_chunks = (n_far_keys + SEL_CHUNK - 1) // SEL_CHUNK

    @pl.when(n_chunks > 0)
    def _():
        far_logits(0, 0)

    def far_body(j, carry):
        v1 = _with_ones(_mx(vs_ref[0, pl.ds(far_start(j), SEL_CHUNK), :]))
        for rs in halves:
            flash_update(rs, s_buf[j % 2, rs, :], v1)
        far_logits(jnp.minimum(j + 1, n_chunks - 1), (j + 1) % 2)
        return carry

    lax.fori_loop(0, n_chunks, far_body, 0)
    kc = _mx(ks_ref[0, pl.ds(nstart, 2 * Q_BLOCK), :])
    v1 = _with_ones(_mx(vs_ref[0, pl.ds(nstart, 2 * Q_BLOCK), :]))
    madd = _dot(mneg[...], _block_of_key(2 * Q_BLOCK, 2 * (i - 1)))
    col2 = lax.broadcasted_iota(jnp.int32, (1, 2 * Q_BLOCK), 1)
    mask_n = jnp.where((col2 < Q_BLOCK) & (i == 0), NEG, 0.0)
    for rs in halves:
        s = _dot_nt(qall[rs, :], kc) + jnp.concatenate([madd] * (GROUP // 2), axis=0) + tsel_ref[rs, :] + mask_n
        flash_update(rs, s, v1)
    acc = acc_s[...]
    o_s = acc[:, :LANES] / acc[:, LANES:]
    for r in range(GROUP):
        tile = oa_acc[:, r * LANES:(r + 1) * LANES] + gate_tile(1, r) * _pair_heads(o_s, r)
        oa_ref[:, r * LANES:(r + 1) * LANES] = tile.astype(oa_ref.dtype)


def _rel_bucket_np(dist):
    n = np.maximum(dist, 0)
    max_exact = REL_BUCKETS // 2
    nf = np.maximum(n, 1).astype(np.float32)
    log_b = max_exact + (np.log(nf / max_exact) / math.log(REL_MAX_DIST / max_exact)
                         * (REL_BUCKETS - max_exact)).astype(np.int32)
    log_b = np.minimum(log_b, REL_BUCKETS - 1)
    return np.where(n < max_exact, n, log_b)


def _toeplitz_bias(tab, pad, width, window, shift_far):
    length = width + Q_BLOCK
    dist = pad + Q_BLOCK - 1 - np.arange(length)
    onehot = np.zeros((length, REL_BUCKETS), np.float32)
    onehot[np.arange(length), _rel_bucket_np(dist)] = 1.0
    vals = jnp.dot(jnp.asarray(onehot), tab, precision=lax.Precision.HIGHEST)
    if shift_far:
        vals = vals - tab[REL_BUCKETS - 1][None, :]
    valid = (dist >= 0) & (dist < window)
    vals = jnp.where(jnp.asarray(valid)[:, None], vals, NEG).T
    skew = jnp.tile(vals, (1, Q_BLOCK))[:, :Q_BLOCK * (length - 1)].reshape(N_HEADS, Q_BLOCK, length - 1)
    return skew[:, :, Q_BLOCK - 1:Q_BLOCK - 1 + width].reshape(N_HEADS * Q_BLOCK, width).astype(F32)


def _attention(proj, kvcmp, sinks, bias_table, bsz, seq):
    assert seq % SEL_CHUNK == 0
    nq = seq // Q_BLOCK
    n_far = seq // CMP_STRIDE
    n_sel = seq // SEL_BLOCK
    n_top = min(SEL_TOP_N, n_sel)
    assert n_top >= SEL_INIT_BLOCKS + SEL_LOCAL_BLOCKS and n_sel <= LANES
    wpad = Q_BLOCK * (-(-(NSA_WINDOW - 1) // Q_BLOCK))
    bpad = Q_BLOCK * (-(-(SWA_WINDOW - 1) // Q_BLOCK))
    pair = lambda tab: tab.astype(F32).reshape(REL_BUCKETS, N_GROUPS, GROUP).transpose(0, 2, 1).reshape(REL_BUCKETS, -1)
    tab_a = pair(bias_table[:, :N_HEADS])
    tab_b = pair(bias_table[:, N_HEADS:])
    near_pad = CMP_STRIDE * CMP_FRONT - (CMP_BLOCK - 1)
    t_near = _toeplitz_bias(tab_a, near_pad, CMP_STRIDE * CMP_NEAR, 1 << 30, True)[:, ::CMP_STRIDE]
    t_sel = _toeplitz_bias(tab_a, Q_BLOCK, 2 * Q_BLOCK, 1 << 30, True)
    t_win = _toeplitz_bias(tab_a, wpad, wpad + Q_BLOCK, NSA_WINDOW, False)
    t_swa = _toeplitz_bias(tab_b, bpad, bpad + Q_BLOCK, SWA_WINDOW, False)
    n_rows = kvcmp.shape[2]
    cn = (np.arange(n_rows) - CMP_FRONT)[:, None] * CMP_STRIDE
    sj = np.arange(LANES)[None, :] * SEL_BLOCK
    cmat = ((cn < sj + SEL_BLOCK) & (cn + CMP_BLOCK > sj) & (cn >= 0) & (cn + CMP_BLOCK <= seq)
            & (sj < seq)).astype(np.float32)
    cmat = jnp.asarray(cmat, F32)
    padded = lambda name, p: jnp.pad(proj[name].reshape(bsz, seq, LANES), ((0, 0), (p, 0), (0, 0)))
    ks, vs = padded('ks', Q_BLOCK), padded('vs', Q_BLOCK)
    kw, vw = padded('kw', wpad), padded('vw', wpad)
    kb, vb = padded('kb', bpad), padded('vb', bpad)
    rows = N_HEADS * Q_BLOCK
    qspec = pl.BlockSpec((Q_BLOCK, 4 * LANES), lambda b, i: (b * nq + i, 0))
    const2 = lambda shape: pl.BlockSpec(shape, lambda b, i: (0, 0))
    batch3 = lambda n: pl.BlockSpec((1, n, LANES), lambda b, i: (b, 0, 0))
    kernel = functools.partial(_attn_kernel, n_far=n_far, n_top=n_top)
    return pl.pallas_call(
        kernel,
        grid=(bsz, nq),
        in_specs=[pl.BlockSpec(memory_space=pltpu.SMEM),
                  qspec, qspec,
                  pl.BlockSpec((Q_BLOCK, LANES), lambda b, i: (b * nq + i, 0)),
                  pl.BlockSpec((1, 1, n_rows, LANES), lambda b, i: (0, b, 0, 0)),
                  pl.BlockSpec((1, 1, n_rows, LANES), lambda b, i: (1, b, 0, 0)),
                  batch3(seq + Q_BLOCK), batch3(seq + Q_BLOCK),
                  batch3(seq + wpad), batch3(seq + wpad),
                  batch3(seq + bpad), batch3(seq + bpad),
                  const2((n_rows, LANES)),
                  const2((rows, CMP_NEAR)),
                  const2((rows, 2 * Q_BLOCK)),
                  const2((rows, wpad + Q_BLOCK)),
                  const2((rows, bpad + Q_BLOCK))],
        out_specs=[qspec, qspec],
        out_shape=[jax.ShapeDtypeStruct((bsz * seq, 4 * LANES), BF16)] * 2,
        scratch_shapes=[pltpu.VMEM((rows, LANES), MXU_DTYPE),
                        pltpu.VMEM((rows, LANES), MXU_DTYPE),
                        pltpu.VMEM((N_GROUPS * Q_BLOCK, LANES), MXU_DTYPE),
                        pltpu.VMEM((N_GROUPS * Q_BLOCK, LANES), MXU_DTYPE),
                        pltpu.VMEM((rows, LANES), F32),
                        pltpu.VMEM((rows, 2 * LANES), F32),
                        pltpu.VMEM((2, rows, SEL_CHUNK), F32),
                        pltpu.VMEM((Q_BLOCK, 4 * LANES), F32)],
        compiler_params=pltpu.CompilerParams(dimension_semantics=("arbitrary", "arbitrary"),
                                             vmem_limit_bytes=VMEM_LIMIT),
        name="attention",
    )(sinks.astype(F32), proj['qa'], proj['qb'], proj['ga'], kvcmp, kvcmp, ks, vs, kw, vw, kb, vb,
      cmat, t_near, t_sel, t_win, t_swa)


def _layer_norm(y, g, b):
    mu = jnp.mean(y, axis=-1, keepdims=True)
    yc = y - mu
    var = jnp.mean(yc * yc, axis=-1, keepdims=True)
    return yc * lax.rsqrt(var + LN_EPS) * g + b


def _outproj_kernel(oa_ref, ob_ref, sg_ref, x_ref, pa_ref, pb_ref, wo_ref, g1_ref, b1_ref, wr_ref, rb_ref, sgu_ref,
                    sd_ref, tri_ref, h_ref, base_ref, eidx_ref, gate_ref, rank_ref, cnt_ref, carry):
    step = pl.program_id(0)
    tm = oa_ref.shape[0]

    @pl.when(step == 0)
    def _():
        carry[...] = jnp.zeros(carry.shape, F32)

    sg = sg_ref[...].astype(F32)
    merged = (sg[:, :D_MODEL] * _dot(_mx(oa_ref[...]), pa_ref[...])
              + sg[:, D_MODEL:] * _dot(_mx(ob_ref[...]), pb_ref[...]))
    mix = _dot(_mx(merged), wo_ref[...])
    h = _layer_norm(DN_ALPHA * x_ref[...] + mix, g1_ref[...], b1_ref[...])
    h_ref[...] = h
    hb = _mx(h)

    gu = _dot(hb, sgu_ref[...])
    shared = _dot(_mx(jax.nn.silu(gu[:, :SHARED_HIDDEN]) * gu[:, SHARED_HIDDEN:]), sd_ref[...])
    base_ref[...] = DN_ALPHA * h + shared

    scores = jax.nn.sigmoid(_dot_nt(wr_ref[...], hb))
    choice = scores + rb_ref[:, 0:1]
    per_group = N_EXPERTS // N_EXPERT_GROUPS
    gs = []
    for g in range(N_EXPERT_GROUPS):
        cg = choice[g * per_group:(g + 1) * per_group]
        m1 = jnp.max(cg, axis=0, keepdims=True)
        is_m = cg == m1
        n_m = jnp.sum(is_m.astype(F32), axis=0, keepdims=True)
        m2 = jnp.max(jnp.where(is_m, -jnp.inf, cg), axis=0, keepdims=True)
        gs.append(m1 + jnp.where(n_m > 1.5, m1, m2))
    gs = jnp.concatenate(gs, axis=0)
    gid = lax.broadcasted_iota(jnp.int32, gs.shape, 0)
    beaten = jnp.zeros(gs.shape, jnp.int32)
    for g in range(N_EXPERT_GROUPS):
        other = gs[g:g + 1]
        beaten = beaten + ((other > gs) | ((other == gs) & (g < gid))).astype(jnp.int32)
    keep_g = beaten < TOPK_EXPERT_GROUPS
    keep = jnp.concatenate([jnp.broadcast_to(keep_g[g:g + 1], (per_group, tm)) for g in range(N_EXPERT_GROUPS)],
                           axis=0)
    cand = jnp.where(keep, choice, -jnp.inf)
    eid = lax.broadcasted_iota(jnp.int32, cand.shape, 0)
    hits = []
    e_rows = []
    w_rows = []
    for _ in range(TOP_K):
        m = jnp.max(cand, axis=0, keepdims=True)
        idx = jnp.min(jnp.where(cand == m, eid, N_EXPERTS), axis=0, keepdims=True)
        hit = eid == idx
        hits.append(hit)
        e_rows.append(idx)
        w_rows.append(jnp.sum(jnp.where(hit, scores, 0.0), axis=0, keepdims=True))
        cand = jnp.where(hit, -jnp.inf, cand)
    w = jnp.concatenate(w_rows, axis=0)
    gate_ref[...] = w / jnp.sum(w, axis=0, keepdims=True) * ROUTED_SCALE
    eidx_ref[...] = jnp.concatenate(e_rows, axis=0)

    onehot = jnp.zeros(cand.shape, F32)
    for hit in hits:
        onehot = onehot + hit.astype(F32)
    before = _dot(onehot.astype(BF16), tri_ref[...]) + carry[:, 0:1]
    rank_ref[...] = jnp.concatenate(
        [jnp.sum(jnp.where(hit, before, 0.0), axis=0, keepdims=True) for hit in hits], axis=0).astype(jnp.int32)
    carry[...] = carry[...] + jnp.sum(onehot, axis=1, keepdims=True)
    cnt_ref[...] = carry[...]


def _out_projection(oa, ob, sg, x2, proj_a, proj_b, w_out, ln_g, ln_b, w_router, router_bias, s_gate, s_up, s_down):
    t = x2.shape[0]
    tm = OUT_TM
    pair_rows = lambda p: p.reshape(N_GROUPS, GROUP, HEAD_DIM, -1).transpose(1, 0, 2, 3).reshape(p.shape)
    pa = pair_rows(proj_a).astype(MXU_DTYPE)
    pb = pair_rows(proj_b).astype(MXU_DTYPE)
    tri = jnp.asarray(np.triu(np.ones((tm, tm), np.float32), 1), BF16)
    row = lambda i: (i, 0)
    fixed = lambda i: (0, 0)
    col = lambda i: (0, i)
    outs = pl.pallas_call(
        _outproj_kernel,
        grid=(t // tm,),
        in_specs=[pl.BlockSpec((tm, 4 * LANES), row), pl.BlockSpec((tm, 4 * LANES), row),
                  pl.BlockSpec((tm, 2 * D_MODEL), row), pl.BlockSpec((tm, D_MODEL), row),
                  pl.BlockSpec((4 * LANES, D_MODEL), fixed), pl.BlockSpec((4 * LANES, D_MODEL), fixed),
                  pl.BlockSpec((D_MODEL, D_MODEL), fixed),
                  pl.BlockSpec((1, D_MODEL), fixed), pl.BlockSpec((1, D_MODEL), fixed),
                  pl.BlockSpec((N_EXPERTS, D_MODEL), fixed), pl.BlockSpec((N_EXPERTS, LANES), fixed),
                  pl.BlockSpec((D_MODEL, 2 * SHARED_HIDDEN), fixed), pl.BlockSpec((SHARED_HIDDEN, D_MODEL), fixed),
                  pl.BlockSpec((tm, tm), fixed)],
        out_specs=[pl.BlockSpec((tm, D_MODEL), row), pl.BlockSpec((tm, D_MODEL), row),
                   pl.BlockSpec((TOP_K, tm), col), pl.BlockSpec((TOP_K, tm), col), pl.BlockSpec((TOP_K, tm), col),
                   pl.BlockSpec((N_EXPERTS, LANES), fixed)],
        out_shape=[jax.ShapeDtypeStruct((t, D_MODEL), F32), jax.ShapeDtypeStruct((t, D_MODEL), F32),
                   jax.ShapeDtypeStruct((TOP_K, t), jnp.int32), jax.ShapeDtypeStruct((TOP_K, t), F32),
                   jax.ShapeDtypeStruct((TOP_K, t), jnp.int32), jax.ShapeDtypeStruct((N_EXPERTS, LANES), F32)],
        scratch_shapes=[pltpu.VMEM((N_EXPERTS, LANES), F32)],
        compiler_params=pltpu.CompilerParams(dimension_semantics=("arbitrary",), vmem_limit_bytes=VMEM_LIMIT),
        name="out_projection_router",
    )(oa, ob, sg, x2, pa, pb, w_out.astype(MXU_DTYPE), ln_g.reshape(1, -1), ln_b.reshape(1, -1),
      w_router.T.astype(MXU_DTYPE), jnp.broadcast_to(router_bias.astype(F32)[:, None], (N_EXPERTS, LANES)),
      jnp.concatenate([s_gate, s_up], axis=1).astype(MXU_DTYPE), s_down.astype(MXU_DTYPE), tri)
    return outs


def _dispatch_kernel(zstart_ref, cnt_ref, dest_ref, h_ref, xs_ref, zeros, sem):
    step = pl.program_id(0)
    tm = h_ref.shape[0]

    @pl.when(step == 0)
    def _():
        zeros[...] = jnp.zeros(zeros.shape, F32)

        def fill(e, c):
            @pl.when(cnt_ref[e] > 0)
            def _():
                cp = pltpu.make_async_copy(zeros, xs_ref.at[pl.ds(zstart_ref[e], MOE_BM)], sem)
                cp.start()
                cp.wait()
            return c
        lax.fori_loop(0, N_EXPERTS, fill, 0)

    def issue(t, c):
        for k in range(TOP_K):
            pltpu.make_async_copy(h_ref.at[t], xs_ref.at[dest_ref[k, t]], sem).start(priority=k % 2)
        return c
    lax.fori_loop(0, tm, issue, 0)
    for k in range(TOP_K):
        pltpu.make_async_copy(h_ref, xs_ref.at[pl.ds(0, tm)], sem).wait()


def _dispatch(h3, dest, zstart, counts, n_rows):
    t = h3.shape[0]
    tm = DISP_TM
    return pl.pallas_call(
        _dispatch_kernel,
        grid_spec=pltpu.PrefetchScalarGridSpec(
            num_scalar_prefetch=2,
            grid=(t // tm,),
            in_specs=[pl.BlockSpec((TOP_K, tm), lambda i, *_: (0, i), memory_space=pltpu.SMEM),
                      pl.BlockSpec((tm,) + ROW_TILE, lambda i, *_: (i, 0, 0))],
            out_specs=pl.BlockSpec(memory_space=pl.ANY),
            scratch_shapes=[pltpu.VMEM((MOE_BM,) + ROW_TILE, F32), pltpu.SemaphoreType.DMA(())]),
        out_shape=jax.ShapeDtypeStruct((n_rows,) + ROW_TILE, F32),
        compiler_params=pltpu.CompilerParams(dimension_semantics=("arbitrary",), vmem_limit_bytes=VMEM_LIMIT),
        name="moe_dispatch",
    )(zstart, counts, dest, h3)


def _experts_kernel(blk_e_ref, nused_ref, xs_ref, wg_ref, wu_ref, wd_ref, ys_ref, wg_s, wu_s, wd_s):
    b = pl.program_id(0)
    prev = blk_e_ref[jnp.maximum(b - 1, 0)]

    @pl.when((b == 0) | (blk_e_ref[b] != prev))
    def _():
        wg_s[...] = _mx(wg_ref[0])
        wu_s[...] = _mx(wu_ref[0])
        wd_s[...] = _mx(wd_ref[0])

    n_tiles = D_MODEL // LANES

    @pl.when(b < nused_ref[0])
    def _():
        xb = _mx(jnp.concatenate([xs_ref[:, c, :] for c in range(n_tiles)], axis=1))
        hid = jax.nn.silu(_dot(xb, wg_s[...])) * _dot(xb, wu_s[...])
        y = _dot(_mx(hid), wd_s[...])
        for c in range(n_tiles):
            ys_ref[:, c, :] = y[:, c * LANES:(c + 1) * LANES]

    @pl.when(b >= nused_ref[0])
    def _():
        ys_ref[...] = jnp.zeros(ys_ref.shape, F32)


def _experts(xs, blk_e, nused, e_gate, e_up, e_down):
    n_rows = xs.shape[0]
    n_blocks = n_rows // MOE_BM
    xmap = lambda b, be, nu: (jnp.minimum(b, nu[0] - 1), 0, 0)
    wmap = lambda b, be, nu: (be[b], 0, 0)
    return pl.pallas_call(
        _experts_kernel,
        grid_spec=pltpu.PrefetchScalarGridSpec(
            num_scalar_prefetch=2,
            grid=(n_blocks,),
            in_specs=[pl.BlockSpec((MOE_BM,) + ROW_TILE, xmap),
                      pl.BlockSpec((1, D_MODEL, EXPERT_HIDDEN), wmap),
                      pl.BlockSpec((1, D_MODEL, EXPERT_HIDDEN), wmap),
                      pl.BlockSpec((1, EXPERT_HIDDEN, D_MODEL), wmap)],
            out_specs=pl.BlockSpec((MOE_BM,) + ROW_TILE, lambda b, be, nu: (b, 0, 0)),
            scratch_shapes=[pltpu.VMEM((D_MODEL, EXPERT_HIDDEN), MXU_DTYPE),
                            pltpu.VMEM((D_MODEL, EXPERT_HIDDEN), MXU_DTYPE),
                            pltpu.VMEM((EXPERT_HIDDEN, D_MODEL), MXU_DTYPE)]),
        out_shape=jax.ShapeDtypeStruct((n_rows,) + ROW_TILE, F32),
        compiler_params=pltpu.CompilerParams(dimension_semantics=("arbitrary",), vmem_limit_bytes=VMEM_LIMIT),
        name="moe_experts",
    )(blk_e, nused, xs, e_gate, e_up, e_down)


def _combine_kernel(dest_ref, gate_ref, base_ref, g2_ref, b2_ref, ys_ref, out_ref, buf, sem):
    tm = base_ref.shape[0]

    def issue(t, c):
        for k in range(TOP_K):
            pltpu.make_async_copy(ys_ref.at[dest_ref[k, t]], buf.at[k, t], sem).start(priority=k % 2)
        return c
    lax.fori_loop(0, tm, issue, 0)
    for k in range(TOP_K):
        pltpu.make_async_copy(ys_ref.at[pl.ds(0, tm)], buf.at[k], sem).wait()

    def token(t, c):
        y = base_ref[t]
        for k in range(TOP_K):
            y = y + gate_ref[k, t] * buf[k, t]
        out_ref[t] = y
        return c
    lax.fori_loop(0, tm, token, 0)
    y = out_ref[...]
    n = float(D_MODEL)
    mu = jnp.sum(jnp.sum(y, axis=2, keepdims=True), axis=1, keepdims=True) / n
    yc = y - mu
    var = jnp.sum(jnp.sum(yc * yc, axis=2, keepdims=True), axis=1, keepdims=True) / n
    out_ref[...] = yc * lax.rsqrt(var + LN_EPS) * g2_ref[...] + b2_ref[...]


def _combine(ys3, dest, gate, base3, ln_g, ln_b):
    t = base3.shape[0]
    tm = COMB_TM
    return pl.pallas_call(
        _combine_kernel,
        grid=(t // tm,),
        in_specs=[pl.BlockSpec((TOP_K, tm), lambda i: (0, i), memory_space=pltpu.SMEM),
                  pl.BlockSpec((TOP_K, tm), lambda i: (0, i), memory_space=pltpu.SMEM),
                  pl.BlockSpec((tm,) + ROW_TILE, lambda i: (i, 0, 0)),
                  pl.BlockSpec(ROW_TILE, lambda i: (0, 0)),
                  pl.BlockSpec(ROW_TILE, lambda i: (0, 0)),
                  pl.BlockSpec(memory_space=pl.ANY)],
        out_specs=pl.BlockSpec((tm,) + ROW_TILE, lambda i: (i, 0, 0)),
        out_shape=jax.ShapeDtypeStruct((t,) + ROW_TILE, F32),
        scratch_shapes=[pltpu.VMEM((TOP_K, tm) + ROW_TILE, F32), pltpu.SemaphoreType.DMA(())],
        compiler_params=pltpu.CompilerParams(dimension_semantics=("arbitrary",), vmem_limit_bytes=VMEM_LIMIT),
        name="moe_combine",
    )(dest, gate, base3, ln_g.reshape(ROW_TILE), ln_b.reshape(ROW_TILE), ys3)


def _moe_layout(eidx, rank, counts):
    n_assign = eidx.size
    n_blocks = (n_assign + N_EXPERTS * (MOE_BM - 1)) // MOE_BM
    padded = (counts + MOE_BM - 1) // MOE_BM * MOE_BM
    pends = jnp.cumsum(padded)
    pstarts = pends - padded
    experts = jnp.arange(N_EXPERTS, dtype=jnp.int32)
    dest = jnp.sum(jnp.where(eidx[..., None] == experts, pstarts, 0), axis=-1) + rank
    block_row = jnp.arange(n_blocks, dtype=jnp.int32) * MOE_BM
    blk_e = jnp.minimum(jnp.sum(pends[None, :] <= block_row[:, None], axis=1), N_EXPERTS - 1).astype(jnp.int32)
    nused = (pends[-1:] // MOE_BM).astype(jnp.int32)
    zstart = jnp.maximum(pends - MOE_BM, 0).astype(jnp.int32)
    return dest.astype(jnp.int32), blk_e, nused, zstart, n_blocks * MOE_BM


def _layer(x, w_in, cmp_pe, cmp_w1, cmp_b1, cmp_w2, sinks, bias_table, proj_a, proj_b, w_out, ln1_g, ln1_b,
           w_router, router_bias, e_gate, e_up, e_down, s_gate, s_up, s_down, ln2_g, ln2_b):
    bsz, seq, d = x.shape
    x2 = x.reshape(bsz * seq, d)
    proj = _in_projection(x2, w_in)
    kvcmp = _compress(proj['kc'], proj['vc'], bsz, seq, cmp_pe, cmp_w1, cmp_b1, cmp_w2)
    oa, ob = _attention(proj, kvcmp, sinks, bias_table, bsz, seq)
    h, base, eidx, gate, rank, cnt = _out_projection(oa, ob, proj['sg'], x2, proj_a, proj_b, w_out, ln1_g, ln1_b,
                                                     w_router, router_bias, s_gate, s_up, s_down)
    counts = cnt[:, 0].astype(jnp.int32)
    dest, blk_e, nused, zstart, n_rows = _moe_layout(eidx, rank, counts)
    xs = _dispatch(h.reshape((-1,) + ROW_TILE), dest, zstart, counts, n_rows)
    ys = _experts(xs, blk_e, nused, e_gate, e_up, e_down)
    out = _combine(ys, dest, gate, base.reshape((-1,) + ROW_TILE), ln2_g, ln2_b)
    return out.reshape(bsz, seq, d)


def kernel(x, w_in, cmp_pe, cmp_w1, cmp_b1, cmp_w2, attn_sinks, rel_bias_table, proj_a, proj_b, w_out, ln1_g, ln1_b,
           w_router, router_bias, expert_w_gate, expert_w_up, expert_w_down, shared_w_gate, shared_w_up,
           shared_w_down, ln2_g, ln2_b):
    h = x
    for l in range(DEPTH):
        h = _layer(h, w_in[l], cmp_pe[l], cmp_w1[l], cmp_b1[l], cmp_w2[l], attn_sinks[l], rel_bias_table, proj_a[l],
                   proj_b[l], w_out[l], ln1_g[l], ln1_b[l], w_router[l], router_bias[l], expert_w_gate[l],
                   expert_w_up[l], expert_w_down[l], shared_w_gate[l], shared_w_up[l], shared_w_down[l], ln2_g[l],
                   ln2_b[l])
    return h
```

```python
import functools
import math

import numpy as np
import jax
import jax.numpy as jnp
from jax import lax
from jax.experimental import pallas as pl
from jax.experimental.pallas import tpu as pltpu

F32 = jnp.float32
BF16 = jnp.bfloat16
MXU_DTYPE = jnp.bfloat16

D_MODEL = 1024
HEAD_DIM = 64
ATTN_SCALE = HEAD_DIM ** -0.5
Q_BLOCK = 128
N_HEADS = 8
N_GROUPS = 2
GROUP = 4
CMP_BLOCK = 32
CMP_STRIDE = 16
CMP_HIDDEN = 128
SEL_BLOCK = 64
SEL_TOP_N = 8
SEL_INIT_BLOCKS = 1
SEL_LOCAL_BLOCKS = 2
NSA_WINDOW = 512
SWA_WINDOW = 128
REL_BUCKETS = 32
REL_MAX_DIST = 128
N_EXPERTS = 256
TOP_K = 8
EXPERT_HIDDEN = 256
SHARED_HIDDEN = 256
N_EXPERT_GROUPS = 8
TOPK_EXPERT_GROUPS = 4
ROUTED_SCALE = 2.5
LN_EPS = 1e-5
DEPTH = 1
DN_ALPHA = (2 * DEPTH) ** 0.25

NEG = -1e30
LANES = 128
ROW_TILE = (8, LANES)
CMP_FRONT = 16
CMP_NEAR = LANES
SEL_CHUNK = 512
VMEM_LIMIT = 56 * 1024 * 1024

IN_TM = 512
OUT_TM = 256
MOE_BM = 256
DISP_TM = 256
COMB_TM = 128


def _dot(a, b):
    return jnp.dot(a, b, preferred_element_type=F32)


def _dot_nt(a, b):
    return lax.dot_general(a, b, (((1,), (1,)), ((), ())), preferred_element_type=F32)


def _mx(a):
    return a.astype(MXU_DTYPE)


_IN_COLS = (('qa', 512), ('qb', 512), ('kc', 128), ('vc', 128), ('ks', 128), ('vs', 128), ('kw', 128),
            ('vw', 128), ('kb', 128), ('vb', 128), ('ga', 128), ('sg', 2048))


def _inproj_kernel(x_ref, w_ref, qa_ref, qb_ref, kc_ref, vc_ref, ks_ref, vs_ref, kw_ref, vw_ref, kb_ref, vb_ref,
                   ga_ref, sg_ref):
    xb = _mx(x_ref[...])
    outs = dict(qa=qa_ref, qb=qb_ref, kc=kc_ref, vc=vc_ref, ks=ks_ref, vs=vs_ref, kw=kw_ref, vw=vw_ref,
                kb=kb_ref, vb=vb_ref, ga=ga_ref, sg=sg_ref)
    off = 0
    for name, width in _IN_COLS:
        for c0 in range(0, width, 512):
            cw = min(512, width - c0)
            y = _dot(xb, w_ref[:, off + c0:off + c0 + cw])
            if name in ('ga', 'sg'):
                y = jax.nn.sigmoid(y)
            outs[name][:, c0:c0 + cw] = y.astype(outs[name].dtype)
        off += width


def _pair_head_columns(w):
    return w.reshape(w.shape[0], N_GROUPS, GROUP, HEAD_DIM).transpose(0, 2, 1, 3).reshape(w.shape[0], -1)


def _in_projection(x2, w_in):
    t = x2.shape[0]
    sizes = (512, 128, 128, 128, 128, 128, 128, 24, 512, 128, 128, 1024, 1024)
    offs = np.cumsum((0,) + sizes)
    part = [w_in[:, offs[k]:offs[k + 1]] for k in range(len(sizes))]
    w_qa, w_kc, w_vc, w_ks, w_vs, w_kw, w_vw, w_g, w_qb, w_kb, w_vb, w_gate_a, w_gate_b = part
    w_qa = _pair_head_columns(w_qa) * ATTN_SCALE
    w_qb = _pair_head_columns(w_qb) * ATTN_SCALE
    w_ga = w_g.reshape(-1, N_GROUPS, GROUP, 3).transpose(0, 3, 2, 1).reshape(-1, 24)
    w_ga = jnp.pad(w_ga, ((0, 0), (0, LANES - 24)))
    w_all = jnp.concatenate([w_qa, w_qb, w_kc, w_vc, w_ks, w_vs, w_kw, w_vw, w_kb, w_vb, w_ga, w_gate_a, w_gate_b],
                            axis=1).astype(MXU_DTYPE)
    n_all = w_all.shape[1]
    out_shape = []
    out_specs = []
    for name, width in _IN_COLS:
        dt = F32 if name == 'ga' else BF16
        out_shape.append(jax.ShapeDtypeStruct((t, width), dt))
        out_specs.append(pl.BlockSpec((IN_TM, width), lambda i: (i, 0)))
    outs = pl.pallas_call(
        _inproj_kernel,
        grid=(t // IN_TM,),
        in_specs=[pl.BlockSpec((IN_TM, D_MODEL), lambda i: (i, 0)),
                  pl.BlockSpec((D_MODEL, n_all), lambda i: (0, 0))],
        out_specs=out_specs,
        out_shape=out_shape,
        compiler_params=pltpu.CompilerParams(dimension_semantics=("arbitrary",), vmem_limit_bytes=VMEM_LIMIT),
        name="in_projection",
    )(x2, w_all)
    return dict(zip([n for n, _ in _IN_COLS], outs))


def _compress_kernel(tok_ref, w1_ref, pe_ref, w1o_ref, b1_ref, w2_ref, out_ref):
    n_chunks = tok_ref.shape[2]
    ab = _dot(tok_ref[0, 0], w1_ref[0])
    a = ab[:, :2 * CMP_HIDDEN]
    b_next = pltpu.roll(ab[:, 2 * CMP_HIDDEN:], n_chunks - 1, 0)
    cb = _dot(_mx(pe_ref[0]), _mx(w1o_ref[0]))[0:1, :] + b1_ref[0]
    cb2 = jnp.concatenate([cb, cb], axis=1)
    hid = jax.nn.gelu(a + b_next + cb2)
    out = _dot(_mx(hid), w2_ref[0])
    row = lax.broadcasted_iota(jnp.int32, out.shape, 0)
    out = jnp.where(row < n_chunks - 1, out, 0.0)
    out_ref[0, 0, 0:CMP_FRONT, :] = jnp.zeros((CMP_FRONT, LANES), F32)
    out_ref[0, 0, CMP_FRONT:CMP_FRONT + n_chunks, :] = out
    out_ref[0, 0, CMP_FRONT + n_chunks:, :] = jnp.zeros((CMP_NEAR - CMP_FRONT, LANES), F32)


def _compress(kc, vc, bsz, seq, cmp_pe, cmp_w1, cmp_b1, cmp_w2):
    n_chunks = seq // CMP_STRIDE
    tok = jnp.stack([kc, vc]).reshape(2, bsz, n_chunks, CMP_STRIDE * LANES)
    eye = jnp.eye(N_GROUPS, dtype=F32)
    w1r = cmp_w1.reshape(2, 2, CMP_STRIDE, HEAD_DIM, CMP_HIDDEN)
    w1 = jnp.einsum('khjdn,gG->kjgdhGn', w1r, eye).reshape(2, CMP_STRIDE * LANES, 4 * CMP_HIDDEN).astype(MXU_DTYPE)
    w2 = jnp.einsum('knd,gG->kgnGd', cmp_w2, eye).reshape(2, 2 * CMP_HIDDEN, LANES).astype(MXU_DTYPE)
    pe = jnp.pad(cmp_pe.reshape(2, 1, CMP_BLOCK * HEAD_DIM), ((0, 0), (0, 7), (0, 0)))
    b1 = cmp_b1.reshape(2, 1, CMP_HIDDEN)
    rows = CMP_FRONT + n_chunks + CMP_NEAR - CMP_FRONT
    return pl.pallas_call(
        _compress_kernel,
        grid=(2, bsz),
        in_specs=[pl.BlockSpec((1, 1, n_chunks, CMP_STRIDE * LANES), lambda k, b: (k, b, 0, 0)),
                  pl.BlockSpec((1, CMP_STRIDE * LANES, 4 * CMP_HIDDEN), lambda k, b: (k, 0, 0)),
                  pl.BlockSpec((1, 8, CMP_BLOCK * HEAD_DIM), lambda k, b: (k, 0, 0)),
                  pl.BlockSpec((1, CMP_BLOCK * HEAD_DIM, CMP_HIDDEN), lambda k, b: (k, 0, 0)),
                  pl.BlockSpec((1, 1, CMP_HIDDEN), lambda k, b: (k, 0, 0)),
                  pl.BlockSpec((1, 2 * CMP_HIDDEN, LANES), lambda k, b: (k, 0, 0))],
        out_specs=pl.BlockSpec((1, 1, rows, LANES), lambda k, b: (k, b, 0, 0)),
        out_shape=jax.ShapeDtypeStruct((2, bsz, rows, LANES), F32),
        compiler_params=pltpu.CompilerParams(dimension_semantics=("arbitrary", "arbitrary"),
                                             vmem_limit_bytes=VMEM_LIMIT),
        name="nsa_compress",
    )(tok, w1, pe, cmp_w1, b1, w2)


def _stack_heads(q_ref, dst):
    lo = lax.broadcasted_iota(jnp.int32, (Q_BLOCK, LANES), 1) < HEAD_DIM
    for r in range(GROUP):
        qr = q_ref[:, r * LANES:(r + 1) * LANES].astype(dst.dtype)
        z = jnp.zeros_like(qr)
        dst[(2 * r) * Q_BLOCK:(2 * r + 1) * Q_BLOCK, :] = jnp.where(lo, qr, z)
        dst[(2 * r + 1) * Q_BLOCK:(2 * r + 2) * Q_BLOCK, :] = jnp.where(lo, z, qr)


def _pair_heads(o, r):
    lo = lax.broadcasted_iota(jnp.int32, (Q_BLOCK, LANES), 1) < HEAD_DIM
    return jnp.where(lo, o[(2 * r) * Q_BLOCK:(2 * r + 1) * Q_BLOCK], o[(2 * r + 1) * Q_BLOCK:(2 * r + 2) * Q_BLOCK])


def _lane_tiles(x):
    return [x[:, t * LANES:(t + 1) * LANES] for t in range(x.shape[1] // LANES)]


def _row_max(tiles):
    mx = tiles[0]
    for t in tiles[1:]:
        mx = jnp.maximum(mx, t)
    return jnp.broadcast_to(jnp.max(mx, axis=1, keepdims=True), mx.shape)


def _with_ones(v):
    return jnp.concatenate([v, jnp.ones(v.shape, v.dtype)], axis=1)


def _block_of_key(n_keys, first_block):
    b = lax.broadcasted_iota(jnp.int32, (LANES, n_keys), 0)
    k = lax.broadcasted_iota(jnp.int32, (LANES, n_keys), 1)
    return (b == (k // SEL_BLOCK) + first_block).astype(MXU_DTYPE)


def _select_blocks_t(imp_t, i, n_top):
    blk = lax.broadcasted_iota(jnp.int32, imp_t.shape, 0)
    qcol = lax.broadcasted_iota(jnp.int32, imp_t.shape, 1)
    back = (2 * i + (qcol >= SEL_BLOCK).astype(jnp.int32)) - blk
    sel = (back >= 0) & ((blk < SEL_INIT_BLOCKS) | (back < SEL_LOCAL_BLOCKS))
    cand = jnp.where((back >= SEL_LOCAL_BLOCKS) & (blk >= SEL_INIT_BLOCKS), imp_t, -1.0)
    blk_f = blk.astype(F32)
    for _ in range(n_top - SEL_INIT_BLOCKS - SEL_LOCAL_BLOCKS):
        m = jnp.max(cand, axis=0, keepdims=True)
        idx = jnp.min(jnp.where(cand == m, blk_f, float(LANES)), axis=0, keepdims=True)
        hit = blk_f == idx
        sel = sel | (hit & (m >= 0.0))
        cand = jnp.where(hit, -2.0, cand)
    return sel


def _attn_kernel(sink_ref, qa_ref, qb_ref, ga_ref, kcmp_ref, vcmp_ref, ks_ref, vs_ref, kw_ref, vw_ref, kb_ref,
                 vb_ref, cmat_ref, tnear_ref, tsel_ref, twin_ref, tswa_ref, oa_ref, ob_ref,
                 qall, qball, mneg, mneg_far, m_s, acc_s, s_buf, oa_acc, *, n_far, n_top):
    i = pl.program_id(1)
    rows = N_HEADS * Q_BLOCK
    half = rows // 2
    halves = (slice(0, half), slice(half, rows))
    _stack_heads(qa_ref, qall)
    _stack_heads(qb_ref, qball)
    nstart = pl.multiple_of(i * Q_BLOCK, Q_BLOCK)
    lo = lax.broadcasted_iota(jnp.int32, (Q_BLOCK, LANES), 1) < HEAD_DIM
    gates = ga_ref[...]

    def gate_tile(c, r):
        return jnp.where(lo, gates[:, c * 8 + 2 * r:c * 8 + 2 * r + 1], gates[:, c * 8 + 2 * r + 1:c * 8 + 2 * r + 2])

    def softmax_pv(s_tiles, v1, fix_max=None):
        m = _row_max(s_tiles)
        if fix_max is not None:
            m = fix_max(m)
        e = [jnp.exp(t - m) for t in s_tiles]
        return e, m, _dot(_mx(jnp.concatenate(e, axis=1)), v1)

    off = pl.multiple_of(i * (Q_BLOCK // CMP_STRIDE), 8)
    k_cmp = _mx(jnp.concatenate([kcmp_ref[0, 0, 0:n_far, :], kcmp_ref[0, 0, pl.ds(off, CMP_NEAR), :]], axis=0))
    v_cmp = _with_ones(_mx(jnp.concatenate([vcmp_ref[0, 0, 0:n_far, :], vcmp_ref[0, 0, pl.ds(off, CMP_NEAR), :]],
                                           axis=0)))
    colf = lax.broadcasted_iota(jnp.int32, (1, n_far), 1)
    coln = lax.broadcasted_iota(jnp.int32, (1, CMP_NEAR), 1)
    col_ok = jnp.concatenate([(colf >= CMP_FRONT) & (colf < off), coln + off >= CMP_FRONT], axis=1)
    mask_c = jnp.where(col_ok, 0.0, NEG)
    no_key = lambda m: jnp.where(m > 0.5 * NEG, m, 0.0)
    p_cmp, o_c = [], []
    for rs in halves:
        tiles = _lane_tiles(_dot_nt(qall[rs, :], k_cmp) + mask_c)
        tiles[-1] = tiles[-1] + tnear_ref[rs, :]
        e, _, ov = softmax_pv(tiles, v_cmp, no_key)
        inv = 1.0 / jnp.maximum(ov[:, LANES:], 1e-30)
        o_c.append(ov[:, :LANES] * inv)
        p_cmp.append([t * inv for t in e])
    o_c = jnp.concatenate(o_c, axis=0)

    wpad = kw_ref.shape[1] - ks_ref.shape[1] + Q_BLOCK
    kwin = _mx(kw_ref[0, pl.ds(nstart, wpad + Q_BLOCK), :])
    vwin = _with_ones(_mx(vw_ref[0, pl.ds(nstart, wpad + Q_BLOCK), :]))
    colw = lax.broadcasted_iota(jnp.int32, (1, wpad + Q_BLOCK), 1)
    mask_w = jnp.where(colw + nstart >= wpad, 0.0, NEG)
    o_w = []
    for rs in halves:
        _, _, ov = softmax_pv(_lane_tiles(_dot_nt(qall[rs, :], kwin) + twin_ref[rs, :] + mask_w), vwin)
        o_w.append(ov[:, :LANES] / ov[:, LANES:])
    o_w = jnp.concatenate(o_w, axis=0)
    for r in range(GROUP):
        oa_acc[:, r * LANES:(r + 1) * LANES] = (gate_tile(0, r) * _pair_heads(o_c, r)
                                                + gate_tile(2, r) * _pair_heads(o_w, r))

    bpad = kb_ref.shape[1] - ks_ref.shape[1] + Q_BLOCK
    kwin = _mx(kb_ref[0, pl.ds(nstart, bpad + Q_BLOCK), :])
    vwin = _with_ones(_mx(vb_ref[0, pl.ds(nstart, bpad + Q_BLOCK), :]))
    colb = lax.broadcasted_iota(jnp.int32, (1, bpad + Q_BLOCK), 1)
    mask_b = jnp.where(colb + nstart >= bpad, 0.0, NEG)
    o_b = []
    for hh, rs in enumerate(halves):
        sink = jnp.concatenate([jnp.full((Q_BLOCK, LANES), sink_ref[(h % 2) * GROUP + h // 2], F32)
                                for h in range(hh * N_HEADS // 2, (hh + 1) * N_HEADS // 2)], axis=0)
        _, m, ov = softmax_pv(_lane_tiles(_dot_nt(qball[rs, :], kwin) + tswa_ref[rs, :] + mask_b), vwin,
                              lambda m: jnp.maximum(m, sink))
        o_b.append(ov[:, :LANES] / (ov[:, LANES:] + jnp.exp(sink - m)))
    o_b = jnp.concatenate(o_b, axis=0)
    for r in range(GROUP):
        ob_ref[:, r * LANES:(r + 1) * LANES] = _pair_heads(o_b, r).astype(ob_ref.dtype)

    blkcol = lax.broadcasted_iota(jnp.int32, (Q_BLOCK, LANES), 1)
    n_tiles = len(p_cmp[0])
    for g in range(N_GROUPS):
        imp = jnp.zeros((Q_BLOCK, LANES), F32)
        for t in range(n_tiles):
            pg = sum(p_cmp[r // 2][t][(2 * (r % 2) + g) * Q_BLOCK:(2 * (r % 2) + g + 1) * Q_BLOCK]
                     for r in range(GROUP))
            if t < n_tiles - 1:
                cm = _mx(cmat_ref[t * LANES:(t + 1) * LANES, :])
            else:
                cm = _mx(cmat_ref[pl.ds(off, CMP_NEAR), :])
            hi = _mx(pg)
            low = _mx(pg - hi.astype(F32))
            imp = imp + _dot(hi, cm) + _dot(low, cm)
        sel = _select_blocks_t(imp.T, i, n_top)
        neg = jnp.where(sel, 0.0, NEG).T
        mneg[g * Q_BLOCK:(g + 1) * Q_BLOCK, :] = neg.astype(mneg.dtype)
        mneg_far[g * Q_BLOCK:(g + 1) * Q_BLOCK, :] = jnp.where(blkcol < 2 * (i - 1), neg, NEG).astype(mneg.dtype)

    m_s[...] = jnp.full(m_s.shape, NEG, F32)
    acc_s[...] = jnp.zeros(acc_s.shape, F32)

    def flash_update(rs, s, v1):
        s_tiles = _lane_tiles(s)
        m_old = m_s[rs, :]
        m_new = jnp.maximum(m_old, _row_max(s_tiles))
        alpha = jnp.exp(m_old - m_new)
        p = jnp.concatenate([jnp.exp(t - m_new) for t in s_tiles], axis=1)
        acc_s[rs, :] = jnp.concatenate([alpha, alpha], axis=1) * acc_s[rs, :] + _dot(_mx(p), v1)
        m_s[rs, :] = m_new

    def far_start(j):
        return pl.multiple_of(Q_BLOCK + j * SEL_CHUNK, Q_BLOCK)

    def far_logits(j, slot):
        kc = _mx(ks_ref[0, pl.ds(far_start(j), SEL_CHUNK), :])
        madd = _dot(mneg_far[...], _block_of_key(SEL_CHUNK, j * (SEL_CHUNK // SEL_BLOCK)))
        for rs in halves:
            s_buf[slot, rs, :] = _dot_nt(qall[rs, :], kc) + jnp.concatenate([madd] * (GROUP // 2), axis=0)

    n_far_keys = jnp.maximum(i - 1, 0) * Q_BLOCK
    n_chunks = (n_far_keys + SEL_CHUNK - 1) // SEL_CHUNK

    @pl.when(n_chunks > 0)
    def _():
        far_logits(0, 0)

    def far_body(j, carry):
        v1 = _with_ones(_mx(vs_ref[0, pl.ds(far_start(j), SEL_CHUNK), :]))
        for rs in halves:
            flash_update(rs, s_buf[j % 2, rs, :], v1)
        far_logits(jnp.minimum(j + 1, n_chunks - 1), (j + 1) % 2)
        return carry

    lax.fori_loop(0, n_chunks, far_body, 0)
    kc = _mx(ks_ref[0, pl.ds(nstart, 2 * Q_BLOCK), :])
    v1 = _with_ones(_mx(vs_ref[0, pl.ds(nstart, 2 * Q_BLOCK), :]))
    madd = _dot(mneg[...], _block_of_key(2 * Q_BLOCK, 2 * (i - 1)))
    col2 = lax.broadcasted_iota(jnp.int32, (1, 2 * Q_BLOCK), 1)
    mask_n = jnp.where((col2 < Q_BLOCK) & (i == 0), NEG, 0.0)
    for rs in halves:
        s = _dot_nt(qall[rs, :], kc) + jnp.concatenate([madd] * (GROUP // 2), axis=0) + tsel_ref[rs, :] + mask_n
        flash_update(rs, s, v1)
    acc = acc_s[...]
    o_s = acc[:, :LANES] / acc[:, LANES:]
    for r in range(GROUP):
        tile = oa_acc[:, r * LANES:(r + 1) * LANES] + gate_tile(1, r) * _pair_heads(o_s, r)
        oa_ref[:, r * LANES:(r + 1) * LANES] = tile.astype(oa_ref.dtype)


def _rel_bucket_np(dist):
    n = np.maximum(dist, 0)
    max_exact = REL_BUCKETS // 2
    nf = np.maximum(n, 1).astype(np.float32)
    log_b = max_exact + (np.log(nf / max_exact) / math.log(REL_MAX_DIST / max_exact)
                         * (REL_BUCKETS - max_exact)).astype(np.int32)
    log_b = np.minimum(log_b, REL_BUCKETS - 1)
    return np.where(n < max_exact, n, log_b)


def _toeplitz_bias(tab, pad, width, window, shift_far):
    length = width + Q_BLOCK
    dist = pad + Q_BLOCK - 1 - np.arange(length)
    onehot = np.zeros((length, REL_BUCKETS), np.float32)
    onehot[np.arange(length), _rel_bucket_np(dist)] = 1.0
    vals = jnp.dot(jnp.asarray(onehot), tab, precision=lax.Precision.HIGHEST)
    if shift_far:
        vals = vals - tab[REL_BUCKETS - 1][None, :]
    valid = (dist >= 0) & (dist < window)
    vals = jnp.where(jnp.asarray(valid)[:, None], vals, NEG).T
    skew = jnp.tile(vals, (1, Q_BLOCK))[:, :Q_BLOCK * (length - 1)].reshape(N_HEADS, Q_BLOCK, length - 1)
    return skew[:, :, Q_BLOCK - 1:Q_BLOCK - 1 + width].reshape(N_HEADS * Q_BLOCK, width).astype(F32)


def _attention(proj, kvcmp, sinks, bias_table, bsz, seq):
    assert seq % SEL_CHUNK == 0
    nq = seq // Q_BLOCK
    n_far = seq // CMP_STRIDE
    n_sel = seq // SEL_BLOCK
    n_top = min(SEL_TOP_N, n_sel)
    assert n_top >= SEL_INIT_BLOCKS + SEL_LOCAL_BLOCKS and n_sel <= LANES
    wpad = Q_BLOCK * (-(-(NSA_WINDOW - 1) // Q_BLOCK))
    bpad = Q_BLOCK * (-(-(SWA_WINDOW - 1) // Q_BLOCK))
    pair = lambda tab: tab.astype(F32).reshape(REL_BUCKETS, N_GROUPS, GROUP).transpose(0, 2, 1).reshape(REL_BUCKETS, -1)
    tab_a = pair(bias_table[:, :N_HEADS])
    tab_b = pair(bias_table[:, N_HEADS:])
    near_pad = CMP_STRIDE * CMP_FRONT - (CMP_BLOCK - 1)
    t_near = _toeplitz_bias(tab_a, near_pad, CMP_STRIDE * CMP_NEAR, 1 << 30, True)[:, ::CMP_STRIDE]
    t_sel = _toeplitz_bias(tab_a, Q_BLOCK, 2 * Q_BLOCK, 1 << 30, True)
    t_win = _toeplitz_bias(tab_a, wpad, wpad + Q_BLOCK, NSA_WINDOW, False)
    t_swa = _toeplitz_bias(tab_b, bpad, bpad + Q_BLOCK, SWA_WINDOW, False)
    n_rows = kvcmp.shape[2]
    cn = (np.arange(n_rows) - CMP_FRONT)[:, None] * CMP_STRIDE
    sj = np.arange(LANES)[None, :] * SEL_BLOCK
    cmat = ((cn < sj + SEL_BLOCK) & (cn + CMP_BLOCK > sj) & (cn >= 0) & (cn + CMP_BLOCK <= seq)
            & (sj < seq)).astype(np.float32)
    cmat = jnp.asarray(cmat, F32)
    padded = lambda name, p: jnp.pad(proj[name].reshape(bsz, seq, LANES), ((0, 0), (p, 0), (0, 0)))
    ks, vs = padded('ks', Q_BLOCK), padded('vs', Q_BLOCK)
    kw, vw = padded('kw', wpad), padded('vw', wpad)
    kb, vb = padded('kb', bpad), padded('vb', bpad)
    rows = N_HEADS * Q_BLOCK
    qspec = pl.BlockSpec((Q_BLOCK, 4 * LANES), lambda b, i: (b * nq + i, 0))
    const2 = lambda shape: pl.BlockSpec(shape, lambda b, i: (0, 0))
    batch3 = lambda n: pl.BlockSpec((1, n, LANES), lambda b, i: (b, 0, 0))
    kernel = functools.partial(_attn_kernel, n_far=n_far, n_top=n_top)
    return pl.pallas_call(
        kernel,
        grid=(bsz, nq),
        in_specs=[pl.BlockSpec(memory_space=pltpu.SMEM),
                  qspec, qspec,
                  pl.BlockSpec((Q_BLOCK, LANES), lambda b, i: (b * nq + i, 0)),
                  pl.BlockSpec((1, 1, n_rows, LANES), lambda b, i: (0, b, 0, 0)),
                  pl.BlockSpec((1, 1, n_rows, LANES), lambda b, i: (1, b, 0, 0)),
                  batch3(seq + Q_BLOCK), batch3(seq + Q_BLOCK),
                  batch3(seq + wpad), batch3(seq + wpad),
                  batch3(seq + bpad), batch3(seq + bpad),
                  const2((n_rows, LANES)),
                  const2((rows, CMP_NEAR)),
                  const2((rows, 2 * Q_BLOCK)),
                  const2((rows, wpad + Q_BLOCK)),
                  const2((rows, bpad + Q_BLOCK))],
        out_specs=[qspec, qspec],
        out_shape=[jax.ShapeDtypeStruct((bsz * seq, 4 * LANES), BF16)] * 2,
        scratch_shapes=[pltpu.VMEM((rows, LANES), MXU_DTYPE),
                        pltpu.VMEM((rows, LANES), MXU_DTYPE),
                        pltpu.VMEM((N_GROUPS * Q_BLOCK, LANES), MXU_DTYPE),
                        pltpu.VMEM((N_GROUPS * Q_BLOCK, LANES), MXU_DTYPE),
                        pltpu.VMEM((rows, LANES), F32),
                        pltpu.VMEM((rows, 2 * LANES), F32),
                        pltpu.VMEM((2, rows, SEL_CHUNK), F32),
                        pltpu.VMEM((Q_BLOCK, 4 * LANES), F32)],
        compiler_params=pltpu.CompilerParams(dimension_semantics=("arbitrary", "arbitrary"),
                                             vmem_limit_bytes=VMEM_LIMIT),
        name="attention",
    )(sinks.astype(F32), proj['qa'], proj['qb'], proj['ga'], kvcmp, kvcmp, ks, vs, kw, vw, kb, vb,
      cmat, t_near, t_sel, t_win, t_swa)


def _layer_norm(y, g, b):
    mu = jnp.mean(y, axis=-1, keepdims=True)
    yc = y - mu
    var = jnp.mean(yc * yc, axis=-1, keepdims=True)
    return yc * lax.rsqrt(var + LN_EPS) * g + b


def _outproj_kernel(oa_ref, ob_ref, sg_ref, x_ref, pa_ref, pb_ref, wo_ref, g1_ref, b1_ref, wr_ref, rb_ref, sgu_ref,
                    sd_ref, tri_ref, h_ref, base_ref, eidx_ref, gate_ref, rank_ref, cnt_ref, carry):
    step = pl.program_id(0)
    tm = oa_ref.shape[0]

    @pl.when(step == 0)
    def _():
        carry[...] = jnp.zeros(carry.shape, F32)

    sg = sg_ref[...].astype(F32)
    merged = (sg[:, :D_MODEL] * _dot(_mx(oa_ref[...]), pa_ref[...])
              + sg[:, D_MODEL:] * _dot(_mx(ob_ref[...]), pb_ref[...]))
    mix = _dot(_mx(merged), wo_ref[...])
    h = _layer_norm(DN_ALPHA * x_ref[...] + mix, g1_ref[...], b1_ref[...])
    h_ref[...] = h
    hb = _mx(h)

    gu = _dot(hb, sgu_ref[...])
    shared = _dot(_mx(jax.nn.silu(gu[:, :SHARED_HIDDEN]) * gu[:, SHARED_HIDDEN:]), sd_ref[...])
    base_ref[...] = DN_ALPHA * h + shared

    scores = jax.nn.sigmoid(_dot_nt(wr_ref[...], hb))
    choice = scores + rb_ref[:, 0:1]
    per_group = N_EXPERTS // N_EXPERT_GROUPS
    gs = []
    for g in range(N_EXPERT_GROUPS):
        cg = choice[g * per_group:(g + 1) * per_group]
        m1 = jnp.max(cg, axis=0, keepdims=True)
        is_m = cg == m1
        n_m = jnp.sum(is_m.astype(F32), axis=0, keepdims=True)
        m2 = jnp.max(jnp.where(is_m, -jnp.inf, cg), axis=0, keepdims=True)
        gs.append(m1 + jnp.where(n_m > 1.5, m1, m2))
    gs = jnp.concatenate(gs, axis=0)
    gid = lax.broadcasted_iota(jnp.int32, gs.shape, 0)
    beaten = jnp.zeros(gs.shape, jnp.int32)
    for g in range(N_EXPERT_GROUPS):
        other = gs[g:g + 1]
        beaten = beaten + ((other > gs) | ((other == gs) & (g < gid))).astype(jnp.int32)
    keep_g = beaten < TOPK_EXPERT_GROUPS
    keep = jnp.concatenate([jnp.broadcast_to(keep_g[g:g + 1], (per_group, tm)) for g in range(N_EXPERT_GROUPS)],
                           axis=0)
    cand = jnp.where(keep, choice, -jnp.inf)
    eid = lax.broadcasted_iota(jnp.int32, cand.shape, 0)
    hits = []
    e_rows = []
    w_rows = []
    for _ in range(TOP_K):
        m = jnp.max(cand, axis=0, keepdims=True)
        idx = jnp.min(jnp.where(cand == m, eid, N_EXPERTS), axis=0, keepdims=True)
        hit = eid == idx
        hits.append(hit)
        e_rows.append(idx)
        w_rows.append(jnp.sum(jnp.where(hit, scores, 0.0), axis=0, keepdims=True))
        cand = jnp.where(hit, -jnp.inf, cand)
    w = jnp.concatenate(w_rows, axis=0)
    gate_ref[...] = w / jnp.sum(w, axis=0, keepdims=True) * ROUTED_SCALE
    eidx_ref[...] = jnp.concatenate(e_rows, axis=0)

    onehot = jnp.zeros(cand.shape, F32)
    for hit in hits:
        onehot = onehot + hit.astype(F32)
    before = _dot(onehot.astype(BF16), tri_ref[...]) + carry[:, 0:1]
    rank_ref[...] = jnp.concatenate(
        [jnp.sum(jnp.where(hit, before, 0.0), axis=0, keepdims=True) for hit in hits], axis=0).astype(jnp.int32)
    carry[...] = carry[...] + jnp.sum(onehot, axis=1, keepdims=True)
    cnt_ref[...] = carry[...]


def _out_projection(oa, ob, sg, x2, proj_a, proj_b, w_out, ln_g, ln_b, w_router, router_bias, s_gate, s_up, s_down):
    t = x2.shape[0]
    tm = OUT_TM
    pair_rows = lambda p: p.reshape(N_GROUPS, GROUP, HEAD_DIM, -1).transpose(1, 0, 2, 3).reshape(p.shape)
    pa = pair_rows(proj_a).astype(MXU_DTYPE)
    pb = pair_rows(proj_b).astype(MXU_DTYPE)
    tri = jnp.asarray(np.triu(np.ones((tm, tm), np.float32), 1), BF16)
    row = lambda i: (i, 0)
    fixed = lambda i: (0, 0)
    col = lambda i: (0, i)
    outs = pl.pallas_call(
        _outproj_kernel,
        grid=(t // tm,),
        in_specs=[pl.BlockSpec((tm, 4 * LANES), row), pl.BlockSpec((tm, 4 * LANES), row),
                  pl.BlockSpec((tm, 2 * D_MODEL), row), pl.BlockSpec((tm, D_MODEL), row),
                  pl.BlockSpec((4 * LANES, D_MODEL), fixed), pl.BlockSpec((4 * LANES, D_MODEL), fixed),
                  pl.BlockSpec((D_MODEL, D_MODEL), fixed),
                  pl.BlockSpec((1, D_MODEL), fixed), pl.BlockSpec((1, D_MODEL), fixed),
                  pl.BlockSpec((N_EXPERTS, D_MODEL), fixed), pl.BlockSpec((N_EXPERTS, LANES), fixed),
                  pl.BlockSpec((D_MODEL, 2 * SHARED_HIDDEN), fixed), pl.BlockSpec((SHARED_HIDDEN, D_MODEL), fixed),
                  pl.BlockSpec((tm, tm), fixed)],
        out_specs=[pl.BlockSpec((tm, D_MODEL), row), pl.BlockSpec((tm, D_MODEL), row),
                   pl.BlockSpec((TOP_K, tm), col), pl.BlockSpec((TOP_K, tm), col), pl.BlockSpec((TOP_K, tm), col),
                   pl.BlockSpec((N_EXPERTS, LANES), fixed)],
        out_shape=[jax.ShapeDtypeStruct((t, D_MODEL), F32), jax.ShapeDtypeStruct((t, D_MODEL), F32),
                   jax.ShapeDtypeStruct((TOP_K, t), jnp.int32), jax.ShapeDtypeStruct((TOP_K, t), F32),
                   jax.ShapeDtypeStruct((TOP_K, t), jnp.int32), jax.ShapeDtypeStruct((N_EXPERTS, LANES), F32)],
        scratch_shapes=[pltpu.VMEM((N_EXPERTS, LANES), F32)],
        compiler_params=pltpu.CompilerParams(dimension_semantics=("arbitrary",), vmem_limit_bytes=VMEM_LIMIT),
        name="out_projection_router",
    )(oa, ob, sg, x2, pa, pb, w_out.astype(MXU_DTYPE), ln_g.reshape(1, -1), ln_b.reshape(1, -1),
      w_router.T.astype(MXU_DTYPE), jnp.broadcast_to(router_bias.astype(F32)[:, None], (N_EXPERTS, LANES)),
      jnp.concatenate([s_gate, s_up], axis=1).astype(MXU_DTYPE), s_down.astype(MXU_DTYPE), tri)
    return outs


def _rows_to_tiles(x):
    return pltpu.einshape("cml->mcl", jnp.stack(_lane_tiles(x), axis=0))


def _tiles_to_rows(x3):
    xt = pltpu.einshape("mcl->cml", x3)
    return jnp.concatenate([xt[c] for c in range(xt.shape[0])], axis=1)


def _dispatch_kernel(zstart_ref, cnt_ref, dest_ref, h2_ref, xs_ref, h_ref, zeros, sem):
    step = pl.program_id(0)
    tm = h_ref.shape[0]
    h_ref[...] = _rows_to_tiles(h2_ref[...])

    @pl.when(step == 0)
    def _():
        zeros[...] = jnp.zeros(zeros.shape, F32)

        def fill(e, c):
            @pl.when(cnt_ref[e] > 0)
            def _():
                cp = pltpu.make_async_copy(zeros, xs_ref.at[pl.ds(zstart_ref[e], MOE_BM)], sem)
                cp.start()
                cp.wait()
            return c
        lax.fori_loop(0, N_EXPERTS, fill, 0)

    def issue(t, c):
        for k in range(TOP_K):
            pltpu.make_async_copy(h_ref.at[t], xs_ref.at[dest_ref[k, t]], sem).start(priority=k % 2)
        return c
    lax.fori_loop(0, tm, issue, 0)
    for k in range(TOP_K):
        pltpu.make_async_copy(h_ref, xs_ref.at[pl.ds(0, tm)], sem).wait()


def _dispatch(h, dest, zstart, counts, n_rows):
    t = h.shape[0]
    tm = DISP_TM
    return pl.pallas_call(
        _dispatch_kernel,
        grid_spec=pltpu.PrefetchScalarGridSpec(
            num_scalar_prefetch=2,
            grid=(t // tm,),
            in_specs=[pl.BlockSpec((TOP_K, tm), lambda i, *_: (0, i), memory_space=pltpu.SMEM),
                      pl.BlockSpec((tm, D_MODEL), lambda i, *_: (i, 0))],
            out_specs=pl.BlockSpec(memory_space=pl.ANY),
            scratch_shapes=[pltpu.VMEM((tm,) + ROW_TILE, F32), pltpu.VMEM((MOE_BM,) + ROW_TILE, F32),
                            pltpu.SemaphoreType.DMA(())]),
        out_shape=jax.ShapeDtypeStruct((n_rows,) + ROW_TILE, F32),
        compiler_params=pltpu.CompilerParams(dimension_semantics=("arbitrary",), vmem_limit_bytes=VMEM_LIMIT),
        name="moe_dispatch",
    )(zstart, counts, dest, h)


def _experts_kernel(blk_e_ref, nused_ref, xs_ref, wg_ref, wu_ref, wd_ref, ys_ref, wg_s, wu_s, wd_s):
    b = pl.program_id(0)
    prev = blk_e_ref[jnp.maximum(b - 1, 0)]

    @pl.when((b == 0) | (blk_e_ref[b] != prev))
    def _():
        wg_s[...] = _mx(wg_ref[0])
        wu_s[...] = _mx(wu_ref[0])
        wd_s[...] = _mx(wd_ref[0])

    n_tiles = D_MODEL // LANES

    @pl.when(b < nused_ref[0])
    def _():
        xt = pltpu.einshape("mcl->cml", xs_ref[...])
        xb = _mx(jnp.concatenate([xt[c] for c in range(n_tiles)], axis=1))
        hid = jax.nn.silu(_dot(xb, wg_s[...])) * _dot(xb, wu_s[...])
        y = _dot(_mx(hid), wd_s[...])
        yt = jnp.stack([y[:, c * LANES:(c + 1) * LANES] for c in range(n_tiles)], axis=0)
        ys_ref[...] = pltpu.einshape("cml->mcl", yt)

    @pl.when(b >= nused_ref[0])
    def _():
        ys_ref[...] = jnp.zeros(ys_ref.shape, F32)


def _experts(xs, blk_e, nused, e_gate, e_up, e_down):
    n_rows = xs.shape[0]
    n_blocks = n_rows // MOE_BM
    xmap = lambda b, be, nu: (jnp.minimum(b, nu[0] - 1), 0, 0)
    wmap = lambda b, be, nu: (be[b], 0, 0)
    return pl.pallas_call(
        _experts_kernel,
        grid_spec=pltpu.PrefetchScalarGridSpec(
            num_scalar_prefetch=2,
            grid=(n_blocks,),
            in_specs=[pl.BlockSpec((MOE_BM,) + ROW_TILE, xmap),
                      pl.BlockSpec((1, D_MODEL, EXPERT_HIDDEN), wmap),
                      pl.BlockSpec((1, D_MODEL, EXPERT_HIDDEN), wmap),
                      pl.BlockSpec((1, EXPERT_HIDDEN, D_MODEL), wmap)],
            out_specs=pl.BlockSpec((MOE_BM,) + ROW_TILE, lambda b, be, nu: (b, 0, 0)),
            scratch_shapes=[pltpu.VMEM((D_MODEL, EXPERT_HIDDEN), MXU_DTYPE),
                            pltpu.VMEM((D_MODEL, EXPERT_HIDDEN), MXU_DTYPE),
                            pltpu.VMEM((EXPERT_HIDDEN, D_MODEL), MXU_DTYPE)]),
        out_shape=jax.ShapeDtypeStruct((n_rows,) + ROW_TILE, F32),
        compiler_params=pltpu.CompilerParams(dimension_semantics=("arbitrary",), vmem_limit_bytes=VMEM_LIMIT),
        name="moe_experts",
    )(blk_e, nused, xs, e_gate, e_up, e_down)


def _combine_kernel(dest_ref, dest_next_ref, gate_ref, base_ref, g2_ref, b2_ref, ys_ref, out_ref, buf, routed, sem):
    step = pl.program_id(0)
    tm = base_ref.shape[0]
    slot = step % 2

    def issue(d_ref, s):
        def body(t, c):
            for k in range(TOP_K):
                pltpu.make_async_copy(ys_ref.at[d_ref[k, t]], buf.at[s, k, t], sem.at[s]).start(priority=k % 2)
            return c
        lax.fori_loop(0, tm, body, 0)

    @pl.when(step == 0)
    def _():
        issue(dest_ref, 0)

    @pl.when(step + 1 < pl.num_programs(0))
    def _():
        issue(dest_next_ref, 1 - slot)

    for k in range(TOP_K):
        pltpu.make_async_copy(ys_ref.at[pl.ds(0, tm)], buf.at[slot, k], sem.at[slot]).wait()

    def token(t, c):
        y = gate_ref[0, t] * buf[slot, 0, t]
        for k in range(1, TOP_K):
            y = y + gate_ref[k, t] * buf[slot, k, t]
        routed[t] = y
        return c
    lax.fori_loop(0, tm, token, 0)
    out_ref[...] = _layer_norm(base_ref[...] + _tiles_to_rows(routed[...]), g2_ref[...], b2_ref[...])


def _combine(ys3, dest, gate, base, ln_g, ln_b):
    t = base.shape[0]
    tm = COMB_TM
    n_tiles = t // tm
    return pl.pallas_call(
        _combine_kernel,
        grid=(n_tiles,),
        in_specs=[pl.BlockSpec((TOP_K, tm), lambda i: (0, i), memory_space=pltpu.SMEM),
                  pl.BlockSpec((TOP_K, tm), lambda i: (0, jnp.minimum(i + 1, n_tiles - 1)), memory_space=pltpu.SMEM),
                  pl.BlockSpec((TOP_K, tm), lambda i: (0, i), memory_space=pltpu.SMEM),
                  pl.BlockSpec((tm, D_MODEL), lambda i: (i, 0)),
                  pl.BlockSpec((1, D_MODEL), lambda i: (0, 0)),
                  pl.BlockSpec((1, D_MODEL), lambda i: (0, 0)),
                  pl.BlockSpec(memory_space=pl.ANY)],
        out_specs=pl.BlockSpec((tm, D_MODEL), lambda i: (i, 0)),
        out_shape=jax.ShapeDtypeStruct((t, D_MODEL), F32),
        scratch_shapes=[pltpu.VMEM((2, TOP_K, tm) + ROW_TILE, F32), pltpu.VMEM((tm,) + ROW_TILE, F32),
                        pltpu.SemaphoreType.DMA((2,))],
        compiler_params=pltpu.CompilerParams(dimension_semantics=("arbitrary",), vmem_limit_bytes=VMEM_LIMIT),
        name="moe_combine",
    )(dest, dest, gate, base, ln_g.reshape(1, -1), ln_b.reshape(1, -1), ys3)


def _moe_layout(eidx, rank, counts):
    n_assign = eidx.size
    n_blocks = (n_assign + N_EXPERTS * (MOE_BM - 1)) // MOE_BM
    padded = (counts + MOE_BM - 1) // MOE_BM * MOE_BM
    pends = jnp.cumsum(padded)
    pstarts = pends - padded
    experts = jnp.arange(N_EXPERTS, dtype=jnp.int32)
    dest = jnp.sum(jnp.where(eidx[..., None] == experts, pstarts, 0), axis=-1) + rank
    block_row = jnp.arange(n_blocks, dtype=jnp.int32) * MOE_BM
    blk_e = jnp.minimum(jnp.sum(pends[None, :] <= block_row[:, None], axis=1), N_EXPERTS - 1).astype(jnp.int32)
    nused = (pends[-1:] // MOE_BM).astype(jnp.int32)
    zstart = jnp.maximum(pends - MOE_BM, 0).astype(jnp.int32)
    return dest.astype(jnp.int32), blk_e, nused, zstart, n_blocks * MOE_BM


def _layer(x, w_in, cmp_pe, cmp_w1, cmp_b1, cmp_w2, sinks, bias_table, proj_a, proj_b, w_out, ln1_g, ln1_b,
           w_router, router_bias, e_gate, e_up, e_down, s_gate, s_up, s_down, ln2_g, ln2_b):
    bsz, seq, d = x.shape
    x2 = x.reshape(bsz * seq, d)
    proj = _in_projection(x2, w_in)
    kvcmp = _compress(proj['kc'], proj['vc'], bsz, seq, cmp_pe, cmp_w1, cmp_b1, cmp_w2)
    oa, ob = _attention(proj, kvcmp, sinks, bias_table, bsz, seq)
    h, base, eidx, gate, rank, cnt = _out_projection(oa, ob, proj['sg'], x2, proj_a, proj_b, w_out, ln1_g, ln1_b,
                                                     w_router, router_bias, s_gate, s_up, s_down)
    counts = cnt[:, 0].astype(jnp.int32)
    dest, blk_e, nused, zstart, n_rows = _moe_layout(eidx, rank, counts)
    xs = _dispatch(h, dest, zstart, counts, n_rows)
    ys = _experts(xs, blk_e, nused, e_gate, e_up, e_down)
    out = _combine(ys, dest, gate, base, ln2_g, ln2_b)
    return out.reshape(bsz, seq, d)


def kernel(x, w_in, cmp_pe, cmp_w1, cmp_b1, cmp_w2, attn_sinks, rel_bias_table, proj_a, proj_b, w_out, ln1_g, ln1_b,
           w_router, router_bias, expert_w_gate, expert_w_up, expert_w_down, shared_w_gate, shared_w_up,
           shared_w_down, ln2_g, ln2_b):
    h = x
    for l in range(DEPTH):
        h = _layer(h, w_in[l], cmp_pe[l], cmp_w1[l], cmp_b1[l], cmp_w2[l], attn_sinks[l], rel_bias_table, proj_a[l],
                   proj_b[l], w_out[l], ln1_g[l], ln1_b[l], w_router[l], router_bias[l], expert_w_gate[l],
                   expert_w_up[l], expert_w_down[l], shared_w_gate[l], shared_w_up[l], shared_w_down[l], ln2_g[l],
                   ln2_b[l])
    return h
```

```python
import functools
import math

import numpy as np
import jax
import jax.numpy as jnp
from jax import lax
from jax.experimental import pallas as pl
from jax.experimental.pallas import tpu as pltpu

F32 = jnp.float32
BF16 = jnp.bfloat16
MXU_DTYPE = jnp.bfloat16

D_MODEL = 1024
HEAD_DIM = 64
ATTN_SCALE = HEAD_DIM ** -0.5
Q_BLOCK = 128
N_HEADS = 8
N_GROUPS = 2
GROUP = 4
CMP_BLOCK = 32
CMP_STRIDE = 16
CMP_HIDDEN = 128
SEL_BLOCK = 64
SEL_TOP_N = 8
SEL_INIT_BLOCKS = 1
SEL_LOCAL_BLOCKS = 2
NSA_WINDOW = 512
SWA_WINDOW = 128
REL_BUCKETS = 32
REL_MAX_DIST = 128
N_EXPERTS = 256
TOP_K = 8
EXPERT_HIDDEN = 256
SHARED_HIDDEN = 256
N_EXPERT_GROUPS = 8
TOPK_EXPERT_GROUPS = 4
ROUTED_SCALE = 2.5
LN_EPS = 1e-5
DEPTH = 1
DN_ALPHA = (2 * DEPTH) ** 0.25

NEG = -1e30
LANES = 128
ROW_TILE = (8, LANES)
PACKED_ROW_TILE = (4, LANES)
CMP_FRONT = 16
CMP_NEAR = LANES
SEL_CHUNK = 512
VMEM_LIMIT = 56 * 1024 * 1024

IN_TM = 512
OUT_TM = 256
MOE_BM = 256
DISP_TM = 256
COMB_TM = 128


def _dot(a, b):
    return jnp.dot(a, b, preferred_element_type=F32)


def _dot_nt(a, b):
    return lax.dot_general(a, b, (((1,), (1,)), ((), ())), preferred_element_type=F32)


def _mx(a):
    return a.astype(MXU_DTYPE)


_IN_COLS = (('qa', 512), ('qb', 512), ('kc', 128), ('vc', 128), ('ks', 128), ('vs', 128), ('kw', 128),
            ('vw', 128), ('kb', 128), ('vb', 128), ('ga', 128), ('sg', 2048))


def _inproj_kernel(x_ref, w_ref, qa_ref, qb_ref, kc_ref, vc_ref, ks_ref, vs_ref, kw_ref, vw_ref, kb_ref, vb_ref,
                   ga_ref, sg_ref):
    xb = _mx(x_ref[...])
    outs = dict(qa=qa_ref, qb=qb_ref, kc=kc_ref, vc=vc_ref, ks=ks_ref, vs=vs_ref, kw=kw_ref, vw=vw_ref,
                kb=kb_ref, vb=vb_ref, ga=ga_ref, sg=sg_ref)
    off = 0
    for name, width in _IN_COLS:
        for c0 in range(0, width, 512):
            cw = min(512, width - c0)
            y = _dot(xb, w_ref[:, off + c0:off + c0 + cw])
            if name in ('ga', 'sg'):
                y = jax.nn.sigmoid(y)
            outs[name][:, c0:c0 + cw] = y.astype(outs[name].dtype)
        off += width


def _pair_head_columns(w):
    return w.reshape(w.shape[0], N_GROUPS, GROUP, HEAD_DIM).transpose(0, 2, 1, 3).reshape(w.shape[0], -1)


def _in_projection(x2, w_in):
    t = x2.shape[0]
    sizes = (512, 128, 128, 128, 128, 128, 128, 24, 512, 128, 128, 1024, 1024)
    offs = np.cumsum((0,) + sizes)
    part = [w_in[:, offs[k]:offs[k + 1]] for k in range(len(sizes))]
    w_qa, w_kc, w_vc, w_ks, w_vs, w_kw, w_vw, w_g, w_qb, w_kb, w_vb, w_gate_a, w_gate_b = part
    w_qa = _pair_head_columns(w_qa) * ATTN_SCALE
    w_qb = _pair_head_columns(w_qb) * ATTN_SCALE
    w_ga = w_g.reshape(-1, N_GROUPS, GROUP, 3).transpose(0, 3, 2, 1).reshape(-1, 24)
    w_ga = jnp.pad(w_ga, ((0, 0), (0, LANES - 24)))
    w_all = jnp.concatenate([w_qa, w_qb, w_kc, w_vc, w_ks, w_vs, w_kw, w_vw, w_kb, w_vb, w_ga, w_gate_a, w_gate_b],
                            axis=1).astype(MXU_DTYPE)
    n_all = w_all.shape[1]
    out_shape = []
    out_specs = []
    for name, width in _IN_COLS:
        dt = F32 if name == 'ga' else BF16
        out_shape.append(jax.ShapeDtypeStruct((t, width), dt))
        out_specs.append(pl.BlockSpec((IN_TM, width), lambda i: (i, 0)))
    outs = pl.pallas_call(
        _inproj_kernel,
        grid=(t // IN_TM,),
        in_specs=[pl.BlockSpec((IN_TM, D_MODEL), lambda i: (i, 0)),
                  pl.BlockSpec((D_MODEL, n_all), lambda i: (0, 0))],
        out_specs=out_specs,
        out_shape=out_shape,
        compiler_params=pltpu.CompilerParams(dimension_semantics=("arbitrary",), vmem_limit_bytes=VMEM_LIMIT),
        name="in_projection",
    )(x2, w_all)
    return dict(zip([n for n, _ in _IN_COLS], outs))


def _compress_kernel(tok_ref, w1_ref, pe_ref, w1o_ref, b1_ref, w2_ref, out_ref):
    n_chunks = tok_ref.shape[2]
    ab = _dot(tok_ref[0, 0], w1_ref[0])
    a = ab[:, :2 * CMP_HIDDEN]
    b_next = pltpu.roll(ab[:, 2 * CMP_HIDDEN:], n_chunks - 1, 0)
    cb = _dot(_mx(pe_ref[0]), _mx(w1o_ref[0]))[0:1, :] + b1_ref[0]
    cb2 = jnp.concatenate([cb, cb], axis=1)
    hid = jax.nn.gelu(a + b_next + cb2)
    out = _dot(_mx(hid), w2_ref[0])
    row = lax.broadcasted_iota(jnp.int32, out.shape, 0)
    out = jnp.where(row < n_chunks - 1, out, 0.0)
    out_ref[0, 0, 0:CMP_FRONT, :] = jnp.zeros((CMP_FRONT, LANES), F32)
    out_ref[0, 0, CMP_FRONT:CMP_FRONT + n_chunks, :] = out
    out_ref[0, 0, CMP_FRONT + n_chunks:, :] = jnp.zeros((CMP_NEAR - CMP_FRONT, LANES), F32)


def _compress(kc, vc, bsz, seq, cmp_pe, cmp_w1, cmp_b1, cmp_w2):
    n_chunks = seq // CMP_STRIDE
    tok = jnp.stack([kc, vc]).reshape(2, bsz, n_chunks, CMP_STRIDE * LANES)
    eye = jnp.eye(N_GROUPS, dtype=F32)
    w1r = cmp_w1.reshape(2, 2, CMP_STRIDE, HEAD_DIM, CMP_HIDDEN)
    w1 = jnp.einsum('khjdn,gG->kjgdhGn', w1r, eye).reshape(2, CMP_STRIDE * LANES, 4 * CMP_HIDDEN).astype(MXU_DTYPE)
    w2 = jnp.einsum('knd,gG->kgnGd', cmp_w2, eye).reshape(2, 2 * CMP_HIDDEN, LANES).astype(MXU_DTYPE)
    pe = jnp.pad(cmp_pe.reshape(2, 1, CMP_BLOCK * HEAD_DIM), ((0, 0), (0, 7), (0, 0)))
    b1 = cmp_b1.reshape(2, 1, CMP_HIDDEN)
    rows = CMP_FRONT + n_chunks + CMP_NEAR - CMP_FRONT
    return pl.pallas_call(
        _compress_kernel,
        grid=(2, bsz),
        in_specs=[pl.BlockSpec((1, 1, n_chunks, CMP_STRIDE * LANES), lambda k, b: (k, b, 0, 0)),
                  pl.BlockSpec((1, CMP_STRIDE * LANES, 4 * CMP_HIDDEN), lambda k, b: (k, 0, 0)),
                  pl.BlockSpec((1, 8, CMP_BLOCK * HEAD_DIM), lambda k, b: (k, 0, 0)),
                  pl.BlockSpec((1, CMP_BLOCK * HEAD_DIM, CMP_HIDDEN), lambda k, b: (k, 0, 0)),
                  pl.BlockSpec((1, 1, CMP_HIDDEN), lambda k, b: (k, 0, 0)),
                  pl.BlockSpec((1, 2 * CMP_HIDDEN, LANES), lambda k, b: (k, 0, 0))],
        out_specs=pl.BlockSpec((1, 1, rows, LANES), lambda k, b: (k, b, 0, 0)),
        out_shape=jax.ShapeDtypeStruct((2, bsz, rows, LANES), F32),
        compiler_params=pltpu.CompilerParams(dimension_semantics=("arbitrary", "arbitrary"),
                                             vmem_limit_bytes=VMEM_LIMIT),
        name="nsa_compress",
    )(tok, w1, pe, cmp_w1, b1, w2)


def _stack_heads(q_ref, dst):
    lo = lax.broadcasted_iota(jnp.int32, (Q_BLOCK, LANES), 1) < HEAD_DIM
    for r in range(GROUP):
        qr = q_ref[:, r * LANES:(r + 1) * LANES].astype(dst.dtype)
        z = jnp.zeros_like(qr)
        dst[(2 * r) * Q_BLOCK:(2 * r + 1) * Q_BLOCK, :] = jnp.where(lo, qr, z)
        dst[(2 * r + 1) * Q_BLOCK:(2 * r + 2) * Q_BLOCK, :] = jnp.where(lo, z, qr)


def _pair_heads(o, r):
    lo = lax.broadcasted_iota(jnp.int32, (Q_BLOCK, LANES), 1) < HEAD_DIM
    return jnp.where(lo, o[(2 * r) * Q_BLOCK:(2 * r + 1) * Q_BLOCK], o[(2 * r + 1) * Q_BLOCK:(2 * r + 2) * Q_BLOCK])


def _lane_tiles(x):
    return [x[:, t * LANES:(t + 1) * LANES] for t in range(x.shape[1] // LANES)]


def _row_max(tiles):
    mx = tiles[0]
    for t in tiles[1:]:
        mx = jnp.maximum(mx, t)
    return jnp.broadcast_to(jnp.max(mx, axis=1, keepdims=True), mx.shape)


def _with_ones(v):
    return jnp.concatenate([v, jnp.ones(v.shape, v.dtype)], axis=1)


def _block_of_key(n_keys, first_block):
    b = lax.broadcasted_iota(jnp.int32, (LANES, n_keys), 0)
    k = lax.broadcasted_iota(jnp.int32, (LANES, n_keys), 1)
    return (b == (k // SEL_BLOCK) + first_block).astype(MXU_DTYPE)


def _select_blocks_t(imp_t, i, n_top):
    blk = lax.broadcasted_iota(jnp.int32, imp_t.shape, 0)
    qcol = lax.broadcasted_iota(jnp.int32, imp_t.shape, 1)
    back = (2 * i + (qcol >= SEL_BLOCK).astype(jnp.int32)) - blk
    sel = (back >= 0) & ((blk < SEL_INIT_BLOCKS) | (back < SEL_LOCAL_BLOCKS))
    cand = jnp.where((back >= SEL_LOCAL_BLOCKS) & (blk >= SEL_INIT_BLOCKS), imp_t, -1.0)
    blk_f = blk.astype(F32)
    for _ in range(n_top - SEL_INIT_BLOCKS - SEL_LOCAL_BLOCKS):
        m = jnp.max(cand, axis=0, keepdims=True)
        idx = jnp.min(jnp.where(cand == m, blk_f, float(LANES)), axis=0, keepdims=True)
        hit = blk_f == idx
        sel = sel | (hit & (m >= 0.0))
        cand = jnp.where(hit, -2.0, cand)
    return sel


def _attn_kernel(sink_ref, qa_ref, qb_ref, ga_ref, kcmp_ref, vcmp_ref, ks_ref, vs_ref, kw_ref, vw_ref, kb_ref,
                 vb_ref, cmat_ref, tnear_ref, tsel_ref, twin_ref, tswa_ref, oa_ref, ob_ref,
                 qall, qball, mneg, mneg_far, m_s, acc_s, s_buf, oa_acc, *, n_far, n_top):
    i = pl.program_id(1)
    rows = N_HEADS * Q_BLOCK
    half = rows // 2
    halves = (slice(0, half), slice(half, rows))
    _stack_heads(qa_ref, qall)
    _stack_heads(qb_ref, qball)
    nstart = pl.multiple_of(i * Q_BLOCK, Q_BLOCK)
    lo = lax.broadcasted_iota(jnp.int32, (Q_BLOCK, LANES), 1) < HEAD_DIM
    gates = ga_ref[...]

    def gate_tile(c, r):
        return jnp.where(lo, gates[:, c * 8 + 2 * r:c * 8 + 2 * r + 1], gates[:, c * 8 + 2 * r + 1:c * 8 + 2 * r + 2])

    def softmax_pv(s_tiles, v1, fix_max=None):
        m = _row_max(s_tiles)
        if fix_max is not None:
            m = fix_max(m)
        e = [jnp.exp(t - m) for t in s_tiles]
        return e, m, _dot(_mx(jnp.concatenate(e, axis=1)), v1)

    off = pl.multiple_of(i * (Q_BLOCK // CMP_STRIDE), 8)
    k_cmp = _mx(jnp.concatenate([kcmp_ref[0, 0, 0:n_far, :], kcmp_ref[0, 0, pl.ds(off, CMP_NEAR), :]], axis=0))
    v_cmp = _with_ones(_mx(jnp.concatenate([vcmp_ref[0, 0, 0:n_far, :], vcmp_ref[0, 0, pl.ds(off, CMP_NEAR), :]],
                                           axis=0)))
    colf = lax.broadcasted_iota(jnp.int32, (1, n_far), 1)
    coln = lax.broadcasted_iota(jnp.int32, (1, CMP_NEAR), 1)
    col_ok = jnp.concatenate([(colf >= CMP_FRONT) & (colf < off), coln + off >= CMP_FRONT], axis=1)
    mask_c = jnp.where(col_ok, 0.0, NEG)
    no_key = lambda m: jnp.where(m > 0.5 * NEG, m, 0.0)
    p_cmp, o_c = [], []
    for rs in halves:
        tiles = _lane_tiles(_dot_nt(qall[rs, :], k_cmp) + mask_c)
        tiles[-1] = tiles[-1] + tnear_ref[rs, :]
        e, _, ov = softmax_pv(tiles, v_cmp, no_key)
        inv = 1.0 / jnp.maximum(ov[:, LANES:], 1e-30)
        o_c.append(ov[:, :LANES] * inv)
        p_cmp.append([t * inv for t in e])
    o_c = jnp.concatenate(o_c, axis=0)

    def far_start(j):
        return pl.multiple_of(Q_BLOCK + j * SEL_CHUNK, Q_BLOCK)

    def far_logits(j, slot, masked):
        kc = _mx(ks_ref[0, pl.ds(far_start(j), SEL_CHUNK), :])
        if masked:
            madd = _dot(mneg_far[...], _block_of_key(SEL_CHUNK, j * (SEL_CHUNK // SEL_BLOCK)))
        for rs in halves:
            s = _dot_nt(qall[rs, :], kc)
            s_buf[slot, rs, :] = s + jnp.concatenate([madd] * (GROUP // 2), axis=0) if masked else s

    far_logits(0, 0, False)

    blkcol = lax.broadcasted_iota(jnp.int32, (Q_BLOCK, LANES), 1)
    n_tiles = len(p_cmp[0])
    for g in range(N_GROUPS):
        imp = jnp.zeros((Q_BLOCK, LANES), F32)
        for t in range(n_tiles):
            pg = sum(p_cmp[r // 2][t][(2 * (r % 2) + g) * Q_BLOCK:(2 * (r % 2) + g + 1) * Q_BLOCK]
                     for r in range(GROUP))
            if t < n_tiles - 1:
                cm = _mx(cmat_ref[t * LANES:(t + 1) * LANES, :])
            else:
                cm = _mx(cmat_ref[pl.ds(off, CMP_NEAR), :])
            hi = _mx(pg)
            low = _mx(pg - hi.astype(F32))
            imp = imp + _dot(hi, cm) + _dot(low, cm)
        sel = _select_blocks_t(imp.T, i, n_top)
        neg = jnp.where(sel, 0.0, NEG).T
        mneg[g * Q_BLOCK:(g + 1) * Q_BLOCK, :] = neg.astype(mneg.dtype)
        mneg_far[g * Q_BLOCK:(g + 1) * Q_BLOCK, :] = jnp.where(blkcol < 2 * (i - 1), neg, NEG).astype(mneg.dtype)

    wpad = kw_ref.shape[1] - ks_ref.shape[1] + Q_BLOCK
    kwin = _mx(kw_ref[0, pl.ds(nstart, wpad + Q_BLOCK), :])
    vwin = _with_ones(_mx(vw_ref[0, pl.ds(nstart, wpad + Q_BLOCK), :]))
    colw = lax.broadcasted_iota(jnp.int32, (1, wpad + Q_BLOCK), 1)
    mask_w = jnp.where(colw + nstart >= wpad, 0.0, NEG)
    o_w = []
    for rs in halves:
        _, _, ov = softmax_pv(_lane_tiles(_dot_nt(qall[rs, :], kwin) + twin_ref[rs, :] + mask_w), vwin)
        o_w.append(ov[:, :LANES] / ov[:, LANES:])
    o_w = jnp.concatenate(o_w, axis=0)
    for r in range(GROUP):
        oa_acc[:, r * LANES:(r + 1) * LANES] = (gate_tile(0, r) * _pair_heads(o_c, r)
                                                + gate_tile(2, r) * _pair_heads(o_w, r))

    bpad = kb_ref.shape[1] - ks_ref.shape[1] + Q_BLOCK
    kwin = _mx(kb_ref[0, pl.ds(nstart, bpad + Q_BLOCK), :])
    vwin = _with_ones(_mx(vb_ref[0, pl.ds(nstart, bpad + Q_BLOCK), :]))
    colb = lax.broadcasted_iota(jnp.int32, (1, bpad + Q_BLOCK), 1)
    mask_b = jnp.where(colb + nstart >= bpad, 0.0, NEG)
    o_b = []
    for hh, rs in enumerate(halves):
        sink = jnp.concatenate([jnp.full((Q_BLOCK, LANES), sink_ref[(h % 2) * GROUP + h // 2], F32)
                                for h in range(hh * N_HEADS // 2, (hh + 1) * N_HEADS // 2)], axis=0)
        _, m, ov = softmax_pv(_lane_tiles(_dot_nt(qball[rs, :], kwin) + tswa_ref[rs, :] + mask_b), vwin,
                              lambda m: jnp.maximum(m, sink))
        o_b.append(ov[:, :LANES] / (ov[:, LANES:] + jnp.exp(sink - m)))
    o_b = jnp.concatenate(o_b, axis=0)
    for r in range(GROUP):
        ob_ref[:, r * LANES:(r + 1) * LANES] = _pair_heads(o_b, r).astype(ob_ref.dtype)

    m_s[...] = jnp.full(m_s.shape, NEG, F32)
    acc_s[...] = jnp.zeros(acc_s.shape, F32)

    def flash_update(rs, s, v1):
        s_tiles = _lane_tiles(s)
        m_old = m_s[rs, :]
        m_new = jnp.maximum(m_old, _row_max(s_tiles))
        alpha = jnp.exp(m_old - m_new)
        p = jnp.concatenate([jnp.exp(t - m_new) for t in s_tiles], axis=1)
        acc_s[rs, :] = jnp.concatenate([alpha, alpha], axis=1) * acc_s[rs, :] + _dot(_mx(p), v1)
        m_s[rs, :] = m_new

    n_far_keys = jnp.maximum(i - 1, 0) * Q_BLOCK
    n_chunks = (n_far_keys + SEL_CHUNK - 1) // SEL_CHUNK

    madd = _dot(mneg_far[...], _block_of_key(SEL_CHUNK, 0))
    for rs in halves:
        s_buf[0, rs, :] = s_buf[0, rs, :] + jnp.concatenate([madd] * (GROUP // 2), axis=0)

    def far_body(j, carry):
        v1 = _with_ones(_mx(vs_ref[0, pl.ds(far_start(j), SEL_CHUNK), :]))
        for rs in halves:
            flash_update(rs, s_buf[j % 2, rs, :], v1)
        far_logits(jnp.minimum(j + 1, n_chunks - 1), (j + 1) % 2, True)
        return carry

    lax.fori_loop(0, n_chunks, far_body, 0)
    kc = _mx(ks_ref[0, pl.ds(nstart, 2 * Q_BLOCK), :])
    v1 = _with_ones(_mx(vs_ref[0, pl.ds(nstart, 2 * Q_BLOCK), :]))
    madd = _dot(mneg[...], _block_of_key(2 * Q_BLOCK, 2 * (i - 1)))
    col2 = lax.broadcasted_iota(jnp.int32, (1, 2 * Q_BLOCK), 1)
    mask_n = jnp.where((col2 < Q_BLOCK) & (i == 0), NEG, 0.0)
    for rs in halves:
        s = _dot_nt(qall[rs, :], kc) + jnp.concatenate([madd] * (GROUP // 2), axis=0) + tsel_ref[rs, :] + mask_n
        flash_update(rs, s, v1)
    acc = acc_s[...]
    o_s = acc[:, :LANES] / acc[:, LANES:]
    for r in range(GROUP):
        tile = oa_acc[:, r * LANES:(r + 1) * LANES] + gate_tile(1, r) * _pair_heads(o_s, r)
        oa_ref[:, r * LANES:(r + 1) * LANES] = tile.astype(oa_ref.dtype)


def _rel_bucket_np(dist):
    n = np.maximum(dist, 0)
    max_exact = REL_BUCKETS // 2
    nf = np.maximum(n, 1).astype(np.float32)
    log_b = max_exact + (np.log(nf / max_exact) / math.log(REL_MAX_DIST / max_exact)
                         * (REL_BUCKETS - max_exact)).astype(np.int32)
    log_b = np.minimum(log_b, REL_BUCKETS - 1)
    return np.where(n < max_exact, n, log_b)


def _toeplitz_bias(tab, pad, width, window, shift_far):
    length = width + Q_BLOCK
    dist = pad + Q_BLOCK - 1 - np.arange(length)
    onehot = np.zeros((length, REL_BUCKETS), np.float32)
    onehot[np.arange(length), _rel_bucket_np(dist)] = 1.0
    vals = jnp.dot(jnp.asarray(onehot), tab, precision=lax.Precision.HIGHEST)
    if shift_far:
        vals = vals - tab[REL_BUCKETS - 1][None, :]
    valid = (dist >= 0) & (dist < window)
    vals = jnp.where(jnp.asarray(valid)[:, None], vals, NEG).T
    skew = jnp.tile(vals, (1, Q_BLOCK))[:, :Q_BLOCK * (length - 1)].reshape(N_HEADS, Q_BLOCK, length - 1)
    return skew[:, :, Q_BLOCK - 1:Q_BLOCK - 1 + width].reshape(N_HEADS * Q_BLOCK, width).astype(F32)


def _attention(proj, kvcmp, sinks, bias_table, bsz, seq):
    assert seq % SEL_CHUNK == 0
    nq = seq // Q_BLOCK
    n_far = seq // CMP_STRIDE
    n_sel = seq // SEL_BLOCK
    n_top = min(SEL_TOP_N, n_sel)
    assert n_top >= SEL_INIT_BLOCKS + SEL_LOCAL_BLOCKS and n_sel <= LANES
    wpad = Q_BLOCK * (-(-(NSA_WINDOW - 1) // Q_BLOCK))
    bpad = Q_BLOCK * (-(-(SWA_WINDOW - 1) // Q_BLOCK))
    pair = lambda tab: tab.astype(F32).reshape(REL_BUCKETS, N_GROUPS, GROUP).transpose(0, 2, 1).reshape(REL_BUCKETS, -1)
    tab_a = pair(bias_table[:, :N_HEADS])
    tab_b = pair(bias_table[:, N_HEADS:])
    near_pad = CMP_STRIDE * CMP_FRONT - (CMP_BLOCK - 1)
    t_near = _toeplitz_bias(tab_a, near_pad, CMP_STRIDE * CMP_NEAR, 1 << 30, True)[:, ::CMP_STRIDE]
    t_sel = _toeplitz_bias(tab_a, Q_BLOCK, 2 * Q_BLOCK, 1 << 30, True)
    t_win = _toeplitz_bias(tab_a, wpad, wpad + Q_BLOCK, NSA_WINDOW, False)
    t_swa = _toeplitz_bias(tab_b, bpad, bpad + Q_BLOCK, SWA_WINDOW, False)
    n_rows = kvcmp.shape[2]
    cn = (np.arange(n_rows) - CMP_FRONT)[:, None] * CMP_STRIDE
    sj = np.arange(LANES)[None, :] * SEL_BLOCK
    cmat = ((cn < sj + SEL_BLOCK) & (cn + CMP_BLOCK > sj) & (cn >= 0) & (cn + CMP_BLOCK <= seq)
            & (sj < seq)).astype(np.float32)
    cmat = jnp.asarray(cmat, F32)
    padded = lambda name, p: jnp.pad(proj[name].reshape(bsz, seq, LANES), ((0, 0), (p, 0), (0, 0)))
    ks, vs = padded('ks', Q_BLOCK), padded('vs', Q_BLOCK)
    kw, vw = padded('kw', wpad), padded('vw', wpad)
    kb, vb = padded('kb', bpad), padded('vb', bpad)
    rows = N_HEADS * Q_BLOCK
    qspec = pl.BlockSpec((Q_BLOCK, 4 * LANES), lambda b, i: (b * nq + i, 0))
    const2 = lambda shape: pl.BlockSpec(shape, lambda b, i: (0, 0))
    batch3 = lambda n: pl.BlockSpec((1, n, LANES), lambda b, i: (b, 0, 0))
    kernel = functools.partial(_attn_kernel, n_far=n_far, n_top=n_top)
    return pl.pallas_call(
        kernel,
        grid=(bsz, nq),
        in_specs=[pl.BlockSpec(memory_space=pltpu.SMEM),
                  qspec, qspec,
                  pl.BlockSpec((Q_BLOCK, LANES), lambda b, i: (b * nq + i, 0)),
                  pl.BlockSpec((1, 1, n_rows, LANES), lambda b, i: (0, b, 0, 0)),
                  pl.BlockSpec((1, 1, n_rows, LANES), lambda b, i: (1, b, 0, 0)),
                  batch3(seq + Q_BLOCK), batch3(seq + Q_BLOCK),
                  batch3(seq + wpad), batch3(seq + wpad),
                  batch3(seq + bpad), batch3(seq + bpad),
                  const2((n_rows, LANES)),
                  const2((rows, CMP_NEAR)),
                  const2((rows, 2 * Q_BLOCK)),
                  const2((rows, wpad + Q_BLOCK)),
                  const2((rows, bpad + Q_BLOCK))],
        out_specs=[qspec, qspec],
        out_shape=[jax.ShapeDtypeStruct((bsz * seq, 4 * LANES), BF16)] * 2,
        scratch_shapes=[pltpu.VMEM((rows, LANES), MXU_DTYPE),
                        pltpu.VMEM((rows, LANES), MXU_DTYPE),
                        pltpu.VMEM((N_GROUPS * Q_BLOCK, LANES), MXU_DTYPE),
                        pltpu.VMEM((N_GROUPS * Q_BLOCK, LANES), MXU_DTYPE),
                        pltpu.VMEM((rows, LANES), F32),
                        pltpu.VMEM((rows, 2 * LANES), F32),
                        pltpu.VMEM((2, rows, SEL_CHUNK), F32),
                        pltpu.VMEM((Q_BLOCK, 4 * LANES), F32)],
        compiler_params=pltpu.CompilerParams(dimension_semantics=("arbitrary", "arbitrary"),
                                             vmem_limit_bytes=VMEM_LIMIT),
        name="attention",
    )(sinks.astype(F32), proj['qa'], proj['qb'], proj['ga'], kvcmp, kvcmp, ks, vs, kw, vw, kb, vb,
      cmat, t_near, t_sel, t_win, t_swa)


def _layer_norm(y, g, b):
    mu = jnp.mean(y, axis=-1, keepdims=True)
    yc = y - mu
    var = jnp.mean(yc * yc, axis=-1, keepdims=True)
    return yc * lax.rsqrt(var + LN_EPS) * g + b


def _outproj_kernel(oa_ref, ob_ref, sg_ref, x_ref, pa_ref, pb_ref, wo_ref, g1_ref, b1_ref, wr_ref, rb_ref, sgu_ref,
                    sd_ref, tri_ref, h_ref, base_ref, eidx_ref, gate_ref, rank_ref, cnt_ref, carry):
    step = pl.program_id(0)
    tm = oa_ref.shape[0]

    @pl.when(step == 0)
    def _():
        carry[...] = jnp.zeros(carry.shape, F32)

    sg = sg_ref[...].astype(F32)
    merged = (sg[:, :D_MODEL] * _dot(_mx(oa_ref[...]), pa_ref[...])
              + sg[:, D_MODEL:] * _dot(_mx(ob_ref[...]), pb_ref[...]))
    mix = _dot(_mx(merged), wo_ref[...])
    h = _layer_norm(DN_ALPHA * x_ref[...] + mix, g1_ref[...], b1_ref[...])
    hb = _mx(h)
    bits = lax.bitcast_convert_type(h.astype(BF16).astype(F32), jnp.uint32)
    h_ref[...] = (bits[:, D_MODEL // 2:] & jnp.uint32(0xFFFF0000)) | (bits[:, :D_MODEL // 2] >> 16)

    gu = _dot(hb, sgu_ref[...])
    shared = _dot(_mx(jax.nn.silu(gu[:, :SHARED_HIDDEN]) * gu[:, SHARED_HIDDEN:]), sd_ref[...])
    base_ref[...] = DN_ALPHA * h + shared

    scores = jax.nn.sigmoid(_dot_nt(wr_ref[...], hb))
    choice = scores + rb_ref[:, 0:1]
    per_group = N_EXPERTS // N_EXPERT_GROUPS
    gs = []
    for g in range(N_EXPERT_GROUPS):
        cg = choice[g * per_group:(g + 1) * per_group]
        m1 = jnp.max(cg, axis=0, keepdims=True)
        is_m = cg == m1
        n_m = jnp.sum(is_m.astype(F32), axis=0, keepdims=True)
        m2 = jnp.max(jnp.where(is_m, -jnp.inf, cg), axis=0, keepdims=True)
        gs.append(m1 + jnp.where(n_m > 1.5, m1, m2))
    gs = jnp.concatenate(gs, axis=0)
    gid = lax.broadcasted_iota(jnp.int32, gs.shape, 0)
    beaten = jnp.zeros(gs.shape, jnp.int32)
    for g in range(N_EXPERT_GROUPS):
        other = gs[g:g + 1]
        beaten = beaten + ((other > gs) | ((other == gs) & (g < gid))).astype(jnp.int32)
    keep_g = beaten < TOPK_EXPERT_GROUPS
    keep = jnp.concatenate([jnp.broadcast_to(keep_g[g:g + 1], (per_group, tm)) for g in range(N_EXPERT_GROUPS)],
                           axis=0)
    cand = jnp.where(keep, choice, -jnp.inf)
    eid = lax.broadcasted_iota(jnp.int32, cand.shape, 0)
    hits = []
    e_rows = []
    w_rows = []
    for _ in range(TOP_K):
        m = jnp.max(cand, axis=0, keepdims=True)
        idx = jnp.min(jnp.where(cand == m, eid, N_EXPERTS), axis=0, keepdims=True)
        hit = eid == idx
        hits.append(hit)
        e_rows.append(idx)
        w_rows.append(jnp.sum(jnp.where(hit, scores, 0.0), axis=0, keepdims=True))
        cand = jnp.where(hit, -jnp.inf, cand)
    w = jnp.concatenate(w_rows, axis=0)
    gate_ref[...] = w / jnp.sum(w, axis=0, keepdims=True) * ROUTED_SCALE
    eidx_ref[...] = jnp.concatenate(e_rows, axis=0)

    onehot = jnp.zeros(cand.shape, F32)
    for hit in hits:
        onehot = onehot + hit.astype(F32)
    before = _dot(onehot.astype(BF16), tri_ref[...]) + carry[:, 0:1]
    rank_ref[...] = jnp.concatenate(
        [jnp.sum(jnp.where(hit, before, 0.0), axis=0, keepdims=True) for hit in hits], axis=0).astype(jnp.int32)
    carry[...] = carry[...] + jnp.sum(onehot, axis=1, keepdims=True)
    cnt_ref[...] = carry[...]


def _out_projection(oa, ob, sg, x2, proj_a, proj_b, w_out, ln_g, ln_b, w_router, router_bias, s_gate, s_up, s_down):
    t = x2.shape[0]
    tm = OUT_TM
    pair_rows = lambda p: p.reshape(N_GROUPS, GROUP, HEAD_DIM, -1).transpose(1, 0, 2, 3).reshape(p.shape)
    pa = pair_rows(proj_a).astype(MXU_DTYPE)
    pb = pair_rows(proj_b).astype(MXU_DTYPE)
    tri = jnp.asarray(np.triu(np.ones((tm, tm), np.float32), 1), BF16)
    row = lambda i: (i, 0)
    fixed = lambda i: (0, 0)
    col = lambda i: (0, i)
    outs = pl.pallas_call(
        _outproj_kernel,
        grid=(t // tm,),
        in_specs=[pl.BlockSpec((tm, 4 * LANES), row), pl.BlockSpec((tm, 4 * LANES), row),
                  pl.BlockSpec((tm, 2 * D_MODEL), row), pl.BlockSpec((tm, D_MODEL), row),
                  pl.BlockSpec((4 * LANES, D_MODEL), fixed), pl.BlockSpec((4 * LANES, D_MODEL), fixed),
                  pl.BlockSpec((D_MODEL, D_MODEL), fixed),
                  pl.BlockSpec((1, D_MODEL), fixed), pl.BlockSpec((1, D_MODEL), fixed),
                  pl.BlockSpec((N_EXPERTS, D_MODEL), fixed), pl.BlockSpec((N_EXPERTS, LANES), fixed),
                  pl.BlockSpec((D_MODEL, 2 * SHARED_HIDDEN), fixed), pl.BlockSpec((SHARED_HIDDEN, D_MODEL), fixed),
                  pl.BlockSpec((tm, tm), fixed)],
        out_specs=[pl.BlockSpec((tm, D_MODEL // 2), row), pl.BlockSpec((tm, D_MODEL), row),
                   pl.BlockSpec((TOP_K, tm), col), pl.BlockSpec((TOP_K, tm), col), pl.BlockSpec((TOP_K, tm), col),
                   pl.BlockSpec((N_EXPERTS, LANES), fixed)],
        out_shape=[jax.ShapeDtypeStruct((t, D_MODEL // 2), jnp.uint32), jax.ShapeDtypeStruct((t, D_MODEL), F32),
                   jax.ShapeDtypeStruct((TOP_K, t), jnp.int32), jax.ShapeDtypeStruct((TOP_K, t), F32),
                   jax.ShapeDtypeStruct((TOP_K, t), jnp.int32), jax.ShapeDtypeStruct((N_EXPERTS, LANES), F32)],
        scratch_shapes=[pltpu.VMEM((N_EXPERTS, LANES), F32)],
        compiler_params=pltpu.CompilerParams(dimension_semantics=("arbitrary",), vmem_limit_bytes=VMEM_LIMIT),
        name="out_projection_router",
    )(oa, ob, sg, x2, pa, pb, w_out.astype(MXU_DTYPE), ln_g.reshape(1, -1), ln_b.reshape(1, -1),
      w_router.T.astype(MXU_DTYPE), jnp.broadcast_to(router_bias.astype(F32)[:, None], (N_EXPERTS, LANES)),
      jnp.concatenate([s_gate, s_up], axis=1).astype(MXU_DTYPE), s_down.astype(MXU_DTYPE), tri)
    return outs


def _rows_to_tiles(x):
    return pltpu.einshape("cml->mcl", jnp.stack(_lane_tiles(x), axis=0))


def _tiles_to_rows(x3):
    xt = pltpu.einshape("mcl->cml", x3)
    return jnp.concatenate([xt[c] for c in range(xt.shape[0])], axis=1)


def _dispatch_kernel(zstart_ref, cnt_ref, dest_ref, h2_ref, xs_ref, h_ref, zeros, sem):
    step = pl.program_id(0)
    tm = h_ref.shape[0]
    h_ref[...] = _rows_to_tiles(h2_ref[...])

    @pl.when(step == 0)
    def _():
        zeros[...] = jnp.zeros(zeros.shape, zeros.dtype)

        def fill(e, c):
            @pl.when(cnt_ref[e] > 0)
            def _():
                cp = pltpu.make_async_copy(zeros, xs_ref.at[pl.ds(zstart_ref[e], MOE_BM)], sem)
                cp.start()
                cp.wait()
            return c
        lax.fori_loop(0, N_EXPERTS, fill, 0)

    def issue(t, c):
        for k in range(TOP_K):
            pltpu.make_async_copy(h_ref.at[t], xs_ref.at[dest_ref[k, t]], sem).start(priority=k % 2)
        return c
    lax.fori_loop(0, tm, issue, 0)
    for k in range(TOP_K):
        pltpu.make_async_copy(h_ref, xs_ref.at[pl.ds(0, tm)], sem).wait()


def _dispatch(h, dest, zstart, counts, n_rows):
    t = h.shape[0]
    tm = DISP_TM
    return pl.pallas_call(
        _dispatch_kernel,
        grid_spec=pltpu.PrefetchScalarGridSpec(
            num_scalar_prefetch=2,
            grid=(t // tm,),
            in_specs=[pl.BlockSpec((TOP_K, tm), lambda i, *_: (0, i), memory_space=pltpu.SMEM),
                      pl.BlockSpec((tm, D_MODEL // 2), lambda i, *_: (i, 0))],
            out_specs=pl.BlockSpec(memory_space=pl.ANY),
            scratch_shapes=[pltpu.VMEM((tm,) + PACKED_ROW_TILE, jnp.uint32),
                            pltpu.VMEM((MOE_BM,) + PACKED_ROW_TILE, jnp.uint32),
                            pltpu.SemaphoreType.DMA(())]),
        out_shape=jax.ShapeDtypeStruct((n_rows,) + PACKED_ROW_TILE, jnp.uint32),
        compiler_params=pltpu.CompilerParams(dimension_semantics=("arbitrary",), vmem_limit_bytes=VMEM_LIMIT),
        name="moe_dispatch",
    )(zstart, counts, dest, h)


def _experts_kernel(blk_e_ref, nused_ref, xs_ref, wg_ref, wu_ref, wd_ref, ys_ref, wg_s, wu_s, wd_s):
    b = pl.program_id(0)
    prev = blk_e_ref[jnp.maximum(b - 1, 0)]

    @pl.when((b == 0) | (blk_e_ref[b] != prev))
    def _():
        wg_s[...] = _mx(wg_ref[0])
        wu_s[...] = _mx(wu_ref[0])
        wd_s[...] = _mx(wd_ref[0])

    n_tiles = D_MODEL // LANES

    @pl.when(b < nused_ref[0])
    def _():
        packed = _tiles_to_rows(xs_ref[...])
        low = lax.bitcast_convert_type(packed << 16, F32)
        high = lax.bitcast_convert_type(packed & jnp.uint32(0xFFFF0000), F32)
        xb = _mx(jnp.concatenate([low, high], axis=1))
        hid = jax.nn.silu(_dot(xb, wg_s[...])) * _dot(xb, wu_s[...])
        y = _dot(_mx(hid), wd_s[...])
        yt = jnp.stack([y[:, c * LANES:(c + 1) * LANES] for c in range(n_tiles)], axis=0)
        ys_ref[...] = pltpu.einshape("cml->mcl", yt)

    @pl.when(b >= nused_ref[0])
    def _():
        ys_ref[...] = jnp.zeros(ys_ref.shape, F32)


def _experts(xs, blk_e, nused, e_gate, e_up, e_down):
    n_rows = xs.shape[0]
    n_blocks = n_rows // MOE_BM
    xmap = lambda b, be, nu: (jnp.minimum(b, nu[0] - 1), 0, 0)
    wmap = lambda b, be, nu: (be[b], 0, 0)
    return pl.pallas_call(
        _experts_kernel,
        grid_spec=pltpu.PrefetchScalarGridSpec(
            num_scalar_prefetch=2,
            grid=(n_blocks,),
            in_specs=[pl.BlockSpec((MOE_BM,) + PACKED_ROW_TILE, xmap),
                      pl.BlockSpec((1, D_MODEL, EXPERT_HIDDEN), wmap),
                      pl.BlockSpec((1, D_MODEL, EXPERT_HIDDEN), wmap),
                      pl.BlockSpec((1, EXPERT_HIDDEN, D_MODEL), wmap)],
            out_specs=pl.BlockSpec((MOE_BM,) + ROW_TILE, lambda b, be, nu: (b, 0, 0)),
            scratch_shapes=[pltpu.VMEM((D_MODEL, EXPERT_HIDDEN), MXU_DTYPE),
                            pltpu.VMEM((D_MODEL, EXPERT_HIDDEN), MXU_DTYPE),
                            pltpu.VMEM((EXPERT_HIDDEN, D_MODEL), MXU_DTYPE)]),
        out_shape=jax.ShapeDtypeStruct((n_rows,) + ROW_TILE, F32),
        compiler_params=pltpu.CompilerParams(dimension_semantics=("arbitrary",), vmem_limit_bytes=VMEM_LIMIT),
        name="moe_experts",
    )(blk_e, nused, xs, e_gate, e_up, e_down)


def _combine_kernel(dest_ref, dest_next_ref, gate_ref, base_ref, g2_ref, b2_ref, ys_ref, out_ref, buf, routed, sem):
    step = pl.program_id(0)
    tm = base_ref.shape[0]
    slot = step % 2

    def issue(d_ref, s):
        def body(t, c):
            for k in range(TOP_K):
                pltpu.make_async_copy(ys_ref.at[d_ref[k, t]], buf.at[s, k, t], sem.at[s]).start(priority=k % 2)
            return c
        lax.fori_loop(0, tm, body, 0)

    @pl.when(step == 0)
    def _():
        issue(dest_ref, 0)

    @pl.when(step + 1 < pl.num_programs(0))
    def _():
        issue(dest_next_ref, 1 - slot)

    for k in range(TOP_K):
        pltpu.make_async_copy(ys_ref.at[pl.ds(0, tm)], buf.at[slot, k], sem.at[slot]).wait()

    def token(t, c):
        y = gate_ref[0, t] * buf[slot, 0, t]
        for k in range(1, TOP_K):
            y = y + gate_ref[k, t] * buf[slot, k, t]
        routed[t] = y
        return c
    lax.fori_loop(0, tm, token, 0)
    out_ref[...] = _layer_norm(base_ref[...] + _tiles_to_rows(routed[...]), g2_ref[...], b2_ref[...])


def _combine(ys3, dest, gate, base, ln_g, ln_b):
    t = base.shape[0]
    tm = COMB_TM
    n_tiles = t // tm
    return pl.pallas_call(
        _combine_kernel,
        grid=(n_tiles,),
        in_specs=[pl.BlockSpec((TOP_K, tm), lambda i: (0, i), memory_space=pltpu.SMEM),
                  pl.BlockSpec((TOP_K, tm), lambda i: (0, jnp.minimum(i + 1, n_tiles - 1)), memory_space=pltpu.SMEM),
                  pl.BlockSpec((TOP_K, tm), lambda i: (0, i), memory_space=pltpu.SMEM),
                  pl.BlockSpec((tm, D_MODEL), lambda i: (i, 0)),
                  pl.BlockSpec((1, D_MODEL), lambda i: (0, 0)),
                  pl.BlockSpec((1, D_MODEL), lambda i: (0, 0)),
                  pl.BlockSpec(memory_space=pl.ANY)],
        out_specs=pl.BlockSpec((tm, D_MODEL), lambda i: (i, 0)),
        out_shape=jax.ShapeDtypeStruct((t, D_MODEL), F32),
        scratch_shapes=[pltpu.VMEM((2, TOP_K, tm) + ROW_TILE, F32), pltpu.VMEM((tm,) + ROW_TILE, F32),
                        pltpu.SemaphoreType.DMA((2,))],
        compiler_params=pltpu.CompilerParams(dimension_semantics=("arbitrary",), vmem_limit_bytes=VMEM_LIMIT),
        name="moe_combine",
    )(dest, dest, gate, base, ln_g.reshape(1, -1), ln_b.reshape(1, -1), ys3)


def _moe_layout(eidx, rank, counts):
    n_assign = eidx.size
    n_blocks = (n_assign + N_EXPERTS * (MOE_BM - 1)) // MOE_BM
    padded = (counts + MOE_BM - 1) // MOE_BM * MOE_BM
    pends = jnp.cumsum(padded)
    pstarts = pends - padded
    experts = jnp.arange(N_EXPERTS, dtype=jnp.int32)
    dest = jnp.sum(jnp.where(eidx[..., None] == experts, pstarts, 0), axis=-1) + rank
    block_row = jnp.arange(n_blocks, dtype=jnp.int32) * MOE_BM
    blk_e = jnp.minimum(jnp.sum(pends[None, :] <= block_row[:, None], axis=1), N_EXPERTS - 1).astype(jnp.int32)
    nused = (pends[-1:] // MOE_BM).astype(jnp.int32)
    zstart = jnp.maximum(pends - MOE_BM, 0).astype(jnp.int32)
    return dest.astype(jnp.int32), blk_e, nused, zstart, n_blocks * MOE_BM


def _layer(x, w_in, cmp_pe, cmp_w1, cmp_b1, cmp_w2, sinks, bias_table, proj_a, proj_b, w_out, ln1_g, ln1_b,
           w_router, router_bias, e_gate, e_up, e_down, s_gate, s_up, s_down, ln2_g, ln2_b):
    bsz, seq, d = x.shape
    x2 = x.reshape(bsz * seq, d)
    proj = _in_projection(x2, w_in)
    kvcmp = _compress(proj['kc'], proj['vc'], bsz, seq, cmp_pe, cmp_w1, cmp_b1, cmp_w2)
    oa, ob = _attention(proj, kvcmp, sinks, bias_table, bsz, seq)
    h, base, eidx, gate, rank, cnt = _out_projection(oa, ob, proj['sg'], x2, proj_a, proj_b, w_out, ln1_g, ln1_b,
                                                     w_router, router_bias, s_gate, s_up, s_down)
    counts = cnt[:, 0].astype(jnp.int32)
    dest, blk_e, nused, zstart, n_rows = _moe_layout(eidx, rank, counts)
    xs = _dispatch(h, dest, zstart, counts, n_rows)
    ys = _experts(xs, blk_e, nused, e_gate, e_up, e_down)
    out = _combine(ys, dest, gate, base, ln2_g, ln2_b)
    return out.reshape(bsz, seq, d)


def kernel(x, w_in, cmp_pe, cmp_w1, cmp_b1, cmp_w2, attn_sinks, rel_bias_table, proj_a, proj_b, w_out, ln1_g, ln1_b,
           w_router, router_bias, expert_w_gate, expert_w_up, expert_w_down, shared_w_gate, shared_w_up,
           shared_w_down, ln2_g, ln2_b):
    h = x
    for l in range(DEPTH):
        h = _layer(h, w_in[l], cmp_pe[l], cmp_w1[l], cmp_b1[l], cmp_w2[l], attn_sinks[l], rel_bias_table, proj_a[l],
                   proj_b[l], w_out[l], ln1_g[l], ln1_b[l], w_router[l], router_bias[l], expert_w_gate[l],
                   expert_w_up[l], expert_w_down[l], shared_w_gate[l], shared_w_up[l], shared_w_down[l], ln2_g[l],
                   ln2_b[l])
    return h
```

```python
import functools
import math

import numpy as np
import jax
import jax.numpy as jnp
from jax import lax
from jax.experimental import pallas as pl
from jax.experimental.pallas import tpu as pltpu

F32 = jnp.float32
BF16 = jnp.bfloat16
MXU_DTYPE = jnp.bfloat16

D_MODEL = 1024
HEAD_DIM = 64
ATTN_SCALE = HEAD_DIM ** -0.5
Q_BLOCK = 128
N_HEADS = 8
N_GROUPS = 2
GROUP = 4
CMP_BLOCK = 32
CMP_STRIDE = 16
CMP_HIDDEN = 128
SEL_BLOCK = 64
SEL_TOP_N = 8
SEL_INIT_BLOCKS = 1
SEL_LOCAL_BLOCKS = 2
NSA_WINDOW = 512
SWA_WINDOW = 128
REL_BUCKETS = 32
REL_MAX_DIST = 128
N_EXPERTS = 256
TOP_K = 8
EXPERT_HIDDEN = 256
SHARED_HIDDEN = 256
N_EXPERT_GROUPS = 8
TOPK_EXPERT_GROUPS = 4
ROUTED_SCALE = 2.5
LN_EPS = 1e-5
DEPTH = 1
DN_ALPHA = (2 * DEPTH) ** 0.25

NEG = -1e30
LANES = 128
ROW_TILE = (8, LANES)
PACKED_ROW_TILE = (4, LANES)
CMP_FRONT = 16
CMP_NEAR = LANES
SEL_CHUNK = 512
VMEM_LIMIT = 56 * 1024 * 1024

IN_TM = 512
OUT_TM = 256
MOE_BM = 256
DISP_TM = 256
COMB_TM = 128


def _dot(a, b):
    return jnp.dot(a, b, preferred_element_type=F32)


def _dot_nt(a, b):
    return lax.dot_general(a, b, (((1,), (1,)), ((), ())), preferred_element_type=F32)


def _mx(a):
    return a.astype(MXU_DTYPE)


def _pack_bf16_pairs(x):
    half = x.shape[1] // 2
    bits = lax.bitcast_convert_type(x.astype(BF16).astype(F32), jnp.uint32)
    return (bits[:, half:] & jnp.uint32(0xFFFF0000)) | (bits[:, :half] >> 16)


def _unpack_bf16_pairs(words):
    return (lax.bitcast_convert_type(words << 16, F32),
            lax.bitcast_convert_type(words & jnp.uint32(0xFFFF0000), F32))


_IN_COLS = (('qa', 512), ('qb', 512), ('kc', 128), ('vc', 128), ('ks', 128), ('vs', 128), ('kw', 128),
            ('vw', 128), ('kb', 128), ('vb', 128), ('ga', 128), ('sg', 2048))


def _inproj_kernel(x_ref, w_ref, qa_ref, qb_ref, kc_ref, vc_ref, ks_ref, vs_ref, kw_ref, vw_ref, kb_ref, vb_ref,
                   ga_ref, sg_ref):
    xb = _mx(x_ref[...])
    outs = dict(qa=qa_ref, qb=qb_ref, kc=kc_ref, vc=vc_ref, ks=ks_ref, vs=vs_ref, kw=kw_ref, vw=vw_ref,
                kb=kb_ref, vb=vb_ref, ga=ga_ref, sg=sg_ref)
    off = 0
    for name, width in _IN_COLS:
        for c0 in range(0, width, 512):
            cw = min(512, width - c0)
            y = _dot(xb, w_ref[:, off + c0:off + c0 + cw])
            if name in ('ga', 'sg'):
                y = jax.nn.sigmoid(y)
            outs[name][:, c0:c0 + cw] = y.astype(outs[name].dtype)
        off += width


def _pair_head_columns(w):
    return w.reshape(w.shape[0], N_GROUPS, GROUP, HEAD_DIM).transpose(0, 2, 1, 3).reshape(w.shape[0], -1)


def _in_projection(x2, w_in):
    t = x2.shape[0]
    sizes = (512, 128, 128, 128, 128, 128, 128, 24, 512, 128, 128, 1024, 1024)
    offs = np.cumsum((0,) + sizes)
    part = [w_in[:, offs[k]:offs[k + 1]] for k in range(len(sizes))]
    w_qa, w_kc, w_vc, w_ks, w_vs, w_kw, w_vw, w_g, w_qb, w_kb, w_vb, w_gate_a, w_gate_b = part
    w_qa = _pair_head_columns(w_qa) * ATTN_SCALE
    w_qb = _pair_head_columns(w_qb) * ATTN_SCALE
    w_ga = w_g.reshape(-1, N_GROUPS, GROUP, 3).transpose(0, 3, 2, 1).reshape(-1, 24)
    w_ga = jnp.pad(w_ga, ((0, 0), (0, LANES - 24)))
    w_all = jnp.concatenate([w_qa, w_qb, w_kc, w_vc, w_ks, w_vs, w_kw, w_vw, w_kb, w_vb, w_ga, w_gate_a, w_gate_b],
                            axis=1).astype(MXU_DTYPE)
    n_all = w_all.shape[1]
    out_shape = []
    out_specs = []
    for name, width in _IN_COLS:
        dt = F32 if name == 'ga' else BF16
        out_shape.append(jax.ShapeDtypeStruct((t, width), dt))
        out_specs.append(pl.BlockSpec((IN_TM, width), lambda i: (i, 0)))
    outs = pl.pallas_call(
        _inproj_kernel,
        grid=(t // IN_TM,),
        in_specs=[pl.BlockSpec((IN_TM, D_MODEL), lambda i: (i, 0)),
                  pl.BlockSpec((D_MODEL, n_all), lambda i: (0, 0))],
        out_specs=out_specs,
        out_shape=out_shape,
        compiler_params=pltpu.CompilerParams(dimension_semantics=("arbitrary",), vmem_limit_bytes=VMEM_LIMIT),
        name="in_projection",
    )(x2, w_all)
    return dict(zip([n for n, _ in _IN_COLS], outs))


def _compress_kernel(tok_ref, w1_ref, pe_ref, w1o_ref, b1_ref, w2_ref, out_ref):
    n_chunks = tok_ref.shape[2]
    ab = _dot(tok_ref[0, 0], w1_ref[0])
    a = ab[:, :2 * CMP_HIDDEN]
    b_next = pltpu.roll(ab[:, 2 * CMP_HIDDEN:], n_chunks - 1, 0)
    cb = _dot(_mx(pe_ref[0]), _mx(w1o_ref[0]))[0:1, :] + b1_ref[0]
    cb2 = jnp.concatenate([cb, cb], axis=1)
    hid = jax.nn.gelu(a + b_next + cb2)
    out = _dot(_mx(hid), w2_ref[0])
    row = lax.broadcasted_iota(jnp.int32, out.shape, 0)
    out = jnp.where(row < n_chunks - 1, out, 0.0)
    out_ref[0, 0, 0:CMP_FRONT, :] = jnp.zeros((CMP_FRONT, LANES), F32)
    out_ref[0, 0, CMP_FRONT:CMP_FRONT + n_chunks, :] = out
    out_ref[0, 0, CMP_FRONT + n_chunks:, :] = jnp.zeros((CMP_NEAR - CMP_FRONT, LANES), F32)


def _compress(kc, vc, bsz, seq, cmp_pe, cmp_w1, cmp_b1, cmp_w2):
    n_chunks = seq // CMP_STRIDE
    tok = jnp.stack([kc, vc]).reshape(2, bsz, n_chunks, CMP_STRIDE * LANES)
    eye = jnp.eye(N_GROUPS, dtype=F32)
    w1r = cmp_w1.reshape(2, 2, CMP_STRIDE, HEAD_DIM, CMP_HIDDEN)
    w1 = jnp.einsum('khjdn,gG->kjgdhGn', w1r, eye).reshape(2, CMP_STRIDE * LANES, 4 * CMP_HIDDEN).astype(MXU_DTYPE)
    w2 = jnp.einsum('knd,gG->kgnGd', cmp_w2, eye).reshape(2, 2 * CMP_HIDDEN, LANES).astype(MXU_DTYPE)
    pe = jnp.pad(cmp_pe.reshape(2, 1, CMP_BLOCK * HEAD_DIM), ((0, 0), (0, 7), (0, 0)))
    b1 = cmp_b1.reshape(2, 1, CMP_HIDDEN)
    rows = CMP_FRONT + n_chunks + CMP_NEAR - CMP_FRONT
    return pl.pallas_call(
        _compress_kernel,
        grid=(2, bsz),
        in_specs=[pl.BlockSpec((1, 1, n_chunks, CMP_STRIDE * LANES), lambda k, b: (k, b, 0, 0)),
                  pl.BlockSpec((1, CMP_STRIDE * LANES, 4 * CMP_HIDDEN), lambda k, b: (k, 0, 0)),
                  pl.BlockSpec((1, 8, CMP_BLOCK * HEAD_DIM), lambda k, b: (k, 0, 0)),
                  pl.BlockSpec((1, CMP_BLOCK * HEAD_DIM, CMP_HIDDEN), lambda k, b: (k, 0, 0)),
                  pl.BlockSpec((1, 1, CMP_HIDDEN), lambda k, b: (k, 0, 0)),
                  pl.BlockSpec((1, 2 * CMP_HIDDEN, LANES), lambda k, b: (k, 0, 0))],
        out_specs=pl.BlockSpec((1, 1, rows, LANES), lambda k, b: (k, b, 0, 0)),
        out_shape=jax.ShapeDtypeStruct((2, bsz, rows, LANES), F32),
        compiler_params=pltpu.CompilerParams(dimension_semantics=("arbitrary", "arbitrary"),
                                             vmem_limit_bytes=VMEM_LIMIT),
        name="nsa_compress",
    )(tok, w1, pe, cmp_w1, b1, w2)


def _stack_heads(q_ref, dst):
    lo = lax.broadcasted_iota(jnp.int32, (Q_BLOCK, LANES), 1) < HEAD_DIM
    for r in range(GROUP):
        qr = q_ref[:, r * LANES:(r + 1) * LANES].astype(dst.dtype)
        z = jnp.zeros_like(qr)
        dst[(2 * r) * Q_BLOCK:(2 * r + 1) * Q_BLOCK, :] = jnp.where(lo, qr, z)
        dst[(2 * r + 1) * Q_BLOCK:(2 * r + 2) * Q_BLOCK, :] = jnp.where(lo, z, qr)


def _pair_heads(o, r):
    lo = lax.broadcasted_iota(jnp.int32, (Q_BLOCK, LANES), 1) < HEAD_DIM
    return jnp.where(lo, o[(2 * r) * Q_BLOCK:(2 * r + 1) * Q_BLOCK], o[(2 * r + 1) * Q_BLOCK:(2 * r + 2) * Q_BLOCK])


def _lane_tiles(x):
    return [x[:, t * LANES:(t + 1) * LANES] for t in range(x.shape[1] // LANES)]


def _row_max(tiles):
    mx = tiles[0]
    for t in tiles[1:]:
        mx = jnp.maximum(mx, t)
    return jnp.broadcast_to(jnp.max(mx, axis=1, keepdims=True), mx.shape)


def _with_ones(v):
    return jnp.concatenate([v, jnp.ones(v.shape, v.dtype)], axis=1)


def _block_of_key(n_keys, first_block):
    b = lax.broadcasted_iota(jnp.int32, (LANES, n_keys), 0)
    k = lax.broadcasted_iota(jnp.int32, (LANES, n_keys), 1)
    return (b == (k // SEL_BLOCK) + first_block).astype(MXU_DTYPE)


def _select_blocks_t(imp_t, i, n_top):
    blk = lax.broadcasted_iota(jnp.int32, imp_t.shape, 0)
    qcol = lax.broadcasted_iota(jnp.int32, imp_t.shape, 1)
    back = (2 * i + (qcol >= SEL_BLOCK).astype(jnp.int32)) - blk
    sel = (back >= 0) & ((blk < SEL_INIT_BLOCKS) | (back < SEL_LOCAL_BLOCKS))
    cand = jnp.where((back >= SEL_LOCAL_BLOCKS) & (blk >= SEL_INIT_BLOCKS), imp_t, -1.0)
    blk_f = blk.astype(F32)
    for _ in range(n_top - SEL_INIT_BLOCKS - SEL_LOCAL_BLOCKS):
        m = jnp.max(cand, axis=0, keepdims=True)
        idx = jnp.min(jnp.where(cand == m, blk_f, float(LANES)), axis=0, keepdims=True)
        hit = blk_f == idx
        sel = sel | (hit & (m >= 0.0))
        cand = jnp.where(hit, -2.0, cand)
    return sel


def _attn_kernel(sink_ref, qa_ref, qb_ref, ga_ref, kcmp_ref, vcmp_ref, ks_ref, vs_ref, kw_ref, vw_ref, kb_ref,
                 vb_ref, cmat_ref, tnear_ref, tsel_ref, twin_ref, tswa_ref, oa_ref, ob_ref,
                 qall, qball, mneg, mneg_far, m_s, acc_s, s_buf, oa_acc, *, n_far, n_top):
    i = pl.program_id(1)
    rows = N_HEADS * Q_BLOCK
    half = rows // 2
    halves = (slice(0, half), slice(half, rows))
    _stack_heads(qa_ref, qall)
    _stack_heads(qb_ref, qball)
    nstart = pl.multiple_of(i * Q_BLOCK, Q_BLOCK)
    lo = lax.broadcasted_iota(jnp.int32, (Q_BLOCK, LANES), 1) < HEAD_DIM
    gates = ga_ref[...]

    def gate_tile(c, r):
        return jnp.where(lo, gates[:, c * 8 + 2 * r:c * 8 + 2 * r + 1], gates[:, c * 8 + 2 * r + 1:c * 8 + 2 * r + 2])

    def softmax_pv(s_tiles, v1, fix_max=None):
        m = _row_max(s_tiles)
        if fix_max is not None:
            m = fix_max(m)
        e = [jnp.exp(t - m) for t in s_tiles]
        return e, m, _dot(_mx(jnp.concatenate(e, axis=1)), v1)

    off = pl.multiple_of(i * (Q_BLOCK // CMP_STRIDE), 8)
    k_cmp = _mx(jnp.concatenate([kcmp_ref[0, 0, 0:n_far, :], kcmp_ref[0, 0, pl.ds(off, CMP_NEAR), :]], axis=0))
    v_cmp = _with_ones(_mx(jnp.concatenate([vcmp_ref[0, 0, 0:n_far, :], vcmp_ref[0, 0, pl.ds(off, CMP_NEAR), :]],
                                           axis=0)))
    colf = lax.broadcasted_iota(jnp.int32, (1, n_far), 1)
    coln = lax.broadcasted_iota(jnp.int32, (1, CMP_NEAR), 1)
    col_ok = jnp.concatenate([(colf >= CMP_FRONT) & (colf < off), coln + off >= CMP_FRONT], axis=1)
    mask_c = jnp.where(col_ok, 0.0, NEG)
    no_key = lambda m: jnp.where(m > 0.5 * NEG, m, 0.0)
    p_cmp, o_c = [], []
    for rs in halves:
        tiles = _lane_tiles(_dot_nt(qall[rs, :], k_cmp) + mask_c)
        tiles[-1] = tiles[-1] + tnear_ref[rs, :]
        e, _, ov = softmax_pv(tiles, v_cmp, no_key)
        inv = 1.0 / jnp.maximum(ov[:, LANES:], 1e-30)
        o_c.append(ov[:, :LANES] * inv)
        p_cmp.append([t * inv for t in e])
    o_c = jnp.concatenate(o_c, axis=0)

    def far_start(j):
        return pl.multiple_of(Q_BLOCK + j * SEL_CHUNK, Q_BLOCK)

    def far_logits(j, slot, masked):
        kc = _mx(ks_ref[0, pl.ds(far_start(j), SEL_CHUNK), :])
        if masked:
            madd = _dot(mneg_far[...], _block_of_key(SEL_CHUNK, j * (SEL_CHUNK // SEL_BLOCK)))
        for rs in halves:
            s = _dot_nt(qall[rs, :], kc)
            s_buf[slot, rs, :] = s + jnp.concatenate([madd] * (GROUP // 2), axis=0) if masked else s

    far_logits(0, 0, False)

    blkcol = lax.broadcasted_iota(jnp.int32, (Q_BLOCK, LANES), 1)
    n_tiles = len(p_cmp[0])
    for g in range(N_GROUPS):
        imp = jnp.zeros((Q_BLOCK, LANES), F32)
        for t in range(n_tiles):
            pg = sum(p_cmp[r // 2][t][(2 * (r % 2) + g) * Q_BLOCK:(2 * (r % 2) + g + 1) * Q_BLOCK]
                     for r in range(GROUP))
            if t < n_tiles - 1:
                cm = _mx(cmat_ref[t * LANES:(t + 1) * LANES, :])
            else:
                cm = _mx(cmat_ref[pl.ds(off, CMP_NEAR), :])
            hi = _mx(pg)
            low = _mx(pg - hi.astype(F32))
            imp = imp + _dot(hi, cm) + _dot(low, cm)
        sel = _select_blocks_t(imp.T, i, n_top)
        neg = jnp.where(sel, 0.0, NEG).T
        mneg[g * Q_BLOCK:(g + 1) * Q_BLOCK, :] = neg.astype(mneg.dtype)
        mneg_far[g * Q_BLOCK:(g + 1) * Q_BLOCK, :] = jnp.where(blkcol < 2 * (i - 1), neg, NEG).astype(mneg.dtype)

    wpad = kw_ref.shape[1] - ks_ref.shape[1] + Q_BLOCK
    kwin = _mx(kw_ref[0, pl.ds(nstart, wpad + Q_BLOCK), :])
    vwin = _with_ones(_mx(vw_ref[0, pl.ds(nstart, wpad + Q_BLOCK), :]))
    colw = lax.broadcasted_iota(jnp.int32, (1, wpad + Q_BLOCK), 1)
    mask_w = jnp.where(colw + nstart >= wpad, 0.0, NEG)
    o_w = []
    for rs in halves:
        _, _, ov = softmax_pv(_lane_tiles(_dot_nt(qall[rs, :], kwin) + twin_ref[rs, :] + mask_w), vwin)
        o_w.append(ov[:, :LANES] / ov[:, LANES:])
    o_w = jnp.concatenate(o_w, axis=0)
    for r in range(GROUP):
        oa_acc[:, r * LANES:(r + 1) * LANES] = (gate_tile(0, r) * _pair_heads(o_c, r)
                                                + gate_tile(2, r) * _pair_heads(o_w, r))

    bpad = kb_ref.shape[1] - ks_ref.shape[1] + Q_BLOCK
    kwin = _mx(kb_ref[0, pl.ds(nstart, bpad + Q_BLOCK), :])
    vwin = _with_ones(_mx(vb_ref[0, pl.ds(nstart, bpad + Q_BLOCK), :]))
    colb = lax.broadcasted_iota(jnp.int32, (1, bpad + Q_BLOCK), 1)
    mask_b = jnp.where(colb + nstart >= bpad, 0.0, NEG)
    o_b = []
    for hh, rs in enumerate(halves):
        sink = jnp.concatenate([jnp.full((Q_BLOCK, LANES), sink_ref[(h % 2) * GROUP + h // 2], F32)
                                for h in range(hh * N_HEADS // 2, (hh + 1) * N_HEADS // 2)], axis=0)
        _, m, ov = softmax_pv(_lane_tiles(_dot_nt(qball[rs, :], kwin) + tswa_ref[rs, :] + mask_b), vwin,
                              lambda m: jnp.maximum(m, sink))
        o_b.append(ov[:, :LANES] / (ov[:, LANES:] + jnp.exp(sink - m)))
    o_b = jnp.concatenate(o_b, axis=0)
    for r in range(GROUP):
        ob_ref[:, r * LANES:(r + 1) * LANES] = _pair_heads(o_b, r).astype(ob_ref.dtype)

    m_s[...] = jnp.full(m_s.shape, NEG, F32)
    acc_s[...] = jnp.zeros(acc_s.shape, F32)

    def flash_update(rs, s, v1):
        s_tiles = _lane_tiles(s)
        m_old = m_s[rs, :]
        m_new = jnp.maximum(m_old, _row_max(s_tiles))
        alpha = jnp.exp(m_old - m_new)
        p = jnp.concatenate([jnp.exp(t - m_new) for t in s_tiles], axis=1)
        acc_s[rs, :] = jnp.concatenate([alpha, alpha], axis=1) * acc_s[rs, :] + _dot(_mx(p), v1)
        m_s[rs, :] = m_new

    n_far_keys = jnp.maximum(i - 1, 0) * Q_BLOCK
    n_chunks = (n_far_keys + SEL_CHUNK - 1) // SEL_CHUNK

    madd = _dot(mneg_far[...], _block_of_key(SEL_CHUNK, 0))
    for rs in halves:
        s_buf[0, rs, :] = s_buf[0, rs, :] + jnp.concatenate([madd] * (GROUP // 2), axis=0)

    def far_body(j, carry):
        v1 = _with_ones(_mx(vs_ref[0, pl.ds(far_start(j), SEL_CHUNK), :]))
        for rs in halves:
            flash_update(rs, s_buf[j % 2, rs, :], v1)
        far_logits(jnp.minimum(j + 1, n_chunks - 1), (j + 1) % 2, True)
        return carry

    lax.fori_loop(0, n_chunks, far_body, 0)
    kc = _mx(ks_ref[0, pl.ds(nstart, 2 * Q_BLOCK), :])
    v1 = _with_ones(_mx(vs_ref[0, pl.ds(nstart, 2 * Q_BLOCK), :]))
    madd = _dot(mneg[...], _block_of_key(2 * Q_BLOCK, 2 * (i - 1)))
    col2 = lax.broadcasted_iota(jnp.int32, (1, 2 * Q_BLOCK), 1)
    mask_n = jnp.where((col2 < Q_BLOCK) & (i == 0), NEG, 0.0)
    for rs in halves:
        s = _dot_nt(qall[rs, :], kc) + jnp.concatenate([madd] * (GROUP // 2), axis=0) + tsel_ref[rs, :] + mask_n
        flash_update(rs, s, v1)
    acc = acc_s[...]
    o_s = acc[:, :LANES] / acc[:, LANES:]
    for r in range(GROUP):
        tile = oa_acc[:, r * LANES:(r + 1) * LANES] + gate_tile(1, r) * _pair_heads(o_s, r)
        oa_ref[:, r * LANES:(r + 1) * LANES] = tile.astype(oa_ref.dtype)


def _rel_bucket_np(dist):
    n = np.maximum(dist, 0)
    max_exact = REL_BUCKETS // 2
    nf = np.maximum(n, 1).astype(np.float32)
    log_b = max_exact + (np.log(nf / max_exact) / math.log(REL_MAX_DIST / max_exact)
                         * (REL_BUCKETS - max_exact)).astype(np.int32)
    log_b = np.minimum(log_b, REL_BUCKETS - 1)
    return np.where(n < max_exact, n, log_b)


def _toeplitz_bias(tab, pad, width, window, shift_far):
    length = width + Q_BLOCK
    dist = pad + Q_BLOCK - 1 - np.arange(length)
    onehot = np.zeros((length, REL_BUCKETS), np.float32)
    onehot[np.arange(length), _rel_bucket_np(dist)] = 1.0
    vals = jnp.dot(jnp.asarray(onehot), tab, precision=lax.Precision.HIGHEST)
    if shift_far:
        vals = vals - tab[REL_BUCKETS - 1][None, :]
    valid = (dist >= 0) & (dist < window)
    vals = jnp.where(jnp.asarray(valid)[:, None], vals, NEG).T
    skew = jnp.tile(vals, (1, Q_BLOCK))[:, :Q_BLOCK * (length - 1)].reshape(N_HEADS, Q_BLOCK, length - 1)
    return skew[:, :, Q_BLOCK - 1:Q_BLOCK - 1 + width].reshape(N_HEADS * Q_BLOCK, width).astype(F32)


def _attention(proj, kvcmp, sinks, bias_table, bsz, seq):
    assert seq % SEL_CHUNK == 0
    nq = seq // Q_BLOCK
    n_far = seq // CMP_STRIDE
    n_sel = seq // SEL_BLOCK
    n_top = min(SEL_TOP_N, n_sel)
    assert n_top >= SEL_INIT_BLOCKS + SEL_LOCAL_BLOCKS and n_sel <= LANES
    wpad = Q_BLOCK * (-(-(NSA_WINDOW - 1) // Q_BLOCK))
    bpad = Q_BLOCK * (-(-(SWA_WINDOW - 1) // Q_BLOCK))
    pair = lambda tab: tab.astype(F32).reshape(REL_BUCKETS, N_GROUPS, GROUP).transpose(0, 2, 1).reshape(REL_BUCKETS, -1)
    tab_a = pair(bias_table[:, :N_HEADS])
    tab_b = pair(bias_table[:, N_HEADS:])
    near_pad = CMP_STRIDE * CMP_FRONT - (CMP_BLOCK - 1)
    t_near = _toeplitz_bias(tab_a, near_pad, CMP_STRIDE * CMP_NEAR, 1 << 30, True)[:, ::CMP_STRIDE]
    t_sel = _toeplitz_bias(tab_a, Q_BLOCK, 2 * Q_BLOCK, 1 << 30, True)
    t_win = _toeplitz_bias(tab_a, wpad, wpad + Q_BLOCK, NSA_WINDOW, False)
    t_swa = _toeplitz_bias(tab_b, bpad, bpad + Q_BLOCK, SWA_WINDOW, False)
    n_rows = kvcmp.shape[2]
    cn = (np.arange(n_rows) - CMP_FRONT)[:, None] * CMP_STRIDE
    sj = np.arange(LANES)[None, :] * SEL_BLOCK
    cmat = ((cn < sj + SEL_BLOCK) & (cn + CMP_BLOCK > sj) & (cn >= 0) & (cn + CMP_BLOCK <= seq)
            & (sj < seq)).astype(np.float32)
    cmat = jnp.asarray(cmat, F32)
    padded = lambda name, p: jnp.pad(proj[name].reshape(bsz, seq, LANES), ((0, 0), (p, 0), (0, 0)))
    ks, vs = padded('ks', Q_BLOCK), padded('vs', Q_BLOCK)
    kw, vw = padded('kw', wpad), padded('vw', wpad)
    kb, vb = padded('kb', bpad), padded('vb', bpad)
    rows = N_HEADS * Q_BLOCK
    qspec = pl.BlockSpec((Q_BLOCK, 4 * LANES), lambda b, i: (b * nq + i, 0))
    const2 = lambda shape: pl.BlockSpec(shape, lambda b, i: (0, 0))
    batch3 = lambda n: pl.BlockSpec((1, n, LANES), lambda b, i: (b, 0, 0))
    kernel = functools.partial(_attn_kernel, n_far=n_far, n_top=n_top)
    return pl.pallas_call(
        kernel,
        grid=(bsz, nq),
        in_specs=[pl.BlockSpec(memory_space=pltpu.SMEM),
                  qspec, qspec,
                  pl.BlockSpec((Q_BLOCK, LANES), lambda b, i: (b * nq + i, 0)),
                  pl.BlockSpec((1, 1, n_rows, LANES), lambda b, i: (0, b, 0, 0)),
                  pl.BlockSpec((1, 1, n_rows, LANES), lambda b, i: (1, b, 0, 0)),
                  batch3(seq + Q_BLOCK), batch3(seq + Q_BLOCK),
                  batch3(seq + wpad), batch3(seq + wpad),
                  batch3(seq + bpad), batch3(seq + bpad),
                  const2((n_rows, LANES)),
                  const2((rows, CMP_NEAR)),
                  const2((rows, 2 * Q_BLOCK)),
                  const2((rows, wpad + Q_BLOCK)),
                  const2((rows, bpad + Q_BLOCK))],
        out_specs=[qspec, qspec],
        out_shape=[jax.ShapeDtypeStruct((bsz * seq, 4 * LANES), BF16)] * 2,
        scratch_shapes=[pltpu.VMEM((rows, LANES), MXU_DTYPE),
                        pltpu.VMEM((rows, LANES), MXU_DTYPE),
                        pltpu.VMEM((N_GROUPS * Q_BLOCK, LANES), MXU_DTYPE),
                        pltpu.VMEM((N_GROUPS * Q_BLOCK, LANES), MXU_DTYPE),
                        pltpu.VMEM((rows, LANES), F32),
                        pltpu.VMEM((rows, 2 * LANES), F32),
                        pltpu.VMEM((2, rows, SEL_CHUNK), F32),
                        pltpu.VMEM((Q_BLOCK, 4 * LANES), F32)],
        compiler_params=pltpu.CompilerParams(dimension_semantics=("arbitrary", "arbitrary"),
                                             vmem_limit_bytes=VMEM_LIMIT),
        name="attention",
    )(sinks.astype(F32), proj['qa'], proj['qb'], proj['ga'], kvcmp, kvcmp, ks, vs, kw, vw, kb, vb,
      cmat, t_near, t_sel, t_win, t_swa)


def _layer_norm(y, g, b):
    mu = jnp.mean(y, axis=-1, keepdims=True)
    yc = y - mu
    var = jnp.mean(yc * yc, axis=-1, keepdims=True)
    return yc * lax.rsqrt(var + LN_EPS) * g + b


def _outproj_kernel(oa_ref, ob_ref, sg_ref, x_ref, pa_ref, pb_ref, wo_ref, g1_ref, b1_ref, wr_ref, rb_ref, sgu_ref,
                    sd_ref, tri_ref, h_ref, base_ref, eidx_ref, gate_ref, rank_ref, cnt_ref, carry):
    step = pl.program_id(0)
    tm = oa_ref.shape[0]

    @pl.when(step == 0)
    def _():
        carry[...] = jnp.zeros(carry.shape, F32)

    sg = sg_ref[...].astype(F32)
    merged = (sg[:, :D_MODEL] * _dot(_mx(oa_ref[...]), pa_ref[...])
              + sg[:, D_MODEL:] * _dot(_mx(ob_ref[...]), pb_ref[...]))
    mix = _dot(_mx(merged), wo_ref[...])
    h = _layer_norm(DN_ALPHA * x_ref[...] + mix, g1_ref[...], b1_ref[...])
    hb = _mx(h)
    h_ref[...] = _pack_bf16_pairs(h)

    gu = _dot(hb, sgu_ref[...])
    shared = _dot(_mx(jax.nn.silu(gu[:, :SHARED_HIDDEN]) * gu[:, SHARED_HIDDEN:]), sd_ref[...])
    base_ref[...] = DN_ALPHA * h + shared

    scores = jax.nn.sigmoid(_dot_nt(wr_ref[...], hb))
    choice = scores + rb_ref[:, 0:1]
    per_group = N_EXPERTS // N_EXPERT_GROUPS
    gs = []
    for g in range(N_EXPERT_GROUPS):
        cg = choice[g * per_group:(g + 1) * per_group]
        m1 = jnp.max(cg, axis=0, keepdims=True)
        is_m = cg == m1
        n_m = jnp.sum(is_m.astype(F32), axis=0, keepdims=True)
        m2 = jnp.max(jnp.where(is_m, -jnp.inf, cg), axis=0, keepdims=True)
        gs.append(m1 + jnp.where(n_m > 1.5, m1, m2))
    gs = jnp.concatenate(gs, axis=0)
    gid = lax.broadcasted_iota(jnp.int32, gs.shape, 0)
    beaten = jnp.zeros(gs.shape, jnp.int32)
    for g in range(N_EXPERT_GROUPS):
        other = gs[g:g + 1]
        beaten = beaten + ((other > gs) | ((other == gs) & (g < gid))).astype(jnp.int32)
    keep_g = beaten < TOPK_EXPERT_GROUPS
    keep = jnp.concatenate([jnp.broadcast_to(keep_g[g:g + 1], (per_group, tm)) for g in range(N_EXPERT_GROUPS)],
                           axis=0)
    cand = jnp.where(keep, choice, -jnp.inf)
    eid = lax.broadcasted_iota(jnp.int32, cand.shape, 0)
    hits = []
    e_rows = []
    w_rows = []
    for _ in range(TOP_K):
        m = jnp.max(cand, axis=0, keepdims=True)
        idx = jnp.min(jnp.where(cand == m, eid, N_EXPERTS), axis=0, keepdims=True)
        hit = eid == idx
        hits.append(hit)
        e_rows.append(idx)
        w_rows.append(jnp.sum(jnp.where(hit, scores, 0.0), axis=0, keepdims=True))
        cand = jnp.where(hit, -jnp.inf, cand)
    w = jnp.concatenate(w_rows, axis=0)
    gate_ref[...] = w / jnp.sum(w, axis=0, keepdims=True) * ROUTED_SCALE
    eidx_ref[...] = jnp.concatenate(e_rows, axis=0)

    onehot = jnp.zeros(cand.shape, F32)
    for hit in hits:
        onehot = onehot + hit.astype(F32)
    before = _dot(onehot.astype(BF16), tri_ref[...]) + carry[:, 0:1]
    rank_ref[...] = jnp.concatenate(
        [jnp.sum(jnp.where(hit, before, 0.0), axis=0, keepdims=True) for hit in hits], axis=0).astype(jnp.int32)
    carry[...] = carry[...] + jnp.sum(onehot, axis=1, keepdims=True)
    cnt_ref[...] = carry[...]


def _out_projection(oa, ob, sg, x2, proj_a, proj_b, w_out, ln_g, ln_b, w_router, router_bias, s_gate, s_up, s_down):
    t = x2.shape[0]
    tm = OUT_TM
    pair_rows = lambda p: p.reshape(N_GROUPS, GROUP, HEAD_DIM, -1).transpose(1, 0, 2, 3).reshape(p.shape)
    pa = pair_rows(proj_a).astype(MXU_DTYPE)
    pb = pair_rows(proj_b).astype(MXU_DTYPE)
    tri = jnp.asarray(np.triu(np.ones((tm, tm), np.float32), 1), BF16)
    row = lambda i: (i, 0)
    fixed = lambda i: (0, 0)
    col = lambda i: (0, i)
    outs = pl.pallas_call(
        _outproj_kernel,
        grid=(t // tm,),
        in_specs=[pl.BlockSpec((tm, 4 * LANES), row), pl.BlockSpec((tm, 4 * LANES), row),
                  pl.BlockSpec((tm, 2 * D_MODEL), row), pl.BlockSpec((tm, D_MODEL), row),
                  pl.BlockSpec((4 * LANES, D_MODEL), fixed), pl.BlockSpec((4 * LANES, D_MODEL), fixed),
                  pl.BlockSpec((D_MODEL, D_MODEL), fixed),
                  pl.BlockSpec((1, D_MODEL), fixed), pl.BlockSpec((1, D_MODEL), fixed),
                  pl.BlockSpec((N_EXPERTS, D_MODEL), fixed), pl.BlockSpec((N_EXPERTS, LANES), fixed),
                  pl.BlockSpec((D_MODEL, 2 * SHARED_HIDDEN), fixed), pl.BlockSpec((SHARED_HIDDEN, D_MODEL), fixed),
                  pl.BlockSpec((tm, tm), fixed)],
        out_specs=[pl.BlockSpec((tm, D_MODEL // 2), row), pl.BlockSpec((tm, D_MODEL), row),
                   pl.BlockSpec((TOP_K, tm), col), pl.BlockSpec((TOP_K, tm), col), pl.BlockSpec((TOP_K, tm), col),
                   pl.BlockSpec((N_EXPERTS, LANES), fixed)],
        out_shape=[jax.ShapeDtypeStruct((t, D_MODEL // 2), jnp.uint32), jax.ShapeDtypeStruct((t, D_MODEL), F32),
                   jax.ShapeDtypeStruct((TOP_K, t), jnp.int32), jax.ShapeDtypeStruct((TOP_K, t), F32),
                   jax.ShapeDtypeStruct((TOP_K, t), jnp.int32), jax.ShapeDtypeStruct((N_EXPERTS, LANES), F32)],
        scratch_shapes=[pltpu.VMEM((N_EXPERTS, LANES), F32)],
        compiler_params=pltpu.CompilerParams(dimension_semantics=("arbitrary",), vmem_limit_bytes=VMEM_LIMIT),
        name="out_projection_router",
    )(oa, ob, sg, x2, pa, pb, w_out.astype(MXU_DTYPE), ln_g.reshape(1, -1), ln_b.reshape(1, -1),
      w_router.T.astype(MXU_DTYPE), jnp.broadcast_to(router_bias.astype(F32)[:, None], (N_EXPERTS, LANES)),
      jnp.concatenate([s_gate, s_up], axis=1).astype(MXU_DTYPE), s_down.astype(MXU_DTYPE), tri)
    return outs


def _rows_to_tiles(x):
    return pltpu.einshape("cml->mcl", jnp.stack(_lane_tiles(x), axis=0))


def _tiles_to_rows(x3):
    xt = pltpu.einshape("mcl->cml", x3)
    return jnp.concatenate([xt[c] for c in range(xt.shape[0])], axis=1)


def _dispatch_kernel(zstart_ref, cnt_ref, dest_ref, h2_ref, xs_ref, h_ref, zeros, sem):
    step = pl.program_id(0)
    tm = h_ref.shape[0]
    h_ref[...] = _rows_to_tiles(h2_ref[...])

    @pl.when(step == 0)
    def _():
        zeros[...] = jnp.zeros(zeros.shape, zeros.dtype)

        def fill(e, c):
            @pl.when(cnt_ref[e] > 0)
            def _():
                cp = pltpu.make_async_copy(zeros, xs_ref.at[pl.ds(zstart_ref[e], MOE_BM)], sem)
                cp.start()
                cp.wait()
            return c
        lax.fori_loop(0, N_EXPERTS, fill, 0)

    def issue(t, c):
        for k in range(TOP_K):
            pltpu.make_async_copy(h_ref.at[t], xs_ref.at[dest_ref[k, t]], sem).start(priority=k % 2)
        return c
    lax.fori_loop(0, tm, issue, 0)
    for k in range(TOP_K):
        pltpu.make_async_copy(h_ref, xs_ref.at[pl.ds(0, tm)], sem).wait()


def _dispatch(h, dest, zstart, counts, n_rows):
    t = h.shape[0]
    tm = DISP_TM
    return pl.pallas_call(
        _dispatch_kernel,
        grid_spec=pltpu.PrefetchScalarGridSpec(
            num_scalar_prefetch=2,
            grid=(t // tm,),
            in_specs=[pl.BlockSpec((TOP_K, tm), lambda i, *_: (0, i), memory_space=pltpu.SMEM),
                      pl.BlockSpec((tm, D_MODEL // 2), lambda i, *_: (i, 0))],
            out_specs=pl.BlockSpec(memory_space=pl.ANY),
            scratch_shapes=[pltpu.VMEM((tm,) + PACKED_ROW_TILE, jnp.uint32),
                            pltpu.VMEM((MOE_BM,) + PACKED_ROW_TILE, jnp.uint32),
                            pltpu.SemaphoreType.DMA(())]),
        out_shape=jax.ShapeDtypeStruct((n_rows,) + PACKED_ROW_TILE, jnp.uint32),
        compiler_params=pltpu.CompilerParams(dimension_semantics=("arbitrary",), vmem_limit_bytes=VMEM_LIMIT),
        name="moe_dispatch",
    )(zstart, counts, dest, h)


def _experts_kernel(blk_e_ref, nused_ref, xs_ref, wg_ref, wu_ref, wd_ref, ys_ref, wg_s, wu_s, wd_s):
    b = pl.program_id(0)
    prev = blk_e_ref[jnp.maximum(b - 1, 0)]

    @pl.when((b == 0) | (blk_e_ref[b] != prev))
    def _():
        wg_s[...] = _mx(wg_ref[0])
        wu_s[...] = _mx(wu_ref[0])
        wd_s[...] = _mx(wd_ref[0])

    @pl.when(b < nused_ref[0])
    def _():
        xb = _mx(jnp.concatenate(_unpack_bf16_pairs(_tiles_to_rows(xs_ref[...])), axis=1))
        hid = jax.nn.silu(_dot(xb, wg_s[...])) * _dot(xb, wu_s[...])
        ys_ref[...] = _rows_to_tiles(_pack_bf16_pairs(_dot(_mx(hid), wd_s[...])))

    @pl.when(b >= nused_ref[0])
    def _():
        ys_ref[...] = jnp.zeros(ys_ref.shape, ys_ref.dtype)


def _experts(xs, blk_e, nused, e_gate, e_up, e_down):
    n_rows = xs.shape[0]
    n_blocks = n_rows // MOE_BM
    xmap = lambda b, be, nu: (jnp.minimum(b, nu[0] - 1), 0, 0)
    wmap = lambda b, be, nu: (be[b], 0, 0)
    return pl.pallas_call(
        _experts_kernel,
        grid_spec=pltpu.PrefetchScalarGridSpec(
            num_scalar_prefetch=2,
            grid=(n_blocks,),
            in_specs=[pl.BlockSpec((MOE_BM,) + PACKED_ROW_TILE, xmap),
                      pl.BlockSpec((1, D_MODEL, EXPERT_HIDDEN), wmap),
                      pl.BlockSpec((1, D_MODEL, EXPERT_HIDDEN), wmap),
                      pl.BlockSpec((1, EXPERT_HIDDEN, D_MODEL), wmap)],
            out_specs=pl.BlockSpec((MOE_BM,) + PACKED_ROW_TILE, lambda b, be, nu: (b, 0, 0)),
            scratch_shapes=[pltpu.VMEM((D_MODEL, EXPERT_HIDDEN), MXU_DTYPE),
                            pltpu.VMEM((D_MODEL, EXPERT_HIDDEN), MXU_DTYPE),
                            pltpu.VMEM((EXPERT_HIDDEN, D_MODEL), MXU_DTYPE)]),
        out_shape=jax.ShapeDtypeStruct((n_rows,) + PACKED_ROW_TILE, jnp.uint32),
        compiler_params=pltpu.CompilerParams(dimension_semantics=("arbitrary",), vmem_limit_bytes=VMEM_LIMIT),
        name="moe_experts",
    )(blk_e, nused, xs, e_gate, e_up, e_down)


def _combine_kernel(dest_ref, dest_next_ref, gate_ref, base_ref, g2_ref, b2_ref, ys_ref, out_ref, buf, routed, sem):
    step = pl.program_id(0)
    tm = base_ref.shape[0]
    slot = step % 2

    def issue(d_ref, s):
        def body(t, c):
            for k in range(TOP_K):
                pltpu.make_async_copy(ys_ref.at[d_ref[k, t]], buf.at[s, k, t], sem.at[s]).start(priority=k % 2)
            return c
        lax.fori_loop(0, tm, body, 0)

    @pl.when(step == 0)
    def _():
        issue(dest_ref, 0)

    @pl.when(step + 1 < pl.num_programs(0))
    def _():
        issue(dest_next_ref, 1 - slot)

    for k in range(TOP_K):
        pltpu.make_async_copy(ys_ref.at[pl.ds(0, tm)], buf.at[slot, k], sem.at[slot]).wait()

    def token(t, c):
        low = jnp.zeros(PACKED_ROW_TILE, F32)
        high = jnp.zeros(PACKED_ROW_TILE, F32)
        for k in range(TOP_K):
            lo_k, hi_k = _unpack_bf16_pairs(buf[slot, k, t])
            low = low + gate_ref[k, t] * lo_k
            high = high + gate_ref[k, t] * hi_k
        routed[t] = jnp.concatenate([low, high], axis=0)
        return c
    lax.fori_loop(0, tm, token, 0)
    out_ref[...] = _layer_norm(base_ref[...] + _tiles_to_rows(routed[...]), g2_ref[...], b2_ref[...])


def _combine(ys3, dest, gate, base, ln_g, ln_b):
    t = base.shape[0]
    tm = COMB_TM
    n_tiles = t // tm
    return pl.pallas_call(
        _combine_kernel,
        grid=(n_tiles,),
        in_specs=[pl.BlockSpec((TOP_K, tm), lambda i: (0, i), memory_space=pltpu.SMEM),
                  pl.BlockSpec((TOP_K, tm), lambda i: (0, jnp.minimum(i + 1, n_tiles - 1)), memory_space=pltpu.SMEM),
                  pl.BlockSpec((TOP_K, tm), lambda i: (0, i), memory_space=pltpu.SMEM),
                  pl.BlockSpec((tm, D_MODEL), lambda i: (i, 0)),
                  pl.BlockSpec((1, D_MODEL), lambda i: (0, 0)),
                  pl.BlockSpec((1, D_MODEL), lambda i: (0, 0)),
                  pl.BlockSpec(memory_space=pl.ANY)],
        out_specs=pl.BlockSpec((tm, D_MODEL), lambda i: (i, 0)),
        out_shape=jax.ShapeDtypeStruct((t, D_MODEL), F32),
        scratch_shapes=[pltpu.VMEM((2, TOP_K, tm) + PACKED_ROW_TILE, jnp.uint32), pltpu.VMEM((tm,) + ROW_TILE, F32),
                        pltpu.SemaphoreType.DMA((2,))],
        compiler_params=pltpu.CompilerParams(dimension_semantics=("arbitrary",), vmem_limit_bytes=VMEM_LIMIT),
        name="moe_combine",
    )(dest, dest, gate, base, ln_g.reshape(1, -1), ln_b.reshape(1, -1), ys3)


def _moe_layout(eidx, rank, counts):
    n_assign = eidx.size
    n_blocks = (n_assign + N_EXPERTS * (MOE_BM - 1)) // MOE_BM
    padded = (counts + MOE_BM - 1) // MOE_BM * MOE_BM
    pends = jnp.cumsum(padded)
    pstarts = pends - padded
    experts = jnp.arange(N_EXPERTS, dtype=jnp.int32)
    dest = jnp.sum(jnp.where(eidx[..., None] == experts, pstarts, 0), axis=-1) + rank
    block_row = jnp.arange(n_blocks, dtype=jnp.int32) * MOE_BM
    blk_e = jnp.minimum(jnp.sum(pends[None, :] <= block_row[:, None], axis=1), N_EXPERTS - 1).astype(jnp.int32)
    nused = (pends[-1:] // MOE_BM).astype(jnp.int32)
    zstart = jnp.maximum(pends - MOE_BM, 0).astype(jnp.int32)
    return dest.astype(jnp.int32), blk_e, nused, zstart, n_blocks * MOE_BM


def _layer(x, w_in, cmp_pe, cmp_w1, cmp_b1, cmp_w2, sinks, bias_table, proj_a, proj_b, w_out, ln1_g, ln1_b,
           w_router, router_bias, e_gate, e_up, e_down, s_gate, s_up, s_down, ln2_g, ln2_b):
    bsz, seq, d = x.shape
    x2 = x.reshape(bsz * seq, d)
    proj = _in_projection(x2, w_in)
    kvcmp = _compress(proj['kc'], proj['vc'], bsz, seq, cmp_pe, cmp_w1, cmp_b1, cmp_w2)
    oa, ob = _attention(proj, kvcmp, sinks, bias_table, bsz, seq)
    h, base, eidx, gate, rank, cnt = _out_projection(oa, ob, proj['sg'], x2, proj_a, proj_b, w_out, ln1_g, ln1_b,
                                                     w_router, router_bias, s_gate, s_up, s_down)
    counts = cnt[:, 0].astype(jnp.int32)
    dest, blk_e, nused, zstart, n_rows = _moe_layout(eidx, rank, counts)
    xs = _dispatch(h, dest, zstart, counts, n_rows)
    ys = _experts(xs, blk_e, nused, e_gate, e_up, e_down)
    out = _combine(ys, dest, gate, base, ln2_g, ln2_b)
    return out.reshape(bsz, seq, d)


def kernel(x, w_in, cmp_pe, cmp_w1, cmp_b1, cmp_w2, attn_sinks, rel_bias_table, proj_a, proj_b, w_out, ln1_g, ln1_b,
           w_router, router_bias, expert_w_gate, expert_w_up, expert_w_down, shared_w_gate, shared_w_up,
           shared_w_down, ln2_g, ln2_b):
    h = x
    for l in range(DEPTH):
        h = _layer(h, w_in[l], cmp_pe[l], cmp_w1[l], cmp_b1[l], cmp_w2[l], attn_sinks[l], rel_bias_table, proj_a[l],
                   proj_b[l], w_out[l], ln1_g[l], ln1_b[l], w_router[l], router_bias[l], expert_w_gate[l],
                   expert_w_up[l], expert_w_down[l], shared_w_gate[l], shared_w_up[l], shared_w_down[l], ln2_g[l],
                   ln2_b[l])
    return h
```

```python
import functools
import math

import numpy as np
import jax
import jax.numpy as jnp
from jax import lax
from jax.experimental import pallas as pl
from jax.experimental.pallas import tpu as pltpu

F32 = jnp.float32
BF16 = jnp.bfloat16
MXU_DTYPE = jnp.bfloat16

D_MODEL = 1024
HEAD_DIM = 64
ATTN_SCALE = HEAD_DIM ** -0.5
Q_BLOCK = 128
N_HEADS = 8
N_GROUPS = 2
GROUP = 4
CMP_BLOCK = 32
CMP_STRIDE = 16
CMP_HIDDEN = 128
SEL_BLOCK = 64
SEL_TOP_N = 8
SEL_INIT_BLOCKS = 1
SEL_LOCAL_BLOCKS = 2
NSA_WINDOW = 512
SWA_WINDOW = 128
REL_BUCKETS = 32
REL_MAX_DIST = 128
N_EXPERTS = 256
TOP_K = 8
EXPERT_HIDDEN = 256
SHARED_HIDDEN = 256
N_EXPERT_GROUPS = 8
TOPK_EXPERT_GROUPS = 4
ROUTED_SCALE = 2.5
LN_EPS = 1e-5
DEPTH = 1
DN_ALPHA = (2 * DEPTH) ** 0.25

NEG = -1e30
LANES = 128
ROW_TILE = (8, LANES)
PACKED_ROW_TILE = (4, LANES)
CMP_FRONT = 16
CMP_NEAR = LANES
SEL_CHUNK = 512
VMEM_LIMIT = 56 * 1024 * 1024

IN_TM = 512
OUT_TM = 512
MOE_BM = 256
DISP_TM = 256
COMB_TM = 128


def _dot(a, b):
    return jnp.dot(a, b, preferred_element_type=F32)


def _dot_nt(a, b):
    return lax.dot_general(a, b, (((1,), (1,)), ((), ())), preferred_element_type=F32)


def _mx(a):
    return a.astype(MXU_DTYPE)


def _pack_bf16_pairs(x):
    half = x.shape[1] // 2
    bits = lax.bitcast_convert_type(x.astype(BF16).astype(F32), jnp.uint32)
    return (bits[:, half:] & jnp.uint32(0xFFFF0000)) | (bits[:, :half] >> 16)


def _unpack_bf16_pairs(words):
    return (lax.bitcast_convert_type(words << 16, F32),
            lax.bitcast_convert_type(words & jnp.uint32(0xFFFF0000), F32))


_IN_COLS = (('qa', 512), ('qb', 512), ('kc', 128), ('vc', 128), ('ks', 128), ('vs', 128), ('kw', 128),
            ('vw', 128), ('kb', 128), ('vb', 128), ('ga', 128), ('sg', 2048))


def _inproj_kernel(x_ref, w_ref, qa_ref, qb_ref, kc_ref, vc_ref, ks_ref, vs_ref, kw_ref, vw_ref, kb_ref, vb_ref,
                   ga_ref, sg_ref):
    xb = _mx(x_ref[...])
    outs = dict(qa=qa_ref, qb=qb_ref, kc=kc_ref, vc=vc_ref, ks=ks_ref, vs=vs_ref, kw=kw_ref, vw=vw_ref,
                kb=kb_ref, vb=vb_ref, ga=ga_ref, sg=sg_ref)
    off = 0
    for name, width in _IN_COLS:
        for c0 in range(0, width, 512):
            cw = min(512, width - c0)
            y = _dot(xb, w_ref[:, off + c0:off + c0 + cw])
            if name in ('ga', 'sg'):
                y = jax.nn.sigmoid(y)
            outs[name][:, c0:c0 + cw] = y.astype(outs[name].dtype)
        off += width


def _pair_head_columns(w):
    return w.reshape(w.shape[0], N_GROUPS, GROUP, HEAD_DIM).transpose(0, 2, 1, 3).reshape(w.shape[0], -1)


def _in_projection(x2, w_in):
    t = x2.shape[0]
    sizes = (512, 128, 128, 128, 128, 128, 128, 24, 512, 128, 128, 1024, 1024)
    offs = np.cumsum((0,) + sizes)
    part = [w_in[:, offs[k]:offs[k + 1]] for k in range(len(sizes))]
    w_qa, w_kc, w_vc, w_ks, w_vs, w_kw, w_vw, w_g, w_qb, w_kb, w_vb, w_gate_a, w_gate_b = part
    w_qa = _pair_head_columns(w_qa) * ATTN_SCALE
    w_qb = _pair_head_columns(w_qb) * ATTN_SCALE
    w_ga = w_g.reshape(-1, N_GROUPS, GROUP, 3).transpose(0, 3, 2, 1).reshape(-1, 24)
    w_ga = jnp.pad(w_ga, ((0, 0), (0, LANES - 24)))
    w_all = jnp.concatenate([w_qa, w_qb, w_kc, w_vc, w_ks, w_vs, w_kw, w_vw, w_kb, w_vb, w_ga, w_gate_a, w_gate_b],
                            axis=1).astype(MXU_DTYPE)
    n_all = w_all.shape[1]
    out_shape = []
    out_specs = []
    for name, width in _IN_COLS:
        dt = F32 if name == 'ga' else BF16
        out_shape.append(jax.ShapeDtypeStruct((t, width), dt))
        out_specs.append(pl.BlockSpec((IN_TM, width), lambda i: (i, 0)))
    outs = pl.pallas_call(
        _inproj_kernel,
        grid=(t // IN_TM,),
        in_specs=[pl.BlockSpec((IN_TM, D_MODEL), lambda i: (i, 0)),
                  pl.BlockSpec((D_MODEL, n_all), lambda i: (0, 0))],
        out_specs=out_specs,
        out_shape=out_shape,
        compiler_params=pltpu.CompilerParams(dimension_semantics=("arbitrary",), vmem_limit_bytes=VMEM_LIMIT),
        name="in_projection",
    )(x2, w_all)
    return dict(zip([n for n, _ in _IN_COLS], outs))


def _compress_kernel(tok_ref, w1_ref, pe_ref, w1o_ref, b1_ref, w2_ref, out_ref):
    n_chunks = tok_ref.shape[2]
    ab = _dot(tok_ref[0, 0], w1_ref[0])
    a = ab[:, :2 * CMP_HIDDEN]
    b_next = pltpu.roll(ab[:, 2 * CMP_HIDDEN:], n_chunks - 1, 0)
    cb = _dot(_mx(pe_ref[0]), _mx(w1o_ref[0]))[0:1, :] + b1_ref[0]
    cb2 = jnp.concatenate([cb, cb], axis=1)
    hid = jax.nn.gelu(a + b_next + cb2)
    out = _dot(_mx(hid), w2_ref[0])
    row = lax.broadcasted_iota(jnp.int32, out.shape, 0)
    out = jnp.where(row < n_chunks - 1, out, 0.0)
    out_ref[0, 0, 0:CMP_FRONT, :] = jnp.zeros((CMP_FRONT, LANES), F32)
    out_ref[0, 0, CMP_FRONT:CMP_FRONT + n_chunks, :] = out
    out_ref[0, 0, CMP_FRONT + n_chunks:, :] = jnp.zeros((CMP_NEAR - CMP_FRONT, LANES), F32)


def _compress(kc, vc, bsz, seq, cmp_pe, cmp_w1, cmp_b1, cmp_w2):
    n_chunks = seq // CMP_STRIDE
    tok = jnp.stack([kc, vc]).reshape(2, bsz, n_chunks, CMP_STRIDE * LANES)
    eye = jnp.eye(N_GROUPS, dtype=F32)
    w1r = cmp_w1.reshape(2, 2, CMP_STRIDE, HEAD_DIM, CMP_HIDDEN)
    w1 = jnp.einsum('khjdn,gG->kjgdhGn', w1r, eye).reshape(2, CMP_STRIDE * LANES, 4 * CMP_HIDDEN).astype(MXU_DTYPE)
    w2 = jnp.einsum('knd,gG->kgnGd', cmp_w2, eye).reshape(2, 2 * CMP_HIDDEN, LANES).astype(MXU_DTYPE)
    pe = jnp.pad(cmp_pe.reshape(2, 1, CMP_BLOCK * HEAD_DIM), ((0, 0), (0, 7), (0, 0)))
    b1 = cmp_b1.reshape(2, 1, CMP_HIDDEN)
    rows = CMP_FRONT + n_chunks + CMP_NEAR - CMP_FRONT
    return pl.pallas_call(
        _compress_kernel,
        grid=(2, bsz),
        in_specs=[pl.BlockSpec((1, 1, n_chunks, CMP_STRIDE * LANES), lambda k, b: (k, b, 0, 0)),
                  pl.BlockSpec((1, CMP_STRIDE * LANES, 4 * CMP_HIDDEN), lambda k, b: (k, 0, 0)),
                  pl.BlockSpec((1, 8, CMP_BLOCK * HEAD_DIM), lambda k, b: (k, 0, 0)),
                  pl.BlockSpec((1, CMP_BLOCK * HEAD_DIM, CMP_HIDDEN), lambda k, b: (k, 0, 0)),
                  pl.BlockSpec((1, 1, CMP_HIDDEN), lambda k, b: (k, 0, 0)),
                  pl.BlockSpec((1, 2 * CMP_HIDDEN, LANES), lambda k, b: (k, 0, 0))],
        out_specs=pl.BlockSpec((1, 1, rows, LANES), lambda k, b: (k, b, 0, 0)),
        out_shape=jax.ShapeDtypeStruct((2, bsz, rows, LANES), F32),
        compiler_params=pltpu.CompilerParams(dimension_semantics=("arbitrary", "arbitrary"),
                                             vmem_limit_bytes=VMEM_LIMIT),
        name="nsa_compress",
    )(tok, w1, pe, cmp_w1, b1, w2)


def _stack_heads(q_ref, dst):
    lo = lax.broadcasted_iota(jnp.int32, (Q_BLOCK, LANES), 1) < HEAD_DIM
    for r in range(GROUP):
        qr = q_ref[:, r * LANES:(r + 1) * LANES].astype(dst.dtype)
        z = jnp.zeros_like(qr)
        dst[(2 * r) * Q_BLOCK:(2 * r + 1) * Q_BLOCK, :] = jnp.where(lo, qr, z)
        dst[(2 * r + 1) * Q_BLOCK:(2 * r + 2) * Q_BLOCK, :] = jnp.where(lo, z, qr)


def _pair_heads(o, r):
    lo = lax.broadcasted_iota(jnp.int32, (Q_BLOCK, LANES), 1) < HEAD_DIM
    return jnp.where(lo, o[(2 * r) * Q_BLOCK:(2 * r + 1) * Q_BLOCK], o[(2 * r + 1) * Q_BLOCK:(2 * r + 2) * Q_BLOCK])


def _lane_tiles(x):
    return [x[:, t * LANES:(t + 1) * LANES] for t in range(x.shape[1] // LANES)]


def _row_max(tiles):
    mx = tiles[0]
    for t in tiles[1:]:
        mx = jnp.maximum(mx, t)
    return jnp.broadcast_to(jnp.max(mx, axis=1, keepdims=True), mx.shape)


def _with_ones(v):
    return jnp.concatenate([v, jnp.ones(v.shape, v.dtype)], axis=1)


def _block_of_key(n_keys, first_block):
    b = lax.broadcasted_iota(jnp.int32, (LANES, n_keys), 0)
    k = lax.broadcasted_iota(jnp.int32, (LANES, n_keys), 1)
    return (b == (k // SEL_BLOCK) + first_block).astype(MXU_DTYPE)


def _select_blocks_t(imp_t, i, n_top):
    blk = lax.broadcasted_iota(jnp.int32, imp_t.shape, 0)
    qcol = lax.broadcasted_iota(jnp.int32, imp_t.shape, 1)
    back = (2 * i + (qcol >= SEL_BLOCK).astype(jnp.int32)) - blk
    sel = (back >= 0) & ((blk < SEL_INIT_BLOCKS) | (back < SEL_LOCAL_BLOCKS))
    cand = jnp.where((back >= SEL_LOCAL_BLOCKS) & (blk >= SEL_INIT_BLOCKS), imp_t, -1.0)
    blk_f = blk.astype(F32)
    for _ in range(n_top - SEL_INIT_BLOCKS - SEL_LOCAL_BLOCKS):
        m = jnp.max(cand, axis=0, keepdims=True)
        idx = jnp.min(jnp.where(cand == m, blk_f, float(LANES)), axis=0, keepdims=True)
        hit = blk_f == idx
        sel = sel | (hit & (m >= 0.0))
        cand = jnp.where(hit, -2.0, cand)
    return sel


def _attn_kernel(sink_ref, qa_ref, qb_ref, ga_ref, kcmp_ref, vcmp_ref, ks_ref, vs_ref, kw_ref, vw_ref, kb_ref,
                 vb_ref, cmat_ref, tnear_ref, tsel_ref, twin_ref, tswa_ref, oa_ref, ob_ref,
                 qall, qball, mneg, mneg_far, m_s, acc_s, s_buf, oa_acc, *, n_far, n_top):
    i = pl.program_id(1)
    rows = N_HEADS * Q_BLOCK
    half = rows // 2
    halves = (slice(0, half), slice(half, rows))
    _stack_heads(qa_ref, qall)
    _stack_heads(qb_ref, qball)
    nstart = pl.multiple_of(i * Q_BLOCK, Q_BLOCK)
    lo = lax.broadcasted_iota(jnp.int32, (Q_BLOCK, LANES), 1) < HEAD_DIM
    gates = ga_ref[...]

    def gate_tile(c, r):
        return jnp.where(lo, gates[:, c * 8 + 2 * r:c * 8 + 2 * r + 1], gates[:, c * 8 + 2 * r + 1:c * 8 + 2 * r + 2])

    def softmax_pv(s_tiles, v1, fix_max=None):
        m = _row_max(s_tiles)
        if fix_max is not None:
            m = fix_max(m)
        e = [jnp.exp(t - m) for t in s_tiles]
        return e, m, _dot(_mx(jnp.concatenate(e, axis=1)), v1)

    off = pl.multiple_of(i * (Q_BLOCK // CMP_STRIDE), 8)
    k_cmp = _mx(jnp.concatenate([kcmp_ref[0, 0, 0:n_far, :], kcmp_ref[0, 0, pl.ds(off, CMP_NEAR), :]], axis=0))
    v_cmp = _with_ones(_mx(jnp.concatenate([vcmp_ref[0, 0, 0:n_far, :], vcmp_ref[0, 0, pl.ds(off, CMP_NEAR), :]],
                                           axis=0)))
    colf = lax.broadcasted_iota(jnp.int32, (1, n_far), 1)
    coln = lax.broadcasted_iota(jnp.int32, (1, CMP_NEAR), 1)
    col_ok = jnp.concatenate([(colf >= CMP_FRONT) & (colf < off), coln + off >= CMP_FRONT], axis=1)
    mask_c = jnp.where(col_ok, 0.0, NEG)
    no_key = lambda m: jnp.where(m > 0.5 * NEG, m, 0.0)
    p_cmp, o_c = [], []
    for rs in halves:
        tiles = _lane_tiles(_dot_nt(qall[rs, :], k_cmp) + mask_c)
        tiles[-1] = tiles[-1] + tnear_ref[rs, :]
        e, _, ov = softmax_pv(tiles, v_cmp, no_key)
        inv = 1.0 / jnp.maximum(ov[:, LANES:], 1e-30)
        o_c.append(ov[:, :LANES] * inv)
        p_cmp.append([t * inv for t in e])
    o_c = jnp.concatenate(o_c, axis=0)

    def far_start(j):
        return pl.multiple_of(Q_BLOCK + j * SEL_CHUNK, Q_BLOCK)

    def far_logits(j, slot, masked):
        kc = _mx(ks_ref[0, pl.ds(far_start(j), SEL_CHUNK), :])
        if masked:
            madd = _dot(mneg_far[...], _block_of_key(SEL_CHUNK, j * (SEL_CHUNK // SEL_BLOCK)))
        for rs in halves:
            s = _dot_nt(qall[rs, :], kc)
            s_buf[slot, rs, :] = s + jnp.concatenate([madd] * (GROUP // 2), axis=0) if masked else s

    far_logits(0, 0, False)

    blkcol = lax.broadcasted_iota(jnp.int32, (Q_BLOCK, LANES), 1)
    n_tiles = len(p_cmp[0])
    for g in range(N_GROUPS):
        imp = jnp.zeros((Q_BLOCK, LANES), F32)
        for t in range(n_tiles):
            pg = sum(p_cmp[r // 2][t][(2 * (r % 2) + g) * Q_BLOCK:(2 * (r % 2) + g + 1) * Q_BLOCK]
                     for r in range(GROUP))
            if t < n_tiles - 1:
                cm = _mx(cmat_ref[t * LANES:(t + 1) * LANES, :])
            else:
                cm = _mx(cmat_ref[pl.ds(off, CMP_NEAR), :])
            hi = _mx(pg)
            low = _mx(pg - hi.astype(F32))
            imp = imp + _dot(hi, cm) + _dot(low, cm)
        sel = _select_blocks_t(imp.T, i, n_top)
        neg = jnp.where(sel, 0.0, NEG).T
        mneg[g * Q_BLOCK:(g + 1) * Q_BLOCK, :] = neg.astype(mneg.dtype)
        mneg_far[g * Q_BLOCK:(g + 1) * Q_BLOCK, :] = jnp.where(blkcol < 2 * (i - 1), neg, NEG).astype(mneg.dtype)

    wpad = kw_ref.shape[1] - ks_ref.shape[1] + Q_BLOCK
    kwin = _mx(kw_ref[0, pl.ds(nstart, wpad + Q_BLOCK), :])
    vwin = _with_ones(_mx(vw_ref[0, pl.ds(nstart, wpad + Q_BLOCK), :]))
    colw = lax.broadcasted_iota(jnp.int32, (1, wpad + Q_BLOCK), 1)
    mask_w = jnp.where(colw + nstart >= wpad, 0.0, NEG)
    o_w = []
    for rs in halves:
        _, _, ov = softmax_pv(_lane_tiles(_dot_nt(qall[rs, :], kwin) + twin_ref[rs, :] + mask_w), vwin)
        o_w.append(ov[:, :LANES] / ov[:, LANES:])
    o_w = jnp.concatenate(o_w, axis=0)
    for r in range(GROUP):
        oa_acc[:, r * LANES:(r + 1) * LANES] = (gate_tile(0, r) * _pair_heads(o_c, r)
                                                + gate_tile(2, r) * _pair_heads(o_w, r))

    bpad = kb_ref.shape[1] - ks_ref.shape[1] + Q_BLOCK
    kwin = _mx(kb_ref[0, pl.ds(nstart, bpad + Q_BLOCK), :])
    vwin = _with_ones(_mx(vb_ref[0, pl.ds(nstart, bpad + Q_BLOCK), :]))
    colb = lax.broadcasted_iota(jnp.int32, (1, bpad + Q_BLOCK), 1)
    mask_b = jnp.where(colb + nstart >= bpad, 0.0, NEG)
    o_b = []
    for hh, rs in enumerate(halves):
        sink = jnp.concatenate([jnp.full((Q_BLOCK, LANES), sink_ref[(h % 2) * GROUP + h // 2], F32)
                                for h in range(hh * N_HEADS // 2, (hh + 1) * N_HEADS // 2)], axis=0)
        _, m, ov = softmax_pv(_lane_tiles(_dot_nt(qball[rs, :], kwin) + tswa_ref[rs, :] + mask_b), vwin,
                              lambda m: jnp.maximum(m, sink))
        o_b.append(ov[:, :LANES] / (ov[:, LANES:] + jnp.exp(sink - m)))
    o_b = jnp.concatenate(o_b, axis=0)
    for r in range(GROUP):
        ob_ref[:, r * LANES:(r + 1) * LANES] = _pair_heads(o_b, r).astype(ob_ref.dtype)

    m_s[...] = jnp.full(m_s.shape, NEG, F32)
    acc_s[...] = jnp.zeros(acc_s.shape, F32)

    def flash_update(rs, s, v1):
        s_tiles = _lane_tiles(s)
        m_old = m_s[rs, :]
        m_new = jnp.maximum(m_old, _row_max(s_tiles))
        alpha = jnp.exp(m_old - m_new)
        p = jnp.concatenate([jnp.exp(t - m_new) for t in s_tiles], axis=1)
        acc_s[rs, :] = jnp.concatenate([alpha, alpha], axis=1) * acc_s[rs, :] + _dot(_mx(p), v1)
        m_s[rs, :] = m_new

    n_far_keys = jnp.maximum(i - 1, 0) * Q_BLOCK
    n_chunks = (n_far_keys + SEL_CHUNK - 1) // SEL_CHUNK

    madd = _dot(mneg_far[...], _block_of_key(SEL_CHUNK, 0))
    for rs in halves:
        s_buf[0, rs, :] = s_buf[0, rs, :] + jnp.concatenate([madd] * (GROUP // 2), axis=0)

    def far_body(j, carry):
        v1 = _with_ones(_mx(vs_ref[0, pl.ds(far_start(j), SEL_CHUNK), :]))
        for rs in halves:
            flash_update(rs, s_buf[j % 2, rs, :], v1)
        far_logits(jnp.minimum(j + 1, n_chunks - 1), (j + 1) % 2, True)
        return carry

    lax.fori_loop(0, n_chunks, far_body, 0)
    kc = _mx(ks_ref[0, pl.ds(nstart, 2 * Q_BLOCK), :])
    v1 = _with_ones(_mx(vs_ref[0, pl.ds(nstart, 2 * Q_BLOCK), :]))
    madd = _dot(mneg[...], _block_of_key(2 * Q_BLOCK, 2 * (i - 1)))
    col2 = lax.broadcasted_iota(jnp.int32, (1, 2 * Q_BLOCK), 1)
    mask_n = jnp.where((col2 < Q_BLOCK) & (i == 0), NEG, 0.0)
    for rs in halves:
        s = _dot_nt(qall[rs, :], kc) + jnp.concatenate([madd] * (GROUP // 2), axis=0) + tsel_ref[rs, :] + mask_n
        flash_update(rs, s, v1)
    acc = acc_s[...]
    o_s = acc[:, :LANES] / acc[:, LANES:]
    for r in range(GROUP):
        tile = oa_acc[:, r * LANES:(r + 1) * LANES] + gate_tile(1, r) * _pair_heads(o_s, r)
        oa_ref[:, r * LANES:(r + 1) * LANES] = tile.astype(oa_ref.dtype)


def _rel_bucket_np(dist):
    n = np.maximum(dist, 0)
    max_exact = REL_BUCKETS // 2
    nf = np.maximum(n, 1).astype(np.float32)
    log_b = max_exact + (np.log(nf / max_exact) / math.log(REL_MAX_DIST / max_exact)
                         * (REL_BUCKETS - max_exact)).astype(np.int32)
    log_b = np.minimum(log_b, REL_BUCKETS - 1)
    return np.where(n < max_exact, n, log_b)


def _toeplitz_bias(tab, pad, width, window, shift_far):
    length = width + Q_BLOCK
    dist = pad + Q_BLOCK - 1 - np.arange(length)
    onehot = np.zeros((length, REL_BUCKETS), np.float32)
    onehot[np.arange(length), _rel_bucket_np(dist)] = 1.0
    vals = jnp.dot(jnp.asarray(onehot), tab, precision=lax.Precision.HIGHEST)
    if shift_far:
        vals = vals - tab[REL_BUCKETS - 1][None, :]
    valid = (dist >= 0) & (dist < window)
    vals = jnp.where(jnp.asarray(valid)[:, None], vals, NEG).T
    skew = jnp.tile(vals, (1, Q_BLOCK))[:, :Q_BLOCK * (length - 1)].reshape(N_HEADS, Q_BLOCK, length - 1)
    return skew[:, :, Q_BLOCK - 1:Q_BLOCK - 1 + width].reshape(N_HEADS * Q_BLOCK, width).astype(F32)


def _attention(proj, kvcmp, sinks, bias_table, bsz, seq):
    assert seq % SEL_CHUNK == 0
    nq = seq // Q_BLOCK
    n_far = seq // CMP_STRIDE
    n_sel = seq // SEL_BLOCK
    n_top = min(SEL_TOP_N, n_sel)
    assert n_top >= SEL_INIT_BLOCKS + SEL_LOCAL_BLOCKS and n_sel <= LANES
    wpad = Q_BLOCK * (-(-(NSA_WINDOW - 1) // Q_BLOCK))
    bpad = Q_BLOCK * (-(-(SWA_WINDOW - 1) // Q_BLOCK))
    pair = lambda tab: tab.astype(F32).reshape(REL_BUCKETS, N_GROUPS, GROUP).transpose(0, 2, 1).reshape(REL_BUCKETS, -1)
    tab_a = pair(bias_table[:, :N_HEADS])
    tab_b = pair(bias_table[:, N_HEADS:])
    near_pad = CMP_STRIDE * CMP_FRONT - (CMP_BLOCK - 1)
    t_near = _toeplitz_bias(tab_a, near_pad, CMP_STRIDE * CMP_NEAR, 1 << 30, True)[:, ::CMP_STRIDE]
    t_sel = _toeplitz_bias(tab_a, Q_BLOCK, 2 * Q_BLOCK, 1 << 30, True)
    t_win = _toeplitz_bias(tab_a, wpad, wpad + Q_BLOCK, NSA_WINDOW, False)
    t_swa = _toeplitz_bias(tab_b, bpad, bpad + Q_BLOCK, SWA_WINDOW, False)
    n_rows = kvcmp.shape[2]
    cn = (np.arange(n_rows) - CMP_FRONT)[:, None] * CMP_STRIDE
    sj = np.arange(LANES)[None, :] * SEL_BLOCK
    cmat = ((cn < sj + SEL_BLOCK) & (cn + CMP_BLOCK > sj) & (cn >= 0) & (cn + CMP_BLOCK <= seq)
            & (sj < seq)).astype(np.float32)
    cmat = jnp.asarray(cmat, F32)
    padded = lambda name, p: jnp.pad(proj[name].reshape(bsz, seq, LANES), ((0, 0), (p, 0), (0, 0)))
    ks, vs = padded('ks', Q_BLOCK), padded('vs', Q_BLOCK)
    kw, vw = padded('kw', wpad), padded('vw', wpad)
    kb, vb = padded('kb', bpad), padded('vb', bpad)
    rows = N_HEADS * Q_BLOCK
    qspec = pl.BlockSpec((Q_BLOCK, 4 * LANES), lambda b, i: (b * nq + i, 0))
    const2 = lambda shape: pl.BlockSpec(shape, lambda b, i: (0, 0))
    batch3 = lambda n: pl.BlockSpec((1, n, LANES), lambda b, i: (b, 0, 0))
    kernel = functools.partial(_attn_kernel, n_far=n_far, n_top=n_top)
    return pl.pallas_call(
        kernel,
        grid=(bsz, nq),
        in_specs=[pl.BlockSpec(memory_space=pltpu.SMEM),
                  qspec, qspec,
                  pl.BlockSpec((Q_BLOCK, LANES), lambda b, i: (b * nq + i, 0)),
                  pl.BlockSpec((1, 1, n_rows, LANES), lambda b, i: (0, b, 0, 0)),
                  pl.BlockSpec((1, 1, n_rows, LANES), lambda b, i: (1, b, 0, 0)),
                  batch3(seq + Q_BLOCK), batch3(seq + Q_BLOCK),
                  batch3(seq + wpad), batch3(seq + wpad),
                  batch3(seq + bpad), batch3(seq + bpad),
                  const2((n_rows, LANES)),
                  const2((rows, CMP_NEAR)),
                  const2((rows, 2 * Q_BLOCK)),
                  const2((rows, wpad + Q_BLOCK)),
                  const2((rows, bpad + Q_BLOCK))],
        out_specs=[qspec, qspec],
        out_shape=[jax.ShapeDtypeStruct((bsz * seq, 4 * LANES), BF16)] * 2,
        scratch_shapes=[pltpu.VMEM((rows, LANES), MXU_DTYPE),
                        pltpu.VMEM((rows, LANES), MXU_DTYPE),
                        pltpu.VMEM((N_GROUPS * Q_BLOCK, LANES), MXU_DTYPE),
                        pltpu.VMEM((N_GROUPS * Q_BLOCK, LANES), MXU_DTYPE),
                        pltpu.VMEM((rows, LANES), F32),
                        pltpu.VMEM((rows, 2 * LANES), F32),
                        pltpu.VMEM((2, rows, SEL_CHUNK), F32),
                        pltpu.VMEM((Q_BLOCK, 4 * LANES), F32)],
        compiler_params=pltpu.CompilerParams(dimension_semantics=("arbitrary", "arbitrary"),
                                             vmem_limit_bytes=VMEM_LIMIT),
        name="attention",
    )(sinks.astype(F32), proj['qa'], proj['qb'], proj['ga'], kvcmp, kvcmp, ks, vs, kw, vw, kb, vb,
      cmat, t_near, t_sel, t_win, t_swa)


def _layer_norm(y, g, b):
    mu = jnp.mean(y, axis=-1, keepdims=True)
    yc = y - mu
    var = jnp.mean(yc * yc, axis=-1, keepdims=True)
    return yc * lax.rsqrt(var + LN_EPS) * g + b


def _outproj_kernel(oa_ref, ob_ref, sg_ref, x_ref, pa_ref, pb_ref, wo_ref, g1_ref, b1_ref, wr_ref, rb_ref, sgu_ref,
                    sd_ref, tri_ref, h_ref, base_ref, eidx_ref, gate_ref, rank_ref, cnt_ref, carry):
    step = pl.program_id(0)
    tm = oa_ref.shape[0]

    @pl.when(step == 0)
    def _():
        carry[...] = jnp.zeros(carry.shape, F32)

    sg = sg_ref[...].astype(F32)
    merged = (sg[:, :D_MODEL] * _dot(_mx(oa_ref[...]), pa_ref[...])
              + sg[:, D_MODEL:] * _dot(_mx(ob_ref[...]), pb_ref[...]))
    mix = _dot(_mx(merged), wo_ref[...])
    h = _layer_norm(DN_ALPHA * x_ref[...] + mix, g1_ref[...], b1_ref[...])
    hb = _mx(h)
    h_ref[...] = _pack_bf16_pairs(h)

    gu = _dot(hb, sgu_ref[...])
    shared = _dot(_mx(jax.nn.silu(gu[:, :SHARED_HIDDEN]) * gu[:, SHARED_HIDDEN:]), sd_ref[...])
    base_ref[...] = DN_ALPHA * h + shared

    scores = jax.nn.sigmoid(_dot_nt(wr_ref[...], hb))
    choice = scores + rb_ref[:, 0:1]
    per_group = N_EXPERTS // N_EXPERT_GROUPS
    gs = []
    for g in range(N_EXPERT_GROUPS):
        cg = choice[g * per_group:(g + 1) * per_group]
        m1 = jnp.max(cg, axis=0, keepdims=True)
        is_m = cg == m1
        n_m = jnp.sum(is_m.astype(F32), axis=0, keepdims=True)
        m2 = jnp.max(jnp.where(is_m, -jnp.inf, cg), axis=0, keepdims=True)
        gs.append(m1 + jnp.where(n_m > 1.5, m1, m2))
    gs = jnp.concatenate(gs, axis=0)
    gid = lax.broadcasted_iota(jnp.int32, gs.shape, 0)
    beaten = jnp.zeros(gs.shape, jnp.int32)
    for g in range(N_EXPERT_GROUPS):
        other = gs[g:g + 1]
        beaten = beaten + ((other > gs) | ((other == gs) & (g < gid))).astype(jnp.int32)
    keep_g = beaten < TOPK_EXPERT_GROUPS
    keep = jnp.concatenate([jnp.broadcast_to(keep_g[g:g + 1], (per_group, tm)) for g in range(N_EXPERT_GROUPS)],
                           axis=0)
    cand = jnp.where(keep, choice, -jnp.inf)
    eid = lax.broadcasted_iota(jnp.int32, cand.shape, 0)
    hits = []
    e_rows = []
    w_rows = []
    for _ in range(TOP_K):
        m = jnp.max(cand, axis=0, keepdims=True)
        idx = jnp.min(jnp.where(cand == m, eid, N_EXPERTS), axis=0, keepdims=True)
        hit = eid == idx
        hits.append(hit)
        e_rows.append(idx)
        w_rows.append(jnp.sum(jnp.where(hit, scores, 0.0), axis=0, keepdims=True))
        cand = jnp.where(hit, -jnp.inf, cand)
    w = jnp.concatenate(w_rows, axis=0)
    gate_ref[...] = w / jnp.sum(w, axis=0, keepdims=True) * ROUTED_SCALE
    eidx_ref[...] = jnp.concatenate(e_rows, axis=0)

    onehot = jnp.zeros(cand.shape, F32)
    for hit in hits:
        onehot = onehot + hit.astype(F32)
    before = _dot(onehot.astype(BF16), tri_ref[...]) + carry[:, 0:1]
    rank_ref[...] = jnp.concatenate(
        [jnp.sum(jnp.where(hit, before, 0.0), axis=0, keepdims=True) for hit in hits], axis=0).astype(jnp.int32)
    carry[...] = carry[...] + jnp.sum(onehot, axis=1, keepdims=True)
    cnt_ref[...] = carry[...]


def _out_projection(oa, ob, sg, x2, proj_a, proj_b, w_out, ln_g, ln_b, w_router, router_bias, s_gate, s_up, s_down):
    t = x2.shape[0]
    tm = OUT_TM
    pair_rows = lambda p: p.reshape(N_GROUPS, GROUP, HEAD_DIM, -1).transpose(1, 0, 2, 3).reshape(p.shape)
    pa = pair_rows(proj_a).astype(MXU_DTYPE)
    pb = pair_rows(proj_b).astype(MXU_DTYPE)
    tri = jnp.asarray(np.triu(np.ones((tm, tm), np.float32), 1), BF16)
    row = lambda i: (i, 0)
    fixed = lambda i: (0, 0)
    col = lambda i: (0, i)
    outs = pl.pallas_call(
        _outproj_kernel,
        grid=(t // tm,),
        in_specs=[pl.BlockSpec((tm, 4 * LANES), row), pl.BlockSpec((tm, 4 * LANES), row),
                  pl.BlockSpec((tm, 2 * D_MODEL), row), pl.BlockSpec((tm, D_MODEL), row),
                  pl.BlockSpec((4 * LANES, D_MODEL), fixed), pl.BlockSpec((4 * LANES, D_MODEL), fixed),
                  pl.BlockSpec((D_MODEL, D_MODEL), fixed),
                  pl.BlockSpec((1, D_MODEL), fixed), pl.BlockSpec((1, D_MODEL), fixed),
                  pl.BlockSpec((N_EXPERTS, D_MODEL), fixed), pl.BlockSpec((N_EXPERTS, LANES), fixed),
                  pl.BlockSpec((D_MODEL, 2 * SHARED_HIDDEN), fixed), pl.BlockSpec((SHARED_HIDDEN, D_MODEL), fixed),
                  pl.BlockSpec((tm, tm), fixed)],
        out_specs=[pl.BlockSpec((tm, D_MODEL // 2), row), pl.BlockSpec((tm, D_MODEL), row),
                   pl.BlockSpec((TOP_K, tm), col), pl.BlockSpec((TOP_K, tm), col), pl.BlockSpec((TOP_K, tm), col),
                   pl.BlockSpec((N_EXPERTS, LANES), fixed)],
        out_shape=[jax.ShapeDtypeStruct((t, D_MODEL // 2), jnp.uint32), jax.ShapeDtypeStruct((t, D_MODEL), F32),
                   jax.ShapeDtypeStruct((TOP_K, t), jnp.int32), jax.ShapeDtypeStruct((TOP_K, t), F32),
                   jax.ShapeDtypeStruct((TOP_K, t), jnp.int32), jax.ShapeDtypeStruct((N_EXPERTS, LANES), F32)],
        scratch_shapes=[pltpu.VMEM((N_EXPERTS, LANES), F32)],
        compiler_params=pltpu.CompilerParams(dimension_semantics=("arbitrary",), vmem_limit_bytes=VMEM_LIMIT),
        name="out_projection_router",
    )(oa, ob, sg, x2, pa, pb, w_out.astype(MXU_DTYPE), ln_g.reshape(1, -1), ln_b.reshape(1, -1),
      w_router.T.astype(MXU_DTYPE), jnp.broadcast_to(router_bias.astype(F32)[:, None], (N_EXPERTS, LANES)),
      jnp.concatenate([s_gate, s_up], axis=1).astype(MXU_DTYPE), s_down.astype(MXU_DTYPE), tri)
    return outs


def _rows_to_tiles(x):
    return pltpu.einshape("cml->mcl", jnp.stack(_lane_tiles(x), axis=0))


def _tiles_to_rows(x3):
    xt = pltpu.einshape("mcl->cml", x3)
    return jnp.concatenate([xt[c] for c in range(xt.shape[0])], axis=1)


def _dispatch_kernel(zstart_ref, cnt_ref, dest_ref, h2_ref, xs_ref, h_ref, zeros, sem, zsem):
    step = pl.program_id(0)
    tm = h2_ref.shape[0]
    slot = step % 2
    h_ref[slot] = _rows_to_tiles(h2_ref[...])

    @pl.when(step == 0)
    def _():
        zeros[...] = jnp.zeros(zeros.shape, zeros.dtype)

        def fill(e, c):
            @pl.when(cnt_ref[e] > 0)
            def _():
                cp = pltpu.make_async_copy(zeros, xs_ref.at[pl.ds(zstart_ref[e], MOE_BM)], zsem)
                cp.start()
                cp.wait()
            return c
        lax.fori_loop(0, N_EXPERTS, fill, 0)

    def issue(t, c):
        for k in range(TOP_K):
            pltpu.make_async_copy(h_ref.at[slot, t], xs_ref.at[dest_ref[k, t]], sem.at[slot]).start(priority=k % 2)
        return c
    lax.fori_loop(0, tm, issue, 0)

    def wait_tile(s):
        for k in range(TOP_K):
            pltpu.make_async_copy(h_ref.at[s], xs_ref.at[pl.ds(0, tm)], sem.at[s]).wait()

    @pl.when(step > 0)
    def _():
        wait_tile(1 - slot)

    @pl.when(step + 1 == pl.num_programs(0))
    def _():
        wait_tile(slot)


def _dispatch(h, dest, zstart, counts, n_rows):
    t = h.shape[0]
    tm = DISP_TM
    return pl.pallas_call(
        _dispatch_kernel,
        grid_spec=pltpu.PrefetchScalarGridSpec(
            num_scalar_prefetch=2,
            grid=(t // tm,),
            in_specs=[pl.BlockSpec((TOP_K, tm), lambda i, *_: (0, i), memory_space=pltpu.SMEM),
                      pl.BlockSpec((tm, D_MODEL // 2), lambda i, *_: (i, 0))],
            out_specs=pl.BlockSpec(memory_space=pl.ANY),
            scratch_shapes=[pltpu.VMEM((2, tm) + PACKED_ROW_TILE, jnp.uint32),
                            pltpu.VMEM((MOE_BM,) + PACKED_ROW_TILE, jnp.uint32),
                            pltpu.SemaphoreType.DMA((2,)), pltpu.SemaphoreType.DMA(())]),
        out_shape=jax.ShapeDtypeStruct((n_rows,) + PACKED_ROW_TILE, jnp.uint32),
        compiler_params=pltpu.CompilerParams(dimension_semantics=("arbitrary",), vmem_limit_bytes=VMEM_LIMIT),
        name="moe_dispatch",
    )(zstart, counts, dest, h)


def _experts_kernel(blk_e_ref, nused_ref, xs_ref, wg_ref, wu_ref, wd_ref, ys_ref, wg_s, wu_s, wd_s):
    b = pl.program_id(0)
    prev = blk_e_ref[jnp.maximum(b - 1, 0)]

    @pl.when((b == 0) | (blk_e_ref[b] != prev))
    def _():
        wg_s[...] = _mx(wg_ref[0])
        wu_s[...] = _mx(wu_ref[0])
        wd_s[...] = _mx(wd_ref[0])

    @pl.when(b < nused_ref[0])
    def _():
        xb = _mx(jnp.concatenate(_unpack_bf16_pairs(_tiles_to_rows(xs_ref[...])), axis=1))
        hid = jax.nn.silu(_dot(xb, wg_s[...])) * _dot(xb, wu_s[...])
        ys_ref[...] = _rows_to_tiles(_pack_bf16_pairs(_dot(_mx(hid), wd_s[...])))

    @pl.when(b >= nused_ref[0])
    def _():
        ys_ref[...] = jnp.zeros(ys_ref.shape, ys_ref.dtype)


def _experts(xs, blk_e, nused, e_gate, e_up, e_down):
    n_rows = xs.shape[0]
    n_blocks = n_rows // MOE_BM
    xmap = lambda b, be, nu: (jnp.minimum(b, nu[0] - 1), 0, 0)
    wmap = lambda b, be, nu: (be[b], 0, 0)
    return pl.pallas_call(
        _experts_kernel,
        grid_spec=pltpu.PrefetchScalarGridSpec(
            num_scalar_prefetch=2,
            grid=(n_blocks,),
            in_specs=[pl.BlockSpec((MOE_BM,) + PACKED_ROW_TILE, xmap),
                      pl.BlockSpec((1, D_MODEL, EXPERT_HIDDEN), wmap),
                      pl.BlockSpec((1, D_MODEL, EXPERT_HIDDEN), wmap),
                      pl.BlockSpec((1, EXPERT_HIDDEN, D_MODEL), wmap)],
            out_specs=pl.BlockSpec((MOE_BM,) + PACKED_ROW_TILE, lambda b, be, nu: (b, 0, 0)),
            scratch_shapes=[pltpu.VMEM((D_MODEL, EXPERT_HIDDEN), MXU_DTYPE),
                            pltpu.VMEM((D_MODEL, EXPERT_HIDDEN), MXU_DTYPE),
                            pltpu.VMEM((EXPERT_HIDDEN, D_MODEL), MXU_DTYPE)]),
        out_shape=jax.ShapeDtypeStruct((n_rows,) + PACKED_ROW_TILE, jnp.uint32),
        compiler_params=pltpu.CompilerParams(dimension_semantics=("arbitrary",), vmem_limit_bytes=VMEM_LIMIT),
        name="moe_experts",
    )(blk_e, nused, xs, e_gate, e_up, e_down)


def _combine_kernel(dest_ref, dest_next_ref, gate_ref, base_ref, g2_ref, b2_ref, ys_ref, out_ref, buf, routed, sem):
    step = pl.program_id(0)
    tm = base_ref.shape[0]
    slot = step % 2

    def gather_rows(d_ref, s, t):
        for k in range(TOP_K):
            pltpu.make_async_copy(ys_ref.at[d_ref[k, t]], buf.at[s, k, t], sem.at[s]).start(priority=k % 2)

    def combine_token(t):
        low = jnp.zeros(PACKED_ROW_TILE, F32)
        high = jnp.zeros(PACKED_ROW_TILE, F32)
        for k in range(TOP_K):
            lo_k, hi_k = _unpack_bf16_pairs(buf[slot, k, t])
            low = low + gate_ref[k, t] * lo_k
            high = high + gate_ref[k, t] * hi_k
        routed[t] = jnp.concatenate([low, high], axis=0)

    def for_tokens(body):
        def step_fn(t, c):
            body(t)
            return c
        lax.fori_loop(0, tm, step_fn, 0)

    @pl.when(step == 0)
    def _():
        for_tokens(lambda t: gather_rows(dest_ref, 0, t))

    for k in range(TOP_K):
        pltpu.make_async_copy(ys_ref.at[pl.ds(0, tm)], buf.at[slot, k], sem.at[slot]).wait()

    @pl.when(step + 1 < pl.num_programs(0))
    def _():
        def both(t):
            gather_rows(dest_next_ref, 1 - slot, t)
            combine_token(t)
        for_tokens(both)

    @pl.when(step + 1 == pl.num_programs(0))
    def _():
        for_tokens(combine_token)

    out_ref[...] = _layer_norm(base_ref[...] + _tiles_to_rows(routed[...]), g2_ref[...], b2_ref[...])


def _combine(ys3, dest, gate, base, ln_g, ln_b):
    t = base.shape[0]
    tm = COMB_TM
    n_tiles = t // tm
    return pl.pallas_call(
        _combine_kernel,
        grid=(n_tiles,),
        in_specs=[pl.BlockSpec((TOP_K, tm), lambda i: (0, i), memory_space=pltpu.SMEM),
                  pl.BlockSpec((TOP_K, tm), lambda i: (0, jnp.minimum(i + 1, n_tiles - 1)), memory_space=pltpu.SMEM),
                  pl.BlockSpec((TOP_K, tm), lambda i: (0, i), memory_space=pltpu.SMEM),
                  pl.BlockSpec((tm, D_MODEL), lambda i: (i, 0)),
                  pl.BlockSpec((1, D_MODEL), lambda i: (0, 0)),
                  pl.BlockSpec((1, D_MODEL), lambda i: (0, 0)),
                  pl.BlockSpec(memory_space=pl.ANY)],
        out_specs=pl.BlockSpec((tm, D_MODEL), lambda i: (i, 0)),
        out_shape=jax.ShapeDtypeStruct((t, D_MODEL), F32),
        scratch_shapes=[pltpu.VMEM((2, TOP_K, tm) + PACKED_ROW_TILE, jnp.uint32), pltpu.VMEM((tm,) + ROW_TILE, F32),
                        pltpu.SemaphoreType.DMA((2,))],
        compiler_params=pltpu.CompilerParams(dimension_semantics=("arbitrary",), vmem_limit_bytes=VMEM_LIMIT),
        name="moe_combine",
    )(dest, dest, gate, base, ln_g.reshape(1, -1), ln_b.reshape(1, -1), ys3)


def _moe_layout(eidx, rank, counts):
    n_assign = eidx.size
    n_blocks = (n_assign + N_EXPERTS * (MOE_BM - 1)) // MOE_BM
    padded = (counts + MOE_BM - 1) // MOE_BM * MOE_BM
    pends = jnp.cumsum(padded)
    pstarts = pends - padded
    experts = jnp.arange(N_EXPERTS, dtype=jnp.int32)
    dest = jnp.sum(jnp.where(eidx[..., None] == experts, pstarts, 0), axis=-1) + rank
    block_row = jnp.arange(n_blocks, dtype=jnp.int32) * MOE_BM
    blk_e = jnp.minimum(jnp.sum(pends[None, :] <= block_row[:, None], axis=1), N_EXPERTS - 1).astype(jnp.int32)
    nused = (pends[-1:] // MOE_BM).astype(jnp.int32)
    zstart = jnp.maximum(pends - MOE_BM, 0).astype(jnp.int32)
    return dest.astype(jnp.int32), blk_e, nused, zstart, n_blocks * MOE_BM


def _layer(x, w_in, cmp_pe, cmp_w1, cmp_b1, cmp_w2, sinks, bias_table, proj_a, proj_b, w_out, ln1_g, ln1_b,
           w_router, router_bias, e_gate, e_up, e_down, s_gate, s_up, s_down, ln2_g, ln2_b):
    bsz, seq, d = x.shape
    x2 = x.reshape(bsz * seq, d)
    proj = _in_projection(x2, w_in)
    kvcmp = _compress(proj['kc'], proj['vc'], bsz, seq, cmp_pe, cmp_w1, cmp_b1, cmp_w2)
    oa, ob = _attention(proj, kvcmp, sinks, bias_table, bsz, seq)
    h, base, eidx, gate, rank, cnt = _out_projection(oa, ob, proj['sg'], x2, proj_a, proj_b, w_out, ln1_g, ln1_b,
                                                     w_router, router_bias, s_gate, s_up, s_down)
    counts = cnt[:, 0].astype(jnp.int32)
    dest, blk_e, nused, zstart, n_rows = _moe_layout(eidx, rank, counts)
    xs = _dispatch(h, dest, zstart, counts, n_rows)
    ys = _experts(xs, blk_e, nused, e_gate, e_up, e_down)
    out = _combine(ys, dest, gate, base, ln2_g, ln2_b)
    return out.reshape(bsz, seq, d)


def kernel(x, w_in, cmp_pe, cmp_w1, cmp_b1, cmp_w2, attn_sinks, rel_bias_table, proj_a, proj_b, w_out, ln1_g, ln1_b,
           w_router, router_bias, expert_w_gate, expert_w_up, expert_w_down, shared_w_gate, shared_w_up,
           shared_w_down, ln2_g, ln2_b):
    h = x
    for l in range(DEPTH):
        h = _layer(h, w_in[l], cmp_pe[l], cmp_w1[l], cmp_b1[l], cmp_w2[l], attn_sinks[l], rel_bias_table, proj_a[l],
                   proj_b[l], w_out[l], ln1_g[l], ln1_b[l], w_router[l], router_bias[l], expert_w_gate[l],
                   expert_w_up[l], expert_w_down[l], shared_w_gate[l], shared_w_up[l], shared_w_down[l], ln2_g[l],
                   ln2_b[l])
    return h
```

```python
import functools
import math

import numpy as np
import jax
import jax.numpy as jnp
from jax import lax
from jax.experimental import pallas as pl
from jax.experimental.pallas import tpu as pltpu

F32 = jnp.float32
BF16 = jnp.bfloat16
MXU_DTYPE = jnp.bfloat16

D_MODEL = 1024
HEAD_DIM = 64
ATTN_SCALE = HEAD_DIM ** -0.5
Q_BLOCK = 128
N_HEADS = 8
N_GROUPS = 2
GROUP = 4
CMP_BLOCK = 32
CMP_STRIDE = 16
CMP_HIDDEN = 128
SEL_BLOCK = 64
SEL_TOP_N = 8
SEL_INIT_BLOCKS = 1
SEL_LOCAL_BLOCKS = 2
NSA_WINDOW = 512
SWA_WINDOW = 128
REL_BUCKETS = 32
REL_MAX_DIST = 128
N_EXPERTS = 256
TOP_K = 8
EXPERT_HIDDEN = 256
SHARED_HIDDEN = 256
N_EXPERT_GROUPS = 8
TOPK_EXPERT_GROUPS = 4
ROUTED_SCALE = 2.5
LN_EPS = 1e-5
DEPTH = 1
DN_ALPHA = (2 * DEPTH) ** 0.25

NEG = -1e30
LANES = 128
ROW_TILE = (8, LANES)
PACKED_ROW_TILE = (4, LANES)
CMP_FRONT = 16
CMP_NEAR = LANES
SEL_CHUNK = 1024
VMEM_LIMIT = 56 * 1024 * 1024

IN_TM = 512
OUT_TM = 512
MOE_BM = 256
DISP_TM = 256
COMB_TM = 128


def _dot(a, b):
    return jnp.dot(a, b, preferred_element_type=F32)


def _dot_nt(a, b):
    return lax.dot_general(a, b, (((1,), (1,)), ((), ())), preferred_element_type=F32)


def _mx(a):
    return a.astype(MXU_DTYPE)


def _pack_bf16_pairs(x):
    half = x.shape[1] // 2
    bits = lax.bitcast_convert_type(x.astype(BF16).astype(F32), jnp.uint32)
    return (bits[:, half:] & jnp.uint32(0xFFFF0000)) | (bits[:, :half] >> 16)


def _unpack_bf16_pairs(words):
    return (lax.bitcast_convert_type(words << 16, F32),
            lax.bitcast_convert_type(words & jnp.uint32(0xFFFF0000), F32))


_IN_COLS = (('qa', 512), ('qb', 512), ('kc', 128), ('vc', 128), ('ks', 128), ('vs', 128), ('kw', 128),
            ('vw', 128), ('kb', 128), ('vb', 128), ('ga', 128), ('sg', 2048))


def _inproj_kernel(x_ref, w_ref, qa_ref, qb_ref, kc_ref, vc_ref, ks_ref, vs_ref, kw_ref, vw_ref, kb_ref, vb_ref,
                   ga_ref, sg_ref):
    xb = _mx(x_ref[...])
    outs = dict(qa=qa_ref, qb=qb_ref, kc=kc_ref, vc=vc_ref, ks=ks_ref, vs=vs_ref, kw=kw_ref, vw=vw_ref,
                kb=kb_ref, vb=vb_ref, ga=ga_ref, sg=sg_ref)
    off = 0
    for name, width in _IN_COLS:
        for c0 in range(0, width, 512):
            cw = min(512, width - c0)
            y = _dot(xb, w_ref[:, off + c0:off + c0 + cw])
            if name in ('ga', 'sg'):
                y = jax.nn.sigmoid(y)
            outs[name][:, c0:c0 + cw] = y.astype(outs[name].dtype)
        off += width


def _pair_head_columns(w):
    return w.reshape(w.shape[0], N_GROUPS, GROUP, HEAD_DIM).transpose(0, 2, 1, 3).reshape(w.shape[0], -1)


def _in_projection(x2, w_in):
    t = x2.shape[0]
    sizes = (512, 128, 128, 128, 128, 128, 128, 24, 512, 128, 128, 1024, 1024)
    offs = np.cumsum((0,) + sizes)
    part = [w_in[:, offs[k]:offs[k + 1]] for k in range(len(sizes))]
    w_qa, w_kc, w_vc, w_ks, w_vs, w_kw, w_vw, w_g, w_qb, w_kb, w_vb, w_gate_a, w_gate_b = part
    w_qa = _pair_head_columns(w_qa) * ATTN_SCALE
    w_qb = _pair_head_columns(w_qb) * ATTN_SCALE
    w_ga = w_g.reshape(-1, N_GROUPS, GROUP, 3).transpose(0, 3, 2, 1).reshape(-1, 24)
    w_ga = jnp.pad(w_ga, ((0, 0), (0, LANES - 24)))
    w_all = jnp.concatenate([w_qa, w_qb, w_kc, w_vc, w_ks, w_vs, w_kw, w_vw, w_kb, w_vb, w_ga, w_gate_a, w_gate_b],
                            axis=1).astype(MXU_DTYPE)
    n_all = w_all.shape[1]
    out_shape = []
    out_specs = []
    for name, width in _IN_COLS:
        dt = F32 if name == 'ga' else BF16
        out_shape.append(jax.ShapeDtypeStruct((t, width), dt))
        out_specs.append(pl.BlockSpec((IN_TM, width), lambda i: (i, 0)))
    outs = pl.pallas_call(
        _inproj_kernel,
        grid=(t // IN_TM,),
        in_specs=[pl.BlockSpec((IN_TM, D_MODEL), lambda i: (i, 0)),
                  pl.BlockSpec((D_MODEL, n_all), lambda i: (0, 0))],
        out_specs=out_specs,
        out_shape=out_shape,
        compiler_params=pltpu.CompilerParams(dimension_semantics=("arbitrary",), vmem_limit_bytes=VMEM_LIMIT),
        name="in_projection",
    )(x2, w_all)
    return dict(zip([n for n, _ in _IN_COLS], outs))


def _compress_kernel(tok_ref, w1_ref, pe_ref, w1o_ref, b1_ref, w2_ref, out_ref):
    n_chunks = tok_ref.shape[2]
    ab = _dot(tok_ref[0, 0], w1_ref[0])
    a = ab[:, :2 * CMP_HIDDEN]
    b_next = pltpu.roll(ab[:, 2 * CMP_HIDDEN:], n_chunks - 1, 0)
    cb = _dot(_mx(pe_ref[0]), _mx(w1o_ref[0]))[0:1, :] + b1_ref[0]
    cb2 = jnp.concatenate([cb, cb], axis=1)
    hid = jax.nn.gelu(a + b_next + cb2)
    out = _dot(_mx(hid), w2_ref[0])
    row = lax.broadcasted_iota(jnp.int32, out.shape, 0)
    out = jnp.where(row < n_chunks - 1, out, 0.0)
    out_ref[0, 0, 0:CMP_FRONT, :] = jnp.zeros((CMP_FRONT, LANES), F32)
    out_ref[0, 0, CMP_FRONT:CMP_FRONT + n_chunks, :] = out
    out_ref[0, 0, CMP_FRONT + n_chunks:, :] = jnp.zeros((CMP_NEAR - CMP_FRONT, LANES), F32)


def _compress(kc, vc, bsz, seq, cmp_pe, cmp_w1, cmp_b1, cmp_w2):
    n_chunks = seq // CMP_STRIDE
    tok = jnp.stack([kc, vc]).reshape(2, bsz, n_chunks, CMP_STRIDE * LANES)
    eye = jnp.eye(N_GROUPS, dtype=F32)
    w1r = cmp_w1.reshape(2, 2, CMP_STRIDE, HEAD_DIM, CMP_HIDDEN)
    w1 = jnp.einsum('khjdn,gG->kjgdhGn', w1r, eye).reshape(2, CMP_STRIDE * LANES, 4 * CMP_HIDDEN).astype(MXU_DTYPE)
    w2 = jnp.einsum('knd,gG->kgnGd', cmp_w2, eye).reshape(2, 2 * CMP_HIDDEN, LANES).astype(MXU_DTYPE)
    pe = jnp.pad(cmp_pe.reshape(2, 1, CMP_BLOCK * HEAD_DIM), ((0, 0), (0, 7), (0, 0)))
    b1 = cmp_b1.reshape(2, 1, CMP_HIDDEN)
    rows = CMP_FRONT + n_chunks + CMP_NEAR - CMP_FRONT
    return pl.pallas_call(
        _compress_kernel,
        grid=(2, bsz),
        in_specs=[pl.BlockSpec((1, 1, n_chunks, CMP_STRIDE * LANES), lambda k, b: (k, b, 0, 0)),
                  pl.BlockSpec((1, CMP_STRIDE * LANES, 4 * CMP_HIDDEN), lambda k, b: (k, 0, 0)),
                  pl.BlockSpec((1, 8, CMP_BLOCK * HEAD_DIM), lambda k, b: (k, 0, 0)),
                  pl.BlockSpec((1, CMP_BLOCK * HEAD_DIM, CMP_HIDDEN), lambda k, b: (k, 0, 0)),
                  pl.BlockSpec((1, 1, CMP_HIDDEN), lambda k, b: (k, 0, 0)),
                  pl.BlockSpec((1, 2 * CMP_HIDDEN, LANES), lambda k, b: (k, 0, 0))],
        out_specs=pl.BlockSpec((1, 1, rows, LANES), lambda k, b: (k, b, 0, 0)),
        out_shape=jax.ShapeDtypeStruct((2, bsz, rows, LANES), F32),
        compiler_params=pltpu.CompilerParams(dimension_semantics=("arbitrary", "arbitrary"),
                                             vmem_limit_bytes=VMEM_LIMIT),
        name="nsa_compress",
    )(tok, w1, pe, cmp_w1, b1, w2)


def _stack_heads(q_ref, dst):
    lo = lax.broadcasted_iota(jnp.int32, (Q_BLOCK, LANES), 1) < HEAD_DIM
    for r in range(GROUP):
        qr = q_ref[:, r * LANES:(r + 1) * LANES].astype(dst.dtype)
        z = jnp.zeros_like(qr)
        dst[(2 * r) * Q_BLOCK:(2 * r + 1) * Q_BLOCK, :] = jnp.where(lo, qr, z)
        dst[(2 * r + 1) * Q_BLOCK:(2 * r + 2) * Q_BLOCK, :] = jnp.where(lo, z, qr)


def _pair_heads(o, r):
    lo = lax.broadcasted_iota(jnp.int32, (Q_BLOCK, LANES), 1) < HEAD_DIM
    return jnp.where(lo, o[(2 * r) * Q_BLOCK:(2 * r + 1) * Q_BLOCK], o[(2 * r + 1) * Q_BLOCK:(2 * r + 2) * Q_BLOCK])


def _lane_tiles(x):
    return [x[:, t * LANES:(t + 1) * LANES] for t in range(x.shape[1] // LANES)]


def _row_max(tiles):
    mx = tiles[0]
    for t in tiles[1:]:
        mx = jnp.maximum(mx, t)
    return jnp.broadcast_to(jnp.max(mx, axis=1, keepdims=True), mx.shape)


def _with_ones(v):
    return jnp.concatenate([v, jnp.ones(v.shape, v.dtype)], axis=1)


def _block_of_key(n_keys, first_block):
    b = lax.broadcasted_iota(jnp.int32, (LANES, n_keys), 0)
    k = lax.broadcasted_iota(jnp.int32, (LANES, n_keys), 1)
    return (b == (k // SEL_BLOCK) + first_block).astype(MXU_DTYPE)


def _select_blocks_t(imp_t, i, n_top):
    blk = lax.broadcasted_iota(jnp.int32, imp_t.shape, 0)
    qcol = lax.broadcasted_iota(jnp.int32, imp_t.shape, 1)
    back = (2 * i + (qcol >= SEL_BLOCK).astype(jnp.int32)) - blk
    sel = (back >= 0) & ((blk < SEL_INIT_BLOCKS) | (back < SEL_LOCAL_BLOCKS))
    cand = jnp.where((back >= SEL_LOCAL_BLOCKS) & (blk >= SEL_INIT_BLOCKS), imp_t, -1.0)
    blk_f = blk.astype(F32)
    for _ in range(n_top - SEL_INIT_BLOCKS - SEL_LOCAL_BLOCKS):
        m = jnp.max(cand, axis=0, keepdims=True)
        idx = jnp.min(jnp.where(cand == m, blk_f, float(LANES)), axis=0, keepdims=True)
        hit = blk_f == idx
        sel = sel | (hit & (m >= 0.0))
        cand = jnp.where(hit, -2.0, cand)
    return sel


def _attn_kernel(sink_ref, qa_ref, qb_ref, ga_ref, kcmp_ref, vcmp_ref, ks_ref, vs_ref, kw_ref, vw_ref, kb_ref,
                 vb_ref, cmat_ref, tnear_ref, tsel_ref, twin_ref, tswa_ref, oa_ref, ob_ref,
                 qall, qball, mneg, mneg_far, m_s, acc_s, s_buf, oa_acc, *, n_far, n_top):
    i = pl.program_id(1)
    rows = N_HEADS * Q_BLOCK
    half = rows // 2
    halves = (slice(0, half), slice(half, rows))
    _stack_heads(qa_ref, qall)
    _stack_heads(qb_ref, qball)
    nstart = pl.multiple_of(i * Q_BLOCK, Q_BLOCK)
    lo = lax.broadcasted_iota(jnp.int32, (Q_BLOCK, LANES), 1) < HEAD_DIM
    gates = ga_ref[...]

    def gate_tile(c, r):
        return jnp.where(lo, gates[:, c * 8 + 2 * r:c * 8 + 2 * r + 1], gates[:, c * 8 + 2 * r + 1:c * 8 + 2 * r + 2])

    def softmax_pv(s_tiles, v1, fix_max=None):
        m = _row_max(s_tiles)
        if fix_max is not None:
            m = fix_max(m)
        e = [jnp.exp(t - m) for t in s_tiles]
        return e, m, _dot(_mx(jnp.concatenate(e, axis=1)), v1)

    off = pl.multiple_of(i * (Q_BLOCK // CMP_STRIDE), 8)
    k_cmp = _mx(jnp.concatenate([kcmp_ref[0, 0, 0:n_far, :], kcmp_ref[0, 0, pl.ds(off, CMP_NEAR), :]], axis=0))
    v_cmp = _with_ones(_mx(jnp.concatenate([vcmp_ref[0, 0, 0:n_far, :], vcmp_ref[0, 0, pl.ds(off, CMP_NEAR), :]],
                                           axis=0)))
    colf = lax.broadcasted_iota(jnp.int32, (1, n_far), 1)
    coln = lax.broadcasted_iota(jnp.int32, (1, CMP_NEAR), 1)
    col_ok = jnp.concatenate([(colf >= CMP_FRONT) & (colf < off), coln + off >= CMP_FRONT], axis=1)
    mask_c = jnp.where(col_ok, 0.0, NEG)
    no_key = lambda m: jnp.where(m > 0.5 * NEG, m, 0.0)
    p_cmp, o_c = [], []
    for rs in halves:
        tiles = _lane_tiles(_dot_nt(qall[rs, :], k_cmp) + mask_c)
        tiles[-1] = tiles[-1] + tnear_ref[rs, :]
        e, _, ov = softmax_pv(tiles, v_cmp, no_key)
        inv = 1.0 / jnp.maximum(ov[:, LANES:], 1e-30)
        o_c.append(ov[:, :LANES] * inv)
        p_cmp.append([t * inv for t in e])
    o_c = jnp.concatenate(o_c, axis=0)

    def far_start(j):
        return pl.multiple_of(Q_BLOCK + j * SEL_CHUNK, Q_BLOCK)

    def far_logits(j, slot, masked):
        kc = _mx(ks_ref[0, pl.ds(far_start(j), SEL_CHUNK), :])
        if masked:
            madd = _dot(mneg_far[...], _block_of_key(SEL_CHUNK, j * (SEL_CHUNK // SEL_BLOCK)))
        for rs in halves:
            s = _dot_nt(qall[rs, :], kc)
            s_buf[slot, rs, :] = s + jnp.concatenate([madd] * (GROUP // 2), axis=0) if masked else s

    far_logits(0, 0, False)

    blkcol = lax.broadcasted_iota(jnp.int32, (Q_BLOCK, LANES), 1)
    n_tiles = len(p_cmp[0])
    for g in range(N_GROUPS):
        imp = jnp.zeros((Q_BLOCK, LANES), F32)
        for t in range(n_tiles):
            pg = sum(p_cmp[r // 2][t][(2 * (r % 2) + g) * Q_BLOCK:(2 * (r % 2) + g + 1) * Q_BLOCK]
                     for r in range(GROUP))
            if t < n_tiles - 1:
                cm = _mx(cmat_ref[t * LANES:(t + 1) * LANES, :])
            else:
                cm = _mx(cmat_ref[pl.ds(off, CMP_NEAR), :])
            hi = _mx(pg)
            low = _mx(pg - hi.astype(F32))
            imp = imp + _dot(hi, cm) + _dot(low, cm)
        sel = _select_blocks_t(imp.T, i, n_top)
        neg = jnp.where(sel, 0.0, NEG).T
        mneg[g * Q_BLOCK:(g + 1) * Q_BLOCK, :] = neg.astype(mneg.dtype)
        mneg_far[g * Q_BLOCK:(g + 1) * Q_BLOCK, :] = jnp.where(blkcol < 2 * (i - 1), neg, NEG).astype(mneg.dtype)

    wpad = kw_ref.shape[1] - ks_ref.shape[1] + Q_BLOCK
    kwin = _mx(kw_ref[0, pl.ds(nstart, wpad + Q_BLOCK), :])
    vwin = _with_ones(_mx(vw_ref[0, pl.ds(nstart, wpad + Q_BLOCK), :]))
    colw = lax.broadcasted_iota(jnp.int32, (1, wpad + Q_BLOCK), 1)
    mask_w = jnp.where(colw + nstart >= wpad, 0.0, NEG)
    o_w = []
    for rs in halves:
        _, _, ov = softmax_pv(_lane_tiles(_dot_nt(qall[rs, :], kwin) + twin_ref[rs, :] + mask_w), vwin)
        o_w.append(ov[:, :LANES] / ov[:, LANES:])
    o_w = jnp.concatenate(o_w, axis=0)
    for r in range(GROUP):
        oa_acc[:, r * LANES:(r + 1) * LANES] = (gate_tile(0, r) * _pair_heads(o_c, r)
                                                + gate_tile(2, r) * _pair_heads(o_w, r))

    bpad = kb_ref.shape[1] - ks_ref.shape[1] + Q_BLOCK
    kwin = _mx(kb_ref[0, pl.ds(nstart, bpad + Q_BLOCK), :])
    vwin = _with_ones(_mx(vb_ref[0, pl.ds(nstart, bpad + Q_BLOCK), :]))
    colb = lax.broadcasted_iota(jnp.int32, (1, bpad + Q_BLOCK), 1)
    mask_b = jnp.where(colb + nstart >= bpad, 0.0, NEG)
    o_b = []
    for hh, rs in enumerate(halves):
        sink = jnp.concatenate([jnp.full((Q_BLOCK, LANES), sink_ref[(h % 2) * GROUP + h // 2], F32)
                                for h in range(hh * N_HEADS // 2, (hh + 1) * N_HEADS // 2)], axis=0)
        _, m, ov = softmax_pv(_lane_tiles(_dot_nt(qball[rs, :], kwin) + tswa_ref[rs, :] + mask_b), vwin,
                              lambda m: jnp.maximum(m, sink))
        o_b.append(ov[:, :LANES] / (ov[:, LANES:] + jnp.exp(sink - m)))
    o_b = jnp.concatenate(o_b, axis=0)
    for r in range(GROUP):
        ob_ref[:, r * LANES:(r + 1) * LANES] = _pair_heads(o_b, r).astype(ob_ref.dtype)

    m_s[...] = jnp.full(m_s.shape, NEG, F32)
    acc_s[...] = jnp.zeros(acc_s.shape, F32)

    def flash_update(rs, s, v1):
        s_tiles = _lane_tiles(s)
        m_old = m_s[rs, :]
        m_new = jnp.maximum(m_old, _row_max(s_tiles))
        alpha = jnp.exp(m_old - m_new)
        p = jnp.concatenate([jnp.exp(t - m_new) for t in s_tiles], axis=1)
        acc_s[rs, :] = jnp.concatenate([alpha, alpha], axis=1) * acc_s[rs, :] + _dot(_mx(p), v1)
        m_s[rs, :] = m_new

    n_far_keys = jnp.maximum(i - 1, 0) * Q_BLOCK
    n_chunks = (n_far_keys + SEL_CHUNK - 1) // SEL_CHUNK

    madd = _dot(mneg_far[...], _block_of_key(SEL_CHUNK, 0))
    for rs in halves:
        s_buf[0, rs, :] = s_buf[0, rs, :] + jnp.concatenate([madd] * (GROUP // 2), axis=0)

    def far_update(j):
        v1 = _with_ones(_mx(vs_ref[0, pl.ds(far_start(j), SEL_CHUNK), :]))
        for rs in halves:
            flash_update(rs, s_buf[j % 2, rs, :], v1)

    def far_body(j, carry):
        far_update(j)
        far_logits(j + 1, (j + 1) % 2, True)
        return carry

    last = jnp.maximum(n_chunks - 1, 0)
    lax.fori_loop(0, last, far_body, 0)
    far_update(last)
    kc = _mx(ks_ref[0, pl.ds(nstart, 2 * Q_BLOCK), :])
    v1 = _with_ones(_mx(vs_ref[0, pl.ds(nstart, 2 * Q_BLOCK), :]))
    madd = _dot(mneg[...], _block_of_key(2 * Q_BLOCK, 2 * (i - 1)))
    col2 = lax.broadcasted_iota(jnp.int32, (1, 2 * Q_BLOCK), 1)
    mask_n = jnp.where((col2 < Q_BLOCK) & (i == 0), NEG, 0.0)
    for rs in halves:
        s = _dot_nt(qall[rs, :], kc) + jnp.concatenate([madd] * (GROUP // 2), axis=0) + tsel_ref[rs, :] + mask_n
        flash_update(rs, s, v1)
    acc = acc_s[...]
    o_s = acc[:, :LANES] / acc[:, LANES:]
    for r in range(GROUP):
        tile = oa_acc[:, r * LANES:(r + 1) * LANES] + gate_tile(1, r) * _pair_heads(o_s, r)
        oa_ref[:, r * LANES:(r + 1) * LANES] = tile.astype(oa_ref.dtype)


def _rel_bucket_np(dist):
    n = np.maximum(dist, 0)
    max_exact = REL_BUCKETS // 2
    nf = np.maximum(n, 1).astype(np.float32)
    log_b = max_exact + (np.log(nf / max_exact) / math.log(REL_MAX_DIST / max_exact)
                         * (REL_BUCKETS - max_exact)).astype(np.int32)
    log_b = np.minimum(log_b, REL_BUCKETS - 1)
    return np.where(n < max_exact, n, log_b)


def _toeplitz_bias(tab, pad, width, window, shift_far):
    length = width + Q_BLOCK
    dist = pad + Q_BLOCK - 1 - np.arange(length)
    onehot = np.zeros((length, REL_BUCKETS), np.float32)
    onehot[np.arange(length), _rel_bucket_np(dist)] = 1.0
    vals = jnp.dot(jnp.asarray(onehot), tab, precision=lax.Precision.HIGHEST)
    if shift_far:
        vals = vals - tab[REL_BUCKETS - 1][None, :]
    valid = (dist >= 0) & (dist < window)
    vals = jnp.where(jnp.asarray(valid)[:, None], vals, NEG).T
    skew = jnp.tile(vals, (1, Q_BLOCK))[:, :Q_BLOCK * (length - 1)].reshape(N_HEADS, Q_BLOCK, length - 1)
    return skew[:, :, Q_BLOCK - 1:Q_BLOCK - 1 + width].reshape(N_HEADS * Q_BLOCK, width).astype(F32)


def _attention(proj, kvcmp, sinks, bias_table, bsz, seq):
    assert seq % SEL_CHUNK == 0
    nq = seq // Q_BLOCK
    n_far = seq // CMP_STRIDE
    n_sel = seq // SEL_BLOCK
    n_top = min(SEL_TOP_N, n_sel)
    assert n_top >= SEL_INIT_BLOCKS + SEL_LOCAL_BLOCKS and n_sel <= LANES
    wpad = Q_BLOCK * (-(-(NSA_WINDOW - 1) // Q_BLOCK))
    bpad = Q_BLOCK * (-(-(SWA_WINDOW - 1) // Q_BLOCK))
    pair = lambda tab: tab.astype(F32).reshape(REL_BUCKETS, N_GROUPS, GROUP).transpose(0, 2, 1).reshape(REL_BUCKETS, -1)
    tab_a = pair(bias_table[:, :N_HEADS])
    tab_b = pair(bias_table[:, N_HEADS:])
    near_pad = CMP_STRIDE * CMP_FRONT - (CMP_BLOCK - 1)
    t_near = _toeplitz_bias(tab_a, near_pad, CMP_STRIDE * CMP_NEAR, 1 << 30, True)[:, ::CMP_STRIDE]
    t_sel = _toeplitz_bias(tab_a, Q_BLOCK, 2 * Q_BLOCK, 1 << 30, True)
    t_win = _toeplitz_bias(tab_a, wpad, wpad + Q_BLOCK, NSA_WINDOW, False)
    t_swa = _toeplitz_bias(tab_b, bpad, bpad + Q_BLOCK, SWA_WINDOW, False)
    n_rows = kvcmp.shape[2]
    cn = (np.arange(n_rows) - CMP_FRONT)[:, None] * CMP_STRIDE
    sj = np.arange(LANES)[None, :] * SEL_BLOCK
    cmat = ((cn < sj + SEL_BLOCK) & (cn + CMP_BLOCK > sj) & (cn >= 0) & (cn + CMP_BLOCK <= seq)
            & (sj < seq)).astype(np.float32)
    cmat = jnp.asarray(cmat, F32)
    padded = lambda name, p: jnp.pad(proj[name].reshape(bsz, seq, LANES), ((0, 0), (p, 0), (0, 0)))
    ks, vs = padded('ks', Q_BLOCK), padded('vs', Q_BLOCK)
    kw, vw = padded('kw', wpad), padded('vw', wpad)
    kb, vb = padded('kb', bpad), padded('vb', bpad)
    rows = N_HEADS * Q_BLOCK
    qspec = pl.BlockSpec((Q_BLOCK, 4 * LANES), lambda b, i: (b * nq + i, 0))
    const2 = lambda shape: pl.BlockSpec(shape, lambda b, i: (0, 0))
    batch3 = lambda n: pl.BlockSpec((1, n, LANES), lambda b, i: (b, 0, 0))
    kernel = functools.partial(_attn_kernel, n_far=n_far, n_top=n_top)
    return pl.pallas_call(
        kernel,
        grid=(bsz, nq),
        in_specs=[pl.BlockSpec(memory_space=pltpu.SMEM),
                  qspec, qspec,
                  pl.BlockSpec((Q_BLOCK, LANES), lambda b, i: (b * nq + i, 0)),
                  pl.BlockSpec((1, 1, n_rows, LANES), lambda b, i: (0, b, 0, 0)),
                  pl.BlockSpec((1, 1, n_rows, LANES), lambda b, i: (1, b, 0, 0)),
                  batch3(seq + Q_BLOCK), batch3(seq + Q_BLOCK),
                  batch3(seq + wpad), batch3(seq + wpad),
                  batch3(seq + bpad), batch3(seq + bpad),
                  const2((n_rows, LANES)),
                  const2((rows, CMP_NEAR)),
                  const2((rows, 2 * Q_BLOCK)),
                  const2((rows, wpad + Q_BLOCK)),
                  const2((rows, bpad + Q_BLOCK))],
        out_specs=[qspec, qspec],
        out_shape=[jax.ShapeDtypeStruct((bsz * seq, 4 * LANES), BF16)] * 2,
        scratch_shapes=[pltpu.VMEM((rows, LANES), MXU_DTYPE),
                        pltpu.VMEM((rows, LANES), MXU_DTYPE),
                        pltpu.VMEM((N_GROUPS * Q_BLOCK, LANES), MXU_DTYPE),
                        pltpu.VMEM((N_GROUPS * Q_BLOCK, LANES), MXU_DTYPE),
                        pltpu.VMEM((rows, LANES), F32),
                        pltpu.VMEM((rows, 2 * LANES), F32),
                        pltpu.VMEM((2, rows, SEL_CHUNK), F32),
                        pltpu.VMEM((Q_BLOCK, 4 * LANES), F32)],
        compiler_params=pltpu.CompilerParams(dimension_semantics=("arbitrary", "arbitrary"),
                                             vmem_limit_bytes=VMEM_LIMIT),
        name="attention",
    )(sinks.astype(F32), proj['qa'], proj['qb'], proj['ga'], kvcmp, kvcmp, ks, vs, kw, vw, kb, vb,
      cmat, t_near, t_sel, t_win, t_swa)


def _layer_norm(y, g, b):
    mu = jnp.mean(y, axis=-1, keepdims=True)
    yc = y - mu
    var = jnp.mean(yc * yc, axis=-1, keepdims=True)
    return yc * lax.rsqrt(var + LN_EPS) * g + b


def _outproj_kernel(oa_ref, ob_ref, sg_ref, x_ref, pa_ref, pb_ref, wo_ref, g1_ref, b1_ref, wr_ref, rb_ref, sgu_ref,
                    sd_ref, tri_ref, h_ref, base_ref, eidx_ref, gate_ref, rank_ref, cnt_ref, carry):
    step = pl.program_id(0)
    tm = oa_ref.shape[0]

    @pl.when(step == 0)
    def _():
        carry[...] = jnp.zeros(carry.shape, F32)

    sg = sg_ref[...].astype(F32)
    merged = (sg[:, :D_MODEL] * _dot(_mx(oa_ref[...]), pa_ref[...])
              + sg[:, D_MODEL:] * _dot(_mx(ob_ref[...]), pb_ref[...]))
    mix = _dot(_mx(merged), wo_ref[...])
    h = _layer_norm(DN_ALPHA * x_ref[...] + mix, g1_ref[...], b1_ref[...])
    hb = _mx(h)
    h_ref[...] = _pack_bf16_pairs(h)

    gu = _dot(hb, sgu_ref[...])
    shared = _dot(_mx(jax.nn.silu(gu[:, :SHARED_HIDDEN]) * gu[:, SHARED_HIDDEN:]), sd_ref[...])
    base_ref[...] = DN_ALPHA * h + shared

    scores = jax.nn.sigmoid(_dot_nt(wr_ref[...], hb))
    choice = scores + rb_ref[:, 0:1]
    per_group = N_EXPERTS // N_EXPERT_GROUPS
    gs = []
    for g in range(N_EXPERT_GROUPS):
        cg = choice[g * per_group:(g + 1) * per_group]
        m1 = jnp.max(cg, axis=0, keepdims=True)
        is_m = cg == m1
        n_m = jnp.sum(is_m.astype(F32), axis=0, keepdims=True)
        m2 = jnp.max(jnp.where(is_m, -jnp.inf, cg), axis=0, keepdims=True)
        gs.append(m1 + jnp.where(n_m > 1.5, m1, m2))
    gs = jnp.concatenate(gs, axis=0)
    gid = lax.broadcasted_iota(jnp.int32, gs.shape, 0)
    beaten = jnp.zeros(gs.shape, jnp.int32)
    for g in range(N_EXPERT_GROUPS):
        other = gs[g:g + 1]
        beaten = beaten + ((other > gs) | ((other == gs) & (g < gid))).astype(jnp.int32)
    keep_g = beaten < TOPK_EXPERT_GROUPS
    keep = jnp.concatenate([jnp.broadcast_to(keep_g[g:g + 1], (per_group, tm)) for g in range(N_EXPERT_GROUPS)],
                           axis=0)
    cand = jnp.where(keep, choice, -jnp.inf)
    eid = lax.broadcasted_iota(jnp.int32, cand.shape, 0)
    hits = []
    e_rows = []
    w_rows = []
    for _ in range(TOP_K):
        m = jnp.max(cand, axis=0, keepdims=True)
        idx = jnp.min(jnp.where(cand == m, eid, N_EXPERTS), axis=0, keepdims=True)
        hit = eid == idx
        hits.append(hit)
        e_rows.append(idx)
        w_rows.append(jnp.sum(jnp.where(hit, scores, 0.0), axis=0, keepdims=True))
        cand = jnp.where(hit, -jnp.inf, cand)
    w = jnp.concatenate(w_rows, axis=0)
    gate_ref[...] = w / jnp.sum(w, axis=0, keepdims=True) * ROUTED_SCALE
    eidx_ref[...] = jnp.concatenate(e_rows, axis=0)

    onehot = jnp.zeros(cand.shape, F32)
    for hit in hits:
        onehot = onehot + hit.astype(F32)
    before = _dot(onehot.astype(BF16), tri_ref[...]) + carry[:, 0:1]
    rank_ref[...] = jnp.concatenate(
        [jnp.sum(jnp.where(hit, before, 0.0), axis=0, keepdims=True) for hit in hits], axis=0).astype(jnp.int32)
    carry[...] = carry[...] + jnp.sum(onehot, axis=1, keepdims=True)
    cnt_ref[...] = carry[...]


def _out_projection(oa, ob, sg, x2, proj_a, proj_b, w_out, ln_g, ln_b, w_router, router_bias, s_gate, s_up, s_down):
    t = x2.shape[0]
    tm = OUT_TM
    pair_rows = lambda p: p.reshape(N_GROUPS, GROUP, HEAD_DIM, -1).transpose(1, 0, 2, 3).reshape(p.shape)
    pa = pair_rows(proj_a).astype(MXU_DTYPE)
    pb = pair_rows(proj_b).astype(MXU_DTYPE)
    tri = jnp.asarray(np.triu(np.ones((tm, tm), np.float32), 1), BF16)
    row = lambda i: (i, 0)
    fixed = lambda i: (0, 0)
    col = lambda i: (0, i)
    outs = pl.pallas_call(
        _outproj_kernel,
        grid=(t // tm,),
        in_specs=[pl.BlockSpec((tm, 4 * LANES), row), pl.BlockSpec((tm, 4 * LANES), row),
                  pl.BlockSpec((tm, 2 * D_MODEL), row), pl.BlockSpec((tm, D_MODEL), row),
                  pl.BlockSpec((4 * LANES, D_MODEL), fixed), pl.BlockSpec((4 * LANES, D_MODEL), fixed),
                  pl.BlockSpec((D_MODEL, D_MODEL), fixed),
                  pl.BlockSpec((1, D_MODEL), fixed), pl.BlockSpec((1, D_MODEL), fixed),
                  pl.BlockSpec((N_EXPERTS, D_MODEL), fixed), pl.BlockSpec((N_EXPERTS, LANES), fixed),
                  pl.BlockSpec((D_MODEL, 2 * SHARED_HIDDEN), fixed), pl.BlockSpec((SHARED_HIDDEN, D_MODEL), fixed),
                  pl.BlockSpec((tm, tm), fixed)],
        out_specs=[pl.BlockSpec((tm, D_MODEL // 2), row), pl.BlockSpec((tm, D_MODEL), row),
                   pl.BlockSpec((TOP_K, tm), col), pl.BlockSpec((TOP_K, tm), col), pl.BlockSpec((TOP_K, tm), col),
                   pl.BlockSpec((N_EXPERTS, LANES), fixed)],
        out_shape=[jax.ShapeDtypeStruct((t, D_MODEL // 2), jnp.uint32), jax.ShapeDtypeStruct((t, D_MODEL), F32),
                   jax.ShapeDtypeStruct((TOP_K, t), jnp.int32), jax.ShapeDtypeStruct((TOP_K, t), F32),
                   jax.ShapeDtypeStruct((TOP_K, t), jnp.int32), jax.ShapeDtypeStruct((N_EXPERTS, LANES), F32)],
        scratch_shapes=[pltpu.VMEM((N_EXPERTS, LANES), F32)],
        compiler_params=pltpu.CompilerParams(dimension_semantics=("arbitrary",), vmem_limit_bytes=VMEM_LIMIT),
        name="out_projection_router",
    )(oa, ob, sg, x2, pa, pb, w_out.astype(MXU_DTYPE), ln_g.reshape(1, -1), ln_b.reshape(1, -1),
      w_router.T.astype(MXU_DTYPE), jnp.broadcast_to(router_bias.astype(F32)[:, None], (N_EXPERTS, LANES)),
      jnp.concatenate([s_gate, s_up], axis=1).astype(MXU_DTYPE), s_down.astype(MXU_DTYPE), tri)
    return outs


def _rows_to_tiles(x):
    return pltpu.einshape("cml->mcl", jnp.stack(_lane_tiles(x), axis=0))


def _tiles_to_rows(x3):
    xt = pltpu.einshape("mcl->cml", x3)
    return jnp.concatenate([xt[c] for c in range(xt.shape[0])], axis=1)


def _dispatch_kernel(zstart_ref, cnt_ref, dest_ref, h2_ref, xs_ref, h_ref, zeros, sem, zsem):
    step = pl.program_id(0)
    tm = h2_ref.shape[0]
    slot = step % 2
    h_ref[slot] = _rows_to_tiles(h2_ref[...])

    @pl.when(step == 0)
    def _():
        zeros[...] = jnp.zeros(zeros.shape, zeros.dtype)

        def fill(e, c):
            @pl.when(cnt_ref[e] > 0)
            def _():
                cp = pltpu.make_async_copy(zeros, xs_ref.at[pl.ds(zstart_ref[e], MOE_BM)], zsem)
                cp.start()
                cp.wait()
            return c
        lax.fori_loop(0, N_EXPERTS, fill, 0)

    def issue(t, c):
        for k in range(TOP_K):
            pltpu.make_async_copy(h_ref.at[slot, t], xs_ref.at[dest_ref[k, t]], sem.at[slot]).start(priority=k % 2)
        return c
    lax.fori_loop(0, tm, issue, 0)

    def wait_tile(s):
        for k in range(TOP_K):
            pltpu.make_async_copy(h_ref.at[s], xs_ref.at[pl.ds(0, tm)], sem.at[s]).wait()

    @pl.when(step > 0)
    def _():
        wait_tile(1 - slot)

    @pl.when(step + 1 == pl.num_programs(0))
    def _():
        wait_tile(slot)


def _dispatch(h, dest, zstart, counts, n_rows):
    t = h.shape[0]
    tm = DISP_TM
    return pl.pallas_call(
        _dispatch_kernel,
        grid_spec=pltpu.PrefetchScalarGridSpec(
            num_scalar_prefetch=2,
            grid=(t // tm,),
            in_specs=[pl.BlockSpec((TOP_K, tm), lambda i, *_: (0, i), memory_space=pltpu.SMEM),
                      pl.BlockSpec((tm, D_MODEL // 2), lambda i, *_: (i, 0))],
            out_specs=pl.BlockSpec(memory_space=pl.ANY),
            scratch_shapes=[pltpu.VMEM((2, tm) + PACKED_ROW_TILE, jnp.uint32),
                            pltpu.VMEM((MOE_BM,) + PACKED_ROW_TILE, jnp.uint32),
                            pltpu.SemaphoreType.DMA((2,)), pltpu.SemaphoreType.DMA(())]),
        out_shape=jax.ShapeDtypeStruct((n_rows,) + PACKED_ROW_TILE, jnp.uint32),
        compiler_params=pltpu.CompilerParams(dimension_semantics=("arbitrary",), vmem_limit_bytes=VMEM_LIMIT),
        name="moe_dispatch",
    )(zstart, counts, dest, h)


def _experts_kernel(blk_e_ref, nused_ref, xs_ref, wg_ref, wu_ref, wd_ref, ys_ref, wg_s, wu_s, wd_s):
    b = pl.program_id(0)
    prev = blk_e_ref[jnp.maximum(b - 1, 0)]

    @pl.when((b == 0) | (blk_e_ref[b] != prev))
    def _():
        wg_s[...] = _mx(wg_ref[0])
        wu_s[...] = _mx(wu_ref[0])
        wd_s[...] = _mx(wd_ref[0])

    @pl.when(b < nused_ref[0])
    def _():
        xb = _mx(jnp.concatenate(_unpack_bf16_pairs(_tiles_to_rows(xs_ref[...])), axis=1))
        hid = jax.nn.silu(_dot(xb, wg_s[...])) * _dot(xb, wu_s[...])
        ys_ref[...] = _rows_to_tiles(_pack_bf16_pairs(_dot(_mx(hid), wd_s[...])))

    @pl.when(b >= nused_ref[0])
    def _():
        ys_ref[...] = jnp.zeros(ys_ref.shape, ys_ref.dtype)


def _experts(xs, blk_e, nused, e_gate, e_up, e_down):
    n_rows = xs.shape[0]
    n_blocks = n_rows // MOE_BM
    xmap = lambda b, be, nu: (jnp.minimum(b, nu[0] - 1), 0, 0)
    wmap = lambda b, be, nu: (be[b], 0, 0)
    return pl.pallas_call(
        _experts_kernel,
        grid_spec=pltpu.PrefetchScalarGridSpec(
            num_scalar_prefetch=2,
            grid=(n_blocks,),
            in_specs=[pl.BlockSpec((MOE_BM,) + PACKED_ROW_TILE, xmap),
                      pl.BlockSpec((1, D_MODEL, EXPERT_HIDDEN), wmap),
                      pl.BlockSpec((1, D_MODEL, EXPERT_HIDDEN), wmap),
                      pl.BlockSpec((1, EXPERT_HIDDEN, D_MODEL), wmap)],
            out_specs=pl.BlockSpec((MOE_BM,) + PACKED_ROW_TILE, lambda b, be, nu: (b, 0, 0)),
            scratch_shapes=[pltpu.VMEM((D_MODEL, EXPERT_HIDDEN), MXU_DTYPE),
                            pltpu.VMEM((D_MODEL, EXPERT_HIDDEN), MXU_DTYPE),
                            pltpu.VMEM((EXPERT_HIDDEN, D_MODEL), MXU_DTYPE)]),
        out_shape=jax.ShapeDtypeStruct((n_rows,) + PACKED_ROW_TILE, jnp.uint32),
        compiler_params=pltpu.CompilerParams(dimension_semantics=("arbitrary",), vmem_limit_bytes=VMEM_LIMIT),
        name="moe_experts",
    )(blk_e, nused, xs, e_gate, e_up, e_down)


def _combine_kernel(dest_ref, dest_next_ref, gate_ref, base_ref, g2_ref, b2_ref, ys_ref, out_ref, buf, routed, sem):
    step = pl.program_id(0)
    tm = base_ref.shape[0]
    slot = step % 2

    def gather_rows(d_ref, s, t):
        for k in range(TOP_K):
            j = t * TOP_K + k
            pltpu.make_async_copy(ys_ref.at[d_ref[j]], buf.at[s, j], sem.at[s]).start(priority=k % 2)

    def combine_token(t):
        low = jnp.zeros(PACKED_ROW_TILE, F32)
        high = jnp.zeros(PACKED_ROW_TILE, F32)
        for k in range(TOP_K):
            j = t * TOP_K + k
            lo_k, hi_k = _unpack_bf16_pairs(buf[slot, j])
            low = low + gate_ref[j] * lo_k
            high = high + gate_ref[j] * hi_k
        routed[t] = jnp.concatenate([low, high], axis=0)

    def for_tokens(body):
        def step_fn(t, c):
            body(t)
            return c
        lax.fori_loop(0, tm, step_fn, 0)

    @pl.when(step == 0)
    def _():
        for_tokens(lambda t: gather_rows(dest_ref, 0, t))

    pltpu.make_async_copy(ys_ref.at[pl.ds(0, tm * TOP_K)], buf.at[slot], sem.at[slot]).wait()

    @pl.when(step + 1 < pl.num_programs(0))
    def _():
        def both(t):
            gather_rows(dest_next_ref, 1 - slot, t)
            combine_token(t)
        for_tokens(both)

    @pl.when(step + 1 == pl.num_programs(0))
    def _():
        for_tokens(combine_token)

    out_ref[...] = _layer_norm(base_ref[...] + _tiles_to_rows(routed[...]), g2_ref[...], b2_ref[...])


def _combine(ys3, dest, gate, base, ln_g, ln_b):
    t = base.shape[0]
    tm = COMB_TM
    n_tiles = t // tm
    dest_tk = dest.T.reshape(-1)
    return pl.pallas_call(
        _combine_kernel,
        grid=(n_tiles,),
        in_specs=[pl.BlockSpec((tm * TOP_K,), lambda i: (i,), memory_space=pltpu.SMEM),
                  pl.BlockSpec((tm * TOP_K,), lambda i: (jnp.minimum(i + 1, n_tiles - 1),), memory_space=pltpu.SMEM),
                  pl.BlockSpec((tm * TOP_K,), lambda i: (i,), memory_space=pltpu.SMEM),
                  pl.BlockSpec((tm, D_MODEL), lambda i: (i, 0)),
                  pl.BlockSpec((1, D_MODEL), lambda i: (0, 0)),
                  pl.BlockSpec((1, D_MODEL), lambda i: (0, 0)),
                  pl.BlockSpec(memory_space=pl.ANY)],
        out_specs=pl.BlockSpec((tm, D_MODEL), lambda i: (i, 0)),
        out_shape=jax.ShapeDtypeStruct((t, D_MODEL), F32),
        scratch_shapes=[pltpu.VMEM((2, tm * TOP_K) + PACKED_ROW_TILE, jnp.uint32), pltpu.VMEM((tm,) + ROW_TILE, F32),
                        pltpu.SemaphoreType.DMA((2,))],
        compiler_params=pltpu.CompilerParams(dimension_semantics=("arbitrary",), vmem_limit_bytes=VMEM_LIMIT),
        name="moe_combine",
    )(dest_tk, dest_tk, gate.T.reshape(-1), base, ln_g.reshape(1, -1), ln_b.reshape(1, -1), ys3)


def _moe_layout(eidx, rank, counts):
    n_assign = eidx.size
    n_blocks = (n_assign + N_EXPERTS * (MOE_BM - 1)) // MOE_BM
    padded = (counts + MOE_BM - 1) // MOE_BM * MOE_BM
    pends = jnp.cumsum(padded)
    pstarts = pends - padded
    experts = jnp.arange(N_EXPERTS, dtype=jnp.int32)
    dest = jnp.sum(jnp.where(eidx[..., None] == experts, pstarts, 0), axis=-1) + rank
    block_row = jnp.arange(n_blocks, dtype=jnp.int32) * MOE_BM
    blk_e = jnp.minimum(jnp.sum(pends[None, :] <= block_row[:, None], axis=1), N_EXPERTS - 1).astype(jnp.int32)
    nused = (pends[-1:] // MOE_BM).astype(jnp.int32)
    zstart = jnp.maximum(pends - MOE_BM, 0).astype(jnp.int32)
    return dest.astype(jnp.int32), blk_e, nused, zstart, n_blocks * MOE_BM


def _layer(x, w_in, cmp_pe, cmp_w1, cmp_b1, cmp_w2, sinks, bias_table, proj_a, proj_b, w_out, ln1_g, ln1_b,
           w_router, router_bias, e_gate, e_up, e_down, s_gate, s_up, s_down, ln2_g, ln2_b):
    bsz, seq, d = x.shape
    x2 = x.reshape(bsz * seq, d)
    proj = _in_projection(x2, w_in)
    kvcmp = _compress(proj['kc'], proj['vc'], bsz, seq, cmp_pe, cmp_w1, cmp_b1, cmp_w2)
    oa, ob = _attention(proj, kvcmp, sinks, bias_table, bsz, seq)
    h, base, eidx, gate, rank, cnt = _out_projection(oa, ob, proj['sg'], x2, proj_a, proj_b, w_out, ln1_g, ln1_b,
                                                     w_router, router_bias, s_gate, s_up, s_down)
    counts = cnt[:, 0].astype(jnp.int32)
    dest, blk_e, nused, zstart, n_rows = _moe_layout(eidx, rank, counts)
    xs = _dispatch(h, dest, zstart, counts, n_rows)
    ys = _experts(xs, blk_e, nused, e_gate, e_up, e_down)
    out = _combine(ys, dest, gate, base, ln2_g, ln2_b)
    return out.reshape(bsz, seq, d)


def kernel(x, w_in, cmp_pe, cmp_w1, cmp_b1, cmp_w2, attn_sinks, rel_bias_table, proj_a, proj_b, w_out, ln1_g, ln1_b,
           w_router, router_bias, expert_w_gate, expert_w_up, expert_w_down, shared_w_gate, shared_w_up,
           shared_w_down, ln2_g, ln2_b):
    h = x
    for l in range(DEPTH):
        h = _layer(h, w_in[l], cmp_pe[l], cmp_w1[l], cmp_b1[l], cmp_w2[l], attn_sinks[l], rel_bias_table, proj_a[l],
                   proj_b[l], w_out[l], ln1_g[l], ln1_b[l], w_router[l], router_bias[l], expert_w_gate[l],
                   expert_w_up[l], expert_w_down[l], shared_w_gate[l], shared_w_up[l], shared_w_down[l], ln2_g[l],
                   ln2_b[l])
    return h
```

```python
import functools
import math

import numpy as np
import jax
import jax.numpy as jnp
from jax import lax
from jax.experimental import pallas as pl
from jax.experimental.pallas import tpu as pltpu

F32 = jnp.float32
BF16 = jnp.bfloat16
MXU_DTYPE = jnp.bfloat16

D_MODEL = 1024
HEAD_DIM = 64
ATTN_SCALE = HEAD_DIM ** -0.5
Q_BLOCK = 128
N_HEADS = 8
N_GROUPS = 2
GROUP = 4
CMP_BLOCK = 32
CMP_STRIDE = 16
CMP_HIDDEN = 128
SEL_BLOCK = 64
SEL_TOP_N = 8
SEL_INIT_BLOCKS = 1
SEL_LOCAL_BLOCKS = 2
NSA_WINDOW = 512
SWA_WINDOW = 128
REL_BUCKETS = 32
REL_MAX_DIST = 128
N_EXPERTS = 256
TOP_K = 8
EXPERT_HIDDEN = 256
SHARED_HIDDEN = 256
N_EXPERT_GROUPS = 8
TOPK_EXPERT_GROUPS = 4
ROUTED_SCALE = 2.5
LN_EPS = 1e-5
DEPTH = 1
DN_ALPHA = (2 * DEPTH) ** 0.25

NEG = -1e30
LANES = 128
ROW_TILE = (8, LANES)
PACKED_ROW_TILE = (4, LANES)
CMP_FRONT = 16
CMP_NEAR = LANES
SEL_CHUNK = 1024
VMEM_LIMIT = 56 * 1024 * 1024

IN_TM = 512
OUT_TM = 512
MOE_BM = 512
DISP_TM = 256
COMB_TM = 256


def _dot(a, b):
    return jnp.dot(a, b, preferred_element_type=F32)


def _dot_nt(a, b):
    return lax.dot_general(a, b, (((1,), (1,)), ((), ())), preferred_element_type=F32)


def _mx(a):
    return a.astype(MXU_DTYPE)


def _pack_bf16_pairs(x):
    half = x.shape[1] // 2
    bits = lax.bitcast_convert_type(x.astype(BF16).astype(F32), jnp.uint32)
    return (bits[:, half:] & jnp.uint32(0xFFFF0000)) | (bits[:, :half] >> 16)


def _unpack_bf16_pairs(words):
    return (lax.bitcast_convert_type(words << 16, F32),
            lax.bitcast_convert_type(words & jnp.uint32(0xFFFF0000), F32))


_IN_COLS = (('qa', 512), ('qb', 512), ('kc', 128), ('vc', 128), ('ks', 128), ('vs', 128), ('kw', 128),
            ('vw', 128), ('kb', 128), ('vb', 128), ('ga', 128), ('sg', 2048))


def _inproj_kernel(x_ref, w_ref, qa_ref, qb_ref, kc_ref, vc_ref, ks_ref, vs_ref, kw_ref, vw_ref, kb_ref, vb_ref,
                   ga_ref, sg_ref):
    xb = _mx(x_ref[...])
    outs = dict(qa=qa_ref, qb=qb_ref, kc=kc_ref, vc=vc_ref, ks=ks_ref, vs=vs_ref, kw=kw_ref, vw=vw_ref,
                kb=kb_ref, vb=vb_ref, ga=ga_ref, sg=sg_ref)
    off = 0
    for name, width in _IN_COLS:
        for c0 in range(0, width, 512):
            cw = min(512, width - c0)
            y = _dot(xb, w_ref[:, off + c0:off + c0 + cw])
            if name in ('ga', 'sg'):
                y = jax.nn.sigmoid(y)
            outs[name][:, c0:c0 + cw] = y.astype(outs[name].dtype)
        off += width


def _pair_head_columns(w):
    return w.reshape(w.shape[0], N_GROUPS, GROUP, HEAD_DIM).transpose(0, 2, 1, 3).reshape(w.shape[0], -1)


def _in_projection(x2, w_in):
    t = x2.shape[0]
    sizes = (512, 128, 128, 128, 128, 128, 128, 24, 512, 128, 128, 1024, 1024)
    offs = np.cumsum((0,) + sizes)
    part = [w_in[:, offs[k]:offs[k + 1]] for k in range(len(sizes))]
    w_qa, w_kc, w_vc, w_ks, w_vs, w_kw, w_vw, w_g, w_qb, w_kb, w_vb, w_gate_a, w_gate_b = part
    w_qa = _pair_head_columns(w_qa) * ATTN_SCALE
    w_qb = _pair_head_columns(w_qb) * ATTN_SCALE
    w_ga = w_g.reshape(-1, N_GROUPS, GROUP, 3).transpose(0, 3, 2, 1).reshape(-1, 24)
    w_ga = jnp.pad(w_ga, ((0, 0), (0, LANES - 24)))
    w_all = jnp.concatenate([w_qa, w_qb, w_kc, w_vc, w_ks, w_vs, w_kw, w_vw, w_kb, w_vb, w_ga, w_gate_a, w_gate_b],
                            axis=1).astype(MXU_DTYPE)
    n_all = w_all.shape[1]
    out_shape = []
    out_specs = []
    for name, width in _IN_COLS:
        dt = F32 if name == 'ga' else BF16
        out_shape.append(jax.ShapeDtypeStruct((t, width), dt))
        out_specs.append(pl.BlockSpec((IN_TM, width), lambda i: (i, 0)))
    outs = pl.pallas_call(
        _inproj_kernel,
        grid=(t // IN_TM,),
        in_specs=[pl.BlockSpec((IN_TM, D_MODEL), lambda i: (i, 0)),
                  pl.BlockSpec((D_MODEL, n_all), lambda i: (0, 0))],
        out_specs=out_specs,
        out_shape=out_shape,
        compiler_params=pltpu.CompilerParams(dimension_semantics=("arbitrary",), vmem_limit_bytes=VMEM_LIMIT),
        name="in_projection",
    )(x2, w_all)
    return dict(zip([n for n, _ in _IN_COLS], outs))


def _compress_kernel(tok_ref, w1_ref, pe_ref, w1o_ref, b1_ref, w2_ref, out_ref):
    n_chunks = tok_ref.shape[2]
    ab = _dot(tok_ref[0, 0], w1_ref[0])
    a = ab[:, :2 * CMP_HIDDEN]
    b_next = pltpu.roll(ab[:, 2 * CMP_HIDDEN:], n_chunks - 1, 0)
    cb = _dot(_mx(pe_ref[0]), _mx(w1o_ref[0]))[0:1, :] + b1_ref[0]
    cb2 = jnp.concatenate([cb, cb], axis=1)
    hid = jax.nn.gelu(a + b_next + cb2)
    out = _dot(_mx(hid), w2_ref[0])
    row = lax.broadcasted_iota(jnp.int32, out.shape, 0)
    out = jnp.where(row < n_chunks - 1, out, 0.0)
    out_ref[0, 0, 0:CMP_FRONT, :] = jnp.zeros((CMP_FRONT, LANES), F32)
    out_ref[0, 0, CMP_FRONT:CMP_FRONT + n_chunks, :] = out
    out_ref[0, 0, CMP_FRONT + n_chunks:, :] = jnp.zeros((CMP_NEAR - CMP_FRONT, LANES), F32)


def _compress(kc, vc, bsz, seq, cmp_pe, cmp_w1, cmp_b1, cmp_w2):
    n_chunks = seq // CMP_STRIDE
    tok = jnp.stack([kc, vc]).reshape(2, bsz, n_chunks, CMP_STRIDE * LANES)
    eye = jnp.eye(N_GROUPS, dtype=F32)
    w1r = cmp_w1.reshape(2, 2, CMP_STRIDE, HEAD_DIM, CMP_HIDDEN)
    w1 = jnp.einsum('khjdn,gG->kjgdhGn', w1r, eye).reshape(2, CMP_STRIDE * LANES, 4 * CMP_HIDDEN).astype(MXU_DTYPE)
    w2 = jnp.einsum('knd,gG->kgnGd', cmp_w2, eye).reshape(2, 2 * CMP_HIDDEN, LANES).astype(MXU_DTYPE)
    pe = jnp.pad(cmp_pe.reshape(2, 1, CMP_BLOCK * HEAD_DIM), ((0, 0), (0, 7), (0, 0)))
    b1 = cmp_b1.reshape(2, 1, CMP_HIDDEN)
    rows = CMP_FRONT + n_chunks + CMP_NEAR - CMP_FRONT
    return pl.pallas_call(
        _compress_kernel,
        grid=(2, bsz),
        in_specs=[pl.BlockSpec((1, 1, n_chunks, CMP_STRIDE * LANES), lambda k, b: (k, b, 0, 0)),
                  pl.BlockSpec((1, CMP_STRIDE * LANES, 4 * CMP_HIDDEN), lambda k, b: (k, 0, 0)),
                  pl.BlockSpec((1, 8, CMP_BLOCK * HEAD_DIM), lambda k, b: (k, 0, 0)),
                  pl.BlockSpec((1, CMP_BLOCK * HEAD_DIM, CMP_HIDDEN), lambda k, b: (k, 0, 0)),
                  pl.BlockSpec((1, 1, CMP_HIDDEN), lambda k, b: (k, 0, 0)),
                  pl.BlockSpec((1, 2 * CMP_HIDDEN, LANES), lambda k, b: (k, 0, 0))],
        out_specs=pl.BlockSpec((1, 1, rows, LANES), lambda k, b: (k, b, 0, 0)),
        out_shape=jax.ShapeDtypeStruct((2, bsz, rows, LANES), F32),
        compiler_params=pltpu.CompilerParams(dimension_semantics=("arbitrary", "arbitrary"),
                                             vmem_limit_bytes=VMEM_LIMIT),
        name="nsa_compress",
    )(tok, w1, pe, cmp_w1, b1, w2)


def _stack_heads(q_ref, dst):
    lo = lax.broadcasted_iota(jnp.int32, (Q_BLOCK, LANES), 1) < HEAD_DIM
    for r in range(GROUP):
        qr = q_ref[:, r * LANES:(r + 1) * LANES].astype(dst.dtype)
        z = jnp.zeros_like(qr)
        dst[(2 * r) * Q_BLOCK:(2 * r + 1) * Q_BLOCK, :] = jnp.where(lo, qr, z)
        dst[(2 * r + 1) * Q_BLOCK:(2 * r + 2) * Q_BLOCK, :] = jnp.where(lo, z, qr)


def _pair_heads(o, r):
    lo = lax.broadcasted_iota(jnp.int32, (Q_BLOCK, LANES), 1) < HEAD_DIM
    return jnp.where(lo, o[(2 * r) * Q_BLOCK:(2 * r + 1) * Q_BLOCK], o[(2 * r + 1) * Q_BLOCK:(2 * r + 2) * Q_BLOCK])


def _lane_tiles(x):
    return [x[:, t * LANES:(t + 1) * LANES] for t in range(x.shape[1] // LANES)]


def _row_max(tiles):
    mx = tiles[0]
    for t in tiles[1:]:
        mx = jnp.maximum(mx, t)
    return jnp.broadcast_to(jnp.max(mx, axis=1, keepdims=True), mx.shape)


def _with_ones(v):
    return jnp.concatenate([v, jnp.ones(v.shape, v.dtype)], axis=1)


def _block_of_key(n_keys, first_block):
    b = lax.broadcasted_iota(jnp.int32, (LANES, n_keys), 0)
    k = lax.broadcasted_iota(jnp.int32, (LANES, n_keys), 1)
    return (b == (k // SEL_BLOCK) + first_block).astype(MXU_DTYPE)


def _select_blocks_t(imp_t, i, n_top):
    blk = lax.broadcasted_iota(jnp.int32, imp_t.shape, 0)
    qcol = lax.broadcasted_iota(jnp.int32, imp_t.shape, 1)
    back = (2 * i + (qcol >= SEL_BLOCK).astype(jnp.int32)) - blk
    sel = (back >= 0) & ((blk < SEL_INIT_BLOCKS) | (back < SEL_LOCAL_BLOCKS))
    cand = jnp.where((back >= SEL_LOCAL_BLOCKS) & (blk >= SEL_INIT_BLOCKS), imp_t, -1.0)
    blk_f = blk.astype(F32)
    for _ in range(n_top - SEL_INIT_BLOCKS - SEL_LOCAL_BLOCKS):
        m = jnp.max(cand, axis=0, keepdims=True)
        idx = jnp.min(jnp.where(cand == m, blk_f, float(LANES)), axis=0, keepdims=True)
        hit = blk_f == idx
        sel = sel | (hit & (m >= 0.0))
        cand = jnp.where(hit, -2.0, cand)
    return sel


def _attn_kernel(sink_ref, qa_ref, qb_ref, ga_ref, kcmp_ref, vcmp_ref, ks_ref, vs_ref, kw_ref, vw_ref, kb_ref,
                 vb_ref, cmat_ref, tnear_ref, tsel_ref, twin_ref, tswa_ref, oa_ref, ob_ref,
                 qall, qball, mneg, mneg_far, m_s, acc_s, s_buf, oa_acc, *, n_far, n_top):
    i = pl.program_id(1)
    rows = N_HEADS * Q_BLOCK
    half = rows // 2
    halves = (slice(0, half), slice(half, rows))
    _stack_heads(qa_ref, qall)
    _stack_heads(qb_ref, qball)
    nstart = pl.multiple_of(i * Q_BLOCK, Q_BLOCK)
    lo = lax.broadcasted_iota(jnp.int32, (Q_BLOCK, LANES), 1) < HEAD_DIM
    gates = ga_ref[...]

    def gate_tile(c, r):
        return jnp.where(lo, gates[:, c * 8 + 2 * r:c * 8 + 2 * r + 1], gates[:, c * 8 + 2 * r + 1:c * 8 + 2 * r + 2])

    def softmax_pv(s_tiles, v1, fix_max=None):
        m = _row_max(s_tiles)
        if fix_max is not None:
            m = fix_max(m)
        e = [jnp.exp(t - m) for t in s_tiles]
        return e, m, _dot(_mx(jnp.concatenate(e, axis=1)), v1)

    off = pl.multiple_of(i * (Q_BLOCK // CMP_STRIDE), 8)
    k_cmp = _mx(jnp.concatenate([kcmp_ref[0, 0, 0:n_far, :], kcmp_ref[0, 0, pl.ds(off, CMP_NEAR), :]], axis=0))
    v_cmp = _with_ones(_mx(jnp.concatenate([vcmp_ref[0, 0, 0:n_far, :], vcmp_ref[0, 0, pl.ds(off, CMP_NEAR), :]],
                                           axis=0)))
    colf = lax.broadcasted_iota(jnp.int32, (1, n_far), 1)
    coln = lax.broadcasted_iota(jnp.int32, (1, CMP_NEAR), 1)
    col_ok = jnp.concatenate([(colf >= CMP_FRONT) & (colf < off), coln + off >= CMP_FRONT], axis=1)
    mask_c = jnp.where(col_ok, 0.0, NEG)
    no_key = lambda m: jnp.where(m > 0.5 * NEG, m, 0.0)
    p_cmp, o_c = [], []
    for rs in halves:
        tiles = _lane_tiles(_dot_nt(qall[rs, :], k_cmp) + mask_c)
        tiles[-1] = tiles[-1] + tnear_ref[rs, :]
        e, _, ov = softmax_pv(tiles, v_cmp, no_key)
        inv = 1.0 / jnp.maximum(ov[:, LANES:], 1e-30)
        o_c.append(ov[:, :LANES] * inv)
        p_cmp.append([t * inv for t in e])
    o_c = jnp.concatenate(o_c, axis=0)

    def far_start(j):
        return pl.multiple_of(Q_BLOCK + j * SEL_CHUNK, Q_BLOCK)

    def far_logits(j, slot, masked):
        kc = _mx(ks_ref[0, pl.ds(far_start(j), SEL_CHUNK), :])
        if masked:
            madd = _dot(mneg_far[...], _block_of_key(SEL_CHUNK, j * (SEL_CHUNK // SEL_BLOCK)))
        for rs in halves:
            s = _dot_nt(qall[rs, :], kc)
            s_buf[slot, rs, :] = s + jnp.concatenate([madd] * (GROUP // 2), axis=0) if masked else s

    far_logits(0, 0, False)

    blkcol = lax.broadcasted_iota(jnp.int32, (Q_BLOCK, LANES), 1)
    n_tiles = len(p_cmp[0])
    for g in range(N_GROUPS):
        imp = jnp.zeros((Q_BLOCK, LANES), F32)
        for t in range(n_tiles):
            pg = sum(p_cmp[r // 2][t][(2 * (r % 2) + g) * Q_BLOCK:(2 * (r % 2) + g + 1) * Q_BLOCK]
                     for r in range(GROUP))
            if t < n_tiles - 1:
                cm = _mx(cmat_ref[t * LANES:(t + 1) * LANES, :])
            else:
                cm = _mx(cmat_ref[pl.ds(off, CMP_NEAR), :])
            hi = _mx(pg)
            low = _mx(pg - hi.astype(F32))
            imp = imp + _dot(hi, cm) + _dot(low, cm)
        sel = _select_blocks_t(imp.T, i, n_top)
        neg = jnp.where(sel, 0.0, NEG).T
        mneg[g * Q_BLOCK:(g + 1) * Q_BLOCK, :] = neg.astype(mneg.dtype)
        mneg_far[g * Q_BLOCK:(g + 1) * Q_BLOCK, :] = jnp.where(blkcol < 2 * (i - 1), neg, NEG).astype(mneg.dtype)

    wpad = kw_ref.shape[1] - ks_ref.shape[1] + Q_BLOCK
    kwin = _mx(kw_ref[0, pl.ds(nstart, wpad + Q_BLOCK), :])
    vwin = _with_ones(_mx(vw_ref[0, pl.ds(nstart, wpad + Q_BLOCK), :]))
    colw = lax.broadcasted_iota(jnp.int32, (1, wpad + Q_BLOCK), 1)
    mask_w = jnp.where(colw + nstart >= wpad, 0.0, NEG)
    o_w = []
    for rs in halves:
        _, _, ov = softmax_pv(_lane_tiles(_dot_nt(qall[rs, :], kwin) + twin_ref[rs, :] + mask_w), vwin)
        o_w.append(ov[:, :LANES] / ov[:, LANES:])
    o_w = jnp.concatenate(o_w, axis=0)
    for r in range(GROUP):
        oa_acc[:, r * LANES:(r + 1) * LANES] = (gate_tile(0, r) * _pair_heads(o_c, r)
                                                + gate_tile(2, r) * _pair_heads(o_w, r))

    bpad = kb_ref.shape[1] - ks_ref.shape[1] + Q_BLOCK
    kwin = _mx(kb_ref[0, pl.ds(nstart, bpad + Q_BLOCK), :])
    vwin = _with_ones(_mx(vb_ref[0, pl.ds(nstart, bpad + Q_BLOCK), :]))
    colb = lax.broadcasted_iota(jnp.int32, (1, bpad + Q_BLOCK), 1)
    mask_b = jnp.where(colb + nstart >= bpad, 0.0, NEG)
    o_b = []
    for hh, rs in enumerate(halves):
        sink = jnp.concatenate([jnp.full((Q_BLOCK, LANES), sink_ref[(h % 2) * GROUP + h // 2], F32)
                                for h in range(hh * N_HEADS // 2, (hh + 1) * N_HEADS // 2)], axis=0)
        _, m, ov = softmax_pv(_lane_tiles(_dot_nt(qball[rs, :], kwin) + tswa_ref[rs, :] + mask_b), vwin,
                              lambda m: jnp.maximum(m, sink))
        o_b.append(ov[:, :LANES] / (ov[:, LANES:] + jnp.exp(sink - m)))
    o_b = jnp.concatenate(o_b, axis=0)
    for r in range(GROUP):
        ob_ref[:, r * LANES:(r + 1) * LANES] = _pair_heads(o_b, r).astype(ob_ref.dtype)

    m_s[...] = jnp.full(m_s.shape, NEG, F32)
    acc_s[...] = jnp.zeros(acc_s.shape, F32)

    def flash_update(rs, s, v1):
        s_tiles = _lane_tiles(s)
        m_old = m_s[rs, :]
        m_new = jnp.maximum(m_old, _row_max(s_tiles))
        alpha = jnp.exp(m_old - m_new)
        p = jnp.concatenate([jnp.exp(t - m_new) for t in s_tiles], axis=1)
        acc_s[rs, :] = jnp.concatenate([alpha, alpha], axis=1) * acc_s[rs, :] + _dot(_mx(p), v1)
        m_s[rs, :] = m_new

    n_far_keys = jnp.maximum(i - 1, 0) * Q_BLOCK
    n_chunks = (n_far_keys + SEL_CHUNK - 1) // SEL_CHUNK

    madd = _dot(mneg_far[...], _block_of_key(SEL_CHUNK, 0))
    for rs in halves:
        s_buf[0, rs, :] = s_buf[0, rs, :] + jnp.concatenate([madd] * (GROUP // 2), axis=0)

    def far_update(j):
        v1 = _with_ones(_mx(vs_ref[0, pl.ds(far_start(j), SEL_CHUNK), :]))
        for rs in halves:
            flash_update(rs, s_buf[j % 2, rs, :], v1)

    def far_body(j, carry):
        far_update(j)
        far_logits(j + 1, (j + 1) % 2, True)
        return carry

    last = jnp.maximum(n_chunks - 1, 0)
    lax.fori_loop(0, last, far_body, 0)
    far_update(last)
    kc = _mx(ks_ref[0, pl.ds(nstart, 2 * Q_BLOCK), :])
    v1 = _with_ones(_mx(vs_ref[0, pl.ds(nstart, 2 * Q_BLOCK), :]))
    madd = _dot(mneg[...], _block_of_key(2 * Q_BLOCK, 2 * (i - 1)))
    col2 = lax.broadcasted_iota(jnp.int32, (1, 2 * Q_BLOCK), 1)
    mask_n = jnp.where((col2 < Q_BLOCK) & (i == 0), NEG, 0.0)
    for rs in halves:
        s = _dot_nt(qall[rs, :], kc) + jnp.concatenate([madd] * (GROUP // 2), axis=0) + tsel_ref[rs, :] + mask_n
        flash_update(rs, s, v1)
    acc = acc_s[...]
    o_s = acc[:, :LANES] / acc[:, LANES:]
    for r in range(GROUP):
        tile = oa_acc[:, r * LANES:(r + 1) * LANES] + gate_tile(1, r) * _pair_heads(o_s, r)
        oa_ref[:, r * LANES:(r + 1) * LANES] = tile.astype(oa_ref.dtype)


def _rel_bucket_np(dist):
    n = np.maximum(dist, 0)
    max_exact = REL_BUCKETS // 2
    nf = np.maximum(n, 1).astype(np.float32)
    log_b = max_exact + (np.log(nf / max_exact) / math.log(REL_MAX_DIST / max_exact)
                         * (REL_BUCKETS - max_exact)).astype(np.int32)
    log_b = np.minimum(log_b, REL_BUCKETS - 1)
    return np.where(n < max_exact, n, log_b)


def _toeplitz_bias(tab, pad, width, window, shift_far):
    length = width + Q_BLOCK
    dist = pad + Q_BLOCK - 1 - np.arange(length)
    onehot = np.zeros((length, REL_BUCKETS), np.float32)
    onehot[np.arange(length), _rel_bucket_np(dist)] = 1.0
    vals = jnp.dot(jnp.asarray(onehot), tab, precision=lax.Precision.HIGHEST)
    if shift_far:
        vals = vals - tab[REL_BUCKETS - 1][None, :]
    valid = (dist >= 0) & (dist < window)
    vals = jnp.where(jnp.asarray(valid)[:, None], vals, NEG).T
    skew = jnp.tile(vals, (1, Q_BLOCK))[:, :Q_BLOCK * (length - 1)].reshape(N_HEADS, Q_BLOCK, length - 1)
    return skew[:, :, Q_BLOCK - 1:Q_BLOCK - 1 + width].reshape(N_HEADS * Q_BLOCK, width).astype(F32)


def _attention(proj, kvcmp, sinks, bias_table, bsz, seq):
    assert seq % SEL_CHUNK == 0
    nq = seq // Q_BLOCK
    n_far = seq // CMP_STRIDE
    n_sel = seq // SEL_BLOCK
    n_top = min(SEL_TOP_N, n_sel)
    assert n_top >= SEL_INIT_BLOCKS + SEL_LOCAL_BLOCKS and n_sel <= LANES
    wpad = Q_BLOCK * (-(-(NSA_WINDOW - 1) // Q_BLOCK))
    bpad = Q_BLOCK * (-(-(SWA_WINDOW - 1) // Q_BLOCK))
    pair = lambda tab: tab.astype(F32).reshape(REL_BUCKETS, N_GROUPS, GROUP).transpose(0, 2, 1).reshape(REL_BUCKETS, -1)
    tab_a = pair(bias_table[:, :N_HEADS])
    tab_b = pair(bias_table[:, N_HEADS:])
    near_pad = CMP_STRIDE * CMP_FRONT - (CMP_BLOCK - 1)
    t_near = _toeplitz_bias(tab_a, near_pad, CMP_STRIDE * CMP_NEAR, 1 << 30, True)[:, ::CMP_STRIDE]
    t_sel = _toeplitz_bias(tab_a, Q_BLOCK, 2 * Q_BLOCK, 1 << 30, True)
    t_win = _toeplitz_bias(tab_a, wpad, wpad + Q_BLOCK, NSA_WINDOW, False)
    t_swa = _toeplitz_bias(tab_b, bpad, bpad + Q_BLOCK, SWA_WINDOW, False)
    n_rows = kvcmp.shape[2]
    cn = (np.arange(n_rows) - CMP_FRONT)[:, None] * CMP_STRIDE
    sj = np.arange(LANES)[None, :] * SEL_BLOCK
    cmat = ((cn < sj + SEL_BLOCK) & (cn + CMP_BLOCK > sj) & (cn >= 0) & (cn + CMP_BLOCK <= seq)
            & (sj < seq)).astype(np.float32)
    cmat = jnp.asarray(cmat, F32)
    padded = lambda name, p: jnp.pad(proj[name].reshape(bsz, seq, LANES), ((0, 0), (p, 0), (0, 0)))
    ks, vs = padded('ks', Q_BLOCK), padded('vs', Q_BLOCK)
    kw, vw = padded('kw', wpad), padded('vw', wpad)
    kb, vb = padded('kb', bpad), padded('vb', bpad)
    rows = N_HEADS * Q_BLOCK
    qspec = pl.BlockSpec((Q_BLOCK, 4 * LANES), lambda b, i: (b * nq + i, 0))
    const2 = lambda shape: pl.BlockSpec(shape, lambda b, i: (0, 0))
    batch3 = lambda n: pl.BlockSpec((1, n, LANES), lambda b, i: (b, 0, 0))
    kernel = functools.partial(_attn_kernel, n_far=n_far, n_top=n_top)
    return pl.pallas_call(
        kernel,
        grid=(bsz, nq),
        in_specs=[pl.BlockSpec(memory_space=pltpu.SMEM),
                  qspec, qspec,
                  pl.BlockSpec((Q_BLOCK, LANES), lambda b, i: (b * nq + i, 0)),
                  pl.BlockSpec((1, 1, n_rows, LANES), lambda b, i: (0, b, 0, 0)),
                  pl.BlockSpec((1, 1, n_rows, LANES), lambda b, i: (1, b, 0, 0)),
                  batch3(seq + Q_BLOCK), batch3(seq + Q_BLOCK),
                  batch3(seq + wpad), batch3(seq + wpad),
                  batch3(seq + bpad), batch3(seq + bpad),
                  const2((n_rows, LANES)),
                  const2((rows, CMP_NEAR)),
                  const2((rows, 2 * Q_BLOCK)),
                  const2((rows, wpad + Q_BLOCK)),
                  const2((rows, bpad + Q_BLOCK))],
        out_specs=[qspec, qspec],
        out_shape=[jax.ShapeDtypeStruct((bsz * seq, 4 * LANES), BF16)] * 2,
        scratch_shapes=[pltpu.VMEM((rows, LANES), MXU_DTYPE),
                        pltpu.VMEM((rows, LANES), MXU_DTYPE),
                        pltpu.VMEM((N_GROUPS * Q_BLOCK, LANES), MXU_DTYPE),
                        pltpu.VMEM((N_GROUPS * Q_BLOCK, LANES), MXU_DTYPE),
                        pltpu.VMEM((rows, LANES), F32),
                        pltpu.VMEM((rows, 2 * LANES), F32),
                        pltpu.VMEM((2, rows, SEL_CHUNK), F32),
                        pltpu.VMEM((Q_BLOCK, 4 * LANES), F32)],
        compiler_params=pltpu.CompilerParams(dimension_semantics=("arbitrary", "arbitrary"),
                                             vmem_limit_bytes=VMEM_LIMIT),
        name="attention",
    )(sinks.astype(F32), proj['qa'], proj['qb'], proj['ga'], kvcmp, kvcmp, ks, vs, kw, vw, kb, vb,
      cmat, t_near, t_sel, t_win, t_swa)


def _layer_norm(y, g, b):
    mu = jnp.mean(y, axis=-1, keepdims=True)
    yc = y - mu
    var = jnp.mean(yc * yc, axis=-1, keepdims=True)
    return yc * lax.rsqrt(var + LN_EPS) * g + b


def _outproj_kernel(oa_ref, ob_ref, sg_ref, x_ref, pa_ref, pb_ref, wo_ref, g1_ref, b1_ref, wr_ref, rb_ref, sgu_ref,
                    sd_ref, tri_ref, h_ref, base_ref, eidx_ref, gate_ref, rank_ref, cnt_ref, carry):
    step = pl.program_id(0)
    tm = oa_ref.shape[0]

    @pl.when(step == 0)
    def _():
        carry[...] = jnp.zeros(carry.shape, F32)

    sg = sg_ref[...].astype(F32)
    merged = (sg[:, :D_MODEL] * _dot(_mx(oa_ref[...]), pa_ref[...])
              + sg[:, D_MODEL:] * _dot(_mx(ob_ref[...]), pb_ref[...]))
    mix = _dot(_mx(merged), wo_ref[...])
    h = _layer_norm(DN_ALPHA * x_ref[...] + mix, g1_ref[...], b1_ref[...])
    hb = _mx(h)
    h_ref[...] = _pack_bf16_pairs(h)

    gu = _dot(hb, sgu_ref[...])
    shared = _dot(_mx(jax.nn.silu(gu[:, :SHARED_HIDDEN]) * gu[:, SHARED_HIDDEN:]), sd_ref[...])
    base_ref[...] = DN_ALPHA * h + shared

    scores = jax.nn.sigmoid(_dot_nt(wr_ref[...], hb))
    choice = scores + rb_ref[:, 0:1]
    per_group = N_EXPERTS // N_EXPERT_GROUPS
    gs = []
    for g in range(N_EXPERT_GROUPS):
        cg = choice[g * per_group:(g + 1) * per_group]
        m1 = jnp.max(cg, axis=0, keepdims=True)
        is_m = cg == m1
        n_m = jnp.sum(is_m.astype(F32), axis=0, keepdims=True)
        m2 = jnp.max(jnp.where(is_m, -jnp.inf, cg), axis=0, keepdims=True)
        gs.append(m1 + jnp.where(n_m > 1.5, m1, m2))
    gs = jnp.concatenate(gs, axis=0)
    gid = lax.broadcasted_iota(jnp.int32, gs.shape, 0)
    beaten = jnp.zeros(gs.shape, jnp.int32)
    for g in range(N_EXPERT_GROUPS):
        other = gs[g:g + 1]
        beaten = beaten + ((other > gs) | ((other == gs) & (g < gid))).astype(jnp.int32)
    keep_g = beaten < TOPK_EXPERT_GROUPS
    keep = jnp.concatenate([jnp.broadcast_to(keep_g[g:g + 1], (per_group, tm)) for g in range(N_EXPERT_GROUPS)],
                           axis=0)
    cand = jnp.where(keep, choice, -jnp.inf)
    eid = lax.broadcasted_iota(jnp.int32, cand.shape, 0)
    hits = []
    e_rows = []
    w_rows = []
    for _ in range(TOP_K):
        m = jnp.max(cand, axis=0, keepdims=True)
        idx = jnp.min(jnp.where(cand == m, eid, N_EXPERTS), axis=0, keepdims=True)
        hit = eid == idx
        hits.append(hit)
        e_rows.append(idx)
        w_rows.append(jnp.sum(jnp.where(hit, scores, 0.0), axis=0, keepdims=True))
        cand = jnp.where(hit, -jnp.inf, cand)
    w = jnp.concatenate(w_rows, axis=0)
    gate_ref[...] = w / jnp.sum(w, axis=0, keepdims=True) * ROUTED_SCALE
    eidx_ref[...] = jnp.concatenate(e_rows, axis=0)

    onehot = jnp.zeros(cand.shape, F32)
    for hit in hits:
        onehot = onehot + hit.astype(F32)
    before = _dot(onehot.astype(BF16), tri_ref[...]) + carry[:, 0:1]
    rank_ref[...] = jnp.concatenate(
        [jnp.sum(jnp.where(hit, before, 0.0), axis=0, keepdims=True) for hit in hits], axis=0).astype(jnp.int32)
    carry[...] = carry[...] + jnp.sum(onehot, axis=1, keepdims=True)
    cnt_ref[...] = carry[...]


def _out_projection(oa, ob, sg, x2, proj_a, proj_b, w_out, ln_g, ln_b, w_router, router_bias, s_gate, s_up, s_down):
    t = x2.shape[0]
    tm = OUT_TM
    pair_rows = lambda p: p.reshape(N_GROUPS, GROUP, HEAD_DIM, -1).transpose(1, 0, 2, 3).reshape(p.shape)
    pa = pair_rows(proj_a).astype(MXU_DTYPE)
    pb = pair_rows(proj_b).astype(MXU_DTYPE)
    tri = jnp.asarray(np.triu(np.ones((tm, tm), np.float32), 1), BF16)
    row = lambda i: (i, 0)
    fixed = lambda i: (0, 0)
    col = lambda i: (0, i)
    outs = pl.pallas_call(
        _outproj_kernel,
        grid=(t // tm,),
        in_specs=[pl.BlockSpec((tm, 4 * LANES), row), pl.BlockSpec((tm, 4 * LANES), row),
                  pl.BlockSpec((tm, 2 * D_MODEL), row), pl.BlockSpec((tm, D_MODEL), row),
                  pl.BlockSpec((4 * LANES, D_MODEL), fixed), pl.BlockSpec((4 * LANES, D_MODEL), fixed),
                  pl.BlockSpec((D_MODEL, D_MODEL), fixed),
                  pl.BlockSpec((1, D_MODEL), fixed), pl.BlockSpec((1, D_MODEL), fixed),
                  pl.BlockSpec((N_EXPERTS, D_MODEL), fixed), pl.BlockSpec((N_EXPERTS, LANES), fixed),
                  pl.BlockSpec((D_MODEL, 2 * SHARED_HIDDEN), fixed), pl.BlockSpec((SHARED_HIDDEN, D_MODEL), fixed),
                  pl.BlockSpec((tm, tm), fixed)],
        out_specs=[pl.BlockSpec((tm, D_MODEL // 2), row), pl.BlockSpec((tm, D_MODEL), row),
                   pl.BlockSpec((TOP_K, tm), col), pl.BlockSpec((TOP_K, tm), col), pl.BlockSpec((TOP_K, tm), col),
                   pl.BlockSpec((N_EXPERTS, LANES), fixed)],
        out_shape=[jax.ShapeDtypeStruct((t, D_MODEL // 2), jnp.uint32), jax.ShapeDtypeStruct((t, D_MODEL), F32),
                   jax.ShapeDtypeStruct((TOP_K, t), jnp.int32), jax.ShapeDtypeStruct((TOP_K, t), F32),
                   jax.ShapeDtypeStruct((TOP_K, t), jnp.int32), jax.ShapeDtypeStruct((N_EXPERTS, LANES), F32)],
        scratch_shapes=[pltpu.VMEM((N_EXPERTS, LANES), F32)],
        compiler_params=pltpu.CompilerParams(dimension_semantics=("arbitrary",), vmem_limit_bytes=VMEM_LIMIT),
        name="out_projection_router",
    )(oa, ob, sg, x2, pa, pb, w_out.astype(MXU_DTYPE), ln_g.reshape(1, -1), ln_b.reshape(1, -1),
      w_router.T.astype(MXU_DTYPE), jnp.broadcast_to(router_bias.astype(F32)[:, None], (N_EXPERTS, LANES)),
      jnp.concatenate([s_gate, s_up], axis=1).astype(MXU_DTYPE), s_down.astype(MXU_DTYPE), tri)
    return outs


def _rows_to_tiles(x):
    return pltpu.einshape("cml->mcl", jnp.stack(_lane_tiles(x), axis=0))


def _tiles_to_rows(x3):
    xt = pltpu.einshape("mcl->cml", x3)
    return jnp.concatenate([xt[c] for c in range(xt.shape[0])], axis=1)


def _dispatch_kernel(zstart_ref, cnt_ref, dest_ref, h2_ref, xs_ref, h_ref, zeros, sem, zsem):
    step = pl.program_id(0)
    tm = h2_ref.shape[0]
    slot = step % 2
    h_ref[slot] = _rows_to_tiles(h2_ref[...])

    @pl.when(step == 0)
    def _():
        zeros[...] = jnp.zeros(zeros.shape, zeros.dtype)

        def fill(e, c):
            @pl.when(cnt_ref[e] > 0)
            def _():
                cp = pltpu.make_async_copy(zeros, xs_ref.at[pl.ds(zstart_ref[e], MOE_BM)], zsem)
                cp.start()
                cp.wait()
            return c
        lax.fori_loop(0, N_EXPERTS, fill, 0)

    def issue(t, c):
        for k in range(TOP_K):
            pltpu.make_async_copy(h_ref.at[slot, t], xs_ref.at[dest_ref[k, t]], sem.at[slot]).start(priority=k % 2)
        return c
    lax.fori_loop(0, tm, issue, 0)

    def wait_tile(s):
        for k in range(TOP_K):
            pltpu.make_async_copy(h_ref.at[s], xs_ref.at[pl.ds(0, tm)], sem.at[s]).wait()

    @pl.when(step > 0)
    def _():
        wait_tile(1 - slot)

    @pl.when(step + 1 == pl.num_programs(0))
    def _():
        wait_tile(slot)


def _dispatch(h, dest, zstart, counts, n_rows):
    t = h.shape[0]
    tm = DISP_TM
    return pl.pallas_call(
        _dispatch_kernel,
        grid_spec=pltpu.PrefetchScalarGridSpec(
            num_scalar_prefetch=2,
            grid=(t // tm,),
            in_specs=[pl.BlockSpec((TOP_K, tm), lambda i, *_: (0, i), memory_space=pltpu.SMEM),
                      pl.BlockSpec((tm, D_MODEL // 2), lambda i, *_: (i, 0))],
            out_specs=pl.BlockSpec(memory_space=pl.ANY),
            scratch_shapes=[pltpu.VMEM((2, tm) + PACKED_ROW_TILE, jnp.uint32),
                            pltpu.VMEM((MOE_BM,) + PACKED_ROW_TILE, jnp.uint32),
                            pltpu.SemaphoreType.DMA((2,)), pltpu.SemaphoreType.DMA(())]),
        out_shape=jax.ShapeDtypeStruct((n_rows,) + PACKED_ROW_TILE, jnp.uint32),
        compiler_params=pltpu.CompilerParams(dimension_semantics=("arbitrary",), vmem_limit_bytes=VMEM_LIMIT),
        name="moe_dispatch",
    )(zstart, counts, dest, h)


def _experts_kernel(blk_e_ref, nused_ref, xs_ref, wg_ref, wu_ref, wd_ref, ys_ref, wg_s, wu_s, wd_s):
    b = pl.program_id(0)
    prev = blk_e_ref[jnp.maximum(b - 1, 0)]

    @pl.when((b == 0) | (blk_e_ref[b] != prev))
    def _():
        wg_s[...] = _mx(wg_ref[0])
        wu_s[...] = _mx(wu_ref[0])
        wd_s[...] = _mx(wd_ref[0])

    @pl.when(b < nused_ref[0])
    def _():
        xb = _mx(jnp.concatenate(_unpack_bf16_pairs(_tiles_to_rows(xs_ref[...])), axis=1))
        hid = jax.nn.silu(_dot(xb, wg_s[...])) * _dot(xb, wu_s[...])
        ys_ref[...] = _rows_to_tiles(_pack_bf16_pairs(_dot(_mx(hid), wd_s[...])))

    @pl.when(b >= nused_ref[0])
    def _():
        ys_ref[...] = jnp.zeros(ys_ref.shape, ys_ref.dtype)


def _experts(xs, blk_e, nused, e_gate, e_up, e_down):
    n_rows = xs.shape[0]
    n_blocks = n_rows // MOE_BM
    xmap = lambda b, be, nu: (jnp.minimum(b, nu[0] - 1), 0, 0)
    wmap = lambda b, be, nu: (be[b], 0, 0)
    return pl.pallas_call(
        _experts_kernel,
        grid_spec=pltpu.PrefetchScalarGridSpec(
            num_scalar_prefetch=2,
            grid=(n_blocks,),
            in_specs=[pl.BlockSpec((MOE_BM,) + PACKED_ROW_TILE, xmap),
                      pl.BlockSpec((1, D_MODEL, EXPERT_HIDDEN), wmap),
                      pl.BlockSpec((1, D_MODEL, EXPERT_HIDDEN), wmap),
                      pl.BlockSpec((1, EXPERT_HIDDEN, D_MODEL), wmap)],
            out_specs=pl.BlockSpec((MOE_BM,) + PACKED_ROW_TILE, lambda b, be, nu: (b, 0, 0)),
            scratch_shapes=[pltpu.VMEM((D_MODEL, EXPERT_HIDDEN), MXU_DTYPE),
                            pltpu.VMEM((D_MODEL, EXPERT_HIDDEN), MXU_DTYPE),
                            pltpu.VMEM((EXPERT_HIDDEN, D_MODEL), MXU_DTYPE)]),
        out_shape=jax.ShapeDtypeStruct((n_rows,) + PACKED_ROW_TILE, jnp.uint32),
        compiler_params=pltpu.CompilerParams(dimension_semantics=("arbitrary",), vmem_limit_bytes=VMEM_LIMIT),
        name="moe_experts",
    )(blk_e, nused, xs, e_gate, e_up, e_down)


def _combine_kernel(dest_ref, dest_next_ref, gate_ref, base_ref, g2_ref, b2_ref, ys_ref, out_ref, buf, routed, sem):
    step = pl.program_id(0)
    tm = base_ref.shape[0]
    slot = step % 2

    def gather_rows(d_ref, s, t):
        for k in range(TOP_K):
            j = t * TOP_K + k
            pltpu.make_async_copy(ys_ref.at[d_ref[j]], buf.at[s, j], sem.at[s]).start(priority=k % 2)

    def combine_token(t):
        low = jnp.zeros(PACKED_ROW_TILE, F32)
        high = jnp.zeros(PACKED_ROW_TILE, F32)
        for k in range(TOP_K):
            j = t * TOP_K + k
            lo_k, hi_k = _unpack_bf16_pairs(buf[slot, j])
            low = low + gate_ref[j] * lo_k
            high = high + gate_ref[j] * hi_k
        routed[t] = jnp.concatenate([low, high], axis=0)

    def for_tokens(body):
        def step_fn(t, c):
            body(t)
            return c
        lax.fori_loop(0, tm, step_fn, 0)

    @pl.when(step == 0)
    def _():
        for_tokens(lambda t: gather_rows(dest_ref, 0, t))

    pltpu.make_async_copy(ys_ref.at[pl.ds(0, tm * TOP_K)], buf.at[slot], sem.at[slot]).wait()

    @pl.when(step + 1 < pl.num_programs(0))
    def _():
        def both(t):
            gather_rows(dest_next_ref, 1 - slot, t)
            combine_token(t)
        for_tokens(both)

    @pl.when(step + 1 == pl.num_programs(0))
    def _():
        for_tokens(combine_token)

    out_ref[...] = _layer_norm(base_ref[...] + _tiles_to_rows(routed[...]), g2_ref[...], b2_ref[...])


def _combine(ys3, dest, gate, base, ln_g, ln_b):
    t = base.shape[0]
    tm = COMB_TM
    n_tiles = t // tm
    dest_tk = dest.T.reshape(-1)
    return pl.pallas_call(
        _combine_kernel,
        grid=(n_tiles,),
        in_specs=[pl.BlockSpec((tm * TOP_K,), lambda i: (i,), memory_space=pltpu.SMEM),
                  pl.BlockSpec((tm * TOP_K,), lambda i: (jnp.minimum(i + 1, n_tiles - 1),), memory_space=pltpu.SMEM),
                  pl.BlockSpec((tm * TOP_K,), lambda i: (i,), memory_space=pltpu.SMEM),
                  pl.BlockSpec((tm, D_MODEL), lambda i: (i, 0)),
                  pl.BlockSpec((1, D_MODEL), lambda i: (0, 0)),
                  pl.BlockSpec((1, D_MODEL), lambda i: (0, 0)),
                  pl.BlockSpec(memory_space=pl.ANY)],
        out_specs=pl.BlockSpec((tm, D_MODEL), lambda i: (i, 0)),
        out_shape=jax.ShapeDtypeStruct((t, D_MODEL), F32),
        scratch_shapes=[pltpu.VMEM((2, tm * TOP_K) + PACKED_ROW_TILE, jnp.uint32), pltpu.VMEM((tm,) + ROW_TILE, F32),
                        pltpu.SemaphoreType.DMA((2,))],
        compiler_params=pltpu.CompilerParams(dimension_semantics=("arbitrary",), vmem_limit_bytes=VMEM_LIMIT),
        name="moe_combine",
    )(dest_tk, dest_tk, gate.T.reshape(-1), base, ln_g.reshape(1, -1), ln_b.reshape(1, -1), ys3)


def _moe_layout(eidx, rank, counts):
    n_assign = eidx.size
    n_blocks = (n_assign + N_EXPERTS * (MOE_BM - 1)) // MOE_BM
    padded = (counts + MOE_BM - 1) // MOE_BM * MOE_BM
    pends = jnp.cumsum(padded)
    pstarts = pends - padded
    experts = jnp.arange(N_EXPERTS, dtype=jnp.int32)
    dest = jnp.sum(jnp.where(eidx[..., None] == experts, pstarts, 0), axis=-1) + rank
    block_row = jnp.arange(n_blocks, dtype=jnp.int32) * MOE_BM
    blk_e = jnp.minimum(jnp.sum(pends[None, :] <= block_row[:, None], axis=1), N_EXPERTS - 1).astype(jnp.int32)
    nused = (pends[-1:] // MOE_BM).astype(jnp.int32)
    zstart = jnp.maximum(pends - MOE_BM, 0).astype(jnp.int32)
    return dest.astype(jnp.int32), blk_e, nused, zstart, n_blocks * MOE_BM


def _layer(x, w_in, cmp_pe, cmp_w1, cmp_b1, cmp_w2, sinks, bias_table, proj_a, proj_b, w_out, ln1_g, ln1_b,
           w_router, router_bias, e_gate, e_up, e_down, s_gate, s_up, s_down, ln2_g, ln2_b):
    bsz, seq, d = x.shape
    x2 = x.reshape(bsz * seq, d)
    proj = _in_projection(x2, w_in)
    kvcmp = _compress(proj['kc'], proj['vc'], bsz, seq, cmp_pe, cmp_w1, cmp_b1, cmp_w2)
    oa, ob = _attention(proj, kvcmp, sinks, bias_table, bsz, seq)
    h, base, eidx, gate, rank, cnt = _out_projection(oa, ob, proj['sg'], x2, proj_a, proj_b, w_out, ln1_g, ln1_b,
                                                     w_router, router_bias, s_gate, s_up, s_down)
    counts = cnt[:, 0].astype(jnp.int32)
    dest, blk_e, nused, zstart, n_rows = _moe_layout(eidx, rank, counts)
    xs = _dispatch(h, dest, zstart, counts, n_rows)
    ys = _experts(xs, blk_e, nused, e_gate, e_up, e_down)
    out = _combine(ys, dest, gate, base, ln2_g, ln2_b)
    return out.reshape(bsz, seq, d)


def kernel(x, w_in, cmp_pe, cmp_w1, cmp_b1, cmp_w2, attn_sinks, rel_bias_table, proj_a, proj_b, w_out, ln1_g, ln1_b,
           w_router, router_bias, expert_w_gate, expert_w_up, expert_w_down, shared_w_gate, shared_w_up,
           shared_w_down, ln2_g, ln2_b):
    h = x
    for l in range(DEPTH):
        h = _layer(h, w_in[l], cmp_pe[l], cmp_w1[l], cmp_b1[l], cmp_w2[l], attn_sinks[l], rel_bias_table, proj_a[l],
                   proj_b[l], w_out[l], ln1_g[l], ln1_b[l], w_router[l], router_bias[l], expert_w_gate[l],
                   expert_w_up[l], expert_w_down[l], shared_w_gate[l], shared_w_up[l], shared_w_down[l], ln2_g[l],
                   ln2_b[l])
    return h
```

```python
import functools
import math

import numpy as np
import jax
import jax.numpy as jnp
from jax import lax
from jax.experimental import pallas as pl
from jax.experimental.pallas import tpu as pltpu

F32 = jnp.float32
BF16 = jnp.bfloat16
MXU_DTYPE = jnp.bfloat16

D_MODEL = 1024
HEAD_DIM = 64
ATTN_SCALE = HEAD_DIM ** -0.5
LOG2E = math.log2(math.e)
Q_BLOCK = 128
N_HEADS = 8
N_GROUPS = 2
GROUP = 4
CMP_BLOCK = 32
CMP_STRIDE = 16
CMP_HIDDEN = 128
SEL_BLOCK = 64
SEL_TOP_N = 8
SEL_INIT_BLOCKS = 1
SEL_LOCAL_BLOCKS = 2
NSA_WINDOW = 512
SWA_WINDOW = 128
REL_BUCKETS = 32
REL_MAX_DIST = 128
N_EXPERTS = 256
TOP_K = 8
EXPERT_HIDDEN = 256
SHARED_HIDDEN = 256
N_EXPERT_GROUPS = 8
TOPK_EXPERT_GROUPS = 4
ROUTED_SCALE = 2.5
LN_EPS = 1e-5
DEPTH = 1
DN_ALPHA = (2 * DEPTH) ** 0.25

NEG = -1e30
LANES = 128
ROW_TILE = (8, LANES)
PACKED_ROW_TILE = (4, LANES)
CMP_FRONT = 16
CMP_NEAR = LANES
SEL_CHUNK = 1024
VMEM_LIMIT = 56 * 1024 * 1024

IN_TM = 512
OUT_TM = 512
MOE_BM = 512
DISP_TM = 256
COMB_TM = 256


def _dot(a, b):
    return jnp.dot(a, b, preferred_element_type=F32)


def _dot_nt(a, b):
    return lax.dot_general(a, b, (((1,), (1,)), ((), ())), preferred_element_type=F32)


def _mx(a):
    return a.astype(MXU_DTYPE)


def _pack_bf16_pairs(x):
    half = x.shape[1] // 2
    bits = lax.bitcast_convert_type(x.astype(BF16).astype(F32), jnp.uint32)
    return (bits[:, half:] & jnp.uint32(0xFFFF0000)) | (bits[:, :half] >> 16)


def _unpack_bf16_pairs(words):
    return (lax.bitcast_convert_type(words << 16, F32),
            lax.bitcast_convert_type(words & jnp.uint32(0xFFFF0000), F32))


_IN_COLS = (('qa', 512), ('qb', 512), ('kc', 128), ('vc', 128), ('ks', 128), ('vs', 128), ('kw', 128),
            ('vw', 128), ('kb', 128), ('vb', 128), ('ga', 128), ('sg', 2048))


def _inproj_kernel(x_ref, w_ref, qa_ref, qb_ref, kc_ref, vc_ref, ks_ref, vs_ref, kw_ref, vw_ref, kb_ref, vb_ref,
                   ga_ref, sg_ref):
    xb = _mx(x_ref[...])
    outs = dict(qa=qa_ref, qb=qb_ref, kc=kc_ref, vc=vc_ref, ks=ks_ref, vs=vs_ref, kw=kw_ref, vw=vw_ref,
                kb=kb_ref, vb=vb_ref, ga=ga_ref, sg=sg_ref)
    tiles = [(name, c) for name, width in _IN_COLS for c in range(0, width, LANES)]
    chunk = 4
    for t0 in range(0, len(tiles), chunk):
        group = tiles[t0:t0 + chunk]
        y = _dot(xb, w_ref[:, t0 * LANES:(t0 + len(group)) * LANES])
        for j, (name, c) in enumerate(group):
            yj = y[:, j * LANES:(j + 1) * LANES]
            if name in ('ga', 'sg'):
                yj = jax.nn.sigmoid(yj)
            outs[name][:, c:c + LANES] = yj.astype(outs[name].dtype)


def _pair_head_columns(w):
    return w.reshape(w.shape[0], N_GROUPS, GROUP, HEAD_DIM).transpose(0, 2, 1, 3).reshape(w.shape[0], -1)


def _in_projection(x2, w_in):
    t = x2.shape[0]
    sizes = (512, 128, 128, 128, 128, 128, 128, 24, 512, 128, 128, 1024, 1024)
    offs = np.cumsum((0,) + sizes)
    part = [w_in[:, offs[k]:offs[k + 1]] for k in range(len(sizes))]
    w_qa, w_kc, w_vc, w_ks, w_vs, w_kw, w_vw, w_g, w_qb, w_kb, w_vb, w_gate_a, w_gate_b = part
    w_qa = _pair_head_columns(w_qa) * (ATTN_SCALE * LOG2E)
    w_qb = _pair_head_columns(w_qb) * (ATTN_SCALE * LOG2E)
    w_ga = w_g.reshape(-1, N_GROUPS, GROUP, 3).transpose(0, 3, 2, 1).reshape(-1, 24)
    w_ga = jnp.pad(w_ga, ((0, 0), (0, LANES - 24)))
    w_all = jnp.concatenate([w_qa, w_qb, w_kc, w_vc, w_ks, w_vs, w_kw, w_vw, w_kb, w_vb, w_ga, w_gate_a, w_gate_b],
                            axis=1).astype(MXU_DTYPE)
    n_all = w_all.shape[1]
    out_shape = []
    out_specs = []
    for name, width in _IN_COLS:
        dt = F32 if name == 'ga' else BF16
        out_shape.append(jax.ShapeDtypeStruct((t, width), dt))
        out_specs.append(pl.BlockSpec((IN_TM, width), lambda i: (i, 0)))
    outs = pl.pallas_call(
        _inproj_kernel,
        grid=(t // IN_TM,),
        in_specs=[pl.BlockSpec((IN_TM, D_MODEL), lambda i: (i, 0)),
                  pl.BlockSpec((D_MODEL, n_all), lambda i: (0, 0))],
        out_specs=out_specs,
        out_shape=out_shape,
        compiler_params=pltpu.CompilerParams(dimension_semantics=("arbitrary",), vmem_limit_bytes=VMEM_LIMIT),
        name="in_projection",
    )(x2, w_all)
    return dict(zip([n for n, _ in _IN_COLS], outs))


def _compress_kernel(tok_ref, w1_ref, pe_ref, w1o_ref, b1_ref, w2_ref, out_ref):
    n_chunks = tok_ref.shape[2]
    ab = _dot(tok_ref[0, 0], w1_ref[0])
    a = ab[:, :2 * CMP_HIDDEN]
    b_next = pltpu.roll(ab[:, 2 * CMP_HIDDEN:], n_chunks - 1, 0)
    cb = _dot(_mx(pe_ref[0]), _mx(w1o_ref[0]))[0:1, :] + b1_ref[0]
    cb2 = jnp.concatenate([cb, cb], axis=1)
    hid = jax.nn.gelu(a + b_next + cb2)
    out = _dot(_mx(hid), w2_ref[0])
    row = lax.broadcasted_iota(jnp.int32, out.shape, 0)
    out = jnp.where(row < n_chunks - 1, out, 0.0)
    out_ref[0, 0, 0:CMP_FRONT, :] = jnp.zeros((CMP_FRONT, LANES), F32)
    out_ref[0, 0, CMP_FRONT:CMP_FRONT + n_chunks, :] = out
    out_ref[0, 0, CMP_FRONT + n_chunks:, :] = jnp.zeros((CMP_NEAR - CMP_FRONT, LANES), F32)


def _compress(kc, vc, bsz, seq, cmp_pe, cmp_w1, cmp_b1, cmp_w2):
    n_chunks = seq // CMP_STRIDE
    tok = jnp.stack([kc, vc]).reshape(2, bsz, n_chunks, CMP_STRIDE * LANES)
    eye = jnp.eye(N_GROUPS, dtype=F32)
    w1r = cmp_w1.reshape(2, 2, CMP_STRIDE, HEAD_DIM, CMP_HIDDEN)
    w1 = jnp.einsum('khjdn,gG->kjgdhGn', w1r, eye).reshape(2, CMP_STRIDE * LANES, 4 * CMP_HIDDEN).astype(MXU_DTYPE)
    w2 = jnp.einsum('knd,gG->kgnGd', cmp_w2, eye).reshape(2, 2 * CMP_HIDDEN, LANES).astype(MXU_DTYPE)
    pe = jnp.pad(cmp_pe.reshape(2, 1, CMP_BLOCK * HEAD_DIM), ((0, 0), (0, 7), (0, 0)))
    b1 = cmp_b1.reshape(2, 1, CMP_HIDDEN)
    rows = CMP_FRONT + n_chunks + CMP_NEAR - CMP_FRONT
    return pl.pallas_call(
        _compress_kernel,
        grid=(2, bsz),
        in_specs=[pl.BlockSpec((1, 1, n_chunks, CMP_STRIDE * LANES), lambda k, b: (k, b, 0, 0)),
                  pl.BlockSpec((1, CMP_STRIDE * LANES, 4 * CMP_HIDDEN), lambda k, b: (k, 0, 0)),
                  pl.BlockSpec((1, 8, CMP_BLOCK * HEAD_DIM), lambda k, b: (k, 0, 0)),
                  pl.BlockSpec((1, CMP_BLOCK * HEAD_DIM, CMP_HIDDEN), lambda k, b: (k, 0, 0)),
                  pl.BlockSpec((1, 1, CMP_HIDDEN), lambda k, b: (k, 0, 0)),
                  pl.BlockSpec((1, 2 * CMP_HIDDEN, LANES), lambda k, b: (k, 0, 0))],
        out_specs=pl.BlockSpec((1, 1, rows, LANES), lambda k, b: (k, b, 0, 0)),
        out_shape=jax.ShapeDtypeStruct((2, bsz, rows, LANES), F32),
        compiler_params=pltpu.CompilerParams(dimension_semantics=("arbitrary", "arbitrary"),
                                             vmem_limit_bytes=VMEM_LIMIT),
        name="nsa_compress",
    )(tok, w1, pe, cmp_w1, b1, w2)


def _stack_heads(q_ref, dst):
    lo = lax.broadcasted_iota(jnp.int32, (Q_BLOCK, LANES), 1) < HEAD_DIM
    for r in range(GROUP):
        qr = q_ref[:, r * LANES:(r + 1) * LANES].astype(dst.dtype)
        z = jnp.zeros_like(qr)
        dst[(2 * r) * Q_BLOCK:(2 * r + 1) * Q_BLOCK, :] = jnp.where(lo, qr, z)
        dst[(2 * r + 1) * Q_BLOCK:(2 * r + 2) * Q_BLOCK, :] = jnp.where(lo, z, qr)


def _pair_heads(o, r):
    lo = lax.broadcasted_iota(jnp.int32, (Q_BLOCK, LANES), 1) < HEAD_DIM
    return jnp.where(lo, o[(2 * r) * Q_BLOCK:(2 * r + 1) * Q_BLOCK], o[(2 * r + 1) * Q_BLOCK:(2 * r + 2) * Q_BLOCK])


def _lane_tiles(x):
    return [x[:, t * LANES:(t + 1) * LANES] for t in range(x.shape[1] // LANES)]


def _row_max(tiles):
    mx = tiles[0]
    for t in tiles[1:]:
        mx = jnp.maximum(mx, t)
    return jnp.broadcast_to(jnp.max(mx, axis=1, keepdims=True), mx.shape)


def _with_ones(v):
    return jnp.concatenate([v, jnp.ones(v.shape, v.dtype)], axis=1)


def _block_of_key(n_keys, first_block):
    b = lax.broadcasted_iota(jnp.int32, (LANES, n_keys), 0)
    k = lax.broadcasted_iota(jnp.int32, (LANES, n_keys), 1)
    return (b == (k // SEL_BLOCK) + first_block).astype(MXU_DTYPE)


def _select_blocks_t(imp_t, i, n_top):
    blk = lax.broadcasted_iota(jnp.int32, imp_t.shape, 0)
    qcol = lax.broadcasted_iota(jnp.int32, imp_t.shape, 1)
    back = (2 * i + (qcol >= SEL_BLOCK).astype(jnp.int32)) - blk
    sel = (back >= 0) & ((blk < SEL_INIT_BLOCKS) | (back < SEL_LOCAL_BLOCKS))
    cand = jnp.where((back >= SEL_LOCAL_BLOCKS) & (blk >= SEL_INIT_BLOCKS), imp_t, -1.0)
    blk_f = blk.astype(F32)
    for _ in range(n_top - SEL_INIT_BLOCKS - SEL_LOCAL_BLOCKS):
        m = jnp.max(cand, axis=0, keepdims=True)
        idx = jnp.min(jnp.where(cand == m, blk_f, float(LANES)), axis=0, keepdims=True)
        hit = blk_f == idx
        sel = sel | (hit & (m >= 0.0))
        cand = jnp.where(hit, -2.0, cand)
    return sel


def _attn_kernel(sink_ref, qa_ref, qb_ref, ga_ref, kcmp_ref, vcmp_ref, ks_ref, vs_ref, kw_ref, vw_ref, kb_ref,
                 vb_ref, cmat_ref, tnear_ref, tsel_ref, twin_ref, tswa_ref, oa_ref, ob_ref,
                 qall, qball, mneg, mneg_far, m_s, acc_s, s_buf, oa_acc, *, n_far, n_top):
    i = pl.program_id(1)
    rows = N_HEADS * Q_BLOCK
    half = rows // 2
    halves = (slice(0, half), slice(half, rows))
    _stack_heads(qa_ref, qall)
    _stack_heads(qb_ref, qball)
    nstart = pl.multiple_of(i * Q_BLOCK, Q_BLOCK)
    lo = lax.broadcasted_iota(jnp.int32, (Q_BLOCK, LANES), 1) < HEAD_DIM
    gates = ga_ref[...]

    def gate_tile(c, r):
        return jnp.where(lo, gates[:, c * 8 + 2 * r:c * 8 + 2 * r + 1], gates[:, c * 8 + 2 * r + 1:c * 8 + 2 * r + 2])

    def softmax_pv(s_tiles, v1, fix_max=None):
        m = _row_max(s_tiles)
        if fix_max is not None:
            m = fix_max(m)
        e = [jnp.exp2(t - m) for t in s_tiles]
        return e, m, _dot(_mx(jnp.concatenate(e, axis=1)), v1)

    off = pl.multiple_of(i * (Q_BLOCK // CMP_STRIDE), 8)
    k_cmp = _mx(jnp.concatenate([kcmp_ref[0, 0, 0:n_far, :], kcmp_ref[0, 0, pl.ds(off, CMP_NEAR), :]], axis=0))
    v_cmp = _with_ones(_mx(jnp.concatenate([vcmp_ref[0, 0, 0:n_far, :], vcmp_ref[0, 0, pl.ds(off, CMP_NEAR), :]],
                                           axis=0)))
    colf = lax.broadcasted_iota(jnp.int32, (1, n_far), 1)
    coln = lax.broadcasted_iota(jnp.int32, (1, CMP_NEAR), 1)
    col_ok = jnp.concatenate([(colf >= CMP_FRONT) & (colf < off), coln + off >= CMP_FRONT], axis=1)
    mask_c = jnp.where(col_ok, 0.0, NEG)
    no_key = lambda m: jnp.where(m > 0.5 * NEG, m, 0.0)
    p_cmp, o_c = [], []
    for rs in halves:
        tiles = _lane_tiles(_dot_nt(qall[rs, :], k_cmp) + mask_c)
        tiles[-1] = tiles[-1] + tnear_ref[rs, :]
        e, _, ov = softmax_pv(tiles, v_cmp, no_key)
        inv = 1.0 / jnp.maximum(ov[:, LANES:], 1e-30)
        o_c.append(ov[:, :LANES] * inv)
        p_cmp.append([t * inv for t in e])
    o_c = jnp.concatenate(o_c, axis=0)

    def far_start(j):
        return pl.multiple_of(Q_BLOCK + j * SEL_CHUNK, Q_BLOCK)

    def far_logits(j, slot, masked):
        kc = _mx(ks_ref[0, pl.ds(far_start(j), SEL_CHUNK), :])
        if masked:
            madd = _dot(mneg_far[...], _block_of_key(SEL_CHUNK, j * (SEL_CHUNK // SEL_BLOCK)))
        for rs in halves:
            s = _dot_nt(qall[rs, :], kc)
            s_buf[slot, rs, :] = s + jnp.concatenate([madd] * (GROUP // 2), axis=0) if masked else s

    far_logits(0, 0, False)

    blkcol = lax.broadcasted_iota(jnp.int32, (Q_BLOCK, LANES), 1)
    n_tiles = len(p_cmp[0])
    for g in range(N_GROUPS):
        imp = jnp.zeros((Q_BLOCK, LANES), F32)
        for t in range(n_tiles):
            pg = sum(p_cmp[r // 2][t][(2 * (r % 2) + g) * Q_BLOCK:(2 * (r % 2) + g + 1) * Q_BLOCK]
                     for r in range(GROUP))
            if t < n_tiles - 1:
                cm = _mx(cmat_ref[t * LANES:(t + 1) * LANES, :])
            else:
                cm = _mx(cmat_ref[pl.ds(off, CMP_NEAR), :])
            hi = _mx(pg)
            low = _mx(pg - hi.astype(F32))
            imp = imp + _dot(hi, cm) + _dot(low, cm)
        sel = _select_blocks_t(imp.T, i, n_top)
        neg = jnp.where(sel, 0.0, NEG).T
        mneg[g * Q_BLOCK:(g + 1) * Q_BLOCK, :] = neg.astype(mneg.dtype)
        mneg_far[g * Q_BLOCK:(g + 1) * Q_BLOCK, :] = jnp.where(blkcol < 2 * (i - 1), neg, NEG).astype(mneg.dtype)

    wpad = kw_ref.shape[1] - ks_ref.shape[1] + Q_BLOCK
    kwin = _mx(kw_ref[0, pl.ds(nstart, wpad + Q_BLOCK), :])
    vwin = _with_ones(_mx(vw_ref[0, pl.ds(nstart, wpad + Q_BLOCK), :]))
    colw = lax.broadcasted_iota(jnp.int32, (1, wpad + Q_BLOCK), 1)
    mask_w = jnp.where(colw + nstart >= wpad, 0.0, NEG)
    o_w = []
    for rs in halves:
        _, _, ov = softmax_pv(_lane_tiles(_dot_nt(qall[rs, :], kwin) + twin_ref[rs, :] + mask_w), vwin)
        o_w.append(ov[:, :LANES] / ov[:, LANES:])
    o_w = jnp.concatenate(o_w, axis=0)
    for r in range(GROUP):
        oa_acc[:, r * LANES:(r + 1) * LANES] = (gate_tile(0, r) * _pair_heads(o_c, r)
                                                + gate_tile(2, r) * _pair_heads(o_w, r))

    bpad = kb_ref.shape[1] - ks_ref.shape[1] + Q_BLOCK
    kwin = _mx(kb_ref[0, pl.ds(nstart, bpad + Q_BLOCK), :])
    vwin = _with_ones(_mx(vb_ref[0, pl.ds(nstart, bpad + Q_BLOCK), :]))
    colb = lax.broadcasted_iota(jnp.int32, (1, bpad + Q_BLOCK), 1)
    mask_b = jnp.where(colb + nstart >= bpad, 0.0, NEG)
    o_b = []
    for hh, rs in enumerate(halves):
        sink = jnp.concatenate([jnp.full((Q_BLOCK, LANES), sink_ref[(h % 2) * GROUP + h // 2], F32)
                                for h in range(hh * N_HEADS // 2, (hh + 1) * N_HEADS // 2)], axis=0)
        _, m, ov = softmax_pv(_lane_tiles(_dot_nt(qball[rs, :], kwin) + tswa_ref[rs, :] + mask_b), vwin,
                              lambda m: jnp.maximum(m, sink))
        o_b.append(ov[:, :LANES] / (ov[:, LANES:] + jnp.exp2(sink - m)))
    o_b = jnp.concatenate(o_b, axis=0)
    for r in range(GROUP):
        ob_ref[:, r * LANES:(r + 1) * LANES] = _pair_heads(o_b, r).astype(ob_ref.dtype)

    m_s[...] = jnp.full(m_s.shape, NEG, F32)
    acc_s[...] = jnp.zeros(acc_s.shape, F32)

    def flash_update(rs, s, v1):
        s_tiles = _lane_tiles(s)
        m_old = m_s[rs, :]
        m_new = jnp.maximum(m_old, _row_max(s_tiles))
        alpha = jnp.exp2(m_old - m_new)
        p = jnp.concatenate([jnp.exp2(t - m_new) for t in s_tiles], axis=1)
        acc_s[rs, :] = jnp.concatenate([alpha, alpha], axis=1) * acc_s[rs, :] + _dot(_mx(p), v1)
        m_s[rs, :] = m_new

    n_far_keys = jnp.maximum(i - 1, 0) * Q_BLOCK
    n_chunks = (n_far_keys + SEL_CHUNK - 1) // SEL_CHUNK

    madd = _dot(mneg_far[...], _block_of_key(SEL_CHUNK, 0))
    for rs in halves:
        s_buf[0, rs, :] = s_buf[0, rs, :] + jnp.concatenate([madd] * (GROUP // 2), axis=0)

    def far_update(j):
        v1 = _with_ones(_mx(vs_ref[0, pl.ds(far_start(j), SEL_CHUNK), :]))
        for rs in halves:
            flash_update(rs, s_buf[j % 2, rs, :], v1)

    def far_body(j, carry):
        far_update(j)
        far_logits(j + 1, (j + 1) % 2, True)
        return carry

    last = jnp.maximum(n_chunks - 1, 0)
    lax.fori_loop(0, last, far_body, 0)
    far_update(last)
    kc = _mx(ks_ref[0, pl.ds(nstart, 2 * Q_BLOCK), :])
    v1 = _with_ones(_mx(vs_ref[0, pl.ds(nstart, 2 * Q_BLOCK), :]))
    madd = _dot(mneg[...], _block_of_key(2 * Q_BLOCK, 2 * (i - 1)))
    col2 = lax.broadcasted_iota(jnp.int32, (1, 2 * Q_BLOCK), 1)
    mask_n = jnp.where((col2 < Q_BLOCK) & (i == 0), NEG, 0.0)
    for rs in halves:
        s = _dot_nt(qall[rs, :], kc) + jnp.concatenate([madd] * (GROUP // 2), axis=0) + tsel_ref[rs, :] + mask_n
        flash_update(rs, s, v1)
    acc = acc_s[...]
    o_s = acc[:, :LANES] / acc[:, LANES:]
    for r in range(GROUP):
        tile = oa_acc[:, r * LANES:(r + 1) * LANES] + gate_tile(1, r) * _pair_heads(o_s, r)
        oa_ref[:, r * LANES:(r + 1) * LANES] = tile.astype(oa_ref.dtype)


def _rel_bucket_np(dist):
    n = np.maximum(dist, 0)
    max_exact = REL_BUCKETS // 2
    nf = np.maximum(n, 1).astype(np.float32)
    log_b = max_exact + (np.log(nf / max_exact) / math.log(REL_MAX_DIST / max_exact)
                         * (REL_BUCKETS - max_exact)).astype(np.int32)
    log_b = np.minimum(log_b, REL_BUCKETS - 1)
    return np.where(n < max_exact, n, log_b)


def _toeplitz_bias(tab, pad, width, window, shift_far):
    length = width + Q_BLOCK
    dist = pad + Q_BLOCK - 1 - np.arange(length)
    onehot = np.zeros((length, REL_BUCKETS), np.float32)
    onehot[np.arange(length), _rel_bucket_np(dist)] = 1.0
    vals = jnp.dot(jnp.asarray(onehot), tab, precision=lax.Precision.HIGHEST)
    if shift_far:
        vals = vals - tab[REL_BUCKETS - 1][None, :]
    vals = vals * LOG2E
    valid = (dist >= 0) & (dist < window)
    vals = jnp.where(jnp.asarray(valid)[:, None], vals, NEG).T
    skew = jnp.tile(vals, (1, Q_BLOCK))[:, :Q_BLOCK * (length - 1)].reshape(N_HEADS, Q_BLOCK, length - 1)
    return skew[:, :, Q_BLOCK - 1:Q_BLOCK - 1 + width].reshape(N_HEADS * Q_BLOCK, width).astype(F32)


def _attention(proj, kvcmp, sinks, bias_table, bsz, seq):
    assert seq % SEL_CHUNK == 0
    nq = seq // Q_BLOCK
    n_far = seq // CMP_STRIDE
    n_sel = seq // SEL_BLOCK
    n_top = min(SEL_TOP_N, n_sel)
    assert n_top >= SEL_INIT_BLOCKS + SEL_LOCAL_BLOCKS and n_sel <= LANES
    wpad = Q_BLOCK * (-(-(NSA_WINDOW - 1) // Q_BLOCK))
    bpad = Q_BLOCK * (-(-(SWA_WINDOW - 1) // Q_BLOCK))
    pair = lambda tab: tab.astype(F32).reshape(REL_BUCKETS, N_GROUPS, GROUP).transpose(0, 2, 1).reshape(REL_BUCKETS, -1)
    tab_a = pair(bias_table[:, :N_HEADS])
    tab_b = pair(bias_table[:, N_HEADS:])
    near_pad = CMP_STRIDE * CMP_FRONT - (CMP_BLOCK - 1)
    t_near = _toeplitz_bias(tab_a, near_pad, CMP_STRIDE * CMP_NEAR, 1 << 30, True)[:, ::CMP_STRIDE]
    t_sel = _toeplitz_bias(tab_a, Q_BLOCK, 2 * Q_BLOCK, 1 << 30, True)
    t_win = _toeplitz_bias(tab_a, wpad, wpad + Q_BLOCK, NSA_WINDOW, False)
    t_swa = _toeplitz_bias(tab_b, bpad, bpad + Q_BLOCK, SWA_WINDOW, False)
    n_rows = kvcmp.shape[2]
    cn = (np.arange(n_rows) - CMP_FRONT)[:, None] * CMP_STRIDE
    sj = np.arange(LANES)[None, :] * SEL_BLOCK
    cmat = ((cn < sj + SEL_BLOCK) & (cn + CMP_BLOCK > sj) & (cn >= 0) & (cn + CMP_BLOCK <= seq)
            & (sj < seq)).astype(np.float32)
    cmat = jnp.asarray(cmat, F32)
    padded = lambda name, p: jnp.pad(proj[name].reshape(bsz, seq, LANES), ((0, 0), (p, 0), (0, 0)))
    ks, vs = padded('ks', Q_BLOCK), padded('vs', Q_BLOCK)
    kw, vw = padded('kw', wpad), padded('vw', wpad)
    kb, vb = padded('kb', bpad), padded('vb', bpad)
    rows = N_HEADS * Q_BLOCK
    qspec = pl.BlockSpec((Q_BLOCK, 4 * LANES), lambda b, i: (b * nq + i, 0))
    const2 = lambda shape: pl.BlockSpec(shape, lambda b, i: (0, 0))
    batch3 = lambda n: pl.BlockSpec((1, n, LANES), lambda b, i: (b, 0, 0))
    kernel = functools.partial(_attn_kernel, n_far=n_far, n_top=n_top)
    return pl.pallas_call(
        kernel,
        grid=(bsz, nq),
        in_specs=[pl.BlockSpec(memory_space=pltpu.SMEM),
                  qspec, qspec,
                  pl.BlockSpec((Q_BLOCK, LANES), lambda b, i: (b * nq + i, 0)),
                  pl.BlockSpec((1, 1, n_rows, LANES), lambda b, i: (0, b, 0, 0)),
                  pl.BlockSpec((1, 1, n_rows, LANES), lambda b, i: (1, b, 0, 0)),
                  batch3(seq + Q_BLOCK), batch3(seq + Q_BLOCK),
                  batch3(seq + wpad), batch3(seq + wpad),
                  batch3(seq + bpad), batch3(seq + bpad),
                  const2((n_rows, LANES)),
                  const2((rows, CMP_NEAR)),
                  const2((rows, 2 * Q_BLOCK)),
                  const2((rows, wpad + Q_BLOCK)),
                  const2((rows, bpad + Q_BLOCK))],
        out_specs=[qspec, qspec],
        out_shape=[jax.ShapeDtypeStruct((bsz * seq, 4 * LANES), BF16)] * 2,
        scratch_shapes=[pltpu.VMEM((rows, LANES), MXU_DTYPE),
                        pltpu.VMEM((rows, LANES), MXU_DTYPE),
                        pltpu.VMEM((N_GROUPS * Q_BLOCK, LANES), MXU_DTYPE),
                        pltpu.VMEM((N_GROUPS * Q_BLOCK, LANES), MXU_DTYPE),
                        pltpu.VMEM((rows, LANES), F32),
                        pltpu.VMEM((rows, 2 * LANES), F32),
                        pltpu.VMEM((2, rows, SEL_CHUNK), F32),
                        pltpu.VMEM((Q_BLOCK, 4 * LANES), F32)],
        compiler_params=pltpu.CompilerParams(dimension_semantics=("arbitrary", "arbitrary"),
                                             vmem_limit_bytes=VMEM_LIMIT),
        name="attention",
    )(sinks.astype(F32) * LOG2E, proj['qa'], proj['qb'], proj['ga'], kvcmp, kvcmp, ks, vs, kw, vw, kb, vb,
      cmat, t_near, t_sel, t_win, t_swa)


def _layer_norm(y, g, b):
    mu = jnp.mean(y, axis=-1, keepdims=True)
    yc = y - mu
    var = jnp.mean(yc * yc, axis=-1, keepdims=True)
    return yc * lax.rsqrt(var + LN_EPS) * g + b


def _outproj_kernel(oa_ref, ob_ref, sg_ref, x_ref, pa_ref, pb_ref, wo_ref, g1_ref, b1_ref, wr_ref, rb_ref, sgu_ref,
                    sd_ref, tri_ref, h_ref, base_ref, eidx_ref, gate_ref, rank_ref, cnt_ref, carry):
    step = pl.program_id(0)
    tm = oa_ref.shape[0]

    @pl.when(step == 0)
    def _():
        carry[...] = jnp.zeros(carry.shape, F32)

    sg = sg_ref[...].astype(F32)
    merged = (sg[:, :D_MODEL] * _dot(_mx(oa_ref[...]), pa_ref[...])
              + sg[:, D_MODEL:] * _dot(_mx(ob_ref[...]), pb_ref[...]))
    mix = _dot(_mx(merged), wo_ref[...])
    h = _layer_norm(DN_ALPHA * x_ref[...] + mix, g1_ref[...], b1_ref[...])
    hb = _mx(h)
    h_ref[...] = _pack_bf16_pairs(h)

    gu = _dot(hb, sgu_ref[...])
    shared = _dot(_mx(jax.nn.silu(gu[:, :SHARED_HIDDEN]) * gu[:, SHARED_HIDDEN:]), sd_ref[...])
    base_ref[...] = DN_ALPHA * h + shared

    scores = jax.nn.sigmoid(_dot_nt(wr_ref[...], hb))
    choice = scores + rb_ref[:, 0:1]
    per_group = N_EXPERTS // N_EXPERT_GROUPS
    gs = []
    for g in range(N_EXPERT_GROUPS):
        cg = choice[g * per_group:(g + 1) * per_group]
        m1 = jnp.max(cg, axis=0, keepdims=True)
        is_m = cg == m1
        n_m = jnp.sum(is_m.astype(F32), axis=0, keepdims=True)
        m2 = jnp.max(jnp.where(is_m, -jnp.inf, cg), axis=0, keepdims=True)
        gs.append(m1 + jnp.where(n_m > 1.5, m1, m2))
    gs = jnp.concatenate(gs, axis=0)
    gid = lax.broadcasted_iota(jnp.int32, gs.shape, 0)
    beaten = jnp.zeros(gs.shape, jnp.int32)
    for g in range(N_EXPERT_GROUPS):
        other = gs[g:g + 1]
        beaten = beaten + ((other > gs) | ((other == gs) & (g < gid))).astype(jnp.int32)
    keep_g = beaten < TOPK_EXPERT_GROUPS
    keep = jnp.concatenate([jnp.broadcast_to(keep_g[g:g + 1], (per_group, tm)) for g in range(N_EXPERT_GROUPS)],
                           axis=0)
    cand = jnp.where(keep, choice, -jnp.inf)
    eid = lax.broadcasted_iota(jnp.int32, cand.shape, 0)
    hits = []
    e_rows = []
    w_rows = []
    for _ in range(TOP_K):
        m = jnp.max(cand, axis=0, keepdims=True)
        idx = jnp.min(jnp.where(cand == m, eid, N_EXPERTS), axis=0, keepdims=True)
        hit = eid == idx
        hits.append(hit)
        e_rows.append(idx)
        w_rows.append(jnp.sum(jnp.where(hit, scores, 0.0), axis=0, keepdims=True))
        cand = jnp.where(hit, -jnp.inf, cand)
    w = jnp.concatenate(w_rows, axis=0)
    gate_ref[...] = w / jnp.sum(w, axis=0, keepdims=True) * ROUTED_SCALE
    eidx_ref[...] = jnp.concatenate(e_rows, axis=0)

    onehot = jnp.zeros(cand.shape, F32)
    for hit in hits:
        onehot = onehot + hit.astype(F32)
    before = _dot(onehot.astype(BF16), tri_ref[...]) + carry[:, 0:1]
    rank_ref[...] = jnp.concatenate(
        [jnp.sum(jnp.where(hit, before, 0.0), axis=0, keepdims=True) for hit in hits], axis=0).astype(jnp.int32)
    carry[...] = carry[...] + jnp.sum(onehot, axis=1, keepdims=True)
    cnt_ref[...] = carry[...]


def _out_projection(oa, ob, sg, x2, proj_a, proj_b, w_out, ln_g, ln_b, w_router, router_bias, s_gate, s_up, s_down):
    t = x2.shape[0]
    tm = OUT_TM
    pair_rows = lambda p: p.reshape(N_GROUPS, GROUP, HEAD_DIM, -1).transpose(1, 0, 2, 3).reshape(p.shape)
    pa = pair_rows(proj_a).astype(MXU_DTYPE)
    pb = pair_rows(proj_b).astype(MXU_DTYPE)
    tri = jnp.asarray(np.triu(np.ones((tm, tm), np.float32), 1), BF16)
    row = lambda i: (i, 0)
    fixed = lambda i: (0, 0)
    col = lambda i: (0, i)
    outs = pl.pallas_call(
        _outproj_kernel,
        grid=(t // tm,),
        in_specs=[pl.BlockSpec((tm, 4 * LANES), row), pl.BlockSpec((tm, 4 * LANES), row),
                  pl.BlockSpec((tm, 2 * D_MODEL), row), pl.BlockSpec((tm, D_MODEL), row),
                  pl.BlockSpec((4 * LANES, D_MODEL), fixed), pl.BlockSpec((4 * LANES, D_MODEL), fixed),
                  pl.BlockSpec((D_MODEL, D_MODEL), fixed),
                  pl.BlockSpec((1, D_MODEL), fixed), pl.BlockSpec((1, D_MODEL), fixed),
                  pl.BlockSpec((N_EXPERTS, D_MODEL), fixed), pl.BlockSpec((N_EXPERTS, LANES), fixed),
                  pl.BlockSpec((D_MODEL, 2 * SHARED_HIDDEN), fixed), pl.BlockSpec((SHARED_HIDDEN, D_MODEL), fixed),
                  pl.BlockSpec((tm, tm), fixed)],
        out_specs=[pl.BlockSpec((tm, D_MODEL // 2), row), pl.BlockSpec((tm, D_MODEL), row),
                   pl.BlockSpec((TOP_K, tm), col), pl.BlockSpec((TOP_K, tm), col), pl.BlockSpec((TOP_K, tm), col),
                   pl.BlockSpec((N_EXPERTS, LANES), fixed)],
        out_shape=[jax.ShapeDtypeStruct((t, D_MODEL // 2), jnp.uint32), jax.ShapeDtypeStruct((t, D_MODEL), F32),
                   jax.ShapeDtypeStruct((TOP_K, t), jnp.int32), jax.ShapeDtypeStruct((TOP_K, t), F32),
                   jax.ShapeDtypeStruct((TOP_K, t), jnp.int32), jax.ShapeDtypeStruct((N_EXPERTS, LANES), F32)],
        scratch_shapes=[pltpu.VMEM((N_EXPERTS, LANES), F32)],
        compiler_params=pltpu.CompilerParams(dimension_semantics=("arbitrary",), vmem_limit_bytes=VMEM_LIMIT),
        name="out_projection_router",
    )(oa, ob, sg, x2, pa, pb, w_out.astype(MXU_DTYPE), ln_g.reshape(1, -1), ln_b.reshape(1, -1),
      w_router.T.astype(MXU_DTYPE), jnp.broadcast_to(router_bias.astype(F32)[:, None], (N_EXPERTS, LANES)),
      jnp.concatenate([s_gate, s_up], axis=1).astype(MXU_DTYPE), s_down.astype(MXU_DTYPE), tri)
    return outs


def _rows_to_tiles(x):
    return pltpu.einshape("cml->mcl", jnp.stack(_lane_tiles(x), axis=0))


def _tiles_to_rows(x3):
    xt = pltpu.einshape("mcl->cml", x3)
    return jnp.concatenate([xt[c] for c in range(xt.shape[0])], axis=1)


def _dispatch_kernel(zstart_ref, cnt_ref, dest_ref, h2_ref, xs_ref, h_ref, zeros, sem, zsem):
    step = pl.program_id(0)
    tm = h2_ref.shape[0]
    slot = step % 2
    h_ref[slot] = _rows_to_tiles(h2_ref[...])

    @pl.when(step == 0)
    def _():
        zeros[...] = jnp.zeros(zeros.shape, zeros.dtype)

        def fill(e, c):
            @pl.when(cnt_ref[e] > 0)
            def _():
                pltpu.make_async_copy(zeros, xs_ref.at[pl.ds(zstart_ref[e], MOE_BM)], zsem).start()
            return c

        def fill_done(e, c):
            @pl.when(cnt_ref[e] > 0)
            def _():
                pltpu.make_async_copy(zeros, xs_ref.at[pl.ds(zstart_ref[e], MOE_BM)], zsem).wait()
            return c
        lax.fori_loop(0, N_EXPERTS, fill, 0)
        lax.fori_loop(0, N_EXPERTS, fill_done, 0)

    def issue(t, c):
        for k in range(TOP_K):
            pltpu.make_async_copy(h_ref.at[slot, t], xs_ref.at[dest_ref[k, t]], sem.at[slot]).start(priority=k % 2)
        return c
    lax.fori_loop(0, tm, issue, 0)

    def wait_tile(s):
        for k in range(TOP_K):
            pltpu.make_async_copy(h_ref.at[s], xs_ref.at[pl.ds(0, tm)], sem.at[s]).wait()

    @pl.when(step > 0)
    def _():
        wait_tile(1 - slot)

    @pl.when(step + 1 == pl.num_programs(0))
    def _():
        wait_tile(slot)


def _dispatch(h, dest, zstart, counts, n_rows):
    t = h.shape[0]
    tm = DISP_TM
    return pl.pallas_call(
        _dispatch_kernel,
        grid_spec=pltpu.PrefetchScalarGridSpec(
            num_scalar_prefetch=2,
            grid=(t // tm,),
            in_specs=[pl.BlockSpec((TOP_K, tm), lambda i, *_: (0, i), memory_space=pltpu.SMEM),
                      pl.BlockSpec((tm, D_MODEL // 2), lambda i, *_: (i, 0))],
            out_specs=pl.BlockSpec(memory_space=pl.ANY),
            scratch_shapes=[pltpu.VMEM((2, tm) + PACKED_ROW_TILE, jnp.uint32),
                            pltpu.VMEM((MOE_BM,) + PACKED_ROW_TILE, jnp.uint32),
                            pltpu.SemaphoreType.DMA((2,)), pltpu.SemaphoreType.DMA(())]),
        out_shape=jax.ShapeDtypeStruct((n_rows,) + PACKED_ROW_TILE, jnp.uint32),
        compiler_params=pltpu.CompilerParams(dimension_semantics=("arbitrary",), vmem_limit_bytes=VMEM_LIMIT),
        name="moe_dispatch",
    )(zstart, counts, dest, h)


def _experts_kernel(blk_e_ref, nused_ref, xs_ref, wg_ref, wu_ref, wd_ref, ys_ref, wg_s, wu_s, wd_s):
    b = pl.program_id(0)
    prev = blk_e_ref[jnp.maximum(b - 1, 0)]

    @pl.when((b == 0) | (blk_e_ref[b] != prev))
    def _():
        wg_s[...] = _mx(wg_ref[0])
        wu_s[...] = _mx(wu_ref[0])
        wd_s[...] = _mx(wd_ref[0])

    @pl.when(b < nused_ref[0])
    def _():
        xb = _mx(jnp.concatenate(_unpack_bf16_pairs(_tiles_to_rows(xs_ref[...])), axis=1))
        hid = jax.nn.silu(_dot(xb, wg_s[...])) * _dot(xb, wu_s[...])
        ys_ref[...] = _rows_to_tiles(_pack_bf16_pairs(_dot(_mx(hid), wd_s[...])))

    @pl.when(b >= nused_ref[0])
    def _():
        ys_ref[...] = jnp.zeros(ys_ref.shape, ys_ref.dtype)


def _experts(xs, blk_e, nused, e_gate, e_up, e_down):
    n_rows = xs.shape[0]
    n_blocks = n_rows // MOE_BM
    xmap = lambda b, be, nu: (jnp.minimum(b, nu[0] - 1), 0, 0)
    wmap = lambda b, be, nu: (be[b], 0, 0)
    return pl.pallas_call(
        _experts_kernel,
        grid_spec=pltpu.PrefetchScalarGridSpec(
            num_scalar_prefetch=2,
            grid=(n_blocks,),
            in_specs=[pl.BlockSpec((MOE_BM,) + PACKED_ROW_TILE, xmap),
                      pl.BlockSpec((1, D_MODEL, EXPERT_HIDDEN), wmap),
                      pl.BlockSpec((1, D_MODEL, EXPERT_HIDDEN), wmap),
                      pl.BlockSpec((1, EXPERT_HIDDEN, D_MODEL), wmap)],
            out_specs=pl.BlockSpec((MOE_BM,) + PACKED_ROW_TILE, lambda b, be, nu: (b, 0, 0)),
            scratch_shapes=[pltpu.VMEM((D_MODEL, EXPERT_HIDDEN), MXU_DTYPE),
                            pltpu.VMEM((D_MODEL, EXPERT_HIDDEN), MXU_DTYPE),
                            pltpu.VMEM((EXPERT_HIDDEN, D_MODEL), MXU_DTYPE)]),
        out_shape=jax.ShapeDtypeStruct((n_rows,) + PACKED_ROW_TILE, jnp.uint32),
        compiler_params=pltpu.CompilerParams(dimension_semantics=("arbitrary",), vmem_limit_bytes=VMEM_LIMIT),
        name="moe_experts",
    )(blk_e, nused, xs, e_gate, e_up, e_down)


def _combine_kernel(dest_ref, dest_next_ref, gate_ref, base_ref, g2_ref, b2_ref, ys_ref, out_ref, buf, routed, sem):
    step = pl.program_id(0)
    tm = base_ref.shape[0]
    slot = step % 2

    def gather_rows(d_ref, s, t):
        for k in range(TOP_K):
            j = t * TOP_K + k
            pltpu.make_async_copy(ys_ref.at[d_ref[j]], buf.at[s, j], sem.at[s]).start(priority=k % 2)

    def combine_token(t):
        low = jnp.zeros(PACKED_ROW_TILE, F32)
        high = jnp.zeros(PACKED_ROW_TILE, F32)
        for k in range(TOP_K):
            j = t * TOP_K + k
            lo_k, hi_k = _unpack_bf16_pairs(buf[slot, j])
            low = low + gate_ref[j] * lo_k
            high = high + gate_ref[j] * hi_k
        routed[t] = jnp.concatenate([low, high], axis=0)

    def for_tokens(body):
        def step_fn(t, c):
            body(t)
            return c
        lax.fori_loop(0, tm, step_fn, 0)

    @pl.when(step == 0)
    def _():
        for_tokens(lambda t: gather_rows(dest_ref, 0, t))

    pltpu.make_async_copy(ys_ref.at[pl.ds(0, tm * TOP_K)], buf.at[slot], sem.at[slot]).wait()

    @pl.when(step + 1 < pl.num_programs(0))
    def _():
        def both(t):
            gather_rows(dest_next_ref, 1 - slot, t)
            combine_token(t)
        for_tokens(both)

    @pl.when(step + 1 == pl.num_programs(0))
    def _():
        for_tokens(combine_token)

    out_ref[...] = _layer_norm(base_ref[...] + _tiles_to_rows(routed[...]), g2_ref[...], b2_ref[...])


def _combine(ys3, dest, gate, base, ln_g, ln_b):
    t = base.shape[0]
    tm = COMB_TM
    n_tiles = t // tm
    dest_tk = dest.T.reshape(-1)
    return pl.pallas_call(
        _combine_kernel,
        grid=(n_tiles,),
        in_specs=[pl.BlockSpec((tm * TOP_K,), lambda i: (i,), memory_space=pltpu.SMEM),
                  pl.BlockSpec((tm * TOP_K,), lambda i: (jnp.minimum(i + 1, n_tiles - 1),), memory_space=pltpu.SMEM),
                  pl.BlockSpec((tm * TOP_K,), lambda i: (i,), memory_space=pltpu.SMEM),
                  pl.BlockSpec((tm, D_MODEL), lambda i: (i, 0)),
                  pl.BlockSpec((1, D_MODEL), lambda i: (0, 0)),
                  pl.BlockSpec((1, D_MODEL), lambda i: (0, 0)),
                  pl.BlockSpec(memory_space=pl.ANY)],
        out_specs=pl.BlockSpec((tm, D_MODEL), lambda i: (i, 0)),
        out_shape=jax.ShapeDtypeStruct((t, D_MODEL), F32),
        scratch_shapes=[pltpu.VMEM((2, tm * TOP_K) + PACKED_ROW_TILE, jnp.uint32), pltpu.VMEM((tm,) + ROW_TILE, F32),
                        pltpu.SemaphoreType.DMA((2,))],
        compiler_params=pltpu.CompilerParams(dimension_semantics=("arbitrary",), vmem_limit_bytes=VMEM_LIMIT),
        name="moe_combine",
    )(dest_tk, dest_tk, gate.T.reshape(-1), base, ln_g.reshape(1, -1), ln_b.reshape(1, -1), ys3)


def _moe_layout(eidx, rank, counts):
    n_assign = eidx.size
    n_blocks = (n_assign + N_EXPERTS * (MOE_BM - 1)) // MOE_BM
    padded = (counts + MOE_BM - 1) // MOE_BM * MOE_BM
    pends = jnp.cumsum(padded)
    pstarts = pends - padded
    experts = jnp.arange(N_EXPERTS, dtype=jnp.int32)
    dest = jnp.sum(jnp.where(eidx[..., None] == experts, pstarts, 0), axis=-1) + rank
    block_row = jnp.arange(n_blocks, dtype=jnp.int32) * MOE_BM
    blk_e = jnp.minimum(jnp.sum(pends[None, :] <= block_row[:, None], axis=1), N_EXPERTS - 1).astype(jnp.int32)
    nused = (pends[-1:] // MOE_BM).astype(jnp.int32)
    zstart = jnp.maximum(pends - MOE_BM, 0).astype(jnp.int32)
    return dest.astype(jnp.int32), blk_e, nused, zstart, n_blocks * MOE_BM


def _layer(x, w_in, cmp_pe, cmp_w1, cmp_b1, cmp_w2, sinks, bias_table, proj_a, proj_b, w_out, ln1_g, ln1_b,
           w_router, router_bias, e_gate, e_up, e_down, s_gate, s_up, s_down, ln2_g, ln2_b):
    bsz, seq, d = x.shape
    x2 = x.reshape(bsz * seq, d)
    proj = _in_projection(x2, w_in)
    kvcmp = _compress(proj['kc'], proj['vc'], bsz, seq, cmp_pe, cmp_w1, cmp_b1, cmp_w2)
    oa, ob = _attention(proj, kvcmp, sinks, bias_table, bsz, seq)
    h, base, eidx, gate, rank, cnt = _out_projection(oa, ob, proj['sg'], x2, proj_a, proj_b, w_out, ln1_g, ln1_b,
                                                     w_router, router_bias, s_gate, s_up, s_down)
    counts = cnt[:, 0].astype(jnp.int32)
    dest, blk_e, nused, zstart, n_rows = _moe_layout(eidx, rank, counts)
    xs = _dispatch(h, dest, zstart, counts, n_rows)
    ys = _experts(xs, blk_e, nused, e_gate, e_up, e_down)
    out = _combine(ys, dest, gate, base, ln2_g, ln2_b)
    return out.reshape(bsz, seq, d)


def kernel(x, w_in, cmp_pe, cmp_w1, cmp_b1, cmp_w2, attn_sinks, rel_bias_table, proj_a, proj_b, w_out, ln1_g, ln1_b,
           w_router, router_bias, expert_w_gate, expert_w_up, expert_w_down, shared_w_gate, shared_w_up,
           shared_w_down, ln2_g, ln2_b):
    h = x
    for l in range(DEPTH):
        h = _layer(h, w_in[l], cmp_pe[l], cmp_w1[l], cmp_b1[l], cmp_w2[l], attn_sinks[l], rel_bias_table, proj_a[l],
                   proj_b[l], w_out[l], ln1_g[l], ln1_b[l], w_router[l], router_bias[l], expert_w_gate[l],
                   expert_w_up[l], expert_w_down[l], shared_w_gate[l], shared_w_up[l], shared_w_down[l], ln2_g[l],
                   ln2_b[l])
    return h
```

```python
import functools
import math

import numpy as np
import jax
import jax.numpy as jnp
from jax import lax
from jax.experimental import pallas as pl
from jax.experimental.pallas import tpu as pltpu

F32 = jnp.float32
BF16 = jnp.bfloat16
MXU_DTYPE = jnp.bfloat16

D_MODEL = 1024
HEAD_DIM = 64
ATTN_SCALE = HEAD_DIM ** -0.5
LOG2E = math.log2(math.e)
Q_BLOCK = 128
N_HEADS = 8
N_GROUPS = 2
GROUP = 4
CMP_BLOCK = 32
CMP_STRIDE = 16
CMP_HIDDEN = 128
SEL_BLOCK = 64
SEL_TOP_N = 8
SEL_INIT_BLOCKS = 1
SEL_LOCAL_BLOCKS = 2
NSA_WINDOW = 512
SWA_WINDOW = 128
REL_BUCKETS = 32
REL_MAX_DIST = 128
N_EXPERTS = 256
TOP_K = 8
EXPERT_HIDDEN = 256
SHARED_HIDDEN = 256
N_EXPERT_GROUPS = 8
TOPK_EXPERT_GROUPS = 4
ROUTED_SCALE = 2.5
LN_EPS = 1e-5
DEPTH = 1
DN_ALPHA = (2 * DEPTH) ** 0.25

NEG = -1e30
LANES = 128
ROW_TILE = (8, LANES)
PACKED_ROW_TILE = (4, LANES)
CMP_FRONT = 16
CMP_NEAR = LANES
SEL_CHUNK = 1024
QB_PER_STEP = 1
VMEM_LIMIT = 56 * 1024 * 1024

IN_TM = 512
OUT_TM = 512
MOE_BM = 512
DISP_TM = 256
COMB_TM = 256


def _dot(a, b):
    return jnp.dot(a, b, preferred_element_type=F32)


def _dot_nt(a, b):
    return lax.dot_general(a, b, (((1,), (1,)), ((), ())), preferred_element_type=F32)


def _mx(a):
    return a.astype(MXU_DTYPE)


def _pack_bf16_pairs(x):
    half = x.shape[1] // 2
    bits = lax.bitcast_convert_type(x.astype(BF16).astype(F32), jnp.uint32)
    return (bits[:, half:] & jnp.uint32(0xFFFF0000)) | (bits[:, :half] >> 16)


def _unpack_bf16_pairs(words):
    return (lax.bitcast_convert_type(words << 16, F32),
            lax.bitcast_convert_type(words & jnp.uint32(0xFFFF0000), F32))


_IN_COLS = (('qa', 512), ('qb', 512), ('kc', 128), ('vc', 128), ('ks', 128), ('vs', 128), ('kw', 128),
            ('vw', 128), ('kb', 128), ('vb', 128), ('ga', 128), ('sg', 2048))


def _inproj_kernel(x_ref, w_ref, qa_ref, qb_ref, kc_ref, vc_ref, ks_ref, vs_ref, kw_ref, vw_ref, kb_ref, vb_ref,
                   ga_ref, sg_ref):
    xb = _mx(x_ref[...])
    outs = dict(qa=qa_ref, qb=qb_ref, kc=kc_ref, vc=vc_ref, ks=ks_ref, vs=vs_ref, kw=kw_ref, vw=vw_ref,
                kb=kb_ref, vb=vb_ref, ga=ga_ref, sg=sg_ref)
    tiles = [(name, c) for name, width in _IN_COLS for c in range(0, width, LANES)]
    chunk = 4
    for t0 in range(0, len(tiles), chunk):
        group = tiles[t0:t0 + chunk]
        y = _dot(xb, w_ref[:, t0 * LANES:(t0 + len(group)) * LANES])
        for j, (name, c) in enumerate(group):
            yj = y[:, j * LANES:(j + 1) * LANES]
            if name in ('ga', 'sg'):
                yj = jax.nn.sigmoid(yj)
            outs[name][:, c:c + LANES] = yj.astype(outs[name].dtype)


def _pair_head_columns(w):
    return w.reshape(w.shape[0], N_GROUPS, GROUP, HEAD_DIM).transpose(0, 2, 1, 3).reshape(w.shape[0], -1)


def _in_projection(x2, w_in):
    t = x2.shape[0]
    sizes = (512, 128, 128, 128, 128, 128, 128, 24, 512, 128, 128, 1024, 1024)
    offs = np.cumsum((0,) + sizes)
    part = [w_in[:, offs[k]:offs[k + 1]] for k in range(len(sizes))]
    w_qa, w_kc, w_vc, w_ks, w_vs, w_kw, w_vw, w_g, w_qb, w_kb, w_vb, w_gate_a, w_gate_b = part
    w_qa = _pair_head_columns(w_qa) * (ATTN_SCALE * LOG2E)
    w_qb = _pair_head_columns(w_qb) * (ATTN_SCALE * LOG2E)
    w_ga = w_g.reshape(-1, N_GROUPS, GROUP, 3).transpose(0, 3, 2, 1).reshape(-1, 24)
    w_ga = jnp.pad(w_ga, ((0, 0), (0, LANES - 24)))
    w_all = jnp.concatenate([w_qa, w_qb, w_kc, w_vc, w_ks, w_vs, w_kw, w_vw, w_kb, w_vb, w_ga, w_gate_a, w_gate_b],
                            axis=1).astype(MXU_DTYPE)
    n_all = w_all.shape[1]
    out_shape = []
    out_specs = []
    for name, width in _IN_COLS:
        dt = F32 if name == 'ga' else BF16
        out_shape.append(jax.ShapeDtypeStruct((t, width), dt))
        out_specs.append(pl.BlockSpec((IN_TM, width), lambda i: (i, 0)))
    outs = pl.pallas_call(
        _inproj_kernel,
        grid=(t // IN_TM,),
        in_specs=[pl.BlockSpec((IN_TM, D_MODEL), lambda i: (i, 0)),
                  pl.BlockSpec((D_MODEL, n_all), lambda i: (0, 0))],
        out_specs=out_specs,
        out_shape=out_shape,
        compiler_params=pltpu.CompilerParams(dimension_semantics=("arbitrary",), vmem_limit_bytes=VMEM_LIMIT),
        name="in_projection",
    )(x2, w_all)
    return dict(zip([n for n, _ in _IN_COLS], outs))


def _compress_kernel(tok_ref, w1_ref, pe_ref, w1o_ref, b1_ref, w2_ref, out_ref):
    n_chunks = tok_ref.shape[2]
    ab = _dot(tok_ref[0, 0], w1_ref[0])
    a = ab[:, :2 * CMP_HIDDEN]
    b_next = pltpu.roll(ab[:, 2 * CMP_HIDDEN:], n_chunks - 1, 0)
    cb = _dot(_mx(pe_ref[0]), _mx(w1o_ref[0]))[0:1, :] + b1_ref[0]
    cb2 = jnp.concatenate([cb, cb], axis=1)
    hid = jax.nn.gelu(a + b_next + cb2)
    out = _dot(_mx(hid), w2_ref[0])
    row = lax.broadcasted_iota(jnp.int32, out.shape, 0)
    out = jnp.where(row < n_chunks - 1, out, 0.0)
    out_ref[0, 0, 0:CMP_FRONT, :] = jnp.zeros((CMP_FRONT, LANES), F32)
    out_ref[0, 0, CMP_FRONT:CMP_FRONT + n_chunks, :] = out
    out_ref[0, 0, CMP_FRONT + n_chunks:, :] = jnp.zeros((CMP_NEAR - CMP_FRONT, LANES), F32)


def _compress(kc, vc, bsz, seq, cmp_pe, cmp_w1, cmp_b1, cmp_w2):
    n_chunks = seq // CMP_STRIDE
    tok = jnp.stack([kc, vc]).reshape(2, bsz, n_chunks, CMP_STRIDE * LANES)
    eye = jnp.eye(N_GROUPS, dtype=F32)
    w1r = cmp_w1.reshape(2, 2, CMP_STRIDE, HEAD_DIM, CMP_HIDDEN)
    w1 = jnp.einsum('khjdn,gG->kjgdhGn', w1r, eye).reshape(2, CMP_STRIDE * LANES, 4 * CMP_HIDDEN).astype(MXU_DTYPE)
    w2 = jnp.einsum('knd,gG->kgnGd', cmp_w2, eye).reshape(2, 2 * CMP_HIDDEN, LANES).astype(MXU_DTYPE)
    pe = jnp.pad(cmp_pe.reshape(2, 1, CMP_BLOCK * HEAD_DIM), ((0, 0), (0, 7), (0, 0)))
    b1 = cmp_b1.reshape(2, 1, CMP_HIDDEN)
    rows = CMP_FRONT + n_chunks + CMP_NEAR - CMP_FRONT
    return pl.pallas_call(
        _compress_kernel,
        grid=(2, bsz),
        in_specs=[pl.BlockSpec((1, 1, n_chunks, CMP_STRIDE * LANES), lambda k, b: (k, b, 0, 0)),
                  pl.BlockSpec((1, CMP_STRIDE * LANES, 4 * CMP_HIDDEN), lambda k, b: (k, 0, 0)),
                  pl.BlockSpec((1, 8, CMP_BLOCK * HEAD_DIM), lambda k, b: (k, 0, 0)),
                  pl.BlockSpec((1, CMP_BLOCK * HEAD_DIM, CMP_HIDDEN), lambda k, b: (k, 0, 0)),
                  pl.BlockSpec((1, 1, CMP_HIDDEN), lambda k, b: (k, 0, 0)),
                  pl.BlockSpec((1, 2 * CMP_HIDDEN, LANES), lambda k, b: (k, 0, 0))],
        out_specs=pl.BlockSpec((1, 1, rows, LANES), lambda k, b: (k, b, 0, 0)),
        out_shape=jax.ShapeDtypeStruct((2, bsz, rows, LANES), F32),
        compiler_params=pltpu.CompilerParams(dimension_semantics=("arbitrary", "arbitrary"),
                                             vmem_limit_bytes=VMEM_LIMIT),
        name="nsa_compress",
    )(tok, w1, pe, cmp_w1, b1, w2)


def _stack_heads(q_ref, dst):
    lo = lax.broadcasted_iota(jnp.int32, (Q_BLOCK, LANES), 1) < HEAD_DIM
    for r in range(GROUP):
        qr = q_ref[:, r * LANES:(r + 1) * LANES].astype(dst.dtype)
        z = jnp.zeros_like(qr)
        dst[(2 * r) * Q_BLOCK:(2 * r + 1) * Q_BLOCK, :] = jnp.where(lo, qr, z)
        dst[(2 * r + 1) * Q_BLOCK:(2 * r + 2) * Q_BLOCK, :] = jnp.where(lo, z, qr)


def _pair_heads(o, r):
    lo = lax.broadcasted_iota(jnp.int32, (Q_BLOCK, LANES), 1) < HEAD_DIM
    return jnp.where(lo, o[(2 * r) * Q_BLOCK:(2 * r + 1) * Q_BLOCK], o[(2 * r + 1) * Q_BLOCK:(2 * r + 2) * Q_BLOCK])


def _lane_tiles(x):
    return [x[:, t * LANES:(t + 1) * LANES] for t in range(x.shape[1] // LANES)]


def _row_max(tiles):
    mx = tiles[0]
    for t in tiles[1:]:
        mx = jnp.maximum(mx, t)
    return jnp.broadcast_to(jnp.max(mx, axis=1, keepdims=True), mx.shape)


def _with_ones(v):
    return jnp.concatenate([v, jnp.ones(v.shape, v.dtype)], axis=1)


def _block_of_key(n_keys, first_block):
    b = lax.broadcasted_iota(jnp.int32, (LANES, n_keys), 0)
    k = lax.broadcasted_iota(jnp.int32, (LANES, n_keys), 1)
    return (b == (k // SEL_BLOCK) + first_block).astype(MXU_DTYPE)


def _select_blocks_t(imp_t, i, n_top):
    blk = lax.broadcasted_iota(jnp.int32, imp_t.shape, 0)
    qcol = lax.broadcasted_iota(jnp.int32, imp_t.shape, 1)
    back = (2 * i + (qcol >= SEL_BLOCK).astype(jnp.int32)) - blk
    sel = (back >= 0) & ((blk < SEL_INIT_BLOCKS) | (back < SEL_LOCAL_BLOCKS))
    cand = jnp.where((back >= SEL_LOCAL_BLOCKS) & (blk >= SEL_INIT_BLOCKS), imp_t, -1.0)
    blk_f = blk.astype(F32)
    for _ in range(n_top - SEL_INIT_BLOCKS - SEL_LOCAL_BLOCKS):
        m = jnp.max(cand, axis=0, keepdims=True)
        idx = jnp.min(jnp.where(cand == m, blk_f, float(LANES)), axis=0, keepdims=True)
        hit = blk_f == idx
        sel = sel | (hit & (m >= 0.0))
        cand = jnp.where(hit, -2.0, cand)
    return sel


def _query_block(i, sink_ref, qa_ref, qb_ref, ga_ref, kcmp_ref, vcmp_ref, ks_ref, vs_ref, kw_ref, vw_ref, kb_ref,
                 vb_ref, cmat_ref, tnear_ref, tsel_ref, twin_ref, tswa_ref, oa_ref, ob_ref,
                 qall, qball, mneg, mneg_far, m_s, acc_s, s_buf, oa_acc, n_far, n_top):
    rows = N_HEADS * Q_BLOCK
    half = rows // 2
    halves = (slice(0, half), slice(half, rows))
    _stack_heads(qa_ref, qall)
    _stack_heads(qb_ref, qball)
    nstart = pl.multiple_of(i * Q_BLOCK, Q_BLOCK)
    lo = lax.broadcasted_iota(jnp.int32, (Q_BLOCK, LANES), 1) < HEAD_DIM
    gates = ga_ref[...]

    def gate_tile(c, r):
        return jnp.where(lo, gates[:, c * 8 + 2 * r:c * 8 + 2 * r + 1], gates[:, c * 8 + 2 * r + 1:c * 8 + 2 * r + 2])

    def softmax_pv(s_tiles, v1, fix_max=None):
        m = _row_max(s_tiles)
        if fix_max is not None:
            m = fix_max(m)
        e = [jnp.exp2(t - m) for t in s_tiles]
        return e, m, _dot(_mx(jnp.concatenate(e, axis=1)), v1)

    off = pl.multiple_of(i * (Q_BLOCK // CMP_STRIDE), 8)
    k_cmp = _mx(jnp.concatenate([kcmp_ref[0, 0, 0:n_far, :], kcmp_ref[0, 0, pl.ds(off, CMP_NEAR), :]], axis=0))
    v_cmp = _with_ones(_mx(jnp.concatenate([vcmp_ref[0, 0, 0:n_far, :], vcmp_ref[0, 0, pl.ds(off, CMP_NEAR), :]],
                                           axis=0)))
    colf = lax.broadcasted_iota(jnp.int32, (1, n_far), 1)
    coln = lax.broadcasted_iota(jnp.int32, (1, CMP_NEAR), 1)
    col_ok = jnp.concatenate([(colf >= CMP_FRONT) & (colf < off), coln + off >= CMP_FRONT], axis=1)
    mask_c = jnp.where(col_ok, 0.0, NEG)
    no_key = lambda m: jnp.where(m > 0.5 * NEG, m, 0.0)
    p_cmp, o_c = [], []
    for rs in halves:
        tiles = _lane_tiles(_dot_nt(qall[rs, :], k_cmp) + mask_c)
        tiles[-1] = tiles[-1] + tnear_ref[rs, :]
        e, _, ov = softmax_pv(tiles, v_cmp, no_key)
        inv = 1.0 / jnp.maximum(ov[:, LANES:], 1e-30)
        o_c.append(ov[:, :LANES] * inv)
        p_cmp.append([t * inv for t in e])
    o_c = jnp.concatenate(o_c, axis=0)
    yield None

    def far_start(j):
        return pl.multiple_of(Q_BLOCK + j * SEL_CHUNK, Q_BLOCK)

    def far_logits(j, slot, masked):
        kc = _mx(ks_ref[0, pl.ds(far_start(j), SEL_CHUNK), :])
        if masked:
            madd = _dot(mneg_far[...], _block_of_key(SEL_CHUNK, j * (SEL_CHUNK // SEL_BLOCK)))
        for rs in halves:
            s = _dot_nt(qall[rs, :], kc)
            s_buf[slot, rs, :] = s + jnp.concatenate([madd] * (GROUP // 2), axis=0) if masked else s

    far_logits(0, 0, False)

    blkcol = lax.broadcasted_iota(jnp.int32, (Q_BLOCK, LANES), 1)
    n_tiles = len(p_cmp[0])
    for g in range(N_GROUPS):
        imp = jnp.zeros((Q_BLOCK, LANES), F32)
        for t in range(n_tiles):
            pg = sum(p_cmp[r // 2][t][(2 * (r % 2) + g) * Q_BLOCK:(2 * (r % 2) + g + 1) * Q_BLOCK]
                     for r in range(GROUP))
            if t < n_tiles - 1:
                cm = _mx(cmat_ref[t * LANES:(t + 1) * LANES, :])
            else:
                cm = _mx(cmat_ref[pl.ds(off, CMP_NEAR), :])
            hi = _mx(pg)
            low = _mx(pg - hi.astype(F32))
            imp = imp + _dot(hi, cm) + _dot(low, cm)
        sel = _select_blocks_t(imp.T, i, n_top)
        neg = jnp.where(sel, 0.0, NEG).T
        mneg[g * Q_BLOCK:(g + 1) * Q_BLOCK, :] = neg.astype(mneg.dtype)
        mneg_far[g * Q_BLOCK:(g + 1) * Q_BLOCK, :] = jnp.where(blkcol < 2 * (i - 1), neg, NEG).astype(mneg.dtype)

    yield None

    wpad = kw_ref.shape[1] - ks_ref.shape[1] + Q_BLOCK
    kwin = _mx(kw_ref[0, pl.ds(nstart, wpad + Q_BLOCK), :])
    vwin = _with_ones(_mx(vw_ref[0, pl.ds(nstart, wpad + Q_BLOCK), :]))
    colw = lax.broadcasted_iota(jnp.int32, (1, wpad + Q_BLOCK), 1)
    mask_w = jnp.where(colw + nstart >= wpad, 0.0, NEG)
    o_w = []
    for rs in halves:
        _, _, ov = softmax_pv(_lane_tiles(_dot_nt(qall[rs, :], kwin) + twin_ref[rs, :] + mask_w), vwin)
        o_w.append(ov[:, :LANES] / ov[:, LANES:])
    o_w = jnp.concatenate(o_w, axis=0)
    for r in range(GROUP):
        oa_acc[:, r * LANES:(r + 1) * LANES] = (gate_tile(0, r) * _pair_heads(o_c, r)
                                                + gate_tile(2, r) * _pair_heads(o_w, r))

    yield None

    bpad = kb_ref.shape[1] - ks_ref.shape[1] + Q_BLOCK
    kwin = _mx(kb_ref[0, pl.ds(nstart, bpad + Q_BLOCK), :])
    vwin = _with_ones(_mx(vb_ref[0, pl.ds(nstart, bpad + Q_BLOCK), :]))
    colb = lax.broadcasted_iota(jnp.int32, (1, bpad + Q_BLOCK), 1)
    mask_b = jnp.where(colb + nstart >= bpad, 0.0, NEG)
    o_b = []
    for hh, rs in enumerate(halves):
        sink = jnp.concatenate([jnp.full((Q_BLOCK, LANES), sink_ref[(h % 2) * GROUP + h // 2], F32)
                                for h in range(hh * N_HEADS // 2, (hh + 1) * N_HEADS // 2)], axis=0)
        _, m, ov = softmax_pv(_lane_tiles(_dot_nt(qball[rs, :], kwin) + tswa_ref[rs, :] + mask_b), vwin,
                              lambda m: jnp.maximum(m, sink))
        o_b.append(ov[:, :LANES] / (ov[:, LANES:] + jnp.exp2(sink - m)))
    o_b = jnp.concatenate(o_b, axis=0)
    for r in range(GROUP):
        ob_ref[:, r * LANES:(r + 1) * LANES] = _pair_heads(o_b, r).astype(ob_ref.dtype)

    yield None

    m_s[...] = jnp.full(m_s.shape, NEG, F32)
    acc_s[...] = jnp.zeros(acc_s.shape, F32)

    def flash_update(rs, s, v1):
        s_tiles = _lane_tiles(s)
        m_old = m_s[rs, :]
        m_new = jnp.maximum(m_old, _row_max(s_tiles))
        alpha = jnp.exp2(m_old - m_new)
        p = jnp.concatenate([jnp.exp2(t - m_new) for t in s_tiles], axis=1)
        acc_s[rs, :] = jnp.concatenate([alpha, alpha], axis=1) * acc_s[rs, :] + _dot(_mx(p), v1)
        m_s[rs, :] = m_new

    madd = _dot(mneg_far[...], _block_of_key(SEL_CHUNK, 0))
    for rs in halves:
        s_buf[0, rs, :] = s_buf[0, rs, :] + jnp.concatenate([madd] * (GROUP // 2), axis=0)

    def far_update(j):
        v1 = _with_ones(_mx(vs_ref[0, pl.ds(far_start(j), SEL_CHUNK), :]))
        for rs in halves:
            flash_update(rs, s_buf[j % 2, rs, :], v1)

    yield far_update, far_logits

    kc = _mx(ks_ref[0, pl.ds(nstart, 2 * Q_BLOCK), :])
    v1 = _with_ones(_mx(vs_ref[0, pl.ds(nstart, 2 * Q_BLOCK), :]))
    madd = _dot(mneg[...], _block_of_key(2 * Q_BLOCK, 2 * (i - 1)))
    col2 = lax.broadcasted_iota(jnp.int32, (1, 2 * Q_BLOCK), 1)
    mask_n = jnp.where((col2 < Q_BLOCK) & (i == 0), NEG, 0.0)
    for rs in halves:
        s = _dot_nt(qall[rs, :], kc) + jnp.concatenate([madd] * (GROUP // 2), axis=0) + tsel_ref[rs, :] + mask_n
        flash_update(rs, s, v1)
    acc = acc_s[...]
    o_s = acc[:, :LANES] / acc[:, LANES:]
    for r in range(GROUP):
        tile = oa_acc[:, r * LANES:(r + 1) * LANES] + gate_tile(1, r) * _pair_heads(o_s, r)
        oa_ref[:, r * LANES:(r + 1) * LANES] = tile.astype(oa_ref.dtype)
    yield None


def _attn_kernel(sink_ref, qa_ref, qb_ref, ga_ref, kcmp_ref, vcmp_ref, ks_ref, vs_ref, kw_ref, vw_ref, kb_ref,
                 vb_ref, cmat_ref, tnear_ref, tsel_ref, twin_ref, tswa_ref, oa_ref, ob_ref,
                 qall, qball, mneg, mneg_far, m_s, acc_s, s_buf, oa_acc, *, n_far, n_top):
    first = pl.program_id(1) * QB_PER_STEP
    blocks, steps = [], []
    for n in range(QB_PER_STEP):
        qrows = pl.ds(n * Q_BLOCK, Q_BLOCK)
        blk = _query_block(first + n, sink_ref, qa_ref.at[qrows, :], qb_ref.at[qrows, :], ga_ref.at[qrows, :],
                           kcmp_ref, vcmp_ref, ks_ref, vs_ref, kw_ref, vw_ref, kb_ref, vb_ref, cmat_ref, tnear_ref,
                           tsel_ref, twin_ref, tswa_ref, oa_ref.at[qrows, :], ob_ref.at[qrows, :],
                           qall.at[n], qball.at[n], mneg.at[n], mneg_far.at[n], m_s.at[n], acc_s.at[n], s_buf.at[n],
                           oa_acc.at[n], n_far, n_top)
        blocks.append(blk)
    steps = [next(blk) for blk in blocks]
    while steps[0] is None:
        steps = [next(blk) for blk in blocks]
    n_far_keys = jnp.maximum(first + QB_PER_STEP - 2, 0) * Q_BLOCK
    n_chunks = (n_far_keys + SEL_CHUNK - 1) // SEL_CHUNK

    def far_body(j, carry):
        for far_update, _ in steps:
            far_update(j)
        for _, far_logits in steps:
            far_logits(j + 1, (j + 1) % 2, True)
        return carry

    last = jnp.maximum(n_chunks - 1, 0)
    lax.fori_loop(0, last, far_body, 0)
    for far_update, _ in steps:
        far_update(last)
    for blk in blocks:
        next(blk)


def _rel_bucket_np(dist):
    n = np.maximum(dist, 0)
    max_exact = REL_BUCKETS // 2
    nf = np.maximum(n, 1).astype(np.float32)
    log_b = max_exact + (np.log(nf / max_exact) / math.log(REL_MAX_DIST / max_exact)
                         * (REL_BUCKETS - max_exact)).astype(np.int32)
    log_b = np.minimum(log_b, REL_BUCKETS - 1)
    return np.where(n < max_exact, n, log_b)


def _toeplitz_bias(tab, pad, width, window, shift_far):
    length = width + Q_BLOCK
    dist = pad + Q_BLOCK - 1 - np.arange(length)
    onehot = np.zeros((length, REL_BUCKETS), np.float32)
    onehot[np.arange(length), _rel_bucket_np(dist)] = 1.0
    vals = jnp.dot(jnp.asarray(onehot), tab, precision=lax.Precision.HIGHEST)
    if shift_far:
        vals = vals - tab[REL_BUCKETS - 1][None, :]
    vals = vals * LOG2E
    valid = (dist >= 0) & (dist < window)
    vals = jnp.where(jnp.asarray(valid)[:, None], vals, NEG).T
    skew = jnp.tile(vals, (1, Q_BLOCK))[:, :Q_BLOCK * (length - 1)].reshape(N_HEADS, Q_BLOCK, length - 1)
    return skew[:, :, Q_BLOCK - 1:Q_BLOCK - 1 + width].reshape(N_HEADS * Q_BLOCK, width).astype(F32)


def _attention(proj, kvcmp, sinks, bias_table, bsz, seq):
    assert seq % SEL_CHUNK == 0
    nq = seq // Q_BLOCK
    n_far = seq // CMP_STRIDE
    n_sel = seq // SEL_BLOCK
    n_top = min(SEL_TOP_N, n_sel)
    assert n_top >= SEL_INIT_BLOCKS + SEL_LOCAL_BLOCKS and n_sel <= LANES
    wpad = Q_BLOCK * (-(-(NSA_WINDOW - 1) // Q_BLOCK))
    bpad = Q_BLOCK * (-(-(SWA_WINDOW - 1) // Q_BLOCK))
    pair = lambda tab: tab.astype(F32).reshape(REL_BUCKETS, N_GROUPS, GROUP).transpose(0, 2, 1).reshape(REL_BUCKETS, -1)
    tab_a = pair(bias_table[:, :N_HEADS])
    tab_b = pair(bias_table[:, N_HEADS:])
    near_pad = CMP_STRIDE * CMP_FRONT - (CMP_BLOCK - 1)
    t_near = _toeplitz_bias(tab_a, near_pad, CMP_STRIDE * CMP_NEAR, 1 << 30, True)[:, ::CMP_STRIDE]
    t_sel = _toeplitz_bias(tab_a, Q_BLOCK, 2 * Q_BLOCK, 1 << 30, True)
    t_win = _toeplitz_bias(tab_a, wpad, wpad + Q_BLOCK, NSA_WINDOW, False)
    t_swa = _toeplitz_bias(tab_b, bpad, bpad + Q_BLOCK, SWA_WINDOW, False)
    n_rows = kvcmp.shape[2]
    cn = (np.arange(n_rows) - CMP_FRONT)[:, None] * CMP_STRIDE
    sj = np.arange(LANES)[None, :] * SEL_BLOCK
    cmat = ((cn < sj + SEL_BLOCK) & (cn + CMP_BLOCK > sj) & (cn >= 0) & (cn + CMP_BLOCK <= seq)
            & (sj < seq)).astype(np.float32)
    cmat = jnp.asarray(cmat, F32)
    padded = lambda name, p: jnp.pad(proj[name].reshape(bsz, seq, LANES), ((0, 0), (p, 0), (0, 0)))
    ks, vs = padded('ks', Q_BLOCK), padded('vs', Q_BLOCK)
    kw, vw = padded('kw', wpad), padded('vw', wpad)
    kb, vb = padded('kb', bpad), padded('vb', bpad)
    rows = N_HEADS * Q_BLOCK
    n_steps = nq // QB_PER_STEP
    qspec = pl.BlockSpec((QB_PER_STEP * Q_BLOCK, 4 * LANES), lambda b, i: (b * n_steps + i, 0))
    const2 = lambda shape: pl.BlockSpec(shape, lambda b, i: (0, 0))
    batch3 = lambda n: pl.BlockSpec((1, n, LANES), lambda b, i: (b, 0, 0))
    per_block = lambda shape, dtype: pltpu.VMEM((QB_PER_STEP,) + shape, dtype)
    kernel = functools.partial(_attn_kernel, n_far=n_far, n_top=n_top)
    return pl.pallas_call(
        kernel,
        grid=(bsz, n_steps),
        in_specs=[pl.BlockSpec(memory_space=pltpu.SMEM),
                  qspec, qspec,
                  pl.BlockSpec((QB_PER_STEP * Q_BLOCK, LANES), lambda b, i: (b * n_steps + i, 0)),
                  pl.BlockSpec((1, 1, n_rows, LANES), lambda b, i: (0, b, 0, 0)),
                  pl.BlockSpec((1, 1, n_rows, LANES), lambda b, i: (1, b, 0, 0)),
                  batch3(seq + Q_BLOCK), batch3(seq + Q_BLOCK),
                  batch3(seq + wpad), batch3(seq + wpad),
                  batch3(seq + bpad), batch3(seq + bpad),
                  const2((n_rows, LANES)),
                  const2((rows, CMP_NEAR)),
                  const2((rows, 2 * Q_BLOCK)),
                  const2((rows, wpad + Q_BLOCK)),
                  const2((rows, bpad + Q_BLOCK))],
        out_specs=[qspec, qspec],
        out_shape=[jax.ShapeDtypeStruct((bsz * seq, 4 * LANES), BF16)] * 2,
        scratch_shapes=[per_block((rows, LANES), MXU_DTYPE),
                        per_block((rows, LANES), MXU_DTYPE),
                        per_block((N_GROUPS * Q_BLOCK, LANES), MXU_DTYPE),
                        per_block((N_GROUPS * Q_BLOCK, LANES), MXU_DTYPE),
                        per_block((rows, LANES), F32),
                        per_block((rows, 2 * LANES), F32),
                        per_block((2, rows, SEL_CHUNK), F32),
                        per_block((Q_BLOCK, 4 * LANES), F32)],
        compiler_params=pltpu.CompilerParams(dimension_semantics=("arbitrary", "arbitrary"),
                                             vmem_limit_bytes=VMEM_LIMIT),
        name="attention",
    )(sinks.astype(F32) * LOG2E, proj['qa'], proj['qb'], proj['ga'], kvcmp, kvcmp, ks, vs, kw, vw, kb, vb,
      cmat, t_near, t_sel, t_win, t_swa)


def _layer_norm(y, g, b):
    mu = jnp.mean(y, axis=-1, keepdims=True)
    yc = y - mu
    var = jnp.mean(yc * yc, axis=-1, keepdims=True)
    return yc * lax.rsqrt(var + LN_EPS) * g + b


def _outproj_kernel(oa_ref, ob_ref, sg_ref, x_ref, pa_ref, pb_ref, wo_ref, g1_ref, b1_ref, wr_ref, rb_ref, sgu_ref,
                    sd_ref, tri_ref, h_ref, base_ref, eidx_ref, gate_ref, rank_ref, cnt_ref, carry):
    step = pl.program_id(0)
    tm = oa_ref.shape[0]

    @pl.when(step == 0)
    def _():
        carry[...] = jnp.zeros(carry.shape, F32)

    sg = sg_ref[...].astype(F32)
    merged = (sg[:, :D_MODEL] * _dot(_mx(oa_ref[...]), pa_ref[...])
              + sg[:, D_MODEL:] * _dot(_mx(ob_ref[...]), pb_ref[...]))
    mix = _dot(_mx(merged), wo_ref[...])
    h = _layer_norm(DN_ALPHA * x_ref[...] + mix, g1_ref[...], b1_ref[...])
    hb = _mx(h)
    h_ref[...] = _pack_bf16_pairs(h)

    gu = _dot(hb, sgu_ref[...])
    shared = _dot(_mx(jax.nn.silu(gu[:, :SHARED_HIDDEN]) * gu[:, SHARED_HIDDEN:]), sd_ref[...])
    base_ref[...] = DN_ALPHA * h + shared

    scores = jax.nn.sigmoid(_dot_nt(wr_ref[...], hb))
    choice = scores + rb_ref[:, 0:1]
    per_group = N_EXPERTS // N_EXPERT_GROUPS
    gs = []
    for g in range(N_EXPERT_GROUPS):
        cg = choice[g * per_group:(g + 1) * per_group]
        m1 = jnp.max(cg, axis=0, keepdims=True)
        is_m = cg == m1
        n_m = jnp.sum(is_m.astype(F32), axis=0, keepdims=True)
        m2 = jnp.max(jnp.where(is_m, -jnp.inf, cg), axis=0, keepdims=True)
        gs.append(m1 + jnp.where(n_m > 1.5, m1, m2))
    gs = jnp.concatenate(gs, axis=0)
    gid = lax.broadcasted_iota(jnp.int32, gs.shape, 0)
    beaten = jnp.zeros(gs.shape, jnp.int32)
    for g in range(N_EXPERT_GROUPS):
        other = gs[g:g + 1]
        beaten = beaten + ((other > gs) | ((other == gs) & (g < gid))).astype(jnp.int32)
    keep_g = beaten < TOPK_EXPERT_GROUPS
    keep = jnp.concatenate([jnp.broadcast_to(keep_g[g:g + 1], (per_group, tm)) for g in range(N_EXPERT_GROUPS)],
                           axis=0)
    cand = jnp.where(keep, choice, -jnp.inf)
    eid = lax.broadcasted_iota(jnp.int32, cand.shape, 0)
    hits = []
    e_rows = []
    w_rows = []
    for _ in range(TOP_K):
        m = jnp.max(cand, axis=0, keepdims=True)
        idx = jnp.min(jnp.where(cand == m, eid, N_EXPERTS), axis=0, keepdims=True)
        hit = eid == idx
        hits.append(hit)
        e_rows.append(idx)
        w_rows.append(jnp.sum(jnp.where(hit, scores, 0.0), axis=0, keepdims=True))
        cand = jnp.where(hit, -jnp.inf, cand)
    w = jnp.concatenate(w_rows, axis=0)
    gate_ref[...] = w / jnp.sum(w, axis=0, keepdims=True) * ROUTED_SCALE
    eidx_ref[...] = jnp.concatenate(e_rows, axis=0)

    onehot = jnp.zeros(cand.shape, F32)
    for hit in hits:
        onehot = onehot + hit.astype(F32)
    before = _dot(onehot.astype(BF16), tri_ref[...]) + carry[:, 0:1]
    rank_ref[...] = jnp.concatenate(
        [jnp.sum(jnp.where(hit, before, 0.0), axis=0, keepdims=True) for hit in hits], axis=0).astype(jnp.int32)
    carry[...] = carry[...] + jnp.sum(onehot, axis=1, keepdims=True)
    cnt_ref[...] = carry[...]


def _out_projection(oa, ob, sg, x2, proj_a, proj_b, w_out, ln_g, ln_b, w_router, router_bias, s_gate, s_up, s_down):
    t = x2.shape[0]
    tm = OUT_TM
    pair_rows = lambda p: p.reshape(N_GROUPS, GROUP, HEAD_DIM, -1).transpose(1, 0, 2, 3).reshape(p.shape)
    pa = pair_rows(proj_a).astype(MXU_DTYPE)
    pb = pair_rows(proj_b).astype(MXU_DTYPE)
    tri = jnp.asarray(np.triu(np.ones((tm, tm), np.float32), 1), BF16)
    row = lambda i: (i, 0)
    fixed = lambda i: (0, 0)
    col = lambda i: (0, i)
    outs = pl.pallas_call(
        _outproj_kernel,
        grid=(t // tm,),
        in_specs=[pl.BlockSpec((tm, 4 * LANES), row), pl.BlockSpec((tm, 4 * LANES), row),
                  pl.BlockSpec((tm, 2 * D_MODEL), row), pl.BlockSpec((tm, D_MODEL), row),
                  pl.BlockSpec((4 * LANES, D_MODEL), fixed), pl.BlockSpec((4 * LANES, D_MODEL), fixed),
                  pl.BlockSpec((D_MODEL, D_MODEL), fixed),
                  pl.BlockSpec((1, D_MODEL), fixed), pl.BlockSpec((1, D_MODEL), fixed),
                  pl.BlockSpec((N_EXPERTS, D_MODEL), fixed), pl.BlockSpec((N_EXPERTS, LANES), fixed),
                  pl.BlockSpec((D_MODEL, 2 * SHARED_HIDDEN), fixed), pl.BlockSpec((SHARED_HIDDEN, D_MODEL), fixed),
                  pl.BlockSpec((tm, tm), fixed)],
        out_specs=[pl.BlockSpec((tm, D_MODEL // 2), row), pl.BlockSpec((tm, D_MODEL), row),
                   pl.BlockSpec((TOP_K, tm), col), pl.BlockSpec((TOP_K, tm), col), pl.BlockSpec((TOP_K, tm), col),
                   pl.BlockSpec((N_EXPERTS, LANES), fixed)],
        out_shape=[jax.ShapeDtypeStruct((t, D_MODEL // 2), jnp.uint32), jax.ShapeDtypeStruct((t, D_MODEL), F32),
                   jax.ShapeDtypeStruct((TOP_K, t), jnp.int32), jax.ShapeDtypeStruct((TOP_K, t), F32),
                   jax.ShapeDtypeStruct((TOP_K, t), jnp.int32), jax.ShapeDtypeStruct((N_EXPERTS, LANES), F32)],
        scratch_shapes=[pltpu.VMEM((N_EXPERTS, LANES), F32)],
        compiler_params=pltpu.CompilerParams(dimension_semantics=("arbitrary",), vmem_limit_bytes=VMEM_LIMIT),
        name="out_projection_router",
    )(oa, ob, sg, x2, pa, pb, w_out.astype(MXU_DTYPE), ln_g.reshape(1, -1), ln_b.reshape(1, -1),
      w_router.T.astype(MXU_DTYPE), jnp.broadcast_to(router_bias.astype(F32)[:, None], (N_EXPERTS, LANES)),
      jnp.concatenate([s_gate, s_up], axis=1).astype(MXU_DTYPE), s_down.astype(MXU_DTYPE), tri)
    return outs


def _rows_to_tiles(x):
    return pltpu.einshape("cml->mcl", jnp.stack(_lane_tiles(x), axis=0))


def _tiles_to_rows(x3):
    xt = pltpu.einshape("mcl->cml", x3)
    return jnp.concatenate([xt[c] for c in range(xt.shape[0])], axis=1)


def _dispatch_kernel(zstart_ref, cnt_ref, dest_ref, h2_ref, xs_ref, h_ref, zeros, sem, zsem):
    step = pl.program_id(0)
    tm = h2_ref.shape[0]
    slot = step % 2
    h_ref[slot] = _rows_to_tiles(h2_ref[...])

    @pl.when(step == 0)
    def _():
        zeros[...] = jnp.zeros(zeros.shape, zeros.dtype)

        def fill(e, c):
            @pl.when(cnt_ref[e] > 0)
            def _():
                pltpu.make_async_copy(zeros, xs_ref.at[pl.ds(zstart_ref[e], MOE_BM)], zsem).start()
            return c

        def fill_done(e, c):
            @pl.when(cnt_ref[e] > 0)
            def _():
                pltpu.make_async_copy(zeros, xs_ref.at[pl.ds(zstart_ref[e], MOE_BM)], zsem).wait()
            return c
        lax.fori_loop(0, N_EXPERTS, fill, 0)
        lax.fori_loop(0, N_EXPERTS, fill_done, 0)

    def issue(t, c):
        for k in range(TOP_K):
            pltpu.make_async_copy(h_ref.at[slot, t], xs_ref.at[dest_ref[k, t]], sem.at[slot]).start(priority=k % 2)
        return c
    lax.fori_loop(0, tm, issue, 0)

    def wait_tile(s):
        for k in range(TOP_K):
            pltpu.make_async_copy(h_ref.at[s], xs_ref.at[pl.ds(0, tm)], sem.at[s]).wait()

    @pl.when(step > 0)
    def _():
        wait_tile(1 - slot)

    @pl.when(step + 1 == pl.num_programs(0))
    def _():
        wait_tile(slot)


def _dispatch(h, dest, zstart, counts, n_rows):
    t = h.shape[0]
    tm = DISP_TM
    return pl.pallas_call(
        _dispatch_kernel,
        grid_spec=pltpu.PrefetchScalarGridSpec(
            num_scalar_prefetch=2,
            grid=(t // tm,),
            in_specs=[pl.BlockSpec((TOP_K, tm), lambda i, *_: (0, i), memory_space=pltpu.SMEM),
                      pl.BlockSpec((tm, D_MODEL // 2), lambda i, *_: (i, 0))],
            out_specs=pl.BlockSpec(memory_space=pl.ANY),
            scratch_shapes=[pltpu.VMEM((2, tm) + PACKED_ROW_TILE, jnp.uint32),
                            pltpu.VMEM((MOE_BM,) + PACKED_ROW_TILE, jnp.uint32),
                            pltpu.SemaphoreType.DMA((2,)), pltpu.SemaphoreType.DMA(())]),
        out_shape=jax.ShapeDtypeStruct((n_rows,) + PACKED_ROW_TILE, jnp.uint32),
        compiler_params=pltpu.CompilerParams(dimension_semantics=("arbitrary",), vmem_limit_bytes=VMEM_LIMIT),
        name="moe_dispatch",
    )(zstart, counts, dest, h)


def _experts_kernel(blk_e_ref, nused_ref, xs_ref, wg_ref, wu_ref, wd_ref, ys_ref, wg_s, wu_s, wd_s):
    b = pl.program_id(0)
    prev = blk_e_ref[jnp.maximum(b - 1, 0)]

    @pl.when((b == 0) | (blk_e_ref[b] != prev))
    def _():
        wg_s[...] = _mx(wg_ref[0])
        wu_s[...] = _mx(wu_ref[0])
        wd_s[...] = _mx(wd_ref[0])

    @pl.when(b < nused_ref[0])
    def _():
        xb = _mx(jnp.concatenate(_unpack_bf16_pairs(_tiles_to_rows(xs_ref[...])), axis=1))
        hid = jax.nn.silu(_dot(xb, wg_s[...])) * _dot(xb, wu_s[...])
        ys_ref[...] = _rows_to_tiles(_pack_bf16_pairs(_dot(_mx(hid), wd_s[...])))

    @pl.when(b >= nused_ref[0])
    def _():
        ys_ref[...] = jnp.zeros(ys_ref.shape, ys_ref.dtype)


def _experts(xs, blk_e, nused, e_gate, e_up, e_down):
    n_rows = xs.shape[0]
    n_blocks = n_rows // MOE_BM
    xmap = lambda b, be, nu: (jnp.minimum(b, nu[0] - 1), 0, 0)
    wmap = lambda b, be, nu: (be[b], 0, 0)
    return pl.pallas_call(
        _experts_kernel,
        grid_spec=pltpu.PrefetchScalarGridSpec(
            num_scalar_prefetch=2,
            grid=(n_blocks,),
            in_specs=[pl.BlockSpec((MOE_BM,) + PACKED_ROW_TILE, xmap),
                      pl.BlockSpec((1, D_MODEL, EXPERT_HIDDEN), wmap),
                      pl.BlockSpec((1, D_MODEL, EXPERT_HIDDEN), wmap),
                      pl.BlockSpec((1, EXPERT_HIDDEN, D_MODEL), wmap)],
            out_specs=pl.BlockSpec((MOE_BM,) + PACKED_ROW_TILE, lambda b, be, nu: (b, 0, 0)),
            scratch_shapes=[pltpu.VMEM((D_MODEL, EXPERT_HIDDEN), MXU_DTYPE),
                            pltpu.VMEM((D_MODEL, EXPERT_HIDDEN), MXU_DTYPE),
                            pltpu.VMEM((EXPERT_HIDDEN, D_MODEL), MXU_DTYPE)]),
        out_shape=jax.ShapeDtypeStruct((n_rows,) + PACKED_ROW_TILE, jnp.uint32),
        compiler_params=pltpu.CompilerParams(dimension_semantics=("arbitrary",), vmem_limit_bytes=VMEM_LIMIT),
        name="moe_experts",
    )(blk_e, nused, xs, e_gate, e_up, e_down)


def _combine_kernel(dest_ref, dest_next_ref, gate_ref, base_ref, g2_ref, b2_ref, ys_ref, out_ref, buf, routed, sem):
    step = pl.program_id(0)
    tm = base_ref.shape[0]
    slot = step % 2

    def gather_rows(d_ref, s, t):
        for k in range(TOP_K):
            j = t * TOP_K + k
            pltpu.make_async_copy(ys_ref.at[d_ref[j]], buf.at[s, j], sem.at[s]).start(priority=k % 2)

    def combine_token(t):
        low = jnp.zeros(PACKED_ROW_TILE, F32)
        high = jnp.zeros(PACKED_ROW_TILE, F32)
        for k in range(TOP_K):
            j = t * TOP_K + k
            lo_k, hi_k = _unpack_bf16_pairs(buf[slot, j])
            low = low + gate_ref[j] * lo_k
            high = high + gate_ref[j] * hi_k
        routed[t] = jnp.concatenate([low, high], axis=0)

    def for_tokens(body):
        def step_fn(t, c):
            body(t)
            return c
        lax.fori_loop(0, tm, step_fn, 0)

    @pl.when(step == 0)
    def _():
        for_tokens(lambda t: gather_rows(dest_ref, 0, t))

    pltpu.make_async_copy(ys_ref.at[pl.ds(0, tm * TOP_K)], buf.at[slot], sem.at[slot]).wait()

    @pl.when(step + 1 < pl.num_programs(0))
    def _():
        def both(t):
            gather_rows(dest_next_ref, 1 - slot, t)
            combine_token(t)
        for_tokens(both)

    @pl.when(step + 1 == pl.num_programs(0))
    def _():
        for_tokens(combine_token)

    out_ref[...] = _layer_norm(base_ref[...] + _tiles_to_rows(routed[...]), g2_ref[...], b2_ref[...])


def _combine(ys3, dest, gate, base, ln_g, ln_b):
    t = base.shape[0]
    tm = COMB_TM
    n_tiles = t // tm
    dest_tk = dest.T.reshape(-1)
    return pl.pallas_call(
        _combine_kernel,
        grid=(n_tiles,),
        in_specs=[pl.BlockSpec((tm * TOP_K,), lambda i: (i,), memory_space=pltpu.SMEM),
                  pl.BlockSpec((tm * TOP_K,), lambda i: (jnp.minimum(i + 1, n_tiles - 1),), memory_space=pltpu.SMEM),
                  pl.BlockSpec((tm * TOP_K,), lambda i: (i,), memory_space=pltpu.SMEM),
                  pl.BlockSpec((tm, D_MODEL), lambda i: (i, 0)),
                  pl.BlockSpec((1, D_MODEL), lambda i: (0, 0)),
                  pl.BlockSpec((1, D_MODEL), lambda i: (0, 0)),
                  pl.BlockSpec(memory_space=pl.ANY)],
        out_specs=pl.BlockSpec((tm, D_MODEL), lambda i: (i, 0)),
        out_shape=jax.ShapeDtypeStruct((t, D_MODEL), F32),
        scratch_shapes=[pltpu.VMEM((2, tm * TOP_K) + PACKED_ROW_TILE, jnp.uint32), pltpu.VMEM((tm,) + ROW_TILE, F32),
                        pltpu.SemaphoreType.DMA((2,))],
        compiler_params=pltpu.CompilerParams(dimension_semantics=("arbitrary",), vmem_limit_bytes=VMEM_LIMIT),
        name="moe_combine",
    )(dest_tk, dest_tk, gate.T.reshape(-1), base, ln_g.reshape(1, -1), ln_b.reshape(1, -1), ys3)


def _dest_kernel(pstart_ref, eidx_ref, rank_ref, dest_ref):
    eidx = eidx_ref[...]

    unroll = 8

    def body(g, dest):
        for u in range(unroll):
            e = g * unroll + u
            dest = dest + jnp.where(eidx == e, pstart_ref[e], 0)
        return dest
    dest_ref[...] = lax.fori_loop(0, N_EXPERTS // unroll, body, rank_ref[...])


def _dest_rows(pstarts, eidx, rank):
    t = eidx.shape[1]
    tl = 2048
    spec = pl.BlockSpec((TOP_K, tl), lambda i, *_: (0, i))
    return pl.pallas_call(
        _dest_kernel,
        grid_spec=pltpu.PrefetchScalarGridSpec(num_scalar_prefetch=1, grid=(t // tl,), in_specs=[spec, spec],
                                               out_specs=spec),
        out_shape=jax.ShapeDtypeStruct(eidx.shape, jnp.int32),
        compiler_params=pltpu.CompilerParams(dimension_semantics=("arbitrary",)),
        name="moe_dest_rows",
    )(pstarts, eidx, rank)


def _moe_layout(eidx, rank, counts):
    n_assign = eidx.size
    n_blocks = (n_assign + N_EXPERTS * (MOE_BM - 1)) // MOE_BM
    padded = (counts + MOE_BM - 1) // MOE_BM * MOE_BM
    pends = jnp.cumsum(padded)
    pstarts = (pends - padded).astype(jnp.int32)
    dest = _dest_rows(pstarts, eidx, rank)
    block_row = jnp.arange(n_blocks, dtype=jnp.int32) * MOE_BM
    blk_e = jnp.minimum(jnp.sum(pends[None, :] <= block_row[:, None], axis=1), N_EXPERTS - 1).astype(jnp.int32)
    nused = (pends[-1:] // MOE_BM).astype(jnp.int32)
    zstart = jnp.maximum(pends - MOE_BM, 0).astype(jnp.int32)
    return dest.astype(jnp.int32), blk_e, nused, zstart, n_blocks * MOE_BM


def _layer(x, w_in, cmp_pe, cmp_w1, cmp_b1, cmp_w2, sinks, bias_table, proj_a, proj_b, w_out, ln1_g, ln1_b,
           w_router, router_bias, e_gate, e_up, e_down, s_gate, s_up, s_down, ln2_g, ln2_b):
    bsz, seq, d = x.shape
    x2 = x.reshape(bsz * seq, d)
    proj = _in_projection(x2, w_in)
    kvcmp = _compress(proj['kc'], proj['vc'], bsz, seq, cmp_pe, cmp_w1, cmp_b1, cmp_w2)
    oa, ob = _attention(proj, kvcmp, sinks, bias_table, bsz, seq)
    h, base, eidx, gate, rank, cnt = _out_projection(oa, ob, proj['sg'], x2, proj_a, proj_b, w_out, ln1_g, ln1_b,
                                                     w_router, router_bias, s_gate, s_up, s_down)
    counts = cnt[:, 0].astype(jnp.int32)
    dest, blk_e, nused, zstart, n_rows = _moe_layout(eidx, rank, counts)
    xs = _dispatch(h, dest, zstart, counts, n_rows)
    ys = _experts(xs, blk_e, nused, e_gate, e_up, e_down)
    out = _combine(ys, dest, gate, base, ln2_g, ln2_b)
    return out.reshape(bsz, seq, d)


def kernel(x, w_in, cmp_pe, cmp_w1, cmp_b1, cmp_w2, attn_sinks, rel_bias_table, proj_a, proj_b, w_out, ln1_g, ln1_b,
           w_router, router_bias, expert_w_gate, expert_w_up, expert_w_down, shared_w_gate, shared_w_up,
           shared_w_down, ln2_g, ln2_b):
    h = x
    for l in range(DEPTH):
        h = _layer(h, w_in[l], cmp_pe[l], cmp_w1[l], cmp_b1[l], cmp_w2[l], attn_sinks[l], rel_bias_table, proj_a[l],
                   proj_b[l], w_out[l], ln1_g[l], ln1_b[l], w_router[l], router_bias[l], expert_w_gate[l],
                   expert_w_up[l], expert_w_down[l], shared_w_gate[l], shared_w_up[l], shared_w_down[l], ln2_g[l],
                   ln2_b[l])
    return h
```

```python
import functools
import math

import numpy as np
import jax
import jax.numpy as jnp
from jax import lax
from jax.experimental import pallas as pl
from jax.experimental.pallas import tpu as pltpu

F32 = jnp.float32
BF16 = jnp.bfloat16
MXU_DTYPE = jnp.bfloat16

D_MODEL = 1024
HEAD_DIM = 64
ATTN_SCALE = HEAD_DIM ** -0.5
LOG2E = math.log2(math.e)
Q_BLOCK = 128
N_HEADS = 8
N_GROUPS = 2
GROUP = 4
CMP_BLOCK = 32
CMP_STRIDE = 16
CMP_HIDDEN = 128
SEL_BLOCK = 64
SEL_TOP_N = 8
SEL_INIT_BLOCKS = 1
SEL_LOCAL_BLOCKS = 2
NSA_WINDOW = 512
SWA_WINDOW = 128
REL_BUCKETS = 32
REL_MAX_DIST = 128
N_EXPERTS = 256
TOP_K = 8
EXPERT_HIDDEN = 256
SHARED_HIDDEN = 256
N_EXPERT_GROUPS = 8
TOPK_EXPERT_GROUPS = 4
ROUTED_SCALE = 2.5
LN_EPS = 1e-5
DEPTH = 1
DN_ALPHA = (2 * DEPTH) ** 0.25

NEG = -1e30
LANES = 128
ROW_TILE = (8, LANES)
PACKED_ROW_TILE = (4, LANES)
CMP_FRONT = 16
CMP_NEAR = LANES
SEL_CHUNK = 1024
QB_PER_STEP = 1
VMEM_LIMIT = 56 * 1024 * 1024

IN_TM = 512
OUT_TM = 512
MOE_BM = 512
DISP_TM = 256
COMB_TM = 256


def _dot(a, b):
    return jnp.dot(a, b, preferred_element_type=F32)


def _dot_nt(a, b):
    return lax.dot_general(a, b, (((1,), (1,)), ((), ())), preferred_element_type=F32)


def _mx(a):
    return a.astype(MXU_DTYPE)


def _pack_bf16_pairs(x):
    half = x.shape[1] // 2
    bits = lax.bitcast_convert_type(x.astype(BF16).astype(F32), jnp.uint32)
    return (bits[:, half:] & jnp.uint32(0xFFFF0000)) | (bits[:, :half] >> 16)


def _unpack_bf16_pairs(words):
    return (lax.bitcast_convert_type(words << 16, F32),
            lax.bitcast_convert_type(words & jnp.uint32(0xFFFF0000), F32))


_IN_COLS = (('qa', 512), ('qb', 512), ('kc', 128), ('vc', 128), ('ks', 128), ('vs', 128), ('kw', 128),
            ('vw', 128), ('kb', 128), ('vb', 128), ('ga', 128), ('sg', 2048))


def _inproj_kernel(x_ref, w_ref, qa_ref, qb_ref, kc_ref, vc_ref, ks_ref, vs_ref, kw_ref, vw_ref, kb_ref, vb_ref,
                   ga_ref, sg_ref):
    xb = _mx(x_ref[...])
    outs = dict(qa=qa_ref, qb=qb_ref, kc=kc_ref, vc=vc_ref, ks=ks_ref, vs=vs_ref, kw=kw_ref, vw=vw_ref,
                kb=kb_ref, vb=vb_ref, ga=ga_ref, sg=sg_ref)
    tiles = [(name, c) for name, width in _IN_COLS for c in range(0, width, LANES)]
    chunk = 4
    for t0 in range(0, len(tiles), chunk):
        group = tiles[t0:t0 + chunk]
        y = _dot(xb, w_ref[:, t0 * LANES:(t0 + len(group)) * LANES])
        for j, (name, c) in enumerate(group):
            yj = y[:, j * LANES:(j + 1) * LANES]
            if name in ('ga', 'sg'):
                yj = jax.nn.sigmoid(yj)
            outs[name][:, c:c + LANES] = yj.astype(outs[name].dtype)


def _pair_head_columns(w):
    return w.reshape(w.shape[0], N_GROUPS, GROUP, HEAD_DIM).transpose(0, 2, 1, 3).reshape(w.shape[0], -1)


def _in_projection(x2, w_in):
    t = x2.shape[0]
    sizes = (512, 128, 128, 128, 128, 128, 128, 24, 512, 128, 128, 1024, 1024)
    offs = np.cumsum((0,) + sizes)
    part = [w_in[:, offs[k]:offs[k + 1]] for k in range(len(sizes))]
    w_qa, w_kc, w_vc, w_ks, w_vs, w_kw, w_vw, w_g, w_qb, w_kb, w_vb, w_gate_a, w_gate_b = part
    w_qa = _pair_head_columns(w_qa) * (ATTN_SCALE * LOG2E)
    w_qb = _pair_head_columns(w_qb) * (ATTN_SCALE * LOG2E)
    w_ga = w_g.reshape(-1, N_GROUPS, GROUP, 3).transpose(0, 3, 2, 1).reshape(-1, 24)
    w_ga = jnp.pad(w_ga, ((0, 0), (0, LANES - 24)))
    w_all = jnp.concatenate([w_qa, w_qb, w_kc, w_vc, w_ks, w_vs, w_kw, w_vw, w_kb, w_vb, w_ga, w_gate_a, w_gate_b],
                            axis=1).astype(MXU_DTYPE)
    n_all = w_all.shape[1]
    out_shape = []
    out_specs = []
    for name, width in _IN_COLS:
        dt = F32 if name == 'ga' else BF16
        out_shape.append(jax.ShapeDtypeStruct((t, width), dt))
        out_specs.append(pl.BlockSpec((IN_TM, width), lambda i: (i, 0)))
    outs = pl.pallas_call(
        _inproj_kernel,
        grid=(t // IN_TM,),
        in_specs=[pl.BlockSpec((IN_TM, D_MODEL), lambda i: (i, 0)),
                  pl.BlockSpec((D_MODEL, n_all), lambda i: (0, 0))],
        out_specs=out_specs,
        out_shape=out_shape,
        compiler_params=pltpu.CompilerParams(dimension_semantics=("arbitrary",), vmem_limit_bytes=VMEM_LIMIT),
        name="in_projection",
    )(x2, w_all)
    return dict(zip([n for n, _ in _IN_COLS], outs))


def _compress_kernel(tok_ref, w1_ref, pe_ref, w1o_ref, b1_ref, w2_ref, out_ref):
    n_chunks = tok_ref.shape[2]
    ab = _dot(tok_ref[0, 0], w1_ref[0])
    a = ab[:, :2 * CMP_HIDDEN]
    b_next = pltpu.roll(ab[:, 2 * CMP_HIDDEN:], n_chunks - 1, 0)
    cb = _dot(_mx(pe_ref[0]), _mx(w1o_ref[0]))[0:1, :] + b1_ref[0]
    cb2 = jnp.concatenate([cb, cb], axis=1)
    hid = jax.nn.gelu(a + b_next + cb2)
    out = _dot(_mx(hid), w2_ref[0])
    row = lax.broadcasted_iota(jnp.int32, out.shape, 0)
    out = jnp.where(row < n_chunks - 1, out, 0.0)
    out_ref[0, 0, 0:CMP_FRONT, :] = jnp.zeros((CMP_FRONT, LANES), F32)
    out_ref[0, 0, CMP_FRONT:CMP_FRONT + n_chunks, :] = out
    out_ref[0, 0, CMP_FRONT + n_chunks:, :] = jnp.zeros((CMP_NEAR - CMP_FRONT, LANES), F32)


def _compress(kc, vc, bsz, seq, cmp_pe, cmp_w1, cmp_b1, cmp_w2):
    n_chunks = seq // CMP_STRIDE
    tok = jnp.stack([kc, vc]).reshape(2, bsz, n_chunks, CMP_STRIDE * LANES)
    eye = jnp.eye(N_GROUPS, dtype=F32)
    w1r = cmp_w1.reshape(2, 2, CMP_STRIDE, HEAD_DIM, CMP_HIDDEN)
    w1 = jnp.einsum('khjdn,gG->kjgdhGn', w1r, eye).reshape(2, CMP_STRIDE * LANES, 4 * CMP_HIDDEN).astype(MXU_DTYPE)
    w2 = jnp.einsum('knd,gG->kgnGd', cmp_w2, eye).reshape(2, 2 * CMP_HIDDEN, LANES).astype(MXU_DTYPE)
    pe = jnp.pad(cmp_pe.reshape(2, 1, CMP_BLOCK * HEAD_DIM), ((0, 0), (0, 7), (0, 0)))
    b1 = cmp_b1.reshape(2, 1, CMP_HIDDEN)
    rows = CMP_FRONT + n_chunks + CMP_NEAR - CMP_FRONT
    return pl.pallas_call(
        _compress_kernel,
        grid=(2, bsz),
        in_specs=[pl.BlockSpec((1, 1, n_chunks, CMP_STRIDE * LANES), lambda k, b: (k, b, 0, 0)),
                  pl.BlockSpec((1, CMP_STRIDE * LANES, 4 * CMP_HIDDEN), lambda k, b: (k, 0, 0)),
                  pl.BlockSpec((1, 8, CMP_BLOCK * HEAD_DIM), lambda k, b: (k, 0, 0)),
                  pl.BlockSpec((1, CMP_BLOCK * HEAD_DIM, CMP_HIDDEN), lambda k, b: (k, 0, 0)),
                  pl.BlockSpec((1, 1, CMP_HIDDEN), lambda k, b: (k, 0, 0)),
                  pl.BlockSpec((1, 2 * CMP_HIDDEN, LANES), lambda k, b: (k, 0, 0))],
        out_specs=pl.BlockSpec((1, 1, rows, LANES), lambda k, b: (k, b, 0, 0)),
        out_shape=jax.ShapeDtypeStruct((2, bsz, rows, LANES), F32),
        compiler_params=pltpu.CompilerParams(dimension_semantics=("arbitrary", "arbitrary"),
                                             vmem_limit_bytes=VMEM_LIMIT),
        name="nsa_compress",
    )(tok, w1, pe, cmp_w1, b1, w2)


def _stack_heads(q_ref, dst):
    lo = lax.broadcasted_iota(jnp.int32, (Q_BLOCK, LANES), 1) < HEAD_DIM
    for r in range(GROUP):
        qr = q_ref[:, r * LANES:(r + 1) * LANES].astype(dst.dtype)
        z = jnp.zeros_like(qr)
        dst[(2 * r) * Q_BLOCK:(2 * r + 1) * Q_BLOCK, :] = jnp.where(lo, qr, z)
        dst[(2 * r + 1) * Q_BLOCK:(2 * r + 2) * Q_BLOCK, :] = jnp.where(lo, z, qr)


def _pair_heads(o, r):
    lo = lax.broadcasted_iota(jnp.int32, (Q_BLOCK, LANES), 1) < HEAD_DIM
    return jnp.where(lo, o[(2 * r) * Q_BLOCK:(2 * r + 1) * Q_BLOCK], o[(2 * r + 1) * Q_BLOCK:(2 * r + 2) * Q_BLOCK])


def _lane_tiles(x):
    return [x[:, t * LANES:(t + 1) * LANES] for t in range(x.shape[1] // LANES)]


def _row_max(tiles):
    mx = tiles[0]
    for t in tiles[1:]:
        mx = jnp.maximum(mx, t)
    return jnp.broadcast_to(jnp.max(mx, axis=1, keepdims=True), mx.shape)


def _with_ones(v):
    return jnp.concatenate([v, jnp.ones(v.shape, v.dtype)], axis=1)


def _block_of_key(n_keys, first_block):
    b = lax.broadcasted_iota(jnp.int32, (LANES, n_keys), 0)
    k = lax.broadcasted_iota(jnp.int32, (LANES, n_keys), 1)
    return (b == (k // SEL_BLOCK) + first_block).astype(MXU_DTYPE)


def _select_blocks_t(imp_t, i, n_top):
    blk = lax.broadcasted_iota(jnp.int32, imp_t.shape, 0)
    qcol = lax.broadcasted_iota(jnp.int32, imp_t.shape, 1)
    back = (2 * i + (qcol >= SEL_BLOCK).astype(jnp.int32)) - blk
    sel = (back >= 0) & ((blk < SEL_INIT_BLOCKS) | (back < SEL_LOCAL_BLOCKS))
    cand = jnp.where((back >= SEL_LOCAL_BLOCKS) & (blk >= SEL_INIT_BLOCKS), imp_t, -1.0)
    blk_f = blk.astype(F32)
    for _ in range(n_top - SEL_INIT_BLOCKS - SEL_LOCAL_BLOCKS):
        m = jnp.max(cand, axis=0, keepdims=True)
        idx = jnp.min(jnp.where(cand == m, blk_f, float(LANES)), axis=0, keepdims=True)
        hit = blk_f == idx
        sel = sel | (hit & (m >= 0.0))
        cand = jnp.where(hit, -2.0, cand)
    return sel


def _query_block(i, sink_ref, qa_ref, qb_ref, ga_ref, kcmp_ref, vcmp_ref, ks_ref, vs_ref, kw_ref, vw_ref, kb_ref,
                 vb_ref, cmat_ref, tnear_ref, tsel_ref, twin_ref, tswa_ref, oa_ref, ob_ref,
                 qall, qball, mneg, mneg_far, m_s, acc_s, s_buf, oa_acc, n_far, n_top):
    rows = N_HEADS * Q_BLOCK
    half = rows // 2
    halves = (slice(0, half), slice(half, rows))
    _stack_heads(qa_ref, qall)
    _stack_heads(qb_ref, qball)
    nstart = pl.multiple_of(i * Q_BLOCK, Q_BLOCK)
    lo = lax.broadcasted_iota(jnp.int32, (Q_BLOCK, LANES), 1) < HEAD_DIM
    gates = ga_ref[...]

    def gate_tile(c, r):
        return jnp.where(lo, gates[:, c * 8 + 2 * r:c * 8 + 2 * r + 1], gates[:, c * 8 + 2 * r + 1:c * 8 + 2 * r + 2])

    def softmax_pv(s_tiles, v1, fix_max=None):
        m = _row_max(s_tiles)
        if fix_max is not None:
            m = fix_max(m)
        e = [jnp.exp2(t - m) for t in s_tiles]
        return e, m, _dot(_mx(jnp.concatenate(e, axis=1)), v1)

    off = pl.multiple_of(i * (Q_BLOCK // CMP_STRIDE), 8)
    k_cmp = _mx(jnp.concatenate([kcmp_ref[0, 0, 0:n_far, :], kcmp_ref[0, 0, pl.ds(off, CMP_NEAR), :]], axis=0))
    v_cmp = _with_ones(_mx(jnp.concatenate([vcmp_ref[0, 0, 0:n_far, :], vcmp_ref[0, 0, pl.ds(off, CMP_NEAR), :]],
                                           axis=0)))
    colf = lax.broadcasted_iota(jnp.int32, (1, n_far), 1)
    coln = lax.broadcasted_iota(jnp.int32, (1, CMP_NEAR), 1)
    col_ok = jnp.concatenate([(colf >= CMP_FRONT) & (colf < off), coln + off >= CMP_FRONT], axis=1)
    mask_c = jnp.where(col_ok, 0.0, NEG)
    no_key = lambda m: jnp.where(m > 0.5 * NEG, m, 0.0)
    p_cmp, o_c = [], []
    for rs in halves:
        tiles = _lane_tiles(_dot_nt(qall[rs, :], k_cmp) + mask_c)
        tiles[-1] = tiles[-1] + tnear_ref[rs, :]
        e, _, ov = softmax_pv(tiles, v_cmp, no_key)
        inv = 1.0 / jnp.maximum(ov[:, LANES:], 1e-30)
        o_c.append(ov[:, :LANES] * inv)
        p_cmp.append([t * inv for t in e])
    o_c = jnp.concatenate(o_c, axis=0)
    yield None

    def far_start(j):
        return pl.multiple_of(Q_BLOCK + j * SEL_CHUNK, Q_BLOCK)

    def far_logits(j, slot, masked):
        kc = _mx(ks_ref[0, pl.ds(far_start(j), SEL_CHUNK), :])
        if masked:
            madd = _dot(mneg_far[...], _block_of_key(SEL_CHUNK, j * (SEL_CHUNK // SEL_BLOCK)))
        for rs in halves:
            s = _dot_nt(qall[rs, :], kc)
            s_buf[slot, rs, :] = s + jnp.concatenate([madd] * (GROUP // 2), axis=0) if masked else s

    far_logits(0, 0, False)

    blkcol = lax.broadcasted_iota(jnp.int32, (Q_BLOCK, LANES), 1)
    n_tiles = len(p_cmp[0])
    for g in range(N_GROUPS):
        imp = jnp.zeros((Q_BLOCK, LANES), F32)
        for t in range(n_tiles):
            pg = sum(p_cmp[r // 2][t][(2 * (r % 2) + g) * Q_BLOCK:(2 * (r % 2) + g + 1) * Q_BLOCK]
                     for r in range(GROUP))
            if t < n_tiles - 1:
                cm = _mx(cmat_ref[t * LANES:(t + 1) * LANES, :])
            else:
                cm = _mx(cmat_ref[pl.ds(off, CMP_NEAR), :])
            hi = _mx(pg)
            low = _mx(pg - hi.astype(F32))
            imp = imp + _dot(hi, cm) + _dot(low, cm)
        sel = _select_blocks_t(imp.T, i, n_top)
        neg = jnp.where(sel, 0.0, NEG).T
        mneg[g * Q_BLOCK:(g + 1) * Q_BLOCK, :] = neg.astype(mneg.dtype)
        mneg_far[g * Q_BLOCK:(g + 1) * Q_BLOCK, :] = jnp.where(blkcol < 2 * (i - 1), neg, NEG).astype(mneg.dtype)

    yield None

    wpad = kw_ref.shape[1] - ks_ref.shape[1] + Q_BLOCK
    kwin = _mx(kw_ref[0, pl.ds(nstart, wpad + Q_BLOCK), :])
    vwin = _with_ones(_mx(vw_ref[0, pl.ds(nstart, wpad + Q_BLOCK), :]))
    colw = lax.broadcasted_iota(jnp.int32, (1, wpad + Q_BLOCK), 1)
    mask_w = jnp.where(colw + nstart >= wpad, 0.0, NEG)
    o_w = []
    for rs in halves:
        _, _, ov = softmax_pv(_lane_tiles(_dot_nt(qall[rs, :], kwin) + twin_ref[rs, :] + mask_w), vwin)
        o_w.append(ov[:, :LANES] / ov[:, LANES:])
    o_w = jnp.concatenate(o_w, axis=0)
    for r in range(GROUP):
        oa_acc[:, r * LANES:(r + 1) * LANES] = (gate_tile(0, r) * _pair_heads(o_c, r)
                                                + gate_tile(2, r) * _pair_heads(o_w, r))

    yield None

    bpad = kb_ref.shape[1] - ks_ref.shape[1] + Q_BLOCK
    kwin = _mx(kb_ref[0, pl.ds(nstart, bpad + Q_BLOCK), :])
    vwin = _with_ones(_mx(vb_ref[0, pl.ds(nstart, bpad + Q_BLOCK), :]))
    colb = lax.broadcasted_iota(jnp.int32, (1, bpad + Q_BLOCK), 1)
    mask_b = jnp.where(colb + nstart >= bpad, 0.0, NEG)
    o_b = []
    for hh, rs in enumerate(halves):
        sink = jnp.concatenate([jnp.full((Q_BLOCK, LANES), sink_ref[(h % 2) * GROUP + h // 2], F32)
                                for h in range(hh * N_HEADS // 2, (hh + 1) * N_HEADS // 2)], axis=0)
        _, m, ov = softmax_pv(_lane_tiles(_dot_nt(qball[rs, :], kwin) + tswa_ref[rs, :] + mask_b), vwin,
                              lambda m: jnp.maximum(m, sink))
        o_b.append(ov[:, :LANES] / (ov[:, LANES:] + jnp.exp2(sink - m)))
    o_b = jnp.concatenate(o_b, axis=0)
    for r in range(GROUP):
        ob_ref[:, r * LANES:(r + 1) * LANES] = _pair_heads(o_b, r).astype(ob_ref.dtype)

    yield None

    m_s[...] = jnp.full(m_s.shape, NEG, F32)
    acc_s[...] = jnp.zeros(acc_s.shape, F32)

    def flash_update(rs, s, v1):
        s_tiles = _lane_tiles(s)
        m_old = m_s[rs, :]
        m_new = jnp.maximum(m_old, _row_max(s_tiles))
        alpha = jnp.exp2(m_old - m_new)
        p = jnp.concatenate([jnp.exp2(t - m_new) for t in s_tiles], axis=1)
        acc_s[rs, :] = jnp.concatenate([alpha, alpha], axis=1) * acc_s[rs, :] + _dot(_mx(p), v1)
        m_s[rs, :] = m_new

    madd = _dot(mneg_far[...], _block_of_key(SEL_CHUNK, 0))
    for rs in halves:
        s_buf[0, rs, :] = s_buf[0, rs, :] + jnp.concatenate([madd] * (GROUP // 2), axis=0)

    def far_update(j):
        v1 = _with_ones(_mx(vs_ref[0, pl.ds(far_start(j), SEL_CHUNK), :]))
        for rs in halves:
            flash_update(rs, s_buf[j % 2, rs, :], v1)

    yield far_update, far_logits

    kc = _mx(ks_ref[0, pl.ds(nstart, 2 * Q_BLOCK), :])
    v1 = _with_ones(_mx(vs_ref[0, pl.ds(nstart, 2 * Q_BLOCK), :]))
    madd = _dot(mneg[...], _block_of_key(2 * Q_BLOCK, 2 * (i - 1)))
    col2 = lax.broadcasted_iota(jnp.int32, (1, 2 * Q_BLOCK), 1)
    mask_n = jnp.where((col2 < Q_BLOCK) & (i == 0), NEG, 0.0)
    for rs in halves:
        s = _dot_nt(qall[rs, :], kc) + jnp.concatenate([madd] * (GROUP // 2), axis=0) + tsel_ref[rs, :] + mask_n
        flash_update(rs, s, v1)
    acc = acc_s[...]
    o_s = acc[:, :LANES] / acc[:, LANES:]
    for r in range(GROUP):
        tile = oa_acc[:, r * LANES:(r + 1) * LANES] + gate_tile(1, r) * _pair_heads(o_s, r)
        oa_ref[:, r * LANES:(r + 1) * LANES] = tile.astype(oa_ref.dtype)
    yield None


def _attn_kernel(sink_ref, qa_ref, qb_ref, ga_ref, kcmp_ref, vcmp_ref, ks_ref, vs_ref, kw_ref, vw_ref, kb_ref,
                 vb_ref, cmat_ref, tnear_ref, tsel_ref, twin_ref, tswa_ref, oa_ref, ob_ref,
                 qall, qball, mneg, mneg_far, m_s, acc_s, s_buf, oa_acc, *, n_far, n_top):
    first = pl.program_id(1) * QB_PER_STEP
    blocks, steps = [], []
    for n in range(QB_PER_STEP):
        qrows = pl.ds(n * Q_BLOCK, Q_BLOCK)
        blk = _query_block(first + n, sink_ref, qa_ref.at[qrows, :], qb_ref.at[qrows, :], ga_ref.at[qrows, :],
                           kcmp_ref, vcmp_ref, ks_ref, vs_ref, kw_ref, vw_ref, kb_ref, vb_ref, cmat_ref, tnear_ref,
                           tsel_ref, twin_ref, tswa_ref, oa_ref.at[qrows, :], ob_ref.at[qrows, :],
                           qall.at[n], qball.at[n], mneg.at[n], mneg_far.at[n], m_s.at[n], acc_s.at[n], s_buf.at[n],
                           oa_acc.at[n], n_far, n_top)
        blocks.append(blk)
    steps = [next(blk) for blk in blocks]
    while steps[0] is None:
        steps = [next(blk) for blk in blocks]
    n_far_keys = jnp.maximum(first + QB_PER_STEP - 2, 0) * Q_BLOCK
    n_chunks = (n_far_keys + SEL_CHUNK - 1) // SEL_CHUNK

    def far_body(j, carry):
        for far_update, _ in steps:
            far_update(j)
        for _, far_logits in steps:
            far_logits(j + 1, (j + 1) % 2, True)
        return carry

    last = jnp.maximum(n_chunks - 1, 0)
    lax.fori_loop(0, last, far_body, 0)
    for far_update, _ in steps:
        far_update(last)
    for blk in blocks:
        next(blk)


def _rel_bucket_np(dist):
    n = np.maximum(dist, 0)
    max_exact = REL_BUCKETS // 2
    nf = np.maximum(n, 1).astype(np.float32)
    log_b = max_exact + (np.log(nf / max_exact) / math.log(REL_MAX_DIST / max_exact)
                         * (REL_BUCKETS - max_exact)).astype(np.int32)
    log_b = np.minimum(log_b, REL_BUCKETS - 1)
    return np.where(n < max_exact, n, log_b)


def _toeplitz_bias(tab, pad, width, window, shift_far):
    length = width + Q_BLOCK
    dist = pad + Q_BLOCK - 1 - np.arange(length)
    onehot = np.zeros((length, REL_BUCKETS), np.float32)
    onehot[np.arange(length), _rel_bucket_np(dist)] = 1.0
    vals = jnp.dot(jnp.asarray(onehot), tab, precision=lax.Precision.HIGHEST)
    if shift_far:
        vals = vals - tab[REL_BUCKETS - 1][None, :]
    vals = vals * LOG2E
    valid = (dist >= 0) & (dist < window)
    vals = jnp.where(jnp.asarray(valid)[:, None], vals, NEG).T
    skew = jnp.tile(vals, (1, Q_BLOCK))[:, :Q_BLOCK * (length - 1)].reshape(N_HEADS, Q_BLOCK, length - 1)
    return skew[:, :, Q_BLOCK - 1:Q_BLOCK - 1 + width].reshape(N_HEADS * Q_BLOCK, width).astype(F32)


def _attention(proj, kvcmp, sinks, bias_table, bsz, seq):
    assert seq % SEL_CHUNK == 0
    nq = seq // Q_BLOCK
    n_far = seq // CMP_STRIDE
    n_sel = seq // SEL_BLOCK
    n_top = min(SEL_TOP_N, n_sel)
    assert n_top >= SEL_INIT_BLOCKS + SEL_LOCAL_BLOCKS and n_sel <= LANES
    wpad = Q_BLOCK * (-(-(NSA_WINDOW - 1) // Q_BLOCK))
    bpad = Q_BLOCK * (-(-(SWA_WINDOW - 1) // Q_BLOCK))
    pair = lambda tab: tab.astype(F32).reshape(REL_BUCKETS, N_GROUPS, GROUP).transpose(0, 2, 1).reshape(REL_BUCKETS, -1)
    tab_a = pair(bias_table[:, :N_HEADS])
    tab_b = pair(bias_table[:, N_HEADS:])
    near_pad = CMP_STRIDE * CMP_FRONT - (CMP_BLOCK - 1)
    t_near = _toeplitz_bias(tab_a, near_pad, CMP_STRIDE * CMP_NEAR, 1 << 30, True)[:, ::CMP_STRIDE]
    t_sel = _toeplitz_bias(tab_a, Q_BLOCK, 2 * Q_BLOCK, 1 << 30, True)
    t_win = _toeplitz_bias(tab_a, wpad, wpad + Q_BLOCK, NSA_WINDOW, False)
    t_swa = _toeplitz_bias(tab_b, bpad, bpad + Q_BLOCK, SWA_WINDOW, False)
    n_rows = kvcmp.shape[2]
    cn = (np.arange(n_rows) - CMP_FRONT)[:, None] * CMP_STRIDE
    sj = np.arange(LANES)[None, :] * SEL_BLOCK
    cmat = ((cn < sj + SEL_BLOCK) & (cn + CMP_BLOCK > sj) & (cn >= 0) & (cn + CMP_BLOCK <= seq)
            & (sj < seq)).astype(np.float32)
    cmat = jnp.asarray(cmat, F32)
    padded = lambda name, p: jnp.pad(proj[name].reshape(bsz, seq, LANES), ((0, 0), (p, 0), (0, 0)))
    ks, vs = padded('ks', Q_BLOCK), padded('vs', Q_BLOCK)
    kw, vw = padded('kw', wpad), padded('vw', wpad)
    kb, vb = padded('kb', bpad), padded('vb', bpad)
    rows = N_HEADS * Q_BLOCK
    n_steps = nq // QB_PER_STEP
    qspec = pl.BlockSpec((QB_PER_STEP * Q_BLOCK, 4 * LANES), lambda b, i: (b * n_steps + i, 0))
    const2 = lambda shape: pl.BlockSpec(shape, lambda b, i: (0, 0))
    batch3 = lambda n: pl.BlockSpec((1, n, LANES), lambda b, i: (b, 0, 0))
    per_block = lambda shape, dtype: pltpu.VMEM((QB_PER_STEP,) + shape, dtype)
    kernel = functools.partial(_attn_kernel, n_far=n_far, n_top=n_top)
    return pl.pallas_call(
        kernel,
        grid=(bsz, n_steps),
        in_specs=[pl.BlockSpec(memory_space=pltpu.SMEM),
                  qspec, qspec,
                  pl.BlockSpec((QB_PER_STEP * Q_BLOCK, LANES), lambda b, i: (b * n_steps + i, 0)),
                  pl.BlockSpec((1, 1, n_rows, LANES), lambda b, i: (0, b, 0, 0)),
                  pl.BlockSpec((1, 1, n_rows, LANES), lambda b, i: (1, b, 0, 0)),
                  batch3(seq + Q_BLOCK), batch3(seq + Q_BLOCK),
                  batch3(seq + wpad), batch3(seq + wpad),
                  batch3(seq + bpad), batch3(seq + bpad),
                  const2((n_rows, LANES)),
                  const2((rows, CMP_NEAR)),
                  const2((rows, 2 * Q_BLOCK)),
                  const2((rows, wpad + Q_BLOCK)),
                  const2((rows, bpad + Q_BLOCK))],
        out_specs=[qspec, qspec],
        out_shape=[jax.ShapeDtypeStruct((bsz * seq, 4 * LANES), BF16)] * 2,
        scratch_shapes=[per_block((rows, LANES), MXU_DTYPE),
                        per_block((rows, LANES), MXU_DTYPE),
                        per_block((N_GROUPS * Q_BLOCK, LANES), MXU_DTYPE),
                        per_block((N_GROUPS * Q_BLOCK, LANES), MXU_DTYPE),
                        per_block((rows, LANES), F32),
                        per_block((rows, 2 * LANES), F32),
                        per_block((2, rows, SEL_CHUNK), F32),
                        per_block((Q_BLOCK, 4 * LANES), F32)],
        compiler_params=pltpu.CompilerParams(dimension_semantics=("arbitrary", "arbitrary"),
                                             vmem_limit_bytes=VMEM_LIMIT),
        name="attention",
    )(sinks.astype(F32) * LOG2E, proj['qa'], proj['qb'], proj['ga'], kvcmp, kvcmp, ks, vs, kw, vw, kb, vb,
      cmat, t_near, t_sel, t_win, t_swa)


def _layer_norm(y, g, b):
    mu = jnp.mean(y, axis=-1, keepdims=True)
    yc = y - mu
    var = jnp.mean(yc * yc, axis=-1, keepdims=True)
    return yc * lax.rsqrt(var + LN_EPS) * g + b


def _outproj_kernel(oa_ref, ob_ref, sg_ref, x_ref, pa_ref, pb_ref, wo_ref, g1_ref, b1_ref, wr_ref, rb_ref, sgu_ref,
                    sd_ref, tri_ref, h_ref, base_ref, eidx_ref, gate_ref, rank_ref, cnt_ref, carry):
    step = pl.program_id(0)
    tm = oa_ref.shape[0]

    @pl.when(step == 0)
    def _():
        carry[...] = jnp.zeros(carry.shape, F32)

    sg = sg_ref[...].astype(F32)
    merged = (sg[:, :D_MODEL] * _dot(_mx(oa_ref[...]), pa_ref[...])
              + sg[:, D_MODEL:] * _dot(_mx(ob_ref[...]), pb_ref[...]))
    mix = _dot(_mx(merged), wo_ref[...])
    h = _layer_norm(DN_ALPHA * x_ref[...] + mix, g1_ref[...], b1_ref[...])
    hb = _mx(h)
    h_ref[...] = _pack_bf16_pairs(h)

    gu = _dot(hb, sgu_ref[...])
    shared = _dot(_mx(jax.nn.silu(gu[:, :SHARED_HIDDEN]) * gu[:, SHARED_HIDDEN:]), sd_ref[...])
    base_ref[...] = DN_ALPHA * h + shared

    scores = jax.nn.sigmoid(_dot_nt(wr_ref[...], hb))
    choice = scores + rb_ref[:, 0:1]
    per_group = N_EXPERTS // N_EXPERT_GROUPS
    gs = []
    for g in range(N_EXPERT_GROUPS):
        cg = choice[g * per_group:(g + 1) * per_group]
        m1 = jnp.max(cg, axis=0, keepdims=True)
        is_m = cg == m1
        n_m = jnp.sum(is_m.astype(F32), axis=0, keepdims=True)
        m2 = jnp.max(jnp.where(is_m, -jnp.inf, cg), axis=0, keepdims=True)
        gs.append(m1 + jnp.where(n_m > 1.5, m1, m2))
    gs = jnp.concatenate(gs, axis=0)
    gid = lax.broadcasted_iota(jnp.int32, gs.shape, 0)
    beaten = jnp.zeros(gs.shape, jnp.int32)
    for g in range(N_EXPERT_GROUPS):
        other = gs[g:g + 1]
        beaten = beaten + ((other > gs) | ((other == gs) & (g < gid))).astype(jnp.int32)
    keep_g = beaten < TOPK_EXPERT_GROUPS
    keep = jnp.concatenate([jnp.broadcast_to(keep_g[g:g + 1], (per_group, tm)) for g in range(N_EXPERT_GROUPS)],
                           axis=0)
    cand = jnp.where(keep, choice, -jnp.inf)
    eid = lax.broadcasted_iota(jnp.int32, cand.shape, 0)
    hits = []
    e_rows = []
    w_rows = []
    for _ in range(TOP_K):
        m = jnp.max(cand, axis=0, keepdims=True)
        idx = jnp.min(jnp.where(cand == m, eid, N_EXPERTS), axis=0, keepdims=True)
        hit = eid == idx
        hits.append(hit)
        e_rows.append(idx)
        w_rows.append(jnp.sum(jnp.where(hit, scores, 0.0), axis=0, keepdims=True))
        cand = jnp.where(hit, -jnp.inf, cand)
    w = jnp.concatenate(w_rows, axis=0)
    gate_ref[...] = w / jnp.sum(w, axis=0, keepdims=True) * ROUTED_SCALE
    eidx_ref[...] = jnp.concatenate(e_rows, axis=0)

    onehot = jnp.zeros(cand.shape, F32)
    for hit in hits:
        onehot = onehot + hit.astype(F32)
    before = _dot(onehot.astype(BF16), tri_ref[...]) + carry[:, 0:1]
    rank_ref[...] = jnp.concatenate(
        [jnp.sum(jnp.where(hit, before, 0.0), axis=0, keepdims=True) for hit in hits], axis=0).astype(jnp.int32)
    carry[...] = carry[...] + jnp.sum(onehot, axis=1, keepdims=True)
    cnt_ref[...] = carry[...]


def _out_projection(oa, ob, sg, x2, proj_a, proj_b, w_out, ln_g, ln_b, w_router, router_bias, s_gate, s_up, s_down):
    t = x2.shape[0]
    tm = OUT_TM
    pair_rows = lambda p: p.reshape(N_GROUPS, GROUP, HEAD_DIM, -1).transpose(1, 0, 2, 3).reshape(p.shape)
    pa = pair_rows(proj_a).astype(MXU_DTYPE)
    pb = pair_rows(proj_b).astype(MXU_DTYPE)
    tri = jnp.asarray(np.triu(np.ones((tm, tm), np.float32), 1), BF16)
    row = lambda i: (i, 0)
    fixed = lambda i: (0, 0)
    col = lambda i: (0, i)
    outs = pl.pallas_call(
        _outproj_kernel,
        grid=(t // tm,),
        in_specs=[pl.BlockSpec((tm, 4 * LANES), row), pl.BlockSpec((tm, 4 * LANES), row),
                  pl.BlockSpec((tm, 2 * D_MODEL), row), pl.BlockSpec((tm, D_MODEL), row),
                  pl.BlockSpec((4 * LANES, D_MODEL), fixed), pl.BlockSpec((4 * LANES, D_MODEL), fixed),
                  pl.BlockSpec((D_MODEL, D_MODEL), fixed),
                  pl.BlockSpec((1, D_MODEL), fixed), pl.BlockSpec((1, D_MODEL), fixed),
                  pl.BlockSpec((N_EXPERTS, D_MODEL), fixed), pl.BlockSpec((N_EXPERTS, LANES), fixed),
                  pl.BlockSpec((D_MODEL, 2 * SHARED_HIDDEN), fixed), pl.BlockSpec((SHARED_HIDDEN, D_MODEL), fixed),
                  pl.BlockSpec((tm, tm), fixed)],
        out_specs=[pl.BlockSpec((tm, D_MODEL // 2), row), pl.BlockSpec((tm, D_MODEL), row),
                   pl.BlockSpec((TOP_K, tm), col), pl.BlockSpec((TOP_K, tm), col), pl.BlockSpec((TOP_K, tm), col),
                   pl.BlockSpec((N_EXPERTS, LANES), fixed)],
        out_shape=[jax.ShapeDtypeStruct((t, D_MODEL // 2), jnp.uint32), jax.ShapeDtypeStruct((t, D_MODEL), F32),
                   jax.ShapeDtypeStruct((TOP_K, t), jnp.int32), jax.ShapeDtypeStruct((TOP_K, t), F32),
                   jax.ShapeDtypeStruct((TOP_K, t), jnp.int32), jax.ShapeDtypeStruct((N_EXPERTS, LANES), F32)],
        scratch_shapes=[pltpu.VMEM((N_EXPERTS, LANES), F32)],
        compiler_params=pltpu.CompilerParams(dimension_semantics=("arbitrary",), vmem_limit_bytes=VMEM_LIMIT),
        name="out_projection_router",
    )(oa, ob, sg, x2, pa, pb, w_out.astype(MXU_DTYPE), ln_g.reshape(1, -1), ln_b.reshape(1, -1),
      w_router.T.astype(MXU_DTYPE), jnp.broadcast_to(router_bias.astype(F32)[:, None], (N_EXPERTS, LANES)),
      jnp.concatenate([s_gate, s_up], axis=1).astype(MXU_DTYPE), s_down.astype(MXU_DTYPE), tri)
    return outs


def _rows_to_tiles(x):
    return pltpu.einshape("cml->mcl", jnp.stack(_lane_tiles(x), axis=0))


def _tiles_to_rows(x3):
    xt = pltpu.einshape("mcl->cml", x3)
    return jnp.concatenate([xt[c] for c in range(xt.shape[0])], axis=1)


def _dispatch_kernel(zstart_ref, cnt_ref, dest_ref, h2_ref, xs_ref, h_ref, zeros, sem, zsem):
    step = pl.program_id(0)
    tm = h2_ref.shape[0]
    slot = step % 2
    h_ref[slot] = _rows_to_tiles(h2_ref[...])

    @pl.when(step == 0)
    def _():
        zeros[...] = jnp.zeros(zeros.shape, zeros.dtype)

        def fill(e, c):
            @pl.when(cnt_ref[e] > 0)
            def _():
                pltpu.make_async_copy(zeros, xs_ref.at[pl.ds(zstart_ref[e], MOE_BM)], zsem).start()
            return c

        def fill_done(e, c):
            @pl.when(cnt_ref[e] > 0)
            def _():
                pltpu.make_async_copy(zeros, xs_ref.at[pl.ds(zstart_ref[e], MOE_BM)], zsem).wait()
            return c
        lax.fori_loop(0, N_EXPERTS, fill, 0)
        lax.fori_loop(0, N_EXPERTS, fill_done, 0)

    def issue(t, c):
        for k in range(TOP_K):
            pltpu.make_async_copy(h_ref.at[slot, t], xs_ref.at[dest_ref[k, t]], sem.at[slot]).start(priority=k % 2)
        return c
    lax.fori_loop(0, tm, issue, 0)

    def wait_tile(s):
        for k in range(TOP_K):
            pltpu.make_async_copy(h_ref.at[s], xs_ref.at[pl.ds(0, tm)], sem.at[s]).wait()

    @pl.when(step > 0)
    def _():
        wait_tile(1 - slot)

    @pl.when(step + 1 == pl.num_programs(0))
    def _():
        wait_tile(slot)


def _dispatch(h, dest, zstart, counts, n_rows):
    t = h.shape[0]
    tm = DISP_TM
    return pl.pallas_call(
        _dispatch_kernel,
        grid_spec=pltpu.PrefetchScalarGridSpec(
            num_scalar_prefetch=2,
            grid=(t // tm,),
            in_specs=[pl.BlockSpec((TOP_K, tm), lambda i, *_: (0, i), memory_space=pltpu.SMEM),
                      pl.BlockSpec((tm, D_MODEL // 2), lambda i, *_: (i, 0))],
            out_specs=pl.BlockSpec(memory_space=pl.ANY),
            scratch_shapes=[pltpu.VMEM((2, tm) + PACKED_ROW_TILE, jnp.uint32),
                            pltpu.VMEM((MOE_BM,) + PACKED_ROW_TILE, jnp.uint32),
                            pltpu.SemaphoreType.DMA((2,)), pltpu.SemaphoreType.DMA(())]),
        out_shape=jax.ShapeDtypeStruct((n_rows,) + PACKED_ROW_TILE, jnp.uint32),
        compiler_params=pltpu.CompilerParams(dimension_semantics=("arbitrary",), vmem_limit_bytes=VMEM_LIMIT),
        name="moe_dispatch",
    )(zstart, counts, dest, h)


def _experts_kernel(blk_e_ref, nused_ref, xs_ref, wg_ref, wu_ref, wd_ref, ys_ref, wg_s, wu_s, wd_s):
    b = pl.program_id(0)
    prev = blk_e_ref[jnp.maximum(b - 1, 0)]

    @pl.when((b == 0) | (blk_e_ref[b] != prev))
    def _():
        wg_s[...] = _mx(wg_ref[0])
        wu_s[...] = _mx(wu_ref[0])
        wd_s[...] = _mx(wd_ref[0])

    @pl.when(b < nused_ref[0])
    def _():
        xb = _mx(jnp.concatenate(_unpack_bf16_pairs(_tiles_to_rows(xs_ref[...])), axis=1))
        hid = jax.nn.silu(_dot(xb, wg_s[...])) * _dot(xb, wu_s[...])
        ys_ref[...] = _rows_to_tiles(_pack_bf16_pairs(_dot(_mx(hid), wd_s[...])))

    @pl.when(b >= nused_ref[0])
    def _():
        ys_ref[...] = jnp.zeros(ys_ref.shape, ys_ref.dtype)


def _experts(xs, blk_e, nused, e_gate, e_up, e_down):
    n_rows = xs.shape[0]
    n_blocks = n_rows // MOE_BM
    xmap = lambda b, be, nu: (jnp.minimum(b, nu[0] - 1), 0, 0)
    wmap = lambda b, be, nu: (be[b], 0, 0)
    return pl.pallas_call(
        _experts_kernel,
        grid_spec=pltpu.PrefetchScalarGridSpec(
            num_scalar_prefetch=2,
            grid=(n_blocks,),
            in_specs=[pl.BlockSpec((MOE_BM,) + PACKED_ROW_TILE, xmap),
                      pl.BlockSpec((1, D_MODEL, EXPERT_HIDDEN), wmap),
                      pl.BlockSpec((1, D_MODEL, EXPERT_HIDDEN), wmap),
                      pl.BlockSpec((1, EXPERT_HIDDEN, D_MODEL), wmap)],
            out_specs=pl.BlockSpec((MOE_BM,) + PACKED_ROW_TILE,
                                   lambda b, be, nu: (jnp.where(b < nu[0], b, n_blocks - 1), 0, 0)),
            scratch_shapes=[pltpu.VMEM((D_MODEL, EXPERT_HIDDEN), MXU_DTYPE),
                            pltpu.VMEM((D_MODEL, EXPERT_HIDDEN), MXU_DTYPE),
                            pltpu.VMEM((EXPERT_HIDDEN, D_MODEL), MXU_DTYPE)]),
        out_shape=jax.ShapeDtypeStruct((n_rows,) + PACKED_ROW_TILE, jnp.uint32),
        compiler_params=pltpu.CompilerParams(dimension_semantics=("arbitrary",), vmem_limit_bytes=VMEM_LIMIT),
        name="moe_experts",
    )(blk_e, nused, xs, e_gate, e_up, e_down)


def _combine_kernel(dest_ref, dest_next_ref, gate_ref, base_ref, g2_ref, b2_ref, ys_ref, out_ref, buf, sem):
    step = pl.program_id(0)
    tm = base_ref.shape[0]
    slot = step % 2

    def gather_rows(d_ref, s):
        def body(t, c):
            for k in range(TOP_K):
                pltpu.make_async_copy(ys_ref.at[d_ref[t * TOP_K + k]], buf.at[s, k * tm + t],
                                      sem.at[s]).start(priority=k % 2)
            return c
        lax.fori_loop(0, tm, body, 0)

    @pl.when(step == 0)
    def _():
        gather_rows(dest_ref, 0)

    pltpu.make_async_copy(ys_ref.at[pl.ds(0, tm * TOP_K)], buf.at[slot], sem.at[slot]).wait()

    @pl.when(step + 1 < pl.num_programs(0))
    def _():
        gather_rows(dest_next_ref, 1 - slot)

    gates = gate_ref[...]
    y = base_ref[...]
    for k in range(TOP_K):
        rows_k = jnp.concatenate(_unpack_bf16_pairs(_tiles_to_rows(buf[slot, k * tm:(k + 1) * tm])), axis=1)
        y = y + gates[:, k:k + 1] * rows_k
    out_ref[...] = _layer_norm(y, g2_ref[...], b2_ref[...])


def _combine(ys3, dest, gate, base, ln_g, ln_b):
    t = base.shape[0]
    tm = COMB_TM
    n_tiles = t // tm
    dest_tk = dest.T.reshape(-1)
    return pl.pallas_call(
        _combine_kernel,
        grid=(n_tiles,),
        in_specs=[pl.BlockSpec((tm * TOP_K,), lambda i: (i,), memory_space=pltpu.SMEM),
                  pl.BlockSpec((tm * TOP_K,), lambda i: (jnp.minimum(i + 1, n_tiles - 1),), memory_space=pltpu.SMEM),
                  pl.BlockSpec((tm, TOP_K), lambda i: (i, 0)),
                  pl.BlockSpec((tm, D_MODEL), lambda i: (i, 0)),
                  pl.BlockSpec((1, D_MODEL), lambda i: (0, 0)),
                  pl.BlockSpec((1, D_MODEL), lambda i: (0, 0)),
                  pl.BlockSpec(memory_space=pl.ANY)],
        out_specs=pl.BlockSpec((tm, D_MODEL), lambda i: (i, 0)),
        out_shape=jax.ShapeDtypeStruct((t, D_MODEL), F32),
        scratch_shapes=[pltpu.VMEM((2, tm * TOP_K) + PACKED_ROW_TILE, jnp.uint32), pltpu.SemaphoreType.DMA((2,))],
        compiler_params=pltpu.CompilerParams(dimension_semantics=("arbitrary",), vmem_limit_bytes=VMEM_LIMIT),
        name="moe_combine",
    )(dest_tk, dest_tk, gate.T, base, ln_g.reshape(1, -1), ln_b.reshape(1, -1), ys3)


def _dest_kernel(pstart_ref, eidx_ref, rank_ref, dest_ref):
    eidx = eidx_ref[...]

    unroll = 8

    def body(g, dest):
        for u in range(unroll):
            e = g * unroll + u
            dest = dest + jnp.where(eidx == e, pstart_ref[e], 0)
        return dest
    dest_ref[...] = lax.fori_loop(0, N_EXPERTS // unroll, body, rank_ref[...])


def _dest_rows(pstarts, eidx, rank):
    t = eidx.shape[1]
    tl = 2048
    spec = pl.BlockSpec((TOP_K, tl), lambda i, *_: (0, i))
    return pl.pallas_call(
        _dest_kernel,
        grid_spec=pltpu.PrefetchScalarGridSpec(num_scalar_prefetch=1, grid=(t // tl,), in_specs=[spec, spec],
                                               out_specs=spec),
        out_shape=jax.ShapeDtypeStruct(eidx.shape, jnp.int32),
        compiler_params=pltpu.CompilerParams(dimension_semantics=("arbitrary",)),
        name="moe_dest_rows",
    )(pstarts, eidx, rank)


def _moe_layout(eidx, rank, counts):
    n_assign = eidx.size
    n_blocks = (n_assign + N_EXPERTS * (MOE_BM - 1)) // MOE_BM
    padded = (counts + MOE_BM - 1) // MOE_BM * MOE_BM
    pends = jnp.cumsum(padded)
    pstarts = (pends - padded).astype(jnp.int32)
    dest = _dest_rows(pstarts, eidx, rank)
    block_row = jnp.arange(n_blocks, dtype=jnp.int32) * MOE_BM
    blk_e = jnp.minimum(jnp.sum(pends[None, :] <= block_row[:, None], axis=1), N_EXPERTS - 1).astype(jnp.int32)
    nused = (pends[-1:] // MOE_BM).astype(jnp.int32)
    zstart = jnp.maximum(pends - MOE_BM, 0).astype(jnp.int32)
    return dest.astype(jnp.int32), blk_e, nused, zstart, n_blocks * MOE_BM


def _layer(x, w_in, cmp_pe, cmp_w1, cmp_b1, cmp_w2, sinks, bias_table, proj_a, proj_b, w_out, ln1_g, ln1_b,
           w_router, router_bias, e_gate, e_up, e_down, s_gate, s_up, s_down, ln2_g, ln2_b):
    bsz, seq, d = x.shape
    x2 = x.reshape(bsz * seq, d)
    proj = _in_projection(x2, w_in)
    kvcmp = _compress(proj['kc'], proj['vc'], bsz, seq, cmp_pe, cmp_w1, cmp_b1, cmp_w2)
    oa, ob = _attention(proj, kvcmp, sinks, bias_table, bsz, seq)
    h, base, eidx, gate, rank, cnt = _out_projection(oa, ob, proj['sg'], x2, proj_a, proj_b, w_out, ln1_g, ln1_b,
                                                     w_router, router_bias, s_gate, s_up, s_down)
    counts = cnt[:, 0].astype(jnp.int32)
    dest, blk_e, nused, zstart, n_rows = _moe_layout(eidx, rank, counts)
    xs = _dispatch(h, dest, zstart, counts, n_rows)
    ys = _experts(xs, blk_e, nused, e_gate, e_up, e_down)
    out = _combine(ys, dest, gate, base, ln2_g, ln2_b)
    return out.reshape(bsz, seq, d)


def kernel(x, w_in, cmp_pe, cmp_w1, cmp_b1, cmp_w2, attn_sinks, rel_bias_table, proj_a, proj_b, w_out, ln1_g, ln1_b,
           w_router, router_bias, expert_w_gate, expert_w_up, expert_w_down, shared_w_gate, shared_w_up,
           shared_w_down, ln2_g, ln2_b):
    h = x
    for l in range(DEPTH):
        h = _layer(h, w_in[l], cmp_pe[l], cmp_w1[l], cmp_b1[l], cmp_w2[l], attn_sinks[l], rel_bias_table, proj_a[l],
                   proj_b[l], w_out[l], ln1_g[l], ln1_b[l], w_router[l], router_bias[l], expert_w_gate[l],
                   expert_w_up[l], expert_w_down[l], shared_w_gate[l], shared_w_up[l], shared_w_down[l], ln2_g[l],
                   ln2_b[l])
    return h
```

```python
import functools
import math

import numpy as np
import jax
import jax.numpy as jnp
from jax import lax
from jax.experimental import pallas as pl
from jax.experimental.pallas import tpu as pltpu

F32 = jnp.float32
BF16 = jnp.bfloat16
MXU_DTYPE = jnp.bfloat16

D_MODEL = 1024
HEAD_DIM = 64
ATTN_SCALE = HEAD_DIM ** -0.5
LOG2E = math.log2(math.e)
Q_BLOCK = 128
N_HEADS = 8
N_GROUPS = 2
GROUP = 4
CMP_BLOCK = 32
CMP_STRIDE = 16
CMP_HIDDEN = 128
SEL_BLOCK = 64
SEL_TOP_N = 8
SEL_INIT_BLOCKS = 1
SEL_LOCAL_BLOCKS = 2
NSA_WINDOW = 512
SWA_WINDOW = 128
REL_BUCKETS = 32
REL_MAX_DIST = 128
N_EXPERTS = 256
TOP_K = 8
EXPERT_HIDDEN = 256
SHARED_HIDDEN = 256
N_EXPERT_GROUPS = 8
TOPK_EXPERT_GROUPS = 4
ROUTED_SCALE = 2.5
LN_EPS = 1e-5
DEPTH = 1
DN_ALPHA = (2 * DEPTH) ** 0.25

NEG = -1e30
LANES = 128
ROW_TILE = (8, LANES)
PACKED_ROW_TILE = (4, LANES)
CMP_FRONT = 16
CMP_NEAR = LANES
SEL_CHUNK = 1024
QB_PER_STEP = 1
VMEM_LIMIT = 56 * 1024 * 1024

IN_TM = 512
OUT_TM = 512
MOE_BM = 512
DISP_TM = 256
COMB_TM = 256


def _dot(a, b):
    return jnp.dot(a, b, preferred_element_type=F32)


def _dot_nt(a, b):
    return lax.dot_general(a, b, (((1,), (1,)), ((), ())), preferred_element_type=F32)


def _mx(a):
    return a.astype(MXU_DTYPE)


def _pack_bf16_pairs(x):
    half = x.shape[1] // 2
    bits = lax.bitcast_convert_type(x.astype(BF16).astype(F32), jnp.uint32)
    return (bits[:, half:] & jnp.uint32(0xFFFF0000)) | (bits[:, :half] >> 16)


def _unpack_bf16_pairs(words):
    return (lax.bitcast_convert_type(words << 16, F32),
            lax.bitcast_convert_type(words & jnp.uint32(0xFFFF0000), F32))


_IN_COLS = (('qa', 512), ('qb', 512), ('kc', 128), ('vc', 128), ('ks', 128), ('vs', 128), ('kw', 128),
            ('vw', 128), ('kb', 128), ('vb', 128), ('ga', 128), ('sg', 2048))


def _inproj_kernel(x_ref, w_ref, qa_ref, qb_ref, kc_ref, vc_ref, ks_ref, vs_ref, kw_ref, vw_ref, kb_ref, vb_ref,
                   ga_ref, sg_ref):
    xb = _mx(x_ref[...])
    outs = dict(qa=qa_ref, qb=qb_ref, kc=kc_ref, vc=vc_ref, ks=ks_ref, vs=vs_ref, kw=kw_ref, vw=vw_ref,
                kb=kb_ref, vb=vb_ref, ga=ga_ref, sg=sg_ref)
    tiles = [(name, c) for name, width in _IN_COLS for c in range(0, width, LANES)]
    chunk = 4
    for t0 in range(0, len(tiles), chunk):
        group = tiles[t0:t0 + chunk]
        y = _dot(xb, w_ref[:, t0 * LANES:(t0 + len(group)) * LANES])
        for j, (name, c) in enumerate(group):
            yj = y[:, j * LANES:(j + 1) * LANES]
            if name in ('ga', 'sg'):
                yj = jax.nn.sigmoid(yj)
            outs[name][:, c:c + LANES] = yj.astype(outs[name].dtype)


def _pair_head_columns(w):
    return w.reshape(w.shape[0], N_GROUPS, GROUP, HEAD_DIM).transpose(0, 2, 1, 3).reshape(w.shape[0], -1)


def _in_projection(x2, w_in):
    t = x2.shape[0]
    sizes = (512, 128, 128, 128, 128, 128, 128, 24, 512, 128, 128, 1024, 1024)
    offs = np.cumsum((0,) + sizes)
    part = [w_in[:, offs[k]:offs[k + 1]] for k in range(len(sizes))]
    w_qa, w_kc, w_vc, w_ks, w_vs, w_kw, w_vw, w_g, w_qb, w_kb, w_vb, w_gate_a, w_gate_b = part
    w_qa = _pair_head_columns(w_qa) * (ATTN_SCALE * LOG2E)
    w_qb = _pair_head_columns(w_qb) * (ATTN_SCALE * LOG2E)
    w_ga = w_g.reshape(-1, N_GROUPS, GROUP, 3).transpose(0, 3, 2, 1).reshape(-1, 24)
    w_ga = jnp.pad(w_ga, ((0, 0), (0, LANES - 24)))
    w_all = jnp.concatenate([w_qa, w_qb, w_kc, w_vc, w_ks, w_vs, w_kw, w_vw, w_kb, w_vb, w_ga, w_gate_a, w_gate_b],
                            axis=1).astype(MXU_DTYPE)
    n_all = w_all.shape[1]
    out_shape = []
    out_specs = []
    for name, width in _IN_COLS:
        dt = F32 if name == 'ga' else BF16
        out_shape.append(jax.ShapeDtypeStruct((t, width), dt))
        out_specs.append(pl.BlockSpec((IN_TM, width), lambda i: (i, 0)))
    outs = pl.pallas_call(
        _inproj_kernel,
        grid=(t // IN_TM,),
        in_specs=[pl.BlockSpec((IN_TM, D_MODEL), lambda i: (i, 0)),
                  pl.BlockSpec((D_MODEL, n_all), lambda i: (0, 0))],
        out_specs=out_specs,
        out_shape=out_shape,
        compiler_params=pltpu.CompilerParams(dimension_semantics=("arbitrary",), vmem_limit_bytes=VMEM_LIMIT),
        name="in_projection",
    )(x2, w_all)
    return dict(zip([n for n, _ in _IN_COLS], outs))


def _compress_kernel(tok_ref, w1_ref, pe_ref, w1o_ref, b1_ref, w2_ref, out_ref):
    n_chunks = tok_ref.shape[2]
    ab = _dot(tok_ref[0, 0], w1_ref[0])
    a = ab[:, :2 * CMP_HIDDEN]
    b_next = pltpu.roll(ab[:, 2 * CMP_HIDDEN:], n_chunks - 1, 0)
    cb = _dot(_mx(pe_ref[0]), _mx(w1o_ref[0]))[0:1, :] + b1_ref[0]
    cb2 = jnp.concatenate([cb, cb], axis=1)
    hid = jax.nn.gelu(a + b_next + cb2)
    out = _dot(_mx(hid), w2_ref[0])
    row = lax.broadcasted_iota(jnp.int32, out.shape, 0)
    out = jnp.where(row < n_chunks - 1, out, 0.0)
    out_ref[0, 0, 0:CMP_FRONT, :] = jnp.zeros((CMP_FRONT, LANES), F32)
    out_ref[0, 0, CMP_FRONT:CMP_FRONT + n_chunks, :] = out
    out_ref[0, 0, CMP_FRONT + n_chunks:, :] = jnp.zeros((CMP_NEAR - CMP_FRONT, LANES), F32)


def _compress(kc, vc, bsz, seq, cmp_pe, cmp_w1, cmp_b1, cmp_w2):
    n_chunks = seq // CMP_STRIDE
    tok = jnp.stack([kc, vc]).reshape(2, bsz, n_chunks, CMP_STRIDE * LANES)
    eye = jnp.eye(N_GROUPS, dtype=F32)
    w1r = cmp_w1.reshape(2, 2, CMP_STRIDE, HEAD_DIM, CMP_HIDDEN)
    w1 = jnp.einsum('khjdn,gG->kjgdhGn', w1r, eye).reshape(2, CMP_STRIDE * LANES, 4 * CMP_HIDDEN).astype(MXU_DTYPE)
    w2 = jnp.einsum('knd,gG->kgnGd', cmp_w2, eye).reshape(2, 2 * CMP_HIDDEN, LANES).astype(MXU_DTYPE)
    pe = jnp.pad(cmp_pe.reshape(2, 1, CMP_BLOCK * HEAD_DIM), ((0, 0), (0, 7), (0, 0)))
    b1 = cmp_b1.reshape(2, 1, CMP_HIDDEN)
    rows = CMP_FRONT + n_chunks + CMP_NEAR - CMP_FRONT
    return pl.pallas_call(
        _compress_kernel,
        grid=(2, bsz),
        in_specs=[pl.BlockSpec((1, 1, n_chunks, CMP_STRIDE * LANES), lambda k, b: (k, b, 0, 0)),
                  pl.BlockSpec((1, CMP_STRIDE * LANES, 4 * CMP_HIDDEN), lambda k, b: (k, 0, 0)),
                  pl.BlockSpec((1, 8, CMP_BLOCK * HEAD_DIM), lambda k, b: (k, 0, 0)),
                  pl.BlockSpec((1, CMP_BLOCK * HEAD_DIM, CMP_HIDDEN), lambda k, b: (k, 0, 0)),
                  pl.BlockSpec((1, 1, CMP_HIDDEN), lambda k, b: (k, 0, 0)),
                  pl.BlockSpec((1, 2 * CMP_HIDDEN, LANES), lambda k, b: (k, 0, 0))],
        out_specs=pl.BlockSpec((1, 1, rows, LANES), lambda k, b: (k, b, 0, 0)),
        out_shape=jax.ShapeDtypeStruct((2, bsz, rows, LANES), F32),
        compiler_params=pltpu.CompilerParams(dimension_semantics=("arbitrary", "arbitrary"),
                                             vmem_limit_bytes=VMEM_LIMIT),
        name="nsa_compress",
    )(tok, w1, pe, cmp_w1, b1, w2)


def _stack_heads(q_ref, dst):
    lo = lax.broadcasted_iota(jnp.int32, (Q_BLOCK, LANES), 1) < HEAD_DIM
    for r in range(GROUP):
        qr = q_ref[:, r * LANES:(r + 1) * LANES].astype(dst.dtype)
        z = jnp.zeros_like(qr)
        dst[(2 * r) * Q_BLOCK:(2 * r + 1) * Q_BLOCK, :] = jnp.where(lo, qr, z)
        dst[(2 * r + 1) * Q_BLOCK:(2 * r + 2) * Q_BLOCK, :] = jnp.where(lo, z, qr)


def _pair_heads(o, r):
    lo = lax.broadcasted_iota(jnp.int32, (Q_BLOCK, LANES), 1) < HEAD_DIM
    return jnp.where(lo, o[(2 * r) * Q_BLOCK:(2 * r + 1) * Q_BLOCK], o[(2 * r + 1) * Q_BLOCK:(2 * r + 2) * Q_BLOCK])


def _lane_tiles(x):
    return [x[:, t * LANES:(t + 1) * LANES] for t in range(x.shape[1] // LANES)]


def _row_max(tiles):
    mx = tiles[0]
    for t in tiles[1:]:
        mx = jnp.maximum(mx, t)
    return jnp.broadcast_to(jnp.max(mx, axis=1, keepdims=True), mx.shape)


def _with_ones(v):
    return jnp.concatenate([v, jnp.ones(v.shape, v.dtype)], axis=1)


def _block_of_key(n_keys, first_block):
    b = lax.broadcasted_iota(jnp.int32, (LANES, n_keys), 0)
    k = lax.broadcasted_iota(jnp.int32, (LANES, n_keys), 1)
    return (b == (k // SEL_BLOCK) + first_block).astype(MXU_DTYPE)


def _select_blocks_t(imp_t, i, n_top):
    blk = lax.broadcasted_iota(jnp.int32, imp_t.shape, 0)
    qcol = lax.broadcasted_iota(jnp.int32, imp_t.shape, 1)
    back = (2 * i + (qcol >= SEL_BLOCK).astype(jnp.int32)) - blk
    sel = (back >= 0) & ((blk < SEL_INIT_BLOCKS) | (back < SEL_LOCAL_BLOCKS))
    cand = jnp.where((back >= SEL_LOCAL_BLOCKS) & (blk >= SEL_INIT_BLOCKS), imp_t, -1.0)
    blk_f = blk.astype(F32)
    for _ in range(n_top - SEL_INIT_BLOCKS - SEL_LOCAL_BLOCKS):
        m = jnp.max(cand, axis=0, keepdims=True)
        idx = jnp.min(jnp.where(cand == m, blk_f, float(LANES)), axis=0, keepdims=True)
        hit = blk_f == idx
        sel = sel | (hit & (m >= 0.0))
        cand = jnp.where(hit, -2.0, cand)
    return sel


def _query_block(i, sink_ref, qa_ref, qb_ref, ga_ref, kcmp_ref, vcmp_ref, ks_ref, vs_ref, kw_ref, vw_ref, kb_ref,
                 vb_ref, cmat_ref, tnear_ref, tsel_ref, twin_ref, tswa_ref, oa_ref, ob_ref,
                 qall, qball, mneg, mneg_far, m_s, acc_s, s_buf, oa_acc, n_far, n_top):
    rows = N_HEADS * Q_BLOCK
    half = rows // 2
    halves = (slice(0, half), slice(half, rows))
    _stack_heads(qa_ref, qall)
    _stack_heads(qb_ref, qball)
    nstart = pl.multiple_of(i * Q_BLOCK, Q_BLOCK)
    lo = lax.broadcasted_iota(jnp.int32, (Q_BLOCK, LANES), 1) < HEAD_DIM
    gates = ga_ref[...]

    def gate_tile(c, r):
        return jnp.where(lo, gates[:, c * 8 + 2 * r:c * 8 + 2 * r + 1], gates[:, c * 8 + 2 * r + 1:c * 8 + 2 * r + 2])

    def softmax_pv(s_tiles, v1, fix_max=None):
        m = _row_max(s_tiles)
        if fix_max is not None:
            m = fix_max(m)
        e = [jnp.exp2(t - m) for t in s_tiles]
        return e, m, _dot(_mx(jnp.concatenate(e, axis=1)), v1)

    off = pl.multiple_of(i * (Q_BLOCK // CMP_STRIDE), 8)
    k_cmp = _mx(jnp.concatenate([kcmp_ref[0, 0, 0:n_far, :], kcmp_ref[0, 0, pl.ds(off, CMP_NEAR), :]], axis=0))
    v_cmp = _with_ones(_mx(jnp.concatenate([vcmp_ref[0, 0, 0:n_far, :], vcmp_ref[0, 0, pl.ds(off, CMP_NEAR), :]],
                                           axis=0)))
    colf = lax.broadcasted_iota(jnp.int32, (1, n_far), 1)
    coln = lax.broadcasted_iota(jnp.int32, (1, CMP_NEAR), 1)
    col_ok = jnp.concatenate([(colf >= CMP_FRONT) & (colf < off), coln + off >= CMP_FRONT], axis=1)
    mask_c = jnp.where(col_ok, 0.0, NEG)
    no_key = lambda m: jnp.where(m > 0.5 * NEG, m, 0.0)
    p_cmp, o_c = [], []
    for rs in halves:
        tiles = _lane_tiles(_dot_nt(qall[rs, :], k_cmp) + mask_c)
        tiles[-1] = tiles[-1] + tnear_ref[rs, :]
        e, _, ov = softmax_pv(tiles, v_cmp, no_key)
        inv = 1.0 / jnp.maximum(ov[:, LANES:], 1e-30)
        o_c.append(ov[:, :LANES] * inv)
        p_cmp.append([t * inv for t in e])
    o_c = jnp.concatenate(o_c, axis=0)
    yield None

    def far_start(j):
        return pl.multiple_of(Q_BLOCK + j * SEL_CHUNK, Q_BLOCK)

    def far_logits(j, slot, masked):
        kc = _mx(ks_ref[0, pl.ds(far_start(j), SEL_CHUNK), :])
        if masked:
            madd = _dot(mneg_far[...], _block_of_key(SEL_CHUNK, j * (SEL_CHUNK // SEL_BLOCK)))
        for rs in halves:
            s = _dot_nt(qall[rs, :], kc)
            s_buf[slot, rs, :] = s + jnp.concatenate([madd] * (GROUP // 2), axis=0) if masked else s

    far_logits(0, 0, False)

    blkcol = lax.broadcasted_iota(jnp.int32, (Q_BLOCK, LANES), 1)
    n_tiles = len(p_cmp[0])
    for g in range(N_GROUPS):
        imp = jnp.zeros((Q_BLOCK, LANES), F32)
        for t in range(n_tiles):
            pg = sum(p_cmp[r // 2][t][(2 * (r % 2) + g) * Q_BLOCK:(2 * (r % 2) + g + 1) * Q_BLOCK]
                     for r in range(GROUP))
            if t < n_tiles - 1:
                cm = _mx(cmat_ref[t * LANES:(t + 1) * LANES, :])
            else:
                cm = _mx(cmat_ref[pl.ds(off, CMP_NEAR), :])
            hi = _mx(pg)
            low = _mx(pg - hi.astype(F32))
            imp = imp + _dot(hi, cm) + _dot(low, cm)
        sel = _select_blocks_t(imp.T, i, n_top)
        neg = jnp.where(sel, 0.0, NEG).T
        mneg[g * Q_BLOCK:(g + 1) * Q_BLOCK, :] = neg.astype(mneg.dtype)
        mneg_far[g * Q_BLOCK:(g + 1) * Q_BLOCK, :] = jnp.where(blkcol < 2 * (i - 1), neg, NEG).astype(mneg.dtype)

    yield None

    wpad = kw_ref.shape[1] - ks_ref.shape[1] + Q_BLOCK
    kwin = _mx(kw_ref[0, pl.ds(nstart, wpad + Q_BLOCK), :])
    vwin = _with_ones(_mx(vw_ref[0, pl.ds(nstart, wpad + Q_BLOCK), :]))
    colw = lax.broadcasted_iota(jnp.int32, (1, wpad + Q_BLOCK), 1)
    mask_w = jnp.where(colw + nstart >= wpad, 0.0, NEG)
    o_w = []
    for rs in halves:
        _, _, ov = softmax_pv(_lane_tiles(_dot_nt(qall[rs, :], kwin) + twin_ref[rs, :] + mask_w), vwin)
        o_w.append(ov[:, :LANES] / ov[:, LANES:])
    o_w = jnp.concatenate(o_w, axis=0)
    for r in range(GROUP):
        oa_acc[:, r * LANES:(r + 1) * LANES] = (gate_tile(0, r) * _pair_heads(o_c, r)
                                                + gate_tile(2, r) * _pair_heads(o_w, r))

    yield None

    bpad = kb_ref.shape[1] - ks_ref.shape[1] + Q_BLOCK
    kwin = _mx(kb_ref[0, pl.ds(nstart, bpad + Q_BLOCK), :])
    vwin = _with_ones(_mx(vb_ref[0, pl.ds(nstart, bpad + Q_BLOCK), :]))
    colb = lax.broadcasted_iota(jnp.int32, (1, bpad + Q_BLOCK), 1)
    mask_b = jnp.where(colb + nstart >= bpad, 0.0, NEG)
    o_b = []
    for hh, rs in enumerate(halves):
        sink = jnp.concatenate([jnp.full((Q_BLOCK, LANES), sink_ref[(h % 2) * GROUP + h // 2], F32)
                                for h in range(hh * N_HEADS // 2, (hh + 1) * N_HEADS // 2)], axis=0)
        _, m, ov = softmax_pv(_lane_tiles(_dot_nt(qball[rs, :], kwin) + tswa_ref[rs, :] + mask_b), vwin,
                              lambda m: jnp.maximum(m, sink))
        o_b.append(ov[:, :LANES] / (ov[:, LANES:] + jnp.exp2(sink - m)))
    o_b = jnp.concatenate(o_b, axis=0)
    for r in range(GROUP):
        ob_ref[:, r * LANES:(r + 1) * LANES] = _pair_heads(o_b, r).astype(ob_ref.dtype)

    yield None

    m_s[...] = jnp.full(m_s.shape, NEG, F32)
    acc_s[...] = jnp.zeros(acc_s.shape, F32)

    def flash_update(rs, s, v1):
        s_tiles = _lane_tiles(s)
        m_old = m_s[rs, :]
        m_new = jnp.maximum(m_old, _row_max(s_tiles))
        alpha = jnp.exp2(m_old - m_new)
        p = jnp.concatenate([jnp.exp2(t - m_new) for t in s_tiles], axis=1)
        acc_s[rs, :] = jnp.concatenate([alpha, alpha], axis=1) * acc_s[rs, :] + _dot(_mx(p), v1)
        m_s[rs, :] = m_new

    madd = _dot(mneg_far[...], _block_of_key(SEL_CHUNK, 0))
    for rs in halves:
        s_buf[0, rs, :] = s_buf[0, rs, :] + jnp.concatenate([madd] * (GROUP // 2), axis=0)

    def far_update(j):
        v1 = _with_ones(_mx(vs_ref[0, pl.ds(far_start(j), SEL_CHUNK), :]))
        for rs in halves:
            flash_update(rs, s_buf[j % 2, rs, :], v1)

    yield far_update, far_logits

    kc = _mx(ks_ref[0, pl.ds(nstart, 2 * Q_BLOCK), :])
    v1 = _with_ones(_mx(vs_ref[0, pl.ds(nstart, 2 * Q_BLOCK), :]))
    madd = _dot(mneg[...], _block_of_key(2 * Q_BLOCK, 2 * (i - 1)))
    col2 = lax.broadcasted_iota(jnp.int32, (1, 2 * Q_BLOCK), 1)
    mask_n = jnp.where((col2 < Q_BLOCK) & (i == 0), NEG, 0.0)
    for rs in halves:
        s = _dot_nt(qall[rs, :], kc) + jnp.concatenate([madd] * (GROUP // 2), axis=0) + tsel_ref[rs, :] + mask_n
        flash_update(rs, s, v1)
    acc = acc_s[...]
    o_s = acc[:, :LANES] / acc[:, LANES:]
    for r in range(GROUP):
        tile = oa_acc[:, r * LANES:(r + 1) * LANES] + gate_tile(1, r) * _pair_heads(o_s, r)
        oa_ref[:, r * LANES:(r + 1) * LANES] = tile.astype(oa_ref.dtype)
    yield None


def _attn_kernel(sink_ref, qa_ref, qb_ref, ga_ref, kcmp_ref, vcmp_ref, ks_ref, vs_ref, kw_ref, vw_ref, kb_ref,
                 vb_ref, cmat_ref, tnear_ref, tsel_ref, twin_ref, tswa_ref, oa_ref, ob_ref,
                 qall, qball, mneg, mneg_far, m_s, acc_s, s_buf, oa_acc, *, n_far, n_top):
    first = pl.program_id(1) * QB_PER_STEP
    blocks, steps = [], []
    for n in range(QB_PER_STEP):
        qrows = pl.ds(n * Q_BLOCK, Q_BLOCK)
        blk = _query_block(first + n, sink_ref, qa_ref.at[qrows, :], qb_ref.at[qrows, :], ga_ref.at[qrows, :],
                           kcmp_ref, vcmp_ref, ks_ref, vs_ref, kw_ref, vw_ref, kb_ref, vb_ref, cmat_ref, tnear_ref,
                           tsel_ref, twin_ref, tswa_ref, oa_ref.at[qrows, :], ob_ref.at[qrows, :],
                           qall.at[n], qball.at[n], mneg.at[n], mneg_far.at[n], m_s.at[n], acc_s.at[n], s_buf.at[n],
                           oa_acc.at[n], n_far, n_top)
        blocks.append(blk)
    steps = [next(blk) for blk in blocks]
    while steps[0] is None:
        steps = [next(blk) for blk in blocks]
    n_far_keys = jnp.maximum(first + QB_PER_STEP - 2, 0) * Q_BLOCK
    n_chunks = (n_far_keys + SEL_CHUNK - 1) // SEL_CHUNK

    def far_body(j, carry):
        for far_update, _ in steps:
            far_update(j)
        for _, far_logits in steps:
            far_logits(j + 1, (j + 1) % 2, True)
        return carry

    last = jnp.maximum(n_chunks - 1, 0)
    lax.fori_loop(0, last, far_body, 0)
    for far_update, _ in steps:
        far_update(last)
    for blk in blocks:
        next(blk)


def _rel_bucket_np(dist):
    n = np.maximum(dist, 0)
    max_exact = REL_BUCKETS // 2
    nf = np.maximum(n, 1).astype(np.float32)
    log_b = max_exact + (np.log(nf / max_exact) / math.log(REL_MAX_DIST / max_exact)
                         * (REL_BUCKETS - max_exact)).astype(np.int32)
    log_b = np.minimum(log_b, REL_BUCKETS - 1)
    return np.where(n < max_exact, n, log_b)


def _toeplitz_bias(tab, pad, width, window, shift_far):
    length = width + Q_BLOCK
    dist = pad + Q_BLOCK - 1 - np.arange(length)
    onehot = np.zeros((length, REL_BUCKETS), np.float32)
    onehot[np.arange(length), _rel_bucket_np(dist)] = 1.0
    vals = jnp.dot(jnp.asarray(onehot), tab, precision=lax.Precision.HIGHEST)
    if shift_far:
        vals = vals - tab[REL_BUCKETS - 1][None, :]
    vals = vals * LOG2E
    valid = (dist >= 0) & (dist < window)
    vals = jnp.where(jnp.asarray(valid)[:, None], vals, NEG).T
    skew = jnp.tile(vals, (1, Q_BLOCK))[:, :Q_BLOCK * (length - 1)].reshape(N_HEADS, Q_BLOCK, length - 1)
    return skew[:, :, Q_BLOCK - 1:Q_BLOCK - 1 + width].reshape(N_HEADS * Q_BLOCK, width).astype(F32)


def _attention(proj, kvcmp, sinks, bias_table, bsz, seq):
    assert seq % SEL_CHUNK == 0
    nq = seq // Q_BLOCK
    n_far = seq // CMP_STRIDE
    n_sel = seq // SEL_BLOCK
    n_top = min(SEL_TOP_N, n_sel)
    assert n_top >= SEL_INIT_BLOCKS + SEL_LOCAL_BLOCKS and n_sel <= LANES
    wpad = Q_BLOCK * (-(-(NSA_WINDOW - 1) // Q_BLOCK))
    bpad = Q_BLOCK * (-(-(SWA_WINDOW - 1) // Q_BLOCK))
    pair = lambda tab: tab.astype(F32).reshape(REL_BUCKETS, N_GROUPS, GROUP).transpose(0, 2, 1).reshape(REL_BUCKETS, -1)
    tab_a = pair(bias_table[:, :N_HEADS])
    tab_b = pair(bias_table[:, N_HEADS:])
    near_pad = CMP_STRIDE * CMP_FRONT - (CMP_BLOCK - 1)
    t_near = _toeplitz_bias(tab_a, near_pad, CMP_STRIDE * CMP_NEAR, 1 << 30, True)[:, ::CMP_STRIDE]
    t_sel = _toeplitz_bias(tab_a, Q_BLOCK, 2 * Q_BLOCK, 1 << 30, True)
    t_win = _toeplitz_bias(tab_a, wpad, wpad + Q_BLOCK, NSA_WINDOW, False)
    t_swa = _toeplitz_bias(tab_b, bpad, bpad + Q_BLOCK, SWA_WINDOW, False)
    n_rows = kvcmp.shape[2]
    cn = (np.arange(n_rows) - CMP_FRONT)[:, None] * CMP_STRIDE
    sj = np.arange(LANES)[None, :] * SEL_BLOCK
    cmat = ((cn < sj + SEL_BLOCK) & (cn + CMP_BLOCK > sj) & (cn >= 0) & (cn + CMP_BLOCK <= seq)
            & (sj < seq)).astype(np.float32)
    cmat = jnp.asarray(cmat, F32)
    padded = lambda name, p: jnp.pad(proj[name].reshape(bsz, seq, LANES), ((0, 0), (p, 0), (0, 0)))
    ks, vs = padded('ks', Q_BLOCK), padded('vs', Q_BLOCK)
    kw, vw = padded('kw', wpad), padded('vw', wpad)
    kb, vb = padded('kb', bpad), padded('vb', bpad)
    rows = N_HEADS * Q_BLOCK
    n_steps = nq // QB_PER_STEP
    qspec = pl.BlockSpec((QB_PER_STEP * Q_BLOCK, 4 * LANES), lambda b, i: (b * n_steps + i, 0))
    const2 = lambda shape: pl.BlockSpec(shape, lambda b, i: (0, 0))
    batch3 = lambda n: pl.BlockSpec((1, n, LANES), lambda b, i: (b, 0, 0))
    per_block = lambda shape, dtype: pltpu.VMEM((QB_PER_STEP,) + shape, dtype)
    kernel = functools.partial(_attn_kernel, n_far=n_far, n_top=n_top)
    return pl.pallas_call(
        kernel,
        grid=(bsz, n_steps),
        in_specs=[pl.BlockSpec(memory_space=pltpu.SMEM),
                  qspec, qspec,
                  pl.BlockSpec((QB_PER_STEP * Q_BLOCK, LANES), lambda b, i: (b * n_steps + i, 0)),
                  pl.BlockSpec((1, 1, n_rows, LANES), lambda b, i: (0, b, 0, 0)),
                  pl.BlockSpec((1, 1, n_rows, LANES), lambda b, i: (1, b, 0, 0)),
                  batch3(seq + Q_BLOCK), batch3(seq + Q_BLOCK),
                  batch3(seq + wpad), batch3(seq + wpad),
                  batch3(seq + bpad), batch3(seq + bpad),
                  const2((n_rows, LANES)),
                  const2((rows, CMP_NEAR)),
                  const2((rows, 2 * Q_BLOCK)),
                  const2((rows, wpad + Q_BLOCK)),
                  const2((rows, bpad + Q_BLOCK))],
        out_specs=[qspec, qspec],
        out_shape=[jax.ShapeDtypeStruct((bsz * seq, 4 * LANES), BF16)] * 2,
        scratch_shapes=[per_block((rows, LANES), MXU_DTYPE),
                        per_block((rows, LANES), MXU_DTYPE),
                        per_block((N_GROUPS * Q_BLOCK, LANES), MXU_DTYPE),
                        per_block((N_GROUPS * Q_BLOCK, LANES), MXU_DTYPE),
                        per_block((rows, LANES), F32),
                        per_block((rows, 2 * LANES), F32),
                        per_block((2, rows, SEL_CHUNK), F32),
                        per_block((Q_BLOCK, 4 * LANES), F32)],
        compiler_params=pltpu.CompilerParams(dimension_semantics=("arbitrary", "arbitrary"),
                                             vmem_limit_bytes=VMEM_LIMIT),
        name="attention",
    )(sinks.astype(F32) * LOG2E, proj['qa'], proj['qb'], proj['ga'], kvcmp, kvcmp, ks, vs, kw, vw, kb, vb,
      cmat, t_near, t_sel, t_win, t_swa)


def _layer_norm(y, g, b):
    mu = jnp.mean(y, axis=-1, keepdims=True)
    yc = y - mu
    var = jnp.mean(yc * yc, axis=-1, keepdims=True)
    return yc * lax.rsqrt(var + LN_EPS) * g + b


def _outproj_kernel(oa_ref, ob_ref, sg_ref, x_ref, pa_ref, pb_ref, wo_ref, g1_ref, b1_ref, wr_ref, rb_ref, sgu_ref,
                    sd_ref, tri_ref, h_ref, base_ref, eidx_ref, gate_ref, rank_ref, cnt_ref, carry):
    step = pl.program_id(0)
    tm = oa_ref.shape[0]

    @pl.when(step == 0)
    def _():
        carry[...] = jnp.zeros(carry.shape, F32)

    sg = sg_ref[...].astype(F32)
    merged = (sg[:, :D_MODEL] * _dot(_mx(oa_ref[...]), pa_ref[...])
              + sg[:, D_MODEL:] * _dot(_mx(ob_ref[...]), pb_ref[...]))
    mix = _dot(_mx(merged), wo_ref[...])
    h = _layer_norm(DN_ALPHA * x_ref[...] + mix, g1_ref[...], b1_ref[...])
    hb = _mx(h)
    h_ref[...] = _pack_bf16_pairs(h)

    gu = _dot(hb, sgu_ref[...])
    shared = _dot(_mx(jax.nn.silu(gu[:, :SHARED_HIDDEN]) * gu[:, SHARED_HIDDEN:]), sd_ref[...])
    base_ref[...] = DN_ALPHA * h + shared

    scores = jax.nn.sigmoid(_dot_nt(wr_ref[...], hb))
    choice = scores + rb_ref[:, 0:1]
    per_group = N_EXPERTS // N_EXPERT_GROUPS
    gs = []
    for g in range(N_EXPERT_GROUPS):
        cg = choice[g * per_group:(g + 1) * per_group]
        m1 = jnp.max(cg, axis=0, keepdims=True)
        is_m = cg == m1
        n_m = jnp.sum(is_m.astype(F32), axis=0, keepdims=True)
        m2 = jnp.max(jnp.where(is_m, -jnp.inf, cg), axis=0, keepdims=True)
        gs.append(m1 + jnp.where(n_m > 1.5, m1, m2))
    gs = jnp.concatenate(gs, axis=0)
    gid = lax.broadcasted_iota(jnp.int32, gs.shape, 0)
    beaten = jnp.zeros(gs.shape, jnp.int32)
    for g in range(N_EXPERT_GROUPS):
        other = gs[g:g + 1]
        beaten = beaten + ((other > gs) | ((other == gs) & (g < gid))).astype(jnp.int32)
    keep_g = beaten < TOPK_EXPERT_GROUPS
    keep = jnp.concatenate([jnp.broadcast_to(keep_g[g:g + 1], (per_group, tm)) for g in range(N_EXPERT_GROUPS)],
                           axis=0)
    cand = jnp.where(keep, choice, -jnp.inf)
    eid = lax.broadcasted_iota(jnp.int32, cand.shape, 0)
    hits = []
    e_rows = []
    w_rows = []
    for _ in range(TOP_K):
        m = jnp.max(cand, axis=0, keepdims=True)
        idx = jnp.min(jnp.where(cand == m, eid, N_EXPERTS), axis=0, keepdims=True)
        hit = eid == idx
        hits.append(hit)
        e_rows.append(idx)
        w_rows.append(jnp.sum(jnp.where(hit, scores, 0.0), axis=0, keepdims=True))
        cand = jnp.where(hit, -jnp.inf, cand)
    w = jnp.concatenate(w_rows, axis=0)
    gate_ref[...] = w / jnp.sum(w, axis=0, keepdims=True) * ROUTED_SCALE
    eidx_ref[...] = jnp.concatenate(e_rows, axis=0)

    onehot = jnp.zeros(cand.shape, F32)
    for hit in hits:
        onehot = onehot + hit.astype(F32)
    before = _dot(onehot.astype(BF16), tri_ref[...]) + carry[:, 0:1]
    rank_ref[...] = jnp.concatenate(
        [jnp.sum(jnp.where(hit, before, 0.0), axis=0, keepdims=True) for hit in hits], axis=0).astype(jnp.int32)
    carry[...] = carry[...] + jnp.sum(onehot, axis=1, keepdims=True)
    cnt_ref[...] = carry[...]


def _out_projection(oa, ob, sg, x2, proj_a, proj_b, w_out, ln_g, ln_b, w_router, router_bias, s_gate, s_up, s_down):
    t = x2.shape[0]
    tm = OUT_TM
    pair_rows = lambda p: p.reshape(N_GROUPS, GROUP, HEAD_DIM, -1).transpose(1, 0, 2, 3).reshape(p.shape)
    pa = pair_rows(proj_a).astype(MXU_DTYPE)
    pb = pair_rows(proj_b).astype(MXU_DTYPE)
    tri = jnp.asarray(np.triu(np.ones((tm, tm), np.float32), 1), BF16)
    row = lambda i: (i, 0)
    fixed = lambda i: (0, 0)
    col = lambda i: (0, i)
    outs = pl.pallas_call(
        _outproj_kernel,
        grid=(t // tm,),
        in_specs=[pl.BlockSpec((tm, 4 * LANES), row), pl.BlockSpec((tm, 4 * LANES), row),
                  pl.BlockSpec((tm, 2 * D_MODEL), row), pl.BlockSpec((tm, D_MODEL), row),
                  pl.BlockSpec((4 * LANES, D_MODEL), fixed), pl.BlockSpec((4 * LANES, D_MODEL), fixed),
                  pl.BlockSpec((D_MODEL, D_MODEL), fixed),
                  pl.BlockSpec((1, D_MODEL), fixed), pl.BlockSpec((1, D_MODEL), fixed),
                  pl.BlockSpec((N_EXPERTS, D_MODEL), fixed), pl.BlockSpec((N_EXPERTS, LANES), fixed),
                  pl.BlockSpec((D_MODEL, 2 * SHARED_HIDDEN), fixed), pl.BlockSpec((SHARED_HIDDEN, D_MODEL), fixed),
                  pl.BlockSpec((tm, tm), fixed)],
        out_specs=[pl.BlockSpec((tm, D_MODEL // 2), row), pl.BlockSpec((tm, D_MODEL), row),
                   pl.BlockSpec((TOP_K, tm), col), pl.BlockSpec((TOP_K, tm), col), pl.BlockSpec((TOP_K, tm), col),
                   pl.BlockSpec((N_EXPERTS, LANES), fixed)],
        out_shape=[jax.ShapeDtypeStruct((t, D_MODEL // 2), jnp.uint32), jax.ShapeDtypeStruct((t, D_MODEL), F32),
                   jax.ShapeDtypeStruct((TOP_K, t), jnp.int32), jax.ShapeDtypeStruct((TOP_K, t), F32),
                   jax.ShapeDtypeStruct((TOP_K, t), jnp.int32), jax.ShapeDtypeStruct((N_EXPERTS, LANES), F32)],
        scratch_shapes=[pltpu.VMEM((N_EXPERTS, LANES), F32)],
        compiler_params=pltpu.CompilerParams(dimension_semantics=("arbitrary",), vmem_limit_bytes=VMEM_LIMIT),
        name="out_projection_router",
    )(oa, ob, sg, x2, pa, pb, w_out.astype(MXU_DTYPE), ln_g.reshape(1, -1), ln_b.reshape(1, -1),
      w_router.T.astype(MXU_DTYPE), jnp.broadcast_to(router_bias.astype(F32)[:, None], (N_EXPERTS, LANES)),
      jnp.concatenate([s_gate, s_up], axis=1).astype(MXU_DTYPE), s_down.astype(MXU_DTYPE), tri)
    return outs


def _rows_to_tiles(x):
    return pltpu.einshape("cml->mcl", jnp.stack(_lane_tiles(x), axis=0))


def _tiles_to_rows(x3):
    xt = pltpu.einshape("mcl->cml", x3)
    return jnp.concatenate([xt[c] for c in range(xt.shape[0])], axis=1)


def _dispatch_kernel(zstart_ref, cnt_ref, dest_ref, h2_ref, xs_ref, h_ref, zeros, sem, zsem):
    step = pl.program_id(0)
    tm = h2_ref.shape[0]
    slot = step % 2
    h_ref[slot] = _rows_to_tiles(h2_ref[...])

    @pl.when(step == 0)
    def _():
        zeros[...] = jnp.zeros(zeros.shape, zeros.dtype)

        def fill(e, c):
            @pl.when(cnt_ref[e] > 0)
            def _():
                pltpu.make_async_copy(zeros, xs_ref.at[pl.ds(zstart_ref[e], MOE_BM)], zsem).start()
            return c

        def fill_done(e, c):
            @pl.when(cnt_ref[e] > 0)
            def _():
                pltpu.make_async_copy(zeros, xs_ref.at[pl.ds(zstart_ref[e], MOE_BM)], zsem).wait()
            return c
        lax.fori_loop(0, N_EXPERTS, fill, 0)
        lax.fori_loop(0, N_EXPERTS, fill_done, 0)

    def issue(t, c):
        for k in range(TOP_K):
            pltpu.make_async_copy(h_ref.at[slot, t], xs_ref.at[dest_ref[k, t]], sem.at[slot]).start(priority=k % 2)
        return c
    lax.fori_loop(0, tm, issue, 0)

    def wait_tile(s):
        for k in range(TOP_K):
            pltpu.make_async_copy(h_ref.at[s], xs_ref.at[pl.ds(0, tm)], sem.at[s]).wait()

    @pl.when(step > 0)
    def _():
        wait_tile(1 - slot)

    @pl.when(step + 1 == pl.num_programs(0))
    def _():
        wait_tile(slot)


def _dispatch(h, dest, zstart, counts, n_rows):
    t = h.shape[0]
    tm = DISP_TM
    return pl.pallas_call(
        _dispatch_kernel,
        grid_spec=pltpu.PrefetchScalarGridSpec(
            num_scalar_prefetch=2,
            grid=(t // tm,),
            in_specs=[pl.BlockSpec((TOP_K, tm), lambda i, *_: (0, i), memory_space=pltpu.SMEM),
                      pl.BlockSpec((tm, D_MODEL // 2), lambda i, *_: (i, 0))],
            out_specs=pl.BlockSpec(memory_space=pl.ANY),
            scratch_shapes=[pltpu.VMEM((2, tm) + PACKED_ROW_TILE, jnp.uint32),
                            pltpu.VMEM((MOE_BM,) + PACKED_ROW_TILE, jnp.uint32),
                            pltpu.SemaphoreType.DMA((2,)), pltpu.SemaphoreType.DMA(())]),
        out_shape=jax.ShapeDtypeStruct((n_rows,) + PACKED_ROW_TILE, jnp.uint32),
        compiler_params=pltpu.CompilerParams(dimension_semantics=("arbitrary",), vmem_limit_bytes=VMEM_LIMIT),
        name="moe_dispatch",
    )(zstart, counts, dest, h)


def _experts_kernel(blk_e_ref, nused_ref, xs_ref, wg_ref, wu_ref, wd_ref, ys_ref, wg_s, wu_s, wd_s):
    b = pl.program_id(0)
    prev = blk_e_ref[jnp.maximum(b - 1, 0)]

    @pl.when((b == 0) | (blk_e_ref[b] != prev))
    def _():
        wg_s[...] = _mx(wg_ref[0])
        wu_s[...] = _mx(wu_ref[0])
        wd_s[...] = _mx(wd_ref[0])

    @pl.when(b < nused_ref[0])
    def _():
        xb = _mx(jnp.concatenate(_unpack_bf16_pairs(_tiles_to_rows(xs_ref[...])), axis=1))
        hid = jax.nn.silu(_dot(xb, wg_s[...])) * _dot(xb, wu_s[...])
        ys_ref[...] = _rows_to_tiles(_pack_bf16_pairs(_dot(_mx(hid), wd_s[...])))

    @pl.when(b >= nused_ref[0])
    def _():
        ys_ref[...] = jnp.zeros(ys_ref.shape, ys_ref.dtype)


def _experts(xs, blk_e, nused, e_gate, e_up, e_down):
    n_rows = xs.shape[0]
    n_blocks = n_rows // MOE_BM
    xmap = lambda b, be, nu: (jnp.minimum(b, nu[0] - 1), 0, 0)
    wmap = lambda b, be, nu: (be[b], 0, 0)
    return pl.pallas_call(
        _experts_kernel,
        grid_spec=pltpu.PrefetchScalarGridSpec(
            num_scalar_prefetch=2,
            grid=(n_blocks,),
            in_specs=[pl.BlockSpec((MOE_BM,) + PACKED_ROW_TILE, xmap),
                      pl.BlockSpec((1, D_MODEL, EXPERT_HIDDEN), wmap),
                      pl.BlockSpec((1, D_MODEL, EXPERT_HIDDEN), wmap),
                      pl.BlockSpec((1, EXPERT_HIDDEN, D_MODEL), wmap)],
            out_specs=pl.BlockSpec((MOE_BM,) + PACKED_ROW_TILE,
                                   lambda b, be, nu: (jnp.where(b < nu[0], b, n_blocks - 1), 0, 0)),
            scratch_shapes=[pltpu.VMEM((D_MODEL, EXPERT_HIDDEN), MXU_DTYPE),
                            pltpu.VMEM((D_MODEL, EXPERT_HIDDEN), MXU_DTYPE),
                            pltpu.VMEM((EXPERT_HIDDEN, D_MODEL), MXU_DTYPE)]),
        out_shape=jax.ShapeDtypeStruct((n_rows,) + PACKED_ROW_TILE, jnp.uint32),
        compiler_params=pltpu.CompilerParams(dimension_semantics=("arbitrary",), vmem_limit_bytes=VMEM_LIMIT),
        name="moe_experts",
    )(blk_e, nused, xs, e_gate, e_up, e_down)


def _combine_kernel(dest_ref, dest_next_ref, gate_ref, base_ref, g2_ref, b2_ref, ys_ref, out_ref, buf, ysum, sem):
    step = pl.program_id(0)
    tm = base_ref.shape[0]
    slot = step % 2

    sub = ROW_TILE[0]

    def gather_rows(d_ref, s, t0):
        for u in range(sub):
            for k in range(TOP_K):
                pltpu.make_async_copy(ys_ref.at[d_ref[(t0 + u) * TOP_K + k]], buf.at[s, k * tm + t0 + u],
                                      sem.at[s]).start(priority=k % 2)

    def combine_rows(t0):
        y = base_ref[pl.ds(t0, sub), :]
        gates = gate_ref[pl.ds(t0, sub), :]
        for k in range(TOP_K):
            words = _tiles_to_rows(buf[slot, pl.ds(k * tm + t0, sub)])
            y = y + gates[:, k:k + 1] * jnp.concatenate(_unpack_bf16_pairs(words), axis=1)
        ysum[pl.ds(t0, sub), :] = y

    def for_token_groups(body):
        def trip(g, c):
            body(pl.multiple_of(g * sub, sub))
            return c
        lax.fori_loop(0, tm // sub, trip, 0)

    @pl.when(step == 0)
    def _():
        for_token_groups(lambda t0: gather_rows(dest_ref, 0, t0))

    pltpu.make_async_copy(ys_ref.at[pl.ds(0, tm * TOP_K)], buf.at[slot], sem.at[slot]).wait()

    @pl.when(step + 1 < pl.num_programs(0))
    def _():
        def both(t0):
            gather_rows(dest_next_ref, 1 - slot, t0)
            combine_rows(t0)
        for_token_groups(both)

    @pl.when(step + 1 == pl.num_programs(0))
    def _():
        for_token_groups(combine_rows)

    out_ref[...] = _layer_norm(ysum[...], g2_ref[...], b2_ref[...])


def _combine(ys3, dest, gate, base, ln_g, ln_b):
    t = base.shape[0]
    tm = COMB_TM
    n_tiles = t // tm
    dest_tk = dest.T.reshape(-1)
    return pl.pallas_call(
        _combine_kernel,
        grid=(n_tiles,),
        in_specs=[pl.BlockSpec((tm * TOP_K,), lambda i: (i,), memory_space=pltpu.SMEM),
                  pl.BlockSpec((tm * TOP_K,), lambda i: (jnp.minimum(i + 1, n_tiles - 1),), memory_space=pltpu.SMEM),
                  pl.BlockSpec((tm, TOP_K), lambda i: (i, 0)),
                  pl.BlockSpec((tm, D_MODEL), lambda i: (i, 0)),
                  pl.BlockSpec((1, D_MODEL), lambda i: (0, 0)),
                  pl.BlockSpec((1, D_MODEL), lambda i: (0, 0)),
                  pl.BlockSpec(memory_space=pl.ANY)],
        out_specs=pl.BlockSpec((tm, D_MODEL), lambda i: (i, 0)),
        out_shape=jax.ShapeDtypeStruct((t, D_MODEL), F32),
        scratch_shapes=[pltpu.VMEM((2, tm * TOP_K) + PACKED_ROW_TILE, jnp.uint32), pltpu.VMEM((tm, D_MODEL), F32),
                        pltpu.SemaphoreType.DMA((2,))],
        compiler_params=pltpu.CompilerParams(dimension_semantics=("arbitrary",), vmem_limit_bytes=VMEM_LIMIT),
        name="moe_combine",
    )(dest_tk, dest_tk, gate.T, base, ln_g.reshape(1, -1), ln_b.reshape(1, -1), ys3)


def _dest_kernel(pstart_ref, eidx_ref, rank_ref, dest_ref):
    eidx = eidx_ref[...]

    unroll = 8

    def body(g, dest):
        for u in range(unroll):
            e = g * unroll + u
            dest = dest + jnp.where(eidx == e, pstart_ref[e], 0)
        return dest
    dest_ref[...] = lax.fori_loop(0, N_EXPERTS // unroll, body, rank_ref[...])


def _dest_rows(pstarts, eidx, rank):
    t = eidx.shape[1]
    tl = 2048
    spec = pl.BlockSpec((TOP_K, tl), lambda i, *_: (0, i))
    return pl.pallas_call(
        _dest_kernel,
        grid_spec=pltpu.PrefetchScalarGridSpec(num_scalar_prefetch=1, grid=(t // tl,), in_specs=[spec, spec],
                                               out_specs=spec),
        out_shape=jax.ShapeDtypeStruct(eidx.shape, jnp.int32),
        compiler_params=pltpu.CompilerParams(dimension_semantics=("arbitrary",)),
        name="moe_dest_rows",
    )(pstarts, eidx, rank)


def _moe_layout(eidx, rank, counts):
    n_assign = eidx.size
    n_blocks = (n_assign + N_EXPERTS * (MOE_BM - 1)) // MOE_BM
    padded = (counts + MOE_BM - 1) // MOE_BM * MOE_BM
    pends = jnp.cumsum(padded)
    pstarts = (pends - padded).astype(jnp.int32)
    dest = _dest_rows(pstarts, eidx, rank)
    block_row = jnp.arange(n_blocks, dtype=jnp.int32) * MOE_BM
    blk_e = jnp.minimum(jnp.sum(pends[None, :] <= block_row[:, None], axis=1), N_EXPERTS - 1).astype(jnp.int32)
    nused = (pends[-1:] // MOE_BM).astype(jnp.int32)
    zstart = jnp.maximum(pends - MOE_BM, 0).astype(jnp.int32)
    return dest.astype(jnp.int32), blk_e, nused, zstart, n_blocks * MOE_BM


def _layer(x, w_in, cmp_pe, cmp_w1, cmp_b1, cmp_w2, sinks, bias_table, proj_a, proj_b, w_out, ln1_g, ln1_b,
           w_router, router_bias, e_gate, e_up, e_down, s_gate, s_up, s_down, ln2_g, ln2_b):
    bsz, seq, d = x.shape
    x2 = x.reshape(bsz * seq, d)
    proj = _in_projection(x2, w_in)
    kvcmp = _compress(proj['kc'], proj['vc'], bsz, seq, cmp_pe, cmp_w1, cmp_b1, cmp_w2)
    oa, ob = _attention(proj, kvcmp, sinks, bias_table, bsz, seq)
    h, base, eidx, gate, rank, cnt = _out_projection(oa, ob, proj['sg'], x2, proj_a, proj_b, w_out, ln1_g, ln1_b,
                                                     w_router, router_bias, s_gate, s_up, s_down)
    counts = cnt[:, 0].astype(jnp.int32)
    dest, blk_e, nused, zstart, n_rows = _moe_layout(eidx, rank, counts)
    xs = _dispatch(h, dest, zstart, counts, n_rows)
    ys = _experts(xs, blk_e, nused, e_gate, e_up, e_down)
    out = _combine(ys, dest, gate, base, ln2_g, ln2_b)
    return out.reshape(bsz, seq, d)


def kernel(x, w_in, cmp_pe, cmp_w1, cmp_b1, cmp_w2, attn_sinks, rel_bias_table, proj_a, proj_b, w_out, ln1_g, ln1_b,
           w_router, router_bias, expert_w_gate, expert_w_up, expert_w_down, shared_w_gate, shared_w_up,
           shared_w_down, ln2_g, ln2_b):
    h = x
    for l in range(DEPTH):
        h = _layer(h, w_in[l], cmp_pe[l], cmp_w1[l], cmp_b1[l], cmp_w2[l], attn_sinks[l], rel_bias_table, proj_a[l],
                   proj_b[l], w_out[l], ln1_g[l], ln1_b[l], w_router[l], router_bias[l], expert_w_gate[l],
                   expert_w_up[l], expert_w_down[l], shared_w_gate[l], shared_w_up[l], shared_w_down[l], ln2_g[l],
                   ln2_b[l])
    return h
```

```python
import functools
import math

import numpy as np
import jax
import jax.numpy as jnp
from jax import lax
from jax.experimental import pallas as pl
from jax.experimental.pallas import tpu as pltpu

F32 = jnp.float32
BF16 = jnp.bfloat16
MXU_DTYPE = jnp.bfloat16

D_MODEL = 1024
HEAD_DIM = 64
ATTN_SCALE = HEAD_DIM ** -0.5
LOG2E = math.log2(math.e)
Q_BLOCK = 128
N_HEADS = 8
N_GROUPS = 2
GROUP = 4
CMP_BLOCK = 32
CMP_STRIDE = 16
CMP_HIDDEN = 128
SEL_BLOCK = 64
SEL_TOP_N = 8
SEL_INIT_BLOCKS = 1
SEL_LOCAL_BLOCKS = 2
NSA_WINDOW = 512
SWA_WINDOW = 128
REL_BUCKETS = 32
REL_MAX_DIST = 128
N_EXPERTS = 256
TOP_K = 8
EXPERT_HIDDEN = 256
SHARED_HIDDEN = 256
N_EXPERT_GROUPS = 8
TOPK_EXPERT_GROUPS = 4
ROUTED_SCALE = 2.5
LN_EPS = 1e-5
DEPTH = 1
DN_ALPHA = (2 * DEPTH) ** 0.25

NEG = -1e30
LANES = 128
ROW_TILE = (8, LANES)
PACKED_ROW_TILE = (4, LANES)
CMP_FRONT = 16
CMP_NEAR = LANES
SEL_CHUNK = 1024
QB_PER_STEP = 1
VMEM_LIMIT = 56 * 1024 * 1024

IN_TM = 1024
OUT_TM = 512
MOE_BM = 512
DISP_TM = 512
COMB_TM = 512


def _dot(a, b):
    return jnp.dot(a, b, preferred_element_type=F32)


def _dot_nt(a, b):
    return lax.dot_general(a, b, (((1,), (1,)), ((), ())), preferred_element_type=F32)


def _mx(a):
    return a.astype(MXU_DTYPE)


def _pack_bf16_pairs(x):
    half = x.shape[1] // 2
    bits = lax.bitcast_convert_type(x.astype(BF16).astype(F32), jnp.uint32)
    return (bits[:, half:] & jnp.uint32(0xFFFF0000)) | (bits[:, :half] >> 16)


def _unpack_bf16_pairs(words):
    return (lax.bitcast_convert_type(words << 16, F32),
            lax.bitcast_convert_type(words & jnp.uint32(0xFFFF0000), F32))


_IN_COLS = (('qa', 512), ('qb', 512), ('kc', 128), ('vc', 128), ('ks', 128), ('vs', 128), ('kw', 128),
            ('vw', 128), ('kb', 128), ('vb', 128), ('ga', 128), ('sg', 2048))


def _inproj_kernel(x_ref, w_ref, qa_ref, qb_ref, kc_ref, vc_ref, ks_ref, vs_ref, kw_ref, vw_ref, kb_ref, vb_ref,
                   ga_ref, sg_ref):
    xb = _mx(x_ref[...])
    outs = dict(qa=qa_ref, qb=qb_ref, kc=kc_ref, vc=vc_ref, ks=ks_ref, vs=vs_ref, kw=kw_ref, vw=vw_ref,
                kb=kb_ref, vb=vb_ref, ga=ga_ref, sg=sg_ref)
    tiles = [(name, c) for name, width in _IN_COLS for c in range(0, width, LANES)]
    chunk = 4
    for t0 in range(0, len(tiles), chunk):
        group = tiles[t0:t0 + chunk]
        y = _dot(xb, w_ref[:, t0 * LANES:(t0 + len(group)) * LANES])
        for j, (name, c) in enumerate(group):
            yj = y[:, j * LANES:(j + 1) * LANES]
            if name in ('ga', 'sg'):
                yj = jax.nn.sigmoid(yj)
            outs[name][:, c:c + LANES] = yj.astype(outs[name].dtype)


def _pair_head_columns(w):
    return w.reshape(w.shape[0], N_GROUPS, GROUP, HEAD_DIM).transpose(0, 2, 1, 3).reshape(w.shape[0], -1)


def _in_projection(x2, w_in):
    t = x2.shape[0]
    sizes = (512, 128, 128, 128, 128, 128, 128, 24, 512, 128, 128, 1024, 1024)
    offs = np.cumsum((0,) + sizes)
    part = [w_in[:, offs[k]:offs[k + 1]] for k in range(len(sizes))]
    w_qa, w_kc, w_vc, w_ks, w_vs, w_kw, w_vw, w_g, w_qb, w_kb, w_vb, w_gate_a, w_gate_b = part
    w_qa = _pair_head_columns(w_qa) * (ATTN_SCALE * LOG2E)
    w_qb = _pair_head_columns(w_qb) * (ATTN_SCALE * LOG2E)
    w_ga = w_g.reshape(-1, N_GROUPS, GROUP, 3).transpose(0, 3, 2, 1).reshape(-1, 24)
    w_ga = jnp.pad(w_ga, ((0, 0), (0, LANES - 24)))
    w_all = jnp.concatenate([w_qa, w_qb, w_kc, w_vc, w_ks, w_vs, w_kw, w_vw, w_kb, w_vb, w_ga, w_gate_a, w_gate_b],
                            axis=1).astype(MXU_DTYPE)
    n_all = w_all.shape[1]
    out_shape = []
    out_specs = []
    for name, width in _IN_COLS:
        dt = F32 if name == 'ga' else BF16
        out_shape.append(jax.ShapeDtypeStruct((t, width), dt))
        out_specs.append(pl.BlockSpec((IN_TM, width), lambda i: (i, 0)))
    outs = pl.pallas_call(
        _inproj_kernel,
        grid=(t // IN_TM,),
        in_specs=[pl.BlockSpec((IN_TM, D_MODEL), lambda i: (i, 0)),
                  pl.BlockSpec((D_MODEL, n_all), lambda i: (0, 0))],
        out_specs=out_specs,
        out_shape=out_shape,
        compiler_params=pltpu.CompilerParams(dimension_semantics=("arbitrary",), vmem_limit_bytes=VMEM_LIMIT),
        name="in_projection",
    )(x2, w_all)
    return dict(zip([n for n, _ in _IN_COLS], outs))


def _compress_kernel(tok_ref, w1_ref, pe_ref, w1o_ref, b1_ref, w2_ref, out_ref):
    n_chunks = tok_ref.shape[2]
    ab = _dot(tok_ref[0, 0], w1_ref[0])
    a = ab[:, :2 * CMP_HIDDEN]
    b_next = pltpu.roll(ab[:, 2 * CMP_HIDDEN:], n_chunks - 1, 0)
    cb = _dot(_mx(pe_ref[0]), _mx(w1o_ref[0]))[0:1, :] + b1_ref[0]
    cb2 = jnp.concatenate([cb, cb], axis=1)
    hid = jax.nn.gelu(a + b_next + cb2)
    out = _dot(_mx(hid), w2_ref[0])
    row = lax.broadcasted_iota(jnp.int32, out.shape, 0)
    out = jnp.where(row < n_chunks - 1, out, 0.0)
    out_ref[0, 0, 0:CMP_FRONT, :] = jnp.zeros((CMP_FRONT, LANES), F32)
    out_ref[0, 0, CMP_FRONT:CMP_FRONT + n_chunks, :] = out
    out_ref[0, 0, CMP_FRONT + n_chunks:, :] = jnp.zeros((CMP_NEAR - CMP_FRONT, LANES), F32)


def _compress(kc, vc, bsz, seq, cmp_pe, cmp_w1, cmp_b1, cmp_w2):
    n_chunks = seq // CMP_STRIDE
    tok = jnp.stack([kc, vc]).reshape(2, bsz, n_chunks, CMP_STRIDE * LANES)
    eye = jnp.eye(N_GROUPS, dtype=F32)
    w1r = cmp_w1.reshape(2, 2, CMP_STRIDE, HEAD_DIM, CMP_HIDDEN)
    w1 = jnp.einsum('khjdn,gG->kjgdhGn', w1r, eye).reshape(2, CMP_STRIDE * LANES, 4 * CMP_HIDDEN).astype(MXU_DTYPE)
    w2 = jnp.einsum('knd,gG->kgnGd', cmp_w2, eye).reshape(2, 2 * CMP_HIDDEN, LANES).astype(MXU_DTYPE)
    pe = jnp.pad(cmp_pe.reshape(2, 1, CMP_BLOCK * HEAD_DIM), ((0, 0), (0, 7), (0, 0)))
    b1 = cmp_b1.reshape(2, 1, CMP_HIDDEN)
    rows = CMP_FRONT + n_chunks + CMP_NEAR - CMP_FRONT
    return pl.pallas_call(
        _compress_kernel,
        grid=(2, bsz),
        in_specs=[pl.BlockSpec((1, 1, n_chunks, CMP_STRIDE * LANES), lambda k, b: (k, b, 0, 0)),
                  pl.BlockSpec((1, CMP_STRIDE * LANES, 4 * CMP_HIDDEN), lambda k, b: (k, 0, 0)),
                  pl.BlockSpec((1, 8, CMP_BLOCK * HEAD_DIM), lambda k, b: (k, 0, 0)),
                  pl.BlockSpec((1, CMP_BLOCK * HEAD_DIM, CMP_HIDDEN), lambda k, b: (k, 0, 0)),
                  pl.BlockSpec((1, 1, CMP_HIDDEN), lambda k, b: (k, 0, 0)),
                  pl.BlockSpec((1, 2 * CMP_HIDDEN, LANES), lambda k, b: (k, 0, 0))],
        out_specs=pl.BlockSpec((1, 1, rows, LANES), lambda k, b: (k, b, 0, 0)),
        out_shape=jax.ShapeDtypeStruct((2, bsz, rows, LANES), F32),
        compiler_params=pltpu.CompilerParams(dimension_semantics=("arbitrary", "arbitrary"),
                                             vmem_limit_bytes=VMEM_LIMIT),
        name="nsa_compress",
    )(tok, w1, pe, cmp_w1, b1, w2)


def _stack_heads(q_ref, dst):
    lo = lax.broadcasted_iota(jnp.int32, (Q_BLOCK, LANES), 1) < HEAD_DIM
    for r in range(GROUP):
        qr = q_ref[:, r * LANES:(r + 1) * LANES].astype(dst.dtype)
        z = jnp.zeros_like(qr)
        dst[(2 * r) * Q_BLOCK:(2 * r + 1) * Q_BLOCK, :] = jnp.where(lo, qr, z)
        dst[(2 * r + 1) * Q_BLOCK:(2 * r + 2) * Q_BLOCK, :] = jnp.where(lo, z, qr)


def _pair_heads(o, r):
    lo = lax.broadcasted_iota(jnp.int32, (Q_BLOCK, LANES), 1) < HEAD_DIM
    return jnp.where(lo, o[(2 * r) * Q_BLOCK:(2 * r + 1) * Q_BLOCK], o[(2 * r + 1) * Q_BLOCK:(2 * r + 2) * Q_BLOCK])


def _lane_tiles(x):
    return [x[:, t * LANES:(t + 1) * LANES] for t in range(x.shape[1] // LANES)]


def _row_max(tiles):
    mx = tiles[0]
    for t in tiles[1:]:
        mx = jnp.maximum(mx, t)
    return jnp.broadcast_to(jnp.max(mx, axis=1, keepdims=True), mx.shape)


def _with_ones(v):
    return jnp.concatenate([v, jnp.ones(v.shape, v.dtype)], axis=1)


def _block_of_key(n_keys, first_block):
    b = lax.broadcasted_iota(jnp.int32, (LANES, n_keys), 0)
    k = lax.broadcasted_iota(jnp.int32, (LANES, n_keys), 1)
    return (b == (k // SEL_BLOCK) + first_block).astype(MXU_DTYPE)


def _select_blocks_t(imp_t, i, n_top):
    blk = lax.broadcasted_iota(jnp.int32, imp_t.shape, 0)
    qcol = lax.broadcasted_iota(jnp.int32, imp_t.shape, 1)
    back = (2 * i + (qcol >= SEL_BLOCK).astype(jnp.int32)) - blk
    sel = (back >= 0) & ((blk < SEL_INIT_BLOCKS) | (back < SEL_LOCAL_BLOCKS))
    cand = jnp.where((back >= SEL_LOCAL_BLOCKS) & (blk >= SEL_INIT_BLOCKS), imp_t, -1.0)
    blk_f = blk.astype(F32)
    for _ in range(n_top - SEL_INIT_BLOCKS - SEL_LOCAL_BLOCKS):
        m = jnp.max(cand, axis=0, keepdims=True)
        idx = jnp.min(jnp.where(cand == m, blk_f, float(LANES)), axis=0, keepdims=True)
        hit = blk_f == idx
        sel = sel | (hit & (m >= 0.0))
        cand = jnp.where(hit, -2.0, cand)
    return sel


def _query_block(i, sink_ref, qa_ref, qb_ref, ga_ref, kcmp_ref, vcmp_ref, ks_ref, vs_ref, kw_ref, vw_ref, kb_ref,
                 vb_ref, cmat_ref, tnear_ref, tsel_ref, twin_ref, tswa_ref, oa_ref, ob_ref,
                 qall, qball, mneg, mneg_far, m_s, acc_s, s_buf, oa_acc, qmask, n_far, n_top):
    rows = N_HEADS * Q_BLOCK
    half = rows // 2
    halves = (slice(0, half), slice(half, rows))
    _stack_heads(qa_ref, qall)
    _stack_heads(qb_ref, qball)
    nstart = pl.multiple_of(i * Q_BLOCK, Q_BLOCK)
    lo = lax.broadcasted_iota(jnp.int32, (Q_BLOCK, LANES), 1) < HEAD_DIM
    gates = ga_ref[...]

    def gate_tile(c, r):
        return jnp.where(lo, gates[:, c * 8 + 2 * r:c * 8 + 2 * r + 1], gates[:, c * 8 + 2 * r + 1:c * 8 + 2 * r + 2])

    def softmax_pv(s_tiles, v1, fix_max=None):
        m = _row_max(s_tiles)
        if fix_max is not None:
            m = fix_max(m)
        e = [jnp.exp2(t - m) for t in s_tiles]
        return e, m, _dot(_mx(jnp.concatenate(e, axis=1)), v1)

    off = pl.multiple_of(i * (Q_BLOCK // CMP_STRIDE), 8)
    k_cmp = _mx(jnp.concatenate([kcmp_ref[0, 0, 0:n_far, :], kcmp_ref[0, 0, pl.ds(off, CMP_NEAR), :]], axis=0))
    v_cmp = _with_ones(_mx(jnp.concatenate([vcmp_ref[0, 0, 0:n_far, :], vcmp_ref[0, 0, pl.ds(off, CMP_NEAR), :]],
                                           axis=0)))
    colf = lax.broadcasted_iota(jnp.int32, (1, n_far), 1)
    coln = lax.broadcasted_iota(jnp.int32, (1, CMP_NEAR), 1)
    col_ok = jnp.concatenate([(colf >= CMP_FRONT) & (colf < off), coln + off >= CMP_FRONT], axis=1)
    mask_c = jnp.where(col_ok, 0.0, NEG)
    no_key = lambda m: jnp.where(m > 0.5 * NEG, m, 0.0)
    p_cmp, o_c = [], []
    for rs in halves:
        tiles = _lane_tiles(_dot_nt(qall[rs, :], k_cmp) + mask_c)
        tiles[-1] = tiles[-1] + tnear_ref[rs, :]
        e, _, ov = softmax_pv(tiles, v_cmp, no_key)
        inv = 1.0 / jnp.maximum(ov[:, LANES:], 1e-30)
        o_c.append(ov[:, :LANES] * inv)
        p_cmp.append([t * inv for t in e])
    o_c = jnp.concatenate(o_c, axis=0)
    yield None

    def far_start(j):
        return pl.multiple_of(Q_BLOCK + j * SEL_CHUNK, Q_BLOCK)

    def far_logits(j, slot, masked):
        kc = _mx(ks_ref[0, pl.ds(far_start(j), SEL_CHUNK), :])
        if masked:
            key = lax.broadcasted_iota(jnp.int32, (SEL_CHUNK, LANES), 0)
            blk = lax.broadcasted_iota(jnp.int32, (SEL_CHUNK, LANES), 1)
            one_hot = (blk == key // SEL_BLOCK + j * (SEL_CHUNK // SEL_BLOCK)).astype(MXU_DTYPE)
            kc = jnp.concatenate([kc, one_hot], axis=1)
        for rs in halves:
            s_buf[slot, rs, :] = _dot_nt(qmask[rs, :] if masked else qall[rs, :], kc)

    far_logits(0, 0, False)

    blkcol = lax.broadcasted_iota(jnp.int32, (Q_BLOCK, LANES), 1)
    n_tiles = len(p_cmp[0])
    for g in range(N_GROUPS):
        imp = jnp.zeros((Q_BLOCK, LANES), F32)
        for t in range(n_tiles):
            pg = sum(p_cmp[r // 2][t][(2 * (r % 2) + g) * Q_BLOCK:(2 * (r % 2) + g + 1) * Q_BLOCK]
                     for r in range(GROUP))
            if t < n_tiles - 1:
                cm = _mx(cmat_ref[t * LANES:(t + 1) * LANES, :])
            else:
                cm = _mx(cmat_ref[pl.ds(off, CMP_NEAR), :])
            hi = _mx(pg)
            low = _mx(pg - hi.astype(F32))
            imp = imp + _dot(hi, cm) + _dot(low, cm)
        sel = _select_blocks_t(imp.T, i, n_top)
        neg = jnp.where(sel, 0.0, NEG).T
        mneg[g * Q_BLOCK:(g + 1) * Q_BLOCK, :] = neg.astype(mneg.dtype)
        neg_far = jnp.where(blkcol < 2 * (i - 1), neg, NEG).astype(mneg.dtype)
        mneg_far[g * Q_BLOCK:(g + 1) * Q_BLOCK, :] = neg_far
        for r in range(GROUP):
            qmask[(2 * r + g) * Q_BLOCK:(2 * r + g + 1) * Q_BLOCK, LANES:] = neg_far
    qmask[:, :LANES] = qall[...]

    yield None

    wpad = kw_ref.shape[1] - ks_ref.shape[1] + Q_BLOCK
    kwin = _mx(kw_ref[0, pl.ds(nstart, wpad + Q_BLOCK), :])
    vwin = _with_ones(_mx(vw_ref[0, pl.ds(nstart, wpad + Q_BLOCK), :]))
    colw = lax.broadcasted_iota(jnp.int32, (1, wpad + Q_BLOCK), 1)
    mask_w = jnp.where(colw + nstart >= wpad, 0.0, NEG)
    o_w = []
    for rs in halves:
        _, _, ov = softmax_pv(_lane_tiles(_dot_nt(qall[rs, :], kwin) + twin_ref[rs, :] + mask_w), vwin)
        o_w.append(ov[:, :LANES] / ov[:, LANES:])
    o_w = jnp.concatenate(o_w, axis=0)
    for r in range(GROUP):
        oa_acc[:, r * LANES:(r + 1) * LANES] = (gate_tile(0, r) * _pair_heads(o_c, r)
                                                + gate_tile(2, r) * _pair_heads(o_w, r))

    yield None

    bpad = kb_ref.shape[1] - ks_ref.shape[1] + Q_BLOCK
    kwin = _mx(kb_ref[0, pl.ds(nstart, bpad + Q_BLOCK), :])
    vwin = _with_ones(_mx(vb_ref[0, pl.ds(nstart, bpad + Q_BLOCK), :]))
    colb = lax.broadcasted_iota(jnp.int32, (1, bpad + Q_BLOCK), 1)
    mask_b = jnp.where(colb + nstart >= bpad, 0.0, NEG)
    o_b = []
    for hh, rs in enumerate(halves):
        sink = jnp.concatenate([jnp.full((Q_BLOCK, LANES), sink_ref[(h % 2) * GROUP + h // 2], F32)
                                for h in range(hh * N_HEADS // 2, (hh + 1) * N_HEADS // 2)], axis=0)
        _, m, ov = softmax_pv(_lane_tiles(_dot_nt(qball[rs, :], kwin) + tswa_ref[rs, :] + mask_b), vwin,
                              lambda m: jnp.maximum(m, sink))
        o_b.append(ov[:, :LANES] / (ov[:, LANES:] + jnp.exp2(sink - m)))
    o_b = jnp.concatenate(o_b, axis=0)
    for r in range(GROUP):
        ob_ref[:, r * LANES:(r + 1) * LANES] = _pair_heads(o_b, r).astype(ob_ref.dtype)

    yield None

    m_s[...] = jnp.full(m_s.shape, NEG, F32)
    acc_s[...] = jnp.zeros(acc_s.shape, F32)

    def flash_update(rs, s, v1):
        s_tiles = _lane_tiles(s)
        m_old = m_s[rs, :]
        m_new = jnp.maximum(m_old, _row_max(s_tiles))
        alpha = jnp.exp2(m_old - m_new)
        p = jnp.concatenate([jnp.exp2(t - m_new) for t in s_tiles], axis=1)
        acc_s[rs, :] = jnp.concatenate([alpha, alpha], axis=1) * acc_s[rs, :] + _dot(_mx(p), v1)
        m_s[rs, :] = m_new

    madd = _dot(mneg_far[...], _block_of_key(SEL_CHUNK, 0))
    for rs in halves:
        s_buf[0, rs, :] = s_buf[0, rs, :] + jnp.concatenate([madd] * (GROUP // 2), axis=0)

    def far_update(j):
        v1 = _with_ones(_mx(vs_ref[0, pl.ds(far_start(j), SEL_CHUNK), :]))
        for rs in halves:
            flash_update(rs, s_buf[j % 2, rs, :], v1)

    yield far_update, far_logits

    kc = _mx(ks_ref[0, pl.ds(nstart, 2 * Q_BLOCK), :])
    v1 = _with_ones(_mx(vs_ref[0, pl.ds(nstart, 2 * Q_BLOCK), :]))
    madd = _dot(mneg[...], _block_of_key(2 * Q_BLOCK, 2 * (i - 1)))
    col2 = lax.broadcasted_iota(jnp.int32, (1, 2 * Q_BLOCK), 1)
    mask_n = jnp.where((col2 < Q_BLOCK) & (i == 0), NEG, 0.0)
    for rs in halves:
        s = _dot_nt(qall[rs, :], kc) + jnp.concatenate([madd] * (GROUP // 2), axis=0) + tsel_ref[rs, :] + mask_n
        flash_update(rs, s, v1)
    acc = acc_s[...]
    o_s = acc[:, :LANES] / acc[:, LANES:]
    for r in range(GROUP):
        tile = oa_acc[:, r * LANES:(r + 1) * LANES] + gate_tile(1, r) * _pair_heads(o_s, r)
        oa_ref[:, r * LANES:(r + 1) * LANES] = tile.astype(oa_ref.dtype)
    yield None


def _attn_kernel(sink_ref, qa_ref, qb_ref, ga_ref, kcmp_ref, vcmp_ref, ks_ref, vs_ref, kw_ref, vw_ref, kb_ref,
                 vb_ref, cmat_ref, tnear_ref, tsel_ref, twin_ref, tswa_ref, oa_ref, ob_ref,
                 qall, qball, mneg, mneg_far, m_s, acc_s, s_buf, oa_acc, qmask, *, n_far, n_top):
    first = pl.program_id(1) * QB_PER_STEP
    blocks, steps = [], []
    for n in range(QB_PER_STEP):
        qrows = pl.ds(n * Q_BLOCK, Q_BLOCK)
        blk = _query_block(first + n, sink_ref, qa_ref.at[qrows, :], qb_ref.at[qrows, :], ga_ref.at[qrows, :],
                           kcmp_ref, vcmp_ref, ks_ref, vs_ref, kw_ref, vw_ref, kb_ref, vb_ref, cmat_ref, tnear_ref,
                           tsel_ref, twin_ref, tswa_ref, oa_ref.at[qrows, :], ob_ref.at[qrows, :],
                           qall.at[n], qball.at[n], mneg.at[n], mneg_far.at[n], m_s.at[n], acc_s.at[n], s_buf.at[n],
                           oa_acc.at[n], qmask.at[n], n_far, n_top)
        blocks.append(blk)
    steps = [next(blk) for blk in blocks]
    while steps[0] is None:
        steps = [next(blk) for blk in blocks]
    n_far_keys = jnp.maximum(first + QB_PER_STEP - 2, 0) * Q_BLOCK
    n_chunks = (n_far_keys + SEL_CHUNK - 1) // SEL_CHUNK

    def far_body(j, carry):
        for far_update, _ in steps:
            far_update(j)
        for _, far_logits in steps:
            far_logits(j + 1, (j + 1) % 2, True)
        return carry

    last = jnp.maximum(n_chunks - 1, 0)
    lax.fori_loop(0, last, far_body, 0)
    for far_update, _ in steps:
        far_update(last)
    for blk in blocks:
        next(blk)


def _rel_bucket_np(dist):
    n = np.maximum(dist, 0)
    max_exact = REL_BUCKETS // 2
    nf = np.maximum(n, 1).astype(np.float32)
    log_b = max_exact + (np.log(nf / max_exact) / math.log(REL_MAX_DIST / max_exact)
                         * (REL_BUCKETS - max_exact)).astype(np.int32)
    log_b = np.minimum(log_b, REL_BUCKETS - 1)
    return np.where(n < max_exact, n, log_b)


def _toeplitz_bias(tab, pad, width, window, shift_far):
    length = width + Q_BLOCK
    dist = pad + Q_BLOCK - 1 - np.arange(length)
    onehot = np.zeros((length, REL_BUCKETS), np.float32)
    onehot[np.arange(length), _rel_bucket_np(dist)] = 1.0
    vals = jnp.dot(jnp.asarray(onehot), tab, precision=lax.Precision.HIGHEST)
    if shift_far:
        vals = vals - tab[REL_BUCKETS - 1][None, :]
    vals = vals * LOG2E
    valid = (dist >= 0) & (dist < window)
    vals = jnp.where(jnp.asarray(valid)[:, None], vals, NEG).T
    skew = jnp.tile(vals, (1, Q_BLOCK))[:, :Q_BLOCK * (length - 1)].reshape(N_HEADS, Q_BLOCK, length - 1)
    return skew[:, :, Q_BLOCK - 1:Q_BLOCK - 1 + width].reshape(N_HEADS * Q_BLOCK, width).astype(F32)


def _attention(proj, kvcmp, sinks, bias_table, bsz, seq):
    assert seq % SEL_CHUNK == 0
    nq = seq // Q_BLOCK
    n_far = seq // CMP_STRIDE
    n_sel = seq // SEL_BLOCK
    n_top = min(SEL_TOP_N, n_sel)
    assert n_top >= SEL_INIT_BLOCKS + SEL_LOCAL_BLOCKS and n_sel <= LANES
    wpad = Q_BLOCK * (-(-(NSA_WINDOW - 1) // Q_BLOCK))
    bpad = Q_BLOCK * (-(-(SWA_WINDOW - 1) // Q_BLOCK))
    pair = lambda tab: tab.astype(F32).reshape(REL_BUCKETS, N_GROUPS, GROUP).transpose(0, 2, 1).reshape(REL_BUCKETS, -1)
    tab_a = pair(bias_table[:, :N_HEADS])
    tab_b = pair(bias_table[:, N_HEADS:])
    near_pad = CMP_STRIDE * CMP_FRONT - (CMP_BLOCK - 1)
    t_near = _toeplitz_bias(tab_a, near_pad, CMP_STRIDE * CMP_NEAR, 1 << 30, True)[:, ::CMP_STRIDE]
    t_sel = _toeplitz_bias(tab_a, Q_BLOCK, 2 * Q_BLOCK, 1 << 30, True)
    t_win = _toeplitz_bias(tab_a, wpad, wpad + Q_BLOCK, NSA_WINDOW, False)
    t_swa = _toeplitz_bias(tab_b, bpad, bpad + Q_BLOCK, SWA_WINDOW, False)
    n_rows = kvcmp.shape[2]
    cn = (np.arange(n_rows) - CMP_FRONT)[:, None] * CMP_STRIDE
    sj = np.arange(LANES)[None, :] * SEL_BLOCK
    cmat = ((cn < sj + SEL_BLOCK) & (cn + CMP_BLOCK > sj) & (cn >= 0) & (cn + CMP_BLOCK <= seq)
            & (sj < seq)).astype(np.float32)
    cmat = jnp.asarray(cmat, F32)
    padded = lambda name, p: jnp.pad(proj[name].reshape(bsz, seq, LANES), ((0, 0), (p, 0), (0, 0)))
    ks, vs = padded('ks', Q_BLOCK), padded('vs', Q_BLOCK)
    kw, vw = padded('kw', wpad), padded('vw', wpad)
    kb, vb = padded('kb', bpad), padded('vb', bpad)
    rows = N_HEADS * Q_BLOCK
    n_steps = nq // QB_PER_STEP
    qspec = pl.BlockSpec((QB_PER_STEP * Q_BLOCK, 4 * LANES), lambda b, i: (b * n_steps + i, 0))
    const2 = lambda shape: pl.BlockSpec(shape, lambda b, i: (0, 0))
    batch3 = lambda n: pl.BlockSpec((1, n, LANES), lambda b, i: (b, 0, 0))
    per_block = lambda shape, dtype: pltpu.VMEM((QB_PER_STEP,) + shape, dtype)
    kernel = functools.partial(_attn_kernel, n_far=n_far, n_top=n_top)
    return pl.pallas_call(
        kernel,
        grid=(bsz, n_steps),
        in_specs=[pl.BlockSpec(memory_space=pltpu.SMEM),
                  qspec, qspec,
                  pl.BlockSpec((QB_PER_STEP * Q_BLOCK, LANES), lambda b, i: (b * n_steps + i, 0)),
                  pl.BlockSpec((1, 1, n_rows, LANES), lambda b, i: (0, b, 0, 0)),
                  pl.BlockSpec((1, 1, n_rows, LANES), lambda b, i: (1, b, 0, 0)),
                  batch3(seq + Q_BLOCK), batch3(seq + Q_BLOCK),
                  batch3(seq + wpad), batch3(seq + wpad),
                  batch3(seq + bpad), batch3(seq + bpad),
                  const2((n_rows, LANES)),
                  const2((rows, CMP_NEAR)),
                  const2((rows, 2 * Q_BLOCK)),
                  const2((rows, wpad + Q_BLOCK)),
                  const2((rows, bpad + Q_BLOCK))],
        out_specs=[qspec, qspec],
        out_shape=[jax.ShapeDtypeStruct((bsz * seq, 4 * LANES), BF16)] * 2,
        scratch_shapes=[per_block((rows, LANES), MXU_DTYPE),
                        per_block((rows, LANES), MXU_DTYPE),
                        per_block((N_GROUPS * Q_BLOCK, LANES), MXU_DTYPE),
                        per_block((N_GROUPS * Q_BLOCK, LANES), MXU_DTYPE),
                        per_block((rows, LANES), F32),
                        per_block((rows, 2 * LANES), F32),
                        per_block((2, rows, SEL_CHUNK), F32),
                        per_block((Q_BLOCK, 4 * LANES), F32),
                        per_block((rows, 2 * LANES), MXU_DTYPE)],
        compiler_params=pltpu.CompilerParams(dimension_semantics=("arbitrary", "arbitrary"),
                                             vmem_limit_bytes=VMEM_LIMIT),
        name="attention",
    )(sinks.astype(F32) * LOG2E, proj['qa'], proj['qb'], proj['ga'], kvcmp, kvcmp, ks, vs, kw, vw, kb, vb,
      cmat, t_near, t_sel, t_win, t_swa)


def _layer_norm(y, g, b):
    mu = jnp.mean(y, axis=-1, keepdims=True)
    yc = y - mu
    var = jnp.mean(yc * yc, axis=-1, keepdims=True)
    return yc * lax.rsqrt(var + LN_EPS) * g + b


def _outproj_kernel(oa_ref, ob_ref, sg_ref, x_ref, pa_ref, pb_ref, wo_ref, g1_ref, b1_ref, wr_ref, rb_ref, sgu_ref,
                    sd_ref, tri_ref, h_ref, base_ref, eidx_ref, gate_ref, rank_ref, cnt_ref, carry):
    step = pl.program_id(0)
    tm = oa_ref.shape[0]

    @pl.when(step == 0)
    def _():
        carry[...] = jnp.zeros(carry.shape, F32)

    sg = sg_ref[...].astype(F32)
    merged = (sg[:, :D_MODEL] * _dot(_mx(oa_ref[...]), pa_ref[...])
              + sg[:, D_MODEL:] * _dot(_mx(ob_ref[...]), pb_ref[...]))
    mix = _dot(_mx(merged), wo_ref[...])
    h = _layer_norm(DN_ALPHA * x_ref[...] + mix, g1_ref[...], b1_ref[...])
    hb = _mx(h)
    h_ref[...] = _pack_bf16_pairs(h)

    gu = _dot(hb, sgu_ref[...])
    shared = _dot(_mx(jax.nn.silu(gu[:, :SHARED_HIDDEN]) * gu[:, SHARED_HIDDEN:]), sd_ref[...])
    base_ref[...] = DN_ALPHA * h + shared

    scores = jax.nn.sigmoid(_dot_nt(wr_ref[...], hb))
    choice = scores + rb_ref[:, 0:1]
    per_group = N_EXPERTS // N_EXPERT_GROUPS
    gs = []
    for g in range(N_EXPERT_GROUPS):
        cg = choice[g * per_group:(g + 1) * per_group]
        m1 = jnp.max(cg, axis=0, keepdims=True)
        is_m = cg == m1
        n_m = jnp.sum(is_m.astype(F32), axis=0, keepdims=True)
        m2 = jnp.max(jnp.where(is_m, -jnp.inf, cg), axis=0, keepdims=True)
        gs.append(m1 + jnp.where(n_m > 1.5, m1, m2))
    gs = jnp.concatenate(gs, axis=0)
    gid = lax.broadcasted_iota(jnp.int32, gs.shape, 0)
    beaten = jnp.zeros(gs.shape, jnp.int32)
    for g in range(N_EXPERT_GROUPS):
        other = gs[g:g + 1]
        beaten = beaten + ((other > gs) | ((other == gs) & (g < gid))).astype(jnp.int32)
    keep_g = beaten < TOPK_EXPERT_GROUPS
    keep = jnp.concatenate([jnp.broadcast_to(keep_g[g:g + 1], (per_group, tm)) for g in range(N_EXPERT_GROUPS)],
                           axis=0)
    cand = jnp.where(keep, choice, -jnp.inf)
    eid = lax.broadcasted_iota(jnp.int32, cand.shape, 0)
    hits = []
    e_rows = []
    w_rows = []
    for _ in range(TOP_K):
        m = jnp.max(cand, axis=0, keepdims=True)
        idx = jnp.min(jnp.where(cand == m, eid, N_EXPERTS), axis=0, keepdims=True)
        hit = eid == idx
        hits.append(hit)
        e_rows.append(idx)
        w_rows.append(jnp.sum(jnp.where(hit, scores, 0.0), axis=0, keepdims=True))
        cand = jnp.where(hit, -jnp.inf, cand)
    w = jnp.concatenate(w_rows, axis=0)
    gate_ref[...] = w / jnp.sum(w, axis=0, keepdims=True) * ROUTED_SCALE
    eidx_ref[...] = jnp.concatenate(e_rows, axis=0)

    onehot = jnp.zeros(cand.shape, F32)
    for hit in hits:
        onehot = onehot + hit.astype(F32)
    before = _dot(onehot.astype(BF16), tri_ref[...]) + carry[:, 0:1]
    rank_ref[...] = jnp.concatenate(
        [jnp.sum(jnp.where(hit, before, 0.0), axis=0, keepdims=True) for hit in hits], axis=0).astype(jnp.int32)
    carry[...] = carry[...] + jnp.sum(onehot, axis=1, keepdims=True)
    cnt_ref[...] = carry[...]


def _out_projection(oa, ob, sg, x2, proj_a, proj_b, w_out, ln_g, ln_b, w_router, router_bias, s_gate, s_up, s_down):
    t = x2.shape[0]
    tm = OUT_TM
    pair_rows = lambda p: p.reshape(N_GROUPS, GROUP, HEAD_DIM, -1).transpose(1, 0, 2, 3).reshape(p.shape)
    pa = pair_rows(proj_a).astype(MXU_DTYPE)
    pb = pair_rows(proj_b).astype(MXU_DTYPE)
    tri = jnp.asarray(np.triu(np.ones((tm, tm), np.float32), 1), BF16)
    row = lambda i: (i, 0)
    fixed = lambda i: (0, 0)
    col = lambda i: (0, i)
    outs = pl.pallas_call(
        _outproj_kernel,
        grid=(t // tm,),
        in_specs=[pl.BlockSpec((tm, 4 * LANES), row), pl.BlockSpec((tm, 4 * LANES), row),
                  pl.BlockSpec((tm, 2 * D_MODEL), row), pl.BlockSpec((tm, D_MODEL), row),
                  pl.BlockSpec((4 * LANES, D_MODEL), fixed), pl.BlockSpec((4 * LANES, D_MODEL), fixed),
                  pl.BlockSpec((D_MODEL, D_MODEL), fixed),
                  pl.BlockSpec((1, D_MODEL), fixed), pl.BlockSpec((1, D_MODEL), fixed),
                  pl.BlockSpec((N_EXPERTS, D_MODEL), fixed), pl.BlockSpec((N_EXPERTS, LANES), fixed),
                  pl.BlockSpec((D_MODEL, 2 * SHARED_HIDDEN), fixed), pl.BlockSpec((SHARED_HIDDEN, D_MODEL), fixed),
                  pl.BlockSpec((tm, tm), fixed)],
        out_specs=[pl.BlockSpec((tm, D_MODEL // 2), row), pl.BlockSpec((tm, D_MODEL), row),
                   pl.BlockSpec((TOP_K, tm), col), pl.BlockSpec((TOP_K, tm), col), pl.BlockSpec((TOP_K, tm), col),
                   pl.BlockSpec((N_EXPERTS, LANES), fixed)],
        out_shape=[jax.ShapeDtypeStruct((t, D_MODEL // 2), jnp.uint32), jax.ShapeDtypeStruct((t, D_MODEL), F32),
                   jax.ShapeDtypeStruct((TOP_K, t), jnp.int32), jax.ShapeDtypeStruct((TOP_K, t), F32),
                   jax.ShapeDtypeStruct((TOP_K, t), jnp.int32), jax.ShapeDtypeStruct((N_EXPERTS, LANES), F32)],
        scratch_shapes=[pltpu.VMEM((N_EXPERTS, LANES), F32)],
        compiler_params=pltpu.CompilerParams(dimension_semantics=("arbitrary",), vmem_limit_bytes=VMEM_LIMIT),
        name="out_projection_router",
    )(oa, ob, sg, x2, pa, pb, w_out.astype(MXU_DTYPE), ln_g.reshape(1, -1), ln_b.reshape(1, -1),
      w_router.T.astype(MXU_DTYPE), jnp.broadcast_to(router_bias.astype(F32)[:, None], (N_EXPERTS, LANES)),
      jnp.concatenate([s_gate, s_up], axis=1).astype(MXU_DTYPE), s_down.astype(MXU_DTYPE), tri)
    return outs


def _rows_to_tiles(x):
    return pltpu.einshape("cml->mcl", jnp.stack(_lane_tiles(x), axis=0))


def _tiles_to_rows(x3):
    xt = pltpu.einshape("mcl->cml", x3)
    return jnp.concatenate([xt[c] for c in range(xt.shape[0])], axis=1)


def _dispatch_kernel(zstart_ref, cnt_ref, dest_ref, h2_ref, xs_ref, h_ref, zeros, sem, zsem):
    step = pl.program_id(0)
    tm = h2_ref.shape[0]
    slot = step % 2
    h_ref[slot] = _rows_to_tiles(h2_ref[...])

    @pl.when(step == 0)
    def _():
        zeros[...] = jnp.zeros(zeros.shape, zeros.dtype)

        def fill(e, c):
            @pl.when(cnt_ref[e] > 0)
            def _():
                pltpu.make_async_copy(zeros, xs_ref.at[pl.ds(zstart_ref[e], MOE_BM)], zsem).start()
            return c

        def fill_done(e, c):
            @pl.when(cnt_ref[e] > 0)
            def _():
                pltpu.make_async_copy(zeros, xs_ref.at[pl.ds(zstart_ref[e], MOE_BM)], zsem).wait()
            return c
        lax.fori_loop(0, N_EXPERTS, fill, 0)
        lax.fori_loop(0, N_EXPERTS, fill_done, 0)

    def issue(t, c):
        for k in range(TOP_K):
            pltpu.make_async_copy(h_ref.at[slot, t], xs_ref.at[dest_ref[k, t]], sem.at[slot]).start(priority=k % 2)
        return c
    lax.fori_loop(0, tm, issue, 0)

    def wait_tile(s):
        for k in range(TOP_K):
            pltpu.make_async_copy(h_ref.at[s], xs_ref.at[pl.ds(0, tm)], sem.at[s]).wait()

    @pl.when(step > 0)
    def _():
        wait_tile(1 - slot)

    @pl.when(step + 1 == pl.num_programs(0))
    def _():
        wait_tile(slot)


def _dispatch(h, dest, zstart, counts, n_rows):
    t = h.shape[0]
    tm = DISP_TM
    return pl.pallas_call(
        _dispatch_kernel,
        grid_spec=pltpu.PrefetchScalarGridSpec(
            num_scalar_prefetch=2,
            grid=(t // tm,),
            in_specs=[pl.BlockSpec((TOP_K, tm), lambda i, *_: (0, i), memory_space=pltpu.SMEM),
                      pl.BlockSpec((tm, D_MODEL // 2), lambda i, *_: (i, 0))],
            out_specs=pl.BlockSpec(memory_space=pl.ANY),
            scratch_shapes=[pltpu.VMEM((2, tm) + PACKED_ROW_TILE, jnp.uint32),
                            pltpu.VMEM((MOE_BM,) + PACKED_ROW_TILE, jnp.uint32),
                            pltpu.SemaphoreType.DMA((2,)), pltpu.SemaphoreType.DMA(())]),
        out_shape=jax.ShapeDtypeStruct((n_rows,) + PACKED_ROW_TILE, jnp.uint32),
        compiler_params=pltpu.CompilerParams(dimension_semantics=("arbitrary",), vmem_limit_bytes=VMEM_LIMIT),
        name="moe_dispatch",
    )(zstart, counts, dest, h)


def _experts_kernel(blk_e_ref, nused_ref, xs_ref, wg_ref, wu_ref, wd_ref, ys_ref, wg_s, wu_s, wd_s):
    b = pl.program_id(0)
    prev = blk_e_ref[jnp.maximum(b - 1, 0)]

    @pl.when((b == 0) | (blk_e_ref[b] != prev))
    def _():
        wg_s[...] = _mx(wg_ref[0])
        wu_s[...] = _mx(wu_ref[0])
        wd_s[...] = _mx(wd_ref[0])

    @pl.when(b < nused_ref[0])
    def _():
        xb = _mx(jnp.concatenate(_unpack_bf16_pairs(_tiles_to_rows(xs_ref[...])), axis=1))
        hid = jax.nn.silu(_dot(xb, wg_s[...])) * _dot(xb, wu_s[...])
        ys_ref[...] = _rows_to_tiles(_pack_bf16_pairs(_dot(_mx(hid), wd_s[...])))

    @pl.when(b >= nused_ref[0])
    def _():
        ys_ref[...] = jnp.zeros(ys_ref.shape, ys_ref.dtype)


def _experts(xs, blk_e, nused, e_gate, e_up, e_down):
    n_rows = xs.shape[0]
    n_blocks = n_rows // MOE_BM
    xmap = lambda b, be, nu: (jnp.minimum(b, nu[0] - 1), 0, 0)
    wmap = lambda b, be, nu: (be[b], 0, 0)
    return pl.pallas_call(
        _experts_kernel,
        grid_spec=pltpu.PrefetchScalarGridSpec(
            num_scalar_prefetch=2,
            grid=(n_blocks,),
            in_specs=[pl.BlockSpec((MOE_BM,) + PACKED_ROW_TILE, xmap),
                      pl.BlockSpec((1, D_MODEL, EXPERT_HIDDEN), wmap),
                      pl.BlockSpec((1, D_MODEL, EXPERT_HIDDEN), wmap),
                      pl.BlockSpec((1, EXPERT_HIDDEN, D_MODEL), wmap)],
            out_specs=pl.BlockSpec((MOE_BM,) + PACKED_ROW_TILE,
                                   lambda b, be, nu: (jnp.where(b < nu[0], b, n_blocks - 1), 0, 0)),
            scratch_shapes=[pltpu.VMEM((D_MODEL, EXPERT_HIDDEN), MXU_DTYPE),
                            pltpu.VMEM((D_MODEL, EXPERT_HIDDEN), MXU_DTYPE),
                            pltpu.VMEM((EXPERT_HIDDEN, D_MODEL), MXU_DTYPE)]),
        out_shape=jax.ShapeDtypeStruct((n_rows,) + PACKED_ROW_TILE, jnp.uint32),
        compiler_params=pltpu.CompilerParams(dimension_semantics=("arbitrary",), vmem_limit_bytes=VMEM_LIMIT),
        name="moe_experts",
    )(blk_e, nused, xs, e_gate, e_up, e_down)


def _combine_kernel(dest_ref, dest_next_ref, gate_ref, base_ref, g2_ref, b2_ref, ys_ref, out_ref, buf, ysum, sem):
    step = pl.program_id(0)
    tm = base_ref.shape[0]
    slot = step % 2

    sub = ROW_TILE[0]

    def gather_rows(d_ref, s, t0):
        for u in range(sub):
            for k in range(TOP_K):
                pltpu.make_async_copy(ys_ref.at[d_ref[(t0 + u) * TOP_K + k]], buf.at[s, k * tm + t0 + u],
                                      sem.at[s]).start(priority=k % 2)

    def combine_rows(t0):
        y = base_ref[pl.ds(t0, sub), :]
        gates = gate_ref[pl.ds(t0, sub), :]
        for k in range(TOP_K):
            words = _tiles_to_rows(buf[slot, pl.ds(k * tm + t0, sub)])
            y = y + gates[:, k:k + 1] * jnp.concatenate(_unpack_bf16_pairs(words), axis=1)
        ysum[pl.ds(t0, sub), :] = y

    def for_token_groups(body):
        def trip(g, c):
            body(pl.multiple_of(g * sub, sub))
            return c
        lax.fori_loop(0, tm // sub, trip, 0)

    @pl.when(step == 0)
    def _():
        for_token_groups(lambda t0: gather_rows(dest_ref, 0, t0))

    pltpu.make_async_copy(ys_ref.at[pl.ds(0, tm * TOP_K)], buf.at[slot], sem.at[slot]).wait()

    @pl.when(step + 1 < pl.num_programs(0))
    def _():
        def both(t0):
            gather_rows(dest_next_ref, 1 - slot, t0)
            combine_rows(t0)
        for_token_groups(both)

    @pl.when(step + 1 == pl.num_programs(0))
    def _():
        for_token_groups(combine_rows)

    out_ref[...] = _layer_norm(ysum[...], g2_ref[...], b2_ref[...])


def _combine(ys3, dest, gate, base, ln_g, ln_b):
    t = base.shape[0]
    tm = COMB_TM
    n_tiles = t // tm
    dest_tk = dest.T.reshape(-1)
    return pl.pallas_call(
        _combine_kernel,
        grid=(n_tiles,),
        in_specs=[pl.BlockSpec((tm * TOP_K,), lambda i: (i,), memory_space=pltpu.SMEM),
                  pl.BlockSpec((tm * TOP_K,), lambda i: (jnp.minimum(i + 1, n_tiles - 1),), memory_space=pltpu.SMEM),
                  pl.BlockSpec((tm, TOP_K), lambda i: (i, 0)),
                  pl.BlockSpec((tm, D_MODEL), lambda i: (i, 0)),
                  pl.BlockSpec((1, D_MODEL), lambda i: (0, 0)),
                  pl.BlockSpec((1, D_MODEL), lambda i: (0, 0)),
                  pl.BlockSpec(memory_space=pl.ANY)],
        out_specs=pl.BlockSpec((tm, D_MODEL), lambda i: (i, 0)),
        out_shape=jax.ShapeDtypeStruct((t, D_MODEL), F32),
        scratch_shapes=[pltpu.VMEM((2, tm * TOP_K) + PACKED_ROW_TILE, jnp.uint32), pltpu.VMEM((tm, D_MODEL), F32),
                        pltpu.SemaphoreType.DMA((2,))],
        compiler_params=pltpu.CompilerParams(dimension_semantics=("arbitrary",), vmem_limit_bytes=VMEM_LIMIT),
        name="moe_combine",
    )(dest_tk, dest_tk, gate.T, base, ln_g.reshape(1, -1), ln_b.reshape(1, -1), ys3)


def _dest_kernel(pstart_ref, eidx_ref, rank_ref, dest_ref):
    eidx = eidx_ref[...]

    unroll = 8

    def body(g, dest):
        for u in range(unroll):
            e = g * unroll + u
            dest = dest + jnp.where(eidx == e, pstart_ref[e], 0)
        return dest
    dest_ref[...] = lax.fori_loop(0, N_EXPERTS // unroll, body, rank_ref[...])


def _dest_rows(pstarts, eidx, rank):
    t = eidx.shape[1]
    tl = 2048
    spec = pl.BlockSpec((TOP_K, tl), lambda i, *_: (0, i))
    return pl.pallas_call(
        _dest_kernel,
        grid_spec=pltpu.PrefetchScalarGridSpec(num_scalar_prefetch=1, grid=(t // tl,), in_specs=[spec, spec],
                                               out_specs=spec),
        out_shape=jax.ShapeDtypeStruct(eidx.shape, jnp.int32),
        compiler_params=pltpu.CompilerParams(dimension_semantics=("arbitrary",)),
        name="moe_dest_rows",
    )(pstarts, eidx, rank)


def _moe_layout(eidx, rank, counts):
    n_assign = eidx.size
    n_blocks = (n_assign + N_EXPERTS * (MOE_BM - 1)) // MOE_BM
    padded = (counts + MOE_BM - 1) // MOE_BM * MOE_BM
    pends = jnp.cumsum(padded)
    pstarts = (pends - padded).astype(jnp.int32)
    dest = _dest_rows(pstarts, eidx, rank)
    block_row = jnp.arange(n_blocks, dtype=jnp.int32) * MOE_BM
    blk_e = jnp.minimum(jnp.sum(pends[None, :] <= block_row[:, None], axis=1), N_EXPERTS - 1).astype(jnp.int32)
    nused = (pends[-1:] // MOE_BM).astype(jnp.int32)
    zstart = jnp.maximum(pends - MOE_BM, 0).astype(jnp.int32)
    return dest.astype(jnp.int32), blk_e, nused, zstart, n_blocks * MOE_BM


def _layer(x, w_in, cmp_pe, cmp_w1, cmp_b1, cmp_w2, sinks, bias_table, proj_a, proj_b, w_out, ln1_g, ln1_b,
           w_router, router_bias, e_gate, e_up, e_down, s_gate, s_up, s_down, ln2_g, ln2_b):
    bsz, seq, d = x.shape
    x2 = x.reshape(bsz * seq, d)
    proj = _in_projection(x2, w_in)
    kvcmp = _compress(proj['kc'], proj['vc'], bsz, seq, cmp_pe, cmp_w1, cmp_b1, cmp_w2)
    oa, ob = _attention(proj, kvcmp, sinks, bias_table, bsz, seq)
    h, base, eidx, gate, rank, cnt = _out_projection(oa, ob, proj['sg'], x2, proj_a, proj_b, w_out, ln1_g, ln1_b,
                                                     w_router, router_bias, s_gate, s_up, s_down)
    counts = cnt[:, 0].astype(jnp.int32)
    dest, blk_e, nused, zstart, n_rows = _moe_layout(eidx, rank, counts)
    xs = _dispatch(h, dest, zstart, counts, n_rows)
    ys = _experts(xs, blk_e, nused, e_gate, e_up, e_down)
    out = _combine(ys, dest, gate, base, ln2_g, ln2_b)
    return out.reshape(bsz, seq, d)


def kernel(x, w_in, cmp_pe, cmp_w1, cmp_b1, cmp_w2, attn_sinks, rel_bias_table, proj_a, proj_b, w_out, ln1_g, ln1_b,
           w_router, router_bias, expert_w_gate, expert_w_up, expert_w_down, shared_w_gate, shared_w_up,
           shared_w_down, ln2_g, ln2_b):
    h = x
    for l in range(DEPTH):
        h = _layer(h, w_in[l], cmp_pe[l], cmp_w1[l], cmp_b1[l], cmp_w2[l], attn_sinks[l], rel_bias_table, proj_a[l],
                   proj_b[l], w_out[l], ln1_g[l], ln1_b[l], w_router[l], router_bias[l], expert_w_gate[l],
                   expert_w_up[l], expert_w_down[l], shared_w_gate[l], shared_w_up[l], shared_w_down[l], ln2_g[l],
                   ln2_b[l])
    return h
```

```python
import functools
import math

import numpy as np
import jax
import jax.numpy as jnp
from jax import lax
from jax.experimental import pallas as pl
from jax.experimental.pallas import tpu as pltpu

F32 = jnp.float32
BF16 = jnp.bfloat16
MXU_DTYPE = jnp.bfloat16

D_MODEL = 1024
HEAD_DIM = 64
ATTN_SCALE = HEAD_DIM ** -0.5
LOG2E = math.log2(math.e)
Q_BLOCK = 128
N_HEADS = 8
N_GROUPS = 2
GROUP = 4
CMP_BLOCK = 32
CMP_STRIDE = 16
CMP_HIDDEN = 128
SEL_BLOCK = 64
SEL_TOP_N = 8
SEL_INIT_BLOCKS = 1
SEL_LOCAL_BLOCKS = 2
NSA_WINDOW = 512
SWA_WINDOW = 128
REL_BUCKETS = 32
REL_MAX_DIST = 128
N_EXPERTS = 256
TOP_K = 8
EXPERT_HIDDEN = 256
SHARED_HIDDEN = 256
N_EXPERT_GROUPS = 8
TOPK_EXPERT_GROUPS = 4
ROUTED_SCALE = 2.5
LN_EPS = 1e-5
DEPTH = 1
DN_ALPHA = (2 * DEPTH) ** 0.25

NEG = -1e30
LANES = 128
ROW_TILE = (8, LANES)
PACKED_ROW_TILE = (4, LANES)
CMP_FRONT = 16
CMP_NEAR = LANES
SEL_CHUNK = 1024
QB_PER_STEP = 1
VMEM_LIMIT = 56 * 1024 * 1024

IN_TM = 1024
OUT_TM = 512
MOE_BM = 512
ZERO_ROWS = 64
DISP_TM = 512
COMB_TM = 512


def _dot(a, b):
    return jnp.dot(a, b, preferred_element_type=F32)


def _dot_nt(a, b):
    return lax.dot_general(a, b, (((1,), (1,)), ((), ())), preferred_element_type=F32)


def _mx(a):
    return a.astype(MXU_DTYPE)


def _pack_bf16_pairs(x):
    half = x.shape[1] // 2
    bits = lax.bitcast_convert_type(x.astype(BF16).astype(F32), jnp.uint32)
    return (bits[:, half:] & jnp.uint32(0xFFFF0000)) | (bits[:, :half] >> 16)


def _unpack_bf16_pairs(words):
    return (lax.bitcast_convert_type(words << 16, F32),
            lax.bitcast_convert_type(words & jnp.uint32(0xFFFF0000), F32))


_IN_COLS = (('qa', 512), ('qb', 512), ('kc', 128), ('vc', 128), ('ks', 128), ('vs', 128), ('kw', 128),
            ('vw', 128), ('kb', 128), ('vb', 128), ('ga', 128), ('sg', 2048))


def _inproj_kernel(x_ref, w_ref, qa_ref, qb_ref, kc_ref, vc_ref, ks_ref, vs_ref, kw_ref, vw_ref, kb_ref, vb_ref,
                   ga_ref, sg_ref):
    xb = _mx(x_ref[...])
    outs = dict(qa=qa_ref, qb=qb_ref, kc=kc_ref, vc=vc_ref, ks=ks_ref, vs=vs_ref, kw=kw_ref, vw=vw_ref,
                kb=kb_ref, vb=vb_ref, ga=ga_ref, sg=sg_ref)
    tiles = [(name, c) for name, width in _IN_COLS for c in range(0, width, LANES)]
    chunk = 4
    for t0 in range(0, len(tiles), chunk):
        group = tiles[t0:t0 + chunk]
        y = _dot(xb, w_ref[:, t0 * LANES:(t0 + len(group)) * LANES])
        for j, (name, c) in enumerate(group):
            yj = y[:, j * LANES:(j + 1) * LANES]
            if name in ('ga', 'sg'):
                yj = jax.nn.sigmoid(yj)
            outs[name][:, c:c + LANES] = yj.astype(outs[name].dtype)


def _pair_head_columns(w):
    return w.reshape(w.shape[0], N_GROUPS, GROUP, HEAD_DIM).transpose(0, 2, 1, 3).reshape(w.shape[0], -1)


def _in_projection(x2, w_in):
    t = x2.shape[0]
    sizes = (512, 128, 128, 128, 128, 128, 128, 24, 512, 128, 128, 1024, 1024)
    offs = np.cumsum((0,) + sizes)
    part = [w_in[:, offs[k]:offs[k + 1]] for k in range(len(sizes))]
    w_qa, w_kc, w_vc, w_ks, w_vs, w_kw, w_vw, w_g, w_qb, w_kb, w_vb, w_gate_a, w_gate_b = part
    w_qa = _pair_head_columns(w_qa) * (ATTN_SCALE * LOG2E)
    w_qb = _pair_head_columns(w_qb) * (ATTN_SCALE * LOG2E)
    w_ga = w_g.reshape(-1, N_GROUPS, GROUP, 3).transpose(0, 3, 2, 1).reshape(-1, 24)
    w_ga = jnp.pad(w_ga, ((0, 0), (0, LANES - 24)))
    w_all = jnp.concatenate([w_qa, w_qb, w_kc, w_vc, w_ks, w_vs, w_kw, w_vw, w_kb, w_vb, w_ga, w_gate_a, w_gate_b],
                            axis=1).astype(MXU_DTYPE)
    n_all = w_all.shape[1]
    out_shape = []
    out_specs = []
    for name, width in _IN_COLS:
        dt = F32 if name == 'ga' else BF16
        out_shape.append(jax.ShapeDtypeStruct((t, width), dt))
        out_specs.append(pl.BlockSpec((IN_TM, width), lambda i: (i, 0)))
    outs = pl.pallas_call(
        _inproj_kernel,
        grid=(t // IN_TM,),
        in_specs=[pl.BlockSpec((IN_TM, D_MODEL), lambda i: (i, 0)),
                  pl.BlockSpec((D_MODEL, n_all), lambda i: (0, 0))],
        out_specs=out_specs,
        out_shape=out_shape,
        compiler_params=pltpu.CompilerParams(dimension_semantics=("arbitrary",), vmem_limit_bytes=VMEM_LIMIT),
        name="in_projection",
    )(x2, w_all)
    return dict(zip([n for n, _ in _IN_COLS], outs))


def _compress_kernel(tok_ref, w1_ref, pe_ref, w1o_ref, b1_ref, w2_ref, out_ref):
    n_chunks = tok_ref.shape[2]
    ab = _dot(tok_ref[0, 0], w1_ref[0])
    a = ab[:, :2 * CMP_HIDDEN]
    b_next = pltpu.roll(ab[:, 2 * CMP_HIDDEN:], n_chunks - 1, 0)
    cb = _dot(_mx(pe_ref[0]), _mx(w1o_ref[0]))[0:1, :] + b1_ref[0]
    cb2 = jnp.concatenate([cb, cb], axis=1)
    hid = jax.nn.gelu(a + b_next + cb2)
    out = _dot(_mx(hid), w2_ref[0])
    row = lax.broadcasted_iota(jnp.int32, out.shape, 0)
    out = jnp.where(row < n_chunks - 1, out, 0.0)
    out_ref[0, 0, 0:CMP_FRONT, :] = jnp.zeros((CMP_FRONT, LANES), F32)
    out_ref[0, 0, CMP_FRONT:CMP_FRONT + n_chunks, :] = out
    out_ref[0, 0, CMP_FRONT + n_chunks:, :] = jnp.zeros((CMP_NEAR - CMP_FRONT, LANES), F32)


def _compress(kc, vc, bsz, seq, cmp_pe, cmp_w1, cmp_b1, cmp_w2):
    n_chunks = seq // CMP_STRIDE
    tok = jnp.stack([kc, vc]).reshape(2, bsz, n_chunks, CMP_STRIDE * LANES)
    eye = jnp.eye(N_GROUPS, dtype=F32)
    w1r = cmp_w1.reshape(2, 2, CMP_STRIDE, HEAD_DIM, CMP_HIDDEN)
    w1 = jnp.einsum('khjdn,gG->kjgdhGn', w1r, eye).reshape(2, CMP_STRIDE * LANES, 4 * CMP_HIDDEN).astype(MXU_DTYPE)
    w2 = jnp.einsum('knd,gG->kgnGd', cmp_w2, eye).reshape(2, 2 * CMP_HIDDEN, LANES).astype(MXU_DTYPE)
    pe = jnp.pad(cmp_pe.reshape(2, 1, CMP_BLOCK * HEAD_DIM), ((0, 0), (0, 7), (0, 0)))
    b1 = cmp_b1.reshape(2, 1, CMP_HIDDEN)
    rows = CMP_FRONT + n_chunks + CMP_NEAR - CMP_FRONT
    return pl.pallas_call(
        _compress_kernel,
        grid=(2, bsz),
        in_specs=[pl.BlockSpec((1, 1, n_chunks, CMP_STRIDE * LANES), lambda k, b: (k, b, 0, 0)),
                  pl.BlockSpec((1, CMP_STRIDE * LANES, 4 * CMP_HIDDEN), lambda k, b: (k, 0, 0)),
                  pl.BlockSpec((1, 8, CMP_BLOCK * HEAD_DIM), lambda k, b: (k, 0, 0)),
                  pl.BlockSpec((1, CMP_BLOCK * HEAD_DIM, CMP_HIDDEN), lambda k, b: (k, 0, 0)),
                  pl.BlockSpec((1, 1, CMP_HIDDEN), lambda k, b: (k, 0, 0)),
                  pl.BlockSpec((1, 2 * CMP_HIDDEN, LANES), lambda k, b: (k, 0, 0))],
        out_specs=pl.BlockSpec((1, 1, rows, LANES), lambda k, b: (k, b, 0, 0)),
        out_shape=jax.ShapeDtypeStruct((2, bsz, rows, LANES), F32),
        compiler_params=pltpu.CompilerParams(dimension_semantics=("arbitrary", "arbitrary"),
                                             vmem_limit_bytes=VMEM_LIMIT),
        name="nsa_compress",
    )(tok, w1, pe, cmp_w1, b1, w2)


def _stack_heads(q_ref, dst):
    lo = lax.broadcasted_iota(jnp.int32, (Q_BLOCK, LANES), 1) < HEAD_DIM
    for r in range(GROUP):
        qr = q_ref[:, r * LANES:(r + 1) * LANES].astype(dst.dtype)
        z = jnp.zeros_like(qr)
        dst[(2 * r) * Q_BLOCK:(2 * r + 1) * Q_BLOCK, :] = jnp.where(lo, qr, z)
        dst[(2 * r + 1) * Q_BLOCK:(2 * r + 2) * Q_BLOCK, :] = jnp.where(lo, z, qr)


def _pair_heads(o, r):
    lo = lax.broadcasted_iota(jnp.int32, (Q_BLOCK, LANES), 1) < HEAD_DIM
    return jnp.where(lo, o[(2 * r) * Q_BLOCK:(2 * r + 1) * Q_BLOCK], o[(2 * r + 1) * Q_BLOCK:(2 * r + 2) * Q_BLOCK])


def _lane_tiles(x):
    return [x[:, t * LANES:(t + 1) * LANES] for t in range(x.shape[1] // LANES)]


def _row_max(tiles):
    mx = tiles[0]
    for t in tiles[1:]:
        mx = jnp.maximum(mx, t)
    return jnp.broadcast_to(jnp.max(mx, axis=1, keepdims=True), mx.shape)


def _with_ones(v):
    return jnp.concatenate([v, jnp.ones(v.shape, v.dtype)], axis=1)


def _block_of_key(n_keys, first_block):
    b = lax.broadcasted_iota(jnp.int32, (LANES, n_keys), 0)
    k = lax.broadcasted_iota(jnp.int32, (LANES, n_keys), 1)
    return (b == (k // SEL_BLOCK) + first_block).astype(MXU_DTYPE)


def _select_blocks_t(imp_t, i, n_top):
    blk = lax.broadcasted_iota(jnp.int32, imp_t.shape, 0)
    qcol = lax.broadcasted_iota(jnp.int32, imp_t.shape, 1)
    back = (2 * i + (qcol >= SEL_BLOCK).astype(jnp.int32)) - blk
    sel = (back >= 0) & ((blk < SEL_INIT_BLOCKS) | (back < SEL_LOCAL_BLOCKS))
    cand = jnp.where((back >= SEL_LOCAL_BLOCKS) & (blk >= SEL_INIT_BLOCKS), imp_t, -1.0)
    blk_f = blk.astype(F32)
    for _ in range(n_top - SEL_INIT_BLOCKS - SEL_LOCAL_BLOCKS):
        m = jnp.max(cand, axis=0, keepdims=True)
        idx = jnp.min(jnp.where(cand == m, blk_f, float(LANES)), axis=0, keepdims=True)
        hit = blk_f == idx
        sel = sel | (hit & (m >= 0.0))
        cand = jnp.where(hit, -2.0, cand)
    return sel


def _query_block(i, sink_ref, qa_ref, qb_ref, ga_ref, kcmp_ref, vcmp_ref, ks_ref, vs_ref, kw_ref, vw_ref, kb_ref,
                 vb_ref, cmat_ref, tnear_ref, tsel_ref, twin_ref, tswa_ref, oa_ref, ob_ref,
                 qall, qball, mneg, mneg_far, m_s, acc_s, s_buf, oa_acc, qmask, n_far, n_top):
    rows = N_HEADS * Q_BLOCK
    half = rows // 2
    halves = (slice(0, half), slice(half, rows))
    _stack_heads(qa_ref, qall)
    _stack_heads(qb_ref, qball)
    nstart = pl.multiple_of(i * Q_BLOCK, Q_BLOCK)
    lo = lax.broadcasted_iota(jnp.int32, (Q_BLOCK, LANES), 1) < HEAD_DIM
    gates = ga_ref[...]

    def gate_tile(c, r):
        return jnp.where(lo, gates[:, c * 8 + 2 * r:c * 8 + 2 * r + 1], gates[:, c * 8 + 2 * r + 1:c * 8 + 2 * r + 2])

    def softmax_pv(s_tiles, v1, fix_max=None):
        m = _row_max(s_tiles)
        if fix_max is not None:
            m = fix_max(m)
        e = [jnp.exp2(t - m) for t in s_tiles]
        return e, m, _dot(_mx(jnp.concatenate(e, axis=1)), v1)

    off = pl.multiple_of(i * (Q_BLOCK // CMP_STRIDE), 8)
    k_cmp = _mx(jnp.concatenate([kcmp_ref[0, 0, 0:n_far, :], kcmp_ref[0, 0, pl.ds(off, CMP_NEAR), :]], axis=0))
    v_cmp = _with_ones(_mx(jnp.concatenate([vcmp_ref[0, 0, 0:n_far, :], vcmp_ref[0, 0, pl.ds(off, CMP_NEAR), :]],
                                           axis=0)))
    colf = lax.broadcasted_iota(jnp.int32, (1, n_far), 1)
    coln = lax.broadcasted_iota(jnp.int32, (1, CMP_NEAR), 1)
    col_ok = jnp.concatenate([(colf >= CMP_FRONT) & (colf < off), coln + off >= CMP_FRONT], axis=1)
    mask_c = jnp.where(col_ok, 0.0, NEG)
    no_key = lambda m: jnp.where(m > 0.5 * NEG, m, 0.0)
    p_cmp, o_c = [], []
    for rs in halves:
        tiles = _lane_tiles(_dot_nt(qall[rs, :], k_cmp) + mask_c)
        tiles[-1] = tiles[-1] + tnear_ref[rs, :]
        e, _, ov = softmax_pv(tiles, v_cmp, no_key)
        inv = 1.0 / jnp.maximum(ov[:, LANES:], 1e-30)
        o_c.append(ov[:, :LANES] * inv)
        p_cmp.append([t * inv for t in e])
    o_c = jnp.concatenate(o_c, axis=0)
    yield None

    def far_start(j):
        return pl.multiple_of(Q_BLOCK + j * SEL_CHUNK, Q_BLOCK)

    def far_logits(j, slot, masked):
        kc = _mx(ks_ref[0, pl.ds(far_start(j), SEL_CHUNK), :])
        if masked:
            key = lax.broadcasted_iota(jnp.int32, (SEL_CHUNK, LANES), 0)
            blk = lax.broadcasted_iota(jnp.int32, (SEL_CHUNK, LANES), 1)
            one_hot = (blk == key // SEL_BLOCK + j * (SEL_CHUNK // SEL_BLOCK)).astype(MXU_DTYPE)
            kc = jnp.concatenate([kc, one_hot], axis=1)
        for rs in halves:
            s_buf[slot, rs, :] = _dot_nt(qmask[rs, :] if masked else qall[rs, :], kc)

    far_logits(0, 0, False)

    blkcol = lax.broadcasted_iota(jnp.int32, (Q_BLOCK, LANES), 1)
    n_tiles = len(p_cmp[0])
    for g in range(N_GROUPS):
        imp = jnp.zeros((Q_BLOCK, LANES), F32)
        for t in range(n_tiles):
            pg = sum(p_cmp[r // 2][t][(2 * (r % 2) + g) * Q_BLOCK:(2 * (r % 2) + g + 1) * Q_BLOCK]
                     for r in range(GROUP))
            if t < n_tiles - 1:
                cm = _mx(cmat_ref[t * LANES:(t + 1) * LANES, :])
            else:
                cm = _mx(cmat_ref[pl.ds(off, CMP_NEAR), :])
            hi = _mx(pg)
            low = _mx(pg - hi.astype(F32))
            imp = imp + _dot(hi, cm) + _dot(low, cm)
        sel = _select_blocks_t(imp.T, i, n_top)
        neg = jnp.where(sel, 0.0, NEG).T
        mneg[g * Q_BLOCK:(g + 1) * Q_BLOCK, :] = neg.astype(mneg.dtype)
        neg_far = jnp.where(blkcol < 2 * (i - 1), neg, NEG).astype(mneg.dtype)
        mneg_far[g * Q_BLOCK:(g + 1) * Q_BLOCK, :] = neg_far
        for r in range(GROUP):
            qmask[(2 * r + g) * Q_BLOCK:(2 * r + g + 1) * Q_BLOCK, LANES:] = neg_far
    qmask[:, :LANES] = qall[...]

    yield None

    wpad = kw_ref.shape[1] - ks_ref.shape[1] + Q_BLOCK
    kwin = _mx(kw_ref[0, pl.ds(nstart, wpad + Q_BLOCK), :])
    vwin = _with_ones(_mx(vw_ref[0, pl.ds(nstart, wpad + Q_BLOCK), :]))
    colw = lax.broadcasted_iota(jnp.int32, (1, wpad + Q_BLOCK), 1)
    mask_w = jnp.where(colw + nstart >= wpad, 0.0, NEG)
    o_w = []
    for rs in halves:
        _, _, ov = softmax_pv(_lane_tiles(_dot_nt(qall[rs, :], kwin) + twin_ref[rs, :] + mask_w), vwin)
        o_w.append(ov[:, :LANES] / ov[:, LANES:])
    o_w = jnp.concatenate(o_w, axis=0)
    for r in range(GROUP):
        oa_acc[:, r * LANES:(r + 1) * LANES] = (gate_tile(0, r) * _pair_heads(o_c, r)
                                                + gate_tile(2, r) * _pair_heads(o_w, r))

    yield None

    bpad = kb_ref.shape[1] - ks_ref.shape[1] + Q_BLOCK
    kwin = _mx(kb_ref[0, pl.ds(nstart, bpad + Q_BLOCK), :])
    vwin = _with_ones(_mx(vb_ref[0, pl.ds(nstart, bpad + Q_BLOCK), :]))
    colb = lax.broadcasted_iota(jnp.int32, (1, bpad + Q_BLOCK), 1)
    mask_b = jnp.where(colb + nstart >= bpad, 0.0, NEG)
    o_b = []
    for hh, rs in enumerate(halves):
        sink = jnp.concatenate([jnp.full((Q_BLOCK, LANES), sink_ref[(h % 2) * GROUP + h // 2], F32)
                                for h in range(hh * N_HEADS // 2, (hh + 1) * N_HEADS // 2)], axis=0)
        _, m, ov = softmax_pv(_lane_tiles(_dot_nt(qball[rs, :], kwin) + tswa_ref[rs, :] + mask_b), vwin,
                              lambda m: jnp.maximum(m, sink))
        o_b.append(ov[:, :LANES] / (ov[:, LANES:] + jnp.exp2(sink - m)))
    o_b = jnp.concatenate(o_b, axis=0)
    for r in range(GROUP):
        ob_ref[:, r * LANES:(r + 1) * LANES] = _pair_heads(o_b, r).astype(ob_ref.dtype)

    yield None

    m_s[...] = jnp.full(m_s.shape, NEG, F32)
    acc_s[...] = jnp.zeros(acc_s.shape, F32)

    def flash_update(rs, s, v1):
        s_tiles = _lane_tiles(s)
        m_old = m_s[rs, :]
        m_new = jnp.maximum(m_old, _row_max(s_tiles))
        alpha = jnp.exp2(m_old - m_new)
        p = jnp.concatenate([jnp.exp2(t - m_new) for t in s_tiles], axis=1)
        acc_s[rs, :] = jnp.concatenate([alpha, alpha], axis=1) * acc_s[rs, :] + _dot(_mx(p), v1)
        m_s[rs, :] = m_new

    madd = _dot(mneg_far[...], _block_of_key(SEL_CHUNK, 0))
    for rs in halves:
        s_buf[0, rs, :] = s_buf[0, rs, :] + jnp.concatenate([madd] * (GROUP // 2), axis=0)

    def far_update(j):
        v1 = _with_ones(_mx(vs_ref[0, pl.ds(far_start(j), SEL_CHUNK), :]))
        for rs in halves:
            flash_update(rs, s_buf[j % 2, rs, :], v1)

    yield far_update, far_logits

    kc = _mx(ks_ref[0, pl.ds(nstart, 2 * Q_BLOCK), :])
    v1 = _with_ones(_mx(vs_ref[0, pl.ds(nstart, 2 * Q_BLOCK), :]))
    madd = _dot(mneg[...], _block_of_key(2 * Q_BLOCK, 2 * (i - 1)))
    col2 = lax.broadcasted_iota(jnp.int32, (1, 2 * Q_BLOCK), 1)
    mask_n = jnp.where((col2 < Q_BLOCK) & (i == 0), NEG, 0.0)
    for rs in halves:
        s = _dot_nt(qall[rs, :], kc) + jnp.concatenate([madd] * (GROUP // 2), axis=0) + tsel_ref[rs, :] + mask_n
        flash_update(rs, s, v1)
    acc = acc_s[...]
    o_s = acc[:, :LANES] / acc[:, LANES:]
    for r in range(GROUP):
        tile = oa_acc[:, r * LANES:(r + 1) * LANES] + gate_tile(1, r) * _pair_heads(o_s, r)
        oa_ref[:, r * LANES:(r + 1) * LANES] = tile.astype(oa_ref.dtype)
    yield None


def _attn_kernel(sink_ref, qa_ref, qb_ref, ga_ref, kcmp_ref, vcmp_ref, ks_ref, vs_ref, kw_ref, vw_ref, kb_ref,
                 vb_ref, cmat_ref, tnear_ref, tsel_ref, twin_ref, tswa_ref, oa_ref, ob_ref,
                 qall, qball, mneg, mneg_far, m_s, acc_s, s_buf, oa_acc, qmask, *, n_far, n_top):
    first = pl.program_id(1) * QB_PER_STEP
    blocks, steps = [], []
    for n in range(QB_PER_STEP):
        qrows = pl.ds(n * Q_BLOCK, Q_BLOCK)
        blk = _query_block(first + n, sink_ref, qa_ref.at[qrows, :], qb_ref.at[qrows, :], ga_ref.at[qrows, :],
                           kcmp_ref, vcmp_ref, ks_ref, vs_ref, kw_ref, vw_ref, kb_ref, vb_ref, cmat_ref, tnear_ref,
                           tsel_ref, twin_ref, tswa_ref, oa_ref.at[qrows, :], ob_ref.at[qrows, :],
                           qall.at[n], qball.at[n], mneg.at[n], mneg_far.at[n], m_s.at[n], acc_s.at[n], s_buf.at[n],
                           oa_acc.at[n], qmask.at[n], n_far, n_top)
        blocks.append(blk)
    steps = [next(blk) for blk in blocks]
    while steps[0] is None:
        steps = [next(blk) for blk in blocks]
    n_far_keys = jnp.maximum(first + QB_PER_STEP - 2, 0) * Q_BLOCK
    n_chunks = (n_far_keys + SEL_CHUNK - 1) // SEL_CHUNK

    def far_body(j, carry):
        for far_update, _ in steps:
            far_update(j)
        for _, far_logits in steps:
            far_logits(j + 1, (j + 1) % 2, True)
        return carry

    last = jnp.maximum(n_chunks - 1, 0)
    lax.fori_loop(0, last, far_body, 0)
    for far_update, _ in steps:
        far_update(last)
    for blk in blocks:
        next(blk)


def _rel_bucket_np(dist):
    n = np.maximum(dist, 0)
    max_exact = REL_BUCKETS // 2
    nf = np.maximum(n, 1).astype(np.float32)
    log_b = max_exact + (np.log(nf / max_exact) / math.log(REL_MAX_DIST / max_exact)
                         * (REL_BUCKETS - max_exact)).astype(np.int32)
    log_b = np.minimum(log_b, REL_BUCKETS - 1)
    return np.where(n < max_exact, n, log_b)


def _toeplitz_bias(tab, pad, width, window, shift_far):
    length = width + Q_BLOCK
    dist = pad + Q_BLOCK - 1 - np.arange(length)
    onehot = np.zeros((length, REL_BUCKETS), np.float32)
    onehot[np.arange(length), _rel_bucket_np(dist)] = 1.0
    vals = jnp.dot(jnp.asarray(onehot), tab, precision=lax.Precision.HIGHEST)
    if shift_far:
        vals = vals - tab[REL_BUCKETS - 1][None, :]
    vals = vals * LOG2E
    valid = (dist >= 0) & (dist < window)
    vals = jnp.where(jnp.asarray(valid)[:, None], vals, NEG).T
    skew = jnp.tile(vals, (1, Q_BLOCK))[:, :Q_BLOCK * (length - 1)].reshape(N_HEADS, Q_BLOCK, length - 1)
    return skew[:, :, Q_BLOCK - 1:Q_BLOCK - 1 + width].reshape(N_HEADS * Q_BLOCK, width).astype(F32)


def _attention(proj, kvcmp, sinks, bias_table, bsz, seq):
    assert seq % SEL_CHUNK == 0
    nq = seq // Q_BLOCK
    n_far = seq // CMP_STRIDE
    n_sel = seq // SEL_BLOCK
    n_top = min(SEL_TOP_N, n_sel)
    assert n_top >= SEL_INIT_BLOCKS + SEL_LOCAL_BLOCKS and n_sel <= LANES
    wpad = Q_BLOCK * (-(-(NSA_WINDOW - 1) // Q_BLOCK))
    bpad = Q_BLOCK * (-(-(SWA_WINDOW - 1) // Q_BLOCK))
    pair = lambda tab: tab.astype(F32).reshape(REL_BUCKETS, N_GROUPS, GROUP).transpose(0, 2, 1).reshape(REL_BUCKETS, -1)
    tab_a = pair(bias_table[:, :N_HEADS])
    tab_b = pair(bias_table[:, N_HEADS:])
    near_pad = CMP_STRIDE * CMP_FRONT - (CMP_BLOCK - 1)
    t_near = _toeplitz_bias(tab_a, near_pad, CMP_STRIDE * CMP_NEAR, 1 << 30, True)[:, ::CMP_STRIDE]
    t_sel = _toeplitz_bias(tab_a, Q_BLOCK, 2 * Q_BLOCK, 1 << 30, True)
    t_win = _toeplitz_bias(tab_a, wpad, wpad + Q_BLOCK, NSA_WINDOW, False)
    t_swa = _toeplitz_bias(tab_b, bpad, bpad + Q_BLOCK, SWA_WINDOW, False)
    n_rows = kvcmp.shape[2]
    cn = (np.arange(n_rows) - CMP_FRONT)[:, None] * CMP_STRIDE
    sj = np.arange(LANES)[None, :] * SEL_BLOCK
    cmat = ((cn < sj + SEL_BLOCK) & (cn + CMP_BLOCK > sj) & (cn >= 0) & (cn + CMP_BLOCK <= seq)
            & (sj < seq)).astype(np.float32)
    cmat = jnp.asarray(cmat, F32)
    padded = lambda name, p: jnp.pad(proj[name].reshape(bsz, seq, LANES), ((0, 0), (p, 0), (0, 0)))
    ks, vs = padded('ks', Q_BLOCK), padded('vs', Q_BLOCK)
    kw, vw = padded('kw', wpad), padded('vw', wpad)
    kb, vb = padded('kb', bpad), padded('vb', bpad)
    rows = N_HEADS * Q_BLOCK
    n_steps = nq // QB_PER_STEP
    qspec = pl.BlockSpec((QB_PER_STEP * Q_BLOCK, 4 * LANES), lambda b, i: (b * n_steps + i, 0))
    const2 = lambda shape: pl.BlockSpec(shape, lambda b, i: (0, 0))
    batch3 = lambda n: pl.BlockSpec((1, n, LANES), lambda b, i: (b, 0, 0))
    per_block = lambda shape, dtype: pltpu.VMEM((QB_PER_STEP,) + shape, dtype)
    kernel = functools.partial(_attn_kernel, n_far=n_far, n_top=n_top)
    return pl.pallas_call(
        kernel,
        grid=(bsz, n_steps),
        in_specs=[pl.BlockSpec(memory_space=pltpu.SMEM),
                  qspec, qspec,
                  pl.BlockSpec((QB_PER_STEP * Q_BLOCK, LANES), lambda b, i: (b * n_steps + i, 0)),
                  pl.BlockSpec((1, 1, n_rows, LANES), lambda b, i: (0, b, 0, 0)),
                  pl.BlockSpec((1, 1, n_rows, LANES), lambda b, i: (1, b, 0, 0)),
                  batch3(seq + Q_BLOCK), batch3(seq + Q_BLOCK),
                  batch3(seq + wpad), batch3(seq + wpad),
                  batch3(seq + bpad), batch3(seq + bpad),
                  const2((n_rows, LANES)),
                  const2((rows, CMP_NEAR)),
                  const2((rows, 2 * Q_BLOCK)),
                  const2((rows, wpad + Q_BLOCK)),
                  const2((rows, bpad + Q_BLOCK))],
        out_specs=[qspec, qspec],
        out_shape=[jax.ShapeDtypeStruct((bsz * seq, 4 * LANES), BF16)] * 2,
        scratch_shapes=[per_block((rows, LANES), MXU_DTYPE),
                        per_block((rows, LANES), MXU_DTYPE),
                        per_block((N_GROUPS * Q_BLOCK, LANES), MXU_DTYPE),
                        per_block((N_GROUPS * Q_BLOCK, LANES), MXU_DTYPE),
                        per_block((rows, LANES), F32),
                        per_block((rows, 2 * LANES), F32),
                        per_block((2, rows, SEL_CHUNK), F32),
                        per_block((Q_BLOCK, 4 * LANES), F32),
                        per_block((rows, 2 * LANES), MXU_DTYPE)],
        compiler_params=pltpu.CompilerParams(dimension_semantics=("arbitrary", "arbitrary"),
                                             vmem_limit_bytes=VMEM_LIMIT),
        name="attention",
    )(sinks.astype(F32) * LOG2E, proj['qa'], proj['qb'], proj['ga'], kvcmp, kvcmp, ks, vs, kw, vw, kb, vb,
      cmat, t_near, t_sel, t_win, t_swa)


def _layer_norm(y, g, b):
    mu = jnp.mean(y, axis=-1, keepdims=True)
    yc = y - mu
    var = jnp.mean(yc * yc, axis=-1, keepdims=True)
    return yc * lax.rsqrt(var + LN_EPS) * g + b


def _outproj_kernel(oa_ref, ob_ref, sg_ref, x_ref, pa_ref, pb_ref, wo_ref, g1_ref, b1_ref, wr_ref, rb_ref, sgu_ref,
                    sd_ref, tri_ref, h_ref, base_ref, eidx_ref, gate_ref, rank_ref, cnt_ref, carry):
    step = pl.program_id(0)
    tm = oa_ref.shape[0]

    @pl.when(step == 0)
    def _():
        carry[...] = jnp.zeros(carry.shape, F32)

    sg = sg_ref[...].astype(F32)
    merged = (sg[:, :D_MODEL] * _dot(_mx(oa_ref[...]), pa_ref[...])
              + sg[:, D_MODEL:] * _dot(_mx(ob_ref[...]), pb_ref[...]))
    mix = _dot(_mx(merged), wo_ref[...])
    h = _layer_norm(DN_ALPHA * x_ref[...] + mix, g1_ref[...], b1_ref[...])
    hb = _mx(h)
    h_ref[...] = _pack_bf16_pairs(h)

    gu = _dot(hb, sgu_ref[...])
    shared = _dot(_mx(jax.nn.silu(gu[:, :SHARED_HIDDEN]) * gu[:, SHARED_HIDDEN:]), sd_ref[...])
    base_ref[...] = DN_ALPHA * h + shared

    scores = jax.nn.sigmoid(_dot_nt(wr_ref[...], hb))
    choice = scores + rb_ref[:, 0:1]
    per_group = N_EXPERTS // N_EXPERT_GROUPS
    gs = []
    for g in range(N_EXPERT_GROUPS):
        cg = choice[g * per_group:(g + 1) * per_group]
        m1 = jnp.max(cg, axis=0, keepdims=True)
        is_m = cg == m1
        n_m = jnp.sum(is_m.astype(F32), axis=0, keepdims=True)
        m2 = jnp.max(jnp.where(is_m, -jnp.inf, cg), axis=0, keepdims=True)
        gs.append(m1 + jnp.where(n_m > 1.5, m1, m2))
    gs = jnp.concatenate(gs, axis=0)
    gid = lax.broadcasted_iota(jnp.int32, gs.shape, 0)
    beaten = jnp.zeros(gs.shape, jnp.int32)
    for g in range(N_EXPERT_GROUPS):
        other = gs[g:g + 1]
        beaten = beaten + ((other > gs) | ((other == gs) & (g < gid))).astype(jnp.int32)
    keep_g = beaten < TOPK_EXPERT_GROUPS
    keep = jnp.concatenate([jnp.broadcast_to(keep_g[g:g + 1], (per_group, tm)) for g in range(N_EXPERT_GROUPS)],
                           axis=0)
    cand = jnp.where(keep, choice, -jnp.inf)
    eid = lax.broadcasted_iota(jnp.int32, cand.shape, 0)
    hits = []
    e_rows = []
    w_rows = []
    for _ in range(TOP_K):
        m = jnp.max(cand, axis=0, keepdims=True)
        idx = jnp.min(jnp.where(cand == m, eid, N_EXPERTS), axis=0, keepdims=True)
        hit = eid == idx
        hits.append(hit)
        e_rows.append(idx)
        w_rows.append(jnp.sum(jnp.where(hit, scores, 0.0), axis=0, keepdims=True))
        cand = jnp.where(hit, -jnp.inf, cand)
    w = jnp.concatenate(w_rows, axis=0)
    gate_ref[...] = w / jnp.sum(w, axis=0, keepdims=True) * ROUTED_SCALE
    eidx_ref[...] = jnp.concatenate(e_rows, axis=0)

    onehot = jnp.zeros(cand.shape, F32)
    for hit in hits:
        onehot = onehot + hit.astype(F32)
    before = _dot(onehot.astype(BF16), tri_ref[...]) + carry[:, 0:1]
    rank_ref[...] = jnp.concatenate(
        [jnp.sum(jnp.where(hit, before, 0.0), axis=0, keepdims=True) for hit in hits], axis=0).astype(jnp.int32)
    carry[...] = carry[...] + jnp.sum(onehot, axis=1, keepdims=True)
    cnt_ref[...] = carry[...]


def _out_projection(oa, ob, sg, x2, proj_a, proj_b, w_out, ln_g, ln_b, w_router, router_bias, s_gate, s_up, s_down):
    t = x2.shape[0]
    tm = OUT_TM
    pair_rows = lambda p: p.reshape(N_GROUPS, GROUP, HEAD_DIM, -1).transpose(1, 0, 2, 3).reshape(p.shape)
    pa = pair_rows(proj_a).astype(MXU_DTYPE)
    pb = pair_rows(proj_b).astype(MXU_DTYPE)
    tri = jnp.asarray(np.triu(np.ones((tm, tm), np.float32), 1), BF16)
    row = lambda i: (i, 0)
    fixed = lambda i: (0, 0)
    col = lambda i: (0, i)
    outs = pl.pallas_call(
        _outproj_kernel,
        grid=(t // tm,),
        in_specs=[pl.BlockSpec((tm, 4 * LANES), row), pl.BlockSpec((tm, 4 * LANES), row),
                  pl.BlockSpec((tm, 2 * D_MODEL), row), pl.BlockSpec((tm, D_MODEL), row),
                  pl.BlockSpec((4 * LANES, D_MODEL), fixed), pl.BlockSpec((4 * LANES, D_MODEL), fixed),
                  pl.BlockSpec((D_MODEL, D_MODEL), fixed),
                  pl.BlockSpec((1, D_MODEL), fixed), pl.BlockSpec((1, D_MODEL), fixed),
                  pl.BlockSpec((N_EXPERTS, D_MODEL), fixed), pl.BlockSpec((N_EXPERTS, LANES), fixed),
                  pl.BlockSpec((D_MODEL, 2 * SHARED_HIDDEN), fixed), pl.BlockSpec((SHARED_HIDDEN, D_MODEL), fixed),
                  pl.BlockSpec((tm, tm), fixed)],
        out_specs=[pl.BlockSpec((tm, D_MODEL // 2), row), pl.BlockSpec((tm, D_MODEL), row),
                   pl.BlockSpec((TOP_K, tm), col), pl.BlockSpec((TOP_K, tm), col), pl.BlockSpec((TOP_K, tm), col),
                   pl.BlockSpec((N_EXPERTS, LANES), fixed)],
        out_shape=[jax.ShapeDtypeStruct((t, D_MODEL // 2), jnp.uint32), jax.ShapeDtypeStruct((t, D_MODEL), F32),
                   jax.ShapeDtypeStruct((TOP_K, t), jnp.int32), jax.ShapeDtypeStruct((TOP_K, t), F32),
                   jax.ShapeDtypeStruct((TOP_K, t), jnp.int32), jax.ShapeDtypeStruct((N_EXPERTS, LANES), F32)],
        scratch_shapes=[pltpu.VMEM((N_EXPERTS, LANES), F32)],
        compiler_params=pltpu.CompilerParams(dimension_semantics=("arbitrary",), vmem_limit_bytes=VMEM_LIMIT),
        name="out_projection_router",
    )(oa, ob, sg, x2, pa, pb, w_out.astype(MXU_DTYPE), ln_g.reshape(1, -1), ln_b.reshape(1, -1),
      w_router.T.astype(MXU_DTYPE), jnp.broadcast_to(router_bias.astype(F32)[:, None], (N_EXPERTS, LANES)),
      jnp.concatenate([s_gate, s_up], axis=1).astype(MXU_DTYPE), s_down.astype(MXU_DTYPE), tri)
    return outs


def _rows_to_tiles(x):
    return pltpu.einshape("cml->mcl", jnp.stack(_lane_tiles(x), axis=0))


def _tiles_to_rows(x3):
    xt = pltpu.einshape("mcl->cml", x3)
    return jnp.concatenate([xt[c] for c in range(xt.shape[0])], axis=1)


def _dispatch_kernel(pend_ref, pad_ref, dest_ref, h2_ref, xs_ref, h_ref, zeros, sem, zsem):
    step = pl.program_id(0)
    tm = h2_ref.shape[0]
    slot = step % 2
    h_ref[slot] = _rows_to_tiles(h2_ref[...])

    @pl.when(step == 0)
    def _():
        zeros[...] = jnp.zeros(zeros.shape, zeros.dtype)

        def for_pieces(action):
            def body(e, c):
                for piece in range(MOE_BM // ZERO_ROWS):
                    @pl.when(pad_ref[e] > piece * ZERO_ROWS)
                    def _():
                        start = pend_ref[e] - (piece + 1) * ZERO_ROWS
                        action(pltpu.make_async_copy(zeros, xs_ref.at[pl.ds(start, ZERO_ROWS)], zsem))
                return c
            lax.fori_loop(0, N_EXPERTS, body, 0)
        for_pieces(lambda cp: cp.start())
        for_pieces(lambda cp: cp.wait())

    def issue(t, c):
        for k in range(TOP_K):
            pltpu.make_async_copy(h_ref.at[slot, t], xs_ref.at[dest_ref[k, t]], sem.at[slot]).start(priority=k % 2)
        return c
    lax.fori_loop(0, tm, issue, 0)

    def wait_tile(s):
        for k in range(TOP_K):
            pltpu.make_async_copy(h_ref.at[s], xs_ref.at[pl.ds(0, tm)], sem.at[s]).wait()

    @pl.when(step > 0)
    def _():
        wait_tile(1 - slot)

    @pl.when(step + 1 == pl.num_programs(0))
    def _():
        wait_tile(slot)


def _dispatch(h, dest, pends, pad_rows, n_rows):
    t = h.shape[0]
    tm = DISP_TM
    return pl.pallas_call(
        _dispatch_kernel,
        grid_spec=pltpu.PrefetchScalarGridSpec(
            num_scalar_prefetch=2,
            grid=(t // tm,),
            in_specs=[pl.BlockSpec((TOP_K, tm), lambda i, *_: (0, i), memory_space=pltpu.SMEM),
                      pl.BlockSpec((tm, D_MODEL // 2), lambda i, *_: (i, 0))],
            out_specs=pl.BlockSpec(memory_space=pl.ANY),
            scratch_shapes=[pltpu.VMEM((2, tm) + PACKED_ROW_TILE, jnp.uint32),
                            pltpu.VMEM((ZERO_ROWS,) + PACKED_ROW_TILE, jnp.uint32),
                            pltpu.SemaphoreType.DMA((2,)), pltpu.SemaphoreType.DMA(())]),
        out_shape=jax.ShapeDtypeStruct((n_rows,) + PACKED_ROW_TILE, jnp.uint32),
        compiler_params=pltpu.CompilerParams(dimension_semantics=("arbitrary",), vmem_limit_bytes=VMEM_LIMIT),
        name="moe_dispatch",
    )(pends, pad_rows, dest, h)


def _experts_kernel(blk_e_ref, nused_ref, xs_ref, wg_ref, wu_ref, wd_ref, ys_ref, wg_s, wu_s, wd_s):
    b = pl.program_id(0)
    prev = blk_e_ref[jnp.maximum(b - 1, 0)]

    @pl.when((b == 0) | (blk_e_ref[b] != prev))
    def _():
        wg_s[...] = _mx(wg_ref[0])
        wu_s[...] = _mx(wu_ref[0])
        wd_s[...] = _mx(wd_ref[0])

    @pl.when(b < nused_ref[0])
    def _():
        xb = _mx(jnp.concatenate(_unpack_bf16_pairs(_tiles_to_rows(xs_ref[...])), axis=1))
        hid = jax.nn.silu(_dot(xb, wg_s[...])) * _dot(xb, wu_s[...])
        ys_ref[...] = _rows_to_tiles(_pack_bf16_pairs(_dot(_mx(hid), wd_s[...])))

    @pl.when(b >= nused_ref[0])
    def _():
        ys_ref[...] = jnp.zeros(ys_ref.shape, ys_ref.dtype)


def _experts(xs, blk_e, nused, e_gate, e_up, e_down):
    n_rows = xs.shape[0]
    n_blocks = n_rows // MOE_BM
    xmap = lambda b, be, nu: (jnp.minimum(b, nu[0] - 1), 0, 0)
    wmap = lambda b, be, nu: (be[b], 0, 0)
    return pl.pallas_call(
        _experts_kernel,
        grid_spec=pltpu.PrefetchScalarGridSpec(
            num_scalar_prefetch=2,
            grid=(n_blocks,),
            in_specs=[pl.BlockSpec((MOE_BM,) + PACKED_ROW_TILE, xmap),
                      pl.BlockSpec((1, D_MODEL, EXPERT_HIDDEN), wmap),
                      pl.BlockSpec((1, D_MODEL, EXPERT_HIDDEN), wmap),
                      pl.BlockSpec((1, EXPERT_HIDDEN, D_MODEL), wmap)],
            out_specs=pl.BlockSpec((MOE_BM,) + PACKED_ROW_TILE,
                                   lambda b, be, nu: (jnp.where(b < nu[0], b, n_blocks - 1), 0, 0)),
            scratch_shapes=[pltpu.VMEM((D_MODEL, EXPERT_HIDDEN), MXU_DTYPE),
                            pltpu.VMEM((D_MODEL, EXPERT_HIDDEN), MXU_DTYPE),
                            pltpu.VMEM((EXPERT_HIDDEN, D_MODEL), MXU_DTYPE)]),
        out_shape=jax.ShapeDtypeStruct((n_rows,) + PACKED_ROW_TILE, jnp.uint32),
        compiler_params=pltpu.CompilerParams(dimension_semantics=("arbitrary",), vmem_limit_bytes=VMEM_LIMIT),
        name="moe_experts",
    )(blk_e, nused, xs, e_gate, e_up, e_down)


def _combine_kernel(dest_ref, dest_next_ref, gate_ref, base_ref, g2_ref, b2_ref, ys_ref, out_ref, buf, ysum, sem):
    step = pl.program_id(0)
    tm = base_ref.shape[0]
    slot = step % 2

    sub = ROW_TILE[0]

    def gather_rows(d_ref, s, t0):
        for u in range(sub):
            for k in range(TOP_K):
                pltpu.make_async_copy(ys_ref.at[d_ref[(t0 + u) * TOP_K + k]], buf.at[s, k * tm + t0 + u],
                                      sem.at[s]).start(priority=k % 2)

    def combine_rows(t0):
        y = base_ref[pl.ds(t0, sub), :]
        gates = gate_ref[pl.ds(t0, sub), :]
        for k in range(TOP_K):
            words = _tiles_to_rows(buf[slot, pl.ds(k * tm + t0, sub)])
            y = y + gates[:, k:k + 1] * jnp.concatenate(_unpack_bf16_pairs(words), axis=1)
        ysum[pl.ds(t0, sub), :] = y

    def for_token_groups(body):
        def trip(g, c):
            body(pl.multiple_of(g * sub, sub))
            return c
        lax.fori_loop(0, tm // sub, trip, 0)

    @pl.when(step == 0)
    def _():
        for_token_groups(lambda t0: gather_rows(dest_ref, 0, t0))

    pltpu.make_async_copy(ys_ref.at[pl.ds(0, tm * TOP_K)], buf.at[slot], sem.at[slot]).wait()

    @pl.when(step + 1 < pl.num_programs(0))
    def _():
        def both(t0):
            gather_rows(dest_next_ref, 1 - slot, t0)
            combine_rows(t0)
        for_token_groups(both)

    @pl.when(step + 1 == pl.num_programs(0))
    def _():
        for_token_groups(combine_rows)

    out_ref[...] = _layer_norm(ysum[...], g2_ref[...], b2_ref[...])


def _combine(ys3, dest, gate, base, ln_g, ln_b):
    t = base.shape[0]
    tm = COMB_TM
    n_tiles = t // tm
    dest_tk = dest.T.reshape(-1)
    return pl.pallas_call(
        _combine_kernel,
        grid=(n_tiles,),
        in_specs=[pl.BlockSpec((tm * TOP_K,), lambda i: (i,), memory_space=pltpu.SMEM),
                  pl.BlockSpec((tm * TOP_K,), lambda i: (jnp.minimum(i + 1, n_tiles - 1),), memory_space=pltpu.SMEM),
                  pl.BlockSpec((tm, TOP_K), lambda i: (i, 0)),
                  pl.BlockSpec((tm, D_MODEL), lambda i: (i, 0)),
                  pl.BlockSpec((1, D_MODEL), lambda i: (0, 0)),
                  pl.BlockSpec((1, D_MODEL), lambda i: (0, 0)),
                  pl.BlockSpec(memory_space=pl.ANY)],
        out_specs=pl.BlockSpec((tm, D_MODEL), lambda i: (i, 0)),
        out_shape=jax.ShapeDtypeStruct((t, D_MODEL), F32),
        scratch_shapes=[pltpu.VMEM((2, tm * TOP_K) + PACKED_ROW_TILE, jnp.uint32), pltpu.VMEM((tm, D_MODEL), F32),
                        pltpu.SemaphoreType.DMA((2,))],
        compiler_params=pltpu.CompilerParams(dimension_semantics=("arbitrary",), vmem_limit_bytes=VMEM_LIMIT),
        name="moe_combine",
    )(dest_tk, dest_tk, gate.T, base, ln_g.reshape(1, -1), ln_b.reshape(1, -1), ys3)


def _dest_kernel(pstart_ref, eidx_ref, rank_ref, dest_ref):
    eidx = eidx_ref[...]

    unroll = 8

    def body(g, dest):
        for u in range(unroll):
            e = g * unroll + u
            dest = dest + jnp.where(eidx == e, pstart_ref[e], 0)
        return dest
    dest_ref[...] = lax.fori_loop(0, N_EXPERTS // unroll, body, rank_ref[...])


def _dest_rows(pstarts, eidx, rank):
    t = eidx.shape[1]
    tl = 2048
    spec = pl.BlockSpec((TOP_K, tl), lambda i, *_: (0, i))
    return pl.pallas_call(
        _dest_kernel,
        grid_spec=pltpu.PrefetchScalarGridSpec(num_scalar_prefetch=1, grid=(t // tl,), in_specs=[spec, spec],
                                               out_specs=spec),
        out_shape=jax.ShapeDtypeStruct(eidx.shape, jnp.int32),
        compiler_params=pltpu.CompilerParams(dimension_semantics=("arbitrary",)),
        name="moe_dest_rows",
    )(pstarts, eidx, rank)


def _moe_layout(eidx, rank, counts):
    n_assign = eidx.size
    n_blocks = (n_assign + N_EXPERTS * (MOE_BM - 1)) // MOE_BM
    padded = (counts + MOE_BM - 1) // MOE_BM * MOE_BM
    pends = jnp.cumsum(padded)
    pstarts = (pends - padded).astype(jnp.int32)
    dest = _dest_rows(pstarts, eidx, rank)
    block_row = jnp.arange(n_blocks, dtype=jnp.int32) * MOE_BM
    blk_e = jnp.minimum(jnp.sum(pends[None, :] <= block_row[:, None], axis=1), N_EXPERTS - 1).astype(jnp.int32)
    nused = (pends[-1:] // MOE_BM).astype(jnp.int32)
    pad_rows = (padded - counts).astype(jnp.int32)
    return dest.astype(jnp.int32), blk_e, nused, pends.astype(jnp.int32), pad_rows, n_blocks * MOE_BM


def _layer(x, w_in, cmp_pe, cmp_w1, cmp_b1, cmp_w2, sinks, bias_table, proj_a, proj_b, w_out, ln1_g, ln1_b,
           w_router, router_bias, e_gate, e_up, e_down, s_gate, s_up, s_down, ln2_g, ln2_b):
    bsz, seq, d = x.shape
    x2 = x.reshape(bsz * seq, d)
    proj = _in_projection(x2, w_in)
    kvcmp = _compress(proj['kc'], proj['vc'], bsz, seq, cmp_pe, cmp_w1, cmp_b1, cmp_w2)
    oa, ob = _attention(proj, kvcmp, sinks, bias_table, bsz, seq)
    h, base, eidx, gate, rank, cnt = _out_projection(oa, ob, proj['sg'], x2, proj_a, proj_b, w_out, ln1_g, ln1_b,
                                                     w_router, router_bias, s_gate, s_up, s_down)
    counts = cnt[:, 0].astype(jnp.int32)
    dest, blk_e, nused, pends, pad_rows, n_rows = _moe_layout(eidx, rank, counts)
    xs = _dispatch(h, dest, pends, pad_rows, n_rows)
    ys = _experts(xs, blk_e, nused, e_gate, e_up, e_down)
    out = _combine(ys, dest, gate, base, ln2_g, ln2_b)
    return out.reshape(bsz, seq, d)


def kernel(x, w_in, cmp_pe, cmp_w1, cmp_b1, cmp_w2, attn_sinks, rel_bias_table, proj_a, proj_b, w_out, ln1_g, ln1_b,
           w_router, router_bias, expert_w_gate, expert_w_up, expert_w_down, shared_w_gate, shared_w_up,
           shared_w_down, ln2_g, ln2_b):
    h = x
    for l in range(DEPTH):
        h = _layer(h, w_in[l], cmp_pe[l], cmp_w1[l], cmp_b1[l], cmp_w2[l], attn_sinks[l], rel_bias_table, proj_a[l],
                   proj_b[l], w_out[l], ln1_g[l], ln1_b[l], w_router[l], router_bias[l], expert_w_gate[l],
                   expert_w_up[l], expert_w_down[l], shared_w_gate[l], shared_w_up[l], shared_w_down[l], ln2_g[l],
                   ln2_b[l])
    return h
```

```python
import functools
import math

import numpy as np
import jax
import jax.numpy as jnp
from jax import lax
from jax.experimental import pallas as pl
from jax.experimental.pallas import tpu as pltpu

F32 = jnp.float32
BF16 = jnp.bfloat16
MXU_DTYPE = jnp.bfloat16

D_MODEL = 1024
HEAD_DIM = 64
ATTN_SCALE = HEAD_DIM ** -0.5
LOG2E = math.log2(math.e)
Q_BLOCK = 128
N_HEADS = 8
N_GROUPS = 2
GROUP = 4
CMP_BLOCK = 32
CMP_STRIDE = 16
CMP_HIDDEN = 128
SEL_BLOCK = 64
SEL_TOP_N = 8
SEL_INIT_BLOCKS = 1
SEL_LOCAL_BLOCKS = 2
NSA_WINDOW = 512
SWA_WINDOW = 128
REL_BUCKETS = 32
REL_MAX_DIST = 128
N_EXPERTS = 256
TOP_K = 8
EXPERT_HIDDEN = 256
SHARED_HIDDEN = 256
N_EXPERT_GROUPS = 8
TOPK_EXPERT_GROUPS = 4
ROUTED_SCALE = 2.5
LN_EPS = 1e-5
DEPTH = 1
DN_ALPHA = (2 * DEPTH) ** 0.25

NEG = -1e30
LANES = 128
ROW_TILE = (8, LANES)
PACKED_ROW_TILE = (4, LANES)
CMP_FRONT = 16
CMP_NEAR = LANES
SEL_CHUNK = 1024
QB_PER_STEP = 1
VMEM_LIMIT = 56 * 1024 * 1024

IN_TM = 1024
OUT_TM = 512
MOE_BM = 512
ZERO_ROWS = 64
DISP_TM = 512
COMB_TM = 512


def _dot(a, b):
    return jnp.dot(a, b, preferred_element_type=F32)


def _dot_nt(a, b):
    return lax.dot_general(a, b, (((1,), (1,)), ((), ())), preferred_element_type=F32)


def _mx(a):
    return a.astype(MXU_DTYPE)


def _pack_bf16_pairs(x):
    half = x.shape[1] // 2
    bits = lax.bitcast_convert_type(x.astype(BF16).astype(F32), jnp.uint32)
    return (bits[:, half:] & jnp.uint32(0xFFFF0000)) | (bits[:, :half] >> 16)


def _unpack_bf16_pairs(words):
    return (lax.bitcast_convert_type(words << 16, F32),
            lax.bitcast_convert_type(words & jnp.uint32(0xFFFF0000), F32))


_IN_COLS = (('qa', 512), ('qb', 512), ('kc', 128), ('vc', 128), ('ks', 128), ('vs', 128), ('kw', 128),
            ('vw', 128), ('kb', 128), ('vb', 128), ('ga', 128), ('sg', 2048))


def _inproj_kernel(x_ref, w_ref, qa_ref, qb_ref, kc_ref, vc_ref, ks_ref, vs_ref, kw_ref, vw_ref, kb_ref, vb_ref,
                   ga_ref, sg_ref):
    xb = _mx(x_ref[...])
    outs = dict(qa=qa_ref, qb=qb_ref, kc=kc_ref, vc=vc_ref, ks=ks_ref, vs=vs_ref, kw=kw_ref, vw=vw_ref,
                kb=kb_ref, vb=vb_ref, ga=ga_ref, sg=sg_ref)
    tiles = [(name, c) for name, width in _IN_COLS for c in range(0, width, LANES)]
    chunk = 4
    for t0 in range(0, len(tiles), chunk):
        group = tiles[t0:t0 + chunk]
        y = _dot(xb, w_ref[:, t0 * LANES:(t0 + len(group)) * LANES])
        for j, (name, c) in enumerate(group):
            yj = y[:, j * LANES:(j + 1) * LANES]
            if name in ('ga', 'sg'):
                yj = jax.nn.sigmoid(yj)
            outs[name][:, c:c + LANES] = yj.astype(outs[name].dtype)


def _pair_head_columns(w):
    return w.reshape(w.shape[0], N_GROUPS, GROUP, HEAD_DIM).transpose(0, 2, 1, 3).reshape(w.shape[0], -1)


def _in_projection(x2, w_in):
    t = x2.shape[0]
    sizes = (512, 128, 128, 128, 128, 128, 128, 24, 512, 128, 128, 1024, 1024)
    offs = np.cumsum((0,) + sizes)
    part = [w_in[:, offs[k]:offs[k + 1]] for k in range(len(sizes))]
    w_qa, w_kc, w_vc, w_ks, w_vs, w_kw, w_vw, w_g, w_qb, w_kb, w_vb, w_gate_a, w_gate_b = part
    w_qa = _pair_head_columns(w_qa) * (ATTN_SCALE * LOG2E)
    w_qb = _pair_head_columns(w_qb) * (ATTN_SCALE * LOG2E)
    w_ga = w_g.reshape(-1, N_GROUPS, GROUP, 3).transpose(0, 3, 2, 1).reshape(-1, 24)
    w_ga = jnp.pad(w_ga, ((0, 0), (0, LANES - 24)))
    w_all = jnp.concatenate([w_qa, w_qb, w_kc, w_vc, w_ks, w_vs, w_kw, w_vw, w_kb, w_vb, w_ga, w_gate_a, w_gate_b],
                            axis=1).astype(MXU_DTYPE)
    n_all = w_all.shape[1]
    out_shape = []
    out_specs = []
    for name, width in _IN_COLS:
        dt = F32 if name == 'ga' else BF16
        out_shape.append(jax.ShapeDtypeStruct((t, width), dt))
        out_specs.append(pl.BlockSpec((IN_TM, width), lambda i: (i, 0)))
    outs = pl.pallas_call(
        _inproj_kernel,
        grid=(t // IN_TM,),
        in_specs=[pl.BlockSpec((IN_TM, D_MODEL), lambda i: (i, 0)),
                  pl.BlockSpec((D_MODEL, n_all), lambda i: (0, 0))],
        out_specs=out_specs,
        out_shape=out_shape,
        compiler_params=pltpu.CompilerParams(dimension_semantics=("arbitrary",), vmem_limit_bytes=VMEM_LIMIT),
        name="in_projection",
    )(x2, w_all)
    return dict(zip([n for n, _ in _IN_COLS], outs))


def _compress_kernel(tok_ref, w1_ref, pe_ref, w1o_ref, b1_ref, w2_ref, out_ref):
    n_chunks = tok_ref.shape[2]
    ab = _dot(tok_ref[0, 0], w1_ref[0])
    a = ab[:, :2 * CMP_HIDDEN]
    b_next = pltpu.roll(ab[:, 2 * CMP_HIDDEN:], n_chunks - 1, 0)
    cb = _dot(_mx(pe_ref[0]), _mx(w1o_ref[0]))[0:1, :] + b1_ref[0]
    cb2 = jnp.concatenate([cb, cb], axis=1)
    hid = jax.nn.gelu(a + b_next + cb2)
    out = _dot(_mx(hid), w2_ref[0])
    row = lax.broadcasted_iota(jnp.int32, out.shape, 0)
    out = jnp.where(row < n_chunks - 1, out, 0.0)
    out_ref[0, 0, 0:CMP_FRONT, :] = jnp.zeros((CMP_FRONT, LANES), F32)
    out_ref[0, 0, CMP_FRONT:CMP_FRONT + n_chunks, :] = out
    out_ref[0, 0, CMP_FRONT + n_chunks:, :] = jnp.zeros((CMP_NEAR - CMP_FRONT, LANES), F32)


def _compress(kc, vc, bsz, seq, cmp_pe, cmp_w1, cmp_b1, cmp_w2):
    n_chunks = seq // CMP_STRIDE
    tok = jnp.stack([kc, vc]).reshape(2, bsz, n_chunks, CMP_STRIDE * LANES)
    eye = jnp.eye(N_GROUPS, dtype=F32)
    w1r = cmp_w1.reshape(2, 2, CMP_STRIDE, HEAD_DIM, CMP_HIDDEN)
    w1 = jnp.einsum('khjdn,gG->kjgdhGn', w1r, eye).reshape(2, CMP_STRIDE * LANES, 4 * CMP_HIDDEN).astype(MXU_DTYPE)
    w2 = jnp.einsum('knd,gG->kgnGd', cmp_w2, eye).reshape(2, 2 * CMP_HIDDEN, LANES).astype(MXU_DTYPE)
    pe = jnp.pad(cmp_pe.reshape(2, 1, CMP_BLOCK * HEAD_DIM), ((0, 0), (0, 7), (0, 0)))
    b1 = cmp_b1.reshape(2, 1, CMP_HIDDEN)
    rows = CMP_FRONT + n_chunks + CMP_NEAR - CMP_FRONT
    return pl.pallas_call(
        _compress_kernel,
        grid=(2, bsz),
        in_specs=[pl.BlockSpec((1, 1, n_chunks, CMP_STRIDE * LANES), lambda k, b: (k, b, 0, 0)),
                  pl.BlockSpec((1, CMP_STRIDE * LANES, 4 * CMP_HIDDEN), lambda k, b: (k, 0, 0)),
                  pl.BlockSpec((1, 8, CMP_BLOCK * HEAD_DIM), lambda k, b: (k, 0, 0)),
                  pl.BlockSpec((1, CMP_BLOCK * HEAD_DIM, CMP_HIDDEN), lambda k, b: (k, 0, 0)),
                  pl.BlockSpec((1, 1, CMP_HIDDEN), lambda k, b: (k, 0, 0)),
                  pl.BlockSpec((1, 2 * CMP_HIDDEN, LANES), lambda k, b: (k, 0, 0))],
        out_specs=pl.BlockSpec((1, 1, rows, LANES), lambda k, b: (k, b, 0, 0)),
        out_shape=jax.ShapeDtypeStruct((2, bsz, rows, LANES), F32),
        compiler_params=pltpu.CompilerParams(dimension_semantics=("arbitrary", "arbitrary"),
                                             vmem_limit_bytes=VMEM_LIMIT),
        name="nsa_compress",
    )(tok, w1, pe, cmp_w1, b1, w2)


def _stack_heads(q_ref, dst):
    lo = lax.broadcasted_iota(jnp.int32, (Q_BLOCK, LANES), 1) < HEAD_DIM
    for r in range(GROUP):
        qr = q_ref[:, r * LANES:(r + 1) * LANES].astype(dst.dtype)
        z = jnp.zeros_like(qr)
        dst[(2 * r) * Q_BLOCK:(2 * r + 1) * Q_BLOCK, :] = jnp.where(lo, qr, z)
        dst[(2 * r + 1) * Q_BLOCK:(2 * r + 2) * Q_BLOCK, :] = jnp.where(lo, z, qr)


def _pair_heads(o, r):
    lo = lax.broadcasted_iota(jnp.int32, (Q_BLOCK, LANES), 1) < HEAD_DIM
    return jnp.where(lo, o[(2 * r) * Q_BLOCK:(2 * r + 1) * Q_BLOCK], o[(2 * r + 1) * Q_BLOCK:(2 * r + 2) * Q_BLOCK])


def _lane_tiles(x):
    return [x[:, t * LANES:(t + 1) * LANES] for t in range(x.shape[1] // LANES)]


def _row_max(tiles):
    mx = tiles[0]
    for t in tiles[1:]:
        mx = jnp.maximum(mx, t)
    return jnp.broadcast_to(jnp.max(mx, axis=1, keepdims=True), mx.shape)


def _with_ones(v):
    return jnp.concatenate([v, jnp.ones(v.shape, v.dtype)], axis=1)


def _block_of_key(n_keys, first_block):
    b = lax.broadcasted_iota(jnp.int32, (LANES, n_keys), 0)
    k = lax.broadcasted_iota(jnp.int32, (LANES, n_keys), 1)
    return (b == (k // SEL_BLOCK) + first_block).astype(MXU_DTYPE)


def _select_blocks_t(imp_t, i, n_top):
    blk = lax.broadcasted_iota(jnp.int32, imp_t.shape, 0)
    qcol = lax.broadcasted_iota(jnp.int32, imp_t.shape, 1)
    back = (2 * i + (qcol >= SEL_BLOCK).astype(jnp.int32)) - blk
    sel = (back >= 0) & ((blk < SEL_INIT_BLOCKS) | (back < SEL_LOCAL_BLOCKS))
    cand = jnp.where((back >= SEL_LOCAL_BLOCKS) & (blk >= SEL_INIT_BLOCKS), imp_t, -1.0)
    blk_f = blk.astype(F32)
    for _ in range(n_top - SEL_INIT_BLOCKS - SEL_LOCAL_BLOCKS):
        m = jnp.max(cand, axis=0, keepdims=True)
        idx = jnp.min(jnp.where(cand == m, blk_f, float(LANES)), axis=0, keepdims=True)
        hit = blk_f == idx
        sel = sel | (hit & (m >= 0.0))
        cand = jnp.where(hit, -2.0, cand)
    return sel


def _query_block(i, sink_ref, qa_ref, qb_ref, ga_ref, kcmp_ref, vcmp_ref, ks_ref, vs_ref, kw_ref, vw_ref, kb_ref,
                 vb_ref, cmat_ref, tnear_ref, tsel_ref, twin_ref, tswa_ref, oa_ref, ob_ref,
                 qall, qball, mneg, mneg_far, m_s, acc_s, s_buf, oa_acc, qmask, n_far, n_top):
    rows = N_HEADS * Q_BLOCK
    half = rows // 2
    halves = (slice(0, half), slice(half, rows))
    _stack_heads(qa_ref, qall)
    _stack_heads(qb_ref, qball)
    nstart = pl.multiple_of(i * Q_BLOCK, Q_BLOCK)
    lo = lax.broadcasted_iota(jnp.int32, (Q_BLOCK, LANES), 1) < HEAD_DIM
    gates = ga_ref[...]

    def gate_tile(c, r):
        return jnp.where(lo, gates[:, c * 8 + 2 * r:c * 8 + 2 * r + 1], gates[:, c * 8 + 2 * r + 1:c * 8 + 2 * r + 2])

    def softmax_pv(s_tiles, v1, fix_max=None):
        m = _row_max(s_tiles)
        if fix_max is not None:
            m = fix_max(m)
        e = [jnp.exp2(t - m) for t in s_tiles]
        return e, m, _dot(_mx(jnp.concatenate(e, axis=1)), v1)

    off = pl.multiple_of(i * (Q_BLOCK // CMP_STRIDE), 8)
    k_cmp = _mx(jnp.concatenate([kcmp_ref[0, 0, 0:n_far, :], kcmp_ref[0, 0, pl.ds(off, CMP_NEAR), :]], axis=0))
    v_cmp = _with_ones(_mx(jnp.concatenate([vcmp_ref[0, 0, 0:n_far, :], vcmp_ref[0, 0, pl.ds(off, CMP_NEAR), :]],
                                           axis=0)))
    colf = lax.broadcasted_iota(jnp.int32, (1, n_far), 1)
    coln = lax.broadcasted_iota(jnp.int32, (1, CMP_NEAR), 1)
    col_ok = jnp.concatenate([(colf >= CMP_FRONT) & (colf < off), coln + off >= CMP_FRONT], axis=1)
    mask_c = jnp.where(col_ok, 0.0, NEG)
    no_key = lambda m: jnp.where(m > 0.5 * NEG, m, 0.0)
    p_cmp, o_c = [], []
    for rs in halves:
        tiles = _lane_tiles(_dot_nt(qall[rs, :], k_cmp) + mask_c)
        tiles[-1] = tiles[-1] + tnear_ref[rs, :]
        e, _, ov = softmax_pv(tiles, v_cmp, no_key)
        inv = 1.0 / jnp.maximum(ov[:, LANES:], 1e-30)
        o_c.append(ov[:, :LANES] * inv)
        p_cmp.append([t * inv for t in e])
    o_c = jnp.concatenate(o_c, axis=0)
    yield None

    def far_start(j):
        return pl.multiple_of(Q_BLOCK + j * SEL_CHUNK, Q_BLOCK)

    def far_logits(j, slot, masked):
        kc = _mx(ks_ref[0, pl.ds(far_start(j), SEL_CHUNK), :])
        if masked:
            key = lax.broadcasted_iota(jnp.int32, (SEL_CHUNK, LANES), 0)
            blk = lax.broadcasted_iota(jnp.int32, (SEL_CHUNK, LANES), 1)
            one_hot = (blk == key // SEL_BLOCK + j * (SEL_CHUNK // SEL_BLOCK)).astype(MXU_DTYPE)
            kc = jnp.concatenate([kc, one_hot], axis=1)
        for rs in halves:
            s_buf[slot, rs, :] = _dot_nt(qmask[rs, :] if masked else qall[rs, :], kc)

    far_logits(0, 0, False)

    blkcol = lax.broadcasted_iota(jnp.int32, (Q_BLOCK, LANES), 1)
    n_tiles = len(p_cmp[0])
    for g in range(N_GROUPS):
        imp = jnp.zeros((Q_BLOCK, LANES), F32)
        for t in range(n_tiles):
            pg = sum(p_cmp[r // 2][t][(2 * (r % 2) + g) * Q_BLOCK:(2 * (r % 2) + g + 1) * Q_BLOCK]
                     for r in range(GROUP))
            if t < n_tiles - 1:
                cm = _mx(cmat_ref[t * LANES:(t + 1) * LANES, :])
            else:
                cm = _mx(cmat_ref[pl.ds(off, CMP_NEAR), :])
            hi = _mx(pg)
            low = _mx(pg - hi.astype(F32))
            imp = imp + _dot(hi, cm) + _dot(low, cm)
        sel = _select_blocks_t(imp.T, i, n_top)
        neg = jnp.where(sel, 0.0, NEG).T
        mneg[g * Q_BLOCK:(g + 1) * Q_BLOCK, :] = neg.astype(mneg.dtype)
        neg_far = jnp.where(blkcol < 2 * (i - 1), neg, NEG).astype(mneg.dtype)
        mneg_far[g * Q_BLOCK:(g + 1) * Q_BLOCK, :] = neg_far
        for r in range(GROUP):
            qmask[(2 * r + g) * Q_BLOCK:(2 * r + g + 1) * Q_BLOCK, LANES:] = neg_far
    qmask[:, :LANES] = qall[...]

    yield None

    wpad = kw_ref.shape[1] - ks_ref.shape[1] + Q_BLOCK
    kwin = _mx(kw_ref[0, pl.ds(nstart, wpad + Q_BLOCK), :])
    vwin = _with_ones(_mx(vw_ref[0, pl.ds(nstart, wpad + Q_BLOCK), :]))
    colw = lax.broadcasted_iota(jnp.int32, (1, wpad + Q_BLOCK), 1)
    mask_w = jnp.where(colw + nstart >= wpad, 0.0, NEG)
    o_w = []
    for rs in halves:
        _, _, ov = softmax_pv(_lane_tiles(_dot_nt(qall[rs, :], kwin) + twin_ref[rs, :] + mask_w), vwin)
        o_w.append(ov[:, :LANES] / ov[:, LANES:])
    o_w = jnp.concatenate(o_w, axis=0)
    for r in range(GROUP):
        oa_acc[:, r * LANES:(r + 1) * LANES] = (gate_tile(0, r) * _pair_heads(o_c, r)
                                                + gate_tile(2, r) * _pair_heads(o_w, r))

    yield None

    bpad = kb_ref.shape[1] - ks_ref.shape[1] + Q_BLOCK
    kwin = _mx(kb_ref[0, pl.ds(nstart, bpad + Q_BLOCK), :])
    vwin = _with_ones(_mx(vb_ref[0, pl.ds(nstart, bpad + Q_BLOCK), :]))
    colb = lax.broadcasted_iota(jnp.int32, (1, bpad + Q_BLOCK), 1)
    mask_b = jnp.where(colb + nstart >= bpad, 0.0, NEG)
    o_b = []
    for hh, rs in enumerate(halves):
        sink = jnp.concatenate([jnp.full((Q_BLOCK, LANES), sink_ref[(h % 2) * GROUP + h // 2], F32)
                                for h in range(hh * N_HEADS // 2, (hh + 1) * N_HEADS // 2)], axis=0)
        _, m, ov = softmax_pv(_lane_tiles(_dot_nt(qball[rs, :], kwin) + tswa_ref[rs, :] + mask_b), vwin,
                              lambda m: jnp.maximum(m, sink))
        o_b.append(ov[:, :LANES] / (ov[:, LANES:] + jnp.exp2(sink - m)))
    o_b = jnp.concatenate(o_b, axis=0)
    for r in range(GROUP):
        ob_ref[:, r * LANES:(r + 1) * LANES] = _pair_heads(o_b, r).astype(ob_ref.dtype)

    yield None

    m_s[...] = jnp.full(m_s.shape, NEG, F32)
    acc_s[...] = jnp.zeros(acc_s.shape, F32)

    def flash_update(rs, s, v1):
        s_tiles = _lane_tiles(s)
        m_old = m_s[rs, :]
        m_new = jnp.maximum(m_old, _row_max(s_tiles))
        alpha = jnp.exp2(m_old - m_new)
        p = jnp.concatenate([jnp.exp2(t - m_new) for t in s_tiles], axis=1)
        acc_s[rs, :] = jnp.concatenate([alpha, alpha], axis=1) * acc_s[rs, :] + _dot(_mx(p), v1)
        m_s[rs, :] = m_new

    madd = _dot(mneg_far[...], _block_of_key(SEL_CHUNK, 0))
    for rs in halves:
        s_buf[0, rs, :] = s_buf[0, rs, :] + jnp.concatenate([madd] * (GROUP // 2), axis=0)

    def far_update(j):
        v1 = _with_ones(_mx(vs_ref[0, pl.ds(far_start(j), SEL_CHUNK), :]))
        for rs in halves:
            flash_update(rs, s_buf[j % 2, rs, :], v1)

    yield far_update, far_logits

    kc = _mx(ks_ref[0, pl.ds(nstart, 2 * Q_BLOCK), :])
    v1 = _with_ones(_mx(vs_ref[0, pl.ds(nstart, 2 * Q_BLOCK), :]))
    madd = _dot(mneg[...], _block_of_key(2 * Q_BLOCK, 2 * (i - 1)))
    col2 = lax.broadcasted_iota(jnp.int32, (1, 2 * Q_BLOCK), 1)
    mask_n = jnp.where((col2 < Q_BLOCK) & (i == 0), NEG, 0.0)
    for rs in halves:
        s = _dot_nt(qall[rs, :], kc) + jnp.concatenate([madd] * (GROUP // 2), axis=0) + tsel_ref[rs, :] + mask_n
        flash_update(rs, s, v1)
    acc = acc_s[...]
    o_s = acc[:, :LANES] / acc[:, LANES:]
    for r in range(GROUP):
        tile = oa_acc[:, r * LANES:(r + 1) * LANES] + gate_tile(1, r) * _pair_heads(o_s, r)
        oa_ref[:, r * LANES:(r + 1) * LANES] = tile.astype(oa_ref.dtype)
    yield None


def _attn_kernel(sink_ref, qa_ref, qb_ref, ga_ref, kcmp_ref, vcmp_ref, ks_ref, vs_ref, kw_ref, vw_ref, kb_ref,
                 vb_ref, cmat_ref, tnear_ref, tsel_ref, twin_ref, tswa_ref, oa_ref, ob_ref,
                 qall, qball, mneg, mneg_far, m_s, acc_s, s_buf, oa_acc, qmask, *, n_far, n_top):
    first = pl.program_id(1) * QB_PER_STEP
    blocks, steps = [], []
    for n in range(QB_PER_STEP):
        qrows = pl.ds(n * Q_BLOCK, Q_BLOCK)
        blk = _query_block(first + n, sink_ref, qa_ref.at[qrows, :], qb_ref.at[qrows, :], ga_ref.at[qrows, :],
                           kcmp_ref, vcmp_ref, ks_ref, vs_ref, kw_ref, vw_ref, kb_ref, vb_ref, cmat_ref, tnear_ref,
                           tsel_ref, twin_ref, tswa_ref, oa_ref.at[qrows, :], ob_ref.at[qrows, :],
                           qall.at[n], qball.at[n], mneg.at[n], mneg_far.at[n], m_s.at[n], acc_s.at[n], s_buf.at[n],
                           oa_acc.at[n], qmask.at[n], n_far, n_top)
        blocks.append(blk)
    steps = [next(blk) for blk in blocks]
    while steps[0] is None:
        steps = [next(blk) for blk in blocks]
    n_far_keys = jnp.maximum(first + QB_PER_STEP - 2, 0) * Q_BLOCK
    n_chunks = (n_far_keys + SEL_CHUNK - 1) // SEL_CHUNK

    def far_body(j, carry):
        for far_update, _ in steps:
            far_update(j)
        for _, far_logits in steps:
            far_logits(j + 1, (j + 1) % 2, True)
        return carry

    last = jnp.maximum(n_chunks - 1, 0)
    lax.fori_loop(0, last, far_body, 0)
    for far_update, _ in steps:
        far_update(last)
    for blk in blocks:
        next(blk)


def _rel_bucket_np(dist):
    n = np.maximum(dist, 0)
    max_exact = REL_BUCKETS // 2
    nf = np.maximum(n, 1).astype(np.float32)
    log_b = max_exact + (np.log(nf / max_exact) / math.log(REL_MAX_DIST / max_exact)
                         * (REL_BUCKETS - max_exact)).astype(np.int32)
    log_b = np.minimum(log_b, REL_BUCKETS - 1)
    return np.where(n < max_exact, n, log_b)


def _toeplitz_bias(tab, pad, width, window, shift_far):
    length = width + Q_BLOCK
    dist = pad + Q_BLOCK - 1 - np.arange(length)
    onehot = np.zeros((length, REL_BUCKETS), np.float32)
    onehot[np.arange(length), _rel_bucket_np(dist)] = 1.0
    vals = jnp.dot(jnp.asarray(onehot), tab, precision=lax.Precision.HIGHEST)
    if shift_far:
        vals = vals - tab[REL_BUCKETS - 1][None, :]
    vals = vals * LOG2E
    valid = (dist >= 0) & (dist < window)
    vals = jnp.where(jnp.asarray(valid)[:, None], vals, NEG).T
    skew = jnp.tile(vals, (1, Q_BLOCK))[:, :Q_BLOCK * (length - 1)].reshape(N_HEADS, Q_BLOCK, length - 1)
    return skew[:, :, Q_BLOCK - 1:Q_BLOCK - 1 + width].reshape(N_HEADS * Q_BLOCK, width).astype(F32)


def _attention(proj, kvcmp, sinks, bias_table, bsz, seq):
    assert seq % SEL_CHUNK == 0
    nq = seq // Q_BLOCK
    n_far = seq // CMP_STRIDE
    n_sel = seq // SEL_BLOCK
    n_top = min(SEL_TOP_N, n_sel)
    assert n_top >= SEL_INIT_BLOCKS + SEL_LOCAL_BLOCKS and n_sel <= LANES
    wpad = Q_BLOCK * (-(-(NSA_WINDOW - 1) // Q_BLOCK))
    bpad = Q_BLOCK * (-(-(SWA_WINDOW - 1) // Q_BLOCK))
    pair = lambda tab: tab.astype(F32).reshape(REL_BUCKETS, N_GROUPS, GROUP).transpose(0, 2, 1).reshape(REL_BUCKETS, -1)
    tab_a = pair(bias_table[:, :N_HEADS])
    tab_b = pair(bias_table[:, N_HEADS:])
    near_pad = CMP_STRIDE * CMP_FRONT - (CMP_BLOCK - 1)
    t_near = _toeplitz_bias(tab_a, near_pad, CMP_STRIDE * CMP_NEAR, 1 << 30, True)[:, ::CMP_STRIDE]
    t_sel = _toeplitz_bias(tab_a, Q_BLOCK, 2 * Q_BLOCK, 1 << 30, True)
    t_win = _toeplitz_bias(tab_a, wpad, wpad + Q_BLOCK, NSA_WINDOW, False)
    t_swa = _toeplitz_bias(tab_b, bpad, bpad + Q_BLOCK, SWA_WINDOW, False)
    n_rows = kvcmp.shape[2]
    cn = (np.arange(n_rows) - CMP_FRONT)[:, None] * CMP_STRIDE
    sj = np.arange(LANES)[None, :] * SEL_BLOCK
    cmat = ((cn < sj + SEL_BLOCK) & (cn + CMP_BLOCK > sj) & (cn >= 0) & (cn + CMP_BLOCK <= seq)
            & (sj < seq)).astype(np.float32)
    cmat = jnp.asarray(cmat, F32)
    padded = lambda name, p: jnp.pad(proj[name].reshape(bsz, seq, LANES), ((0, 0), (p, 0), (0, 0)))
    ks, vs = padded('ks', Q_BLOCK), padded('vs', Q_BLOCK)
    kw, vw = padded('kw', wpad), padded('vw', wpad)
    kb, vb = padded('kb', bpad), padded('vb', bpad)
    rows = N_HEADS * Q_BLOCK
    n_steps = nq // QB_PER_STEP
    qspec = pl.BlockSpec((QB_PER_STEP * Q_BLOCK, 4 * LANES), lambda b, i: (b * n_steps + i, 0))
    const2 = lambda shape: pl.BlockSpec(shape, lambda b, i: (0, 0))
    batch3 = lambda n: pl.BlockSpec((1, n, LANES), lambda b, i: (b, 0, 0))
    per_block = lambda shape, dtype: pltpu.VMEM((QB_PER_STEP,) + shape, dtype)
    kernel = functools.partial(_attn_kernel, n_far=n_far, n_top=n_top)
    return pl.pallas_call(
        kernel,
        grid=(bsz, n_steps),
        in_specs=[pl.BlockSpec(memory_space=pltpu.SMEM),
                  qspec, qspec,
                  pl.BlockSpec((QB_PER_STEP * Q_BLOCK, LANES), lambda b, i: (b * n_steps + i, 0)),
                  pl.BlockSpec((1, 1, n_rows, LANES), lambda b, i: (0, b, 0, 0)),
                  pl.BlockSpec((1, 1, n_rows, LANES), lambda b, i: (1, b, 0, 0)),
                  batch3(seq + Q_BLOCK), batch3(seq + Q_BLOCK),
                  batch3(seq + wpad), batch3(seq + wpad),
                  batch3(seq + bpad), batch3(seq + bpad),
                  const2((n_rows, LANES)),
                  const2((rows, CMP_NEAR)),
                  const2((rows, 2 * Q_BLOCK)),
                  const2((rows, wpad + Q_BLOCK)),
                  const2((rows, bpad + Q_BLOCK))],
        out_specs=[qspec, qspec],
        out_shape=[jax.ShapeDtypeStruct((bsz * seq, 4 * LANES), BF16)] * 2,
        scratch_shapes=[per_block((rows, LANES), MXU_DTYPE),
                        per_block((rows, LANES), MXU_DTYPE),
                        per_block((N_GROUPS * Q_BLOCK, LANES), MXU_DTYPE),
                        per_block((N_GROUPS * Q_BLOCK, LANES), MXU_DTYPE),
                        per_block((rows, LANES), F32),
                        per_block((rows, 2 * LANES), F32),
                        per_block((2, rows, SEL_CHUNK), F32),
                        per_block((Q_BLOCK, 4 * LANES), F32),
                        per_block((rows, 2 * LANES), MXU_DTYPE)],
        compiler_params=pltpu.CompilerParams(dimension_semantics=("arbitrary", "arbitrary"),
                                             vmem_limit_bytes=VMEM_LIMIT),
        name="attention",
    )(sinks.astype(F32) * LOG2E, proj['qa'], proj['qb'], proj['ga'], kvcmp, kvcmp, ks, vs, kw, vw, kb, vb,
      cmat, t_near, t_sel, t_win, t_swa)


def _layer_norm(y, g, b):
    mu = jnp.mean(y, axis=-1, keepdims=True)
    yc = y - mu
    var = jnp.mean(yc * yc, axis=-1, keepdims=True)
    return yc * lax.rsqrt(var + LN_EPS) * g + b


def _outproj_kernel(oa_ref, ob_ref, sg_ref, x_ref, pa_ref, pb_ref, wo_ref, g1_ref, b1_ref, wr_ref, rb_ref, sgu_ref,
                    sd_ref, tri_ref, h_ref, base_ref, eidx_ref, gate_ref, rank_ref, cnt_ref, carry):
    step = pl.program_id(0)
    tm = oa_ref.shape[0]

    @pl.when(step == 0)
    def _():
        carry[...] = jnp.zeros(carry.shape, F32)

    sg = sg_ref[...].astype(F32)
    merged = (sg[:, :D_MODEL] * _dot(_mx(oa_ref[...]), pa_ref[...])
              + sg[:, D_MODEL:] * _dot(_mx(ob_ref[...]), pb_ref[...]))
    mix = _dot(_mx(merged), wo_ref[...])
    h = _layer_norm(DN_ALPHA * x_ref[...] + mix, g1_ref[...], b1_ref[...])
    hb = _mx(h)
    h_ref[...] = _pack_bf16_pairs(h)

    gu = _dot(hb, sgu_ref[...])
    shared = _dot(_mx(jax.nn.silu(gu[:, :SHARED_HIDDEN]) * gu[:, SHARED_HIDDEN:]), sd_ref[...])
    base_ref[...] = DN_ALPHA * h + shared

    scores = jax.nn.sigmoid(_dot_nt(wr_ref[...], hb))
    choice = scores + rb_ref[:, 0:1]
    per_group = N_EXPERTS // N_EXPERT_GROUPS
    gs = []
    for g in range(N_EXPERT_GROUPS):
        cg = choice[g * per_group:(g + 1) * per_group]
        m1 = jnp.max(cg, axis=0, keepdims=True)
        is_m = cg == m1
        n_m = jnp.sum(is_m.astype(F32), axis=0, keepdims=True)
        m2 = jnp.max(jnp.where(is_m, -jnp.inf, cg), axis=0, keepdims=True)
        gs.append(m1 + jnp.where(n_m > 1.5, m1, m2))
    gs = jnp.concatenate(gs, axis=0)
    gid = lax.broadcasted_iota(jnp.int32, gs.shape, 0)
    beaten = jnp.zeros(gs.shape, jnp.int32)
    for g in range(N_EXPERT_GROUPS):
        other = gs[g:g + 1]
        beaten = beaten + ((other > gs) | ((other == gs) & (g < gid))).astype(jnp.int32)
    keep_g = beaten < TOPK_EXPERT_GROUPS
    keep = jnp.concatenate([jnp.broadcast_to(keep_g[g:g + 1], (per_group, tm)) for g in range(N_EXPERT_GROUPS)],
                           axis=0)
    cand = jnp.where(keep, choice, -jnp.inf)
    eid = lax.broadcasted_iota(jnp.int32, cand.shape, 0)
    hits = []
    e_rows = []
    w_rows = []
    for _ in range(TOP_K):
        m = jnp.max(cand, axis=0, keepdims=True)
        idx = jnp.min(jnp.where(cand == m, eid, N_EXPERTS), axis=0, keepdims=True)
        hit = eid == idx
        hits.append(hit)
        e_rows.append(idx)
        w_rows.append(jnp.sum(jnp.where(hit, scores, 0.0), axis=0, keepdims=True))
        cand = jnp.where(hit, -jnp.inf, cand)
    w = jnp.concatenate(w_rows, axis=0)
    gate_ref[...] = w / jnp.sum(w, axis=0, keepdims=True) * ROUTED_SCALE
    eidx_ref[...] = jnp.concatenate(e_rows, axis=0)

    onehot = jnp.zeros(cand.shape, F32)
    for hit in hits:
        onehot = onehot + hit.astype(F32)
    before = _dot(onehot.astype(BF16), tri_ref[...]) + carry[:, 0:1]
    rank_ref[...] = jnp.concatenate(
        [jnp.sum(jnp.where(hit, before, 0.0), axis=0, keepdims=True) for hit in hits], axis=0).astype(jnp.int32)
    carry[...] = carry[...] + jnp.sum(onehot, axis=1, keepdims=True)
    cnt_ref[...] = carry[...]


def _out_projection(oa, ob, sg, x2, proj_a, proj_b, w_out, ln_g, ln_b, w_router, router_bias, s_gate, s_up, s_down):
    t = x2.shape[0]
    tm = OUT_TM
    pair_rows = lambda p: p.reshape(N_GROUPS, GROUP, HEAD_DIM, -1).transpose(1, 0, 2, 3).reshape(p.shape)
    pa = pair_rows(proj_a).astype(MXU_DTYPE)
    pb = pair_rows(proj_b).astype(MXU_DTYPE)
    tri = jnp.asarray(np.triu(np.ones((tm, tm), np.float32), 1), BF16)
    row = lambda i: (i, 0)
    fixed = lambda i: (0, 0)
    col = lambda i: (0, i)
    outs = pl.pallas_call(
        _outproj_kernel,
        grid=(t // tm,),
        in_specs=[pl.BlockSpec((tm, 4 * LANES), row), pl.BlockSpec((tm, 4 * LANES), row),
                  pl.BlockSpec((tm, 2 * D_MODEL), row), pl.BlockSpec((tm, D_MODEL), row),
                  pl.BlockSpec((4 * LANES, D_MODEL), fixed), pl.BlockSpec((4 * LANES, D_MODEL), fixed),
                  pl.BlockSpec((D_MODEL, D_MODEL), fixed),
                  pl.BlockSpec((1, D_MODEL), fixed), pl.BlockSpec((1, D_MODEL), fixed),
                  pl.BlockSpec((N_EXPERTS, D_MODEL), fixed), pl.BlockSpec((N_EXPERTS, LANES), fixed),
                  pl.BlockSpec((D_MODEL, 2 * SHARED_HIDDEN), fixed), pl.BlockSpec((SHARED_HIDDEN, D_MODEL), fixed),
                  pl.BlockSpec((tm, tm), fixed)],
        out_specs=[pl.BlockSpec((tm, D_MODEL // 2), row), pl.BlockSpec((tm, D_MODEL), row),
                   pl.BlockSpec((TOP_K, tm), col), pl.BlockSpec((TOP_K, tm), col), pl.BlockSpec((TOP_K, tm), col),
                   pl.BlockSpec((N_EXPERTS, LANES), fixed)],
        out_shape=[jax.ShapeDtypeStruct((t, D_MODEL // 2), jnp.uint32), jax.ShapeDtypeStruct((t, D_MODEL), F32),
                   jax.ShapeDtypeStruct((TOP_K, t), jnp.int32), jax.ShapeDtypeStruct((TOP_K, t), F32),
                   jax.ShapeDtypeStruct((TOP_K, t), jnp.int32), jax.ShapeDtypeStruct((N_EXPERTS, LANES), F32)],
        scratch_shapes=[pltpu.VMEM((N_EXPERTS, LANES), F32)],
        compiler_params=pltpu.CompilerParams(dimension_semantics=("arbitrary",), vmem_limit_bytes=VMEM_LIMIT),
        name="out_projection_router",
    )(oa, ob, sg, x2, pa, pb, w_out.astype(MXU_DTYPE), ln_g.reshape(1, -1), ln_b.reshape(1, -1),
      w_router.T.astype(MXU_DTYPE), jnp.broadcast_to(router_bias.astype(F32)[:, None], (N_EXPERTS, LANES)),
      jnp.concatenate([s_gate, s_up], axis=1).astype(MXU_DTYPE), s_down.astype(MXU_DTYPE), tri)
    return outs


def _rows_to_tiles(x):
    return pltpu.einshape("cml->mcl", jnp.stack(_lane_tiles(x), axis=0))


def _tiles_to_rows(x3):
    xt = pltpu.einshape("mcl->cml", x3)
    return jnp.concatenate([xt[c] for c in range(xt.shape[0])], axis=1)


def _dispatch_kernel(pend_ref, pad_ref, dest_ref, h2_ref, xs_ref, h_ref, zeros, sem, zsem):
    step = pl.program_id(0)
    tm = h2_ref.shape[0]
    slot = step % 2
    h_ref[slot] = _rows_to_tiles(h2_ref[...])

    @pl.when(step == 0)
    def _():
        zeros[...] = jnp.zeros(zeros.shape, zeros.dtype)

        def for_pieces(action):
            def body(e, c):
                for piece in range(MOE_BM // ZERO_ROWS):
                    @pl.when(pad_ref[e] > piece * ZERO_ROWS)
                    def _():
                        start = pend_ref[e] - (piece + 1) * ZERO_ROWS
                        action(pltpu.make_async_copy(zeros, xs_ref.at[pl.ds(start, ZERO_ROWS)], zsem))
                return c
            lax.fori_loop(0, N_EXPERTS, body, 0)
        for_pieces(lambda cp: cp.start())
        for_pieces(lambda cp: cp.wait())

    def issue(t, c):
        for k in range(TOP_K):
            pltpu.make_async_copy(h_ref.at[slot, t], xs_ref.at[dest_ref[k, t]], sem.at[slot]).start(priority=k % 2)
        return c
    lax.fori_loop(0, tm, issue, 0)

    def wait_tile(s):
        for k in range(TOP_K):
            pltpu.make_async_copy(h_ref.at[s], xs_ref.at[pl.ds(0, tm)], sem.at[s]).wait()

    @pl.when(step > 0)
    def _():
        wait_tile(1 - slot)

    @pl.when(step + 1 == pl.num_programs(0))
    def _():
        wait_tile(slot)


def _dispatch(h, dest, pends, pad_rows, n_rows):
    t = h.shape[0]
    tm = DISP_TM
    return pl.pallas_call(
        _dispatch_kernel,
        grid_spec=pltpu.PrefetchScalarGridSpec(
            num_scalar_prefetch=2,
            grid=(t // tm,),
            in_specs=[pl.BlockSpec((TOP_K, tm), lambda i, *_: (0, i), memory_space=pltpu.SMEM),
                      pl.BlockSpec((tm, D_MODEL // 2), lambda i, *_: (i, 0))],
            out_specs=pl.BlockSpec(memory_space=pl.ANY),
            scratch_shapes=[pltpu.VMEM((2, tm) + PACKED_ROW_TILE, jnp.uint32),
                            pltpu.VMEM((ZERO_ROWS,) + PACKED_ROW_TILE, jnp.uint32),
                            pltpu.SemaphoreType.DMA((2,)), pltpu.SemaphoreType.DMA(())]),
        out_shape=jax.ShapeDtypeStruct((n_rows,) + PACKED_ROW_TILE, jnp.uint32),
        compiler_params=pltpu.CompilerParams(dimension_semantics=("arbitrary",), vmem_limit_bytes=VMEM_LIMIT),
        name="moe_dispatch",
    )(pends, pad_rows, dest, h)


def _experts_kernel(blk_e_ref, nused_ref, xs_ref, wg_ref, wu_ref, wd_ref, ys_ref, wg_s, wu_s, wd_s, xb_s, y_s):
    s = pl.program_id(0)
    nused = nused_ref[0]
    n_blocks = pl.num_programs(0) - 2
    cur = jnp.clip(s - 1, 0, n_blocks - 1)

    @pl.when(s == 0)
    def _():
        xb_s[...] = jnp.zeros(xb_s.shape, xb_s.dtype)
        y_s[...] = jnp.zeros(y_s.shape, y_s.dtype)

    @pl.when((s <= 1) | (blk_e_ref[cur] != blk_e_ref[jnp.maximum(s - 2, 0)]))
    def _():
        wg_s[...] = _mx(wg_ref[0])
        wu_s[...] = _mx(wu_ref[0])
        wd_s[...] = _mx(wd_ref[0])

    def stages(slot):
        xb_s[slot] = _mx(jnp.concatenate(_unpack_bf16_pairs(_tiles_to_rows(xs_ref[...])), axis=1))
        xb = xb_s[1 - slot]
        hid = jax.nn.silu(_dot(xb, wg_s[...])) * _dot(xb, wu_s[...])
        ys_ref[...] = _rows_to_tiles(_pack_bf16_pairs(y_s[slot]))
        y_s[1 - slot] = _dot(_mx(hid), wd_s[...])

    for slot in range(2):
        @pl.when((s <= nused + 1) & (s % 2 == slot))
        def _():
            stages(slot)

    @pl.when(s > nused + 1)
    def _():
        ys_ref[...] = jnp.zeros(ys_ref.shape, ys_ref.dtype)


def _experts(xs, blk_e, nused, e_gate, e_up, e_down):
    n_rows = xs.shape[0]
    n_blocks = n_rows // MOE_BM
    xmap = lambda s, be, nu: (jnp.minimum(s, nu[0] - 1), 0, 0)
    wmap = lambda s, be, nu: (be[jnp.clip(s - 1, 0, n_blocks - 1)], 0, 0)
    omap = lambda s, be, nu: (jnp.where(s - 2 < nu[0], jnp.maximum(s - 2, 0), n_blocks - 1), 0, 0)
    return pl.pallas_call(
        _experts_kernel,
        grid_spec=pltpu.PrefetchScalarGridSpec(
            num_scalar_prefetch=2,
            grid=(n_blocks + 2,),
            in_specs=[pl.BlockSpec((MOE_BM,) + PACKED_ROW_TILE, xmap),
                      pl.BlockSpec((1, D_MODEL, EXPERT_HIDDEN), wmap),
                      pl.BlockSpec((1, D_MODEL, EXPERT_HIDDEN), wmap),
                      pl.BlockSpec((1, EXPERT_HIDDEN, D_MODEL), wmap)],
            out_specs=pl.BlockSpec((MOE_BM,) + PACKED_ROW_TILE, omap),
            scratch_shapes=[pltpu.VMEM((D_MODEL, EXPERT_HIDDEN), MXU_DTYPE),
                            pltpu.VMEM((D_MODEL, EXPERT_HIDDEN), MXU_DTYPE),
                            pltpu.VMEM((EXPERT_HIDDEN, D_MODEL), MXU_DTYPE),
                            pltpu.VMEM((2, MOE_BM, D_MODEL), MXU_DTYPE),
                            pltpu.VMEM((2, MOE_BM, D_MODEL), F32)]),
        out_shape=jax.ShapeDtypeStruct((n_rows,) + PACKED_ROW_TILE, jnp.uint32),
        compiler_params=pltpu.CompilerParams(dimension_semantics=("arbitrary",), vmem_limit_bytes=VMEM_LIMIT),
        name="moe_experts",
    )(blk_e, nused, xs, e_gate, e_up, e_down)


def _combine_kernel(dest_ref, dest_next_ref, gate_ref, base_ref, g2_ref, b2_ref, ys_ref, out_ref, buf, ysum, sem):
    step = pl.program_id(0)
    tm = base_ref.shape[0]
    slot = step % 2

    sub = ROW_TILE[0]

    def gather_rows(d_ref, s, t0):
        for u in range(sub):
            for k in range(TOP_K):
                pltpu.make_async_copy(ys_ref.at[d_ref[(t0 + u) * TOP_K + k]], buf.at[s, k * tm + t0 + u],
                                      sem.at[s]).start(priority=k % 2)

    def combine_rows(t0):
        y = base_ref[pl.ds(t0, sub), :]
        gates = gate_ref[pl.ds(t0, sub), :]
        for k in range(TOP_K):
            words = _tiles_to_rows(buf[slot, pl.ds(k * tm + t0, sub)])
            y = y + gates[:, k:k + 1] * jnp.concatenate(_unpack_bf16_pairs(words), axis=1)
        ysum[pl.ds(t0, sub), :] = y

    def for_token_groups(body):
        def trip(g, c):
            body(pl.multiple_of(g * sub, sub))
            return c
        lax.fori_loop(0, tm // sub, trip, 0)

    @pl.when(step == 0)
    def _():
        for_token_groups(lambda t0: gather_rows(dest_ref, 0, t0))

    pltpu.make_async_copy(ys_ref.at[pl.ds(0, tm * TOP_K)], buf.at[slot], sem.at[slot]).wait()

    @pl.when(step + 1 < pl.num_programs(0))
    def _():
        def both(t0):
            gather_rows(dest_next_ref, 1 - slot, t0)
            combine_rows(t0)
        for_token_groups(both)

    @pl.when(step + 1 == pl.num_programs(0))
    def _():
        for_token_groups(combine_rows)

    out_ref[...] = _layer_norm(ysum[...], g2_ref[...], b2_ref[...])


def _combine(ys3, dest, gate, base, ln_g, ln_b):
    t = base.shape[0]
    tm = COMB_TM
    n_tiles = t // tm
    dest_tk = dest.T.reshape(-1)
    return pl.pallas_call(
        _combine_kernel,
        grid=(n_tiles,),
        in_specs=[pl.BlockSpec((tm * TOP_K,), lambda i: (i,), memory_space=pltpu.SMEM),
                  pl.BlockSpec((tm * TOP_K,), lambda i: (jnp.minimum(i + 1, n_tiles - 1),), memory_space=pltpu.SMEM),
                  pl.BlockSpec((tm, TOP_K), lambda i: (i, 0)),
                  pl.BlockSpec((tm, D_MODEL), lambda i: (i, 0)),
                  pl.BlockSpec((1, D_MODEL), lambda i: (0, 0)),
                  pl.BlockSpec((1, D_MODEL), lambda i: (0, 0)),
                  pl.BlockSpec(memory_space=pl.ANY)],
        out_specs=pl.BlockSpec((tm, D_MODEL), lambda i: (i, 0)),
        out_shape=jax.ShapeDtypeStruct((t, D_MODEL), F32),
        scratch_shapes=[pltpu.VMEM((2, tm * TOP_K) + PACKED_ROW_TILE, jnp.uint32), pltpu.VMEM((tm, D_MODEL), F32),
                        pltpu.SemaphoreType.DMA((2,))],
        compiler_params=pltpu.CompilerParams(dimension_semantics=("arbitrary",), vmem_limit_bytes=VMEM_LIMIT),
        name="moe_combine",
    )(dest_tk, dest_tk, gate.T, base, ln_g.reshape(1, -1), ln_b.reshape(1, -1), ys3)


def _dest_kernel(pstart_ref, eidx_ref, rank_ref, dest_ref):
    eidx = eidx_ref[...]

    unroll = 8

    def body(g, dest):
        for u in range(unroll):
            e = g * unroll + u
            dest = dest + jnp.where(eidx == e, pstart_ref[e], 0)
        return dest
    dest_ref[...] = lax.fori_loop(0, N_EXPERTS // unroll, body, rank_ref[...])


def _dest_rows(pstarts, eidx, rank):
    t = eidx.shape[1]
    tl = 2048
    spec = pl.BlockSpec((TOP_K, tl), lambda i, *_: (0, i))
    return pl.pallas_call(
        _dest_kernel,
        grid_spec=pltpu.PrefetchScalarGridSpec(num_scalar_prefetch=1, grid=(t // tl,), in_specs=[spec, spec],
                                               out_specs=spec),
        out_shape=jax.ShapeDtypeStruct(eidx.shape, jnp.int32),
        compiler_params=pltpu.CompilerParams(dimension_semantics=("arbitrary",)),
        name="moe_dest_rows",
    )(pstarts, eidx, rank)


def _moe_layout(eidx, rank, counts):
    n_assign = eidx.size
    n_blocks = (n_assign + N_EXPERTS * (MOE_BM - 1)) // MOE_BM
    padded = (counts + MOE_BM - 1) // MOE_BM * MOE_BM
    pends = jnp.cumsum(padded)
    pstarts = (pends - padded).astype(jnp.int32)
    dest = _dest_rows(pstarts, eidx, rank)
    block_row = jnp.arange(n_blocks, dtype=jnp.int32) * MOE_BM
    blk_e = jnp.minimum(jnp.sum(pends[None, :] <= block_row[:, None], axis=1), N_EXPERTS - 1).astype(jnp.int32)
    nused = (pends[-1:] // MOE_BM).astype(jnp.int32)
    pad_rows = (padded - counts).astype(jnp.int32)
    return dest.astype(jnp.int32), blk_e, nused, pends.astype(jnp.int32), pad_rows, n_blocks * MOE_BM


def _layer(x, w_in, cmp_pe, cmp_w1, cmp_b1, cmp_w2, sinks, bias_table, proj_a, proj_b, w_out, ln1_g, ln1_b,
           w_router, router_bias, e_gate, e_up, e_down, s_gate, s_up, s_down, ln2_g, ln2_b):
    bsz, seq, d = x.shape
    x2 = x.reshape(bsz * seq, d)
    proj = _in_projection(x2, w_in)
    kvcmp = _compress(proj['kc'], proj['vc'], bsz, seq, cmp_pe, cmp_w1, cmp_b1, cmp_w2)
    oa, ob = _attention(proj, kvcmp, sinks, bias_table, bsz, seq)
    h, base, eidx, gate, rank, cnt = _out_projection(oa, ob, proj['sg'], x2, proj_a, proj_b, w_out, ln1_g, ln1_b,
                                                     w_router, router_bias, s_gate, s_up, s_down)
    counts = cnt[:, 0].astype(jnp.int32)
    dest, blk_e, nused, pends, pad_rows, n_rows = _moe_layout(eidx, rank, counts)
    xs = _dispatch(h, dest, pends, pad_rows, n_rows)
    ys = _experts(xs, blk_e, nused, e_gate, e_up, e_down)
    out = _combine(ys, dest, gate, base, ln2_g, ln2_b)
    return out.reshape(bsz, seq, d)


def kernel(x, w_in, cmp_pe, cmp_w1, cmp_b1, cmp_w2, attn_sinks, rel_bias_table, proj_a, proj_b, w_out, ln1_g, ln1_b,
           w_router, router_bias, expert_w_gate, expert_w_up, expert_w_down, shared_w_gate, shared_w_up,
           shared_w_down, ln2_g, ln2_b):
    h = x
    for l in range(DEPTH):
        h = _layer(h, w_in[l], cmp_pe[l], cmp_w1[l], cmp_b1[l], cmp_w2[l], attn_sinks[l], rel_bias_table, proj_a[l],
                   proj_b[l], w_out[l], ln1_g[l], ln1_b[l], w_router[l], router_bias[l], expert_w_gate[l],
                   expert_w_up[l], expert_w_down[l], shared_w_gate[l], shared_w_up[l], shared_w_down[l], ln2_g[l],
                   ln2_b[l])
    return h
```

```python
import functools
import math

import numpy as np
import jax
import jax.numpy as jnp
from jax import lax
from jax.experimental import pallas as pl
from jax.experimental.pallas import tpu as pltpu

F32 = jnp.float32
BF16 = jnp.bfloat16
MXU_DTYPE = jnp.bfloat16

D_MODEL = 1024
HEAD_DIM = 64
ATTN_SCALE = HEAD_DIM ** -0.5
LOG2E = math.log2(math.e)
Q_BLOCK = 128
N_HEADS = 8
N_GROUPS = 2
GROUP = 4
CMP_BLOCK = 32
CMP_STRIDE = 16
CMP_HIDDEN = 128
SEL_BLOCK = 64
SEL_TOP_N = 8
SEL_INIT_BLOCKS = 1
SEL_LOCAL_BLOCKS = 2
NSA_WINDOW = 512
SWA_WINDOW = 128
REL_BUCKETS = 32
REL_MAX_DIST = 128
N_EXPERTS = 256
TOP_K = 8
EXPERT_HIDDEN = 256
SHARED_HIDDEN = 256
N_EXPERT_GROUPS = 8
TOPK_EXPERT_GROUPS = 4
ROUTED_SCALE = 2.5
LN_EPS = 1e-5
DEPTH = 1
DN_ALPHA = (2 * DEPTH) ** 0.25

NEG = -1e30
LANES = 128
ROW_TILE = (8, LANES)
PACKED_ROW_TILE = (4, LANES)
CMP_FRONT = 16
CMP_NEAR = LANES
SEL_CHUNK = 1024
QB_PER_STEP = 1
VMEM_LIMIT = 56 * 1024 * 1024

IN_TM = 1024
OUT_TM = 512
MOE_BM = 512
ZERO_ROWS = 64
HBM_COPIES = 2
DISP_TM = 512
COMB_TM = 512


def _dot(a, b):
    return jnp.dot(a, b, preferred_element_type=F32)


def _dot_nt(a, b):
    return lax.dot_general(a, b, (((1,), (1,)), ((), ())), preferred_element_type=F32)


def _mx(a):
    return a.astype(MXU_DTYPE)


def _pack_bf16_pairs(x):
    half = x.shape[1] // 2
    bits = lax.bitcast_convert_type(x.astype(BF16).astype(F32), jnp.uint32)
    return (bits[:, half:] & jnp.uint32(0xFFFF0000)) | (bits[:, :half] >> 16)


def _unpack_bf16_pairs(words):
    return (lax.bitcast_convert_type(words << 16, F32),
            lax.bitcast_convert_type(words & jnp.uint32(0xFFFF0000), F32))


_IN_COLS = (('qa', 512), ('qb', 512), ('kc', 128), ('vc', 128), ('ks', 128), ('vs', 128), ('kw', 128),
            ('vw', 128), ('kb', 128), ('vb', 128), ('ga', 128), ('sg', 2048))


def _inproj_kernel(x_ref, w_ref, qa_ref, qb_ref, kc_ref, vc_ref, ks_ref, vs_ref, kw_ref, vw_ref, kb_ref, vb_ref,
                   ga_ref, sg_ref):
    xb = _mx(x_ref[...])
    outs = dict(qa=qa_ref, qb=qb_ref, kc=kc_ref, vc=vc_ref, ks=ks_ref, vs=vs_ref, kw=kw_ref, vw=vw_ref,
                kb=kb_ref, vb=vb_ref, ga=ga_ref, sg=sg_ref)
    tiles = [(name, c) for name, width in _IN_COLS for c in range(0, width, LANES)]
    chunk = 4
    for t0 in range(0, len(tiles), chunk):
        group = tiles[t0:t0 + chunk]
        y = _dot(xb, w_ref[:, t0 * LANES:(t0 + len(group)) * LANES])
        for j, (name, c) in enumerate(group):
            yj = y[:, j * LANES:(j + 1) * LANES]
            if name in ('ga', 'sg'):
                yj = jax.nn.sigmoid(yj)
            outs[name][:, c:c + LANES] = yj.astype(outs[name].dtype)


def _pair_head_columns(w):
    return w.reshape(w.shape[0], N_GROUPS, GROUP, HEAD_DIM).transpose(0, 2, 1, 3).reshape(w.shape[0], -1)


def _in_projection(x2, w_in):
    t = x2.shape[0]
    sizes = (512, 128, 128, 128, 128, 128, 128, 24, 512, 128, 128, 1024, 1024)
    offs = np.cumsum((0,) + sizes)
    part = [w_in[:, offs[k]:offs[k + 1]] for k in range(len(sizes))]
    w_qa, w_kc, w_vc, w_ks, w_vs, w_kw, w_vw, w_g, w_qb, w_kb, w_vb, w_gate_a, w_gate_b = part
    w_qa = _pair_head_columns(w_qa) * (ATTN_SCALE * LOG2E)
    w_qb = _pair_head_columns(w_qb) * (ATTN_SCALE * LOG2E)
    w_ga = w_g.reshape(-1, N_GROUPS, GROUP, 3).transpose(0, 3, 2, 1).reshape(-1, 24)
    w_ga = jnp.pad(w_ga, ((0, 0), (0, LANES - 24)))
    w_all = jnp.concatenate([w_qa, w_qb, w_kc, w_vc, w_ks, w_vs, w_kw, w_vw, w_kb, w_vb, w_ga, w_gate_a, w_gate_b],
                            axis=1).astype(MXU_DTYPE)
    n_all = w_all.shape[1]
    out_shape = []
    out_specs = []
    for name, width in _IN_COLS:
        dt = F32 if name == 'ga' else BF16
        out_shape.append(jax.ShapeDtypeStruct((t, width), dt))
        out_specs.append(pl.BlockSpec((IN_TM, width), lambda i: (i, 0)))
    outs = pl.pallas_call(
        _inproj_kernel,
        grid=(t // IN_TM,),
        in_specs=[pl.BlockSpec((IN_TM, D_MODEL), lambda i: (i, 0)),
                  pl.BlockSpec((D_MODEL, n_all), lambda i: (0, 0))],
        out_specs=out_specs,
        out_shape=out_shape,
        compiler_params=pltpu.CompilerParams(dimension_semantics=("arbitrary",), vmem_limit_bytes=VMEM_LIMIT),
        name="in_projection",
    )(x2, w_all)
    return dict(zip([n for n, _ in _IN_COLS], outs))


def _compress_kernel(tok_ref, w1_ref, pe_ref, w1o_ref, b1_ref, w2_ref, out_ref):
    n_chunks = tok_ref.shape[2]
    ab = _dot(tok_ref[0, 0], w1_ref[0])
    a = ab[:, :2 * CMP_HIDDEN]
    b_next = pltpu.roll(ab[:, 2 * CMP_HIDDEN:], n_chunks - 1, 0)
    cb = _dot(_mx(pe_ref[0]), _mx(w1o_ref[0]))[0:1, :] + b1_ref[0]
    cb2 = jnp.concatenate([cb, cb], axis=1)
    hid = jax.nn.gelu(a + b_next + cb2)
    out = _dot(_mx(hid), w2_ref[0])
    row = lax.broadcasted_iota(jnp.int32, out.shape, 0)
    out = jnp.where(row < n_chunks - 1, out, 0.0)
    out_ref[0, 0, 0:CMP_FRONT, :] = jnp.zeros((CMP_FRONT, LANES), F32)
    out_ref[0, 0, CMP_FRONT:CMP_FRONT + n_chunks, :] = out
    out_ref[0, 0, CMP_FRONT + n_chunks:, :] = jnp.zeros((CMP_NEAR - CMP_FRONT, LANES), F32)


def _compress(kc, vc, bsz, seq, cmp_pe, cmp_w1, cmp_b1, cmp_w2):
    n_chunks = seq // CMP_STRIDE
    tok = jnp.stack([kc, vc]).reshape(2, bsz, n_chunks, CMP_STRIDE * LANES)
    eye = jnp.eye(N_GROUPS, dtype=F32)
    w1r = cmp_w1.reshape(2, 2, CMP_STRIDE, HEAD_DIM, CMP_HIDDEN)
    w1 = jnp.einsum('khjdn,gG->kjgdhGn', w1r, eye).reshape(2, CMP_STRIDE * LANES, 4 * CMP_HIDDEN).astype(MXU_DTYPE)
    w2 = jnp.einsum('knd,gG->kgnGd', cmp_w2, eye).reshape(2, 2 * CMP_HIDDEN, LANES).astype(MXU_DTYPE)
    pe = jnp.pad(cmp_pe.reshape(2, 1, CMP_BLOCK * HEAD_DIM), ((0, 0), (0, 7), (0, 0)))
    b1 = cmp_b1.reshape(2, 1, CMP_HIDDEN)
    rows = CMP_FRONT + n_chunks + CMP_NEAR - CMP_FRONT
    return pl.pallas_call(
        _compress_kernel,
        grid=(2, bsz),
        in_specs=[pl.BlockSpec((1, 1, n_chunks, CMP_STRIDE * LANES), lambda k, b: (k, b, 0, 0)),
                  pl.BlockSpec((1, CMP_STRIDE * LANES, 4 * CMP_HIDDEN), lambda k, b: (k, 0, 0)),
                  pl.BlockSpec((1, 8, CMP_BLOCK * HEAD_DIM), lambda k, b: (k, 0, 0)),
                  pl.BlockSpec((1, CMP_BLOCK * HEAD_DIM, CMP_HIDDEN), lambda k, b: (k, 0, 0)),
                  pl.BlockSpec((1, 1, CMP_HIDDEN), lambda k, b: (k, 0, 0)),
                  pl.BlockSpec((1, 2 * CMP_HIDDEN, LANES), lambda k, b: (k, 0, 0))],
        out_specs=pl.BlockSpec((1, 1, rows, LANES), lambda k, b: (k, b, 0, 0)),
        out_shape=jax.ShapeDtypeStruct((2, bsz, rows, LANES), F32),
        compiler_params=pltpu.CompilerParams(dimension_semantics=("arbitrary", "arbitrary"),
                                             vmem_limit_bytes=VMEM_LIMIT),
        name="nsa_compress",
    )(tok, w1, pe, cmp_w1, b1, w2)


def _stack_heads(q_ref, dst):
    lo = lax.broadcasted_iota(jnp.int32, (Q_BLOCK, LANES), 1) < HEAD_DIM
    for r in range(GROUP):
        qr = q_ref[:, r * LANES:(r + 1) * LANES].astype(dst.dtype)
        z = jnp.zeros_like(qr)
        dst[(2 * r) * Q_BLOCK:(2 * r + 1) * Q_BLOCK, :] = jnp.where(lo, qr, z)
        dst[(2 * r + 1) * Q_BLOCK:(2 * r + 2) * Q_BLOCK, :] = jnp.where(lo, z, qr)


def _pair_heads(o, r):
    lo = lax.broadcasted_iota(jnp.int32, (Q_BLOCK, LANES), 1) < HEAD_DIM
    return jnp.where(lo, o[(2 * r) * Q_BLOCK:(2 * r + 1) * Q_BLOCK], o[(2 * r + 1) * Q_BLOCK:(2 * r + 2) * Q_BLOCK])


def _lane_tiles(x):
    return [x[:, t * LANES:(t + 1) * LANES] for t in range(x.shape[1] // LANES)]


def _row_max(tiles):
    mx = tiles[0]
    for t in tiles[1:]:
        mx = jnp.maximum(mx, t)
    return jnp.broadcast_to(jnp.max(mx, axis=1, keepdims=True), mx.shape)


def _with_ones(v):
    return jnp.concatenate([v, jnp.ones(v.shape, v.dtype)], axis=1)


def _block_of_key(n_keys, first_block):
    b = lax.broadcasted_iota(jnp.int32, (LANES, n_keys), 0)
    k = lax.broadcasted_iota(jnp.int32, (LANES, n_keys), 1)
    return (b == (k // SEL_BLOCK) + first_block).astype(MXU_DTYPE)


def _select_blocks_t(imp_t, i, n_top):
    blk = lax.broadcasted_iota(jnp.int32, imp_t.shape, 0)
    qcol = lax.broadcasted_iota(jnp.int32, imp_t.shape, 1)
    back = (2 * i + (qcol >= SEL_BLOCK).astype(jnp.int32)) - blk
    sel = (back >= 0) & ((blk < SEL_INIT_BLOCKS) | (back < SEL_LOCAL_BLOCKS))
    cand = jnp.where((back >= SEL_LOCAL_BLOCKS) & (blk >= SEL_INIT_BLOCKS), imp_t, -1.0)
    blk_f = blk.astype(F32)
    for _ in range(n_top - SEL_INIT_BLOCKS - SEL_LOCAL_BLOCKS):
        m = jnp.max(cand, axis=0, keepdims=True)
        idx = jnp.min(jnp.where(cand == m, blk_f, float(LANES)), axis=0, keepdims=True)
        hit = blk_f == idx
        sel = sel | (hit & (m >= 0.0))
        cand = jnp.where(hit, -2.0, cand)
    return sel


def _query_block(i, sink_ref, qa_ref, qb_ref, ga_ref, kcmp_ref, vcmp_ref, ks_ref, vs_ref, kw_ref, vw_ref, kb_ref,
                 vb_ref, cmat_ref, tnear_ref, tsel_ref, twin_ref, tswa_ref, oa_ref, ob_ref,
                 qall, qball, mneg, mneg_far, m_s, acc_s, s_buf, oa_acc, qmask, n_far, n_top):
    rows = N_HEADS * Q_BLOCK
    half = rows // 2
    halves = (slice(0, half), slice(half, rows))
    _stack_heads(qa_ref, qall)
    _stack_heads(qb_ref, qball)
    nstart = pl.multiple_of(i * Q_BLOCK, Q_BLOCK)
    lo = lax.broadcasted_iota(jnp.int32, (Q_BLOCK, LANES), 1) < HEAD_DIM
    gates = ga_ref[...]

    def gate_tile(c, r):
        return jnp.where(lo, gates[:, c * 8 + 2 * r:c * 8 + 2 * r + 1], gates[:, c * 8 + 2 * r + 1:c * 8 + 2 * r + 2])

    def softmax_pv(s_tiles, v1, fix_max=None):
        m = _row_max(s_tiles)
        if fix_max is not None:
            m = fix_max(m)
        e = [jnp.exp2(t - m) for t in s_tiles]
        return e, m, _dot(_mx(jnp.concatenate(e, axis=1)), v1)

    off = pl.multiple_of(i * (Q_BLOCK // CMP_STRIDE), 8)
    k_cmp = _mx(jnp.concatenate([kcmp_ref[0, 0, 0:n_far, :], kcmp_ref[0, 0, pl.ds(off, CMP_NEAR), :]], axis=0))
    v_cmp = _with_ones(_mx(jnp.concatenate([vcmp_ref[0, 0, 0:n_far, :], vcmp_ref[0, 0, pl.ds(off, CMP_NEAR), :]],
                                           axis=0)))
    colf = lax.broadcasted_iota(jnp.int32, (1, n_far), 1)
    coln = lax.broadcasted_iota(jnp.int32, (1, CMP_NEAR), 1)
    col_ok = jnp.concatenate([(colf >= CMP_FRONT) & (colf < off), coln + off >= CMP_FRONT], axis=1)
    mask_c = jnp.where(col_ok, 0.0, NEG)
    no_key = lambda m: jnp.where(m > 0.5 * NEG, m, 0.0)
    p_cmp, o_c = [], []
    for rs in halves:
        tiles = _lane_tiles(_dot_nt(qall[rs, :], k_cmp) + mask_c)
        tiles[-1] = tiles[-1] + tnear_ref[rs, :]
        e, _, ov = softmax_pv(tiles, v_cmp, no_key)
        inv = 1.0 / jnp.maximum(ov[:, LANES:], 1e-30)
        o_c.append(ov[:, :LANES] * inv)
        p_cmp.append([t * inv for t in e])
    o_c = jnp.concatenate(o_c, axis=0)
    yield None

    def far_start(j):
        return pl.multiple_of(Q_BLOCK + j * SEL_CHUNK, Q_BLOCK)

    def far_logits(j, slot, masked):
        kc = _mx(ks_ref[0, pl.ds(far_start(j), SEL_CHUNK), :])
        if masked:
            key = lax.broadcasted_iota(jnp.int32, (SEL_CHUNK, LANES), 0)
            blk = lax.broadcasted_iota(jnp.int32, (SEL_CHUNK, LANES), 1)
            one_hot = (blk == key // SEL_BLOCK + j * (SEL_CHUNK // SEL_BLOCK)).astype(MXU_DTYPE)
            kc = jnp.concatenate([kc, one_hot], axis=1)
        for rs in halves:
            s_buf[slot, rs, :] = _dot_nt(qmask[rs, :] if masked else qall[rs, :], kc)

    far_logits(0, 0, False)

    blkcol = lax.broadcasted_iota(jnp.int32, (Q_BLOCK, LANES), 1)
    n_tiles = len(p_cmp[0])
    for g in range(N_GROUPS):
        imp = jnp.zeros((Q_BLOCK, LANES), F32)
        for t in range(n_tiles):
            pg = sum(p_cmp[r // 2][t][(2 * (r % 2) + g) * Q_BLOCK:(2 * (r % 2) + g + 1) * Q_BLOCK]
                     for r in range(GROUP))
            if t < n_tiles - 1:
                cm = _mx(cmat_ref[t * LANES:(t + 1) * LANES, :])
            else:
                cm = _mx(cmat_ref[pl.ds(off, CMP_NEAR), :])
            hi = _mx(pg)
            low = _mx(pg - hi.astype(F32))
            imp = imp + _dot(hi, cm) + _dot(low, cm)
        sel = _select_blocks_t(imp.T, i, n_top)
        neg = jnp.where(sel, 0.0, NEG).T
        mneg[g * Q_BLOCK:(g + 1) * Q_BLOCK, :] = neg.astype(mneg.dtype)
        neg_far = jnp.where(blkcol < 2 * (i - 1), neg, NEG).astype(mneg.dtype)
        mneg_far[g * Q_BLOCK:(g + 1) * Q_BLOCK, :] = neg_far
        for r in range(GROUP):
            qmask[(2 * r + g) * Q_BLOCK:(2 * r + g + 1) * Q_BLOCK, LANES:] = neg_far
    qmask[:, :LANES] = qall[...]

    yield None

    wpad = kw_ref.shape[1] - ks_ref.shape[1] + Q_BLOCK
    kwin = _mx(kw_ref[0, pl.ds(nstart, wpad + Q_BLOCK), :])
    vwin = _with_ones(_mx(vw_ref[0, pl.ds(nstart, wpad + Q_BLOCK), :]))
    colw = lax.broadcasted_iota(jnp.int32, (1, wpad + Q_BLOCK), 1)
    mask_w = jnp.where(colw + nstart >= wpad, 0.0, NEG)
    o_w = []
    for rs in halves:
        _, _, ov = softmax_pv(_lane_tiles(_dot_nt(qall[rs, :], kwin) + twin_ref[rs, :] + mask_w), vwin)
        o_w.append(ov[:, :LANES] / ov[:, LANES:])
    o_w = jnp.concatenate(o_w, axis=0)
    for r in range(GROUP):
        oa_acc[:, r * LANES:(r + 1) * LANES] = (gate_tile(0, r) * _pair_heads(o_c, r)
                                                + gate_tile(2, r) * _pair_heads(o_w, r))

    yield None

    bpad = kb_ref.shape[1] - ks_ref.shape[1] + Q_BLOCK
    kwin = _mx(kb_ref[0, pl.ds(nstart, bpad + Q_BLOCK), :])
    vwin = _with_ones(_mx(vb_ref[0, pl.ds(nstart, bpad + Q_BLOCK), :]))
    colb = lax.broadcasted_iota(jnp.int32, (1, bpad + Q_BLOCK), 1)
    mask_b = jnp.where(colb + nstart >= bpad, 0.0, NEG)
    o_b = []
    for hh, rs in enumerate(halves):
        sink = jnp.concatenate([jnp.full((Q_BLOCK, LANES), sink_ref[(h % 2) * GROUP + h // 2], F32)
                                for h in range(hh * N_HEADS // 2, (hh + 1) * N_HEADS // 2)], axis=0)
        _, m, ov = softmax_pv(_lane_tiles(_dot_nt(qball[rs, :], kwin) + tswa_ref[rs, :] + mask_b), vwin,
                              lambda m: jnp.maximum(m, sink))
        o_b.append(ov[:, :LANES] / (ov[:, LANES:] + jnp.exp2(sink - m)))
    o_b = jnp.concatenate(o_b, axis=0)
    for r in range(GROUP):
        ob_ref[:, r * LANES:(r + 1) * LANES] = _pair_heads(o_b, r).astype(ob_ref.dtype)

    yield None

    m_s[...] = jnp.full(m_s.shape, NEG, F32)
    acc_s[...] = jnp.zeros(acc_s.shape, F32)

    def flash_update(rs, s, v1):
        s_tiles = _lane_tiles(s)
        m_old = m_s[rs, :]
        m_new = jnp.maximum(m_old, _row_max(s_tiles))
        alpha = jnp.exp2(m_old - m_new)
        p = jnp.concatenate([jnp.exp2(t - m_new) for t in s_tiles], axis=1)
        acc_s[rs, :] = jnp.concatenate([alpha, alpha], axis=1) * acc_s[rs, :] + _dot(_mx(p), v1)
        m_s[rs, :] = m_new

    madd = _dot(mneg_far[...], _block_of_key(SEL_CHUNK, 0))
    for rs in halves:
        s_buf[0, rs, :] = s_buf[0, rs, :] + jnp.concatenate([madd] * (GROUP // 2), axis=0)

    def far_update(j):
        v1 = _with_ones(_mx(vs_ref[0, pl.ds(far_start(j), SEL_CHUNK), :]))
        for rs in halves:
            flash_update(rs, s_buf[j % 2, rs, :], v1)

    yield far_update, far_logits

    kc = _mx(ks_ref[0, pl.ds(nstart, 2 * Q_BLOCK), :])
    v1 = _with_ones(_mx(vs_ref[0, pl.ds(nstart, 2 * Q_BLOCK), :]))
    madd = _dot(mneg[...], _block_of_key(2 * Q_BLOCK, 2 * (i - 1)))
    col2 = lax.broadcasted_iota(jnp.int32, (1, 2 * Q_BLOCK), 1)
    mask_n = jnp.where((col2 < Q_BLOCK) & (i == 0), NEG, 0.0)
    for rs in halves:
        s = _dot_nt(qall[rs, :], kc) + jnp.concatenate([madd] * (GROUP // 2), axis=0) + tsel_ref[rs, :] + mask_n
        flash_update(rs, s, v1)
    acc = acc_s[...]
    o_s = acc[:, :LANES] / acc[:, LANES:]
    for r in range(GROUP):
        tile = oa_acc[:, r * LANES:(r + 1) * LANES] + gate_tile(1, r) * _pair_heads(o_s, r)
        oa_ref[:, r * LANES:(r + 1) * LANES] = tile.astype(oa_ref.dtype)
    yield None


def _attn_kernel(sink_ref, qa_ref, qb_ref, ga_ref, kcmp_ref, vcmp_ref, ks_ref, vs_ref, kw_ref, vw_ref, kb_ref,
                 vb_ref, cmat_ref, tnear_ref, tsel_ref, twin_ref, tswa_ref, oa_ref, ob_ref,
                 qall, qball, mneg, mneg_far, m_s, acc_s, s_buf, oa_acc, qmask, *, n_far, n_top):
    first = pl.program_id(1) * QB_PER_STEP
    blocks, steps = [], []
    for n in range(QB_PER_STEP):
        qrows = pl.ds(n * Q_BLOCK, Q_BLOCK)
        blk = _query_block(first + n, sink_ref, qa_ref.at[qrows, :], qb_ref.at[qrows, :], ga_ref.at[qrows, :],
                           kcmp_ref, vcmp_ref, ks_ref, vs_ref, kw_ref, vw_ref, kb_ref, vb_ref, cmat_ref, tnear_ref,
                           tsel_ref, twin_ref, tswa_ref, oa_ref.at[qrows, :], ob_ref.at[qrows, :],
                           qall.at[n], qball.at[n], mneg.at[n], mneg_far.at[n], m_s.at[n], acc_s.at[n], s_buf.at[n],
                           oa_acc.at[n], qmask.at[n], n_far, n_top)
        blocks.append(blk)
    steps = [next(blk) for blk in blocks]
    while steps[0] is None:
        steps = [next(blk) for blk in blocks]
    n_far_keys = jnp.maximum(first + QB_PER_STEP - 2, 0) * Q_BLOCK
    n_chunks = (n_far_keys + SEL_CHUNK - 1) // SEL_CHUNK

    def far_body(j, carry):
        for far_update, _ in steps:
            far_update(j)
        for _, far_logits in steps:
            far_logits(j + 1, (j + 1) % 2, True)
        return carry

    last = jnp.maximum(n_chunks - 1, 0)
    lax.fori_loop(0, last, far_body, 0)
    for far_update, _ in steps:
        far_update(last)
    for blk in blocks:
        next(blk)


def _rel_bucket_np(dist):
    n = np.maximum(dist, 0)
    max_exact = REL_BUCKETS // 2
    nf = np.maximum(n, 1).astype(np.float32)
    log_b = max_exact + (np.log(nf / max_exact) / math.log(REL_MAX_DIST / max_exact)
                         * (REL_BUCKETS - max_exact)).astype(np.int32)
    log_b = np.minimum(log_b, REL_BUCKETS - 1)
    return np.where(n < max_exact, n, log_b)


def _toeplitz_bias(tab, pad, width, window, shift_far):
    length = width + Q_BLOCK
    dist = pad + Q_BLOCK - 1 - np.arange(length)
    onehot = np.zeros((length, REL_BUCKETS), np.float32)
    onehot[np.arange(length), _rel_bucket_np(dist)] = 1.0
    vals = jnp.dot(jnp.asarray(onehot), tab, precision=lax.Precision.HIGHEST)
    if shift_far:
        vals = vals - tab[REL_BUCKETS - 1][None, :]
    vals = vals * LOG2E
    valid = (dist >= 0) & (dist < window)
    vals = jnp.where(jnp.asarray(valid)[:, None], vals, NEG).T
    skew = jnp.tile(vals, (1, Q_BLOCK))[:, :Q_BLOCK * (length - 1)].reshape(N_HEADS, Q_BLOCK, length - 1)
    return skew[:, :, Q_BLOCK - 1:Q_BLOCK - 1 + width].reshape(N_HEADS * Q_BLOCK, width).astype(F32)


def _attention(proj, kvcmp, sinks, bias_table, bsz, seq):
    assert seq % SEL_CHUNK == 0
    nq = seq // Q_BLOCK
    n_far = seq // CMP_STRIDE
    n_sel = seq // SEL_BLOCK
    n_top = min(SEL_TOP_N, n_sel)
    assert n_top >= SEL_INIT_BLOCKS + SEL_LOCAL_BLOCKS and n_sel <= LANES
    wpad = Q_BLOCK * (-(-(NSA_WINDOW - 1) // Q_BLOCK))
    bpad = Q_BLOCK * (-(-(SWA_WINDOW - 1) // Q_BLOCK))
    pair = lambda tab: tab.astype(F32).reshape(REL_BUCKETS, N_GROUPS, GROUP).transpose(0, 2, 1).reshape(REL_BUCKETS, -1)
    tab_a = pair(bias_table[:, :N_HEADS])
    tab_b = pair(bias_table[:, N_HEADS:])
    near_pad = CMP_STRIDE * CMP_FRONT - (CMP_BLOCK - 1)
    t_near = _toeplitz_bias(tab_a, near_pad, CMP_STRIDE * CMP_NEAR, 1 << 30, True)[:, ::CMP_STRIDE]
    t_sel = _toeplitz_bias(tab_a, Q_BLOCK, 2 * Q_BLOCK, 1 << 30, True)
    t_win = _toeplitz_bias(tab_a, wpad, wpad + Q_BLOCK, NSA_WINDOW, False)
    t_swa = _toeplitz_bias(tab_b, bpad, bpad + Q_BLOCK, SWA_WINDOW, False)
    n_rows = kvcmp.shape[2]
    cn = (np.arange(n_rows) - CMP_FRONT)[:, None] * CMP_STRIDE
    sj = np.arange(LANES)[None, :] * SEL_BLOCK
    cmat = ((cn < sj + SEL_BLOCK) & (cn + CMP_BLOCK > sj) & (cn >= 0) & (cn + CMP_BLOCK <= seq)
            & (sj < seq)).astype(np.float32)
    cmat = jnp.asarray(cmat, F32)
    padded = lambda name, p: jnp.pad(proj[name].reshape(bsz, seq, LANES), ((0, 0), (p, 0), (0, 0)))
    ks, vs = padded('ks', Q_BLOCK), padded('vs', Q_BLOCK)
    kw, vw = padded('kw', wpad), padded('vw', wpad)
    kb, vb = padded('kb', bpad), padded('vb', bpad)
    rows = N_HEADS * Q_BLOCK
    n_steps = nq // QB_PER_STEP
    qspec = pl.BlockSpec((QB_PER_STEP * Q_BLOCK, 4 * LANES), lambda b, i: (b * n_steps + i, 0))
    const2 = lambda shape: pl.BlockSpec(shape, lambda b, i: (0, 0))
    batch3 = lambda n: pl.BlockSpec((1, n, LANES), lambda b, i: (b, 0, 0))
    per_block = lambda shape, dtype: pltpu.VMEM((QB_PER_STEP,) + shape, dtype)
    kernel = functools.partial(_attn_kernel, n_far=n_far, n_top=n_top)
    return pl.pallas_call(
        kernel,
        grid=(bsz, n_steps),
        in_specs=[pl.BlockSpec(memory_space=pltpu.SMEM),
                  qspec, qspec,
                  pl.BlockSpec((QB_PER_STEP * Q_BLOCK, LANES), lambda b, i: (b * n_steps + i, 0)),
                  pl.BlockSpec((1, 1, n_rows, LANES), lambda b, i: (0, b, 0, 0)),
                  pl.BlockSpec((1, 1, n_rows, LANES), lambda b, i: (1, b, 0, 0)),
                  batch3(seq + Q_BLOCK), batch3(seq + Q_BLOCK),
                  batch3(seq + wpad), batch3(seq + wpad),
                  batch3(seq + bpad), batch3(seq + bpad),
                  const2((n_rows, LANES)),
                  const2((rows, CMP_NEAR)),
                  const2((rows, 2 * Q_BLOCK)),
                  const2((rows, wpad + Q_BLOCK)),
                  const2((rows, bpad + Q_BLOCK))],
        out_specs=[qspec, qspec],
        out_shape=[jax.ShapeDtypeStruct((bsz * seq, 4 * LANES), BF16)] * 2,
        scratch_shapes=[per_block((rows, LANES), MXU_DTYPE),
                        per_block((rows, LANES), MXU_DTYPE),
                        per_block((N_GROUPS * Q_BLOCK, LANES), MXU_DTYPE),
                        per_block((N_GROUPS * Q_BLOCK, LANES), MXU_DTYPE),
                        per_block((rows, LANES), F32),
                        per_block((rows, 2 * LANES), F32),
                        per_block((2, rows, SEL_CHUNK), F32),
                        per_block((Q_BLOCK, 4 * LANES), F32),
                        per_block((rows, 2 * LANES), MXU_DTYPE)],
        compiler_params=pltpu.CompilerParams(dimension_semantics=("arbitrary", "arbitrary"),
                                             vmem_limit_bytes=VMEM_LIMIT),
        name="attention",
    )(sinks.astype(F32) * LOG2E, proj['qa'], proj['qb'], proj['ga'], kvcmp, kvcmp, ks, vs, kw, vw, kb, vb,
      cmat, t_near, t_sel, t_win, t_swa)


def _layer_norm(y, g, b):
    mu = jnp.mean(y, axis=-1, keepdims=True)
    yc = y - mu
    var = jnp.mean(yc * yc, axis=-1, keepdims=True)
    return yc * lax.rsqrt(var + LN_EPS) * g + b


def _outproj_kernel(oa_ref, ob_ref, sg_ref, x_ref, pa_ref, pb_ref, wo_ref, g1_ref, b1_ref, wr_ref, rb_ref, sgu_ref,
                    sd_ref, tri_ref, h_ref, base_ref, eidx_ref, gate_ref, rank_ref, cnt_ref, carry):
    step = pl.program_id(0)
    tm = oa_ref.shape[0]

    @pl.when(step == 0)
    def _():
        carry[...] = jnp.zeros(carry.shape, F32)

    sg = sg_ref[...].astype(F32)
    merged = (sg[:, :D_MODEL] * _dot(_mx(oa_ref[...]), pa_ref[...])
              + sg[:, D_MODEL:] * _dot(_mx(ob_ref[...]), pb_ref[...]))
    mix = _dot(_mx(merged), wo_ref[...])
    h = _layer_norm(DN_ALPHA * x_ref[...] + mix, g1_ref[...], b1_ref[...])
    hb = _mx(h)
    h_ref[...] = _rows_to_tiles(_pack_bf16_pairs(h))

    gu = _dot(hb, sgu_ref[...])
    shared = _dot(_mx(jax.nn.silu(gu[:, :SHARED_HIDDEN]) * gu[:, SHARED_HIDDEN:]), sd_ref[...])
    base_ref[...] = DN_ALPHA * h + shared

    scores = jax.nn.sigmoid(_dot_nt(wr_ref[...], hb))
    choice = scores + rb_ref[:, 0:1]
    per_group = N_EXPERTS // N_EXPERT_GROUPS
    gs = []
    for g in range(N_EXPERT_GROUPS):
        cg = choice[g * per_group:(g + 1) * per_group]
        m1 = jnp.max(cg, axis=0, keepdims=True)
        is_m = cg == m1
        n_m = jnp.sum(is_m.astype(F32), axis=0, keepdims=True)
        m2 = jnp.max(jnp.where(is_m, -jnp.inf, cg), axis=0, keepdims=True)
        gs.append(m1 + jnp.where(n_m > 1.5, m1, m2))
    gs = jnp.concatenate(gs, axis=0)
    gid = lax.broadcasted_iota(jnp.int32, gs.shape, 0)
    beaten = jnp.zeros(gs.shape, jnp.int32)
    for g in range(N_EXPERT_GROUPS):
        other = gs[g:g + 1]
        beaten = beaten + ((other > gs) | ((other == gs) & (g < gid))).astype(jnp.int32)
    keep_g = beaten < TOPK_EXPERT_GROUPS
    keep = jnp.concatenate([jnp.broadcast_to(keep_g[g:g + 1], (per_group, tm)) for g in range(N_EXPERT_GROUPS)],
                           axis=0)
    cand = jnp.where(keep, choice, -jnp.inf)
    eid = lax.broadcasted_iota(jnp.int32, cand.shape, 0)
    hits = []
    e_rows = []
    w_rows = []
    for _ in range(TOP_K):
        m = jnp.max(cand, axis=0, keepdims=True)
        idx = jnp.min(jnp.where(cand == m, eid, N_EXPERTS), axis=0, keepdims=True)
        hit = eid == idx
        hits.append(hit)
        e_rows.append(idx)
        w_rows.append(jnp.sum(jnp.where(hit, scores, 0.0), axis=0, keepdims=True))
        cand = jnp.where(hit, -jnp.inf, cand)
    w = jnp.concatenate(w_rows, axis=0)
    gate_ref[...] = w / jnp.sum(w, axis=0, keepdims=True) * ROUTED_SCALE
    eidx_ref[...] = jnp.concatenate(e_rows, axis=0)

    onehot = jnp.zeros(cand.shape, F32)
    for hit in hits:
        onehot = onehot + hit.astype(F32)
    before = _dot(onehot.astype(BF16), tri_ref[...]) + carry[:, 0:1]
    rank_ref[...] = jnp.concatenate(
        [jnp.sum(jnp.where(hit, before, 0.0), axis=0, keepdims=True) for hit in hits], axis=0).astype(jnp.int32)
    carry[...] = carry[...] + jnp.sum(onehot, axis=1, keepdims=True)
    cnt_ref[...] = carry[...]


def _out_projection(oa, ob, sg, x2, proj_a, proj_b, w_out, ln_g, ln_b, w_router, router_bias, s_gate, s_up, s_down):
    t = x2.shape[0]
    tm = OUT_TM
    pair_rows = lambda p: p.reshape(N_GROUPS, GROUP, HEAD_DIM, -1).transpose(1, 0, 2, 3).reshape(p.shape)
    pa = pair_rows(proj_a).astype(MXU_DTYPE)
    pb = pair_rows(proj_b).astype(MXU_DTYPE)
    tri = jnp.asarray(np.triu(np.ones((tm, tm), np.float32), 1), BF16)
    row = lambda i: (i, 0)
    fixed = lambda i: (0, 0)
    col = lambda i: (0, i)
    outs = pl.pallas_call(
        _outproj_kernel,
        grid=(t // tm,),
        in_specs=[pl.BlockSpec((tm, 4 * LANES), row), pl.BlockSpec((tm, 4 * LANES), row),
                  pl.BlockSpec((tm, 2 * D_MODEL), row), pl.BlockSpec((tm, D_MODEL), row),
                  pl.BlockSpec((4 * LANES, D_MODEL), fixed), pl.BlockSpec((4 * LANES, D_MODEL), fixed),
                  pl.BlockSpec((D_MODEL, D_MODEL), fixed),
                  pl.BlockSpec((1, D_MODEL), fixed), pl.BlockSpec((1, D_MODEL), fixed),
                  pl.BlockSpec((N_EXPERTS, D_MODEL), fixed), pl.BlockSpec((N_EXPERTS, LANES), fixed),
                  pl.BlockSpec((D_MODEL, 2 * SHARED_HIDDEN), fixed), pl.BlockSpec((SHARED_HIDDEN, D_MODEL), fixed),
                  pl.BlockSpec((tm, tm), fixed)],
        out_specs=[pl.BlockSpec((tm,) + PACKED_ROW_TILE, lambda i: (i, 0, 0)), pl.BlockSpec((tm, D_MODEL), row),
                   pl.BlockSpec((TOP_K, tm), col), pl.BlockSpec((TOP_K, tm), col), pl.BlockSpec((TOP_K, tm), col),
                   pl.BlockSpec((N_EXPERTS, LANES), fixed)],
        out_shape=[jax.ShapeDtypeStruct((t,) + PACKED_ROW_TILE, jnp.uint32), jax.ShapeDtypeStruct((t, D_MODEL), F32),
                   jax.ShapeDtypeStruct((TOP_K, t), jnp.int32), jax.ShapeDtypeStruct((TOP_K, t), F32),
                   jax.ShapeDtypeStruct((TOP_K, t), jnp.int32), jax.ShapeDtypeStruct((N_EXPERTS, LANES), F32)],
        scratch_shapes=[pltpu.VMEM((N_EXPERTS, LANES), F32)],
        compiler_params=pltpu.CompilerParams(dimension_semantics=("arbitrary",), vmem_limit_bytes=VMEM_LIMIT),
        name="out_projection_router",
    )(oa, ob, sg, x2, pa, pb, w_out.astype(MXU_DTYPE), ln_g.reshape(1, -1), ln_b.reshape(1, -1),
      w_router.T.astype(MXU_DTYPE), jnp.broadcast_to(router_bias.astype(F32)[:, None], (N_EXPERTS, LANES)),
      jnp.concatenate([s_gate, s_up], axis=1).astype(MXU_DTYPE), s_down.astype(MXU_DTYPE), tri)
    return outs


def _rows_to_tiles(x):
    return pltpu.einshape("cml->mcl", jnp.stack(_lane_tiles(x), axis=0))


def _tiles_to_rows(x3):
    xt = pltpu.einshape("mcl->cml", x3)
    return jnp.concatenate([xt[c] for c in range(xt.shape[0])], axis=1)


def _dispatch_kernel(pend_ref, pad_ref, dest_ref, h3_ref, hh_ref, xs_ref, h_ref, zeros, sem, hsem, zsem):
    step = pl.program_id(0)
    tm = h3_ref.shape[0]
    slot = step % 2
    h_ref[slot] = h3_ref[...]

    @pl.when(step == 0)
    def _():
        zeros[...] = jnp.zeros(zeros.shape, zeros.dtype)

        def for_pieces(action):
            def body(e, c):
                for piece in range(MOE_BM // ZERO_ROWS):
                    @pl.when(pad_ref[e] > piece * ZERO_ROWS)
                    def _():
                        start = pend_ref[e] - (piece + 1) * ZERO_ROWS
                        action(pltpu.make_async_copy(zeros, xs_ref.at[pl.ds(start, ZERO_ROWS)], zsem))
                return c
            lax.fori_loop(0, N_EXPERTS, body, 0)
        for_pieces(lambda cp: cp.start())
        for_pieces(lambda cp: cp.wait())

    n_vmem = TOP_K - HBM_COPIES

    def issue(t, c):
        for k in range(n_vmem):
            pltpu.make_async_copy(h_ref.at[slot, t], xs_ref.at[dest_ref[k, t]], sem.at[slot]).start(priority=k % 2)
        for k in range(n_vmem, TOP_K):
            pltpu.make_async_copy(hh_ref.at[step * tm + t], xs_ref.at[dest_ref[k, t]], hsem.at[slot]).start()
        return c
    lax.fori_loop(0, tm, issue, 0)

    def wait_tile(s):
        for k in range(n_vmem):
            pltpu.make_async_copy(h_ref.at[s], xs_ref.at[pl.ds(0, tm)], sem.at[s]).wait()
        for k in range(n_vmem, TOP_K):
            pltpu.make_async_copy(hh_ref.at[pl.ds(0, tm)], xs_ref.at[pl.ds(0, tm)], hsem.at[s]).wait()

    @pl.when(step > 0)
    def _():
        wait_tile(1 - slot)

    @pl.when(step + 1 == pl.num_programs(0))
    def _():
        wait_tile(slot)


def _dispatch(h, dest, pends, pad_rows, n_rows):
    t = h.shape[0]
    tm = DISP_TM
    return pl.pallas_call(
        _dispatch_kernel,
        grid_spec=pltpu.PrefetchScalarGridSpec(
            num_scalar_prefetch=2,
            grid=(t // tm,),
            in_specs=[pl.BlockSpec((TOP_K, tm), lambda i, *_: (0, i), memory_space=pltpu.SMEM),
                      pl.BlockSpec((tm,) + PACKED_ROW_TILE, lambda i, *_: (i, 0, 0)),
                      pl.BlockSpec(memory_space=pl.ANY)],
            out_specs=pl.BlockSpec(memory_space=pl.ANY),
            scratch_shapes=[pltpu.VMEM((2, tm) + PACKED_ROW_TILE, jnp.uint32),
                            pltpu.VMEM((ZERO_ROWS,) + PACKED_ROW_TILE, jnp.uint32),
                            pltpu.SemaphoreType.DMA((2,)), pltpu.SemaphoreType.DMA((2,)),
                            pltpu.SemaphoreType.DMA(())]),
        out_shape=jax.ShapeDtypeStruct((n_rows,) + PACKED_ROW_TILE, jnp.uint32),
        compiler_params=pltpu.CompilerParams(dimension_semantics=("arbitrary",), vmem_limit_bytes=VMEM_LIMIT),
        name="moe_dispatch",
    )(pends, pad_rows, dest, h, h)


def _experts_kernel(blk_e_ref, nused_ref, xs_ref, wg_ref, wu_ref, wd_ref, ys_ref, wg_s, wu_s, wd_s):
    b = pl.program_id(0)
    prev = blk_e_ref[jnp.maximum(b - 1, 0)]

    @pl.when((b == 0) | (blk_e_ref[b] != prev))
    def _():
        wg_s[...] = _mx(wg_ref[0])
        wu_s[...] = _mx(wu_ref[0])
        wd_s[...] = _mx(wd_ref[0])

    @pl.when(b < nused_ref[0])
    def _():
        xb = _mx(jnp.concatenate(_unpack_bf16_pairs(_tiles_to_rows(xs_ref[...])), axis=1))
        hid = jax.nn.silu(_dot(xb, wg_s[...])) * _dot(xb, wu_s[...])
        ys_ref[...] = _rows_to_tiles(_pack_bf16_pairs(_dot(_mx(hid), wd_s[...])))

    @pl.when(b >= nused_ref[0])
    def _():
        ys_ref[...] = jnp.zeros(ys_ref.shape, ys_ref.dtype)


def _experts(xs, blk_e, nused, e_gate, e_up, e_down):
    n_rows = xs.shape[0]
    n_blocks = n_rows // MOE_BM
    xmap = lambda b, be, nu: (jnp.minimum(b, nu[0] - 1), 0, 0)
    wmap = lambda b, be, nu: (be[b], 0, 0)
    return pl.pallas_call(
        _experts_kernel,
        grid_spec=pltpu.PrefetchScalarGridSpec(
            num_scalar_prefetch=2,
            grid=(n_blocks,),
            in_specs=[pl.BlockSpec((MOE_BM,) + PACKED_ROW_TILE, xmap),
                      pl.BlockSpec((1, D_MODEL, EXPERT_HIDDEN), wmap),
                      pl.BlockSpec((1, D_MODEL, EXPERT_HIDDEN), wmap),
                      pl.BlockSpec((1, EXPERT_HIDDEN, D_MODEL), wmap)],
            out_specs=pl.BlockSpec((MOE_BM,) + PACKED_ROW_TILE,
                                   lambda b, be, nu: (jnp.where(b < nu[0], b, n_blocks - 1), 0, 0)),
            scratch_shapes=[pltpu.VMEM((D_MODEL, EXPERT_HIDDEN), MXU_DTYPE),
                            pltpu.VMEM((D_MODEL, EXPERT_HIDDEN), MXU_DTYPE),
                            pltpu.VMEM((EXPERT_HIDDEN, D_MODEL), MXU_DTYPE)]),
        out_shape=jax.ShapeDtypeStruct((n_rows,) + PACKED_ROW_TILE, jnp.uint32),
        compiler_params=pltpu.CompilerParams(dimension_semantics=("arbitrary",), vmem_limit_bytes=VMEM_LIMIT),
        name="moe_experts",
    )(blk_e, nused, xs, e_gate, e_up, e_down)


def _combine_kernel(dest_ref, dest_next_ref, gate_ref, base_ref, g2_ref, b2_ref, ys_ref, out_ref, buf, ysum, sem):
    step = pl.program_id(0)
    tm = base_ref.shape[0]
    slot = step % 2

    sub = ROW_TILE[0]

    def gather_rows(d_ref, s, t0):
        for u in range(sub):
            for k in range(TOP_K):
                pltpu.make_async_copy(ys_ref.at[d_ref[(t0 + u) * TOP_K + k]], buf.at[s, k * tm + t0 + u],
                                      sem.at[s]).start(priority=k % 2)

    def combine_rows(t0):
        y = base_ref[pl.ds(t0, sub), :]
        gates = gate_ref[pl.ds(t0, sub), :]
        for k in range(TOP_K):
            words = _tiles_to_rows(buf[slot, pl.ds(k * tm + t0, sub)])
            y = y + gates[:, k:k + 1] * jnp.concatenate(_unpack_bf16_pairs(words), axis=1)
        ysum[pl.ds(t0, sub), :] = y

    def for_token_groups(body):
        def trip(g, c):
            body(pl.multiple_of(g * sub, sub))
            return c
        lax.fori_loop(0, tm // sub, trip, 0)

    @pl.when(step == 0)
    def _():
        for_token_groups(lambda t0: gather_rows(dest_ref, 0, t0))

    pltpu.make_async_copy(ys_ref.at[pl.ds(0, tm * TOP_K)], buf.at[slot], sem.at[slot]).wait()

    @pl.when(step + 1 < pl.num_programs(0))
    def _():
        def both(t0):
            gather_rows(dest_next_ref, 1 - slot, t0)
            combine_rows(t0)
        for_token_groups(both)

    @pl.when(step + 1 == pl.num_programs(0))
    def _():
        for_token_groups(combine_rows)

    out_ref[...] = _layer_norm(ysum[...], g2_ref[...], b2_ref[...])


def _combine(ys3, dest, gate, base, ln_g, ln_b):
    t = base.shape[0]
    tm = COMB_TM
    n_tiles = t // tm
    dest_tk = dest.T.reshape(-1)
    return pl.pallas_call(
        _combine_kernel,
        grid=(n_tiles,),
        in_specs=[pl.BlockSpec((tm * TOP_K,), lambda i: (i,), memory_space=pltpu.SMEM),
                  pl.BlockSpec((tm * TOP_K,), lambda i: (jnp.minimum(i + 1, n_tiles - 1),), memory_space=pltpu.SMEM),
                  pl.BlockSpec((tm, TOP_K), lambda i: (i, 0)),
                  pl.BlockSpec((tm, D_MODEL), lambda i: (i, 0)),
                  pl.BlockSpec((1, D_MODEL), lambda i: (0, 0)),
                  pl.BlockSpec((1, D_MODEL), lambda i: (0, 0)),
                  pl.BlockSpec(memory_space=pl.ANY)],
        out_specs=pl.BlockSpec((tm, D_MODEL), lambda i: (i, 0)),
        out_shape=jax.ShapeDtypeStruct((t, D_MODEL), F32),
        scratch_shapes=[pltpu.VMEM((2, tm * TOP_K) + PACKED_ROW_TILE, jnp.uint32), pltpu.VMEM((tm, D_MODEL), F32),
                        pltpu.SemaphoreType.DMA((2,))],
        compiler_params=pltpu.CompilerParams(dimension_semantics=("arbitrary",), vmem_limit_bytes=VMEM_LIMIT),
        name="moe_combine",
    )(dest_tk, dest_tk, gate.T, base, ln_g.reshape(1, -1), ln_b.reshape(1, -1), ys3)


def _dest_kernel(pstart_ref, eidx_ref, rank_ref, dest_ref):
    eidx = eidx_ref[...]

    unroll = 8

    def body(g, dest):
        for u in range(unroll):
            e = g * unroll + u
            dest = dest + jnp.where(eidx == e, pstart_ref[e], 0)
        return dest
    dest_ref[...] = lax.fori_loop(0, N_EXPERTS // unroll, body, rank_ref[...])


def _dest_rows(pstarts, eidx, rank):
    t = eidx.shape[1]
    tl = 2048
    spec = pl.BlockSpec((TOP_K, tl), lambda i, *_: (0, i))
    return pl.pallas_call(
        _dest_kernel,
        grid_spec=pltpu.PrefetchScalarGridSpec(num_scalar_prefetch=1, grid=(t // tl,), in_specs=[spec, spec],
                                               out_specs=spec),
        out_shape=jax.ShapeDtypeStruct(eidx.shape, jnp.int32),
        compiler_params=pltpu.CompilerParams(dimension_semantics=("arbitrary",)),
        name="moe_dest_rows",
    )(pstarts, eidx, rank)


def _moe_layout(eidx, rank, counts):
    n_assign = eidx.size
    n_blocks = (n_assign + N_EXPERTS * (MOE_BM - 1)) // MOE_BM
    padded = (counts + MOE_BM - 1) // MOE_BM * MOE_BM
    pends = jnp.cumsum(padded)
    pstarts = (pends - padded).astype(jnp.int32)
    dest = _dest_rows(pstarts, eidx, rank)
    block_row = jnp.arange(n_blocks, dtype=jnp.int32) * MOE_BM
    blk_e = jnp.minimum(jnp.sum(pends[None, :] <= block_row[:, None], axis=1), N_EXPERTS - 1).astype(jnp.int32)
    nused = (pends[-1:] // MOE_BM).astype(jnp.int32)
    pad_rows = (padded - counts).astype(jnp.int32)
    return dest.astype(jnp.int32), blk_e, nused, pends.astype(jnp.int32), pad_rows, n_blocks * MOE_BM


def _layer(x, w_in, cmp_pe, cmp_w1, cmp_b1, cmp_w2, sinks, bias_table, proj_a, proj_b, w_out, ln1_g, ln1_b,
           w_router, router_bias, e_gate, e_up, e_down, s_gate, s_up, s_down, ln2_g, ln2_b):
    bsz, seq, d = x.shape
    x2 = x.reshape(bsz * seq, d)
    proj = _in_projection(x2, w_in)
    kvcmp = _compress(proj['kc'], proj['vc'], bsz, seq, cmp_pe, cmp_w1, cmp_b1, cmp_w2)
    oa, ob = _attention(proj, kvcmp, sinks, bias_table, bsz, seq)
    h, base, eidx, gate, rank, cnt = _out_projection(oa, ob, proj['sg'], x2, proj_a, proj_b, w_out, ln1_g, ln1_b,
                                                     w_router, router_bias, s_gate, s_up, s_down)
    counts = cnt[:, 0].astype(jnp.int32)
    dest, blk_e, nused, pends, pad_rows, n_rows = _moe_layout(eidx, rank, counts)
    xs = _dispatch(h, dest, pends, pad_rows, n_rows)
    ys = _experts(xs, blk_e, nused, e_gate, e_up, e_down)
    out = _combine(ys, dest, gate, base, ln2_g, ln2_b)
    return out.reshape(bsz, seq, d)


def kernel(x, w_in, cmp_pe, cmp_w1, cmp_b1, cmp_w2, attn_sinks, rel_bias_table, proj_a, proj_b, w_out, ln1_g, ln1_b,
           w_router, router_bias, expert_w_gate, expert_w_up, expert_w_down, shared_w_gate, shared_w_up,
           shared_w_down, ln2_g, ln2_b):
    h = x
    for l in range(DEPTH):
        h = _layer(h, w_in[l], cmp_pe[l], cmp_w1[l], cmp_b1[l], cmp_w2[l], attn_sinks[l], rel_bias_table, proj_a[l],
                   proj_b[l], w_out[l], ln1_g[l], ln1_b[l], w_router[l], router_bias[l], expert_w_gate[l],
                   expert_w_up[l], expert_w_down[l], shared_w_gate[l], shared_w_up[l], shared_w_down[l], ln2_g[l],
                   ln2_b[l])
    return h
```

```python
import functools
import math

import numpy as np
import jax
import jax.numpy as jnp
from jax import lax
from jax.experimental import pallas as pl
from jax.experimental.pallas import tpu as pltpu

F32 = jnp.float32
BF16 = jnp.bfloat16
MXU_DTYPE = jnp.bfloat16

D_MODEL = 1024
HEAD_DIM = 64
ATTN_SCALE = HEAD_DIM ** -0.5
LOG2E = math.log2(math.e)
Q_BLOCK = 128
N_HEADS = 8
N_GROUPS = 2
GROUP = 4
CMP_BLOCK = 32
CMP_STRIDE = 16
CMP_HIDDEN = 128
SEL_BLOCK = 64
SEL_TOP_N = 8
SEL_INIT_BLOCKS = 1
SEL_LOCAL_BLOCKS = 2
NSA_WINDOW = 512
SWA_WINDOW = 128
REL_BUCKETS = 32
REL_MAX_DIST = 128
N_EXPERTS = 256
TOP_K = 8
EXPERT_HIDDEN = 256
SHARED_HIDDEN = 256
N_EXPERT_GROUPS = 8
TOPK_EXPERT_GROUPS = 4
ROUTED_SCALE = 2.5
LN_EPS = 1e-5
DEPTH = 1
DN_ALPHA = (2 * DEPTH) ** 0.25

NEG = -1e30
LANES = 128
ROW_TILE = (8, LANES)
PACKED_ROW_TILE = (4, LANES)
CMP_FRONT = 16
CMP_NEAR = LANES
SEL_CHUNK = 1024
QB_PER_STEP = 1
VMEM_LIMIT = 56 * 1024 * 1024

IN_TM = 1024
OUT_TM = 512
MOE_BM = 512
ZERO_ROWS = 64
DISP_TM = 512
COMB_TM = 512


def _dot(a, b):
    return jnp.dot(a, b, preferred_element_type=F32)


def _dot_nt(a, b):
    return lax.dot_general(a, b, (((1,), (1,)), ((), ())), preferred_element_type=F32)


def _mx(a):
    return a.astype(MXU_DTYPE)


def _pack_bf16_pairs(x):
    half = x.shape[1] // 2
    bits = lax.bitcast_convert_type(x.astype(BF16).astype(F32), jnp.uint32)
    return (bits[:, half:] & jnp.uint32(0xFFFF0000)) | (bits[:, :half] >> 16)


def _unpack_bf16_pairs(words):
    return (lax.bitcast_convert_type(words << 16, F32),
            lax.bitcast_convert_type(words & jnp.uint32(0xFFFF0000), F32))


_IN_COLS = (('qa', 512), ('qb', 512), ('kc', 128), ('vc', 128), ('ks', 128), ('vs', 128), ('kw', 128),
            ('vw', 128), ('kb', 128), ('vb', 128), ('ga', 128), ('sg', 2048))


def _inproj_kernel(x_ref, w_ref, qa_ref, qb_ref, kc_ref, vc_ref, ks_ref, vs_ref, kw_ref, vw_ref, kb_ref, vb_ref,
                   ga_ref, sg_ref):
    xb = _mx(x_ref[...])
    outs = dict(qa=qa_ref, qb=qb_ref, kc=kc_ref, vc=vc_ref, ks=ks_ref, vs=vs_ref, kw=kw_ref, vw=vw_ref,
                kb=kb_ref, vb=vb_ref, ga=ga_ref, sg=sg_ref)
    tiles = [(name, c) for name, width in _IN_COLS for c in range(0, width, LANES)]
    chunk = 4
    for t0 in range(0, len(tiles), chunk):
        group = tiles[t0:t0 + chunk]
        y = _dot(xb, w_ref[:, t0 * LANES:(t0 + len(group)) * LANES])
        for j, (name, c) in enumerate(group):
            yj = y[:, j * LANES:(j + 1) * LANES]
            if name in ('ga', 'sg'):
                yj = jax.nn.sigmoid(yj)
            outs[name][:, c:c + LANES] = yj.astype(outs[name].dtype)


def _pair_head_columns(w):
    return w.reshape(w.shape[0], N_GROUPS, GROUP, HEAD_DIM).transpose(0, 2, 1, 3).reshape(w.shape[0], -1)


def _in_projection(x2, w_in):
    t = x2.shape[0]
    sizes = (512, 128, 128, 128, 128, 128, 128, 24, 512, 128, 128, 1024, 1024)
    offs = np.cumsum((0,) + sizes)
    part = [w_in[:, offs[k]:offs[k + 1]] for k in range(len(sizes))]
    w_qa, w_kc, w_vc, w_ks, w_vs, w_kw, w_vw, w_g, w_qb, w_kb, w_vb, w_gate_a, w_gate_b = part
    w_qa = _pair_head_columns(w_qa) * (ATTN_SCALE * LOG2E)
    w_qb = _pair_head_columns(w_qb) * (ATTN_SCALE * LOG2E)
    w_ga = w_g.reshape(-1, N_GROUPS, GROUP, 3).transpose(0, 3, 2, 1).reshape(-1, 24)
    w_ga = jnp.pad(w_ga, ((0, 0), (0, LANES - 24)))
    w_all = jnp.concatenate([w_qa, w_qb, w_kc, w_vc, w_ks, w_vs, w_kw, w_vw, w_kb, w_vb, w_ga, w_gate_a, w_gate_b],
                            axis=1).astype(MXU_DTYPE)
    n_all = w_all.shape[1]
    out_shape = []
    out_specs = []
    for name, width in _IN_COLS:
        dt = F32 if name == 'ga' else BF16
        out_shape.append(jax.ShapeDtypeStruct((t, width), dt))
        out_specs.append(pl.BlockSpec((IN_TM, width), lambda i: (i, 0)))
    outs = pl.pallas_call(
        _inproj_kernel,
        grid=(t // IN_TM,),
        in_specs=[pl.BlockSpec((IN_TM, D_MODEL), lambda i: (i, 0)),
                  pl.BlockSpec((D_MODEL, n_all), lambda i: (0, 0))],
        out_specs=out_specs,
        out_shape=out_shape,
        compiler_params=pltpu.CompilerParams(dimension_semantics=("arbitrary",), vmem_limit_bytes=VMEM_LIMIT),
        name="in_projection",
    )(x2, w_all)
    return dict(zip([n for n, _ in _IN_COLS], outs))


def _compress_kernel(tok_ref, w1_ref, pe_ref, w1o_ref, b1_ref, w2_ref, out_ref):
    n_chunks = tok_ref.shape[2]
    ab = _dot(tok_ref[0, 0], w1_ref[0])
    a = ab[:, :2 * CMP_HIDDEN]
    b_next = pltpu.roll(ab[:, 2 * CMP_HIDDEN:], n_chunks - 1, 0)
    cb = _dot(_mx(pe_ref[0]), _mx(w1o_ref[0]))[0:1, :] + b1_ref[0]
    cb2 = jnp.concatenate([cb, cb], axis=1)
    hid = jax.nn.gelu(a + b_next + cb2)
    out = _dot(_mx(hid), w2_ref[0])
    row = lax.broadcasted_iota(jnp.int32, out.shape, 0)
    out = jnp.where(row < n_chunks - 1, out, 0.0)
    out_ref[0, 0, 0:CMP_FRONT, :] = jnp.zeros((CMP_FRONT, LANES), F32)
    out_ref[0, 0, CMP_FRONT:CMP_FRONT + n_chunks, :] = out
    out_ref[0, 0, CMP_FRONT + n_chunks:, :] = jnp.zeros((CMP_NEAR - CMP_FRONT, LANES), F32)


def _compress(kc, vc, bsz, seq, cmp_pe, cmp_w1, cmp_b1, cmp_w2):
    n_chunks = seq // CMP_STRIDE
    tok = jnp.stack([kc, vc]).reshape(2, bsz, n_chunks, CMP_STRIDE * LANES)
    eye = jnp.eye(N_GROUPS, dtype=F32)
    w1r = cmp_w1.reshape(2, 2, CMP_STRIDE, HEAD_DIM, CMP_HIDDEN)
    w1 = jnp.einsum('khjdn,gG->kjgdhGn', w1r, eye).reshape(2, CMP_STRIDE * LANES, 4 * CMP_HIDDEN).astype(MXU_DTYPE)
    w2 = jnp.einsum('knd,gG->kgnGd', cmp_w2, eye).reshape(2, 2 * CMP_HIDDEN, LANES).astype(MXU_DTYPE)
    pe = jnp.pad(cmp_pe.reshape(2, 1, CMP_BLOCK * HEAD_DIM), ((0, 0), (0, 7), (0, 0)))
    b1 = cmp_b1.reshape(2, 1, CMP_HIDDEN)
    rows = CMP_FRONT + n_chunks + CMP_NEAR - CMP_FRONT
    return pl.pallas_call(
        _compress_kernel,
        grid=(2, bsz),
        in_specs=[pl.BlockSpec((1, 1, n_chunks, CMP_STRIDE * LANES), lambda k, b: (k, b, 0, 0)),
                  pl.BlockSpec((1, CMP_STRIDE * LANES, 4 * CMP_HIDDEN), lambda k, b: (k, 0, 0)),
                  pl.BlockSpec((1, 8, CMP_BLOCK * HEAD_DIM), lambda k, b: (k, 0, 0)),
                  pl.BlockSpec((1, CMP_BLOCK * HEAD_DIM, CMP_HIDDEN), lambda k, b: (k, 0, 0)),
                  pl.BlockSpec((1, 1, CMP_HIDDEN), lambda k, b: (k, 0, 0)),
                  pl.BlockSpec((1, 2 * CMP_HIDDEN, LANES), lambda k, b: (k, 0, 0))],
        out_specs=pl.BlockSpec((1, 1, rows, LANES), lambda k, b: (k, b, 0, 0)),
        out_shape=jax.ShapeDtypeStruct((2, bsz, rows, LANES), F32),
        compiler_params=pltpu.CompilerParams(dimension_semantics=("arbitrary", "arbitrary"),
                                             vmem_limit_bytes=VMEM_LIMIT),
        name="nsa_compress",
    )(tok, w1, pe, cmp_w1, b1, w2)


def _stack_heads(q_ref, dst):
    lo = lax.broadcasted_iota(jnp.int32, (Q_BLOCK, LANES), 1) < HEAD_DIM
    for r in range(GROUP):
        qr = q_ref[:, r * LANES:(r + 1) * LANES].astype(dst.dtype)
        z = jnp.zeros_like(qr)
        dst[(2 * r) * Q_BLOCK:(2 * r + 1) * Q_BLOCK, :] = jnp.where(lo, qr, z)
        dst[(2 * r + 1) * Q_BLOCK:(2 * r + 2) * Q_BLOCK, :] = jnp.where(lo, z, qr)


def _pair_heads(o, r):
    lo = lax.broadcasted_iota(jnp.int32, (Q_BLOCK, LANES), 1) < HEAD_DIM
    return jnp.where(lo, o[(2 * r) * Q_BLOCK:(2 * r + 1) * Q_BLOCK], o[(2 * r + 1) * Q_BLOCK:(2 * r + 2) * Q_BLOCK])


def _lane_tiles(x):
    return [x[:, t * LANES:(t + 1) * LANES] for t in range(x.shape[1] // LANES)]


def _row_max(tiles):
    mx = tiles[0]
    for t in tiles[1:]:
        mx = jnp.maximum(mx, t)
    return jnp.broadcast_to(jnp.max(mx, axis=1, keepdims=True), mx.shape)


def _with_ones(v):
    return jnp.concatenate([v, jnp.ones(v.shape, v.dtype)], axis=1)


def _block_of_key(n_keys, first_block):
    b = lax.broadcasted_iota(jnp.int32, (LANES, n_keys), 0)
    k = lax.broadcasted_iota(jnp.int32, (LANES, n_keys), 1)
    return (b == (k // SEL_BLOCK) + first_block).astype(MXU_DTYPE)


def _select_blocks_t(imp_t, i, n_top):
    blk = lax.broadcasted_iota(jnp.int32, imp_t.shape, 0)
    qcol = lax.broadcasted_iota(jnp.int32, imp_t.shape, 1)
    back = (2 * i + (qcol >= SEL_BLOCK).astype(jnp.int32)) - blk
    sel = (back >= 0) & ((blk < SEL_INIT_BLOCKS) | (back < SEL_LOCAL_BLOCKS))
    cand = jnp.where((back >= SEL_LOCAL_BLOCKS) & (blk >= SEL_INIT_BLOCKS), imp_t, -1.0)
    blk_f = blk.astype(F32)
    for _ in range(n_top - SEL_INIT_BLOCKS - SEL_LOCAL_BLOCKS):
        m = jnp.max(cand, axis=0, keepdims=True)
        idx = jnp.min(jnp.where(cand == m, blk_f, float(LANES)), axis=0, keepdims=True)
        hit = blk_f == idx
        sel = sel | (hit & (m >= 0.0))
        cand = jnp.where(hit, -2.0, cand)
    return sel


def _query_block(i, sink_ref, qa_ref, qb_ref, ga_ref, kcmp_ref, vcmp_ref, ks_ref, vs_ref, kw_ref, vw_ref, kb_ref,
                 vb_ref, cmat_ref, tnear_ref, tsel_ref, twin_ref, tswa_ref, oa_ref, ob_ref,
                 qall, qball, mneg, mneg_far, m_s, acc_s, s_buf, oa_acc, qmask, n_far, n_top):
    rows = N_HEADS * Q_BLOCK
    half = rows // 2
    halves = (slice(0, half), slice(half, rows))
    _stack_heads(qa_ref, qall)
    _stack_heads(qb_ref, qball)
    nstart = pl.multiple_of(i * Q_BLOCK, Q_BLOCK)
    lo = lax.broadcasted_iota(jnp.int32, (Q_BLOCK, LANES), 1) < HEAD_DIM
    gates = ga_ref[...]

    def gate_tile(c, r):
        return jnp.where(lo, gates[:, c * 8 + 2 * r:c * 8 + 2 * r + 1], gates[:, c * 8 + 2 * r + 1:c * 8 + 2 * r + 2])

    def softmax_pv(s_tiles, v1, fix_max=None):
        m = _row_max(s_tiles)
        if fix_max is not None:
            m = fix_max(m)
        e = [jnp.exp2(t - m) for t in s_tiles]
        return e, m, _dot(_mx(jnp.concatenate(e, axis=1)), v1)

    off = pl.multiple_of(i * (Q_BLOCK // CMP_STRIDE), 8)
    k_cmp = _mx(jnp.concatenate([kcmp_ref[0, 0, 0:n_far, :], kcmp_ref[0, 0, pl.ds(off, CMP_NEAR), :]], axis=0))
    v_cmp = _with_ones(_mx(jnp.concatenate([vcmp_ref[0, 0, 0:n_far, :], vcmp_ref[0, 0, pl.ds(off, CMP_NEAR), :]],
                                           axis=0)))
    colf = lax.broadcasted_iota(jnp.int32, (1, n_far), 1)
    coln = lax.broadcasted_iota(jnp.int32, (1, CMP_NEAR), 1)
    col_ok = jnp.concatenate([(colf >= CMP_FRONT) & (colf < off), coln + off >= CMP_FRONT], axis=1)
    mask_c = jnp.where(col_ok, 0.0, NEG)
    no_key = lambda m: jnp.where(m > 0.5 * NEG, m, 0.0)
    p_cmp, o_c = [], []
    for rs in halves:
        tiles = _lane_tiles(_dot_nt(qall[rs, :], k_cmp) + mask_c)
        tiles[-1] = tiles[-1] + tnear_ref[rs, :]
        e, _, ov = softmax_pv(tiles, v_cmp, no_key)
        inv = 1.0 / jnp.maximum(ov[:, LANES:], 1e-30)
        o_c.append(ov[:, :LANES] * inv)
        p_cmp.append([t * inv for t in e])
    o_c = jnp.concatenate(o_c, axis=0)
    yield None

    def far_start(j):
        return pl.multiple_of(Q_BLOCK + j * SEL_CHUNK, Q_BLOCK)

    def far_logits(j, slot, masked):
        kc = _mx(ks_ref[0, pl.ds(far_start(j), SEL_CHUNK), :])
        if masked:
            key = lax.broadcasted_iota(jnp.int32, (SEL_CHUNK, LANES), 0)
            blk = lax.broadcasted_iota(jnp.int32, (SEL_CHUNK, LANES), 1)
            one_hot = (blk == key // SEL_BLOCK + j * (SEL_CHUNK // SEL_BLOCK)).astype(MXU_DTYPE)
            kc = jnp.concatenate([kc, one_hot], axis=1)
        for rs in halves:
            s_buf[slot, rs, :] = _dot_nt(qmask[rs, :] if masked else qall[rs, :], kc)

    far_logits(0, 0, False)

    blkcol = lax.broadcasted_iota(jnp.int32, (Q_BLOCK, LANES), 1)
    n_tiles = len(p_cmp[0])
    for g in range(N_GROUPS):
        imp = jnp.zeros((Q_BLOCK, LANES), F32)
        for t in range(n_tiles):
            pg = sum(p_cmp[r // 2][t][(2 * (r % 2) + g) * Q_BLOCK:(2 * (r % 2) + g + 1) * Q_BLOCK]
                     for r in range(GROUP))
            if t < n_tiles - 1:
                cm = _mx(cmat_ref[t * LANES:(t + 1) * LANES, :])
            else:
                cm = _mx(cmat_ref[pl.ds(off, CMP_NEAR), :])
            hi = _mx(pg)
            low = _mx(pg - hi.astype(F32))
            imp = imp + _dot(hi, cm) + _dot(low, cm)
        sel = _select_blocks_t(imp.T, i, n_top)
        neg = jnp.where(sel, 0.0, NEG).T
        mneg[g * Q_BLOCK:(g + 1) * Q_BLOCK, :] = neg.astype(mneg.dtype)
        neg_far = jnp.where(blkcol < 2 * (i - 1), neg, NEG).astype(mneg.dtype)
        mneg_far[g * Q_BLOCK:(g + 1) * Q_BLOCK, :] = neg_far
        for r in range(GROUP):
            qmask[(2 * r + g) * Q_BLOCK:(2 * r + g + 1) * Q_BLOCK, LANES:] = neg_far
    qmask[:, :LANES] = qall[...]

    yield None

    wpad = kw_ref.shape[1] - ks_ref.shape[1] + Q_BLOCK
    kwin = _mx(kw_ref[0, pl.ds(nstart, wpad + Q_BLOCK), :])
    vwin = _with_ones(_mx(vw_ref[0, pl.ds(nstart, wpad + Q_BLOCK), :]))
    colw = lax.broadcasted_iota(jnp.int32, (1, wpad + Q_BLOCK), 1)
    mask_w = jnp.where(colw + nstart >= wpad, 0.0, NEG)
    o_w = []
    for rs in halves:
        _, _, ov = softmax_pv(_lane_tiles(_dot_nt(qall[rs, :], kwin) + twin_ref[rs, :] + mask_w), vwin)
        o_w.append(ov[:, :LANES] / ov[:, LANES:])
    o_w = jnp.concatenate(o_w, axis=0)
    for r in range(GROUP):
        oa_acc[:, r * LANES:(r + 1) * LANES] = (gate_tile(0, r) * _pair_heads(o_c, r)
                                                + gate_tile(2, r) * _pair_heads(o_w, r))

    yield None

    bpad = kb_ref.shape[1] - ks_ref.shape[1] + Q_BLOCK
    kwin = _mx(kb_ref[0, pl.ds(nstart, bpad + Q_BLOCK), :])
    vwin = _with_ones(_mx(vb_ref[0, pl.ds(nstart, bpad + Q_BLOCK), :]))
    colb = lax.broadcasted_iota(jnp.int32, (1, bpad + Q_BLOCK), 1)
    mask_b = jnp.where(colb + nstart >= bpad, 0.0, NEG)
    o_b = []
    for hh, rs in enumerate(halves):
        sink = jnp.concatenate([jnp.full((Q_BLOCK, LANES), sink_ref[(h % 2) * GROUP + h // 2], F32)
                                for h in range(hh * N_HEADS // 2, (hh + 1) * N_HEADS // 2)], axis=0)
        _, m, ov = softmax_pv(_lane_tiles(_dot_nt(qball[rs, :], kwin) + tswa_ref[rs, :] + mask_b), vwin,
                              lambda m: jnp.maximum(m, sink))
        o_b.append(ov[:, :LANES] / (ov[:, LANES:] + jnp.exp2(sink - m)))
    o_b = jnp.concatenate(o_b, axis=0)
    for r in range(GROUP):
        ob_ref[:, r * LANES:(r + 1) * LANES] = _pair_heads(o_b, r).astype(ob_ref.dtype)

    yield None

    m_s[...] = jnp.full(m_s.shape, NEG, F32)
    acc_s[...] = jnp.zeros(acc_s.shape, F32)

    def flash_update(rs, s, v1):
        s_tiles = _lane_tiles(s)
        m_old = m_s[rs, :]
        m_new = jnp.maximum(m_old, _row_max(s_tiles))
        alpha = jnp.exp2(m_old - m_new)
        p = jnp.concatenate([jnp.exp2(t - m_new) for t in s_tiles], axis=1)
        acc_s[rs, :] = jnp.concatenate([alpha, alpha], axis=1) * acc_s[rs, :] + _dot(_mx(p), v1)
        m_s[rs, :] = m_new

    madd = _dot(mneg_far[...], _block_of_key(SEL_CHUNK, 0))
    for rs in halves:
        s_buf[0, rs, :] = s_buf[0, rs, :] + jnp.concatenate([madd] * (GROUP // 2), axis=0)

    def far_update(j):
        v1 = _with_ones(_mx(vs_ref[0, pl.ds(far_start(j), SEL_CHUNK), :]))
        for rs in halves:
            flash_update(rs, s_buf[j % 2, rs, :], v1)

    yield far_update, far_logits

    kc = _mx(ks_ref[0, pl.ds(nstart, 2 * Q_BLOCK), :])
    v1 = _with_ones(_mx(vs_ref[0, pl.ds(nstart, 2 * Q_BLOCK), :]))
    madd = _dot(mneg[...], _block_of_key(2 * Q_BLOCK, 2 * (i - 1)))
    col2 = lax.broadcasted_iota(jnp.int32, (1, 2 * Q_BLOCK), 1)
    mask_n = jnp.where((col2 < Q_BLOCK) & (i == 0), NEG, 0.0)
    for rs in halves:
        s = _dot_nt(qall[rs, :], kc) + jnp.concatenate([madd] * (GROUP // 2), axis=0) + tsel_ref[rs, :] + mask_n
        flash_update(rs, s, v1)
    acc = acc_s[...]
    o_s = acc[:, :LANES] / acc[:, LANES:]
    for r in range(GROUP):
        tile = oa_acc[:, r * LANES:(r + 1) * LANES] + gate_tile(1, r) * _pair_heads(o_s, r)
        oa_ref[:, r * LANES:(r + 1) * LANES] = tile.astype(oa_ref.dtype)
    yield None


def _attn_kernel(sink_ref, qa_ref, qb_ref, ga_ref, kcmp_ref, vcmp_ref, ks_ref, vs_ref, kw_ref, vw_ref, kb_ref,
                 vb_ref, cmat_ref, tnear_ref, tsel_ref, twin_ref, tswa_ref, oa_ref, ob_ref,
                 qall, qball, mneg, mneg_far, m_s, acc_s, s_buf, oa_acc, qmask, *, n_far, n_top):
    first = pl.program_id(1) * QB_PER_STEP
    blocks, steps = [], []
    for n in range(QB_PER_STEP):
        qrows = pl.ds(n * Q_BLOCK, Q_BLOCK)
        blk = _query_block(first + n, sink_ref, qa_ref.at[qrows, :], qb_ref.at[qrows, :], ga_ref.at[qrows, :],
                           kcmp_ref, vcmp_ref, ks_ref, vs_ref, kw_ref, vw_ref, kb_ref, vb_ref, cmat_ref, tnear_ref,
                           tsel_ref, twin_ref, tswa_ref, oa_ref.at[qrows, :], ob_ref.at[qrows, :],
                           qall.at[n], qball.at[n], mneg.at[n], mneg_far.at[n], m_s.at[n], acc_s.at[n], s_buf.at[n],
                           oa_acc.at[n], qmask.at[n], n_far, n_top)
        blocks.append(blk)
    steps = [next(blk) for blk in blocks]
    while steps[0] is None:
        steps = [next(blk) for blk in blocks]
    n_far_keys = jnp.maximum(first + QB_PER_STEP - 2, 0) * Q_BLOCK
    n_chunks = (n_far_keys + SEL_CHUNK - 1) // SEL_CHUNK

    def far_body(j, carry):
        for far_update, _ in steps:
            far_update(j)
        for _, far_logits in steps:
            far_logits(j + 1, (j + 1) % 2, True)
        return carry

    last = jnp.maximum(n_chunks - 1, 0)
    lax.fori_loop(0, last, far_body, 0)
    for far_update, _ in steps:
        far_update(last)
    for blk in blocks:
        next(blk)


def _rel_bucket_np(dist):
    n = np.maximum(dist, 0)
    max_exact = REL_BUCKETS // 2
    nf = np.maximum(n, 1).astype(np.float32)
    log_b = max_exact + (np.log(nf / max_exact) / math.log(REL_MAX_DIST / max_exact)
                         * (REL_BUCKETS - max_exact)).astype(np.int32)
    log_b = np.minimum(log_b, REL_BUCKETS - 1)
    return np.where(n < max_exact, n, log_b)


def _toeplitz_bias(tab, pad, width, window, shift_far):
    length = width + Q_BLOCK
    dist = pad + Q_BLOCK - 1 - np.arange(length)
    onehot = np.zeros((length, REL_BUCKETS), np.float32)
    onehot[np.arange(length), _rel_bucket_np(dist)] = 1.0
    vals = jnp.dot(jnp.asarray(onehot), tab, precision=lax.Precision.HIGHEST)
    if shift_far:
        vals = vals - tab[REL_BUCKETS - 1][None, :]
    vals = vals * LOG2E
    valid = (dist >= 0) & (dist < window)
    vals = jnp.where(jnp.asarray(valid)[:, None], vals, NEG).T
    skew = jnp.tile(vals, (1, Q_BLOCK))[:, :Q_BLOCK * (length - 1)].reshape(N_HEADS, Q_BLOCK, length - 1)
    return skew[:, :, Q_BLOCK - 1:Q_BLOCK - 1 + width].reshape(N_HEADS * Q_BLOCK, width).astype(F32)


def _attention(proj, kvcmp, sinks, bias_table, bsz, seq):
    assert seq % SEL_CHUNK == 0
    nq = seq // Q_BLOCK
    n_far = seq // CMP_STRIDE
    n_sel = seq // SEL_BLOCK
    n_top = min(SEL_TOP_N, n_sel)
    assert n_top >= SEL_INIT_BLOCKS + SEL_LOCAL_BLOCKS and n_sel <= LANES
    wpad = Q_BLOCK * (-(-(NSA_WINDOW - 1) // Q_BLOCK))
    bpad = Q_BLOCK * (-(-(SWA_WINDOW - 1) // Q_BLOCK))
    pair = lambda tab: tab.astype(F32).reshape(REL_BUCKETS, N_GROUPS, GROUP).transpose(0, 2, 1).reshape(REL_BUCKETS, -1)
    tab_a = pair(bias_table[:, :N_HEADS])
    tab_b = pair(bias_table[:, N_HEADS:])
    near_pad = CMP_STRIDE * CMP_FRONT - (CMP_BLOCK - 1)
    t_near = _toeplitz_bias(tab_a, near_pad, CMP_STRIDE * CMP_NEAR, 1 << 30, True)[:, ::CMP_STRIDE]
    t_sel = _toeplitz_bias(tab_a, Q_BLOCK, 2 * Q_BLOCK, 1 << 30, True)
    t_win = _toeplitz_bias(tab_a, wpad, wpad + Q_BLOCK, NSA_WINDOW, False)
    t_swa = _toeplitz_bias(tab_b, bpad, bpad + Q_BLOCK, SWA_WINDOW, False)
    n_rows = kvcmp.shape[2]
    cn = (np.arange(n_rows) - CMP_FRONT)[:, None] * CMP_STRIDE
    sj = np.arange(LANES)[None, :] * SEL_BLOCK
    cmat = ((cn < sj + SEL_BLOCK) & (cn + CMP_BLOCK > sj) & (cn >= 0) & (cn + CMP_BLOCK <= seq)
            & (sj < seq)).astype(np.float32)
    cmat = jnp.asarray(cmat, F32)
    padded = lambda name, p: jnp.pad(proj[name].reshape(bsz, seq, LANES), ((0, 0), (p, 0), (0, 0)))
    ks, vs = padded('ks', Q_BLOCK), padded('vs', Q_BLOCK)
    kw, vw = padded('kw', wpad), padded('vw', wpad)
    kb, vb = padded('kb', bpad), padded('vb', bpad)
    rows = N_HEADS * Q_BLOCK
    n_steps = nq // QB_PER_STEP
    qspec = pl.BlockSpec((QB_PER_STEP * Q_BLOCK, 4 * LANES), lambda b, i: (b * n_steps + i, 0))
    const2 = lambda shape: pl.BlockSpec(shape, lambda b, i: (0, 0))
    batch3 = lambda n: pl.BlockSpec((1, n, LANES), lambda b, i: (b, 0, 0))
    per_block = lambda shape, dtype: pltpu.VMEM((QB_PER_STEP,) + shape, dtype)
    kernel = functools.partial(_attn_kernel, n_far=n_far, n_top=n_top)
    return pl.pallas_call(
        kernel,
        grid=(bsz, n_steps),
        in_specs=[pl.BlockSpec(memory_space=pltpu.SMEM),
                  qspec, qspec,
                  pl.BlockSpec((QB_PER_STEP * Q_BLOCK, LANES), lambda b, i: (b * n_steps + i, 0)),
                  pl.BlockSpec((1, 1, n_rows, LANES), lambda b, i: (0, b, 0, 0)),
                  pl.BlockSpec((1, 1, n_rows, LANES), lambda b, i: (1, b, 0, 0)),
                  batch3(seq + Q_BLOCK), batch3(seq + Q_BLOCK),
                  batch3(seq + wpad), batch3(seq + wpad),
                  batch3(seq + bpad), batch3(seq + bpad),
                  const2((n_rows, LANES)),
                  const2((rows, CMP_NEAR)),
                  const2((rows, 2 * Q_BLOCK)),
                  const2((rows, wpad + Q_BLOCK)),
                  const2((rows, bpad + Q_BLOCK))],
        out_specs=[qspec, qspec],
        out_shape=[jax.ShapeDtypeStruct((bsz * seq, 4 * LANES), BF16)] * 2,
        scratch_shapes=[per_block((rows, LANES), MXU_DTYPE),
                        per_block((rows, LANES), MXU_DTYPE),
                        per_block((N_GROUPS * Q_BLOCK, LANES), MXU_DTYPE),
                        per_block((N_GROUPS * Q_BLOCK, LANES), MXU_DTYPE),
                        per_block((rows, LANES), F32),
                        per_block((rows, 2 * LANES), F32),
                        per_block((2, rows, SEL_CHUNK), F32),
                        per_block((Q_BLOCK, 4 * LANES), F32),
                        per_block((rows, 2 * LANES), MXU_DTYPE)],
        compiler_params=pltpu.CompilerParams(dimension_semantics=("arbitrary", "arbitrary"),
                                             vmem_limit_bytes=VMEM_LIMIT),
        name="attention",
    )(sinks.astype(F32) * LOG2E, proj['qa'], proj['qb'], proj['ga'], kvcmp, kvcmp, ks, vs, kw, vw, kb, vb,
      cmat, t_near, t_sel, t_win, t_swa)


def _layer_norm(y, g, b):
    mu = jnp.mean(y, axis=-1, keepdims=True)
    yc = y - mu
    var = jnp.mean(yc * yc, axis=-1, keepdims=True)
    return yc * lax.rsqrt(var + LN_EPS) * g + b


def _outproj_kernel(oa_ref, ob_ref, sg_ref, x_ref, pa_ref, pb_ref, wo_ref, g1_ref, b1_ref, wr_ref, rb_ref, sgu_ref,
                    sd_ref, tri_ref, h_ref, base_ref, eidx_ref, gate_ref, rank_ref, cnt_ref, carry):
    step = pl.program_id(0)
    tm = oa_ref.shape[0]

    @pl.when(step == 0)
    def _():
        carry[...] = jnp.zeros(carry.shape, F32)

    sg = sg_ref[...].astype(F32)
    merged = (sg[:, :D_MODEL] * _dot(_mx(oa_ref[...]), pa_ref[...])
              + sg[:, D_MODEL:] * _dot(_mx(ob_ref[...]), pb_ref[...]))
    mix = _dot(_mx(merged), wo_ref[...])
    h = _layer_norm(DN_ALPHA * x_ref[...] + mix, g1_ref[...], b1_ref[...])
    hb = _mx(h)
    h_ref[...] = _pack_bf16_pairs(h)

    gu = _dot(hb, sgu_ref[...])
    shared = _dot(_mx(jax.nn.silu(gu[:, :SHARED_HIDDEN]) * gu[:, SHARED_HIDDEN:]), sd_ref[...])
    base_ref[...] = DN_ALPHA * h + shared

    scores = jax.nn.sigmoid(_dot_nt(wr_ref[...], hb))
    choice = scores + rb_ref[:, 0:1]
    per_group = N_EXPERTS // N_EXPERT_GROUPS
    gs = []
    for g in range(N_EXPERT_GROUPS):
        cg = choice[g * per_group:(g + 1) * per_group]
        m1 = jnp.max(cg, axis=0, keepdims=True)
        is_m = cg == m1
        n_m = jnp.sum(is_m.astype(F32), axis=0, keepdims=True)
        m2 = jnp.max(jnp.where(is_m, -jnp.inf, cg), axis=0, keepdims=True)
        gs.append(m1 + jnp.where(n_m > 1.5, m1, m2))
    gs = jnp.concatenate(gs, axis=0)
    gid = lax.broadcasted_iota(jnp.int32, gs.shape, 0)
    beaten = jnp.zeros(gs.shape, jnp.int32)
    for g in range(N_EXPERT_GROUPS):
        other = gs[g:g + 1]
        beaten = beaten + ((other > gs) | ((other == gs) & (g < gid))).astype(jnp.int32)
    keep_g = beaten < TOPK_EXPERT_GROUPS
    keep = jnp.concatenate([jnp.broadcast_to(keep_g[g:g + 1], (per_group, tm)) for g in range(N_EXPERT_GROUPS)],
                           axis=0)
    cand = jnp.where(keep, choice, -jnp.inf)
    eid = lax.broadcasted_iota(jnp.int32, cand.shape, 0)
    hits = []
    e_rows = []
    w_rows = []
    for _ in range(TOP_K):
        m = jnp.max(cand, axis=0, keepdims=True)
        idx = jnp.min(jnp.where(cand == m, eid, N_EXPERTS), axis=0, keepdims=True)
        hit = eid == idx
        hits.append(hit)
        e_rows.append(idx)
        w_rows.append(jnp.sum(jnp.where(hit, scores, 0.0), axis=0, keepdims=True))
        cand = jnp.where(hit, -jnp.inf, cand)
    w = jnp.concatenate(w_rows, axis=0)
    gate_ref[...] = w / jnp.sum(w, axis=0, keepdims=True) * ROUTED_SCALE
    eidx_ref[...] = jnp.concatenate(e_rows, axis=0)

    onehot = jnp.zeros(cand.shape, F32)
    for hit in hits:
        onehot = onehot + hit.astype(F32)
    before = _dot(onehot.astype(BF16), tri_ref[...]) + carry[:, 0:1]
    rank_ref[...] = jnp.concatenate(
        [jnp.sum(jnp.where(hit, before, 0.0), axis=0, keepdims=True) for hit in hits], axis=0).astype(jnp.int32)
    carry[...] = carry[...] + jnp.sum(onehot, axis=1, keepdims=True)
    cnt_ref[...] = carry[...]


def _out_projection(oa, ob, sg, x2, proj_a, proj_b, w_out, ln_g, ln_b, w_router, router_bias, s_gate, s_up, s_down):
    t = x2.shape[0]
    tm = OUT_TM
    pair_rows = lambda p: p.reshape(N_GROUPS, GROUP, HEAD_DIM, -1).transpose(1, 0, 2, 3).reshape(p.shape)
    pa = pair_rows(proj_a).astype(MXU_DTYPE)
    pb = pair_rows(proj_b).astype(MXU_DTYPE)
    tri = jnp.asarray(np.triu(np.ones((tm, tm), np.float32), 1), BF16)
    row = lambda i: (i, 0)
    fixed = lambda i: (0, 0)
    col = lambda i: (0, i)
    outs = pl.pallas_call(
        _outproj_kernel,
        grid=(t // tm,),
        in_specs=[pl.BlockSpec((tm, 4 * LANES), row), pl.BlockSpec((tm, 4 * LANES), row),
                  pl.BlockSpec((tm, 2 * D_MODEL), row), pl.BlockSpec((tm, D_MODEL), row),
                  pl.BlockSpec((4 * LANES, D_MODEL), fixed), pl.BlockSpec((4 * LANES, D_MODEL), fixed),
                  pl.BlockSpec((D_MODEL, D_MODEL), fixed),
                  pl.BlockSpec((1, D_MODEL), fixed), pl.BlockSpec((1, D_MODEL), fixed),
                  pl.BlockSpec((N_EXPERTS, D_MODEL), fixed), pl.BlockSpec((N_EXPERTS, LANES), fixed),
                  pl.BlockSpec((D_MODEL, 2 * SHARED_HIDDEN), fixed), pl.BlockSpec((SHARED_HIDDEN, D_MODEL), fixed),
                  pl.BlockSpec((tm, tm), fixed)],
        out_specs=[pl.BlockSpec((tm, D_MODEL // 2), row), pl.BlockSpec((tm, D_MODEL), row),
                   pl.BlockSpec((TOP_K, tm), col), pl.BlockSpec((TOP_K, tm), col), pl.BlockSpec((TOP_K, tm), col),
                   pl.BlockSpec((N_EXPERTS, LANES), fixed)],
        out_shape=[jax.ShapeDtypeStruct((t, D_MODEL // 2), jnp.uint32), jax.ShapeDtypeStruct((t, D_MODEL), F32),
                   jax.ShapeDtypeStruct((TOP_K, t), jnp.int32), jax.ShapeDtypeStruct((TOP_K, t), F32),
                   jax.ShapeDtypeStruct((TOP_K, t), jnp.int32), jax.ShapeDtypeStruct((N_EXPERTS, LANES), F32)],
        scratch_shapes=[pltpu.VMEM((N_EXPERTS, LANES), F32)],
        compiler_params=pltpu.CompilerParams(dimension_semantics=("arbitrary",), vmem_limit_bytes=VMEM_LIMIT),
        name="out_projection_router",
    )(oa, ob, sg, x2, pa, pb, w_out.astype(MXU_DTYPE), ln_g.reshape(1, -1), ln_b.reshape(1, -1),
      w_router.T.astype(MXU_DTYPE), jnp.broadcast_to(router_bias.astype(F32)[:, None], (N_EXPERTS, LANES)),
      jnp.concatenate([s_gate, s_up], axis=1).astype(MXU_DTYPE), s_down.astype(MXU_DTYPE), tri)
    return outs


def _rows_to_tiles(x):
    return pltpu.einshape("cml->mcl", jnp.stack(_lane_tiles(x), axis=0))


def _tiles_to_rows(x3):
    xt = pltpu.einshape("mcl->cml", x3)
    return jnp.concatenate([xt[c] for c in range(xt.shape[0])], axis=1)


def _dispatch_kernel(pend_ref, pad_ref, dest_ref, h2_ref, xs_ref, h_ref, zeros, sem, zsem):
    step = pl.program_id(0)
    tm = h2_ref.shape[0]
    slot = step % 2
    h_ref[slot] = _rows_to_tiles(h2_ref[...])

    @pl.when(step == 0)
    def _():
        zeros[...] = jnp.zeros(zeros.shape, zeros.dtype)

        def for_pieces(action):
            def body(e, c):
                for piece in range(MOE_BM // ZERO_ROWS):
                    @pl.when(pad_ref[e] > piece * ZERO_ROWS)
                    def _():
                        start = pend_ref[e] - (piece + 1) * ZERO_ROWS
                        action(pltpu.make_async_copy(zeros, xs_ref.at[pl.ds(start, ZERO_ROWS)], zsem))
                return c
            lax.fori_loop(0, N_EXPERTS, body, 0)
        for_pieces(lambda cp: cp.start())
        for_pieces(lambda cp: cp.wait())

    def issue(t, c):
        for k in range(TOP_K):
            pltpu.make_async_copy(h_ref.at[slot, t], xs_ref.at[dest_ref[k, t]], sem.at[slot]).start(priority=k % 2)
        return c
    lax.fori_loop(0, tm, issue, 0)

    def wait_tile(s):
        for k in range(TOP_K):
            pltpu.make_async_copy(h_ref.at[s], xs_ref.at[pl.ds(0, tm)], sem.at[s]).wait()

    @pl.when(step > 0)
    def _():
        wait_tile(1 - slot)

    @pl.when(step + 1 == pl.num_programs(0))
    def _():
        wait_tile(slot)


def _dispatch(h, dest, pends, pad_rows, n_rows):
    t = h.shape[0]
    tm = DISP_TM
    return pl.pallas_call(
        _dispatch_kernel,
        grid_spec=pltpu.PrefetchScalarGridSpec(
            num_scalar_prefetch=2,
            grid=(t // tm,),
            in_specs=[pl.BlockSpec((TOP_K, tm), lambda i, *_: (0, i), memory_space=pltpu.SMEM),
                      pl.BlockSpec((tm, D_MODEL // 2), lambda i, *_: (i, 0))],
            out_specs=pl.BlockSpec(memory_space=pl.ANY),
            scratch_shapes=[pltpu.VMEM((2, tm) + PACKED_ROW_TILE, jnp.uint32),
                            pltpu.VMEM((ZERO_ROWS,) + PACKED_ROW_TILE, jnp.uint32),
                            pltpu.SemaphoreType.DMA((2,)), pltpu.SemaphoreType.DMA(())]),
        out_shape=jax.ShapeDtypeStruct((n_rows,) + PACKED_ROW_TILE, jnp.uint32),
        compiler_params=pltpu.CompilerParams(dimension_semantics=("arbitrary",), vmem_limit_bytes=VMEM_LIMIT),
        name="moe_dispatch",
    )(pends, pad_rows, dest, h)


def _experts_kernel(blk_e_ref, nused_ref, first_ref, slot_ref, has_next_ref, next_e_ref, xs_ref, wg_hbm, wu_hbm,
                    wd_hbm, ys_ref, wg_s, wu_s, wd_s, wg_f, wu_f, wd_f, sem):
    b = pl.program_id(0)

    def weight_copies(e, s):
        return (pltpu.make_async_copy(wg_hbm.at[e], wg_f.at[s], sem.at[s]),
                pltpu.make_async_copy(wu_hbm.at[e], wu_f.at[s], sem.at[s]),
                pltpu.make_async_copy(wd_hbm.at[e], wd_f.at[s], sem.at[s]))

    @pl.when(first_ref[b] == 1)
    def _():
        s = slot_ref[b]

        @pl.when(b == 0)
        def _():
            for cp in weight_copies(blk_e_ref[0], s):
                cp.start()
        for cp in weight_copies(blk_e_ref[b], s):
            cp.wait()
        wg_s[...] = _mx(wg_f[s])
        wu_s[...] = _mx(wu_f[s])
        wd_s[...] = _mx(wd_f[s])

        @pl.when(has_next_ref[b] == 1)
        def _():
            for cp in weight_copies(next_e_ref[b], 1 - s):
                cp.start()

    @pl.when(b < nused_ref[0])
    def _():
        xb = _mx(jnp.concatenate(_unpack_bf16_pairs(_tiles_to_rows(xs_ref[...])), axis=1))
        hid = jax.nn.silu(_dot(xb, wg_s[...])) * _dot(xb, wu_s[...])
        ys_ref[...] = _rows_to_tiles(_pack_bf16_pairs(_dot(_mx(hid), wd_s[...])))

    @pl.when(b >= nused_ref[0])
    def _():
        ys_ref[...] = jnp.zeros(ys_ref.shape, ys_ref.dtype)


def _experts(xs, blk_e, nused, e_gate, e_up, e_down):
    n_rows = xs.shape[0]
    n_blocks = n_rows // MOE_BM
    idx = jnp.arange(n_blocks, dtype=jnp.int32)
    first = (idx < nused[0]) & ((idx == 0) | (blk_e != jnp.roll(blk_e, 1)))
    slot = ((jnp.cumsum(first.astype(jnp.int32)) - 1) % 2).astype(jnp.int32)
    later_first = lax.cummin(jnp.where(first, idx, n_blocks), reverse=True)
    next_first = jnp.concatenate([later_first[1:], jnp.full((1,), n_blocks, jnp.int32)])
    has_next = (next_first < n_blocks).astype(jnp.int32)
    next_e = blk_e[jnp.minimum(next_first, n_blocks - 1)]
    xmap = lambda b, *refs: (jnp.minimum(b, refs[1][0] - 1), 0, 0)
    omap = lambda b, *refs: (jnp.where(b < refs[1][0], b, n_blocks - 1), 0, 0)
    hbm = pl.BlockSpec(memory_space=pl.ANY)
    return pl.pallas_call(
        _experts_kernel,
        grid_spec=pltpu.PrefetchScalarGridSpec(
            num_scalar_prefetch=6,
            grid=(n_blocks,),
            in_specs=[pl.BlockSpec((MOE_BM,) + PACKED_ROW_TILE, xmap), hbm, hbm, hbm],
            out_specs=pl.BlockSpec((MOE_BM,) + PACKED_ROW_TILE, omap),
            scratch_shapes=[pltpu.VMEM((D_MODEL, EXPERT_HIDDEN), MXU_DTYPE),
                            pltpu.VMEM((D_MODEL, EXPERT_HIDDEN), MXU_DTYPE),
                            pltpu.VMEM((EXPERT_HIDDEN, D_MODEL), MXU_DTYPE),
                            pltpu.VMEM((2, D_MODEL, EXPERT_HIDDEN), F32),
                            pltpu.VMEM((2, D_MODEL, EXPERT_HIDDEN), F32),
                            pltpu.VMEM((2, EXPERT_HIDDEN, D_MODEL), F32),
                            pltpu.SemaphoreType.DMA((2,))]),
        out_shape=jax.ShapeDtypeStruct((n_rows,) + PACKED_ROW_TILE, jnp.uint32),
        compiler_params=pltpu.CompilerParams(dimension_semantics=("arbitrary",), vmem_limit_bytes=VMEM_LIMIT),
        name="moe_experts",
    )(blk_e, nused, first.astype(jnp.int32), slot, has_next, next_e.astype(jnp.int32), xs, e_gate, e_up, e_down)


def _combine_kernel(dest_ref, dest_next_ref, gate_ref, base_ref, g2_ref, b2_ref, ys_ref, out_ref, buf, ysum, sem):
    step = pl.program_id(0)
    tm = base_ref.shape[0]
    slot = step % 2

    sub = ROW_TILE[0]

    def gather_rows(d_ref, s, t0):
        for u in range(sub):
            for k in range(TOP_K):
                pltpu.make_async_copy(ys_ref.at[d_ref[(t0 + u) * TOP_K + k]], buf.at[s, k * tm + t0 + u],
                                      sem.at[s]).start(priority=k % 2)

    def combine_rows(t0):
        y = base_ref[pl.ds(t0, sub), :]
        gates = gate_ref[pl.ds(t0, sub), :]
        for k in range(TOP_K):
            words = _tiles_to_rows(buf[slot, pl.ds(k * tm + t0, sub)])
            y = y + gates[:, k:k + 1] * jnp.concatenate(_unpack_bf16_pairs(words), axis=1)
        ysum[pl.ds(t0, sub), :] = y

    def for_token_groups(body):
        def trip(g, c):
            body(pl.multiple_of(g * sub, sub))
            return c
        lax.fori_loop(0, tm // sub, trip, 0)

    @pl.when(step == 0)
    def _():
        for_token_groups(lambda t0: gather_rows(dest_ref, 0, t0))

    pltpu.make_async_copy(ys_ref.at[pl.ds(0, tm * TOP_K)], buf.at[slot], sem.at[slot]).wait()

    @pl.when(step + 1 < pl.num_programs(0))
    def _():
        def both(t0):
            gather_rows(dest_next_ref, 1 - slot, t0)
            combine_rows(t0)
        for_token_groups(both)

    @pl.when(step + 1 == pl.num_programs(0))
    def _():
        for_token_groups(combine_rows)

    out_ref[...] = _layer_norm(ysum[...], g2_ref[...], b2_ref[...])


def _combine(ys3, dest, gate, base, ln_g, ln_b):
    t = base.shape[0]
    tm = COMB_TM
    n_tiles = t // tm
    dest_tk = dest.T.reshape(-1)
    return pl.pallas_call(
        _combine_kernel,
        grid=(n_tiles,),
        in_specs=[pl.BlockSpec((tm * TOP_K,), lambda i: (i,), memory_space=pltpu.SMEM),
                  pl.BlockSpec((tm * TOP_K,), lambda i: (jnp.minimum(i + 1, n_tiles - 1),), memory_space=pltpu.SMEM),
                  pl.BlockSpec((tm, TOP_K), lambda i: (i, 0)),
                  pl.BlockSpec((tm, D_MODEL), lambda i: (i, 0)),
                  pl.BlockSpec((1, D_MODEL), lambda i: (0, 0)),
                  pl.BlockSpec((1, D_MODEL), lambda i: (0, 0)),
                  pl.BlockSpec(memory_space=pl.ANY)],
        out_specs=pl.BlockSpec((tm, D_MODEL), lambda i: (i, 0)),
        out_shape=jax.ShapeDtypeStruct((t, D_MODEL), F32),
        scratch_shapes=[pltpu.VMEM((2, tm * TOP_K) + PACKED_ROW_TILE, jnp.uint32), pltpu.VMEM((tm, D_MODEL), F32),
                        pltpu.SemaphoreType.DMA((2,))],
        compiler_params=pltpu.CompilerParams(dimension_semantics=("arbitrary",), vmem_limit_bytes=VMEM_LIMIT),
        name="moe_combine",
    )(dest_tk, dest_tk, gate.T, base, ln_g.reshape(1, -1), ln_b.reshape(1, -1), ys3)


def _dest_kernel(pstart_ref, eidx_ref, rank_ref, dest_ref):
    eidx = eidx_ref[...]

    unroll = 8

    def body(g, dest):
        for u in range(unroll):
            e = g * unroll + u
            dest = dest + jnp.where(eidx == e, pstart_ref[e], 0)
        return dest
    dest_ref[...] = lax.fori_loop(0, N_EXPERTS // unroll, body, rank_ref[...])


def _dest_rows(pstarts, eidx, rank):
    t = eidx.shape[1]
    tl = 2048
    spec = pl.BlockSpec((TOP_K, tl), lambda i, *_: (0, i))
    return pl.pallas_call(
        _dest_kernel,
        grid_spec=pltpu.PrefetchScalarGridSpec(num_scalar_prefetch=1, grid=(t // tl,), in_specs=[spec, spec],
                                               out_specs=spec),
        out_shape=jax.ShapeDtypeStruct(eidx.shape, jnp.int32),
        compiler_params=pltpu.CompilerParams(dimension_semantics=("arbitrary",)),
        name="moe_dest_rows",
    )(pstarts, eidx, rank)


def _moe_layout(eidx, rank, counts):
    n_assign = eidx.size
    n_blocks = (n_assign + N_EXPERTS * (MOE_BM - 1)) // MOE_BM
    padded = (counts + MOE_BM - 1) // MOE_BM * MOE_BM
    pends = jnp.cumsum(padded)
    pstarts = (pends - padded).astype(jnp.int32)
    dest = _dest_rows(pstarts, eidx, rank)
    block_row = jnp.arange(n_blocks, dtype=jnp.int32) * MOE_BM
    blk_e = jnp.minimum(jnp.sum(pends[None, :] <= block_row[:, None], axis=1), N_EXPERTS - 1).astype(jnp.int32)
    nused = (pends[-1:] // MOE_BM).astype(jnp.int32)
    pad_rows = (padded - counts).astype(jnp.int32)
    return dest.astype(jnp.int32), blk_e, nused, pends.astype(jnp.int32), pad_rows, n_blocks * MOE_BM


def _layer(x, w_in, cmp_pe, cmp_w1, cmp_b1, cmp_w2, sinks, bias_table, proj_a, proj_b, w_out, ln1_g, ln1_b,
           w_router, router_bias, e_gate, e_up, e_down, s_gate, s_up, s_down, ln2_g, ln2_b):
    bsz, seq, d = x.shape
    x2 = x.reshape(bsz * seq, d)
    proj = _in_projection(x2, w_in)
    kvcmp = _compress(proj['kc'], proj['vc'], bsz, seq, cmp_pe, cmp_w1, cmp_b1, cmp_w2)
    oa, ob = _attention(proj, kvcmp, sinks, bias_table, bsz, seq)
    h, base, eidx, gate, rank, cnt = _out_projection(oa, ob, proj['sg'], x2, proj_a, proj_b, w_out, ln1_g, ln1_b,
                                                     w_router, router_bias, s_gate, s_up, s_down)
    counts = cnt[:, 0].astype(jnp.int32)
    dest, blk_e, nused, pends, pad_rows, n_rows = _moe_layout(eidx, rank, counts)
    xs = _dispatch(h, dest, pends, pad_rows, n_rows)
    ys = _experts(xs, blk_e, nused, e_gate, e_up, e_down)
    out = _combine(ys, dest, gate, base, ln2_g, ln2_b)
    return out.reshape(bsz, seq, d)


def kernel(x, w_in, cmp_pe, cmp_w1, cmp_b1, cmp_w2, attn_sinks, rel_bias_table, proj_a, proj_b, w_out, ln1_g, ln1_b,
           w_router, router_bias, expert_w_gate, expert_w_up, expert_w_down, shared_w_gate, shared_w_up,
           shared_w_down, ln2_g, ln2_b):
    h = x
    for l in range(DEPTH):
        h = _layer(h, w_in[l], cmp_pe[l], cmp_w1[l], cmp_b1[l], cmp_w2[l], attn_sinks[l], rel_bias_table, proj_a[l],
                   proj_b[l], w_out[l], ln1_g[l], ln1_b[l], w_router[l], router_bias[l], expert_w_gate[l],
                   expert_w_up[l], expert_w_down[l], shared_w_gate[l], shared_w_up[l], shared_w_down[l], ln2_g[l],
                   ln2_b[l])
    return h
```

```python
import functools
import math

import numpy as np
import jax
import jax.numpy as jnp
from jax import lax
from jax.experimental import pallas as pl
from jax.experimental.pallas import tpu as pltpu

F32 = jnp.float32
BF16 = jnp.bfloat16
MXU_DTYPE = jnp.bfloat16

D_MODEL = 1024
HEAD_DIM = 64
ATTN_SCALE = HEAD_DIM ** -0.5
LOG2E = math.log2(math.e)
Q_BLOCK = 128
N_HEADS = 8
N_GROUPS = 2
GROUP = 4
CMP_BLOCK = 32
CMP_STRIDE = 16
CMP_HIDDEN = 128
SEL_BLOCK = 64
SEL_TOP_N = 8
SEL_INIT_BLOCKS = 1
SEL_LOCAL_BLOCKS = 2
NSA_WINDOW = 512
SWA_WINDOW = 128
REL_BUCKETS = 32
REL_MAX_DIST = 128
N_EXPERTS = 256
TOP_K = 8
EXPERT_HIDDEN = 256
SHARED_HIDDEN = 256
N_EXPERT_GROUPS = 8
TOPK_EXPERT_GROUPS = 4
ROUTED_SCALE = 2.5
LN_EPS = 1e-5
DEPTH = 1
DN_ALPHA = (2 * DEPTH) ** 0.25

NEG = -1e30
LANES = 128
ROW_TILE = (8, LANES)
PACKED_ROW_TILE = (4, LANES)
CMP_FRONT = 16
CMP_NEAR = LANES
SEL_CHUNK = 1024
QB_PER_STEP = 1
VMEM_LIMIT = 56 * 1024 * 1024

IN_TM = 1024
OUT_TM = 512
MOE_BM = 512
ZERO_ROWS = 64
X_SLOTS = 3
DISP_TM = 512
COMB_TM = 512


def _dot(a, b):
    return jnp.dot(a, b, preferred_element_type=F32)


def _dot_nt(a, b):
    return lax.dot_general(a, b, (((1,), (1,)), ((), ())), preferred_element_type=F32)


def _mx(a):
    return a.astype(MXU_DTYPE)


def _pack_bf16_pairs(x):
    half = x.shape[1] // 2
    bits = lax.bitcast_convert_type(x.astype(BF16).astype(F32), jnp.uint32)
    return (bits[:, half:] & jnp.uint32(0xFFFF0000)) | (bits[:, :half] >> 16)


def _unpack_bf16_pairs(words):
    return (lax.bitcast_convert_type(words << 16, F32),
            lax.bitcast_convert_type(words & jnp.uint32(0xFFFF0000), F32))


_IN_COLS = (('qa', 512), ('qb', 512), ('kc', 128), ('vc', 128), ('ks', 128), ('vs', 128), ('kw', 128),
            ('vw', 128), ('kb', 128), ('vb', 128), ('ga', 128), ('sg', 2048))


def _inproj_kernel(x_ref, w_ref, qa_ref, qb_ref, kc_ref, vc_ref, ks_ref, vs_ref, kw_ref, vw_ref, kb_ref, vb_ref,
                   ga_ref, sg_ref):
    xb = _mx(x_ref[...])
    outs = dict(qa=qa_ref, qb=qb_ref, kc=kc_ref, vc=vc_ref, ks=ks_ref, vs=vs_ref, kw=kw_ref, vw=vw_ref,
                kb=kb_ref, vb=vb_ref, ga=ga_ref, sg=sg_ref)
    tiles = [(name, c) for name, width in _IN_COLS for c in range(0, width, LANES)]
    chunk = 4
    for t0 in range(0, len(tiles), chunk):
        group = tiles[t0:t0 + chunk]
        y = _dot(xb, w_ref[:, t0 * LANES:(t0 + len(group)) * LANES])
        for j, (name, c) in enumerate(group):
            yj = y[:, j * LANES:(j + 1) * LANES]
            if name in ('ga', 'sg'):
                yj = jax.nn.sigmoid(yj)
            outs[name][:, c:c + LANES] = yj.astype(outs[name].dtype)


def _pair_head_columns(w):
    return w.reshape(w.shape[0], N_GROUPS, GROUP, HEAD_DIM).transpose(0, 2, 1, 3).reshape(w.shape[0], -1)


def _in_projection(x2, w_in):
    t = x2.shape[0]
    sizes = (512, 128, 128, 128, 128, 128, 128, 24, 512, 128, 128, 1024, 1024)
    offs = np.cumsum((0,) + sizes)
    part = [w_in[:, offs[k]:offs[k + 1]] for k in range(len(sizes))]
    w_qa, w_kc, w_vc, w_ks, w_vs, w_kw, w_vw, w_g, w_qb, w_kb, w_vb, w_gate_a, w_gate_b = part
    w_qa = _pair_head_columns(w_qa) * (ATTN_SCALE * LOG2E)
    w_qb = _pair_head_columns(w_qb) * (ATTN_SCALE * LOG2E)
    w_ga = w_g.reshape(-1, N_GROUPS, GROUP, 3).transpose(0, 3, 2, 1).reshape(-1, 24)
    w_ga = jnp.pad(w_ga, ((0, 0), (0, LANES - 24)))
    w_all = jnp.concatenate([w_qa, w_qb, w_kc, w_vc, w_ks, w_vs, w_kw, w_vw, w_kb, w_vb, w_ga, w_gate_a, w_gate_b],
                            axis=1).astype(MXU_DTYPE)
    n_all = w_all.shape[1]
    out_shape = []
    out_specs = []
    for name, width in _IN_COLS:
        dt = F32 if name == 'ga' else BF16
        out_shape.append(jax.ShapeDtypeStruct((t, width), dt))
        out_specs.append(pl.BlockSpec((IN_TM, width), lambda i: (i, 0)))
    outs = pl.pallas_call(
        _inproj_kernel,
        grid=(t // IN_TM,),
        in_specs=[pl.BlockSpec((IN_TM, D_MODEL), lambda i: (i, 0)),
                  pl.BlockSpec((D_MODEL, n_all), lambda i: (0, 0))],
        out_specs=out_specs,
        out_shape=out_shape,
        compiler_params=pltpu.CompilerParams(dimension_semantics=("arbitrary",), vmem_limit_bytes=VMEM_LIMIT),
        name="in_projection",
    )(x2, w_all)
    return dict(zip([n for n, _ in _IN_COLS], outs))


def _compress_kernel(tok_ref, w1_ref, pe_ref, w1o_ref, b1_ref, w2_ref, out_ref):
    n_chunks = tok_ref.shape[2]
    ab = _dot(tok_ref[0, 0], w1_ref[0])
    a = ab[:, :2 * CMP_HIDDEN]
    b_next = pltpu.roll(ab[:, 2 * CMP_HIDDEN:], n_chunks - 1, 0)
    cb = _dot(_mx(pe_ref[0]), _mx(w1o_ref[0]))[0:1, :] + b1_ref[0]
    cb2 = jnp.concatenate([cb, cb], axis=1)
    hid = jax.nn.gelu(a + b_next + cb2)
    out = _dot(_mx(hid), w2_ref[0])
    row = lax.broadcasted_iota(jnp.int32, out.shape, 0)
    out = jnp.where(row < n_chunks - 1, out, 0.0)
    out_ref[0, 0, 0:CMP_FRONT, :] = jnp.zeros((CMP_FRONT, LANES), F32)
    out_ref[0, 0, CMP_FRONT:CMP_FRONT + n_chunks, :] = out
    out_ref[0, 0, CMP_FRONT + n_chunks:, :] = jnp.zeros((CMP_NEAR - CMP_FRONT, LANES), F32)


def _compress(kc, vc, bsz, seq, cmp_pe, cmp_w1, cmp_b1, cmp_w2):
    n_chunks = seq // CMP_STRIDE
    tok = jnp.stack([kc, vc]).reshape(2, bsz, n_chunks, CMP_STRIDE * LANES)
    eye = jnp.eye(N_GROUPS, dtype=F32)
    w1r = cmp_w1.reshape(2, 2, CMP_STRIDE, HEAD_DIM, CMP_HIDDEN)
    w1 = jnp.einsum('khjdn,gG->kjgdhGn', w1r, eye).reshape(2, CMP_STRIDE * LANES, 4 * CMP_HIDDEN).astype(MXU_DTYPE)
    w2 = jnp.einsum('knd,gG->kgnGd', cmp_w2, eye).reshape(2, 2 * CMP_HIDDEN, LANES).astype(MXU_DTYPE)
    pe = jnp.pad(cmp_pe.reshape(2, 1, CMP_BLOCK * HEAD_DIM), ((0, 0), (0, 7), (0, 0)))
    b1 = cmp_b1.reshape(2, 1, CMP_HIDDEN)
    rows = CMP_FRONT + n_chunks + CMP_NEAR - CMP_FRONT
    return pl.pallas_call(
        _compress_kernel,
        grid=(2, bsz),
        in_specs=[pl.BlockSpec((1, 1, n_chunks, CMP_STRIDE * LANES), lambda k, b: (k, b, 0, 0)),
                  pl.BlockSpec((1, CMP_STRIDE * LANES, 4 * CMP_HIDDEN), lambda k, b: (k, 0, 0)),
                  pl.BlockSpec((1, 8, CMP_BLOCK * HEAD_DIM), lambda k, b: (k, 0, 0)),
                  pl.BlockSpec((1, CMP_BLOCK * HEAD_DIM, CMP_HIDDEN), lambda k, b: (k, 0, 0)),
                  pl.BlockSpec((1, 1, CMP_HIDDEN), lambda k, b: (k, 0, 0)),
                  pl.BlockSpec((1, 2 * CMP_HIDDEN, LANES), lambda k, b: (k, 0, 0))],
        out_specs=pl.BlockSpec((1, 1, rows, LANES), lambda k, b: (k, b, 0, 0)),
        out_shape=jax.ShapeDtypeStruct((2, bsz, rows, LANES), F32),
        compiler_params=pltpu.CompilerParams(dimension_semantics=("arbitrary", "arbitrary"),
                                             vmem_limit_bytes=VMEM_LIMIT),
        name="nsa_compress",
    )(tok, w1, pe, cmp_w1, b1, w2)


def _stack_heads(q_ref, dst):
    lo = lax.broadcasted_iota(jnp.int32, (Q_BLOCK, LANES), 1) < HEAD_DIM
    for r in range(GROUP):
        qr = q_ref[:, r * LANES:(r + 1) * LANES].astype(dst.dtype)
        z = jnp.zeros_like(qr)
        dst[(2 * r) * Q_BLOCK:(2 * r + 1) * Q_BLOCK, :] = jnp.where(lo, qr, z)
        dst[(2 * r + 1) * Q_BLOCK:(2 * r + 2) * Q_BLOCK, :] = jnp.where(lo, z, qr)


def _pair_heads(o, r):
    lo = lax.broadcasted_iota(jnp.int32, (Q_BLOCK, LANES), 1) < HEAD_DIM
    return jnp.where(lo, o[(2 * r) * Q_BLOCK:(2 * r + 1) * Q_BLOCK], o[(2 * r + 1) * Q_BLOCK:(2 * r + 2) * Q_BLOCK])


def _lane_tiles(x):
    return [x[:, t * LANES:(t + 1) * LANES] for t in range(x.shape[1] // LANES)]


def _row_max(tiles):
    mx = tiles[0]
    for t in tiles[1:]:
        mx = jnp.maximum(mx, t)
    return jnp.broadcast_to(jnp.max(mx, axis=1, keepdims=True), mx.shape)


def _with_ones(v):
    return jnp.concatenate([v, jnp.ones(v.shape, v.dtype)], axis=1)


def _block_of_key(n_keys, first_block):
    b = lax.broadcasted_iota(jnp.int32, (LANES, n_keys), 0)
    k = lax.broadcasted_iota(jnp.int32, (LANES, n_keys), 1)
    return (b == (k // SEL_BLOCK) + first_block).astype(MXU_DTYPE)


def _select_blocks_t(imp_t, i, n_top):
    blk = lax.broadcasted_iota(jnp.int32, imp_t.shape, 0)
    qcol = lax.broadcasted_iota(jnp.int32, imp_t.shape, 1)
    back = (2 * i + (qcol >= SEL_BLOCK).astype(jnp.int32)) - blk
    sel = (back >= 0) & ((blk < SEL_INIT_BLOCKS) | (back < SEL_LOCAL_BLOCKS))
    cand = jnp.where((back >= SEL_LOCAL_BLOCKS) & (blk >= SEL_INIT_BLOCKS), imp_t, -1.0)
    blk_f = blk.astype(F32)
    for _ in range(n_top - SEL_INIT_BLOCKS - SEL_LOCAL_BLOCKS):
        m = jnp.max(cand, axis=0, keepdims=True)
        idx = jnp.min(jnp.where(cand == m, blk_f, float(LANES)), axis=0, keepdims=True)
        hit = blk_f == idx
        sel = sel | (hit & (m >= 0.0))
        cand = jnp.where(hit, -2.0, cand)
    return sel


def _query_block(i, sink_ref, qa_ref, qb_ref, ga_ref, kcmp_ref, vcmp_ref, ks_ref, vs_ref, kw_ref, vw_ref, kb_ref,
                 vb_ref, cmat_ref, tnear_ref, tsel_ref, twin_ref, tswa_ref, oa_ref, ob_ref,
                 qall, qball, mneg, mneg_far, m_s, acc_s, s_buf, oa_acc, qmask, n_far, n_top):
    rows = N_HEADS * Q_BLOCK
    half = rows // 2
    halves = (slice(0, half), slice(half, rows))
    _stack_heads(qa_ref, qall)
    _stack_heads(qb_ref, qball)
    nstart = pl.multiple_of(i * Q_BLOCK, Q_BLOCK)
    lo = lax.broadcasted_iota(jnp.int32, (Q_BLOCK, LANES), 1) < HEAD_DIM
    gates = ga_ref[...]

    def gate_tile(c, r):
        return jnp.where(lo, gates[:, c * 8 + 2 * r:c * 8 + 2 * r + 1], gates[:, c * 8 + 2 * r + 1:c * 8 + 2 * r + 2])

    def softmax_pv(s_tiles, v1, fix_max=None):
        m = _row_max(s_tiles)
        if fix_max is not None:
            m = fix_max(m)
        e = [jnp.exp2(t - m) for t in s_tiles]
        return e, m, _dot(_mx(jnp.concatenate(e, axis=1)), v1)

    off = pl.multiple_of(i * (Q_BLOCK // CMP_STRIDE), 8)
    k_cmp = _mx(jnp.concatenate([kcmp_ref[0, 0, 0:n_far, :], kcmp_ref[0, 0, pl.ds(off, CMP_NEAR), :]], axis=0))
    v_cmp = _with_ones(_mx(jnp.concatenate([vcmp_ref[0, 0, 0:n_far, :], vcmp_ref[0, 0, pl.ds(off, CMP_NEAR), :]],
                                           axis=0)))
    colf = lax.broadcasted_iota(jnp.int32, (1, n_far), 1)
    coln = lax.broadcasted_iota(jnp.int32, (1, CMP_NEAR), 1)
    col_ok = jnp.concatenate([(colf >= CMP_FRONT) & (colf < off), coln + off >= CMP_FRONT], axis=1)
    mask_c = jnp.where(col_ok, 0.0, NEG)
    no_key = lambda m: jnp.where(m > 0.5 * NEG, m, 0.0)
    p_cmp, o_c = [], []
    for rs in halves:
        tiles = _lane_tiles(_dot_nt(qall[rs, :], k_cmp) + mask_c)
        tiles[-1] = tiles[-1] + tnear_ref[rs, :]
        e, _, ov = softmax_pv(tiles, v_cmp, no_key)
        inv = 1.0 / jnp.maximum(ov[:, LANES:], 1e-30)
        o_c.append(ov[:, :LANES] * inv)
        p_cmp.append([t * inv for t in e])
    o_c = jnp.concatenate(o_c, axis=0)
    yield None

    def far_start(j):
        return pl.multiple_of(Q_BLOCK + j * SEL_CHUNK, Q_BLOCK)

    def far_logits(j, slot, masked):
        kc = _mx(ks_ref[0, pl.ds(far_start(j), SEL_CHUNK), :])
        if masked:
            key = lax.broadcasted_iota(jnp.int32, (SEL_CHUNK, LANES), 0)
            blk = lax.broadcasted_iota(jnp.int32, (SEL_CHUNK, LANES), 1)
            one_hot = (blk == key // SEL_BLOCK + j * (SEL_CHUNK // SEL_BLOCK)).astype(MXU_DTYPE)
            kc = jnp.concatenate([kc, one_hot], axis=1)
        for rs in halves:
            s_buf[slot, rs, :] = _dot_nt(qmask[rs, :] if masked else qall[rs, :], kc)

    far_logits(0, 0, False)

    blkcol = lax.broadcasted_iota(jnp.int32, (Q_BLOCK, LANES), 1)
    n_tiles = len(p_cmp[0])
    for g in range(N_GROUPS):
        imp = jnp.zeros((Q_BLOCK, LANES), F32)
        for t in range(n_tiles):
            pg = sum(p_cmp[r // 2][t][(2 * (r % 2) + g) * Q_BLOCK:(2 * (r % 2) + g + 1) * Q_BLOCK]
                     for r in range(GROUP))
            if t < n_tiles - 1:
                cm = _mx(cmat_ref[t * LANES:(t + 1) * LANES, :])
            else:
                cm = _mx(cmat_ref[pl.ds(off, CMP_NEAR), :])
            hi = _mx(pg)
            low = _mx(pg - hi.astype(F32))
            imp = imp + _dot(hi, cm) + _dot(low, cm)
        sel = _select_blocks_t(imp.T, i, n_top)
        neg = jnp.where(sel, 0.0, NEG).T
        mneg[g * Q_BLOCK:(g + 1) * Q_BLOCK, :] = neg.astype(mneg.dtype)
        neg_far = jnp.where(blkcol < 2 * (i - 1), neg, NEG).astype(mneg.dtype)
        mneg_far[g * Q_BLOCK:(g + 1) * Q_BLOCK, :] = neg_far
        for r in range(GROUP):
            qmask[(2 * r + g) * Q_BLOCK:(2 * r + g + 1) * Q_BLOCK, LANES:] = neg_far
    qmask[:, :LANES] = qall[...]

    yield None

    wpad = kw_ref.shape[1] - ks_ref.shape[1] + Q_BLOCK
    kwin = _mx(kw_ref[0, pl.ds(nstart, wpad + Q_BLOCK), :])
    vwin = _with_ones(_mx(vw_ref[0, pl.ds(nstart, wpad + Q_BLOCK), :]))
    colw = lax.broadcasted_iota(jnp.int32, (1, wpad + Q_BLOCK), 1)
    mask_w = jnp.where(colw + nstart >= wpad, 0.0, NEG)
    o_w = []
    for rs in halves:
        _, _, ov = softmax_pv(_lane_tiles(_dot_nt(qall[rs, :], kwin) + twin_ref[rs, :] + mask_w), vwin)
        o_w.append(ov[:, :LANES] / ov[:, LANES:])
    o_w = jnp.concatenate(o_w, axis=0)
    for r in range(GROUP):
        oa_acc[:, r * LANES:(r + 1) * LANES] = (gate_tile(0, r) * _pair_heads(o_c, r)
                                                + gate_tile(2, r) * _pair_heads(o_w, r))

    yield None

    bpad = kb_ref.shape[1] - ks_ref.shape[1] + Q_BLOCK
    kwin = _mx(kb_ref[0, pl.ds(nstart, bpad + Q_BLOCK), :])
    vwin = _with_ones(_mx(vb_ref[0, pl.ds(nstart, bpad + Q_BLOCK), :]))
    colb = lax.broadcasted_iota(jnp.int32, (1, bpad + Q_BLOCK), 1)
    mask_b = jnp.where(colb + nstart >= bpad, 0.0, NEG)
    o_b = []
    for hh, rs in enumerate(halves):
        sink = jnp.concatenate([jnp.full((Q_BLOCK, LANES), sink_ref[(h % 2) * GROUP + h // 2], F32)
                                for h in range(hh * N_HEADS // 2, (hh + 1) * N_HEADS // 2)], axis=0)
        _, m, ov = softmax_pv(_lane_tiles(_dot_nt(qball[rs, :], kwin) + tswa_ref[rs, :] + mask_b), vwin,
                              lambda m: jnp.maximum(m, sink))
        o_b.append(ov[:, :LANES] / (ov[:, LANES:] + jnp.exp2(sink - m)))
    o_b = jnp.concatenate(o_b, axis=0)
    for r in range(GROUP):
        ob_ref[:, r * LANES:(r + 1) * LANES] = _pair_heads(o_b, r).astype(ob_ref.dtype)

    yield None

    m_s[...] = jnp.full(m_s.shape, NEG, F32)
    acc_s[...] = jnp.zeros(acc_s.shape, F32)

    def flash_update(rs, s, v1):
        s_tiles = _lane_tiles(s)
        m_old = m_s[rs, :]
        m_new = jnp.maximum(m_old, _row_max(s_tiles))
        alpha = jnp.exp2(m_old - m_new)
        p = jnp.concatenate([jnp.exp2(t - m_new) for t in s_tiles], axis=1)
        acc_s[rs, :] = jnp.concatenate([alpha, alpha], axis=1) * acc_s[rs, :] + _dot(_mx(p), v1)
        m_s[rs, :] = m_new

    madd = _dot(mneg_far[...], _block_of_key(SEL_CHUNK, 0))
    for rs in halves:
        s_buf[0, rs, :] = s_buf[0, rs, :] + jnp.concatenate([madd] * (GROUP // 2), axis=0)

    def far_update(j):
        v1 = _with_ones(_mx(vs_ref[0, pl.ds(far_start(j), SEL_CHUNK), :]))
        for rs in halves:
            flash_update(rs, s_buf[j % 2, rs, :], v1)

    yield far_update, far_logits

    kc = _mx(ks_ref[0, pl.ds(nstart, 2 * Q_BLOCK), :])
    v1 = _with_ones(_mx(vs_ref[0, pl.ds(nstart, 2 * Q_BLOCK), :]))
    madd = _dot(mneg[...], _block_of_key(2 * Q_BLOCK, 2 * (i - 1)))
    col2 = lax.broadcasted_iota(jnp.int32, (1, 2 * Q_BLOCK), 1)
    mask_n = jnp.where((col2 < Q_BLOCK) & (i == 0), NEG, 0.0)
    for rs in halves:
        s = _dot_nt(qall[rs, :], kc) + jnp.concatenate([madd] * (GROUP // 2), axis=0) + tsel_ref[rs, :] + mask_n
        flash_update(rs, s, v1)
    acc = acc_s[...]
    o_s = acc[:, :LANES] / acc[:, LANES:]
    for r in range(GROUP):
        tile = oa_acc[:, r * LANES:(r + 1) * LANES] + gate_tile(1, r) * _pair_heads(o_s, r)
        oa_ref[:, r * LANES:(r + 1) * LANES] = tile.astype(oa_ref.dtype)
    yield None


def _attn_kernel(sink_ref, qa_ref, qb_ref, ga_ref, kcmp_ref, vcmp_ref, ks_ref, vs_ref, kw_ref, vw_ref, kb_ref,
                 vb_ref, cmat_ref, tnear_ref, tsel_ref, twin_ref, tswa_ref, oa_ref, ob_ref,
                 qall, qball, mneg, mneg_far, m_s, acc_s, s_buf, oa_acc, qmask, *, n_far, n_top):
    first = pl.program_id(1) * QB_PER_STEP
    blocks, steps = [], []
    for n in range(QB_PER_STEP):
        qrows = pl.ds(n * Q_BLOCK, Q_BLOCK)
        blk = _query_block(first + n, sink_ref, qa_ref.at[qrows, :], qb_ref.at[qrows, :], ga_ref.at[qrows, :],
                           kcmp_ref, vcmp_ref, ks_ref, vs_ref, kw_ref, vw_ref, kb_ref, vb_ref, cmat_ref, tnear_ref,
                           tsel_ref, twin_ref, tswa_ref, oa_ref.at[qrows, :], ob_ref.at[qrows, :],
                           qall.at[n], qball.at[n], mneg.at[n], mneg_far.at[n], m_s.at[n], acc_s.at[n], s_buf.at[n],
                           oa_acc.at[n], qmask.at[n], n_far, n_top)
        blocks.append(blk)
    steps = [next(blk) for blk in blocks]
    while steps[0] is None:
        steps = [next(blk) for blk in blocks]
    n_far_keys = jnp.maximum(first + QB_PER_STEP - 2, 0) * Q_BLOCK
    n_chunks = (n_far_keys + SEL_CHUNK - 1) // SEL_CHUNK

    def far_body(j, carry):
        for far_update, _ in steps:
            far_update(j)
        for _, far_logits in steps:
            far_logits(j + 1, (j + 1) % 2, True)
        return carry

    last = jnp.maximum(n_chunks - 1, 0)
    lax.fori_loop(0, last, far_body, 0)
    for far_update, _ in steps:
        far_update(last)
    for blk in blocks:
        next(blk)


def _rel_bucket_np(dist):
    n = np.maximum(dist, 0)
    max_exact = REL_BUCKETS // 2
    nf = np.maximum(n, 1).astype(np.float32)
    log_b = max_exact + (np.log(nf / max_exact) / math.log(REL_MAX_DIST / max_exact)
                         * (REL_BUCKETS - max_exact)).astype(np.int32)
    log_b = np.minimum(log_b, REL_BUCKETS - 1)
    return np.where(n < max_exact, n, log_b)


def _toeplitz_bias(tab, pad, width, window, shift_far):
    length = width + Q_BLOCK
    dist = pad + Q_BLOCK - 1 - np.arange(length)
    onehot = np.zeros((length, REL_BUCKETS), np.float32)
    onehot[np.arange(length), _rel_bucket_np(dist)] = 1.0
    vals = jnp.dot(jnp.asarray(onehot), tab, precision=lax.Precision.HIGHEST)
    if shift_far:
        vals = vals - tab[REL_BUCKETS - 1][None, :]
    vals = vals * LOG2E
    valid = (dist >= 0) & (dist < window)
    vals = jnp.where(jnp.asarray(valid)[:, None], vals, NEG).T
    skew = jnp.tile(vals, (1, Q_BLOCK))[:, :Q_BLOCK * (length - 1)].reshape(N_HEADS, Q_BLOCK, length - 1)
    return skew[:, :, Q_BLOCK - 1:Q_BLOCK - 1 + width].reshape(N_HEADS * Q_BLOCK, width).astype(F32)


def _attention(proj, kvcmp, sinks, bias_table, bsz, seq):
    assert seq % SEL_CHUNK == 0
    nq = seq // Q_BLOCK
    n_far = seq // CMP_STRIDE
    n_sel = seq // SEL_BLOCK
    n_top = min(SEL_TOP_N, n_sel)
    assert n_top >= SEL_INIT_BLOCKS + SEL_LOCAL_BLOCKS and n_sel <= LANES
    wpad = Q_BLOCK * (-(-(NSA_WINDOW - 1) // Q_BLOCK))
    bpad = Q_BLOCK * (-(-(SWA_WINDOW - 1) // Q_BLOCK))
    pair = lambda tab: tab.astype(F32).reshape(REL_BUCKETS, N_GROUPS, GROUP).transpose(0, 2, 1).reshape(REL_BUCKETS, -1)
    tab_a = pair(bias_table[:, :N_HEADS])
    tab_b = pair(bias_table[:, N_HEADS:])
    near_pad = CMP_STRIDE * CMP_FRONT - (CMP_BLOCK - 1)
    t_near = _toeplitz_bias(tab_a, near_pad, CMP_STRIDE * CMP_NEAR, 1 << 30, True)[:, ::CMP_STRIDE]
    t_sel = _toeplitz_bias(tab_a, Q_BLOCK, 2 * Q_BLOCK, 1 << 30, True)
    t_win = _toeplitz_bias(tab_a, wpad, wpad + Q_BLOCK, NSA_WINDOW, False)
    t_swa = _toeplitz_bias(tab_b, bpad, bpad + Q_BLOCK, SWA_WINDOW, False)
    n_rows = kvcmp.shape[2]
    cn = (np.arange(n_rows) - CMP_FRONT)[:, None] * CMP_STRIDE
    sj = np.arange(LANES)[None, :] * SEL_BLOCK
    cmat = ((cn < sj + SEL_BLOCK) & (cn + CMP_BLOCK > sj) & (cn >= 0) & (cn + CMP_BLOCK <= seq)
            & (sj < seq)).astype(np.float32)
    cmat = jnp.asarray(cmat, F32)
    padded = lambda name, p: jnp.pad(proj[name].reshape(bsz, seq, LANES), ((0, 0), (p, 0), (0, 0)))
    ks, vs = padded('ks', Q_BLOCK), padded('vs', Q_BLOCK)
    kw, vw = padded('kw', wpad), padded('vw', wpad)
    kb, vb = padded('kb', bpad), padded('vb', bpad)
    rows = N_HEADS * Q_BLOCK
    n_steps = nq // QB_PER_STEP
    qspec = pl.BlockSpec((QB_PER_STEP * Q_BLOCK, 4 * LANES), lambda b, i: (b * n_steps + i, 0))
    const2 = lambda shape: pl.BlockSpec(shape, lambda b, i: (0, 0))
    batch3 = lambda n: pl.BlockSpec((1, n, LANES), lambda b, i: (b, 0, 0))
    per_block = lambda shape, dtype: pltpu.VMEM((QB_PER_STEP,) + shape, dtype)
    kernel = functools.partial(_attn_kernel, n_far=n_far, n_top=n_top)
    return pl.pallas_call(
        kernel,
        grid=(bsz, n_steps),
        in_specs=[pl.BlockSpec(memory_space=pltpu.SMEM),
                  qspec, qspec,
                  pl.BlockSpec((QB_PER_STEP * Q_BLOCK, LANES), lambda b, i: (b * n_steps + i, 0)),
                  pl.BlockSpec((1, 1, n_rows, LANES), lambda b, i: (0, b, 0, 0)),
                  pl.BlockSpec((1, 1, n_rows, LANES), lambda b, i: (1, b, 0, 0)),
                  batch3(seq + Q_BLOCK), batch3(seq + Q_BLOCK),
                  batch3(seq + wpad), batch3(seq + wpad),
                  batch3(seq + bpad), batch3(seq + bpad),
                  const2((n_rows, LANES)),
                  const2((rows, CMP_NEAR)),
                  const2((rows, 2 * Q_BLOCK)),
                  const2((rows, wpad + Q_BLOCK)),
                  const2((rows, bpad + Q_BLOCK))],
        out_specs=[qspec, qspec],
        out_shape=[jax.ShapeDtypeStruct((bsz * seq, 4 * LANES), BF16)] * 2,
        scratch_shapes=[per_block((rows, LANES), MXU_DTYPE),
                        per_block((rows, LANES), MXU_DTYPE),
                        per_block((N_GROUPS * Q_BLOCK, LANES), MXU_DTYPE),
                        per_block((N_GROUPS * Q_BLOCK, LANES), MXU_DTYPE),
                        per_block((rows, LANES), F32),
                        per_block((rows, 2 * LANES), F32),
                        per_block((2, rows, SEL_CHUNK), F32),
                        per_block((Q_BLOCK, 4 * LANES), F32),
                        per_block((rows, 2 * LANES), MXU_DTYPE)],
        compiler_params=pltpu.CompilerParams(dimension_semantics=("arbitrary", "arbitrary"),
                                             vmem_limit_bytes=VMEM_LIMIT),
        name="attention",
    )(sinks.astype(F32) * LOG2E, proj['qa'], proj['qb'], proj['ga'], kvcmp, kvcmp, ks, vs, kw, vw, kb, vb,
      cmat, t_near, t_sel, t_win, t_swa)


def _layer_norm(y, g, b):
    mu = jnp.mean(y, axis=-1, keepdims=True)
    yc = y - mu
    var = jnp.mean(yc * yc, axis=-1, keepdims=True)
    return yc * lax.rsqrt(var + LN_EPS) * g + b


def _outproj_kernel(oa_ref, ob_ref, sg_ref, x_ref, pa_ref, pb_ref, wo_ref, g1_ref, b1_ref, wr_ref, rb_ref, sgu_ref,
                    sd_ref, tri_ref, h_ref, base_ref, eidx_ref, gate_ref, rank_ref, cnt_ref, carry):
    step = pl.program_id(0)
    tm = oa_ref.shape[0]

    @pl.when(step == 0)
    def _():
        carry[...] = jnp.zeros(carry.shape, F32)

    sg = sg_ref[...].astype(F32)
    merged = (sg[:, :D_MODEL] * _dot(_mx(oa_ref[...]), pa_ref[...])
              + sg[:, D_MODEL:] * _dot(_mx(ob_ref[...]), pb_ref[...]))
    mix = _dot(_mx(merged), wo_ref[...])
    h = _layer_norm(DN_ALPHA * x_ref[...] + mix, g1_ref[...], b1_ref[...])
    hb = _mx(h)
    h_ref[...] = _pack_bf16_pairs(h)

    gu = _dot(hb, sgu_ref[...])
    shared = _dot(_mx(jax.nn.silu(gu[:, :SHARED_HIDDEN]) * gu[:, SHARED_HIDDEN:]), sd_ref[...])
    base_ref[...] = DN_ALPHA * h + shared

    scores = jax.nn.sigmoid(_dot_nt(wr_ref[...], hb))
    choice = scores + rb_ref[:, 0:1]
    per_group = N_EXPERTS // N_EXPERT_GROUPS
    gs = []
    for g in range(N_EXPERT_GROUPS):
        cg = choice[g * per_group:(g + 1) * per_group]
        m1 = jnp.max(cg, axis=0, keepdims=True)
        is_m = cg == m1
        n_m = jnp.sum(is_m.astype(F32), axis=0, keepdims=True)
        m2 = jnp.max(jnp.where(is_m, -jnp.inf, cg), axis=0, keepdims=True)
        gs.append(m1 + jnp.where(n_m > 1.5, m1, m2))
    gs = jnp.concatenate(gs, axis=0)
    gid = lax.broadcasted_iota(jnp.int32, gs.shape, 0)
    beaten = jnp.zeros(gs.shape, jnp.int32)
    for g in range(N_EXPERT_GROUPS):
        other = gs[g:g + 1]
        beaten = beaten + ((other > gs) | ((other == gs) & (g < gid))).astype(jnp.int32)
    keep_g = beaten < TOPK_EXPERT_GROUPS
    keep = jnp.concatenate([jnp.broadcast_to(keep_g[g:g + 1], (per_group, tm)) for g in range(N_EXPERT_GROUPS)],
                           axis=0)
    cand = jnp.where(keep, choice, -jnp.inf)
    eid = lax.broadcasted_iota(jnp.int32, cand.shape, 0)
    hits = []
    e_rows = []
    w_rows = []
    for _ in range(TOP_K):
        m = jnp.max(cand, axis=0, keepdims=True)
        idx = jnp.min(jnp.where(cand == m, eid, N_EXPERTS), axis=0, keepdims=True)
        hit = eid == idx
        hits.append(hit)
        e_rows.append(idx)
        w_rows.append(jnp.sum(jnp.where(hit, scores, 0.0), axis=0, keepdims=True))
        cand = jnp.where(hit, -jnp.inf, cand)
    w = jnp.concatenate(w_rows, axis=0)
    gate_ref[...] = w / jnp.sum(w, axis=0, keepdims=True) * ROUTED_SCALE
    eidx_ref[...] = jnp.concatenate(e_rows, axis=0)

    onehot = jnp.zeros(cand.shape, F32)
    for hit in hits:
        onehot = onehot + hit.astype(F32)
    before = _dot(onehot.astype(BF16), tri_ref[...]) + carry[:, 0:1]
    rank_ref[...] = jnp.concatenate(
        [jnp.sum(jnp.where(hit, before, 0.0), axis=0, keepdims=True) for hit in hits], axis=0).astype(jnp.int32)
    carry[...] = carry[...] + jnp.sum(onehot, axis=1, keepdims=True)
    cnt_ref[...] = carry[...]


def _out_projection(oa, ob, sg, x2, proj_a, proj_b, w_out, ln_g, ln_b, w_router, router_bias, s_gate, s_up, s_down):
    t = x2.shape[0]
    tm = OUT_TM
    pair_rows = lambda p: p.reshape(N_GROUPS, GROUP, HEAD_DIM, -1).transpose(1, 0, 2, 3).reshape(p.shape)
    pa = pair_rows(proj_a).astype(MXU_DTYPE)
    pb = pair_rows(proj_b).astype(MXU_DTYPE)
    tri = jnp.asarray(np.triu(np.ones((tm, tm), np.float32), 1), BF16)
    row = lambda i: (i, 0)
    fixed = lambda i: (0, 0)
    col = lambda i: (0, i)
    outs = pl.pallas_call(
        _outproj_kernel,
        grid=(t // tm,),
        in_specs=[pl.BlockSpec((tm, 4 * LANES), row), pl.BlockSpec((tm, 4 * LANES), row),
                  pl.BlockSpec((tm, 2 * D_MODEL), row), pl.BlockSpec((tm, D_MODEL), row),
                  pl.BlockSpec((4 * LANES, D_MODEL), fixed), pl.BlockSpec((4 * LANES, D_MODEL), fixed),
                  pl.BlockSpec((D_MODEL, D_MODEL), fixed),
                  pl.BlockSpec((1, D_MODEL), fixed), pl.BlockSpec((1, D_MODEL), fixed),
                  pl.BlockSpec((N_EXPERTS, D_MODEL), fixed), pl.BlockSpec((N_EXPERTS, LANES), fixed),
                  pl.BlockSpec((D_MODEL, 2 * SHARED_HIDDEN), fixed), pl.BlockSpec((SHARED_HIDDEN, D_MODEL), fixed),
                  pl.BlockSpec((tm, tm), fixed)],
        out_specs=[pl.BlockSpec((tm, D_MODEL // 2), row), pl.BlockSpec((tm, D_MODEL), row),
                   pl.BlockSpec((TOP_K, tm), col), pl.BlockSpec((TOP_K, tm), col), pl.BlockSpec((TOP_K, tm), col),
                   pl.BlockSpec((N_EXPERTS, LANES), fixed)],
        out_shape=[jax.ShapeDtypeStruct((t, D_MODEL // 2), jnp.uint32), jax.ShapeDtypeStruct((t, D_MODEL), F32),
                   jax.ShapeDtypeStruct((TOP_K, t), jnp.int32), jax.ShapeDtypeStruct((TOP_K, t), F32),
                   jax.ShapeDtypeStruct((TOP_K, t), jnp.int32), jax.ShapeDtypeStruct((N_EXPERTS, LANES), F32)],
        scratch_shapes=[pltpu.VMEM((N_EXPERTS, LANES), F32)],
        compiler_params=pltpu.CompilerParams(dimension_semantics=("arbitrary",), vmem_limit_bytes=VMEM_LIMIT),
        name="out_projection_router",
    )(oa, ob, sg, x2, pa, pb, w_out.astype(MXU_DTYPE), ln_g.reshape(1, -1), ln_b.reshape(1, -1),
      w_router.T.astype(MXU_DTYPE), jnp.broadcast_to(router_bias.astype(F32)[:, None], (N_EXPERTS, LANES)),
      jnp.concatenate([s_gate, s_up], axis=1).astype(MXU_DTYPE), s_down.astype(MXU_DTYPE), tri)
    return outs


def _rows_to_tiles(x):
    return pltpu.einshape("cml->mcl", jnp.stack(_lane_tiles(x), axis=0))


def _tiles_to_rows(x3):
    xt = pltpu.einshape("mcl->cml", x3)
    return jnp.concatenate([xt[c] for c in range(xt.shape[0])], axis=1)


def _dispatch_kernel(pend_ref, pad_ref, dest_ref, h2_ref, xs_ref, h_ref, zeros, sem, zsem):
    step = pl.program_id(0)
    tm = h2_ref.shape[0]
    slot = step % 2
    h_ref[slot] = _rows_to_tiles(h2_ref[...])

    @pl.when(step == 0)
    def _():
        zeros[...] = jnp.zeros(zeros.shape, zeros.dtype)

        def for_pieces(action):
            def body(e, c):
                for piece in range(MOE_BM // ZERO_ROWS):
                    @pl.when(pad_ref[e] > piece * ZERO_ROWS)
                    def _():
                        start = pend_ref[e] - (piece + 1) * ZERO_ROWS
                        action(pltpu.make_async_copy(zeros, xs_ref.at[pl.ds(start, ZERO_ROWS)], zsem))
                return c
            lax.fori_loop(0, N_EXPERTS, body, 0)
        for_pieces(lambda cp: cp.start())
        for_pieces(lambda cp: cp.wait())

    def issue(t, c):
        for k in range(TOP_K):
            pltpu.make_async_copy(h_ref.at[slot, t], xs_ref.at[dest_ref[k, t]], sem.at[slot]).start(priority=k % 2)
        return c
    lax.fori_loop(0, tm, issue, 0)

    def wait_tile(s):
        for k in range(TOP_K):
            pltpu.make_async_copy(h_ref.at[s], xs_ref.at[pl.ds(0, tm)], sem.at[s]).wait()

    @pl.when(step > 0)
    def _():
        wait_tile(1 - slot)

    @pl.when(step + 1 == pl.num_programs(0))
    def _():
        wait_tile(slot)


def _dispatch(h, dest, pends, pad_rows, n_rows):
    t = h.shape[0]
    tm = DISP_TM
    return pl.pallas_call(
        _dispatch_kernel,
        grid_spec=pltpu.PrefetchScalarGridSpec(
            num_scalar_prefetch=2,
            grid=(t // tm,),
            in_specs=[pl.BlockSpec((TOP_K, tm), lambda i, *_: (0, i), memory_space=pltpu.SMEM),
                      pl.BlockSpec((tm, D_MODEL // 2), lambda i, *_: (i, 0))],
            out_specs=pl.BlockSpec(memory_space=pl.ANY),
            scratch_shapes=[pltpu.VMEM((2, tm) + PACKED_ROW_TILE, jnp.uint32),
                            pltpu.VMEM((ZERO_ROWS,) + PACKED_ROW_TILE, jnp.uint32),
                            pltpu.SemaphoreType.DMA((2,)), pltpu.SemaphoreType.DMA(())]),
        out_shape=jax.ShapeDtypeStruct((n_rows,) + PACKED_ROW_TILE, jnp.uint32),
        compiler_params=pltpu.CompilerParams(dimension_semantics=("arbitrary",), vmem_limit_bytes=VMEM_LIMIT),
        name="moe_dispatch",
    )(pends, pad_rows, dest, h)


def _experts_kernel(blk_e_ref, nused_ref, first_ref, slot_ref, has_next_ref, next_e_ref, xs_hbm, wg_hbm, wu_hbm,
                    wd_hbm, ys_ref, wg_s, wu_s, wd_s, wg_f, wu_f, wd_f, sem, xbuf, xsem):
    b = pl.program_id(0)

    def weight_copies(e, s):
        return (pltpu.make_async_copy(wg_hbm.at[e], wg_f.at[s], sem.at[s]),
                pltpu.make_async_copy(wu_hbm.at[e], wu_f.at[s], sem.at[s]),
                pltpu.make_async_copy(wd_hbm.at[e], wd_f.at[s], sem.at[s]))

    @pl.when(first_ref[b] == 1)
    def _():
        s = slot_ref[b]

        @pl.when(b == 0)
        def _():
            for cp in weight_copies(blk_e_ref[0], s):
                cp.start()
        for cp in weight_copies(blk_e_ref[b], s):
            cp.wait()
        wg_s[...] = _mx(wg_f[s])
        wu_s[...] = _mx(wu_f[s])
        wd_s[...] = _mx(wd_f[s])

        @pl.when(has_next_ref[b] == 1)
        def _():
            for cp in weight_copies(next_e_ref[b], 1 - s):
                cp.start()

    def row_copy(blk):
        s = blk % X_SLOTS
        return pltpu.make_async_copy(xs_hbm.at[pl.ds(blk * MOE_BM, MOE_BM)], xbuf.at[s], xsem.at[s])

    @pl.when(b == 0)
    def _():
        for ahead in range(X_SLOTS - 1):
            @pl.when(ahead < nused_ref[0])
            def _():
                row_copy(ahead).start()

    @pl.when(b + X_SLOTS - 1 < nused_ref[0])
    def _():
        row_copy(b + X_SLOTS - 1).start()

    @pl.when(b < nused_ref[0])
    def _():
        row_copy(b).wait()
        xb = _mx(jnp.concatenate(_unpack_bf16_pairs(_tiles_to_rows(xbuf[b % X_SLOTS])), axis=1))
        hid = jax.nn.silu(_dot(xb, wg_s[...])) * _dot(xb, wu_s[...])
        ys_ref[...] = _rows_to_tiles(_pack_bf16_pairs(_dot(_mx(hid), wd_s[...])))

    @pl.when(b >= nused_ref[0])
    def _():
        ys_ref[...] = jnp.zeros(ys_ref.shape, ys_ref.dtype)


def _experts(xs, blk_e, nused, e_gate, e_up, e_down):
    n_rows = xs.shape[0]
    n_blocks = n_rows // MOE_BM
    idx = jnp.arange(n_blocks, dtype=jnp.int32)
    first = (idx < nused[0]) & ((idx == 0) | (blk_e != jnp.roll(blk_e, 1)))
    slot = ((jnp.cumsum(first.astype(jnp.int32)) - 1) % 2).astype(jnp.int32)
    later_first = lax.cummin(jnp.where(first, idx, n_blocks), reverse=True)
    next_first = jnp.concatenate([later_first[1:], jnp.full((1,), n_blocks, jnp.int32)])
    has_next = (next_first < n_blocks).astype(jnp.int32)
    next_e = blk_e[jnp.minimum(next_first, n_blocks - 1)]
    omap = lambda b, *refs: (jnp.where(b < refs[1][0], b, n_blocks - 1), 0, 0)
    hbm = pl.BlockSpec(memory_space=pl.ANY)
    return pl.pallas_call(
        _experts_kernel,
        grid_spec=pltpu.PrefetchScalarGridSpec(
            num_scalar_prefetch=6,
            grid=(n_blocks,),
            in_specs=[hbm, hbm, hbm, hbm],
            out_specs=pl.BlockSpec((MOE_BM,) + PACKED_ROW_TILE, omap),
            scratch_shapes=[pltpu.VMEM((D_MODEL, EXPERT_HIDDEN), MXU_DTYPE),
                            pltpu.VMEM((D_MODEL, EXPERT_HIDDEN), MXU_DTYPE),
                            pltpu.VMEM((EXPERT_HIDDEN, D_MODEL), MXU_DTYPE),
                            pltpu.VMEM((2, D_MODEL, EXPERT_HIDDEN), F32),
                            pltpu.VMEM((2, D_MODEL, EXPERT_HIDDEN), F32),
                            pltpu.VMEM((2, EXPERT_HIDDEN, D_MODEL), F32),
                            pltpu.SemaphoreType.DMA((2,)),
                            pltpu.VMEM((X_SLOTS, MOE_BM) + PACKED_ROW_TILE, jnp.uint32),
                            pltpu.SemaphoreType.DMA((X_SLOTS,))]),
        out_shape=jax.ShapeDtypeStruct((n_rows,) + PACKED_ROW_TILE, jnp.uint32),
        compiler_params=pltpu.CompilerParams(dimension_semantics=("arbitrary",), vmem_limit_bytes=VMEM_LIMIT),
        name="moe_experts",
    )(blk_e, nused, first.astype(jnp.int32), slot, has_next, next_e.astype(jnp.int32), xs, e_gate, e_up, e_down)


def _combine_kernel(dest_ref, dest_next_ref, gate_ref, base_ref, g2_ref, b2_ref, ys_ref, out_ref, buf, ysum, sem):
    step = pl.program_id(0)
    tm = base_ref.shape[0]
    slot = step % 2

    sub = ROW_TILE[0]

    def gather_rows(d_ref, s, t0):
        for u in range(sub):
            for k in range(TOP_K):
                pltpu.make_async_copy(ys_ref.at[d_ref[(t0 + u) * TOP_K + k]], buf.at[s, k * tm + t0 + u],
                                      sem.at[s]).start(priority=k % 2)

    def combine_rows(t0):
        y = base_ref[pl.ds(t0, sub), :]
        gates = gate_ref[pl.ds(t0, sub), :]
        for k in range(TOP_K):
            words = _tiles_to_rows(buf[slot, pl.ds(k * tm + t0, sub)])
            y = y + gates[:, k:k + 1] * jnp.concatenate(_unpack_bf16_pairs(words), axis=1)
        ysum[pl.ds(t0, sub), :] = y

    def for_token_groups(body):
        def trip(g, c):
            body(pl.multiple_of(g * sub, sub))
            return c
        lax.fori_loop(0, tm // sub, trip, 0)

    @pl.when(step == 0)
    def _():
        for_token_groups(lambda t0: gather_rows(dest_ref, 0, t0))

    pltpu.make_async_copy(ys_ref.at[pl.ds(0, tm * TOP_K)], buf.at[slot], sem.at[slot]).wait()

    @pl.when(step + 1 < pl.num_programs(0))
    def _():
        def both(t0):
            gather_rows(dest_next_ref, 1 - slot, t0)
            combine_rows(t0)
        for_token_groups(both)

    @pl.when(step + 1 == pl.num_programs(0))
    def _():
        for_token_groups(combine_rows)

    out_ref[...] = _layer_norm(ysum[...], g2_ref[...], b2_ref[...])


def _combine(ys3, dest, gate, base, ln_g, ln_b):
    t = base.shape[0]
    tm = COMB_TM
    n_tiles = t // tm
    dest_tk = dest.T.reshape(-1)
    return pl.pallas_call(
        _combine_kernel,
        grid=(n_tiles,),
        in_specs=[pl.BlockSpec((tm * TOP_K,), lambda i: (i,), memory_space=pltpu.SMEM),
                  pl.BlockSpec((tm * TOP_K,), lambda i: (jnp.minimum(i + 1, n_tiles - 1),), memory_space=pltpu.SMEM),
                  pl.BlockSpec((tm, TOP_K), lambda i: (i, 0)),
                  pl.BlockSpec((tm, D_MODEL), lambda i: (i, 0)),
                  pl.BlockSpec((1, D_MODEL), lambda i: (0, 0)),
                  pl.BlockSpec((1, D_MODEL), lambda i: (0, 0)),
                  pl.BlockSpec(memory_space=pl.ANY)],
        out_specs=pl.BlockSpec((tm, D_MODEL), lambda i: (i, 0)),
        out_shape=jax.ShapeDtypeStruct((t, D_MODEL), F32),
        scratch_shapes=[pltpu.VMEM((2, tm * TOP_K) + PACKED_ROW_TILE, jnp.uint32), pltpu.VMEM((tm, D_MODEL), F32),
                        pltpu.SemaphoreType.DMA((2,))],
        compiler_params=pltpu.CompilerParams(dimension_semantics=("arbitrary",), vmem_limit_bytes=VMEM_LIMIT),
        name="moe_combine",
    )(dest_tk, dest_tk, gate.T, base, ln_g.reshape(1, -1), ln_b.reshape(1, -1), ys3)


def _dest_kernel(pstart_ref, eidx_ref, rank_ref, dest_ref):
    eidx = eidx_ref[...]

    unroll = 8

    def body(g, dest):
        for u in range(unroll):
            e = g * unroll + u
            dest = dest + jnp.where(eidx == e, pstart_ref[e], 0)
        return dest
    dest_ref[...] = lax.fori_loop(0, N_EXPERTS // unroll, body, rank_ref[...])


def _dest_rows(pstarts, eidx, rank):
    t = eidx.shape[1]
    tl = 2048
    spec = pl.BlockSpec((TOP_K, tl), lambda i, *_: (0, i))
    return pl.pallas_call(
        _dest_kernel,
        grid_spec=pltpu.PrefetchScalarGridSpec(num_scalar_prefetch=1, grid=(t // tl,), in_specs=[spec, spec],
                                               out_specs=spec),
        out_shape=jax.ShapeDtypeStruct(eidx.shape, jnp.int32),
        compiler_params=pltpu.CompilerParams(dimension_semantics=("arbitrary",)),
        name="moe_dest_rows",
    )(pstarts, eidx, rank)


def _moe_layout(eidx, rank, counts):
    n_assign = eidx.size
    n_blocks = (n_assign + N_EXPERTS * (MOE_BM - 1)) // MOE_BM
    padded = (counts + MOE_BM - 1) // MOE_BM * MOE_BM
    pends = jnp.cumsum(padded)
    pstarts = (pends - padded).astype(jnp.int32)
    dest = _dest_rows(pstarts, eidx, rank)
    block_row = jnp.arange(n_blocks, dtype=jnp.int32) * MOE_BM
    blk_e = jnp.minimum(jnp.sum(pends[None, :] <= block_row[:, None], axis=1), N_EXPERTS - 1).astype(jnp.int32)
    nused = (pends[-1:] // MOE_BM).astype(jnp.int32)
    pad_rows = (padded - counts).astype(jnp.int32)
    return dest.astype(jnp.int32), blk_e, nused, pends.astype(jnp.int32), pad_rows, n_blocks * MOE_BM


def _layer(x, w_in, cmp_pe, cmp_w1, cmp_b1, cmp_w2, sinks, bias_table, proj_a, proj_b, w_out, ln1_g, ln1_b,
           w_router, router_bias, e_gate, e_up, e_down, s_gate, s_up, s_down, ln2_g, ln2_b):
    bsz, seq, d = x.shape
    x2 = x.reshape(bsz * seq, d)
    proj = _in_projection(x2, w_in)
    kvcmp = _compress(proj['kc'], proj['vc'], bsz, seq, cmp_pe, cmp_w1, cmp_b1, cmp_w2)
    oa, ob = _attention(proj, kvcmp, sinks, bias_table, bsz, seq)
    h, base, eidx, gate, rank, cnt = _out_projection(oa, ob, proj['sg'], x2, proj_a, proj_b, w_out, ln1_g, ln1_b,
                                                     w_router, router_bias, s_gate, s_up, s_down)
    counts = cnt[:, 0].astype(jnp.int32)
    dest, blk_e, nused, pends, pad_rows, n_rows = _moe_layout(eidx, rank, counts)
    xs = _dispatch(h, dest, pends, pad_rows, n_rows)
    ys = _experts(xs, blk_e, nused, e_gate, e_up, e_down)
    out = _combine(ys, dest, gate, base, ln2_g, ln2_b)
    return out.reshape(bsz, seq, d)


def kernel(x, w_in, cmp_pe, cmp_w1, cmp_b1, cmp_w2, attn_sinks, rel_bias_table, proj_a, proj_b, w_out, ln1_g, ln1_b,
           w_router, router_bias, expert_w_gate, expert_w_up, expert_w_down, shared_w_gate, shared_w_up,
           shared_w_down, ln2_g, ln2_b):
    h = x
    for l in range(DEPTH):
        h = _layer(h, w_in[l], cmp_pe[l], cmp_w1[l], cmp_b1[l], cmp_w2[l], attn_sinks[l], rel_bias_table, proj_a[l],
                   proj_b[l], w_out[l], ln1_g[l], ln1_b[l], w_router[l], router_bias[l], expert_w_gate[l],
                   expert_w_up[l], expert_w_down[l], shared_w_gate[l], shared_w_up[l], shared_w_down[l], ln2_g[l],
                   ln2_b[l])
    return h
```

```python
import functools
import math

import numpy as np
import jax
import jax.numpy as jnp
from jax import lax
from jax.experimental import pallas as pl
from jax.experimental.pallas import tpu as pltpu

F32 = jnp.float32
BF16 = jnp.bfloat16
MXU_DTYPE = jnp.bfloat16

D_MODEL = 1024
HEAD_DIM = 64
ATTN_SCALE = HEAD_DIM ** -0.5
LOG2E = math.log2(math.e)
Q_BLOCK = 128
N_HEADS = 8
N_GROUPS = 2
GROUP = 4
CMP_BLOCK = 32
CMP_STRIDE = 16
CMP_HIDDEN = 128
SEL_BLOCK = 64
SEL_TOP_N = 8
SEL_INIT_BLOCKS = 1
SEL_LOCAL_BLOCKS = 2
NSA_WINDOW = 512
SWA_WINDOW = 128
REL_BUCKETS = 32
REL_MAX_DIST = 128
N_EXPERTS = 256
TOP_K = 8
EXPERT_HIDDEN = 256
SHARED_HIDDEN = 256
N_EXPERT_GROUPS = 8
TOPK_EXPERT_GROUPS = 4
ROUTED_SCALE = 2.5
LN_EPS = 1e-5
DEPTH = 1
DN_ALPHA = (2 * DEPTH) ** 0.25

NEG = -1e30
LANES = 128
ROW_TILE = (8, LANES)
PACKED_ROW_TILE = (4, LANES)
CMP_FRONT = 16
CMP_NEAR = LANES
SEL_CHUNK = 1024
QB_PER_STEP = 2
VMEM_LIMIT = 56 * 1024 * 1024

IN_TM = 1024
OUT_TM = 512
MOE_BM = 512
ZERO_ROWS = 64
X_SLOTS = 3
DISP_TM = 512
COMB_TM = 512


def _dot(a, b):
    return jnp.dot(a, b, preferred_element_type=F32)


def _dot_nt(a, b):
    return lax.dot_general(a, b, (((1,), (1,)), ((), ())), preferred_element_type=F32)


def _mx(a):
    return a.astype(MXU_DTYPE)


def _pack_bf16_pairs(x):
    half = x.shape[1] // 2
    bits = lax.bitcast_convert_type(x.astype(BF16).astype(F32), jnp.uint32)
    return (bits[:, half:] & jnp.uint32(0xFFFF0000)) | (bits[:, :half] >> 16)


def _unpack_bf16_pairs(words):
    return (lax.bitcast_convert_type(words << 16, F32),
            lax.bitcast_convert_type(words & jnp.uint32(0xFFFF0000), F32))


_IN_COLS = (('qa', 512), ('qb', 512), ('kc', 128), ('vc', 128), ('ks', 128), ('vs', 128), ('kw', 128),
            ('vw', 128), ('kb', 128), ('vb', 128), ('ga', 128), ('sg', 2048))


def _inproj_kernel(x_ref, w_ref, qa_ref, qb_ref, kc_ref, vc_ref, ks_ref, vs_ref, kw_ref, vw_ref, kb_ref, vb_ref,
                   ga_ref, sg_ref):
    xb = _mx(x_ref[...])
    outs = dict(qa=qa_ref, qb=qb_ref, kc=kc_ref, vc=vc_ref, ks=ks_ref, vs=vs_ref, kw=kw_ref, vw=vw_ref,
                kb=kb_ref, vb=vb_ref, ga=ga_ref, sg=sg_ref)
    tiles = [(name, c) for name, width in _IN_COLS for c in range(0, width, LANES)]
    chunk = 4
    for t0 in range(0, len(tiles), chunk):
        group = tiles[t0:t0 + chunk]
        y = _dot(xb, w_ref[:, t0 * LANES:(t0 + len(group)) * LANES])
        for j, (name, c) in enumerate(group):
            yj = y[:, j * LANES:(j + 1) * LANES]
            if name in ('ga', 'sg'):
                yj = jax.nn.sigmoid(yj)
            outs[name][:, c:c + LANES] = yj.astype(outs[name].dtype)


def _pair_head_columns(w):
    return w.reshape(w.shape[0], N_GROUPS, GROUP, HEAD_DIM).transpose(0, 2, 1, 3).reshape(w.shape[0], -1)


def _in_projection(x2, w_in):
    t = x2.shape[0]
    sizes = (512, 128, 128, 128, 128, 128, 128, 24, 512, 128, 128, 1024, 1024)
    offs = np.cumsum((0,) + sizes)
    part = [w_in[:, offs[k]:offs[k + 1]] for k in range(len(sizes))]
    w_qa, w_kc, w_vc, w_ks, w_vs, w_kw, w_vw, w_g, w_qb, w_kb, w_vb, w_gate_a, w_gate_b = part
    w_qa = _pair_head_columns(w_qa) * (ATTN_SCALE * LOG2E)
    w_qb = _pair_head_columns(w_qb) * (ATTN_SCALE * LOG2E)
    w_ga = w_g.reshape(-1, N_GROUPS, GROUP, 3).transpose(0, 3, 2, 1).reshape(-1, 24)
    w_ga = jnp.pad(w_ga, ((0, 0), (0, LANES - 24)))
    w_all = jnp.concatenate([w_qa, w_qb, w_kc, w_vc, w_ks, w_vs, w_kw, w_vw, w_kb, w_vb, w_ga, w_gate_a, w_gate_b],
                            axis=1).astype(MXU_DTYPE)
    n_all = w_all.shape[1]
    out_shape = []
    out_specs = []
    for name, width in _IN_COLS:
        dt = F32 if name == 'ga' else BF16
        out_shape.append(jax.ShapeDtypeStruct((t, width), dt))
        out_specs.append(pl.BlockSpec((IN_TM, width), lambda i: (i, 0)))
    outs = pl.pallas_call(
        _inproj_kernel,
        grid=(t // IN_TM,),
        in_specs=[pl.BlockSpec((IN_TM, D_MODEL), lambda i: (i, 0)),
                  pl.BlockSpec((D_MODEL, n_all), lambda i: (0, 0))],
        out_specs=out_specs,
        out_shape=out_shape,
        compiler_params=pltpu.CompilerParams(dimension_semantics=("arbitrary",), vmem_limit_bytes=VMEM_LIMIT),
        name="in_projection",
    )(x2, w_all)
    return dict(zip([n for n, _ in _IN_COLS], outs))


def _compress_kernel(tok_ref, w1_ref, pe_ref, w1o_ref, b1_ref, w2_ref, out_ref):
    n_chunks = tok_ref.shape[2]
    ab = _dot(tok_ref[0, 0], w1_ref[0])
    a = ab[:, :2 * CMP_HIDDEN]
    b_next = pltpu.roll(ab[:, 2 * CMP_HIDDEN:], n_chunks - 1, 0)
    cb = _dot(_mx(pe_ref[0]), _mx(w1o_ref[0]))[0:1, :] + b1_ref[0]
    cb2 = jnp.concatenate([cb, cb], axis=1)
    hid = jax.nn.gelu(a + b_next + cb2)
    out = _dot(_mx(hid), w2_ref[0])
    row = lax.broadcasted_iota(jnp.int32, out.shape, 0)
    out = jnp.where(row < n_chunks - 1, out, 0.0)
    out_ref[0, 0, 0:CMP_FRONT, :] = jnp.zeros((CMP_FRONT, LANES), F32)
    out_ref[0, 0, CMP_FRONT:CMP_FRONT + n_chunks, :] = out
    out_ref[0, 0, CMP_FRONT + n_chunks:, :] = jnp.zeros((CMP_NEAR - CMP_FRONT, LANES), F32)


def _compress(kc, vc, bsz, seq, cmp_pe, cmp_w1, cmp_b1, cmp_w2):
    n_chunks = seq // CMP_STRIDE
    tok = jnp.stack([kc, vc]).reshape(2, bsz, n_chunks, CMP_STRIDE * LANES)
    eye = jnp.eye(N_GROUPS, dtype=F32)
    w1r = cmp_w1.reshape(2, 2, CMP_STRIDE, HEAD_DIM, CMP_HIDDEN)
    w1 = jnp.einsum('khjdn,gG->kjgdhGn', w1r, eye).reshape(2, CMP_STRIDE * LANES, 4 * CMP_HIDDEN).astype(MXU_DTYPE)
    w2 = jnp.einsum('knd,gG->kgnGd', cmp_w2, eye).reshape(2, 2 * CMP_HIDDEN, LANES).astype(MXU_DTYPE)
    pe = jnp.pad(cmp_pe.reshape(2, 1, CMP_BLOCK * HEAD_DIM), ((0, 0), (0, 7), (0, 0)))
    b1 = cmp_b1.reshape(2, 1, CMP_HIDDEN)
    rows = CMP_FRONT + n_chunks + CMP_NEAR - CMP_FRONT
    return pl.pallas_call(
        _compress_kernel,
        grid=(2, bsz),
        in_specs=[pl.BlockSpec((1, 1, n_chunks, CMP_STRIDE * LANES), lambda k, b: (k, b, 0, 0)),
                  pl.BlockSpec((1, CMP_STRIDE * LANES, 4 * CMP_HIDDEN), lambda k, b: (k, 0, 0)),
                  pl.BlockSpec((1, 8, CMP_BLOCK * HEAD_DIM), lambda k, b: (k, 0, 0)),
                  pl.BlockSpec((1, CMP_BLOCK * HEAD_DIM, CMP_HIDDEN), lambda k, b: (k, 0, 0)),
                  pl.BlockSpec((1, 1, CMP_HIDDEN), lambda k, b: (k, 0, 0)),
                  pl.BlockSpec((1, 2 * CMP_HIDDEN, LANES), lambda k, b: (k, 0, 0))],
        out_specs=pl.BlockSpec((1, 1, rows, LANES), lambda k, b: (k, b, 0, 0)),
        out_shape=jax.ShapeDtypeStruct((2, bsz, rows, LANES), F32),
        compiler_params=pltpu.CompilerParams(dimension_semantics=("arbitrary", "arbitrary"),
                                             vmem_limit_bytes=VMEM_LIMIT),
        name="nsa_compress",
    )(tok, w1, pe, cmp_w1, b1, w2)


def _stack_heads(q_ref, dst):
    lo = lax.broadcasted_iota(jnp.int32, (Q_BLOCK, LANES), 1) < HEAD_DIM
    for r in range(GROUP):
        qr = q_ref[:, r * LANES:(r + 1) * LANES].astype(dst.dtype)
        z = jnp.zeros_like(qr)
        dst[(2 * r) * Q_BLOCK:(2 * r + 1) * Q_BLOCK, :] = jnp.where(lo, qr, z)
        dst[(2 * r + 1) * Q_BLOCK:(2 * r + 2) * Q_BLOCK, :] = jnp.where(lo, z, qr)


def _pair_heads(o, r):
    lo = lax.broadcasted_iota(jnp.int32, (Q_BLOCK, LANES), 1) < HEAD_DIM
    return jnp.where(lo, o[(2 * r) * Q_BLOCK:(2 * r + 1) * Q_BLOCK], o[(2 * r + 1) * Q_BLOCK:(2 * r + 2) * Q_BLOCK])


def _lane_tiles(x):
    return [x[:, t * LANES:(t + 1) * LANES] for t in range(x.shape[1] // LANES)]


def _row_max(tiles):
    mx = tiles[0]
    for t in tiles[1:]:
        mx = jnp.maximum(mx, t)
    return jnp.broadcast_to(jnp.max(mx, axis=1, keepdims=True), mx.shape)


def _with_ones(v):
    return jnp.concatenate([v, jnp.ones(v.shape, v.dtype)], axis=1)


def _block_of_key(n_keys, first_block):
    b = lax.broadcasted_iota(jnp.int32, (LANES, n_keys), 0)
    k = lax.broadcasted_iota(jnp.int32, (LANES, n_keys), 1)
    return (b == (k // SEL_BLOCK) + first_block).astype(MXU_DTYPE)


def _select_blocks_t(imp_t, i, n_top):
    blk = lax.broadcasted_iota(jnp.int32, imp_t.shape, 0)
    qcol = lax.broadcasted_iota(jnp.int32, imp_t.shape, 1)
    back = (2 * i + (qcol >= SEL_BLOCK).astype(jnp.int32)) - blk
    sel = (back >= 0) & ((blk < SEL_INIT_BLOCKS) | (back < SEL_LOCAL_BLOCKS))
    cand = jnp.where((back >= SEL_LOCAL_BLOCKS) & (blk >= SEL_INIT_BLOCKS), imp_t, -1.0)
    blk_f = blk.astype(F32)
    for _ in range(n_top - SEL_INIT_BLOCKS - SEL_LOCAL_BLOCKS):
        m = jnp.max(cand, axis=0, keepdims=True)
        idx = jnp.min(jnp.where(cand == m, blk_f, float(LANES)), axis=0, keepdims=True)
        hit = blk_f == idx
        sel = sel | (hit & (m >= 0.0))
        cand = jnp.where(hit, -2.0, cand)
    return sel


def _query_block(i, sink_ref, qa_ref, qb_ref, ga_ref, kcmp_ref, vcmp_ref, ks_ref, vs_ref, kw_ref, vw_ref, kb_ref,
                 vb_ref, cmat_ref, tnear_ref, tsel_ref, twin_ref, tswa_ref, oa_ref, ob_ref,
                 qall, qball, mneg, mneg_far, m_s, acc_s, s_buf, oa_acc, qmask, n_far, n_top):
    rows = N_HEADS * Q_BLOCK
    half = rows // 2
    halves = (slice(0, half), slice(half, rows))
    _stack_heads(qa_ref, qall)
    _stack_heads(qb_ref, qball)
    nstart = pl.multiple_of(i * Q_BLOCK, Q_BLOCK)
    lo = lax.broadcasted_iota(jnp.int32, (Q_BLOCK, LANES), 1) < HEAD_DIM
    gates = ga_ref[...]

    def gate_tile(c, r):
        return jnp.where(lo, gates[:, c * 8 + 2 * r:c * 8 + 2 * r + 1], gates[:, c * 8 + 2 * r + 1:c * 8 + 2 * r + 2])

    def softmax_pv(s_tiles, v1, fix_max=None):
        m = _row_max(s_tiles)
        if fix_max is not None:
            m = fix_max(m)
        e = [jnp.exp2(t - m) for t in s_tiles]
        return e, m, _dot(_mx(jnp.concatenate(e, axis=1)), v1)

    off = pl.multiple_of(i * (Q_BLOCK // CMP_STRIDE), 8)
    k_cmp = _mx(jnp.concatenate([kcmp_ref[0, 0, 0:n_far, :], kcmp_ref[0, 0, pl.ds(off, CMP_NEAR), :]], axis=0))
    v_cmp = _with_ones(_mx(jnp.concatenate([vcmp_ref[0, 0, 0:n_far, :], vcmp_ref[0, 0, pl.ds(off, CMP_NEAR), :]],
                                           axis=0)))
    colf = lax.broadcasted_iota(jnp.int32, (1, n_far), 1)
    coln = lax.broadcasted_iota(jnp.int32, (1, CMP_NEAR), 1)
    col_ok = jnp.concatenate([(colf >= CMP_FRONT) & (colf < off), coln + off >= CMP_FRONT], axis=1)
    mask_c = jnp.where(col_ok, 0.0, NEG)
    no_key = lambda m: jnp.where(m > 0.5 * NEG, m, 0.0)
    p_cmp, o_c = [], []
    for rs in halves:
        tiles = _lane_tiles(_dot_nt(qall[rs, :], k_cmp) + mask_c)
        tiles[-1] = tiles[-1] + tnear_ref[rs, :]
        e, _, ov = softmax_pv(tiles, v_cmp, no_key)
        inv = 1.0 / jnp.maximum(ov[:, LANES:], 1e-30)
        o_c.append(ov[:, :LANES] * inv)
        p_cmp.append([t * inv for t in e])
    o_c = jnp.concatenate(o_c, axis=0)
    yield None

    def far_start(j):
        return pl.multiple_of(Q_BLOCK + j * SEL_CHUNK, Q_BLOCK)

    def far_logits(j, slot, masked):
        kc = _mx(ks_ref[0, pl.ds(far_start(j), SEL_CHUNK), :])
        if masked:
            key = lax.broadcasted_iota(jnp.int32, (SEL_CHUNK, LANES), 0)
            blk = lax.broadcasted_iota(jnp.int32, (SEL_CHUNK, LANES), 1)
            one_hot = (blk == key // SEL_BLOCK + j * (SEL_CHUNK // SEL_BLOCK)).astype(MXU_DTYPE)
            kc = jnp.concatenate([kc, one_hot], axis=1)
        for rs in halves:
            s_buf[slot, rs, :] = _dot_nt(qmask[rs, :] if masked else qall[rs, :], kc)

    far_logits(0, 0, False)

    blkcol = lax.broadcasted_iota(jnp.int32, (Q_BLOCK, LANES), 1)
    n_tiles = len(p_cmp[0])
    for g in range(N_GROUPS):
        imp = jnp.zeros((Q_BLOCK, LANES), F32)
        for t in range(n_tiles):
            pg = sum(p_cmp[r // 2][t][(2 * (r % 2) + g) * Q_BLOCK:(2 * (r % 2) + g + 1) * Q_BLOCK]
                     for r in range(GROUP))
            if t < n_tiles - 1:
                cm = _mx(cmat_ref[t * LANES:(t + 1) * LANES, :])
            else:
                cm = _mx(cmat_ref[pl.ds(off, CMP_NEAR), :])
            hi = _mx(pg)
            low = _mx(pg - hi.astype(F32))
            imp = imp + _dot(hi, cm) + _dot(low, cm)
        sel = _select_blocks_t(imp.T, i, n_top)
        neg = jnp.where(sel, 0.0, NEG).T
        mneg[g * Q_BLOCK:(g + 1) * Q_BLOCK, :] = neg.astype(mneg.dtype)
        neg_far = jnp.where(blkcol < 2 * (i - 1), neg, NEG).astype(mneg.dtype)
        mneg_far[g * Q_BLOCK:(g + 1) * Q_BLOCK, :] = neg_far
        for r in range(GROUP):
            qmask[(2 * r + g) * Q_BLOCK:(2 * r + g + 1) * Q_BLOCK, LANES:] = neg_far
    qmask[:, :LANES] = qall[...]

    yield None

    wpad = kw_ref.shape[1] - ks_ref.shape[1] + Q_BLOCK
    kwin = _mx(kw_ref[0, pl.ds(nstart, wpad + Q_BLOCK), :])
    vwin = _with_ones(_mx(vw_ref[0, pl.ds(nstart, wpad + Q_BLOCK), :]))
    colw = lax.broadcasted_iota(jnp.int32, (1, wpad + Q_BLOCK), 1)
    mask_w = jnp.where(colw + nstart >= wpad, 0.0, NEG)
    o_w = []
    for rs in halves:
        _, _, ov = softmax_pv(_lane_tiles(_dot_nt(qall[rs, :], kwin) + twin_ref[rs, :] + mask_w), vwin)
        o_w.append(ov[:, :LANES] / ov[:, LANES:])
    o_w = jnp.concatenate(o_w, axis=0)
    for r in range(GROUP):
        oa_acc[:, r * LANES:(r + 1) * LANES] = (gate_tile(0, r) * _pair_heads(o_c, r)
                                                + gate_tile(2, r) * _pair_heads(o_w, r))

    yield None

    bpad = kb_ref.shape[1] - ks_ref.shape[1] + Q_BLOCK
    kwin = _mx(kb_ref[0, pl.ds(nstart, bpad + Q_BLOCK), :])
    vwin = _with_ones(_mx(vb_ref[0, pl.ds(nstart, bpad + Q_BLOCK), :]))
    colb = lax.broadcasted_iota(jnp.int32, (1, bpad + Q_BLOCK), 1)
    mask_b = jnp.where(colb + nstart >= bpad, 0.0, NEG)
    o_b = []
    for hh, rs in enumerate(halves):
        sink = jnp.concatenate([jnp.full((Q_BLOCK, LANES), sink_ref[(h % 2) * GROUP + h // 2], F32)
                                for h in range(hh * N_HEADS // 2, (hh + 1) * N_HEADS // 2)], axis=0)
        _, m, ov = softmax_pv(_lane_tiles(_dot_nt(qball[rs, :], kwin) + tswa_ref[rs, :] + mask_b), vwin,
                              lambda m: jnp.maximum(m, sink))
        o_b.append(ov[:, :LANES] / (ov[:, LANES:] + jnp.exp2(sink - m)))
    o_b = jnp.concatenate(o_b, axis=0)
    for r in range(GROUP):
        ob_ref[:, r * LANES:(r + 1) * LANES] = _pair_heads(o_b, r).astype(ob_ref.dtype)

    yield None

    m_s[...] = jnp.full(m_s.shape, NEG, F32)
    acc_s[...] = jnp.zeros(acc_s.shape, F32)

    def flash_update(rs, s, v1):
        s_tiles = _lane_tiles(s)
        m_old = m_s[rs, :]
        m_new = jnp.maximum(m_old, _row_max(s_tiles))
        alpha = jnp.exp2(m_old - m_new)
        p = jnp.concatenate([jnp.exp2(t - m_new) for t in s_tiles], axis=1)
        acc_s[rs, :] = jnp.concatenate([alpha, alpha], axis=1) * acc_s[rs, :] + _dot(_mx(p), v1)
        m_s[rs, :] = m_new

    madd = _dot(mneg_far[...], _block_of_key(SEL_CHUNK, 0))
    for rs in halves:
        s_buf[0, rs, :] = s_buf[0, rs, :] + jnp.concatenate([madd] * (GROUP // 2), axis=0)

    def far_update(j):
        v1 = _with_ones(_mx(vs_ref[0, pl.ds(far_start(j), SEL_CHUNK), :]))
        for rs in halves:
            flash_update(rs, s_buf[j % 2, rs, :], v1)

    yield far_update, far_logits

    kc = _mx(ks_ref[0, pl.ds(nstart, 2 * Q_BLOCK), :])
    v1 = _with_ones(_mx(vs_ref[0, pl.ds(nstart, 2 * Q_BLOCK), :]))
    madd = _dot(mneg[...], _block_of_key(2 * Q_BLOCK, 2 * (i - 1)))
    col2 = lax.broadcasted_iota(jnp.int32, (1, 2 * Q_BLOCK), 1)
    mask_n = jnp.where((col2 < Q_BLOCK) & (i == 0), NEG, 0.0)
    for rs in halves:
        s = _dot_nt(qall[rs, :], kc) + jnp.concatenate([madd] * (GROUP // 2), axis=0) + tsel_ref[rs, :] + mask_n
        flash_update(rs, s, v1)
    acc = acc_s[...]
    o_s = acc[:, :LANES] / acc[:, LANES:]
    for r in range(GROUP):
        tile = oa_acc[:, r * LANES:(r + 1) * LANES] + gate_tile(1, r) * _pair_heads(o_s, r)
        oa_ref[:, r * LANES:(r + 1) * LANES] = tile.astype(oa_ref.dtype)
    yield None


def _attn_kernel(sink_ref, qa_ref, qb_ref, ga_ref, kcmp_ref, vcmp_ref, ks_ref, vs_ref, kw_ref, vw_ref, kb_ref,
                 vb_ref, cmat_ref, tnear_ref, tsel_ref, twin_ref, tswa_ref, oa_ref, ob_ref,
                 qall, qball, mneg, mneg_far, m_s, acc_s, s_buf, oa_acc, qmask, *, n_far, n_top):
    first = pl.program_id(1) * QB_PER_STEP
    blocks, steps = [], []
    for n in range(QB_PER_STEP):
        qrows = pl.ds(n * Q_BLOCK, Q_BLOCK)
        blk = _query_block(first + n, sink_ref, qa_ref.at[qrows, :], qb_ref.at[qrows, :], ga_ref.at[qrows, :],
                           kcmp_ref, vcmp_ref, ks_ref, vs_ref, kw_ref, vw_ref, kb_ref, vb_ref, cmat_ref, tnear_ref,
                           tsel_ref, twin_ref, tswa_ref, oa_ref.at[qrows, :], ob_ref.at[qrows, :],
                           qall.at[n], qball.at[n], mneg.at[n], mneg_far.at[n], m_s.at[n], acc_s.at[n], s_buf.at[n],
                           oa_acc.at[n], qmask.at[n], n_far, n_top)
        blocks.append(blk)
    steps = [next(blk) for blk in blocks]
    while steps[0] is None:
        steps = [next(blk) for blk in blocks]
    n_far_keys = jnp.maximum(first + QB_PER_STEP - 2, 0) * Q_BLOCK
    n_chunks = (n_far_keys + SEL_CHUNK - 1) // SEL_CHUNK

    def far_body(j, carry):
        for far_update, _ in steps:
            far_update(j)
        for _, far_logits in steps:
            far_logits(j + 1, (j + 1) % 2, True)
        return carry

    last = jnp.maximum(n_chunks - 1, 0)
    lax.fori_loop(0, last, far_body, 0)
    for far_update, _ in steps:
        far_update(last)
    for blk in blocks:
        next(blk)


def _rel_bucket_np(dist):
    n = np.maximum(dist, 0)
    max_exact = REL_BUCKETS // 2
    nf = np.maximum(n, 1).astype(np.float32)
    log_b = max_exact + (np.log(nf / max_exact) / math.log(REL_MAX_DIST / max_exact)
                         * (REL_BUCKETS - max_exact)).astype(np.int32)
    log_b = np.minimum(log_b, REL_BUCKETS - 1)
    return np.where(n < max_exact, n, log_b)


def _toeplitz_bias(tab, pad, width, window, shift_far):
    length = width + Q_BLOCK
    dist = pad + Q_BLOCK - 1 - np.arange(length)
    onehot = np.zeros((length, REL_BUCKETS), np.float32)
    onehot[np.arange(length), _rel_bucket_np(dist)] = 1.0
    vals = jnp.dot(jnp.asarray(onehot), tab, precision=lax.Precision.HIGHEST)
    if shift_far:
        vals = vals - tab[REL_BUCKETS - 1][None, :]
    vals = vals * LOG2E
    valid = (dist >= 0) & (dist < window)
    vals = jnp.where(jnp.asarray(valid)[:, None], vals, NEG).T
    skew = jnp.tile(vals, (1, Q_BLOCK))[:, :Q_BLOCK * (length - 1)].reshape(N_HEADS, Q_BLOCK, length - 1)
    return skew[:, :, Q_BLOCK - 1:Q_BLOCK - 1 + width].reshape(N_HEADS * Q_BLOCK, width).astype(F32)


def _attention(proj, kvcmp, sinks, bias_table, bsz, seq):
    assert seq % SEL_CHUNK == 0
    nq = seq // Q_BLOCK
    n_far = seq // CMP_STRIDE
    n_sel = seq // SEL_BLOCK
    n_top = min(SEL_TOP_N, n_sel)
    assert n_top >= SEL_INIT_BLOCKS + SEL_LOCAL_BLOCKS and n_sel <= LANES
    wpad = Q_BLOCK * (-(-(NSA_WINDOW - 1) // Q_BLOCK))
    bpad = Q_BLOCK * (-(-(SWA_WINDOW - 1) // Q_BLOCK))
    pair = lambda tab: tab.astype(F32).reshape(REL_BUCKETS, N_GROUPS, GROUP).transpose(0, 2, 1).reshape(REL_BUCKETS, -1)
    tab_a = pair(bias_table[:, :N_HEADS])
    tab_b = pair(bias_table[:, N_HEADS:])
    near_pad = CMP_STRIDE * CMP_FRONT - (CMP_BLOCK - 1)
    t_near = _toeplitz_bias(tab_a, near_pad, CMP_STRIDE * CMP_NEAR, 1 << 30, True)[:, ::CMP_STRIDE]
    t_sel = _toeplitz_bias(tab_a, Q_BLOCK, 2 * Q_BLOCK, 1 << 30, True)
    t_win = _toeplitz_bias(tab_a, wpad, wpad + Q_BLOCK, NSA_WINDOW, False)
    t_swa = _toeplitz_bias(tab_b, bpad, bpad + Q_BLOCK, SWA_WINDOW, False)
    n_rows = kvcmp.shape[2]
    cn = (np.arange(n_rows) - CMP_FRONT)[:, None] * CMP_STRIDE
    sj = np.arange(LANES)[None, :] * SEL_BLOCK
    cmat = ((cn < sj + SEL_BLOCK) & (cn + CMP_BLOCK > sj) & (cn >= 0) & (cn + CMP_BLOCK <= seq)
            & (sj < seq)).astype(np.float32)
    cmat = jnp.asarray(cmat, F32)
    padded = lambda name, p: jnp.pad(proj[name].reshape(bsz, seq, LANES), ((0, 0), (p, 0), (0, 0)))
    ks, vs = padded('ks', Q_BLOCK), padded('vs', Q_BLOCK)
    kw, vw = padded('kw', wpad), padded('vw', wpad)
    kb, vb = padded('kb', bpad), padded('vb', bpad)
    rows = N_HEADS * Q_BLOCK
    n_steps = nq // QB_PER_STEP
    qspec = pl.BlockSpec((QB_PER_STEP * Q_BLOCK, 4 * LANES), lambda b, i: (b * n_steps + i, 0))
    const2 = lambda shape: pl.BlockSpec(shape, lambda b, i: (0, 0))
    batch3 = lambda n: pl.BlockSpec((1, n, LANES), lambda b, i: (b, 0, 0), pipeline_mode=pl.Buffered(1))
    per_block = lambda shape, dtype: pltpu.VMEM((QB_PER_STEP,) + shape, dtype)
    kernel = functools.partial(_attn_kernel, n_far=n_far, n_top=n_top)
    return pl.pallas_call(
        kernel,
        grid=(bsz, n_steps),
        in_specs=[pl.BlockSpec(memory_space=pltpu.SMEM),
                  qspec, qspec,
                  pl.BlockSpec((QB_PER_STEP * Q_BLOCK, LANES), lambda b, i: (b * n_steps + i, 0)),
                  pl.BlockSpec((1, 1, n_rows, LANES), lambda b, i: (0, b, 0, 0)),
                  pl.BlockSpec((1, 1, n_rows, LANES), lambda b, i: (1, b, 0, 0)),
                  batch3(seq + Q_BLOCK), batch3(seq + Q_BLOCK),
                  batch3(seq + wpad), batch3(seq + wpad),
                  batch3(seq + bpad), batch3(seq + bpad),
                  const2((n_rows, LANES)),
                  const2((rows, CMP_NEAR)),
                  const2((rows, 2 * Q_BLOCK)),
                  const2((rows, wpad + Q_BLOCK)),
                  const2((rows, bpad + Q_BLOCK))],
        out_specs=[qspec, qspec],
        out_shape=[jax.ShapeDtypeStruct((bsz * seq, 4 * LANES), BF16)] * 2,
        scratch_shapes=[per_block((rows, LANES), MXU_DTYPE),
                        per_block((rows, LANES), MXU_DTYPE),
                        per_block((N_GROUPS * Q_BLOCK, LANES), MXU_DTYPE),
                        per_block((N_GROUPS * Q_BLOCK, LANES), MXU_DTYPE),
                        per_block((rows, LANES), F32),
                        per_block((rows, 2 * LANES), F32),
                        per_block((2, rows, SEL_CHUNK), F32),
                        per_block((Q_BLOCK, 4 * LANES), F32),
                        per_block((rows, 2 * LANES), MXU_DTYPE)],
        compiler_params=pltpu.CompilerParams(dimension_semantics=("arbitrary", "arbitrary"),
                                             vmem_limit_bytes=VMEM_LIMIT),
        name="attention",
    )(sinks.astype(F32) * LOG2E, proj['qa'], proj['qb'], proj['ga'], kvcmp, kvcmp, ks, vs, kw, vw, kb, vb,
      cmat, t_near, t_sel, t_win, t_swa)


def _layer_norm(y, g, b):
    mu = jnp.mean(y, axis=-1, keepdims=True)
    yc = y - mu
    var = jnp.mean(yc * yc, axis=-1, keepdims=True)
    return yc * lax.rsqrt(var + LN_EPS) * g + b


def _outproj_kernel(oa_ref, ob_ref, sg_ref, x_ref, pa_ref, pb_ref, wo_ref, g1_ref, b1_ref, wr_ref, rb_ref, sgu_ref,
                    sd_ref, tri_ref, h_ref, base_ref, eidx_ref, gate_ref, rank_ref, cnt_ref, carry):
    step = pl.program_id(0)
    tm = oa_ref.shape[0]

    @pl.when(step == 0)
    def _():
        carry[...] = jnp.zeros(carry.shape, F32)

    sg = sg_ref[...].astype(F32)
    merged = (sg[:, :D_MODEL] * _dot(_mx(oa_ref[...]), pa_ref[...])
              + sg[:, D_MODEL:] * _dot(_mx(ob_ref[...]), pb_ref[...]))
    mix = _dot(_mx(merged), wo_ref[...])
    h = _layer_norm(DN_ALPHA * x_ref[...] + mix, g1_ref[...], b1_ref[...])
    hb = _mx(h)
    h_ref[...] = _pack_bf16_pairs(h)

    gu = _dot(hb, sgu_ref[...])
    shared = _dot(_mx(jax.nn.silu(gu[:, :SHARED_HIDDEN]) * gu[:, SHARED_HIDDEN:]), sd_ref[...])
    base_ref[...] = DN_ALPHA * h + shared

    scores = jax.nn.sigmoid(_dot_nt(wr_ref[...], hb))
    choice = scores + rb_ref[:, 0:1]
    per_group = N_EXPERTS // N_EXPERT_GROUPS
    gs = []
    for g in range(N_EXPERT_GROUPS):
        cg = choice[g * per_group:(g + 1) * per_group]
        m1 = jnp.max(cg, axis=0, keepdims=True)
        is_m = cg == m1
        n_m = jnp.sum(is_m.astype(F32), axis=0, keepdims=True)
        m2 = jnp.max(jnp.where(is_m, -jnp.inf, cg), axis=0, keepdims=True)
        gs.append(m1 + jnp.where(n_m > 1.5, m1, m2))
    gs = jnp.concatenate(gs, axis=0)
    gid = lax.broadcasted_iota(jnp.int32, gs.shape, 0)
    beaten = jnp.zeros(gs.shape, jnp.int32)
    for g in range(N_EXPERT_GROUPS):
        other = gs[g:g + 1]
        beaten = beaten + ((other > gs) | ((other == gs) & (g < gid))).astype(jnp.int32)
    keep_g = beaten < TOPK_EXPERT_GROUPS
    keep = jnp.concatenate([jnp.broadcast_to(keep_g[g:g + 1], (per_group, tm)) for g in range(N_EXPERT_GROUPS)],
                           axis=0)
    cand = jnp.where(keep, choice, -jnp.inf)
    eid = lax.broadcasted_iota(jnp.int32, cand.shape, 0)
    hits = []
    e_rows = []
    w_rows = []
    for _ in range(TOP_K):
        m = jnp.max(cand, axis=0, keepdims=True)
        idx = jnp.min(jnp.where(cand == m, eid, N_EXPERTS), axis=0, keepdims=True)
        hit = eid == idx
        hits.append(hit)
        e_rows.append(idx)
        w_rows.append(jnp.sum(jnp.where(hit, scores, 0.0), axis=0, keepdims=True))
        cand = jnp.where(hit, -jnp.inf, cand)
    w = jnp.concatenate(w_rows, axis=0)
    gate_ref[...] = w / jnp.sum(w, axis=0, keepdims=True) * ROUTED_SCALE
    eidx_ref[...] = jnp.concatenate(e_rows, axis=0)

    onehot = jnp.zeros(cand.shape, F32)
    for hit in hits:
        onehot = onehot + hit.astype(F32)
    before = _dot(onehot.astype(BF16), tri_ref[...]) + carry[:, 0:1]
    rank_ref[...] = jnp.concatenate(
        [jnp.sum(jnp.where(hit, before, 0.0), axis=0, keepdims=True) for hit in hits], axis=0).astype(jnp.int32)
    carry[...] = carry[...] + jnp.sum(onehot, axis=1, keepdims=True)
    cnt_ref[...] = carry[...]


def _out_projection(oa, ob, sg, x2, proj_a, proj_b, w_out, ln_g, ln_b, w_router, router_bias, s_gate, s_up, s_down):
    t = x2.shape[0]
    tm = OUT_TM
    pair_rows = lambda p: p.reshape(N_GROUPS, GROUP, HEAD_DIM, -1).transpose(1, 0, 2, 3).reshape(p.shape)
    pa = pair_rows(proj_a).astype(MXU_DTYPE)
    pb = pair_rows(proj_b).astype(MXU_DTYPE)
    tri = jnp.asarray(np.triu(np.ones((tm, tm), np.float32), 1), BF16)
    row = lambda i: (i, 0)
    fixed = lambda i: (0, 0)
    col = lambda i: (0, i)
    outs = pl.pallas_call(
        _outproj_kernel,
        grid=(t // tm,),
        in_specs=[pl.BlockSpec((tm, 4 * LANES), row), pl.BlockSpec((tm, 4 * LANES), row),
                  pl.BlockSpec((tm, 2 * D_MODEL), row), pl.BlockSpec((tm, D_MODEL), row),
                  pl.BlockSpec((4 * LANES, D_MODEL), fixed), pl.BlockSpec((4 * LANES, D_MODEL), fixed),
                  pl.BlockSpec((D_MODEL, D_MODEL), fixed),
                  pl.BlockSpec((1, D_MODEL), fixed), pl.BlockSpec((1, D_MODEL), fixed),
                  pl.BlockSpec((N_EXPERTS, D_MODEL), fixed), pl.BlockSpec((N_EXPERTS, LANES), fixed),
                  pl.BlockSpec((D_MODEL, 2 * SHARED_HIDDEN), fixed), pl.BlockSpec((SHARED_HIDDEN, D_MODEL), fixed),
                  pl.BlockSpec((tm, tm), fixed)],
        out_specs=[pl.BlockSpec((tm, D_MODEL // 2), row), pl.BlockSpec((tm, D_MODEL), row),
                   pl.BlockSpec((TOP_K, tm), col), pl.BlockSpec((TOP_K, tm), col), pl.BlockSpec((TOP_K, tm), col),
                   pl.BlockSpec((N_EXPERTS, LANES), fixed)],
        out_shape=[jax.ShapeDtypeStruct((t, D_MODEL // 2), jnp.uint32), jax.ShapeDtypeStruct((t, D_MODEL), F32),
                   jax.ShapeDtypeStruct((TOP_K, t), jnp.int32), jax.ShapeDtypeStruct((TOP_K, t), F32),
                   jax.ShapeDtypeStruct((TOP_K, t), jnp.int32), jax.ShapeDtypeStruct((N_EXPERTS, LANES), F32)],
        scratch_shapes=[pltpu.VMEM((N_EXPERTS, LANES), F32)],
        compiler_params=pltpu.CompilerParams(dimension_semantics=("arbitrary",), vmem_limit_bytes=VMEM_LIMIT),
        name="out_projection_router",
    )(oa, ob, sg, x2, pa, pb, w_out.astype(MXU_DTYPE), ln_g.reshape(1, -1), ln_b.reshape(1, -1),
      w_router.T.astype(MXU_DTYPE), jnp.broadcast_to(router_bias.astype(F32)[:, None], (N_EXPERTS, LANES)),
      jnp.concatenate([s_gate, s_up], axis=1).astype(MXU_DTYPE), s_down.astype(MXU_DTYPE), tri)
    return outs


def _rows_to_tiles(x):
    return pltpu.einshape("cml->mcl", jnp.stack(_lane_tiles(x), axis=0))


def _tiles_to_rows(x3):
    xt = pltpu.einshape("mcl->cml", x3)
    return jnp.concatenate([xt[c] for c in range(xt.shape[0])], axis=1)


def _dispatch_kernel(pend_ref, pad_ref, dest_ref, h2_ref, xs_ref, h_ref, zeros, sem, zsem):
    step = pl.program_id(0)
    tm = h2_ref.shape[0]
    slot = step % 2
    h_ref[slot] = _rows_to_tiles(h2_ref[...])

    @pl.when(step == 0)
    def _():
        zeros[...] = jnp.zeros(zeros.shape, zeros.dtype)

        def for_pieces(action):
            def body(e, c):
                for piece in range(MOE_BM // ZERO_ROWS):
                    @pl.when(pad_ref[e] > piece * ZERO_ROWS)
                    def _():
                        start = pend_ref[e] - (piece + 1) * ZERO_ROWS
                        action(pltpu.make_async_copy(zeros, xs_ref.at[pl.ds(start, ZERO_ROWS)], zsem))
                return c
            lax.fori_loop(0, N_EXPERTS, body, 0)
        for_pieces(lambda cp: cp.start())
        for_pieces(lambda cp: cp.wait())

    def issue(t, c):
        for k in range(TOP_K):
            pltpu.make_async_copy(h_ref.at[slot, t], xs_ref.at[dest_ref[k, t]], sem.at[slot]).start(priority=k % 2)
        return c
    lax.fori_loop(0, tm, issue, 0)

    def wait_tile(s):
        for k in range(TOP_K):
            pltpu.make_async_copy(h_ref.at[s], xs_ref.at[pl.ds(0, tm)], sem.at[s]).wait()

    @pl.when(step > 0)
    def _():
        wait_tile(1 - slot)

    @pl.when(step + 1 == pl.num_programs(0))
    def _():
        wait_tile(slot)


def _dispatch(h, dest, pends, pad_rows, n_rows):
    t = h.shape[0]
    tm = DISP_TM
    return pl.pallas_call(
        _dispatch_kernel,
        grid_spec=pltpu.PrefetchScalarGridSpec(
            num_scalar_prefetch=2,
            grid=(t // tm,),
            in_specs=[pl.BlockSpec((TOP_K, tm), lambda i, *_: (0, i), memory_space=pltpu.SMEM),
                      pl.BlockSpec((tm, D_MODEL // 2), lambda i, *_: (i, 0))],
            out_specs=pl.BlockSpec(memory_space=pl.ANY),
            scratch_shapes=[pltpu.VMEM((2, tm) + PACKED_ROW_TILE, jnp.uint32),
                            pltpu.VMEM((ZERO_ROWS,) + PACKED_ROW_TILE, jnp.uint32),
                            pltpu.SemaphoreType.DMA((2,)), pltpu.SemaphoreType.DMA(())]),
        out_shape=jax.ShapeDtypeStruct((n_rows,) + PACKED_ROW_TILE, jnp.uint32),
        compiler_params=pltpu.CompilerParams(dimension_semantics=("arbitrary",), vmem_limit_bytes=VMEM_LIMIT),
        name="moe_dispatch",
    )(pends, pad_rows, dest, h)


def _experts_kernel(blk_e_ref, nused_ref, first_ref, slot_ref, has_next_ref, next_e_ref, xs_hbm, wg_hbm, wu_hbm,
                    wd_hbm, ys_ref, wg_s, wu_s, wd_s, wg_f, wu_f, wd_f, sem, xbuf, xsem):
    b = pl.program_id(0)

    def weight_copies(e, s):
        return (pltpu.make_async_copy(wg_hbm.at[e], wg_f.at[s], sem.at[s]),
                pltpu.make_async_copy(wu_hbm.at[e], wu_f.at[s], sem.at[s]),
                pltpu.make_async_copy(wd_hbm.at[e], wd_f.at[s], sem.at[s]))

    @pl.when(first_ref[b] == 1)
    def _():
        s = slot_ref[b]

        @pl.when(b == 0)
        def _():
            for cp in weight_copies(blk_e_ref[0], s):
                cp.start()
        for cp in weight_copies(blk_e_ref[b], s):
            cp.wait()
        wg_s[...] = _mx(wg_f[s])
        wu_s[...] = _mx(wu_f[s])
        wd_s[...] = _mx(wd_f[s])

        @pl.when(has_next_ref[b] == 1)
        def _():
            for cp in weight_copies(next_e_ref[b], 1 - s):
                cp.start()

    def row_copy(blk):
        s = blk % X_SLOTS
        return pltpu.make_async_copy(xs_hbm.at[pl.ds(blk * MOE_BM, MOE_BM)], xbuf.at[s], xsem.at[s])

    @pl.when(b == 0)
    def _():
        for ahead in range(X_SLOTS - 1):
            @pl.when(ahead < nused_ref[0])
            def _():
                row_copy(ahead).start()

    @pl.when(b + X_SLOTS - 1 < nused_ref[0])
    def _():
        row_copy(b + X_SLOTS - 1).start()

    @pl.when(b < nused_ref[0])
    def _():
        row_copy(b).wait()
        xb = _mx(jnp.concatenate(_unpack_bf16_pairs(_tiles_to_rows(xbuf[b % X_SLOTS])), axis=1))
        hid = jax.nn.silu(_dot(xb, wg_s[...])) * _dot(xb, wu_s[...])
        ys_ref[...] = _rows_to_tiles(_pack_bf16_pairs(_dot(_mx(hid), wd_s[...])))

    @pl.when(b >= nused_ref[0])
    def _():
        ys_ref[...] = jnp.zeros(ys_ref.shape, ys_ref.dtype)


def _experts(xs, blk_e, nused, e_gate, e_up, e_down):
    n_rows = xs.shape[0]
    n_blocks = n_rows // MOE_BM
    idx = jnp.arange(n_blocks, dtype=jnp.int32)
    first = (idx < nused[0]) & ((idx == 0) | (blk_e != jnp.roll(blk_e, 1)))
    slot = ((jnp.cumsum(first.astype(jnp.int32)) - 1) % 2).astype(jnp.int32)
    later_first = lax.cummin(jnp.where(first, idx, n_blocks), reverse=True)
    next_first = jnp.concatenate([later_first[1:], jnp.full((1,), n_blocks, jnp.int32)])
    has_next = (next_first < n_blocks).astype(jnp.int32)
    next_e = blk_e[jnp.minimum(next_first, n_blocks - 1)]
    omap = lambda b, *refs: (jnp.where(b < refs[1][0], b, n_blocks - 1), 0, 0)
    hbm = pl.BlockSpec(memory_space=pl.ANY)
    return pl.pallas_call(
        _experts_kernel,
        grid_spec=pltpu.PrefetchScalarGridSpec(
            num_scalar_prefetch=6,
            grid=(n_blocks,),
            in_specs=[hbm, hbm, hbm, hbm],
            out_specs=pl.BlockSpec((MOE_BM,) + PACKED_ROW_TILE, omap),
            scratch_shapes=[pltpu.VMEM((D_MODEL, EXPERT_HIDDEN), MXU_DTYPE),
                            pltpu.VMEM((D_MODEL, EXPERT_HIDDEN), MXU_DTYPE),
                            pltpu.VMEM((EXPERT_HIDDEN, D_MODEL), MXU_DTYPE),
                            pltpu.VMEM((2, D_MODEL, EXPERT_HIDDEN), F32),
                            pltpu.VMEM((2, D_MODEL, EXPERT_HIDDEN), F32),
                            pltpu.VMEM((2, EXPERT_HIDDEN, D_MODEL), F32),
                            pltpu.SemaphoreType.DMA((2,)),
                            pltpu.VMEM((X_SLOTS, MOE_BM) + PACKED_ROW_TILE, jnp.uint32),
                            pltpu.SemaphoreType.DMA((X_SLOTS,))]),
        out_shape=jax.ShapeDtypeStruct((n_rows,) + PACKED_ROW_TILE, jnp.uint32),
        compiler_params=pltpu.CompilerParams(dimension_semantics=("arbitrary",), vmem_limit_bytes=VMEM_LIMIT),
        name="moe_experts",
    )(blk_e, nused, first.astype(jnp.int32), slot, has_next, next_e.astype(jnp.int32), xs, e_gate, e_up, e_down)


def _combine_kernel(dest_ref, dest_next_ref, gate_ref, base_ref, g2_ref, b2_ref, ys_ref, out_ref, buf, ysum, sem):
    step = pl.program_id(0)
    tm = base_ref.shape[0]
    slot = step % 2

    sub = ROW_TILE[0]

    def gather_rows(d_ref, s, t0):
        for u in range(sub):
            for k in range(TOP_K):
                pltpu.make_async_copy(ys_ref.at[d_ref[(t0 + u) * TOP_K + k]], buf.at[s, k * tm + t0 + u],
                                      sem.at[s]).start(priority=k % 2)

    def combine_rows(t0):
        y = base_ref[pl.ds(t0, sub), :]
        gates = gate_ref[pl.ds(t0, sub), :]
        for k in range(TOP_K):
            words = _tiles_to_rows(buf[slot, pl.ds(k * tm + t0, sub)])
            y = y + gates[:, k:k + 1] * jnp.concatenate(_unpack_bf16_pairs(words), axis=1)
        ysum[pl.ds(t0, sub), :] = y

    def for_token_groups(body):
        def trip(g, c):
            body(pl.multiple_of(g * sub, sub))
            return c
        lax.fori_loop(0, tm // sub, trip, 0)

    @pl.when(step == 0)
    def _():
        for_token_groups(lambda t0: gather_rows(dest_ref, 0, t0))

    pltpu.make_async_copy(ys_ref.at[pl.ds(0, tm * TOP_K)], buf.at[slot], sem.at[slot]).wait()

    @pl.when(step + 1 < pl.num_programs(0))
    def _():
        def both(t0):
            gather_rows(dest_next_ref, 1 - slot, t0)
            combine_rows(t0)
        for_token_groups(both)

    @pl.when(step + 1 == pl.num_programs(0))
    def _():
        for_token_groups(combine_rows)

    out_ref[...] = _layer_norm(ysum[...], g2_ref[...], b2_ref[...])


def _combine(ys3, dest, gate, base, ln_g, ln_b):
    t = base.shape[0]
    tm = COMB_TM
    n_tiles = t // tm
    dest_tk = dest.T.reshape(-1)
    return pl.pallas_call(
        _combine_kernel,
        grid=(n_tiles,),
        in_specs=[pl.BlockSpec((tm * TOP_K,), lambda i: (i,), memory_space=pltpu.SMEM),
                  pl.BlockSpec((tm * TOP_K,), lambda i: (jnp.minimum(i + 1, n_tiles - 1),), memory_space=pltpu.SMEM),
                  pl.BlockSpec((tm, TOP_K), lambda i: (i, 0)),
                  pl.BlockSpec((tm, D_MODEL), lambda i: (i, 0)),
                  pl.BlockSpec((1, D_MODEL), lambda i: (0, 0)),
                  pl.BlockSpec((1, D_MODEL), lambda i: (0, 0)),
                  pl.BlockSpec(memory_space=pl.ANY)],
        out_specs=pl.BlockSpec((tm, D_MODEL), lambda i: (i, 0)),
        out_shape=jax.ShapeDtypeStruct((t, D_MODEL), F32),
        scratch_shapes=[pltpu.VMEM((2, tm * TOP_K) + PACKED_ROW_TILE, jnp.uint32), pltpu.VMEM((tm, D_MODEL), F32),
                        pltpu.SemaphoreType.DMA((2,))],
        compiler_params=pltpu.CompilerParams(dimension_semantics=("arbitrary",), vmem_limit_bytes=VMEM_LIMIT),
        name="moe_combine",
    )(dest_tk, dest_tk, gate.T, base, ln_g.reshape(1, -1), ln_b.reshape(1, -1), ys3)


def _dest_kernel(pstart_ref, eidx_ref, rank_ref, dest_ref):
    eidx = eidx_ref[...]

    unroll = 8

    def body(g, dest):
        for u in range(unroll):
            e = g * unroll + u
            dest = dest + jnp.where(eidx == e, pstart_ref[e], 0)
        return dest
    dest_ref[...] = lax.fori_loop(0, N_EXPERTS // unroll, body, rank_ref[...])


def _dest_rows(pstarts, eidx, rank):
    t = eidx.shape[1]
    tl = 2048
    spec = pl.BlockSpec((TOP_K, tl), lambda i, *_: (0, i))
    return pl.pallas_call(
        _dest_kernel,
        grid_spec=pltpu.PrefetchScalarGridSpec(num_scalar_prefetch=1, grid=(t // tl,), in_specs=[spec, spec],
                                               out_specs=spec),
        out_shape=jax.ShapeDtypeStruct(eidx.shape, jnp.int32),
        compiler_params=pltpu.CompilerParams(dimension_semantics=("arbitrary",)),
        name="moe_dest_rows",
    )(pstarts, eidx, rank)


def _moe_layout(eidx, rank, counts):
    n_assign = eidx.size
    n_blocks = (n_assign + N_EXPERTS * (MOE_BM - 1)) // MOE_BM
    padded = (counts + MOE_BM - 1) // MOE_BM * MOE_BM
    pends = jnp.cumsum(padded)
    pstarts = (pends - padded).astype(jnp.int32)
    dest = _dest_rows(pstarts, eidx, rank)
    block_row = jnp.arange(n_blocks, dtype=jnp.int32) * MOE_BM
    blk_e = jnp.minimum(jnp.sum(pends[None, :] <= block_row[:, None], axis=1), N_EXPERTS - 1).astype(jnp.int32)
    nused = (pends[-1:] // MOE_BM).astype(jnp.int32)
    pad_rows = (padded - counts).astype(jnp.int32)
    return dest.astype(jnp.int32), blk_e, nused, pends.astype(jnp.int32), pad_rows, n_blocks * MOE_BM


def _layer(x, w_in, cmp_pe, cmp_w1, cmp_b1, cmp_w2, sinks, bias_table, proj_a, proj_b, w_out, ln1_g, ln1_b,
           w_router, router_bias, e_gate, e_up, e_down, s_gate, s_up, s_down, ln2_g, ln2_b):
    bsz, seq, d = x.shape
    x2 = x.reshape(bsz * seq, d)
    proj = _in_projection(x2, w_in)
    kvcmp = _compress(proj['kc'], proj['vc'], bsz, seq, cmp_pe, cmp_w1, cmp_b1, cmp_w2)
    oa, ob = _attention(proj, kvcmp, sinks, bias_table, bsz, seq)
    h, base, eidx, gate, rank, cnt = _out_projection(oa, ob, proj['sg'], x2, proj_a, proj_b, w_out, ln1_g, ln1_b,
                                                     w_router, router_bias, s_gate, s_up, s_down)
    counts = cnt[:, 0].astype(jnp.int32)
    dest, blk_e, nused, pends, pad_rows, n_rows = _moe_layout(eidx, rank, counts)
    xs = _dispatch(h, dest, pends, pad_rows, n_rows)
    ys = _experts(xs, blk_e, nused, e_gate, e_up, e_down)
    out = _combine(ys, dest, gate, base, ln2_g, ln2_b)
    return out.reshape(bsz, seq, d)


def kernel(x, w_in, cmp_pe, cmp_w1, cmp_b1, cmp_w2, attn_sinks, rel_bias_table, proj_a, proj_b, w_out, ln1_g, ln1_b,
           w_router, router_bias, expert_w_gate, expert_w_up, expert_w_down, shared_w_gate, shared_w_up,
           shared_w_down, ln2_g, ln2_b):
    h = x
    for l in range(DEPTH):
        h = _layer(h, w_in[l], cmp_pe[l], cmp_w1[l], cmp_b1[l], cmp_w2[l], attn_sinks[l], rel_bias_table, proj_a[l],
                   proj_b[l], w_out[l], ln1_g[l], ln1_b[l], w_router[l], router_bias[l], expert_w_gate[l],
                   expert_w_up[l], expert_w_down[l], shared_w_gate[l], shared_w_up[l], shared_w_down[l], ln2_g[l],
                   ln2_b[l])
    return h
```
